```python
import math
import jax, jax.numpy as jnp
from jax import lax
import numpy as np

D_MODEL = 1024
BATCH = 2
SEQ = 8192
DEPTH = 2

CHUNK = 64
Q_BLOCK = 128
EPS = 1e-6

DA_HEADS = 4
DA_HEAD_DIM = 32
DA_V_DIM = 2 * DA_HEAD_DIM
DA_WIDTH = DA_HEADS * DA_V_DIM
DA_QK = DA_HEADS * 2 * DA_HEAD_DIM
SSM_HEADS = 8
SSM_HEAD_DIM = 64
SSM_WIDTH = SSM_HEADS * SSM_HEAD_DIM
SSM_STATE = 64
SSM_GROUPS = 2
SSM_CONV = 4
SSM_CONV_CH = SSM_WIDTH + 2 * SSM_GROUPS * SSM_STATE
RET_HEADS = 4
RET_KEY_DIM = 64
RET_V_DIM = 64
RET_WIDTH = RET_HEADS * RET_V_DIM
RET_QK = RET_HEADS * RET_KEY_DIM
ROPE_BASE = 10000.0

MIX_WIDTH = DA_WIDTH + SSM_WIDTH + RET_WIDTH
IN_SIZES = (DA_QK, DA_QK, DA_WIDTH,
            SSM_WIDTH, SSM_CONV_CH, SSM_HEADS,
            RET_QK, RET_QK, RET_WIDTH, RET_WIDTH)
IN_TOTAL = sum(IN_SIZES)

REL_BUCKETS = 32
REL_MAX_DIST = 128

FFN_DENSE = 2752
N_EXPERTS = 8
TOP_K = 2
FFN_EXPERT = 3584
N_DENSE_LAYERS = (DEPTH + 1) // 2
N_MOE_LAYERS = DEPTH // 2

kernel_name = "hybrid_diffattn_ssd_retention_moe"


def _split_points():
    pts, acc = [], 0
    for n in IN_SIZES[:-1]:
        acc += n
        pts.append(acc)
    return pts


def rmsnorm(x, g):
    xf = x.astype(jnp.float32)
    y = xf * lax.rsqrt(jnp.mean(xf * xf, axis=-1, keepdims=True) + EPS)
    return (y * g.astype(jnp.float32)).astype(x.dtype)


def t5_bucket(rel):
    half = REL_BUCKETS // 2
    max_exact = half // 2
    ret = jnp.where(rel > 0, half, 0)
    n = jnp.abs(rel)
    nf = jnp.maximum(n, 1).astype(jnp.float32)
    large = max_exact + (jnp.log(nf / max_exact) / math.log(REL_MAX_DIST / max_exact)
                         * (half - max_exact)).astype(jnp.int32)
    large = jnp.minimum(large, half - 1)
    return ret + jnp.where(n < max_exact, n, large)


def rotary(u):
    s, dh = u.shape[1], u.shape[-1]
    inv = 1.0 / (ROPE_BASE ** (jnp.arange(0, dh, 2, dtype=jnp.float32) / dh))
    ang = jnp.arange(s, dtype=jnp.float32)[:, None] * inv[None, :]
    cos = jnp.cos(ang)[None, :, None, :]
    sin = jnp.sin(ang)[None, :, None, :]
    u1, u2 = u[..., : dh // 2], u[..., dh // 2:]
    return jnp.concatenate([u1 * cos - u2 * sin, u1 * sin + u2 * cos], axis=-1).astype(u.dtype)


def diff_attention(q, k, v, lam, lam_init, rel_table, head_gain):
    b, s = q.shape[:2]
    nblk = s // Q_BLOCK
    scale = DA_HEAD_DIM ** -0.5
    k_pos = jnp.arange(s)
    qb = q.reshape(b, nblk, Q_BLOCK, DA_HEADS, 2, DA_HEAD_DIM).transpose(1, 0, 2, 3, 4, 5)

    def block(args):
        q_blk, i = args
        q_pos = i * Q_BLOCK + jnp.arange(Q_BLOCK)
        bias = rel_table[t5_bucket(k_pos[None, :] - q_pos[:, None])]
        bias = bias.astype(jnp.float32).transpose(2, 0, 1)
        allowed = (k_pos[None, :] // CHUNK) <= (q_pos[:, None] // CHUNK)
        logits = jnp.einsum('bqhtd,bkhtd->bhtqk', q_blk, k).astype(jnp.float32) * scale
        logits = jnp.where(allowed, logits + bias[None, :, None], -jnp.inf)
        p = jax.nn.softmax(logits, axis=-1)
        attn = p[:, :, 0] - lam * p[:, :, 1]
        return jnp.einsum('bhqk,bkhe->bqhe', attn.astype(v.dtype), v)

    out = lax.map(block, (qb, jnp.arange(nblk)))
    out = out.transpose(1, 0, 2, 3, 4).reshape(b, s, DA_HEADS, DA_V_DIM)
    out = rmsnorm(out, head_gain) * (1.0 - lam_init)
    return out.reshape(b, s, DA_WIDTH)


def causal_dwconv(u, w, bias):
    c = u.shape[-1]
    y = lax.conv_general_dilated(u, w[:, None, :].astype(u.dtype), window_strides=(1,),
                                 padding=[(SSM_CONV - 1, 0)],
                                 dimension_numbers=('NWC', 'WIO', 'NWC'),
                                 feature_group_count=c)
    return y + bias


def ssd_mixer(z, xbc_raw, dt_raw, conv_w, conv_b, dt_bias, a_log, d_skip, norm_g):
    b, s, _ = z.shape
    nc = s // CHUNK
    rep = SSM_HEADS // SSM_GROUPS
    xbc = jax.nn.silu(causal_dwconv(xbc_raw, conv_w, conv_b))
    xs, bm, cm = jnp.split(xbc, [SSM_WIDTH, SSM_WIDTH + SSM_GROUPS * SSM_STATE], axis=-1)
    xs = xs.reshape(b, nc, CHUNK, SSM_HEADS, SSM_HEAD_DIM)
    bm = jnp.repeat(bm.reshape(b, nc, CHUNK, SSM_GROUPS, SSM_STATE), rep, axis=3)
    cm = jnp.repeat(cm.reshape(b, nc, CHUNK, SSM_GROUPS, SSM_STATE), rep, axis=3)
    dt = jax.nn.softplus(dt_raw.astype(jnp.float32) + dt_bias)
    a = (-jnp.exp(a_log) * dt).reshape(b, nc, CHUNK, SSM_HEADS).transpose(0, 3, 1, 2)
    a_cs = jnp.cumsum(a, axis=-1)
    xdt = xs * dt.reshape(b, nc, CHUNK, SSM_HEADS)[..., None]
    causal = jnp.tril(jnp.ones((CHUNK, CHUNK), dtype=bool))
    seg = a_cs[..., :, None] - a_cs[..., None, :]
    lmat = jnp.exp(jnp.where(causal, seg, -jnp.inf))
    cb = jnp.einsum('bclhn,bcshn->bhcls', cm, bm) * lmat
    y_diag = jnp.einsum('bhcls,bcshp->bclhp', cb, xdt)
    decay_states = jnp.exp(a_cs[..., -1:] - a_cs)
    states = jnp.einsum('bclhn,bhcl,bclhp->bchpn', bm, decay_states, xdt)
    chunk_decay = jnp.exp(a_cs[..., -1])

    def step(h, inp):
        st, dec = inp
        return h * dec[..., None, None] + st, h

    h0 = jnp.zeros((b, SSM_HEADS, SSM_HEAD_DIM, SSM_STATE), states.dtype)
    _, prev = lax.scan(step, h0, (states.transpose(1, 0, 2, 3, 4), chunk_decay.transpose(2, 0, 1)))
    prev = prev.transpose(1, 0, 2, 3, 4)
    y_off = jnp.einsum('bclhn,bchpn,bhcl->bclhp', cm, prev, jnp.exp(a_cs))
    y = (y_diag + y_off + xs * d_skip[:, None]).reshape(b, s, SSM_WIDTH)
    y = (y * jax.nn.silu(z)).reshape(b, s, SSM_GROUPS, SSM_WIDTH // SSM_GROUPS)
    y = rmsnorm(y, norm_g.reshape(SSM_GROUPS, SSM_WIDTH // SSM_GROUPS))
    return y.reshape(b, s, SSM_WIDTH)


def retention(q, k, v, g, head_gain):
    b, s, _ = q.shape
    nc = s // CHUNK
    q = rotary(q.reshape(b, s, RET_HEADS, RET_KEY_DIM))
    k = rotary(k.reshape(b, s, RET_HEADS, RET_KEY_DIM)) * (RET_KEY_DIM ** -0.5)
    v = v.reshape(b, nc, CHUNK, RET_HEADS, RET_V_DIM)
    q = q.reshape(b, nc, CHUNK, RET_HEADS, RET_KEY_DIM)
    k = k.reshape(b, nc, CHUNK, RET_HEADS, RET_KEY_DIM)
    log_gamma = jnp.log1p(-jnp.power(2.0, -5.0 - jnp.arange(RET_HEADS, dtype=jnp.float32)))
    idx = jnp.arange(CHUNK, dtype=jnp.float32)
    intra_decay = jnp.exp(log_gamma[:, None, None] * jnp.abs(idx[:, None] - idx[None, :]))
    scores = jnp.einsum('bclhd,bcshd->bhcls', q, k) * intra_decay[:, None]
    o_intra = jnp.einsum('bhcls,bcshe->bclhe', scores, v)
    key_decay = jnp.exp(log_gamma[:, None] * (CHUNK - 1 - idx)[None, :])
    kv = jnp.einsum('bcshd,hs,bcshe->bchde', k, key_decay, v)
    chunk_decay = jnp.exp(log_gamma * CHUNK)

    def step(state, kv_c):
        return state * chunk_decay[:, None, None] + kv_c, state

    s0 = jnp.zeros((b, RET_HEADS, RET_KEY_DIM, RET_V_DIM), kv.dtype)
    _, prev = lax.scan(step, s0, kv.transpose(1, 0, 2, 3, 4))
    prev = prev.transpose(1, 0, 2, 3, 4)
    query_decay = jnp.exp(log_gamma[:, None] * (idx + 1.0)[None, :])
    o_cross = jnp.einsum('bclhd,bchde,hl->bclhe', q, prev, query_decay)
    o = (o_intra + o_cross).reshape(b, s, RET_HEADS, RET_V_DIM)
    o = rmsnorm(o, head_gain).reshape(b, s, RET_WIDTH)
    return jax.nn.silu(g) * o


def swiglu(h, w_gate, w_up, w_down):
    return (jax.nn.silu(h @ w_gate) * (h @ w_up)) @ w_down


def moe_swiglu(h, router_w, e_gate, e_up, e_down):
    b, s, d = h.shape
    t = h.reshape(b * s, d)
    logits = (t @ router_w).astype(jnp.float32)
    top_val, top_idx = lax.top_k(logits, TOP_K)
    gates = jax.nn.softmax(top_val, axis=-1)
    combine = jnp.sum(jax.nn.one_hot(top_idx, N_EXPERTS, dtype=jnp.float32) * gates[..., None], axis=1)
    out = jnp.zeros_like(t)
    for e in range(N_EXPERTS):
        y = swiglu(t, e_gate[e], e_up[e], e_down[e])
        out = out + combine[:, e:e + 1].astype(t.dtype) * y
    return out.reshape(b, s, d)


def setup_inputs(seed: int = 0) -> dict:
    key = jax.random.key(seed)
    ks = jax.random.split(key, 32)
    f32 = jnp.float32
    nrm = lambda k, shape, sc: jax.random.normal(k, shape, f32) * sc
    gain = lambda k, shape: 1.0 + 0.02 * jax.random.normal(k, shape, f32)
    dt0 = jnp.exp(jax.random.uniform(ks[13], (DEPTH, SSM_HEADS), f32, math.log(1e-3), math.log(1e-1)))
    return {
        "x": nrm(ks[0], (BATCH, SEQ, D_MODEL), 1.0),
        "w_in": nrm(ks[1], (DEPTH, D_MODEL, IN_TOTAL), D_MODEL ** -0.5),
        "w_out": nrm(ks[2], (DEPTH, MIX_WIDTH, D_MODEL), MIX_WIDTH ** -0.5),
        "attn_norm": gain(ks[3], (DEPTH, D_MODEL)),
        "ffn_norm": gain(ks[4], (DEPTH, D_MODEL)),
        "final_norm": gain(ks[5], (D_MODEL,)),
        "rel_bias": nrm(ks[6], (REL_BUCKETS, DA_HEADS), 0.3),
        "lambda_q1": nrm(ks[7], (DEPTH, DA_HEAD_DIM), 0.1),
        "lambda_k1": nrm(ks[8], (DEPTH, DA_HEAD_DIM), 0.1),
        "lambda_q2": nrm(ks[9], (DEPTH, DA_HEAD_DIM), 0.1),
        "lambda_k2": nrm(ks[10], (DEPTH, DA_HEAD_DIM), 0.1),
        "da_head_norm": gain(ks[11], (DEPTH, DA_V_DIM)),
        "conv_w": nrm(ks[12], (DEPTH, SSM_CONV, SSM_CONV_CH), SSM_CONV ** -0.5),
        "conv_b": nrm(ks[14], (DEPTH, SSM_CONV_CH), 0.02),
        "dt_bias": dt0 + jnp.log(-jnp.expm1(-dt0)),
        "a_log": jnp.log(jax.random.uniform(ks[15], (DEPTH, SSM_HEADS), f32, 1.0, 16.0)),
        "d_skip": 1.0 + 0.1 * jax.random.normal(ks[16], (DEPTH, SSM_HEADS), f32),
        "ssm_norm": gain(ks[17], (DEPTH, SSM_WIDTH)),
        "ret_head_norm": gain(ks[18], (DEPTH, RET_V_DIM)),
        "w_gate": nrm(ks[19], (N_DENSE_LAYERS, D_MODEL, FFN_DENSE), D_MODEL ** -0.5),
        "w_up": nrm(ks[20], (N_DENSE_LAYERS, D_MODEL, FFN_DENSE), D_MODEL ** -0.5),
        "w_down": nrm(ks[21], (N_DENSE_LAYERS, FFN_DENSE, D_MODEL), FFN_DENSE ** -0.5),
        "router_w": nrm(ks[22], (N_MOE_LAYERS, D_MODEL, N_EXPERTS), D_MODEL ** -0.5),
        "e_gate": nrm(ks[23], (N_MOE_LAYERS, N_EXPERTS, D_MODEL, FFN_EXPERT), D_MODEL ** -0.5),
        "e_up": nrm(ks[24], (N_MOE_LAYERS, N_EXPERTS, D_MODEL, FFN_EXPERT), D_MODEL ** -0.5),
        "e_down": nrm(ks[25], (N_MOE_LAYERS, N_EXPERTS, FFN_EXPERT, D_MODEL), FFN_EXPERT ** -0.5),
    }


def reference(x, w_in, w_out, attn_norm, ffn_norm, final_norm, rel_bias,
              lambda_q1, lambda_k1, lambda_q2, lambda_k2, da_head_norm,
              conv_w, conv_b, dt_bias, a_log, d_skip, ssm_norm, ret_head_norm,
              w_gate, w_up, w_down, router_w, e_gate, e_up, e_down):
    b, s, _ = x.shape
    split_pts = _split_points()
    for layer in range(DEPTH):
        h = rmsnorm(x, attn_norm[layer])
        proj = h @ w_in[layer]
        (da_q, da_k, da_v, ssm_z, ssm_xbc, ssm_dt,
         ret_q, ret_k, ret_v, ret_g) = jnp.split(proj, split_pts, axis=-1)
        lam_init = 0.8 - 0.6 * math.exp(-0.3 * layer)
        lam = (jnp.exp(jnp.sum(lambda_q1[layer] * lambda_k1[layer]).astype(jnp.float32))
               - jnp.exp(jnp.sum(lambda_q2[layer] * lambda_k2[layer]).astype(jnp.float32)) + lam_init)
        out_a = diff_attention(da_q.reshape(b, s, DA_HEADS, 2, DA_HEAD_DIM),
                               da_k.reshape(b, s, DA_HEADS, 2, DA_HEAD_DIM),
                               da_v.reshape(b, s, DA_HEADS, DA_V_DIM),
                               lam, lam_init, rel_bias, da_head_norm[layer])
        out_b = ssd_mixer(ssm_z, ssm_xbc, ssm_dt, conv_w[layer], conv_b[layer], dt_bias[layer],
                          a_log[layer], d_skip[layer], ssm_norm[layer])
        out_c = retention(ret_q, ret_k, ret_v, ret_g, ret_head_norm[layer])
        mix = jnp.concatenate([out_a.astype(x.dtype), out_b.astype(x.dtype), out_c.astype(x.dtype)], axis=-1)
        x = x + mix @ w_out[layer]
        h = rmsnorm(x, ffn_norm[layer])
        if layer % 2 == 0:
            i = layer // 2
            x = x + swiglu(h, w_gate[i], w_up[i], w_down[i])
        else:
            i = layer // 2
            x = x + moe_swiglu(h, router_w[i], e_gate[i], e_up[i], e_down[i])
    return rmsnorm(x, final_norm)
```

```python
import functools
import math

import jax
import jax.numpy as jnp
import numpy as np
from jax import lax
from jax.experimental import pallas as pl
from jax.experimental.pallas import tpu as pltpu

F32 = jnp.float32
BF16 = jnp.bfloat16

D_MODEL = 1024
DEPTH = 2
CHUNK = 64
EPS = 1e-6
DA_HEADS = 4
DA_HEAD_DIM = 32
DA_V_DIM = 64
DA_WIDTH = 256
SSM_HEADS = 8
SSM_HEAD_DIM = 64
SSM_WIDTH = 512
SSM_STATE = 64
SSM_GROUPS = 2
SSM_CONV = 4
RET_HEADS = 4
RET_KEY_DIM = 64
RET_WIDTH = 256
ROPE_BASE = 10000.0
REL_BUCKETS = 32
REL_MAX_DIST = 128
FFN_DENSE = 2752
N_EXPERTS = 8
FFN_EXPERT = 3584

LANES = 128
VMEM_LIMIT = 48 * 1024 * 1024
NEG_BIG = -1e30

TM_PROJ = 512
ATT_T = 512
BLK = 256
FFN_PAD = 2816
TM_FFN = 1024
TF_FFN = 256
TF_MOE = 512

C_Z, C_XS, C_BC, C_RQ, C_RK, C_RV, C_RG, C_DT = 0, 512, 1024, 1280, 1536, 1792, 2048, 2304
PF_W = 2432
PF_CHUNK = 640
QKV_W = 768


def _cparams(sem):
    return pltpu.CompilerParams(dimension_semantics=sem, vmem_limit_bytes=VMEM_LIMIT)


def _group_mask(shape, axis, group, idx):
    io = lax.broadcasted_iota(jnp.int32, shape, axis)
    return (io >= idx * group) & (io < (idx + 1) * group)


def _silu(x):
    return x * (1.0 / (1.0 + jnp.exp(-x)))


def _inproj_kernel(x_ref, g_ref, w_ref, qkv_ref, pf_ref):
    x = x_ref[...]
    ms = jnp.mean(x * x, axis=-1, keepdims=True)
    h = (x * lax.rsqrt(ms + EPS) * g_ref[...]).astype(BF16)
    qkv_ref[...] = jnp.dot(h, w_ref[:, :QKV_W], preferred_element_type=F32).astype(BF16)
    for lo in range(0, PF_W, PF_CHUNK):
        hi = min(lo + PF_CHUNK, PF_W)
        pf_ref[:, lo:hi] = jnp.dot(h, w_ref[:, QKV_W + lo:QKV_W + hi], preferred_element_type=F32)


def _inproj(x2, gain, w):
    t = x2.shape[0]
    return pl.pallas_call(
        _inproj_kernel,
        grid=(t // TM_PROJ,),
        in_specs=[
            pl.BlockSpec((TM_PROJ, D_MODEL), lambda i: (i, 0)),
            pl.BlockSpec((1, D_MODEL), lambda i: (0, 0)),
            pl.BlockSpec((D_MODEL, QKV_W + PF_W), lambda i: (0, 0)),
        ],
        out_specs=[
            pl.BlockSpec((TM_PROJ, QKV_W), lambda i: (i, 0)),
            pl.BlockSpec((TM_PROJ, PF_W), lambda i: (i, 0)),
        ],
        out_shape=[
            jax.ShapeDtypeStruct((t, QKV_W), BF16),
            jax.ShapeDtypeStruct((t, PF_W), F32),
        ],
        compiler_params=_cparams(("parallel",)),
        name="inproj",
    )(x2, gain, w)


def _attn_kernel(qi_ref, kj_ref, q_ref, k_ref, v_ref, bias_ref, gain_ref, lam_ref, o_ref,
                 qm_ref, m_ref, l_ref, acc_ref):
    p = pl.program_id(1)
    qi = qi_ref[p]
    kj = kj_ref[p]
    n_ht = 2 * DA_HEADS

    @pl.when(kj == 0)
    def _init():
        q = q_ref[...]
        for ht in range(n_ht):
            qm_ref[ht] = jnp.where(_group_mask(q.shape, 1, DA_HEAD_DIM, ht), q, jnp.zeros_like(q))
        m_ref[...] = jnp.full(m_ref.shape, NEG_BIG, F32)
        l_ref[...] = jnp.zeros(l_ref.shape, F32)
        acc_ref[...] = jnp.zeros(acc_ref.shape, F32)

    k = k_ref[...]
    v = v_ref[...]
    for ht in range(n_ht):
        h = ht // 2
        s = lax.dot_general(qm_ref[ht], k, (((1,), (1,)), ((), ())), preferred_element_type=F32)
        s = s + bias_ref[0, h]
        m_prev = m_ref[ht]
        m_new = jnp.maximum(m_prev, jnp.max(s, axis=-1, keepdims=True))
        alpha = jnp.exp(m_prev - m_new)
        e = jnp.exp(s - pltpu.repeat(m_new, ATT_T // LANES, axis=1))
        l_ref[ht] = alpha * l_ref[ht] + jnp.sum(e, axis=-1, keepdims=True)
        m_ref[ht] = m_new
        pv = jnp.dot(e.astype(BF16), v, preferred_element_type=F32)
        acc_ref[ht] = acc_ref[ht] * pltpu.repeat(alpha, DA_WIDTH // LANES, axis=1) + pv

    @pl.when(kj == qi)
    def _finish():
        lam = lam_ref[...]
        out = jnp.zeros((ATT_T, DA_WIDTH), F32)
        for h in range(DA_HEADS):
            inv1 = pltpu.repeat(1.0 / l_ref[2 * h], DA_WIDTH // LANES, axis=1)
            inv2 = pltpu.repeat(1.0 / l_ref[2 * h + 1], DA_WIDTH // LANES, axis=1)
            o = acc_ref[2 * h] * inv1 - lam * (acc_ref[2 * h + 1] * inv2)
            cm = _group_mask(o.shape, 1, DA_V_DIM, h)
            ms = jnp.sum(jnp.where(cm, o * o, 0.0), axis=-1, keepdims=True) * (1.0 / DA_V_DIM)
            out = out + jnp.where(cm, o * lax.rsqrt(ms + EPS), 0.0)
        o_ref[...] = (out * gain_ref[...]).astype(BF16)


def _attention(qkv, bias_tiles, gain, lam, batch, seq):
    nq = seq // ATT_T
    qi_l, kj_l = [], []
    for qi in range(nq):
        for kj in range(qi + 1):
            qi_l.append(qi)
            kj_l.append(kj)
    qi_tbl = jnp.asarray(np.array(qi_l, np.int32))
    kj_tbl = jnp.asarray(np.array(kj_l, np.int32))
    n_pairs = len(qi_l)
    n_ht = 2 * DA_HEADS

    grid_spec = pltpu.PrefetchScalarGridSpec(
        num_scalar_prefetch=2,
        grid=(batch, n_pairs),
        in_specs=[
            pl.BlockSpec((ATT_T, DA_WIDTH), lambda b, p, qi, kj: (b * nq + qi[p], 0)),
            pl.BlockSpec((ATT_T, DA_WIDTH), lambda b, p, qi, kj: (b * nq + kj[p], 1)),
            pl.BlockSpec((ATT_T, DA_WIDTH), lambda b, p, qi, kj: (b * nq + kj[p], 2)),
            pl.BlockSpec((1, DA_HEADS, ATT_T, ATT_T),
                         lambda b, p, qi, kj: (jnp.minimum(qi[p] - kj[p], 2), 0, 0, 0)),
            pl.BlockSpec((1, DA_WIDTH), lambda b, p, qi, kj: (0, 0)),
            pl.BlockSpec((1, DA_WIDTH), lambda b, p, qi, kj: (0, 0)),
        ],
        out_specs=pl.BlockSpec((ATT_T, DA_WIDTH), lambda b, p, qi, kj: (b * nq + qi[p], 0)),
        scratch_shapes=[
            pltpu.VMEM((n_ht, ATT_T, DA_WIDTH), BF16),
            pltpu.VMEM((n_ht, ATT_T, LANES), F32),
            pltpu.VMEM((n_ht, ATT_T, LANES), F32),
            pltpu.VMEM((n_ht, ATT_T, DA_WIDTH), F32),
        ],
    )
    return pl.pallas_call(
        _attn_kernel,
        grid_spec=grid_spec,
        out_shape=jax.ShapeDtypeStruct((batch * seq, DA_WIDTH), BF16),
        compiler_params=_cparams(("parallel", "arbitrary")),
        name="diff_attention",
    )(qi_tbl, kj_tbl, qkv, qkv, qkv, bias_tiles, gain, lam)


def _t5_bucket(rel):
    half = REL_BUCKETS // 2
    max_exact = half // 2
    ret = jnp.where(rel > 0, half, 0)
    n = jnp.abs(rel)
    nf = jnp.maximum(n, 1).astype(F32)
    large = max_exact + (jnp.log(nf / max_exact) / math.log(REL_MAX_DIST / max_exact)
                         * (half - max_exact)).astype(jnp.int32)
    large = jnp.minimum(large, half - 1)
    return ret + jnp.where(n < max_exact, n, large)


def _bias_tiles(rel_bias):
    i = jnp.arange(ATT_T)[:, None]
    j = jnp.arange(ATT_T)[None, :]
    tiles = []
    for off in range(3):
        rel = (j - off * ATT_T) - i
        b = rel_bias[_t5_bucket(rel)].astype(F32).transpose(2, 0, 1)
        if off == 0:
            allowed = (j // CHUNK) <= (i // CHUNK)
            b = jnp.where(allowed[None], b, NEG_BIG)
        tiles.append(b)
    return jnp.stack(tiles)


def _cumsum_rows(x):
    n = x.shape[0]
    row = lax.broadcasted_iota(jnp.int32, x.shape, 0)
    sh = 1
    while sh < n:
        x = x + jnp.where(row >= sh, pltpu.roll(x, sh, axis=0), 0.0)
        sh *= 2
    return x


def _expand_heads(v, width, group):
    out = jnp.zeros((v.shape[0], width), F32)
    for h in range(width // group):
        col = jnp.broadcast_to(v[:, h:h + 1], (v.shape[0], width))
        out = jnp.where(_group_mask(out.shape, 1, group, h), col, out)
    return out


def _ssd_kernel(z_ref, xs_ref, bc_ref, dt_ref, cwx_ref, cwb_ref, cbx_ref, cbb_ref, dtb_ref, alog_ref,
                dskip_ref, ng_ref, o_ref, extx_ref, extb_ref, st_ref):
    j = pl.program_id(1)

    @pl.when(j == 0)
    def _reset():
        extx_ref[0:8, :] = jnp.zeros((8, SSM_WIDTH), F32)
        extb_ref[0:8, :] = jnp.zeros((8, 2 * LANES), F32)
        st_ref[...] = jnp.zeros(st_ref.shape, F32)

    extx_ref[8:8 + BLK, :] = xs_ref[...]
    extb_ref[8:8 + BLK, :] = bc_ref[...]
    yx = jnp.broadcast_to(cbx_ref[...], (BLK, SSM_WIDTH))
    yb = jnp.broadcast_to(cbb_ref[...], (BLK, 2 * LANES))
    for w in range(SSM_CONV):
        lo = 8 - (SSM_CONV - 1) + w
        yx = yx + extx_ref[lo:lo + BLK, :] * cwx_ref[w:w + 1, :]
        yb = yb + extb_ref[lo:lo + BLK, :] * cwb_ref[w:w + 1, :]
    tail_x = extx_ref[BLK:BLK + 8, :]
    tail_b = extb_ref[BLK:BLK + 8, :]
    extx_ref[0:8, :] = tail_x
    extb_ref[0:8, :] = tail_b
    xs = _silu(yx)
    bc = _silu(yb)
    bm = bc[:, :LANES]
    cm = bc[:, LANES:]

    dt = jax.nn.softplus(dt_ref[...] + dtb_ref[...])
    a = -jnp.exp(alog_ref[...]) * dt
    a_cs = _cumsum_rows(a)
    a_cs_t = a_cs.T
    a_last = a_cs[BLK - 1:BLK, :]

    dt_full = _expand_heads(dt, SSM_WIDTH, SSM_HEAD_DIM)
    ea_full = _expand_heads(jnp.exp(a_cs), SSM_WIDTH, SSM_HEAD_DIM)
    dec_full = _expand_heads(jnp.exp(a_last - a_cs), SSM_WIDTH, SSM_HEAD_DIM)
    cdec_full = _expand_heads(jnp.exp(a_last), SSM_WIDTH, SSM_HEAD_DIM)

    xdt = xs * dt_full
    xdt_b = xdt.astype(BF16)
    cm_b = cm.astype(BF16)
    bm_b = bm.astype(BF16)

    st = st_ref[...]
    y = jnp.dot(cm_b, st.astype(BF16), preferred_element_type=F32) * ea_full + xs * dskip_ref[...]

    row = lax.broadcasted_iota(jnp.int32, (BLK, BLK), 0)
    colio = lax.broadcasted_iota(jnp.int32, (BLK, BLK), 1)
    causal = colio <= row
    rep = SSM_HEADS // SSM_GROUPS
    lane_lo = lax.broadcasted_iota(jnp.int32, (BLK, LANES), 1) < SSM_HEAD_DIM
    pieces = []
    for g in range(SSM_GROUPS):
        cg = jnp.where(_group_mask(cm_b.shape, 1, SSM_STATE, g), cm_b, jnp.zeros_like(cm_b))
        cb = lax.dot_general(cg, bm_b, (((1,), (1,)), ((), ())), preferred_element_type=F32)
        for pair in range(rep // 2):
            halves = []
            for sub in range(2):
                h = g * rep + pair * 2 + sub
                seg = jnp.broadcast_to(a_cs[:, h:h + 1], (BLK, BLK)) - a_cs_t[h:h + 1, :]
                lmat = jnp.exp(jnp.where(causal, seg, NEG_BIG))
                mh = (cb * lmat).astype(BF16)
                c0 = (h // 2) * LANES
                halves.append(jnp.dot(mh, xdt_b[:, c0:c0 + LANES], preferred_element_type=F32))
            pieces.append(jnp.where(lane_lo, halves[0], halves[1]))
    y = y + jnp.concatenate(pieces, axis=1)

    upd = jnp.dot(bm.T.astype(BF16), (xdt * dec_full).astype(BF16), preferred_element_type=F32)
    r_io = lax.broadcasted_iota(jnp.int32, upd.shape, 0) // SSM_STATE
    c_io = lax.broadcasted_iota(jnp.int32, upd.shape, 1) // (SSM_HEAD_DIM * rep)
    st_ref[...] = st * cdec_full + jnp.where(r_io == c_io, upd, 0.0)

    y = y * _silu(z_ref[...])
    gw = SSM_WIDTH // SSM_GROUPS
    for g in range(SSM_GROUPS):
        yg = y[:, g * gw:(g + 1) * gw]
        ms = jnp.mean(yg * yg, axis=-1, keepdims=True)
        o_ref[:, g * gw:(g + 1) * gw] = (yg * lax.rsqrt(ms + EPS) * ng_ref[:, g * gw:(g + 1) * gw]).astype(BF16)


def _ssd(pf, cwx, cwb, cbx, cbb, dtb, alog, dskip, ng, batch, seq):
    nb = seq // BLK
    row = lambda w, c: pl.BlockSpec((BLK, w), lambda b, j: (b * nb + j, c))
    const = lambda r, w: pl.BlockSpec((r, w), lambda b, j: (0, 0))
    return pl.pallas_call(
        _ssd_kernel,
        grid=(batch, nb),
        in_specs=[
            row(SSM_WIDTH, C_Z // SSM_WIDTH), row(SSM_WIDTH, C_XS // SSM_WIDTH), row(2 * LANES, C_BC // (2 * LANES)),
            row(LANES, C_DT // LANES),
            const(SSM_CONV, SSM_WIDTH), const(SSM_CONV, 2 * LANES), const(1, SSM_WIDTH), const(1, 2 * LANES),
            const(1, LANES), const(1, LANES), const(1, SSM_WIDTH), const(1, SSM_WIDTH),
        ],
        out_specs=pl.BlockSpec((BLK, SSM_WIDTH), lambda b, j: (b * nb + j, 0)),
        out_shape=jax.ShapeDtypeStruct((batch * seq, SSM_WIDTH), BF16),
        scratch_shapes=[
            pltpu.VMEM((BLK + 8, SSM_WIDTH), F32),
            pltpu.VMEM((BLK + 8, 2 * LANES), F32),
            pltpu.VMEM((SSM_GROUPS * SSM_STATE, SSM_WIDTH), F32),
        ],
        compiler_params=_cparams(("parallel", "arbitrary")),
        name="ssd_mixer",
    )(pf, pf, pf, pf, cwx, cwb, cbx, cbb, dtb, alog, dskip, ng)


def _rotary(u, cos, sin_signed):
    first = (lax.broadcasted_iota(jnp.int32, u.shape, 1) % RET_KEY_DIM) < (RET_KEY_DIM // 2)
    half = RET_KEY_DIM // 2
    swapped = jnp.where(first, pltpu.roll(u, u.shape[1] - half, axis=1), pltpu.roll(u, half, axis=1))
    return u * cos + swapped * sin_signed


def _ret_kernel(q_ref, k_ref, v_ref, g_ref, cos_ref, sin_ref, dmat_ref, qdec_ref, kdec_ref, gdec_ref,
                bd_ref, gain_ref, o_ref, st_ref):
    j = pl.program_id(1)

    @pl.when(j == 0)
    def _reset():
        st_ref[...] = jnp.zeros(st_ref.shape, F32)

    cos = cos_ref[...]
    sin = sin_ref[...]
    q = _rotary(q_ref[...], cos, sin)
    k = _rotary(k_ref[...], cos, sin) * (RET_KEY_DIM ** -0.5)
    v_b = v_ref[...].astype(BF16)
    q_b = q.astype(BF16)
    k_b = k.astype(BF16)

    st = st_ref[...]
    o = jnp.dot((q * qdec_ref[...]).astype(BF16), st.astype(BF16), preferred_element_type=F32)
    for h in range(RET_HEADS):
        cm = _group_mask(q_b.shape, 1, RET_KEY_DIM, h)
        qh = jnp.where(cm, q_b, jnp.zeros_like(q_b))
        s = lax.dot_general(qh, k_b, (((1,), (1,)), ((), ())), preferred_element_type=F32) * dmat_ref[h]
        oh = jnp.dot(s.astype(BF16), v_b, preferred_element_type=F32)
        o = o + jnp.where(cm, oh, 0.0)

    upd = jnp.dot((k * kdec_ref[...]).T.astype(BF16), v_b, preferred_element_type=F32)
    st_ref[...] = st * gdec_ref[...] + upd * bd_ref[...]

    out = jnp.zeros(o.shape, F32)
    for h in range(RET_HEADS):
        cm = _group_mask(o.shape, 1, RET_KEY_DIM, h)
        ms = jnp.sum(jnp.where(cm, o * o, 0.0), axis=-1, keepdims=True) * (1.0 / RET_KEY_DIM)
        out = out + jnp.where(cm, o * lax.rsqrt(ms + EPS), 0.0)
    o_ref[...] = (_silu(g_ref[...]) * (out * gain_ref[...])).astype(BF16)


def _retention(pf, cos, sin, dmat, qdec, kdec, gdec, bd, gain, batch, seq):
    nb = seq // BLK
    row = lambda c: pl.BlockSpec((BLK, RET_WIDTH), lambda b, j: (b * nb + j, c))
    const2 = pl.BlockSpec((BLK, RET_WIDTH), lambda b, j: (0, 0))
    return pl.pallas_call(
        _ret_kernel,
        grid=(batch, nb),
        in_specs=[
            row(C_RQ // RET_WIDTH), row(C_RK // RET_WIDTH), row(C_RV // RET_WIDTH), row(C_RG // RET_WIDTH),
            pl.BlockSpec((BLK, RET_WIDTH), lambda b, j: (j, 0)),
            pl.BlockSpec((BLK, RET_WIDTH), lambda b, j: (j, 0)),
            pl.BlockSpec((RET_HEADS, BLK, BLK), lambda b, j: (0, 0, 0)),
            const2, const2, const2, const2,
            pl.BlockSpec((1, RET_WIDTH), lambda b, j: (0, 0)),
        ],
        out_specs=pl.BlockSpec((BLK, RET_WIDTH), lambda b, j: (b * nb + j, 0)),
        out_shape=jax.ShapeDtypeStruct((batch * seq, RET_WIDTH), BF16),
        scratch_shapes=[pltpu.VMEM((RET_WIDTH, RET_WIDTH), F32)],
        compiler_params=_cparams(("parallel", "arbitrary")),
        name="retention",
    )(pf, pf, pf, pf, cos, sin, dmat, qdec, kdec, gdec, bd, gain)


def _retention_tables(seq):
    half = RET_KEY_DIM // 2
    inv = 1.0 / (ROPE_BASE ** (jnp.arange(0, RET_KEY_DIM, 2, dtype=F32) / RET_KEY_DIM))
    ang = jnp.arange(seq, dtype=F32)[:, None] * inv[None, :]
    cos_h = jnp.concatenate([jnp.cos(ang), jnp.cos(ang)], axis=-1)
    sin_h = jnp.concatenate([-jnp.sin(ang), jnp.sin(ang)], axis=-1)
    cos = jnp.tile(cos_h, (1, RET_HEADS))
    sin = jnp.tile(sin_h, (1, RET_HEADS))
    log_gamma = jnp.log1p(-jnp.power(2.0, -5.0 - jnp.arange(RET_HEADS, dtype=F32)))
    idx = jnp.arange(BLK, dtype=F32)
    ii = jnp.arange(BLK)
    same_or_earlier_chunk = (ii[None, :] // CHUNK) <= (ii[:, None] // CHUNK)
    dmat = jnp.exp(log_gamma[:, None, None] * jnp.abs(idx[:, None] - idx[None, :]))
    dmat = jnp.where(same_or_earlier_chunk[None], dmat, 0.0)
    lg_cols = jnp.repeat(log_gamma, RET_KEY_DIM)
    qdec = jnp.exp(lg_cols[None, :] * (idx + 1.0)[:, None])
    kdec = jnp.exp(lg_cols[None, :] * (BLK - 1.0 - idx)[:, None])
    head_of = jnp.arange(RET_WIDTH) // RET_KEY_DIM
    bd = (head_of[:, None] == head_of[None, :]).astype(F32)
    gdec = jnp.exp(lg_cols * BLK)[:, None] * bd
    return cos, sin, dmat, qdec, kdec, gdec, bd


def _outproj_kernel(x_ref, a_ref, b_ref, c_ref, w_ref, g_ref, *rest, with_router):
    if with_router:
        rw_ref, xo_ref, h_ref, comb_ref = rest
    else:
        xo_ref, h_ref = rest
    acc = x_ref[...]
    acc = acc + jnp.dot(a_ref[...], w_ref[0:DA_WIDTH, :], preferred_element_type=F32)
    acc = acc + jnp.dot(b_ref[...], w_ref[DA_WIDTH:DA_WIDTH + SSM_WIDTH, :], preferred_element_type=F32)
    acc = acc + jnp.dot(c_ref[...], w_ref[DA_WIDTH + SSM_WIDTH:, :], preferred_element_type=F32)
    xo_ref[...] = acc
    ms = jnp.mean(acc * acc, axis=-1, keepdims=True)
    h = (acc * lax.rsqrt(ms + EPS) * g_ref[...]).astype(BF16)
    h_ref[...] = h
    if with_router:
        logits = jnp.dot(h, rw_ref[...], preferred_element_type=F32)
        lane = lax.broadcasted_iota(jnp.int32, logits.shape, 1).astype(F32)
        logits = jnp.where(lane < N_EXPERTS, logits, NEG_BIG)
        v1 = jnp.max(logits, axis=-1, keepdims=True)
        i1 = jnp.min(jnp.where(logits == v1, lane, float(LANES)), axis=-1, keepdims=True)
        rest_l = jnp.where(lane == i1, NEG_BIG, logits)
        v2 = jnp.max(rest_l, axis=-1, keepdims=True)
        i2 = jnp.min(jnp.where(rest_l == v2, lane, float(LANES)), axis=-1, keepdims=True)
        e2 = jnp.exp(v2 - v1)
        g1 = 1.0 / (1.0 + e2)
        g2 = e2 * g1
        comb_ref[...] = jnp.where(lane == i1, g1, 0.0) + jnp.where(lane == i2, g2, 0.0)


def _outproj(x2, oa, ob, oc, w, gain, router_w=None):
    t = x2.shape[0]
    with_router = router_w is not None
    tok = lambda w_: pl.BlockSpec((TM_PROJ, w_), lambda i: (i, 0))
    in_specs = [tok(D_MODEL), tok(DA_WIDTH), tok(SSM_WIDTH), tok(RET_WIDTH),
                pl.BlockSpec((D_MODEL, D_MODEL), lambda i: (0, 0)),
                pl.BlockSpec((1, D_MODEL), lambda i: (0, 0))]
    out_specs = [tok(D_MODEL), tok(D_MODEL)]
    out_shape = [jax.ShapeDtypeStruct((t, D_MODEL), F32), jax.ShapeDtypeStruct((t, D_MODEL), BF16)]
    args = [x2, oa, ob, oc, w, gain]
    if with_router:
        in_specs.append(pl.BlockSpec((D_MODEL, LANES), lambda i: (0, 0)))
        out_specs.append(tok(LANES))
        out_shape.append(jax.ShapeDtypeStruct((t, LANES), F32))
        args.append(router_w)
    return pl.pallas_call(
        functools.partial(_outproj_kernel, with_router=with_router),
        grid=(t // TM_PROJ,),
        in_specs=in_specs,
        out_specs=out_specs,
        out_shape=out_shape,
        compiler_params=_cparams(("parallel",)),
        name="outproj_router" if with_router else "outproj",
    )(*args)


def _ffn_kernel(x_ref, h_ref, wg_ref, wu_ref, wd_ref, o_ref, acc_ref):
    f = pl.program_id(1)

    @pl.when(f == 0)
    def _init():
        acc_ref[...] = x_ref[...]

    h = h_ref[...]
    gate = jnp.dot(h, wg_ref[...], preferred_element_type=F32)
    up = jnp.dot(h, wu_ref[...], preferred_element_type=F32)
    act = (_silu(gate) * up).astype(BF16)
    acc_ref[...] += jnp.dot(act, wd_ref[...], preferred_element_type=F32)

    @pl.when(f == pl.num_programs(1) - 1)
    def _done():
        o_ref[...] = acc_ref[...]


def _dense_ffn(x2, h, wg, wu, wd):
    t = x2.shape[0]
    nf = wg.shape[1] // TF_FFN
    return pl.pallas_call(
        _ffn_kernel,
        grid=(t // TM_FFN, nf),
        in_specs=[
            pl.BlockSpec((TM_FFN, D_MODEL), lambda i, f: (i, 0)),
            pl.BlockSpec((TM_FFN, D_MODEL), lambda i, f: (i, 0)),
            pl.BlockSpec((D_MODEL, TF_FFN), lambda i, f: (0, f)),
            pl.BlockSpec((D_MODEL, TF_FFN), lambda i, f: (0, f)),
            pl.BlockSpec((TF_FFN, D_MODEL), lambda i, f: (f, 0)),
        ],
        out_specs=pl.BlockSpec((TM_FFN, D_MODEL), lambda i, f: (i, 0)),
        out_shape=jax.ShapeDtypeStruct((t, D_MODEL), F32),
        scratch_shapes=[pltpu.VMEM((TM_FFN, D_MODEL), F32)],
        compiler_params=_cparams(("parallel", "arbitrary")),
        name="dense_swiglu",
    )(x2, h, wg, wu, wd)


def _moe_kernel(x_ref, h_ref, comb_ref, wg_ref, wu_ref, wd_ref, fg_ref, o_ref, acc_ref):
    e = pl.program_id(1)
    f = pl.program_id(2)

    @pl.when((e == 0) & (f == 0))
    def _init():
        acc_ref[...] = x_ref[...]

    comb = comb_ref[...]
    lane = lax.broadcasted_iota(jnp.int32, comb.shape, 1)
    ce = jnp.sum(jnp.where(lane == e, comb, 0.0), axis=-1, keepdims=True)
    h = h_ref[...]
    gate = jnp.dot(h, wg_ref[0], preferred_element_type=F32)
    up = jnp.dot(h, wu_ref[0], preferred_element_type=F32)
    act = (_silu(gate) * up * ce).astype(BF16)
    acc_ref[...] += jnp.dot(act, wd_ref[0], preferred_element_type=F32)

    @pl.when((e == pl.num_programs(1) - 1) & (f == pl.num_programs(2) - 1))
    def _done():
        y = acc_ref[...]
        ms = jnp.mean(y * y, axis=-1, keepdims=True)
        o_ref[...] = y * lax.rsqrt(ms + EPS) * fg_ref[...]


def _moe_dense(x2, h, comb, wg, wu, wd, final_gain):
    t = x2.shape[0]
    nf = wg.shape[2] // TF_MOE
    return pl.pallas_call(
        _moe_kernel,
        grid=(t // TM_FFN, N_EXPERTS, nf),
        in_specs=[
            pl.BlockSpec((TM_FFN, D_MODEL), lambda i, e, f: (i, 0)),
            pl.BlockSpec((TM_FFN, D_MODEL), lambda i, e, f: (i, 0)),
            pl.BlockSpec((TM_FFN, LANES), lambda i, e, f: (i, 0)),
            pl.BlockSpec((1, D_MODEL, TF_MOE), lambda i, e, f: (e, 0, f)),
            pl.BlockSpec((1, D_MODEL, TF_MOE), lambda i, e, f: (e, 0, f)),
            pl.BlockSpec((1, TF_MOE, D_MODEL), lambda i, e, f: (e, f, 0)),
            pl.BlockSpec((1, D_MODEL), lambda i, e, f: (0, 0)),
        ],
        out_specs=pl.BlockSpec((TM_FFN, D_MODEL), lambda i, e, f: (i, 0)),
        out_shape=jax.ShapeDtypeStruct((t, D_MODEL), F32),
        scratch_shapes=[pltpu.VMEM((TM_FFN, D_MODEL), F32)],
        compiler_params=_cparams(("parallel", "arbitrary", "arbitrary")),
        name="moe_swiglu",
    )(x2, h, comb, wg, wu, wd, final_gain)


def _permute_w_in(w):
    sizes = (256, 256, 256, 512, 768, 8, 256, 256, 256, 256)
    offs = np.concatenate([[0], np.cumsum(sizes)])
    part = lambda n: w[:, offs[n]:offs[n + 1]]
    dt_pad = jnp.zeros((w.shape[0], LANES - SSM_HEADS), w.dtype)
    cols = [part(0) * (DA_HEAD_DIM ** -0.5), part(1), part(2), part(3), part(4),
            part(6), part(7), part(8), part(9), part(5), dt_pad]
    return jnp.concatenate(cols, axis=1).astype(BF16)


def _pad_lanes(v, fill=0.0):
    return jnp.concatenate([v, jnp.full((LANES - v.shape[0],), fill, v.dtype)])[None, :]


def kernel(x, w_in, w_out, attn_norm, ffn_norm, final_norm, rel_bias, lambda_q1, lambda_k1, lambda_q2,
           lambda_k2, da_head_norm, conv_w, conv_b, dt_bias, a_log, d_skip, ssm_norm, ret_head_norm,
           w_gate, w_up, w_down, router_w, e_gate, e_up, e_down):
    batch, seq, _ = x.shape
    x2 = x.reshape(batch * seq, D_MODEL)
    bias_tiles = _bias_tiles(rel_bias)
    ret_tabs = _retention_tables(seq)

    for layer in range(DEPTH):
        lam_init = 0.8 - 0.6 * math.exp(-0.3 * layer)
        lam = (jnp.exp(jnp.sum(lambda_q1[layer] * lambda_k1[layer]))
               - jnp.exp(jnp.sum(lambda_q2[layer] * lambda_k2[layer])) + lam_init).astype(F32)
        lam = jnp.full((1, DA_WIDTH), lam, F32)

        qkv, pf = _inproj(x2, attn_norm[layer][None, :], _permute_w_in(w_in[layer]))

        da_gain = (jnp.tile(da_head_norm[layer], DA_HEADS) * (1.0 - lam_init))[None, :]
        out_a = _attention(qkv, bias_tiles, da_gain, lam, batch, seq)

        cw, cb = conv_w[layer], conv_b[layer]
        out_b = _ssd(pf, cw[:, :SSM_WIDTH], cw[:, SSM_WIDTH:], cb[None, :SSM_WIDTH], cb[None, SSM_WIDTH:],
                     _pad_lanes(dt_bias[layer]), _pad_lanes(a_log[layer]),
                     jnp.repeat(d_skip[layer], SSM_HEAD_DIM)[None, :], ssm_norm[layer][None, :], batch, seq)

        ret_gain = jnp.tile(ret_head_norm[layer], RET_HEADS)[None, :]
        out_c = _retention(pf, *ret_tabs, ret_gain, batch, seq)

        w_o = w_out[layer].astype(BF16)
        if layer % 2 == 0:
            i = layer // 2
            x2, h = _outproj(x2, out_a, out_b, out_c, w_o, ffn_norm[layer][None, :])
            pad = FFN_PAD - FFN_DENSE
            wg = jnp.pad(w_gate[i], ((0, 0), (0, pad))).astype(BF16)
            wu = jnp.pad(w_up[i], ((0, 0), (0, pad))).astype(BF16)
            wd = jnp.pad(w_down[i], ((0, pad), (0, 0))).astype(BF16)
            x2 = _dense_ffn(x2, h, wg, wu, wd)
        else:
            i = layer // 2
            rw = jnp.pad(router_w[i], ((0, 0), (0, LANES - N_EXPERTS))).astype(BF16)
            x2, h, comb = _outproj(x2, out_a, out_b, out_c, w_o, ffn_norm[layer][None, :], rw)
            x2 = _moe_dense(x2, h, comb, e_gate[i].astype(BF16), e_up[i].astype(BF16), e_down[i].astype(BF16),
                            final_norm[None, :])
    return x2.reshape(batch, seq, D_MODEL)
```

```python
import functools
import math

import jax
import jax.numpy as jnp
import numpy as np
from jax import lax
from jax.experimental import pallas as pl
from jax.experimental.pallas import tpu as pltpu

F32 = jnp.float32
BF16 = jnp.bfloat16

D_MODEL = 1024
DEPTH = 2
CHUNK = 64
EPS = 1e-6
DA_HEADS = 4
DA_HEAD_DIM = 32
DA_V_DIM = 64
DA_WIDTH = 256
SSM_HEADS = 8
SSM_HEAD_DIM = 64
SSM_WIDTH = 512
SSM_STATE = 64
SSM_GROUPS = 2
SSM_CONV = 4
RET_HEADS = 4
RET_KEY_DIM = 64
RET_WIDTH = 256
ROPE_BASE = 10000.0
REL_BUCKETS = 32
REL_MAX_DIST = 128
FFN_DENSE = 2752
N_EXPERTS = 8
FFN_EXPERT = 3584

LANES = 128
VMEM_LIMIT = 48 * 1024 * 1024
NEG_BIG = -1e30

TM_PROJ = 512
ATT_T = 512
BLK = 256
FFN_PAD = 2816
TM_FFN = 1024
TF_FFN = 256
TF_MOE = 512

C_Z, C_XS, C_BC, C_RQ, C_RK, C_RV, C_RG, C_DT = 0, 512, 1024, 1280, 1536, 1792, 2048, 2304
PF_W = 2432
PF_CHUNK = 640
QKV_W = 768


def _cparams(sem):
    return pltpu.CompilerParams(dimension_semantics=sem, vmem_limit_bytes=VMEM_LIMIT)


def _group_mask(shape, axis, group, idx):
    io = lax.broadcasted_iota(jnp.int32, shape, axis)
    return (io >= idx * group) & (io < (idx + 1) * group)


def _silu(x):
    return x * (1.0 / (1.0 + jnp.exp(-x)))


def _inproj_kernel(x_ref, g_ref, w_ref, qkv_ref, pf_ref):
    x = x_ref[...]
    ms = jnp.mean(x * x, axis=-1, keepdims=True)
    h = (x * lax.rsqrt(ms + EPS) * g_ref[...]).astype(BF16)
    qkv_ref[...] = jnp.dot(h, w_ref[:, :QKV_W], preferred_element_type=F32).astype(BF16)
    for lo in range(0, PF_W, PF_CHUNK):
        hi = min(lo + PF_CHUNK, PF_W)
        pf_ref[:, lo:hi] = jnp.dot(h, w_ref[:, QKV_W + lo:QKV_W + hi], preferred_element_type=F32)


def _inproj(x2, gain, w):
    t = x2.shape[0]
    return pl.pallas_call(
        _inproj_kernel,
        grid=(t // TM_PROJ,),
        in_specs=[
            pl.BlockSpec((TM_PROJ, D_MODEL), lambda i: (i, 0)),
            pl.BlockSpec((1, D_MODEL), lambda i: (0, 0)),
            pl.BlockSpec((D_MODEL, QKV_W + PF_W), lambda i: (0, 0)),
        ],
        out_specs=[
            pl.BlockSpec((TM_PROJ, QKV_W), lambda i: (i, 0)),
            pl.BlockSpec((TM_PROJ, PF_W), lambda i: (i, 0)),
        ],
        out_shape=[
            jax.ShapeDtypeStruct((t, QKV_W), BF16),
            jax.ShapeDtypeStruct((t, PF_W), F32),
        ],
        compiler_params=_cparams(("parallel",)),
        name="inproj",
    )(x2, gain, w)


def _attn_kernel(qi_ref, kj_ref, q_ref, k_ref, v_ref, bias_ref, gain_ref, lam_ref, o_ref,
                 qm_ref, m_ref, l_ref, acc_ref):
    p = pl.program_id(1)
    qi = qi_ref[p]
    kj = kj_ref[p]
    n_ht = 2 * DA_HEADS

    @pl.when(kj == 0)
    def _init():
        q = q_ref[...]
        for ht in range(n_ht):
            qm_ref[ht] = jnp.where(_group_mask(q.shape, 1, DA_HEAD_DIM, ht), q, jnp.zeros_like(q))
        m_ref[...] = jnp.full(m_ref.shape, NEG_BIG, F32)
        l_ref[...] = jnp.zeros(l_ref.shape, F32)
        acc_ref[...] = jnp.zeros(acc_ref.shape, F32)

    k = k_ref[...]
    v = v_ref[...]
    for ht in range(n_ht):
        h = ht // 2
        s = lax.dot_general(qm_ref[ht], k, (((1,), (1,)), ((), ())), preferred_element_type=F32)
        s = s + bias_ref[0, h]
        m_prev = m_ref[ht]
        m_new = jnp.maximum(m_prev, jnp.max(s, axis=-1, keepdims=True))
        alpha = jnp.exp(m_prev - m_new)
        e = jnp.exp(s - pltpu.repeat(m_new, ATT_T // LANES, axis=1))
        l_ref[ht] = alpha * l_ref[ht] + jnp.sum(e, axis=-1, keepdims=True)
        m_ref[ht] = m_new
        pv = jnp.dot(e.astype(BF16), v, preferred_element_type=F32)
        acc_ref[ht] = acc_ref[ht] * pltpu.repeat(alpha, DA_WIDTH // LANES, axis=1) + pv

    @pl.when(kj == qi)
    def _finish():
        lam = lam_ref[...]
        out = jnp.zeros((ATT_T, DA_WIDTH), F32)
        for h in range(DA_HEADS):
            inv1 = pltpu.repeat(1.0 / l_ref[2 * h], DA_WIDTH // LANES, axis=1)
            inv2 = pltpu.repeat(1.0 / l_ref[2 * h + 1], DA_WIDTH // LANES, axis=1)
            o = acc_ref[2 * h] * inv1 - lam * (acc_ref[2 * h + 1] * inv2)
            cm = _group_mask(o.shape, 1, DA_V_DIM, h)
            ms = jnp.sum(jnp.where(cm, o * o, 0.0), axis=-1, keepdims=True) * (1.0 / DA_V_DIM)
            out = out + jnp.where(cm, o * lax.rsqrt(ms + EPS), 0.0)
        o_ref[...] = (out * gain_ref[...]).astype(BF16)


def _attention(qkv, bias_tiles, gain, lam, batch, seq):
    nq = seq // ATT_T
    qi_l, kj_l = [], []
    for qi in range(nq):
        for kj in range(qi + 1):
            qi_l.append(qi)
            kj_l.append(kj)
    qi_tbl = jnp.asarray(np.array(qi_l, np.int32))
    kj_tbl = jnp.asarray(np.array(kj_l, np.int32))
    n_pairs = len(qi_l)
    n_ht = 2 * DA_HEADS

    grid_spec = pltpu.PrefetchScalarGridSpec(
        num_scalar_prefetch=2,
        grid=(batch, n_pairs),
        in_specs=[
            pl.BlockSpec((ATT_T, DA_WIDTH), lambda b, p, qi, kj: (b * nq + qi[p], 0)),
            pl.BlockSpec((ATT_T, DA_WIDTH), lambda b, p, qi, kj: (b * nq + kj[p], 1)),
            pl.BlockSpec((ATT_T, DA_WIDTH), lambda b, p, qi, kj: (b * nq + kj[p], 2)),
            pl.BlockSpec((1, DA_HEADS, ATT_T, ATT_T),
                         lambda b, p, qi, kj: (jnp.minimum(qi[p] - kj[p], 2), 0, 0, 0)),
            pl.BlockSpec((1, DA_WIDTH), lambda b, p, qi, kj: (0, 0)),
            pl.BlockSpec((1, DA_WIDTH), lambda b, p, qi, kj: (0, 0)),
        ],
        out_specs=pl.BlockSpec((ATT_T, DA_WIDTH), lambda b, p, qi, kj: (b * nq + qi[p], 0)),
        scratch_shapes=[
            pltpu.VMEM((n_ht, ATT_T, DA_WIDTH), BF16),
            pltpu.VMEM((n_ht, ATT_T, LANES), F32),
            pltpu.VMEM((n_ht, ATT_T, LANES), F32),
            pltpu.VMEM((n_ht, ATT_T, DA_WIDTH), F32),
        ],
    )
    return pl.pallas_call(
        _attn_kernel,
        grid_spec=grid_spec,
        out_shape=jax.ShapeDtypeStruct((batch * seq, DA_WIDTH), BF16),
        compiler_params=_cparams(("parallel", "arbitrary")),
        name="diff_attention",
    )(qi_tbl, kj_tbl, qkv, qkv, qkv, bias_tiles, gain, lam)


def _t5_bucket(rel):
    half = REL_BUCKETS // 2
    max_exact = half // 2
    ret = jnp.where(rel > 0, half, 0)
    n = jnp.abs(rel)
    nf = jnp.maximum(n, 1).astype(F32)
    large = max_exact + (jnp.log(nf / max_exact) / math.log(REL_MAX_DIST / max_exact)
                         * (half - max_exact)).astype(jnp.int32)
    large = jnp.minimum(large, half - 1)
    return ret + jnp.where(n < max_exact, n, large)


def _bias_tiles(rel_bias):
    i = jnp.arange(ATT_T)[:, None]
    j = jnp.arange(ATT_T)[None, :]
    tiles = []
    for off in range(3):
        bucket = _t5_bucket((j - off * ATT_T) - i)
        b = jnp.zeros((DA_HEADS, ATT_T, ATT_T), F32)
        for c in range(REL_BUCKETS):
            b = jnp.where((bucket == c)[None], rel_bias[c].astype(F32)[:, None, None], b)
        if off == 0:
            allowed = (j // CHUNK) <= (i // CHUNK)
            b = jnp.where(allowed[None], b, NEG_BIG)
        tiles.append(b)
    return jnp.stack(tiles)


def _cumsum_rows(x):
    n = x.shape[0]
    row = lax.broadcasted_iota(jnp.int32, x.shape, 0)
    sh = 1
    while sh < n:
        x = x + jnp.where(row >= sh, pltpu.roll(x, sh, axis=0), 0.0)
        sh *= 2
    return x


def _expand_heads(v, width, group):
    out = jnp.zeros((v.shape[0], width), F32)
    for h in range(width // group):
        col = jnp.broadcast_to(v[:, h:h + 1], (v.shape[0], width))
        out = jnp.where(_group_mask(out.shape, 1, group, h), col, out)
    return out


def _ssd_kernel(z_ref, xs_ref, bc_ref, dt_ref, cwx_ref, cwb_ref, cbx_ref, cbb_ref, dtb_ref, alog_ref,
                dskip_ref, ng_ref, o_ref, extx_ref, extb_ref, st_ref):
    j = pl.program_id(1)

    @pl.when(j == 0)
    def _reset():
        extx_ref[0:8, :] = jnp.zeros((8, SSM_WIDTH), F32)
        extb_ref[0:8, :] = jnp.zeros((8, 2 * LANES), F32)
        st_ref[...] = jnp.zeros(st_ref.shape, F32)

    extx_ref[8:8 + BLK, :] = xs_ref[...]
    extb_ref[8:8 + BLK, :] = bc_ref[...]
    yx = jnp.broadcast_to(cbx_ref[...], (BLK, SSM_WIDTH))
    yb = jnp.broadcast_to(cbb_ref[...], (BLK, 2 * LANES))
    for w in range(SSM_CONV):
        lo = 8 - (SSM_CONV - 1) + w
        yx = yx + extx_ref[lo:lo + BLK, :] * cwx_ref[w:w + 1, :]
        yb = yb + extb_ref[lo:lo + BLK, :] * cwb_ref[w:w + 1, :]
    tail_x = extx_ref[BLK:BLK + 8, :]
    tail_b = extb_ref[BLK:BLK + 8, :]
    extx_ref[0:8, :] = tail_x
    extb_ref[0:8, :] = tail_b
    xs = _silu(yx)
    bc = _silu(yb)
    bm = bc[:, :LANES]
    cm = bc[:, LANES:]

    dt = jax.nn.softplus(dt_ref[...] + dtb_ref[...])
    a = -jnp.exp(alog_ref[...]) * dt
    a_cs = _cumsum_rows(a)
    a_cs_t = a_cs.T
    a_last = a_cs[BLK - 1:BLK, :]

    dt_full = _expand_heads(dt, SSM_WIDTH, SSM_HEAD_DIM)
    ea_full = _expand_heads(jnp.exp(a_cs), SSM_WIDTH, SSM_HEAD_DIM)
    dec_full = _expand_heads(jnp.exp(a_last - a_cs), SSM_WIDTH, SSM_HEAD_DIM)
    cdec_full = _expand_heads(jnp.exp(a_last), SSM_WIDTH, SSM_HEAD_DIM)

    xdt = xs * dt_full
    xdt_b = xdt.astype(BF16)
    cm_b = cm.astype(BF16)
    bm_b = bm.astype(BF16)

    st = st_ref[...]
    y = jnp.dot(cm_b, st.astype(BF16), preferred_element_type=F32) * ea_full + xs * dskip_ref[...]

    row = lax.broadcasted_iota(jnp.int32, (BLK, BLK), 0)
    colio = lax.broadcasted_iota(jnp.int32, (BLK, BLK), 1)
    causal = colio <= row
    rep = SSM_HEADS // SSM_GROUPS
    lane_lo = lax.broadcasted_iota(jnp.int32, (BLK, LANES), 1) < SSM_HEAD_DIM
    pieces = []
    for g in range(SSM_GROUPS):
        cg = jnp.where(_group_mask(cm_b.shape, 1, SSM_STATE, g), cm_b, jnp.zeros_like(cm_b))
        cb = lax.dot_general(cg, bm_b, (((1,), (1,)), ((), ())), preferred_element_type=F32)
        for pair in range(rep // 2):
            halves = []
            for sub in range(2):
                h = g * rep + pair * 2 + sub
                seg = jnp.broadcast_to(a_cs[:, h:h + 1], (BLK, BLK)) - a_cs_t[h:h + 1, :]
                lmat = jnp.exp(jnp.where(causal, seg, NEG_BIG))
                mh = (cb * lmat).astype(BF16)
                c0 = (h // 2) * LANES
                halves.append(jnp.dot(mh, xdt_b[:, c0:c0 + LANES], preferred_element_type=F32))
            pieces.append(jnp.where(lane_lo, halves[0], halves[1]))
    y = y + jnp.concatenate(pieces, axis=1)

    upd = jnp.dot(bm.T.astype(BF16), (xdt * dec_full).astype(BF16), preferred_element_type=F32)
    r_io = lax.broadcasted_iota(jnp.int32, upd.shape, 0) // SSM_STATE
    c_io = lax.broadcasted_iota(jnp.int32, upd.shape, 1) // (SSM_HEAD_DIM * rep)
    st_ref[...] = st * cdec_full + jnp.where(r_io == c_io, upd, 0.0)

    y = y * _silu(z_ref[...])
    gw = SSM_WIDTH // SSM_GROUPS
    for g in range(SSM_GROUPS):
        yg = y[:, g * gw:(g + 1) * gw]
        ms = jnp.mean(yg * yg, axis=-1, keepdims=True)
        o_ref[:, g * gw:(g + 1) * gw] = (yg * lax.rsqrt(ms + EPS) * ng_ref[:, g * gw:(g + 1) * gw]).astype(BF16)


def _ssd(pf, cwx, cwb, cbx, cbb, dtb, alog, dskip, ng, batch, seq):
    nb = seq // BLK
    row = lambda w, c: pl.BlockSpec((BLK, w), lambda b, j: (b * nb + j, c))
    const = lambda r, w: pl.BlockSpec((r, w), lambda b, j: (0, 0))
    return pl.pallas_call(
        _ssd_kernel,
        grid=(batch, nb),
        in_specs=[
            row(SSM_WIDTH, C_Z // SSM_WIDTH), row(SSM_WIDTH, C_XS // SSM_WIDTH), row(2 * LANES, C_BC // (2 * LANES)),
            row(LANES, C_DT // LANES),
            const(SSM_CONV, SSM_WIDTH), const(SSM_CONV, 2 * LANES), const(1, SSM_WIDTH), const(1, 2 * LANES),
            const(1, LANES), const(1, LANES), const(1, SSM_WIDTH), const(1, SSM_WIDTH),
        ],
        out_specs=pl.BlockSpec((BLK, SSM_WIDTH), lambda b, j: (b * nb + j, 0)),
        out_shape=jax.ShapeDtypeStruct((batch * seq, SSM_WIDTH), BF16),
        scratch_shapes=[
            pltpu.VMEM((BLK + 8, SSM_WIDTH), F32),
            pltpu.VMEM((BLK + 8, 2 * LANES), F32),
            pltpu.VMEM((SSM_GROUPS * SSM_STATE, SSM_WIDTH), F32),
        ],
        compiler_params=_cparams(("parallel", "arbitrary")),
        name="ssd_mixer",
    )(pf, pf, pf, pf, cwx, cwb, cbx, cbb, dtb, alog, dskip, ng)


def _rotary(u, cos, sin_signed):
    first = (lax.broadcasted_iota(jnp.int32, u.shape, 1) % RET_KEY_DIM) < (RET_KEY_DIM // 2)
    half = RET_KEY_DIM // 2
    swapped = jnp.where(first, pltpu.roll(u, u.shape[1] - half, axis=1), pltpu.roll(u, half, axis=1))
    return u * cos + swapped * sin_signed


def _ret_kernel(q_ref, k_ref, v_ref, g_ref, cos_ref, sin_ref, dmat_ref, qdec_ref, kdec_ref, gdec_ref,
                bd_ref, gain_ref, o_ref, st_ref):
    j = pl.program_id(1)

    @pl.when(j == 0)
    def _reset():
        st_ref[...] = jnp.zeros(st_ref.shape, F32)

    cos = cos_ref[...]
    sin = sin_ref[...]
    q = _rotary(q_ref[...], cos, sin)
    k = _rotary(k_ref[...], cos, sin) * (RET_KEY_DIM ** -0.5)
    v_b = v_ref[...].astype(BF16)
    q_b = q.astype(BF16)
    k_b = k.astype(BF16)

    st = st_ref[...]
    o = jnp.dot((q * qdec_ref[...]).astype(BF16), st.astype(BF16), preferred_element_type=F32)
    for h in range(RET_HEADS):
        cm = _group_mask(q_b.shape, 1, RET_KEY_DIM, h)
        qh = jnp.where(cm, q_b, jnp.zeros_like(q_b))
        s = lax.dot_general(qh, k_b, (((1,), (1,)), ((), ())), preferred_element_type=F32) * dmat_ref[h]
        oh = jnp.dot(s.astype(BF16), v_b, preferred_element_type=F32)
        o = o + jnp.where(cm, oh, 0.0)

    upd = jnp.dot((k * kdec_ref[...]).T.astype(BF16), v_b, preferred_element_type=F32)
    st_ref[...] = st * gdec_ref[...] + upd * bd_ref[...]

    out = jnp.zeros(o.shape, F32)
    for h in range(RET_HEADS):
        cm = _group_mask(o.shape, 1, RET_KEY_DIM, h)
        ms = jnp.sum(jnp.where(cm, o * o, 0.0), axis=-1, keepdims=True) * (1.0 / RET_KEY_DIM)
        out = out + jnp.where(cm, o * lax.rsqrt(ms + EPS), 0.0)
    o_ref[...] = (_silu(g_ref[...]) * (out * gain_ref[...])).astype(BF16)


def _retention(pf, cos, sin, dmat, qdec, kdec, gdec, bd, gain, batch, seq):
    nb = seq // BLK
    row = lambda c: pl.BlockSpec((BLK, RET_WIDTH), lambda b, j: (b * nb + j, c))
    const2 = pl.BlockSpec((BLK, RET_WIDTH), lambda b, j: (0, 0))
    return pl.pallas_call(
        _ret_kernel,
        grid=(batch, nb),
        in_specs=[
            row(C_RQ // RET_WIDTH), row(C_RK // RET_WIDTH), row(C_RV // RET_WIDTH), row(C_RG // RET_WIDTH),
            pl.BlockSpec((BLK, RET_WIDTH), lambda b, j: (j, 0)),
            pl.BlockSpec((BLK, RET_WIDTH), lambda b, j: (j, 0)),
            pl.BlockSpec((RET_HEADS, BLK, BLK), lambda b, j: (0, 0, 0)),
            const2, const2, const2, const2,
            pl.BlockSpec((1, RET_WIDTH), lambda b, j: (0, 0)),
        ],
        out_specs=pl.BlockSpec((BLK, RET_WIDTH), lambda b, j: (b * nb + j, 0)),
        out_shape=jax.ShapeDtypeStruct((batch * seq, RET_WIDTH), BF16),
        scratch_shapes=[pltpu.VMEM((RET_WIDTH, RET_WIDTH), F32)],
        compiler_params=_cparams(("parallel", "arbitrary")),
        name="retention",
    )(pf, pf, pf, pf, cos, sin, dmat, qdec, kdec, gdec, bd, gain)


def _retention_tables(seq):
    half = RET_KEY_DIM // 2
    inv = 1.0 / (ROPE_BASE ** (jnp.arange(0, RET_KEY_DIM, 2, dtype=F32) / RET_KEY_DIM))
    ang = jnp.arange(seq, dtype=F32)[:, None] * inv[None, :]
    cos_h = jnp.concatenate([jnp.cos(ang), jnp.cos(ang)], axis=-1)
    sin_h = jnp.concatenate([-jnp.sin(ang), jnp.sin(ang)], axis=-1)
    cos = jnp.tile(cos_h, (1, RET_HEADS))
    sin = jnp.tile(sin_h, (1, RET_HEADS))
    log_gamma = jnp.log1p(-jnp.power(2.0, -5.0 - jnp.arange(RET_HEADS, dtype=F32)))
    idx = jnp.arange(BLK, dtype=F32)
    ii = jnp.arange(BLK)
    same_or_earlier_chunk = (ii[None, :] // CHUNK) <= (ii[:, None] // CHUNK)
    dmat = jnp.exp(log_gamma[:, None, None] * jnp.abs(idx[:, None] - idx[None, :]))
    dmat = jnp.where(same_or_earlier_chunk[None], dmat, 0.0)
    lg_cols = jnp.repeat(log_gamma, RET_KEY_DIM)
    qdec = jnp.exp(lg_cols[None, :] * (idx + 1.0)[:, None])
    kdec = jnp.exp(lg_cols[None, :] * (BLK - 1.0 - idx)[:, None])
    head_of = jnp.arange(RET_WIDTH) // RET_KEY_DIM
    bd = (head_of[:, None] == head_of[None, :]).astype(F32)
    gdec = jnp.exp(lg_cols * BLK)[:, None] * bd
    return cos, sin, dmat, qdec, kdec, gdec, bd


def _outproj_kernel(x_ref, a_ref, b_ref, c_ref, w_ref, g_ref, *rest, with_router):
    if with_router:
        rw_ref, xo_ref, h_ref, comb_ref = rest
    else:
        xo_ref, h_ref = rest
    acc = x_ref[...]
    acc = acc + jnp.dot(a_ref[...], w_ref[0:DA_WIDTH, :], preferred_element_type=F32)
    acc = acc + jnp.dot(b_ref[...], w_ref[DA_WIDTH:DA_WIDTH + SSM_WIDTH, :], preferred_element_type=F32)
    acc = acc + jnp.dot(c_ref[...], w_ref[DA_WIDTH + SSM_WIDTH:, :], preferred_element_type=F32)
    xo_ref[...] = acc
    ms = jnp.mean(acc * acc, axis=-1, keepdims=True)
    h = (acc * lax.rsqrt(ms + EPS) * g_ref[...]).astype(BF16)
    h_ref[...] = h
    if with_router:
        logits = jnp.dot(h, rw_ref[...], preferred_element_type=F32)
        lane = lax.broadcasted_iota(jnp.int32, logits.shape, 1).astype(F32)
        logits = jnp.where(lane < N_EXPERTS, logits, NEG_BIG)
        v1 = jnp.max(logits, axis=-1, keepdims=True)
        i1 = jnp.min(jnp.where(logits == v1, lane, float(LANES)), axis=-1, keepdims=True)
        rest_l = jnp.where(lane == i1, NEG_BIG, logits)
        v2 = jnp.max(rest_l, axis=-1, keepdims=True)
        i2 = jnp.min(jnp.where(rest_l == v2, lane, float(LANES)), axis=-1, keepdims=True)
        e2 = jnp.exp(v2 - v1)
        g1 = 1.0 / (1.0 + e2)
        g2 = e2 * g1
        comb_ref[...] = jnp.where(lane == i1, g1, 0.0) + jnp.where(lane == i2, g2, 0.0)


def _outproj(x2, oa, ob, oc, w, gain, router_w=None):
    t = x2.shape[0]
    with_router = router_w is not None
    tok = lambda w_: pl.BlockSpec((TM_PROJ, w_), lambda i: (i, 0))
    in_specs = [tok(D_MODEL), tok(DA_WIDTH), tok(SSM_WIDTH), tok(RET_WIDTH),
                pl.BlockSpec((D_MODEL, D_MODEL), lambda i: (0, 0)),
                pl.BlockSpec((1, D_MODEL), lambda i: (0, 0))]
    out_specs = [tok(D_MODEL), tok(D_MODEL)]
    out_shape = [jax.ShapeDtypeStruct((t, D_MODEL), F32), jax.ShapeDtypeStruct((t, D_MODEL), BF16)]
    args = [x2, oa, ob, oc, w, gain]
    if with_router:
        in_specs.append(pl.BlockSpec((D_MODEL, LANES), lambda i: (0, 0)))
        out_specs.append(tok(LANES))
        out_shape.append(jax.ShapeDtypeStruct((t, LANES), F32))
        args.append(router_w)
    return pl.pallas_call(
        functools.partial(_outproj_kernel, with_router=with_router),
        grid=(t // TM_PROJ,),
        in_specs=in_specs,
        out_specs=out_specs,
        out_shape=out_shape,
        compiler_params=_cparams(("parallel",)),
        name="outproj_router" if with_router else "outproj",
    )(*args)


def _ffn_kernel(x_ref, h_ref, wg_ref, wu_ref, wd_ref, o_ref, acc_ref):
    f = pl.program_id(1)

    @pl.when(f == 0)
    def _init():
        acc_ref[...] = x_ref[...]

    h = h_ref[...]
    gate = jnp.dot(h, wg_ref[...], preferred_element_type=F32)
    up = jnp.dot(h, wu_ref[...], preferred_element_type=F32)
    act = (_silu(gate) * up).astype(BF16)
    acc_ref[...] += jnp.dot(act, wd_ref[...], preferred_element_type=F32)

    @pl.when(f == pl.num_programs(1) - 1)
    def _done():
        o_ref[...] = acc_ref[...]


def _dense_ffn(x2, h, wg, wu, wd):
    t = x2.shape[0]
    nf = wg.shape[1] // TF_FFN
    return pl.pallas_call(
        _ffn_kernel,
        grid=(t // TM_FFN, nf),
        in_specs=[
            pl.BlockSpec((TM_FFN, D_MODEL), lambda i, f: (i, 0)),
            pl.BlockSpec((TM_FFN, D_MODEL), lambda i, f: (i, 0)),
            pl.BlockSpec((D_MODEL, TF_FFN), lambda i, f: (0, f)),
            pl.BlockSpec((D_MODEL, TF_FFN), lambda i, f: (0, f)),
            pl.BlockSpec((TF_FFN, D_MODEL), lambda i, f: (f, 0)),
        ],
        out_specs=pl.BlockSpec((TM_FFN, D_MODEL), lambda i, f: (i, 0)),
        out_shape=jax.ShapeDtypeStruct((t, D_MODEL), F32),
        scratch_shapes=[pltpu.VMEM((TM_FFN, D_MODEL), F32)],
        compiler_params=_cparams(("parallel", "arbitrary")),
        name="dense_swiglu",
    )(x2, h, wg, wu, wd)


def _moe_kernel(x_ref, h_ref, comb_ref, wg_ref, wu_ref, wd_ref, fg_ref, o_ref, acc_ref):
    e = pl.program_id(1)
    f = pl.program_id(2)

    @pl.when((e == 0) & (f == 0))
    def _init():
        acc_ref[...] = x_ref[...]

    comb = comb_ref[...]
    lane = lax.broadcasted_iota(jnp.int32, comb.shape, 1)
    ce = jnp.sum(jnp.where(lane == e, comb, 0.0), axis=-1, keepdims=True)
    h = h_ref[...]
    gate = jnp.dot(h, wg_ref[0], preferred_element_type=F32)
    up = jnp.dot(h, wu_ref[0], preferred_element_type=F32)
    act = (_silu(gate) * up * ce).astype(BF16)
    acc_ref[...] += jnp.dot(act, wd_ref[0], preferred_element_type=F32)

    @pl.when((e == pl.num_programs(1) - 1) & (f == pl.num_programs(2) - 1))
    def _done():
        y = acc_ref[...]
        ms = jnp.mean(y * y, axis=-1, keepdims=True)
        o_ref[...] = y * lax.rsqrt(ms + EPS) * fg_ref[...]


def _moe_dense(x2, h, comb, wg, wu, wd, final_gain):
    t = x2.shape[0]
    nf = wg.shape[2] // TF_MOE
    return pl.pallas_call(
        _moe_kernel,
        grid=(t // TM_FFN, N_EXPERTS, nf),
        in_specs=[
            pl.BlockSpec((TM_FFN, D_MODEL), lambda i, e, f: (i, 0)),
            pl.BlockSpec((TM_FFN, D_MODEL), lambda i, e, f: (i, 0)),
            pl.BlockSpec((TM_FFN, LANES), lambda i, e, f: (i, 0)),
            pl.BlockSpec((1, D_MODEL, TF_MOE), lambda i, e, f: (e, 0, f)),
            pl.BlockSpec((1, D_MODEL, TF_MOE), lambda i, e, f: (e, 0, f)),
            pl.BlockSpec((1, TF_MOE, D_MODEL), lambda i, e, f: (e, f, 0)),
            pl.BlockSpec((1, D_MODEL), lambda i, e, f: (0, 0)),
        ],
        out_specs=pl.BlockSpec((TM_FFN, D_MODEL), lambda i, e, f: (i, 0)),
        out_shape=jax.ShapeDtypeStruct((t, D_MODEL), F32),
        scratch_shapes=[pltpu.VMEM((TM_FFN, D_MODEL), F32)],
        compiler_params=_cparams(("parallel", "arbitrary", "arbitrary")),
        name="moe_swiglu",
    )(x2, h, comb, wg, wu, wd, final_gain)


def _permute_w_in(w):
    sizes = (256, 256, 256, 512, 768, 8, 256, 256, 256, 256)
    offs = np.concatenate([[0], np.cumsum(sizes)])
    part = lambda n: w[:, offs[n]:offs[n + 1]]
    dt_pad = jnp.zeros((w.shape[0], LANES - SSM_HEADS), w.dtype)
    cols = [part(0) * (DA_HEAD_DIM ** -0.5), part(1), part(2), part(3), part(4),
            part(6), part(7), part(8), part(9), part(5), dt_pad]
    return jnp.concatenate(cols, axis=1).astype(BF16)


def _pad_lanes(v, fill=0.0):
    return jnp.concatenate([v, jnp.full((LANES - v.shape[0],), fill, v.dtype)])[None, :]


def kernel(x, w_in, w_out, attn_norm, ffn_norm, final_norm, rel_bias, lambda_q1, lambda_k1, lambda_q2,
           lambda_k2, da_head_norm, conv_w, conv_b, dt_bias, a_log, d_skip, ssm_norm, ret_head_norm,
           w_gate, w_up, w_down, router_w, e_gate, e_up, e_down):
    batch, seq, _ = x.shape
    x2 = x.reshape(batch * seq, D_MODEL)
    bias_tiles = _bias_tiles(rel_bias)
    ret_tabs = _retention_tables(seq)

    for layer in range(DEPTH):
        lam_init = 0.8 - 0.6 * math.exp(-0.3 * layer)
        lam = (jnp.exp(jnp.sum(lambda_q1[layer] * lambda_k1[layer]))
               - jnp.exp(jnp.sum(lambda_q2[layer] * lambda_k2[layer])) + lam_init).astype(F32)
        lam = jnp.full((1, DA_WIDTH), lam, F32)

        qkv, pf = _inproj(x2, attn_norm[layer][None, :], _permute_w_in(w_in[layer]))

        da_gain = (jnp.tile(da_head_norm[layer], DA_HEADS) * (1.0 - lam_init))[None, :]
        out_a = _attention(qkv, bias_tiles, da_gain, lam, batch, seq)

        cw, cb = conv_w[layer], conv_b[layer]
        out_b = _ssd(pf, cw[:, :SSM_WIDTH], cw[:, SSM_WIDTH:], cb[None, :SSM_WIDTH], cb[None, SSM_WIDTH:],
                     _pad_lanes(dt_bias[layer]), _pad_lanes(a_log[layer]),
                     jnp.repeat(d_skip[layer], SSM_HEAD_DIM)[None, :], ssm_norm[layer][None, :], batch, seq)

        ret_gain = jnp.tile(ret_head_norm[layer], RET_HEADS)[None, :]
        out_c = _retention(pf, *ret_tabs, ret_gain, batch, seq)

        w_o = w_out[layer].astype(BF16)
        if layer % 2 == 0:
            i = layer // 2
            x2, h = _outproj(x2, out_a, out_b, out_c, w_o, ffn_norm[layer][None, :])
            pad = FFN_PAD - FFN_DENSE
            wg = jnp.pad(w_gate[i], ((0, 0), (0, pad))).astype(BF16)
            wu = jnp.pad(w_up[i], ((0, 0), (0, pad))).astype(BF16)
            wd = jnp.pad(w_down[i], ((0, pad), (0, 0))).astype(BF16)
            x2 = _dense_ffn(x2, h, wg, wu, wd)
        else:
            i = layer // 2
            rw = jnp.pad(router_w[i], ((0, 0), (0, LANES - N_EXPERTS))).astype(BF16)
            x2, h, comb = _outproj(x2, out_a, out_b, out_c, w_o, ffn_norm[layer][None, :], rw)
            x2 = _moe_dense(x2, h, comb, e_gate[i].astype(BF16), e_up[i].astype(BF16), e_down[i].astype(BF16),
                            final_norm[None, :])
    return x2.reshape(batch, seq, D_MODEL)
```

```python
import functools
import math

import jax
import jax.numpy as jnp
import numpy as np
from jax import lax
from jax.experimental import pallas as pl
from jax.experimental.pallas import tpu as pltpu

F32 = jnp.float32
BF16 = jnp.bfloat16

D_MODEL = 1024
DEPTH = 2
CHUNK = 64
EPS = 1e-6
DA_HEADS = 4
DA_HEAD_DIM = 32
DA_V_DIM = 64
DA_WIDTH = 256
SSM_HEADS = 8
SSM_HEAD_DIM = 64
SSM_WIDTH = 512
SSM_STATE = 64
SSM_GROUPS = 2
SSM_CONV = 4
RET_HEADS = 4
RET_KEY_DIM = 64
RET_WIDTH = 256
ROPE_BASE = 10000.0
REL_BUCKETS = 32
REL_MAX_DIST = 128
FFN_DENSE = 2752
N_EXPERTS = 8
FFN_EXPERT = 3584

LANES = 128
VMEM_LIMIT = 48 * 1024 * 1024
NEG_BIG = -1e30

TM_PROJ = 512
ATT_T = 512
BLK = 256
FFN_PAD = 2816
TM_FFN = 1024
TF_FFN = 256
TF_MOE = 512
TR_MOE = 512
TM_COMB = 256
TOP_K = 2

C_Z, C_XS, C_BC, C_RQ, C_RK, C_RV, C_RG, C_DT = 0, 512, 1024, 1280, 1536, 1792, 2048, 2304
PF_W = 2432
PF_CHUNK = 640
QKV_W = 768


def _cparams(sem):
    return pltpu.CompilerParams(dimension_semantics=sem, vmem_limit_bytes=VMEM_LIMIT)


def _group_mask(shape, axis, group, idx):
    io = lax.broadcasted_iota(jnp.int32, shape, axis)
    return (io >= idx * group) & (io < (idx + 1) * group)


def _lane_tile(x, n, axis=1):
    return jnp.concatenate([x] * n, axis=axis)


def _silu(x):
    return x * (1.0 / (1.0 + jnp.exp(-x)))


def _inproj_kernel(x_ref, g_ref, w_ref, qkv_ref, pf_ref):
    x = x_ref[...]
    ms = jnp.mean(x * x, axis=-1, keepdims=True)
    h = (x * lax.rsqrt(ms + EPS) * g_ref[...]).astype(BF16)
    qkv_ref[...] = jnp.dot(h, w_ref[:, :QKV_W], preferred_element_type=F32).astype(BF16)
    for lo in range(0, PF_W, PF_CHUNK):
        hi = min(lo + PF_CHUNK, PF_W)
        pf_ref[:, lo:hi] = jnp.dot(h, w_ref[:, QKV_W + lo:QKV_W + hi], preferred_element_type=F32)


def _inproj(x2, gain, w):
    t = x2.shape[0]
    return pl.pallas_call(
        _inproj_kernel,
        grid=(t // TM_PROJ,),
        in_specs=[
            pl.BlockSpec((TM_PROJ, D_MODEL), lambda i: (i, 0)),
            pl.BlockSpec((1, D_MODEL), lambda i: (0, 0)),
            pl.BlockSpec((D_MODEL, QKV_W + PF_W), lambda i: (0, 0)),
        ],
        out_specs=[
            pl.BlockSpec((TM_PROJ, QKV_W), lambda i: (i, 0)),
            pl.BlockSpec((TM_PROJ, PF_W), lambda i: (i, 0)),
        ],
        out_shape=[
            jax.ShapeDtypeStruct((t, QKV_W), BF16),
            jax.ShapeDtypeStruct((t, PF_W), F32),
        ],
        compiler_params=_cparams(("parallel",)),
        name="inproj",
    )(x2, gain, w)


def _attn_kernel(qi_ref, kj_ref, q_ref, k_ref, v_ref, bias_ref, gain_ref, lam_ref, o_ref,
                 qm_ref, m_ref, l_ref, acc_ref):
    p = pl.program_id(1)
    qi = qi_ref[p]
    kj = kj_ref[p]
    n_ht = 2 * DA_HEADS

    @pl.when(kj == 0)
    def _init():
        q = q_ref[...]
        for ht in range(n_ht):
            qm_ref[ht] = jnp.where(_group_mask(q.shape, 1, DA_HEAD_DIM, ht), q, jnp.zeros_like(q))
        m_ref[...] = jnp.full(m_ref.shape, NEG_BIG, F32)
        l_ref[...] = jnp.zeros(l_ref.shape, F32)
        acc_ref[...] = jnp.zeros(acc_ref.shape, F32)

    k = k_ref[...]
    v = v_ref[...]
    for ht in range(n_ht):
        h = ht // 2
        s = lax.dot_general(qm_ref[ht], k, (((1,), (1,)), ((), ())), preferred_element_type=F32)
        s = s + bias_ref[0, h]
        m_prev = m_ref[ht]
        m_new = jnp.maximum(m_prev, jnp.max(s, axis=-1, keepdims=True))
        alpha = jnp.exp(m_prev - m_new)
        e = jnp.exp(s - _lane_tile(m_new, ATT_T // LANES, axis=1))
        l_ref[ht] = alpha * l_ref[ht] + jnp.sum(e, axis=-1, keepdims=True)
        m_ref[ht] = m_new
        pv = jnp.dot(e.astype(BF16), v, preferred_element_type=F32)
        acc_ref[ht] = acc_ref[ht] * _lane_tile(alpha, DA_WIDTH // LANES, axis=1) + pv

    @pl.when(kj == qi)
    def _finish():
        lam = lam_ref[...]
        out = jnp.zeros((ATT_T, DA_WIDTH), F32)
        for h in range(DA_HEADS):
            inv1 = _lane_tile(1.0 / l_ref[2 * h], DA_WIDTH // LANES, axis=1)
            inv2 = _lane_tile(1.0 / l_ref[2 * h + 1], DA_WIDTH // LANES, axis=1)
            o = acc_ref[2 * h] * inv1 - lam * (acc_ref[2 * h + 1] * inv2)
            cm = _group_mask(o.shape, 1, DA_V_DIM, h)
            ms = jnp.sum(jnp.where(cm, o * o, 0.0), axis=-1, keepdims=True) * (1.0 / DA_V_DIM)
            out = out + jnp.where(cm, o * lax.rsqrt(ms + EPS), 0.0)
        o_ref[...] = (out * gain_ref[...]).astype(BF16)


def _attention(qkv, bias_tiles, gain, lam, batch, seq):
    nq = seq // ATT_T
    qi_l, kj_l = [], []
    for qi in range(nq):
        for kj in range(qi + 1):
            qi_l.append(qi)
            kj_l.append(kj)
    qi_tbl = jnp.asarray(np.array(qi_l, np.int32))
    kj_tbl = jnp.asarray(np.array(kj_l, np.int32))
    n_pairs = len(qi_l)
    n_ht = 2 * DA_HEADS

    grid_spec = pltpu.PrefetchScalarGridSpec(
        num_scalar_prefetch=2,
        grid=(batch, n_pairs),
        in_specs=[
            pl.BlockSpec((ATT_T, DA_WIDTH), lambda b, p, qi, kj: (b * nq + qi[p], 0)),
            pl.BlockSpec((ATT_T, DA_WIDTH), lambda b, p, qi, kj: (b * nq + kj[p], 1)),
            pl.BlockSpec((ATT_T, DA_WIDTH), lambda b, p, qi, kj: (b * nq + kj[p], 2)),
            pl.BlockSpec((1, DA_HEADS, ATT_T, ATT_T),
                         lambda b, p, qi, kj: (jnp.minimum(qi[p] - kj[p], 2), 0, 0, 0)),
            pl.BlockSpec((1, DA_WIDTH), lambda b, p, qi, kj: (0, 0)),
            pl.BlockSpec((1, DA_WIDTH), lambda b, p, qi, kj: (0, 0)),
        ],
        out_specs=pl.BlockSpec((ATT_T, DA_WIDTH), lambda b, p, qi, kj: (b * nq + qi[p], 0)),
        scratch_shapes=[
            pltpu.VMEM((n_ht, ATT_T, DA_WIDTH), BF16),
            pltpu.VMEM((n_ht, ATT_T, LANES), F32),
            pltpu.VMEM((n_ht, ATT_T, LANES), F32),
            pltpu.VMEM((n_ht, ATT_T, DA_WIDTH), F32),
        ],
    )
    return pl.pallas_call(
        _attn_kernel,
        grid_spec=grid_spec,
        out_shape=jax.ShapeDtypeStruct((batch * seq, DA_WIDTH), BF16),
        compiler_params=_cparams(("parallel", "arbitrary")),
        name="diff_attention",
    )(qi_tbl, kj_tbl, qkv, qkv, qkv, bias_tiles, gain, lam)


def _t5_bucket(rel):
    half = REL_BUCKETS // 2
    max_exact = half // 2
    ret = jnp.where(rel > 0, half, 0)
    n = jnp.abs(rel)
    nf = jnp.maximum(n, 1).astype(F32)
    large = max_exact + (jnp.log(nf / max_exact) / math.log(REL_MAX_DIST / max_exact)
                         * (half - max_exact)).astype(jnp.int32)
    large = jnp.minimum(large, half - 1)
    return ret + jnp.where(n < max_exact, n, large)


def _bias_tiles(rel_bias):
    i = jnp.arange(ATT_T)[:, None]
    j = jnp.arange(ATT_T)[None, :]
    tiles = []
    for off in range(3):
        bucket = _t5_bucket((j - off * ATT_T) - i)
        b = jnp.zeros((DA_HEADS, ATT_T, ATT_T), F32)
        for c in range(REL_BUCKETS):
            b = jnp.where((bucket == c)[None], rel_bias[c].astype(F32)[:, None, None], b)
        if off == 0:
            allowed = (j // CHUNK) <= (i // CHUNK)
            b = jnp.where(allowed[None], b, NEG_BIG)
        tiles.append(b)
    return jnp.stack(tiles)


def _cumsum_rows(x):
    n = x.shape[0]
    row = lax.broadcasted_iota(jnp.int32, x.shape, 0)
    sh = 1
    while sh < n:
        x = x + jnp.where(row >= sh, pltpu.roll(x, sh, axis=0), 0.0)
        sh *= 2
    return x


def _expand_heads(v, width, group):
    out = jnp.zeros((v.shape[0], width), F32)
    for h in range(width // group):
        col = jnp.broadcast_to(v[:, h:h + 1], (v.shape[0], width))
        out = jnp.where(_group_mask(out.shape, 1, group, h), col, out)
    return out


def _ssd_kernel(z_ref, xs_ref, bc_ref, dt_ref, cwx_ref, cwb_ref, cbx_ref, cbb_ref, dtb_ref, alog_ref,
                dskip_ref, ng_ref, o_ref, extx_ref, extb_ref, st_ref):
    j = pl.program_id(1)

    @pl.when(j == 0)
    def _reset():
        extx_ref[0:8, :] = jnp.zeros((8, SSM_WIDTH), F32)
        extb_ref[0:8, :] = jnp.zeros((8, 2 * LANES), F32)
        st_ref[...] = jnp.zeros(st_ref.shape, F32)

    extx_ref[8:8 + BLK, :] = xs_ref[...]
    extb_ref[8:8 + BLK, :] = bc_ref[...]
    yx = jnp.broadcast_to(cbx_ref[...], (BLK, SSM_WIDTH))
    yb = jnp.broadcast_to(cbb_ref[...], (BLK, 2 * LANES))
    for w in range(SSM_CONV):
        lo = 8 - (SSM_CONV - 1) + w
        yx = yx + extx_ref[lo:lo + BLK, :] * cwx_ref[w:w + 1, :]
        yb = yb + extb_ref[lo:lo + BLK, :] * cwb_ref[w:w + 1, :]
    tail_x = extx_ref[BLK:BLK + 8, :]
    tail_b = extb_ref[BLK:BLK + 8, :]
    extx_ref[0:8, :] = tail_x
    extb_ref[0:8, :] = tail_b
    xs = _silu(yx)
    bc = _silu(yb)
    bm = bc[:, :LANES]
    cm = bc[:, LANES:]

    dt = jax.nn.softplus(dt_ref[...] + dtb_ref[...])
    a = -jnp.exp(alog_ref[...]) * dt
    a_cs = _cumsum_rows(a)
    a_cs_t = a_cs.T
    a_last = a_cs[BLK - 1:BLK, :]

    dt_full = _expand_heads(dt, SSM_WIDTH, SSM_HEAD_DIM)
    ea_full = _expand_heads(jnp.exp(a_cs), SSM_WIDTH, SSM_HEAD_DIM)
    dec_full = _expand_heads(jnp.exp(a_last - a_cs), SSM_WIDTH, SSM_HEAD_DIM)
    cdec_full = _expand_heads(jnp.exp(a_last), SSM_WIDTH, SSM_HEAD_DIM)

    xdt = xs * dt_full
    xdt_b = xdt.astype(BF16)
    cm_b = cm.astype(BF16)
    bm_b = bm.astype(BF16)

    st = st_ref[...]
    y = jnp.dot(cm_b, st.astype(BF16), preferred_element_type=F32) * ea_full + xs * dskip_ref[...]

    row = lax.broadcasted_iota(jnp.int32, (BLK, BLK), 0)
    colio = lax.broadcasted_iota(jnp.int32, (BLK, BLK), 1)
    causal = colio <= row
    rep = SSM_HEADS // SSM_GROUPS
    lane_lo = lax.broadcasted_iota(jnp.int32, (BLK, LANES), 1) < SSM_HEAD_DIM
    pieces = []
    for g in range(SSM_GROUPS):
        cg = jnp.where(_group_mask(cm_b.shape, 1, SSM_STATE, g), cm_b, jnp.zeros_like(cm_b))
        cb = lax.dot_general(cg, bm_b, (((1,), (1,)), ((), ())), preferred_element_type=F32)
        for pair in range(rep // 2):
            halves = []
            for sub in range(2):
                h = g * rep + pair * 2 + sub
                seg = jnp.broadcast_to(a_cs[:, h:h + 1], (BLK, BLK)) - a_cs_t[h:h + 1, :]
                lmat = jnp.exp(jnp.where(causal, seg, NEG_BIG))
                mh = (cb * lmat).astype(BF16)
                c0 = (h // 2) * LANES
                halves.append(jnp.dot(mh, xdt_b[:, c0:c0 + LANES], preferred_element_type=F32))
            pieces.append(jnp.where(lane_lo, halves[0], halves[1]))
    y = y + jnp.concatenate(pieces, axis=1)

    upd = jnp.dot(bm.T.astype(BF16), (xdt * dec_full).astype(BF16), preferred_element_type=F32)
    r_io = lax.broadcasted_iota(jnp.int32, upd.shape, 0) // SSM_STATE
    c_io = lax.broadcasted_iota(jnp.int32, upd.shape, 1) // (SSM_HEAD_DIM * rep)
    st_ref[...] = st * cdec_full + jnp.where(r_io == c_io, upd, 0.0)

    y = y * _silu(z_ref[...])
    gw = SSM_WIDTH // SSM_GROUPS
    for g in range(SSM_GROUPS):
        yg = y[:, g * gw:(g + 1) * gw]
        ms = jnp.mean(yg * yg, axis=-1, keepdims=True)
        o_ref[:, g * gw:(g + 1) * gw] = (yg * lax.rsqrt(ms + EPS) * ng_ref[:, g * gw:(g + 1) * gw]).astype(BF16)


def _ssd(pf, cwx, cwb, cbx, cbb, dtb, alog, dskip, ng, batch, seq):
    nb = seq // BLK
    row = lambda w, c: pl.BlockSpec((BLK, w), lambda b, j: (b * nb + j, c))
    const = lambda r, w: pl.BlockSpec((r, w), lambda b, j: (0, 0))
    return pl.pallas_call(
        _ssd_kernel,
        grid=(batch, nb),
        in_specs=[
            row(SSM_WIDTH, C_Z // SSM_WIDTH), row(SSM_WIDTH, C_XS // SSM_WIDTH), row(2 * LANES, C_BC // (2 * LANES)),
            row(LANES, C_DT // LANES),
            const(SSM_CONV, SSM_WIDTH), const(SSM_CONV, 2 * LANES), const(1, SSM_WIDTH), const(1, 2 * LANES),
            const(1, LANES), const(1, LANES), const(1, SSM_WIDTH), const(1, SSM_WIDTH),
        ],
        out_specs=pl.BlockSpec((BLK, SSM_WIDTH), lambda b, j: (b * nb + j, 0)),
        out_shape=jax.ShapeDtypeStruct((batch * seq, SSM_WIDTH), BF16),
        scratch_shapes=[
            pltpu.VMEM((BLK + 8, SSM_WIDTH), F32),
            pltpu.VMEM((BLK + 8, 2 * LANES), F32),
            pltpu.VMEM((SSM_GROUPS * SSM_STATE, SSM_WIDTH), F32),
        ],
        compiler_params=_cparams(("parallel", "arbitrary")),
        name="ssd_mixer",
    )(pf, pf, pf, pf, cwx, cwb, cbx, cbb, dtb, alog, dskip, ng)


def _rotary(u, cos, sin_signed):
    first = (lax.broadcasted_iota(jnp.int32, u.shape, 1) % RET_KEY_DIM) < (RET_KEY_DIM // 2)
    half = RET_KEY_DIM // 2
    swapped = jnp.where(first, pltpu.roll(u, u.shape[1] - half, axis=1), pltpu.roll(u, half, axis=1))
    return u * cos + swapped * sin_signed


def _ret_kernel(q_ref, k_ref, v_ref, g_ref, cos_ref, sin_ref, dmat_ref, qdec_ref, kdec_ref, gdec_ref,
                bd_ref, gain_ref, o_ref, st_ref):
    j = pl.program_id(1)

    @pl.when(j == 0)
    def _reset():
        st_ref[...] = jnp.zeros(st_ref.shape, F32)

    cos = cos_ref[...]
    sin = sin_ref[...]
    q = _rotary(q_ref[...], cos, sin)
    k = _rotary(k_ref[...], cos, sin) * (RET_KEY_DIM ** -0.5)
    v_b = v_ref[...].astype(BF16)
    q_b = q.astype(BF16)
    k_b = k.astype(BF16)

    st = st_ref[...]
    o = jnp.dot((q * qdec_ref[...]).astype(BF16), st.astype(BF16), preferred_element_type=F32)
    for h in range(RET_HEADS):
        cm = _group_mask(q_b.shape, 1, RET_KEY_DIM, h)
        qh = jnp.where(cm, q_b, jnp.zeros_like(q_b))
        s = lax.dot_general(qh, k_b, (((1,), (1,)), ((), ())), preferred_element_type=F32) * dmat_ref[h]
        oh = jnp.dot(s.astype(BF16), v_b, preferred_element_type=F32)
        o = o + jnp.where(cm, oh, 0.0)

    upd = jnp.dot((k * kdec_ref[...]).T.astype(BF16), v_b, preferred_element_type=F32)
    st_ref[...] = st * gdec_ref[...] + upd * bd_ref[...]

    out = jnp.zeros(o.shape, F32)
    for h in range(RET_HEADS):
        cm = _group_mask(o.shape, 1, RET_KEY_DIM, h)
        ms = jnp.sum(jnp.where(cm, o * o, 0.0), axis=-1, keepdims=True) * (1.0 / RET_KEY_DIM)
        out = out + jnp.where(cm, o * lax.rsqrt(ms + EPS), 0.0)
    o_ref[...] = (_silu(g_ref[...]) * (out * gain_ref[...])).astype(BF16)


def _retention(pf, cos, sin, dmat, qdec, kdec, gdec, bd, gain, batch, seq):
    nb = seq // BLK
    row = lambda c: pl.BlockSpec((BLK, RET_WIDTH), lambda b, j: (b * nb + j, c))
    const2 = pl.BlockSpec((BLK, RET_WIDTH), lambda b, j: (0, 0))
    return pl.pallas_call(
        _ret_kernel,
        grid=(batch, nb),
        in_specs=[
            row(C_RQ // RET_WIDTH), row(C_RK // RET_WIDTH), row(C_RV // RET_WIDTH), row(C_RG // RET_WIDTH),
            pl.BlockSpec((BLK, RET_WIDTH), lambda b, j: (j, 0)),
            pl.BlockSpec((BLK, RET_WIDTH), lambda b, j: (j, 0)),
            pl.BlockSpec((RET_HEADS, BLK, BLK), lambda b, j: (0, 0, 0)),
            const2, const2, const2, const2,
            pl.BlockSpec((1, RET_WIDTH), lambda b, j: (0, 0)),
        ],
        out_specs=pl.BlockSpec((BLK, RET_WIDTH), lambda b, j: (b * nb + j, 0)),
        out_shape=jax.ShapeDtypeStruct((batch * seq, RET_WIDTH), BF16),
        scratch_shapes=[pltpu.VMEM((RET_WIDTH, RET_WIDTH), F32)],
        compiler_params=_cparams(("parallel", "arbitrary")),
        name="retention",
    )(pf, pf, pf, pf, cos, sin, dmat, qdec, kdec, gdec, bd, gain)


def _retention_tables(seq):
    half = RET_KEY_DIM // 2
    inv = 1.0 / (ROPE_BASE ** (jnp.arange(0, RET_KEY_DIM, 2, dtype=F32) / RET_KEY_DIM))
    ang = jnp.arange(seq, dtype=F32)[:, None] * inv[None, :]
    cos_h = jnp.concatenate([jnp.cos(ang), jnp.cos(ang)], axis=-1)
    sin_h = jnp.concatenate([-jnp.sin(ang), jnp.sin(ang)], axis=-1)
    cos = jnp.tile(cos_h, (1, RET_HEADS))
    sin = jnp.tile(sin_h, (1, RET_HEADS))
    log_gamma = jnp.log1p(-jnp.power(2.0, -5.0 - jnp.arange(RET_HEADS, dtype=F32)))
    idx = jnp.arange(BLK, dtype=F32)
    ii = jnp.arange(BLK)
    same_or_earlier_chunk = (ii[None, :] // CHUNK) <= (ii[:, None] // CHUNK)
    dmat = jnp.exp(log_gamma[:, None, None] * jnp.abs(idx[:, None] - idx[None, :]))
    dmat = jnp.where(same_or_earlier_chunk[None], dmat, 0.0)
    lg_cols = jnp.repeat(log_gamma, RET_KEY_DIM)
    qdec = jnp.exp(lg_cols[None, :] * (idx + 1.0)[:, None])
    kdec = jnp.exp(lg_cols[None, :] * (BLK - 1.0 - idx)[:, None])
    head_of = jnp.arange(RET_WIDTH) // RET_KEY_DIM
    bd = (head_of[:, None] == head_of[None, :]).astype(F32)
    gdec = jnp.exp(lg_cols * BLK)[:, None] * bd
    return cos, sin, dmat, qdec, kdec, gdec, bd


def _outproj_kernel(x_ref, a_ref, b_ref, c_ref, w_ref, g_ref, *rest, with_router):
    if with_router:
        rw_ref, xo_ref, h_ref, rinfo_ref = rest
    else:
        xo_ref, h_ref = rest
    acc = x_ref[...]
    acc = acc + jnp.dot(a_ref[...], w_ref[0:DA_WIDTH, :], preferred_element_type=F32)
    acc = acc + jnp.dot(b_ref[...], w_ref[DA_WIDTH:DA_WIDTH + SSM_WIDTH, :], preferred_element_type=F32)
    acc = acc + jnp.dot(c_ref[...], w_ref[DA_WIDTH + SSM_WIDTH:, :], preferred_element_type=F32)
    xo_ref[...] = acc
    ms = jnp.mean(acc * acc, axis=-1, keepdims=True)
    hf = acc * lax.rsqrt(ms + EPS) * g_ref[...]
    h = hf.astype(BF16)
    h_ref[...] = hf.astype(h_ref.dtype)
    if with_router:
        logits = jnp.dot(h, rw_ref[...], preferred_element_type=F32)
        lane = lax.broadcasted_iota(jnp.int32, logits.shape, 1).astype(F32)
        logits = jnp.where(lane < N_EXPERTS, logits, NEG_BIG)
        v1 = jnp.max(logits, axis=-1, keepdims=True)
        i1 = jnp.min(jnp.where(logits == v1, lane, float(LANES)), axis=-1, keepdims=True)
        rest_l = jnp.where(lane == i1, NEG_BIG, logits)
        v2 = jnp.max(rest_l, axis=-1, keepdims=True)
        i2 = jnp.min(jnp.where(rest_l == v2, lane, float(LANES)), axis=-1, keepdims=True)
        e2 = jnp.exp(v2 - v1)
        g1 = 1.0 / (1.0 + e2)
        g2 = e2 * g1
        rinfo_ref[...] = (jnp.where(lane == 0.0, g1, 0.0) + jnp.where(lane == 1.0, g2, 0.0)
                          + jnp.where(lane == 2.0, i1, 0.0) + jnp.where(lane == 3.0, i2, 0.0))


def _outproj(x2, oa, ob, oc, w, gain, router_w=None):
    t = x2.shape[0]
    with_router = router_w is not None
    tok = lambda w_: pl.BlockSpec((TM_PROJ, w_), lambda i: (i, 0))
    in_specs = [tok(D_MODEL), tok(DA_WIDTH), tok(SSM_WIDTH), tok(RET_WIDTH),
                pl.BlockSpec((D_MODEL, D_MODEL), lambda i: (0, 0)),
                pl.BlockSpec((1, D_MODEL), lambda i: (0, 0))]
    out_specs = [tok(D_MODEL), tok(D_MODEL)]
    h_dtype = F32 if with_router else BF16
    out_shape = [jax.ShapeDtypeStruct((t, D_MODEL), F32), jax.ShapeDtypeStruct((t, D_MODEL), h_dtype)]
    args = [x2, oa, ob, oc, w, gain]
    if with_router:
        in_specs.append(pl.BlockSpec((D_MODEL, LANES), lambda i: (0, 0)))
        out_specs.append(tok(LANES))
        out_shape.append(jax.ShapeDtypeStruct((t, LANES), F32))
        args.append(router_w)
    return pl.pallas_call(
        functools.partial(_outproj_kernel, with_router=with_router),
        grid=(t // TM_PROJ,),
        in_specs=in_specs,
        out_specs=out_specs,
        out_shape=out_shape,
        compiler_params=_cparams(("parallel",)),
        name="outproj_router" if with_router else "outproj",
    )(*args)


def _ffn_kernel(x_ref, h_ref, wg_ref, wu_ref, wd_ref, o_ref, acc_ref):
    f = pl.program_id(1)

    @pl.when(f == 0)
    def _init():
        acc_ref[...] = x_ref[...]

    h = h_ref[...]
    gate = jnp.dot(h, wg_ref[...], preferred_element_type=F32)
    up = jnp.dot(h, wu_ref[...], preferred_element_type=F32)
    act = (_silu(gate) * up).astype(BF16)
    acc_ref[...] += jnp.dot(act, wd_ref[...], preferred_element_type=F32)

    @pl.when(f == pl.num_programs(1) - 1)
    def _done():
        o_ref[...] = acc_ref[...]


def _dense_ffn(x2, h, wg, wu, wd):
    t = x2.shape[0]
    nf = wg.shape[1] // TF_FFN
    return pl.pallas_call(
        _ffn_kernel,
        grid=(t // TM_FFN, nf),
        in_specs=[
            pl.BlockSpec((TM_FFN, D_MODEL), lambda i, f: (i, 0)),
            pl.BlockSpec((TM_FFN, D_MODEL), lambda i, f: (i, 0)),
            pl.BlockSpec((D_MODEL, TF_FFN), lambda i, f: (0, f)),
            pl.BlockSpec((D_MODEL, TF_FFN), lambda i, f: (0, f)),
            pl.BlockSpec((TF_FFN, D_MODEL), lambda i, f: (f, 0)),
        ],
        out_specs=pl.BlockSpec((TM_FFN, D_MODEL), lambda i, f: (i, 0)),
        out_shape=jax.ShapeDtypeStruct((t, D_MODEL), F32),
        scratch_shapes=[pltpu.VMEM((TM_FFN, D_MODEL), F32)],
        compiler_params=_cparams(("parallel", "arbitrary")),
        name="dense_swiglu",
    )(x2, h, wg, wu, wd)


def _issue_row_gather(src_hbm, idx_ref, idx_base, dst, sem, n_rows):
    def body(r, carry):
        tok = idx_ref[idx_base + r]
        pltpu.make_async_copy(src_hbm.at[pl.ds(tok, 1)], dst.at[pl.ds(r, 1)], sem).start()
        return carry
    lax.fori_loop(0, n_rows, body, 0, unroll=8)


def _wait_row_gather(src_hbm, dst, sem, n_rows):
    pltpu.make_async_copy(src_hbm.at[pl.ds(0, n_rows)], dst, sem).wait()


def _moe_ffn_kernel(te_ref, nv_ref, rt_ref, h_hbm, wg_ref, wu_ref, wd_ref, y_ref, hbuf, xb_ref, acc_ref, sem):
    i = pl.program_id(0)
    f = pl.program_id(1)
    n_valid = nv_ref[0]
    valid = i < n_valid
    slot = i % 2

    @pl.when((f == 0) & (i == 0) & valid)
    def _first():
        _issue_row_gather(h_hbm, rt_ref, 0, hbuf.at[0], sem.at[0], TR_MOE)

    @pl.when((f == 0) & valid)
    def _arrive():
        _wait_row_gather(h_hbm, hbuf.at[slot], sem.at[slot], TR_MOE)
        xb_ref[...] = hbuf[slot].astype(BF16)

    @pl.when((f == 0) & (i + 1 < n_valid))
    def _prefetch():
        _issue_row_gather(h_hbm, rt_ref, (i + 1) * TR_MOE, hbuf.at[1 - slot], sem.at[1 - slot], TR_MOE)

    @pl.when(valid)
    def _compute():
        xb = xb_ref[...]
        gate = jnp.dot(xb, wg_ref[0], preferred_element_type=F32)
        up = jnp.dot(xb, wu_ref[0], preferred_element_type=F32)
        act = (_silu(gate) * up).astype(BF16)
        contrib = jnp.dot(act, wd_ref[0], preferred_element_type=F32)

        @pl.when(f == 0)
        def _():
            acc_ref[...] = contrib

        @pl.when(f > 0)
        def _():
            acc_ref[...] += contrib

    @pl.when(f == pl.num_programs(1) - 1)
    def _store():
        y_ref[...] = jnp.where(valid, acc_ref[...], 0.0)


def _moe_ffn(h, tile_expert, n_valid, row_token, wg, wu, wd):
    n_tiles = tile_expert.shape[0]
    nf = wg.shape[2] // TF_MOE

    def w_in_map(i, f, te, nv, rt):
        return (te[i], 0, jnp.where(i < nv[0], f, nf - 1))

    def w_out_map(i, f, te, nv, rt):
        return (te[i], jnp.where(i < nv[0], f, nf - 1), 0)

    grid_spec = pltpu.PrefetchScalarGridSpec(
        num_scalar_prefetch=3,
        grid=(n_tiles, nf),
        in_specs=[
            pl.BlockSpec(memory_space=pl.ANY),
            pl.BlockSpec((1, D_MODEL, TF_MOE), w_in_map),
            pl.BlockSpec((1, D_MODEL, TF_MOE), w_in_map),
            pl.BlockSpec((1, TF_MOE, D_MODEL), w_out_map),
        ],
        out_specs=pl.BlockSpec((TR_MOE, D_MODEL), lambda i, f, te, nv, rt: (i, 0)),
        scratch_shapes=[
            pltpu.VMEM((2, TR_MOE, D_MODEL), F32),
            pltpu.VMEM((TR_MOE, D_MODEL), BF16),
            pltpu.VMEM((TR_MOE, D_MODEL), F32),
            pltpu.SemaphoreType.DMA((2,)),
        ],
    )
    return pl.pallas_call(
        _moe_ffn_kernel,
        grid_spec=grid_spec,
        out_shape=jax.ShapeDtypeStruct((n_tiles * TR_MOE, D_MODEL), F32),
        compiler_params=_cparams(("arbitrary", "arbitrary")),
        name="moe_expert_ffn",
    )(tile_expert, n_valid, row_token, h, wg, wu, wd)


def _moe_combine_kernel(tr_ref, x_ref, rinfo_ref, y_hbm, fg_ref, o_ref, ybuf, sem):
    i = pl.program_id(0)
    slot = i % 2
    n_rows = TOP_K * TM_COMB

    @pl.when(i == 0)
    def _first():
        _issue_row_gather(y_hbm, tr_ref, 0, ybuf.at[0], sem.at[0], n_rows)

    _wait_row_gather(y_hbm, ybuf.at[slot], sem.at[slot], n_rows)

    @pl.when(i + 1 < pl.num_programs(0))
    def _prefetch():
        _issue_row_gather(y_hbm, tr_ref, (i + 1) * n_rows, ybuf.at[1 - slot], sem.at[1 - slot], n_rows)

    rinfo = rinfo_ref[...]
    g1 = jnp.broadcast_to(rinfo[:, 0:1], (TM_COMB, D_MODEL))
    g2 = jnp.broadcast_to(rinfo[:, 1:2], (TM_COMB, D_MODEL))
    y = x_ref[...] + g1 * ybuf[slot, 0:TM_COMB, :] + g2 * ybuf[slot, TM_COMB:n_rows, :]
    ms = jnp.mean(y * y, axis=-1, keepdims=True)
    o_ref[...] = y * lax.rsqrt(ms + EPS) * fg_ref[...]


def _moe_combine(x2, rinfo, y_rows, tok_rows, final_gain):
    t = x2.shape[0]
    grid_spec = pltpu.PrefetchScalarGridSpec(
        num_scalar_prefetch=1,
        grid=(t // TM_COMB,),
        in_specs=[
            pl.BlockSpec((TM_COMB, D_MODEL), lambda i, tr: (i, 0)),
            pl.BlockSpec((TM_COMB, LANES), lambda i, tr: (i, 0)),
            pl.BlockSpec(memory_space=pl.ANY),
            pl.BlockSpec((1, D_MODEL), lambda i, tr: (0, 0)),
        ],
        out_specs=pl.BlockSpec((TM_COMB, D_MODEL), lambda i, tr: (i, 0)),
        scratch_shapes=[
            pltpu.VMEM((2, TOP_K * TM_COMB, D_MODEL), F32),
            pltpu.SemaphoreType.DMA((2,)),
        ],
    )
    return pl.pallas_call(
        _moe_combine_kernel,
        grid_spec=grid_spec,
        out_shape=jax.ShapeDtypeStruct((t, D_MODEL), F32),
        compiler_params=_cparams(("arbitrary",)),
        name="moe_combine_norm",
    )(tok_rows, x2, rinfo, y_rows, final_gain)


def _routing_tables(rinfo):
    t = rinfo.shape[0]
    n_tiles = (TOP_K * t) // TR_MOE + N_EXPERTS
    flat_e = rinfo[:, 2:2 + TOP_K].astype(jnp.int32).reshape(-1)
    onehot = (flat_e[:, None] == jnp.arange(N_EXPERTS)[None, :]).astype(jnp.int32)
    csum = jnp.cumsum(onehot, axis=0)
    rank = jnp.sum((csum - onehot) * onehot, axis=1)
    counts = csum[-1]
    tiles_per_e = (counts + TR_MOE - 1) // TR_MOE
    tile_end = jnp.cumsum(tiles_per_e)
    row_off = (tile_end - tiles_per_e) * TR_MOE
    dest = row_off[flat_e] + rank
    token = jnp.arange(TOP_K * t, dtype=jnp.int32) // TOP_K
    row_token = jnp.zeros((n_tiles * TR_MOE,), jnp.int32).at[dest].set(token, unique_indices=True)
    n_valid = tile_end[-1:].astype(jnp.int32)
    tile_expert = jnp.searchsorted(tile_end, jnp.arange(n_tiles), side="right").astype(jnp.int32)
    last_e = jnp.searchsorted(tile_end, n_valid[0] - 1, side="right").astype(jnp.int32)
    tile_expert = jnp.where(jnp.arange(n_tiles) < n_valid[0], tile_expert, last_e)
    tok_rows = dest.reshape(t // TM_COMB, TM_COMB, TOP_K).transpose(0, 2, 1).reshape(-1).astype(jnp.int32)
    return tile_expert, n_valid, row_token, tok_rows


def _permute_w_in(w):
    sizes = (256, 256, 256, 512, 768, 8, 256, 256, 256, 256)
    offs = np.concatenate([[0], np.cumsum(sizes)])
    part = lambda n: w[:, offs[n]:offs[n + 1]]
    dt_pad = jnp.zeros((w.shape[0], LANES - SSM_HEADS), w.dtype)
    cols = [part(0) * (DA_HEAD_DIM ** -0.5), part(1), part(2), part(3), part(4),
            part(6), part(7), part(8), part(9), part(5), dt_pad]
    return jnp.concatenate(cols, axis=1).astype(BF16)


def _pad_lanes(v, fill=0.0):
    return jnp.concatenate([v, jnp.full((LANES - v.shape[0],), fill, v.dtype)])[None, :]


def kernel(x, w_in, w_out, attn_norm, ffn_norm, final_norm, rel_bias, lambda_q1, lambda_k1, lambda_q2,
           lambda_k2, da_head_norm, conv_w, conv_b, dt_bias, a_log, d_skip, ssm_norm, ret_head_norm,
           w_gate, w_up, w_down, router_w, e_gate, e_up, e_down):
    batch, seq, _ = x.shape
    x2 = x.reshape(batch * seq, D_MODEL)
    bias_tiles = _bias_tiles(rel_bias)
    ret_tabs = _retention_tables(seq)

    for layer in range(DEPTH):
        lam_init = 0.8 - 0.6 * math.exp(-0.3 * layer)
        lam = (jnp.exp(jnp.sum(lambda_q1[layer] * lambda_k1[layer]))
               - jnp.exp(jnp.sum(lambda_q2[layer] * lambda_k2[layer])) + lam_init).astype(F32)
        lam = jnp.full((1, DA_WIDTH), lam, F32)

        qkv, pf = _inproj(x2, attn_norm[layer][None, :], _permute_w_in(w_in[layer]))

        da_gain = (jnp.tile(da_head_norm[layer], DA_HEADS) * (1.0 - lam_init))[None, :]
        out_a = _attention(qkv, bias_tiles, da_gain, lam, batch, seq)

        cw, cb = conv_w[layer], conv_b[layer]
        out_b = _ssd(pf, cw[:, :SSM_WIDTH], cw[:, SSM_WIDTH:], cb[None, :SSM_WIDTH], cb[None, SSM_WIDTH:],
                     _pad_lanes(dt_bias[layer]), _pad_lanes(a_log[layer]),
                     jnp.repeat(d_skip[layer], SSM_HEAD_DIM)[None, :], ssm_norm[layer][None, :], batch, seq)

        ret_gain = jnp.tile(ret_head_norm[layer], RET_HEADS)[None, :]
        out_c = _retention(pf, *ret_tabs, ret_gain, batch, seq)

        w_o = w_out[layer].astype(BF16)
        if layer % 2 == 0:
            i = layer // 2
            x2, h = _outproj(x2, out_a, out_b, out_c, w_o, ffn_norm[layer][None, :])
            pad = FFN_PAD - FFN_DENSE
            wg = jnp.pad(w_gate[i], ((0, 0), (0, pad))).astype(BF16)
            wu = jnp.pad(w_up[i], ((0, 0), (0, pad))).astype(BF16)
            wd = jnp.pad(w_down[i], ((0, pad), (0, 0))).astype(BF16)
            x2 = _dense_ffn(x2, h, wg, wu, wd)
        else:
            i = layer // 2
            rw = jnp.pad(router_w[i], ((0, 0), (0, LANES - N_EXPERTS))).astype(BF16)
            x2, h, rinfo = _outproj(x2, out_a, out_b, out_c, w_o, ffn_norm[layer][None, :], rw)
            tile_expert, n_valid, row_token, tok_rows = _routing_tables(rinfo)
            y_rows = _moe_ffn(h, tile_expert, n_valid, row_token,
                              e_gate[i].astype(BF16), e_up[i].astype(BF16), e_down[i].astype(BF16))
            x2 = _moe_combine(x2, rinfo, y_rows, tok_rows, final_norm[None, :])
    return x2.reshape(batch, seq, D_MODEL)
```

```python
import functools
import math

import jax
import jax.numpy as jnp
import numpy as np
from jax import lax
from jax.experimental import pallas as pl
from jax.experimental.pallas import tpu as pltpu

F32 = jnp.float32
BF16 = jnp.bfloat16

D_MODEL = 1024
DEPTH = 2
CHUNK = 64
EPS = 1e-6
DA_HEADS = 4
DA_HEAD_DIM = 32
DA_V_DIM = 64
DA_WIDTH = 256
SSM_HEADS = 8
SSM_HEAD_DIM = 64
SSM_WIDTH = 512
SSM_STATE = 64
SSM_GROUPS = 2
SSM_CONV = 4
RET_HEADS = 4
RET_KEY_DIM = 64
RET_WIDTH = 256
ROPE_BASE = 10000.0
REL_BUCKETS = 32
REL_MAX_DIST = 128
FFN_DENSE = 2752
N_EXPERTS = 8
FFN_EXPERT = 3584

LANES = 128
VMEM_LIMIT = 48 * 1024 * 1024
NEG_BIG = -1e30

TM_PROJ = 512
ATT_T = 512
ATT_RC = 32
LOG2E = math.log2(math.e)
BLK = 256
FFN_PAD = 2816
TM_FFN = 1024
TF_FFN = 256
TF_MOE = 512
TR_MOE = 512
TM_COMB = 256
TOP_K = 2

C_Z, C_XS, C_BC, C_RQ, C_RK, C_RV, C_RG, C_DT = 0, 512, 1024, 1280, 1536, 1792, 2048, 2304
PF_W = 2432
PF_CHUNK = 640
QKV_W = 768


def _cparams(sem):
    return pltpu.CompilerParams(dimension_semantics=sem, vmem_limit_bytes=VMEM_LIMIT)


def _group_mask(shape, axis, group, idx):
    io = lax.broadcasted_iota(jnp.int32, shape, axis)
    return (io >= idx * group) & (io < (idx + 1) * group)


def _lane_tile(x, n, axis=1):
    return jnp.concatenate([x] * n, axis=axis)


def _silu(x):
    return x * (1.0 / (1.0 + jnp.exp(-x)))


def _inproj_kernel(x_ref, g_ref, w_ref, qkv_ref, pf_ref):
    x = x_ref[...]
    ms = jnp.mean(x * x, axis=-1, keepdims=True)
    h = (x * lax.rsqrt(ms + EPS) * g_ref[...]).astype(BF16)
    qkv_ref[...] = jnp.dot(h, w_ref[:, :QKV_W], preferred_element_type=F32).astype(BF16)
    for lo in range(0, PF_W, PF_CHUNK):
        hi = min(lo + PF_CHUNK, PF_W)
        pf_ref[:, lo:hi] = jnp.dot(h, w_ref[:, QKV_W + lo:QKV_W + hi], preferred_element_type=F32)


def _inproj(x2, gain, w):
    t = x2.shape[0]
    return pl.pallas_call(
        _inproj_kernel,
        grid=(t // TM_PROJ,),
        in_specs=[
            pl.BlockSpec((TM_PROJ, D_MODEL), lambda i: (i, 0)),
            pl.BlockSpec((1, D_MODEL), lambda i: (0, 0)),
            pl.BlockSpec((D_MODEL, QKV_W + PF_W), lambda i: (0, 0)),
        ],
        out_specs=[
            pl.BlockSpec((TM_PROJ, QKV_W), lambda i: (i, 0)),
            pl.BlockSpec((TM_PROJ, PF_W), lambda i: (i, 0)),
        ],
        out_shape=[
            jax.ShapeDtypeStruct((t, QKV_W), BF16),
            jax.ShapeDtypeStruct((t, PF_W), F32),
        ],
        compiler_params=_cparams(("parallel",)),
        name="inproj",
    )(x2, gain, w)


def _attn_kernel(qi_ref, kj_ref, q_ref, k_ref, v_ref, bias_ref, gain_ref, lam_ref, o_ref,
                 qm_ref, m_ref, l_ref, acc_ref, s_ref, e_ref, al_ref):
    p = pl.program_id(1)
    qi = qi_ref[p]
    kj = kj_ref[p]
    n_ht = 2 * DA_HEADS

    @pl.when(kj == 0)
    def _init():
        q = q_ref[...]
        for ht in range(n_ht):
            qm_ref[ht] = jnp.where(_group_mask(q.shape, 1, DA_HEAD_DIM, ht), q, jnp.zeros_like(q))
        m_ref[...] = jnp.full(m_ref.shape, NEG_BIG, F32)
        l_ref[...] = jnp.zeros(l_ref.shape, F32)
        acc_ref[...] = jnp.zeros(acc_ref.shape, F32)

    def step(with_bias):
        k = k_ref[...]
        for ht in range(n_ht):
            h = ht // 2
            buf = ht % 2
            s_ref[buf] = lax.dot_general(qm_ref[ht], k, (((1,), (1,)), ((), ())), preferred_element_type=F32)
            for r0 in range(0, ATT_T, ATT_RC):
                rows = slice(r0, r0 + ATT_RC)
                s = s_ref[buf, rows, :]
                if with_bias:
                    s = s + bias_ref[0, h, rows, :]
                m_prev = m_ref[ht, rows, :]
                m_new = jnp.maximum(m_prev, jnp.max(s, axis=-1, keepdims=True))
                alpha = jnp.exp2(m_prev - m_new)
                e = jnp.exp2(s - _lane_tile(m_new, ATT_T // LANES))
                l_ref[ht, rows, :] = alpha * l_ref[ht, rows, :] + jnp.sum(e, axis=-1, keepdims=True)
                m_ref[ht, rows, :] = m_new
                al_ref[buf, rows, :] = alpha
                e_ref[buf, rows, :] = e.astype(BF16)
            c0 = (h // 2) * LANES
            pv = jnp.dot(e_ref[buf], v_ref[:, c0:c0 + LANES], preferred_element_type=F32)
            acc_ref[ht] = acc_ref[ht] * al_ref[buf] + pv

    @pl.when(kj >= qi - 1)
    def _near():
        step(True)

    @pl.when(kj < qi - 1)
    def _far():
        step(False)

    @pl.when(kj == qi)
    def _finish():
        lam = lam_ref[...]
        lane_lo = lax.broadcasted_iota(jnp.int32, (ATT_T, LANES), 1) < DA_V_DIM
        for pair in range(DA_HEADS // 2):
            halves = []
            for sub in range(2):
                h = 2 * pair + sub
                o = acc_ref[2 * h] * (1.0 / l_ref[2 * h]) - lam * (acc_ref[2 * h + 1] * (1.0 / l_ref[2 * h + 1]))
                own = lane_lo if sub == 0 else jnp.logical_not(lane_lo)
                ms = jnp.sum(jnp.where(own, o * o, 0.0), axis=-1, keepdims=True) * (1.0 / DA_V_DIM)
                halves.append(o * lax.rsqrt(ms + EPS))
            blk = jnp.where(lane_lo, halves[0], halves[1])
            c0 = pair * LANES
            o_ref[:, c0:c0 + LANES] = (blk * gain_ref[:, c0:c0 + LANES]).astype(BF16)


def _attention(qkv, bias_tiles, gain, lam, batch, seq):
    nq = seq // ATT_T
    qi_l, kj_l = [], []
    for qi in range(nq):
        for kj in range(qi + 1):
            qi_l.append(qi)
            kj_l.append(kj)
    qi_tbl = jnp.asarray(np.array(qi_l, np.int32))
    kj_tbl = jnp.asarray(np.array(kj_l, np.int32))
    n_pairs = len(qi_l)
    n_ht = 2 * DA_HEADS

    grid_spec = pltpu.PrefetchScalarGridSpec(
        num_scalar_prefetch=2,
        grid=(batch, n_pairs),
        in_specs=[
            pl.BlockSpec((ATT_T, DA_WIDTH), lambda b, p, qi, kj: (b * nq + qi[p], 0)),
            pl.BlockSpec((ATT_T, DA_WIDTH), lambda b, p, qi, kj: (b * nq + kj[p], 1)),
            pl.BlockSpec((ATT_T, DA_WIDTH), lambda b, p, qi, kj: (b * nq + kj[p], 2)),
            pl.BlockSpec((1, DA_HEADS, ATT_T, ATT_T),
                         lambda b, p, qi, kj: (jnp.minimum(qi[p] - kj[p], 1), 0, 0, 0)),
            pl.BlockSpec((1, DA_WIDTH), lambda b, p, qi, kj: (0, 0)),
            pl.BlockSpec((1, LANES), lambda b, p, qi, kj: (0, 0)),
        ],
        out_specs=pl.BlockSpec((ATT_T, DA_WIDTH), lambda b, p, qi, kj: (b * nq + qi[p], 0)),
        scratch_shapes=[
            pltpu.VMEM((n_ht, ATT_T, DA_WIDTH), BF16),
            pltpu.VMEM((n_ht, ATT_T, LANES), F32),
            pltpu.VMEM((n_ht, ATT_T, LANES), F32),
            pltpu.VMEM((n_ht, ATT_T, LANES), F32),
            pltpu.VMEM((2, ATT_T, ATT_T), F32),
            pltpu.VMEM((2, ATT_T, ATT_T), BF16),
            pltpu.VMEM((2, ATT_T, LANES), F32),
        ],
    )
    return pl.pallas_call(
        _attn_kernel,
        grid_spec=grid_spec,
        out_shape=jax.ShapeDtypeStruct((batch * seq, DA_WIDTH), BF16),
        compiler_params=_cparams(("parallel", "arbitrary")),
        name="diff_attention",
    )(qi_tbl, kj_tbl, qkv, qkv, qkv, bias_tiles, gain, lam)


def _t5_bucket(rel):
    half = REL_BUCKETS // 2
    max_exact = half // 2
    ret = jnp.where(rel > 0, half, 0)
    n = jnp.abs(rel)
    nf = jnp.maximum(n, 1).astype(F32)
    large = max_exact + (jnp.log(nf / max_exact) / math.log(REL_MAX_DIST / max_exact)
                         * (half - max_exact)).astype(jnp.int32)
    large = jnp.minimum(large, half - 1)
    return ret + jnp.where(n < max_exact, n, large)


def _bias_tiles(rel_bias):
    i = jnp.arange(ATT_T)[:, None]
    j = jnp.arange(ATT_T)[None, :]
    far = rel_bias[_t5_bucket(jnp.int32(-2 * ATT_T))].astype(F32)
    tiles = []
    for off in range(2):
        bucket = _t5_bucket((j - off * ATT_T) - i)
        b = jnp.zeros((DA_HEADS, ATT_T, ATT_T), F32)
        for c in range(REL_BUCKETS):
            b = jnp.where((bucket == c)[None], rel_bias[c].astype(F32)[:, None, None], b)
        b = (b - far[:, None, None]) * LOG2E
        if off == 0:
            allowed = (j // CHUNK) <= (i // CHUNK)
            b = jnp.where(allowed[None], b, NEG_BIG)
        tiles.append(b)
    return jnp.stack(tiles)


def _cumsum_rows(x):
    n = x.shape[0]
    row = lax.broadcasted_iota(jnp.int32, x.shape, 0)
    sh = 1
    while sh < n:
        x = x + jnp.where(row >= sh, pltpu.roll(x, sh, axis=0), 0.0)
        sh *= 2
    return x


def _expand_heads(v, width, group):
    out = jnp.zeros((v.shape[0], width), F32)
    for h in range(width // group):
        col = jnp.broadcast_to(v[:, h:h + 1], (v.shape[0], width))
        out = jnp.where(_group_mask(out.shape, 1, group, h), col, out)
    return out


def _ssd_kernel(z_ref, xs_ref, bc_ref, dt_ref, cwx_ref, cwb_ref, cbx_ref, cbb_ref, dtb_ref, alog_ref,
                dskip_ref, ng_ref, o_ref, extx_ref, extb_ref, st_ref):
    j = pl.program_id(1)

    @pl.when(j == 0)
    def _reset():
        extx_ref[0:8, :] = jnp.zeros((8, SSM_WIDTH), F32)
        extb_ref[0:8, :] = jnp.zeros((8, 2 * LANES), F32)
        st_ref[...] = jnp.zeros(st_ref.shape, F32)

    extx_ref[8:8 + BLK, :] = xs_ref[...]
    extb_ref[8:8 + BLK, :] = bc_ref[...]
    yx = jnp.broadcast_to(cbx_ref[...], (BLK, SSM_WIDTH))
    yb = jnp.broadcast_to(cbb_ref[...], (BLK, 2 * LANES))
    for w in range(SSM_CONV):
        lo = 8 - (SSM_CONV - 1) + w
        yx = yx + extx_ref[lo:lo + BLK, :] * cwx_ref[w:w + 1, :]
        yb = yb + extb_ref[lo:lo + BLK, :] * cwb_ref[w:w + 1, :]
    tail_x = extx_ref[BLK:BLK + 8, :]
    tail_b = extb_ref[BLK:BLK + 8, :]
    extx_ref[0:8, :] = tail_x
    extb_ref[0:8, :] = tail_b
    xs = _silu(yx)
    bc = _silu(yb)
    bm = bc[:, :LANES]
    cm = bc[:, LANES:]

    dt = jax.nn.softplus(dt_ref[...] + dtb_ref[...])
    a = -jnp.exp(alog_ref[...]) * dt
    a_cs = _cumsum_rows(a)
    a_cs_t = a_cs.T
    a_last = a_cs[BLK - 1:BLK, :]

    dt_full = _expand_heads(dt, SSM_WIDTH, SSM_HEAD_DIM)
    ea_full = _expand_heads(jnp.exp(a_cs), SSM_WIDTH, SSM_HEAD_DIM)
    dec_full = _expand_heads(jnp.exp(a_last - a_cs), SSM_WIDTH, SSM_HEAD_DIM)
    cdec_full = _expand_heads(jnp.exp(a_last), SSM_WIDTH, SSM_HEAD_DIM)

    xdt = xs * dt_full
    xdt_b = xdt.astype(BF16)
    cm_b = cm.astype(BF16)
    bm_b = bm.astype(BF16)

    st = st_ref[...]
    y = jnp.dot(cm_b, st.astype(BF16), preferred_element_type=F32) * ea_full + xs * dskip_ref[...]

    row = lax.broadcasted_iota(jnp.int32, (BLK, BLK), 0)
    colio = lax.broadcasted_iota(jnp.int32, (BLK, BLK), 1)
    causal = colio <= row
    rep = SSM_HEADS // SSM_GROUPS
    lane_lo = lax.broadcasted_iota(jnp.int32, (BLK, LANES), 1) < SSM_HEAD_DIM
    pieces = []
    for g in range(SSM_GROUPS):
        cg = jnp.where(_group_mask(cm_b.shape, 1, SSM_STATE, g), cm_b, jnp.zeros_like(cm_b))
        cb = lax.dot_general(cg, bm_b, (((1,), (1,)), ((), ())), preferred_element_type=F32)
        for pair in range(rep // 2):
            halves = []
            for sub in range(2):
                h = g * rep + pair * 2 + sub
                seg = jnp.broadcast_to(a_cs[:, h:h + 1], (BLK, BLK)) - a_cs_t[h:h + 1, :]
                lmat = jnp.exp(jnp.where(causal, seg, NEG_BIG))
                mh = (cb * lmat).astype(BF16)
                c0 = (h // 2) * LANES
                halves.append(jnp.dot(mh, xdt_b[:, c0:c0 + LANES], preferred_element_type=F32))
            pieces.append(jnp.where(lane_lo, halves[0], halves[1]))
    y = y + jnp.concatenate(pieces, axis=1)

    upd = jnp.dot(bm.T.astype(BF16), (xdt * dec_full).astype(BF16), preferred_element_type=F32)
    r_io = lax.broadcasted_iota(jnp.int32, upd.shape, 0) // SSM_STATE
    c_io = lax.broadcasted_iota(jnp.int32, upd.shape, 1) // (SSM_HEAD_DIM * rep)
    st_ref[...] = st * cdec_full + jnp.where(r_io == c_io, upd, 0.0)

    y = y * _silu(z_ref[...])
    gw = SSM_WIDTH // SSM_GROUPS
    for g in range(SSM_GROUPS):
        yg = y[:, g * gw:(g + 1) * gw]
        ms = jnp.mean(yg * yg, axis=-1, keepdims=True)
        o_ref[:, g * gw:(g + 1) * gw] = (yg * lax.rsqrt(ms + EPS) * ng_ref[:, g * gw:(g + 1) * gw]).astype(BF16)


def _ssd(pf, cwx, cwb, cbx, cbb, dtb, alog, dskip, ng, batch, seq):
    nb = seq // BLK
    row = lambda w, c: pl.BlockSpec((BLK, w), lambda b, j: (b * nb + j, c))
    const = lambda r, w: pl.BlockSpec((r, w), lambda b, j: (0, 0))
    return pl.pallas_call(
        _ssd_kernel,
        grid=(batch, nb),
        in_specs=[
            row(SSM_WIDTH, C_Z // SSM_WIDTH), row(SSM_WIDTH, C_XS // SSM_WIDTH), row(2 * LANES, C_BC // (2 * LANES)),
            row(LANES, C_DT // LANES),
            const(SSM_CONV, SSM_WIDTH), const(SSM_CONV, 2 * LANES), const(1, SSM_WIDTH), const(1, 2 * LANES),
            const(1, LANES), const(1, LANES), const(1, SSM_WIDTH), const(1, SSM_WIDTH),
        ],
        out_specs=pl.BlockSpec((BLK, SSM_WIDTH), lambda b, j: (b * nb + j, 0)),
        out_shape=jax.ShapeDtypeStruct((batch * seq, SSM_WIDTH), BF16),
        scratch_shapes=[
            pltpu.VMEM((BLK + 8, SSM_WIDTH), F32),
            pltpu.VMEM((BLK + 8, 2 * LANES), F32),
            pltpu.VMEM((SSM_GROUPS * SSM_STATE, SSM_WIDTH), F32),
        ],
        compiler_params=_cparams(("parallel", "arbitrary")),
        name="ssd_mixer",
    )(pf, pf, pf, pf, cwx, cwb, cbx, cbb, dtb, alog, dskip, ng)


def _rotary(u, cos, sin_signed):
    first = (lax.broadcasted_iota(jnp.int32, u.shape, 1) % RET_KEY_DIM) < (RET_KEY_DIM // 2)
    half = RET_KEY_DIM // 2
    swapped = jnp.where(first, pltpu.roll(u, u.shape[1] - half, axis=1), pltpu.roll(u, half, axis=1))
    return u * cos + swapped * sin_signed


def _ret_kernel(q_ref, k_ref, v_ref, g_ref, cos_ref, sin_ref, dmat_ref, qdec_ref, kdec_ref, gdec_ref,
                bd_ref, gain_ref, o_ref, st_ref):
    j = pl.program_id(1)

    @pl.when(j == 0)
    def _reset():
        st_ref[...] = jnp.zeros(st_ref.shape, F32)

    cos = cos_ref[...]
    sin = sin_ref[...]
    q = _rotary(q_ref[...], cos, sin)
    k = _rotary(k_ref[...], cos, sin) * (RET_KEY_DIM ** -0.5)
    v_b = v_ref[...].astype(BF16)
    q_b = q.astype(BF16)
    k_b = k.astype(BF16)

    st = st_ref[...]
    o = jnp.dot((q * qdec_ref[...]).astype(BF16), st.astype(BF16), preferred_element_type=F32)
    for h in range(RET_HEADS):
        cm = _group_mask(q_b.shape, 1, RET_KEY_DIM, h)
        qh = jnp.where(cm, q_b, jnp.zeros_like(q_b))
        s = lax.dot_general(qh, k_b, (((1,), (1,)), ((), ())), preferred_element_type=F32) * dmat_ref[h]
        oh = jnp.dot(s.astype(BF16), v_b, preferred_element_type=F32)
        o = o + jnp.where(cm, oh, 0.0)

    upd = jnp.dot((k * kdec_ref[...]).T.astype(BF16), v_b, preferred_element_type=F32)
    st_ref[...] = st * gdec_ref[...] + upd * bd_ref[...]

    out = jnp.zeros(o.shape, F32)
    for h in range(RET_HEADS):
        cm = _group_mask(o.shape, 1, RET_KEY_DIM, h)
        ms = jnp.sum(jnp.where(cm, o * o, 0.0), axis=-1, keepdims=True) * (1.0 / RET_KEY_DIM)
        out = out + jnp.where(cm, o * lax.rsqrt(ms + EPS), 0.0)
    o_ref[...] = (_silu(g_ref[...]) * (out * gain_ref[...])).astype(BF16)


def _retention(pf, cos, sin, dmat, qdec, kdec, gdec, bd, gain, batch, seq):
    nb = seq // BLK
    row = lambda c: pl.BlockSpec((BLK, RET_WIDTH), lambda b, j: (b * nb + j, c))
    const2 = pl.BlockSpec((BLK, RET_WIDTH), lambda b, j: (0, 0))
    return pl.pallas_call(
        _ret_kernel,
        grid=(batch, nb),
        in_specs=[
            row(C_RQ // RET_WIDTH), row(C_RK // RET_WIDTH), row(C_RV // RET_WIDTH), row(C_RG // RET_WIDTH),
            pl.BlockSpec((BLK, RET_WIDTH), lambda b, j: (j, 0)),
            pl.BlockSpec((BLK, RET_WIDTH), lambda b, j: (j, 0)),
            pl.BlockSpec((RET_HEADS, BLK, BLK), lambda b, j: (0, 0, 0)),
            const2, const2, const2, const2,
            pl.BlockSpec((1, RET_WIDTH), lambda b, j: (0, 0)),
        ],
        out_specs=pl.BlockSpec((BLK, RET_WIDTH), lambda b, j: (b * nb + j, 0)),
        out_shape=jax.ShapeDtypeStruct((batch * seq, RET_WIDTH), BF16),
        scratch_shapes=[pltpu.VMEM((RET_WIDTH, RET_WIDTH), F32)],
        compiler_params=_cparams(("parallel", "arbitrary")),
        name="retention",
    )(pf, pf, pf, pf, cos, sin, dmat, qdec, kdec, gdec, bd, gain)


def _retention_tables(seq):
    half = RET_KEY_DIM // 2
    inv = 1.0 / (ROPE_BASE ** (jnp.arange(0, RET_KEY_DIM, 2, dtype=F32) / RET_KEY_DIM))
    ang = jnp.arange(seq, dtype=F32)[:, None] * inv[None, :]
    cos_h = jnp.concatenate([jnp.cos(ang), jnp.cos(ang)], axis=-1)
    sin_h = jnp.concatenate([-jnp.sin(ang), jnp.sin(ang)], axis=-1)
    cos = jnp.tile(cos_h, (1, RET_HEADS))
    sin = jnp.tile(sin_h, (1, RET_HEADS))
    log_gamma = jnp.log1p(-jnp.power(2.0, -5.0 - jnp.arange(RET_HEADS, dtype=F32)))
    idx = jnp.arange(BLK, dtype=F32)
    ii = jnp.arange(BLK)
    same_or_earlier_chunk = (ii[None, :] // CHUNK) <= (ii[:, None] // CHUNK)
    dmat = jnp.exp(log_gamma[:, None, None] * jnp.abs(idx[:, None] - idx[None, :]))
    dmat = jnp.where(same_or_earlier_chunk[None], dmat, 0.0)
    lg_cols = jnp.repeat(log_gamma, RET_KEY_DIM)
    qdec = jnp.exp(lg_cols[None, :] * (idx + 1.0)[:, None])
    kdec = jnp.exp(lg_cols[None, :] * (BLK - 1.0 - idx)[:, None])
    head_of = jnp.arange(RET_WIDTH) // RET_KEY_DIM
    bd = (head_of[:, None] == head_of[None, :]).astype(F32)
    gdec = jnp.exp(lg_cols * BLK)[:, None] * bd
    return cos, sin, dmat, qdec, kdec, gdec, bd


def _outproj_kernel(x_ref, a_ref, b_ref, c_ref, w_ref, g_ref, *rest, with_router):
    if with_router:
        rw_ref, xo_ref, h_ref, rinfo_ref = rest
    else:
        xo_ref, h_ref = rest
    acc = x_ref[...]
    acc = acc + jnp.dot(a_ref[...], w_ref[0:DA_WIDTH, :], preferred_element_type=F32)
    acc = acc + jnp.dot(b_ref[...], w_ref[DA_WIDTH:DA_WIDTH + SSM_WIDTH, :], preferred_element_type=F32)
    acc = acc + jnp.dot(c_ref[...], w_ref[DA_WIDTH + SSM_WIDTH:, :], preferred_element_type=F32)
    xo_ref[...] = acc
    ms = jnp.mean(acc * acc, axis=-1, keepdims=True)
    hf = acc * lax.rsqrt(ms + EPS) * g_ref[...]
    h = hf.astype(BF16)
    h_ref[...] = hf.astype(h_ref.dtype)
    if with_router:
        logits = jnp.dot(h, rw_ref[...], preferred_element_type=F32)
        lane = lax.broadcasted_iota(jnp.int32, logits.shape, 1).astype(F32)
        logits = jnp.where(lane < N_EXPERTS, logits, NEG_BIG)
        v1 = jnp.max(logits, axis=-1, keepdims=True)
        i1 = jnp.min(jnp.where(logits == v1, lane, float(LANES)), axis=-1, keepdims=True)
        rest_l = jnp.where(lane == i1, NEG_BIG, logits)
        v2 = jnp.max(rest_l, axis=-1, keepdims=True)
        i2 = jnp.min(jnp.where(rest_l == v2, lane, float(LANES)), axis=-1, keepdims=True)
        e2 = jnp.exp(v2 - v1)
        g1 = 1.0 / (1.0 + e2)
        g2 = e2 * g1
        rinfo_ref[...] = (jnp.where(lane == 0.0, g1, 0.0) + jnp.where(lane == 1.0, g2, 0.0)
                          + jnp.where(lane == 2.0, i1, 0.0) + jnp.where(lane == 3.0, i2, 0.0))


def _outproj(x2, oa, ob, oc, w, gain, router_w=None):
    t = x2.shape[0]
    with_router = router_w is not None
    tok = lambda w_: pl.BlockSpec((TM_PROJ, w_), lambda i: (i, 0))
    in_specs = [tok(D_MODEL), tok(DA_WIDTH), tok(SSM_WIDTH), tok(RET_WIDTH),
                pl.BlockSpec((D_MODEL, D_MODEL), lambda i: (0, 0)),
                pl.BlockSpec((1, D_MODEL), lambda i: (0, 0))]
    out_specs = [tok(D_MODEL), tok(D_MODEL)]
    h_dtype = F32 if with_router else BF16
    out_shape = [jax.ShapeDtypeStruct((t, D_MODEL), F32), jax.ShapeDtypeStruct((t, D_MODEL), h_dtype)]
    args = [x2, oa, ob, oc, w, gain]
    if with_router:
        in_specs.append(pl.BlockSpec((D_MODEL, LANES), lambda i: (0, 0)))
        out_specs.append(tok(LANES))
        out_shape.append(jax.ShapeDtypeStruct((t, LANES), F32))
        args.append(router_w)
    return pl.pallas_call(
        functools.partial(_outproj_kernel, with_router=with_router),
        grid=(t // TM_PROJ,),
        in_specs=in_specs,
        out_specs=out_specs,
        out_shape=out_shape,
        compiler_params=_cparams(("parallel",)),
        name="outproj_router" if with_router else "outproj",
    )(*args)


def _ffn_kernel(x_ref, h_ref, wg_ref, wu_ref, wd_ref, o_ref, acc_ref):
    f = pl.program_id(1)

    @pl.when(f == 0)
    def _init():
        acc_ref[...] = x_ref[...]

    h = h_ref[...]
    gate = jnp.dot(h, wg_ref[...], preferred_element_type=F32)
    up = jnp.dot(h, wu_ref[...], preferred_element_type=F32)
    act = (_silu(gate) * up).astype(BF16)
    acc_ref[...] += jnp.dot(act, wd_ref[...], preferred_element_type=F32)

    @pl.when(f == pl.num_programs(1) - 1)
    def _done():
        o_ref[...] = acc_ref[...]


def _dense_ffn(x2, h, wg, wu, wd):
    t = x2.shape[0]
    nf = wg.shape[1] // TF_FFN
    return pl.pallas_call(
        _ffn_kernel,
        grid=(t // TM_FFN, nf),
        in_specs=[
            pl.BlockSpec((TM_FFN, D_MODEL), lambda i, f: (i, 0)),
            pl.BlockSpec((TM_FFN, D_MODEL), lambda i, f: (i, 0)),
            pl.BlockSpec((D_MODEL, TF_FFN), lambda i, f: (0, f)),
            pl.BlockSpec((D_MODEL, TF_FFN), lambda i, f: (0, f)),
            pl.BlockSpec((TF_FFN, D_MODEL), lambda i, f: (f, 0)),
        ],
        out_specs=pl.BlockSpec((TM_FFN, D_MODEL), lambda i, f: (i, 0)),
        out_shape=jax.ShapeDtypeStruct((t, D_MODEL), F32),
        scratch_shapes=[pltpu.VMEM((TM_FFN, D_MODEL), F32)],
        compiler_params=_cparams(("parallel", "arbitrary")),
        name="dense_swiglu",
    )(x2, h, wg, wu, wd)


def _issue_row_gather(src_hbm, idx_ref, idx_base, dst, sem, n_rows):
    def body(r, carry):
        tok = idx_ref[idx_base + r]
        pltpu.make_async_copy(src_hbm.at[pl.ds(tok, 1)], dst.at[pl.ds(r, 1)], sem).start()
        return carry
    lax.fori_loop(0, n_rows, body, 0, unroll=8)


def _wait_row_gather(src_hbm, dst, sem, n_rows):
    pltpu.make_async_copy(src_hbm.at[pl.ds(0, n_rows)], dst, sem).wait()


def _moe_ffn_kernel(te_ref, nv_ref, rt_ref, h_hbm, wg_ref, wu_ref, wd_ref, y_ref, hbuf, xb_ref, acc_ref, sem):
    i = pl.program_id(0)
    f = pl.program_id(1)
    n_valid = nv_ref[0]
    valid = i < n_valid
    slot = i % 2

    @pl.when((f == 0) & (i == 0) & valid)
    def _first():
        _issue_row_gather(h_hbm, rt_ref, 0, hbuf.at[0], sem.at[0], TR_MOE)

    @pl.when((f == 0) & valid)
    def _arrive():
        _wait_row_gather(h_hbm, hbuf.at[slot], sem.at[slot], TR_MOE)
        xb_ref[...] = hbuf[slot].astype(BF16)

    @pl.when((f == 0) & (i + 1 < n_valid))
    def _prefetch():
        _issue_row_gather(h_hbm, rt_ref, (i + 1) * TR_MOE, hbuf.at[1 - slot], sem.at[1 - slot], TR_MOE)

    @pl.when(valid)
    def _compute():
        xb = xb_ref[...]
        gate = jnp.dot(xb, wg_ref[0], preferred_element_type=F32)
        up = jnp.dot(xb, wu_ref[0], preferred_element_type=F32)
        act = (_silu(gate) * up).astype(BF16)
        contrib = jnp.dot(act, wd_ref[0], preferred_element_type=F32)

        @pl.when(f == 0)
        def _():
            acc_ref[...] = contrib

        @pl.when(f > 0)
        def _():
            acc_ref[...] += contrib

    @pl.when(f == pl.num_programs(1) - 1)
    def _store():
        y_ref[...] = jnp.where(valid, acc_ref[...], 0.0)


def _moe_ffn(h, tile_expert, n_valid, row_token, wg, wu, wd):
    n_tiles = tile_expert.shape[0]
    nf = wg.shape[2] // TF_MOE

    def w_in_map(i, f, te, nv, rt):
        return (te[i], 0, jnp.where(i < nv[0], f, nf - 1))

    def w_out_map(i, f, te, nv, rt):
        return (te[i], jnp.where(i < nv[0], f, nf - 1), 0)

    grid_spec = pltpu.PrefetchScalarGridSpec(
        num_scalar_prefetch=3,
        grid=(n_tiles, nf),
        in_specs=[
            pl.BlockSpec(memory_space=pl.ANY),
            pl.BlockSpec((1, D_MODEL, TF_MOE), w_in_map),
            pl.BlockSpec((1, D_MODEL, TF_MOE), w_in_map),
            pl.BlockSpec((1, TF_MOE, D_MODEL), w_out_map),
        ],
        out_specs=pl.BlockSpec((TR_MOE, D_MODEL), lambda i, f, te, nv, rt: (i, 0)),
        scratch_shapes=[
            pltpu.VMEM((2, TR_MOE, D_MODEL), F32),
            pltpu.VMEM((TR_MOE, D_MODEL), BF16),
            pltpu.VMEM((TR_MOE, D_MODEL), F32),
            pltpu.SemaphoreType.DMA((2,)),
        ],
    )
    return pl.pallas_call(
        _moe_ffn_kernel,
        grid_spec=grid_spec,
        out_shape=jax.ShapeDtypeStruct((n_tiles * TR_MOE, D_MODEL), F32),
        compiler_params=_cparams(("arbitrary", "arbitrary")),
        name="moe_expert_ffn",
    )(tile_expert, n_valid, row_token, h, wg, wu, wd)


def _moe_combine_kernel(tr_ref, x_ref, rinfo_ref, y_hbm, fg_ref, o_ref, ybuf, sem):
    i = pl.program_id(0)
    slot = i % 2
    n_rows = TOP_K * TM_COMB

    @pl.when(i == 0)
    def _first():
        _issue_row_gather(y_hbm, tr_ref, 0, ybuf.at[0], sem.at[0], n_rows)

    _wait_row_gather(y_hbm, ybuf.at[slot], sem.at[slot], n_rows)

    @pl.when(i + 1 < pl.num_programs(0))
    def _prefetch():
        _issue_row_gather(y_hbm, tr_ref, (i + 1) * n_rows, ybuf.at[1 - slot], sem.at[1 - slot], n_rows)

    rinfo = rinfo_ref[...]
    g1 = jnp.broadcast_to(rinfo[:, 0:1], (TM_COMB, D_MODEL))
    g2 = jnp.broadcast_to(rinfo[:, 1:2], (TM_COMB, D_MODEL))
    y = x_ref[...] + g1 * ybuf[slot, 0:TM_COMB, :] + g2 * ybuf[slot, TM_COMB:n_rows, :]
    ms = jnp.mean(y * y, axis=-1, keepdims=True)
    o_ref[...] = y * lax.rsqrt(ms + EPS) * fg_ref[...]


def _moe_combine(x2, rinfo, y_rows, tok_rows, final_gain):
    t = x2.shape[0]
    grid_spec = pltpu.PrefetchScalarGridSpec(
        num_scalar_prefetch=1,
        grid=(t // TM_COMB,),
        in_specs=[
            pl.BlockSpec((TM_COMB, D_MODEL), lambda i, tr: (i, 0)),
            pl.BlockSpec((TM_COMB, LANES), lambda i, tr: (i, 0)),
            pl.BlockSpec(memory_space=pl.ANY),
            pl.BlockSpec((1, D_MODEL), lambda i, tr: (0, 0)),
        ],
        out_specs=pl.BlockSpec((TM_COMB, D_MODEL), lambda i, tr: (i, 0)),
        scratch_shapes=[
            pltpu.VMEM((2, TOP_K * TM_COMB, D_MODEL), F32),
            pltpu.SemaphoreType.DMA((2,)),
        ],
    )
    return pl.pallas_call(
        _moe_combine_kernel,
        grid_spec=grid_spec,
        out_shape=jax.ShapeDtypeStruct((t, D_MODEL), F32),
        compiler_params=_cparams(("arbitrary",)),
        name="moe_combine_norm",
    )(tok_rows, x2, rinfo, y_rows, final_gain)


def _routing_tables(rinfo):
    t = rinfo.shape[0]
    n_tiles = (TOP_K * t) // TR_MOE + N_EXPERTS
    flat_e = rinfo[:, 2:2 + TOP_K].astype(jnp.int32).reshape(-1)
    onehot = (flat_e[:, None] == jnp.arange(N_EXPERTS)[None, :]).astype(jnp.int32)
    csum = jnp.cumsum(onehot, axis=0)
    rank = jnp.sum((csum - onehot) * onehot, axis=1)
    counts = csum[-1]
    tiles_per_e = (counts + TR_MOE - 1) // TR_MOE
    tile_end = jnp.cumsum(tiles_per_e)
    row_off = (tile_end - tiles_per_e) * TR_MOE
    dest = row_off[flat_e] + rank
    token = jnp.arange(TOP_K * t, dtype=jnp.int32) // TOP_K
    row_token = jnp.zeros((n_tiles * TR_MOE,), jnp.int32).at[dest].set(token, unique_indices=True)
    n_valid = tile_end[-1:].astype(jnp.int32)
    tile_expert = jnp.searchsorted(tile_end, jnp.arange(n_tiles), side="right").astype(jnp.int32)
    last_e = jnp.searchsorted(tile_end, n_valid[0] - 1, side="right").astype(jnp.int32)
    tile_expert = jnp.where(jnp.arange(n_tiles) < n_valid[0], tile_expert, last_e)
    tok_rows = dest.reshape(t // TM_COMB, TM_COMB, TOP_K).transpose(0, 2, 1).reshape(-1).astype(jnp.int32)
    return tile_expert, n_valid, row_token, tok_rows


def _permute_w_in(w):
    sizes = (256, 256, 256, 512, 768, 8, 256, 256, 256, 256)
    offs = np.concatenate([[0], np.cumsum(sizes)])
    part = lambda n: w[:, offs[n]:offs[n + 1]]
    dt_pad = jnp.zeros((w.shape[0], LANES - SSM_HEADS), w.dtype)
    cols = [part(0) * (DA_HEAD_DIM ** -0.5 * LOG2E), part(1), part(2), part(3), part(4),
            part(6), part(7), part(8), part(9), part(5), dt_pad]
    return jnp.concatenate(cols, axis=1).astype(BF16)


def _pad_lanes(v, fill=0.0):
    return jnp.concatenate([v, jnp.full((LANES - v.shape[0],), fill, v.dtype)])[None, :]


def kernel(x, w_in, w_out, attn_norm, ffn_norm, final_norm, rel_bias, lambda_q1, lambda_k1, lambda_q2,
           lambda_k2, da_head_norm, conv_w, conv_b, dt_bias, a_log, d_skip, ssm_norm, ret_head_norm,
           w_gate, w_up, w_down, router_w, e_gate, e_up, e_down):
    batch, seq, _ = x.shape
    x2 = x.reshape(batch * seq, D_MODEL)
    bias_tiles = _bias_tiles(rel_bias)
    ret_tabs = _retention_tables(seq)

    for layer in range(DEPTH):
        lam_init = 0.8 - 0.6 * math.exp(-0.3 * layer)
        lam = (jnp.exp(jnp.sum(lambda_q1[layer] * lambda_k1[layer]))
               - jnp.exp(jnp.sum(lambda_q2[layer] * lambda_k2[layer])) + lam_init).astype(F32)
        lam = jnp.full((1, LANES), lam, F32)

        qkv, pf = _inproj(x2, attn_norm[layer][None, :], _permute_w_in(w_in[layer]))

        da_gain = (jnp.tile(da_head_norm[layer], DA_HEADS) * (1.0 - lam_init))[None, :]
        out_a = _attention(qkv, bias_tiles, da_gain, lam, batch, seq)

        cw, cb = conv_w[layer], conv_b[layer]
        out_b = _ssd(pf, cw[:, :SSM_WIDTH], cw[:, SSM_WIDTH:], cb[None, :SSM_WIDTH], cb[None, SSM_WIDTH:],
                     _pad_lanes(dt_bias[layer]), _pad_lanes(a_log[layer]),
                     jnp.repeat(d_skip[layer], SSM_HEAD_DIM)[None, :], ssm_norm[layer][None, :], batch, seq)

        ret_gain = jnp.tile(ret_head_norm[layer], RET_HEADS)[None, :]
        out_c = _retention(pf, *ret_tabs, ret_gain, batch, seq)

        w_o = w_out[layer].astype(BF16)
        if layer % 2 == 0:
            i = layer // 2
            x2, h = _outproj(x2, out_a, out_b, out_c, w_o, ffn_norm[layer][None, :])
            pad = FFN_PAD - FFN_DENSE
            wg = jnp.pad(w_gate[i], ((0, 0), (0, pad))).astype(BF16)
            wu = jnp.pad(w_up[i], ((0, 0), (0, pad))).astype(BF16)
            wd = jnp.pad(w_down[i], ((0, pad), (0, 0))).astype(BF16)
            x2 = _dense_ffn(x2, h, wg, wu, wd)
        else:
            i = layer // 2
            rw = jnp.pad(router_w[i], ((0, 0), (0, LANES - N_EXPERTS))).astype(BF16)
            x2, h, rinfo = _outproj(x2, out_a, out_b, out_c, w_o, ffn_norm[layer][None, :], rw)
            tile_expert, n_valid, row_token, tok_rows = _routing_tables(rinfo)
            y_rows = _moe_ffn(h, tile_expert, n_valid, row_token,
                              e_gate[i].astype(BF16), e_up[i].astype(BF16), e_down[i].astype(BF16))
            x2 = _moe_combine(x2, rinfo, y_rows, tok_rows, final_norm[None, :])
    return x2.reshape(batch, seq, D_MODEL)
```

```python
import functools
import math

import jax
import jax.numpy as jnp
import numpy as np
from jax import lax
from jax.experimental import pallas as pl
from jax.experimental.pallas import tpu as pltpu

F32 = jnp.float32
BF16 = jnp.bfloat16

D_MODEL = 1024
DEPTH = 2
CHUNK = 64
EPS = 1e-6
DA_HEADS = 4
DA_HEAD_DIM = 32
DA_V_DIM = 64
DA_WIDTH = 256
SSM_HEADS = 8
SSM_HEAD_DIM = 64
SSM_WIDTH = 512
SSM_STATE = 64
SSM_GROUPS = 2
SSM_CONV = 4
RET_HEADS = 4
RET_KEY_DIM = 64
RET_WIDTH = 256
ROPE_BASE = 10000.0
REL_BUCKETS = 32
REL_MAX_DIST = 128
FFN_DENSE = 2752
N_EXPERTS = 8
FFN_EXPERT = 3584

LANES = 128
VMEM_LIMIT = 48 * 1024 * 1024
NEG_BIG = -1e30

TM_PROJ = 512
ATT_T = 512
LOG2E = math.log2(math.e)
BLK = 256
FFN_PAD = 2816
TM_FFN = 1024
TF_FFN = 256
TF_MOE = 512
TR_MOE = 512
TM_COMB = 256
TOP_K = 2

C_Z, C_XS, C_BC, C_RQ, C_RK, C_RV, C_RG, C_DT = 0, 512, 1024, 1280, 1536, 1792, 2048, 2304
PF_W = 2432
PF_CHUNK = 640
QKV_W = 768


def _cparams(sem):
    return pltpu.CompilerParams(dimension_semantics=sem, vmem_limit_bytes=VMEM_LIMIT)


def _group_mask(shape, axis, group, idx):
    io = lax.broadcasted_iota(jnp.int32, shape, axis)
    return (io >= idx * group) & (io < (idx + 1) * group)


def _lane_tile(x, n, axis=1):
    return jnp.concatenate([x] * n, axis=axis)


def _silu(x):
    return x * (1.0 / (1.0 + jnp.exp(-x)))


def _inproj_kernel(x_ref, g_ref, w_ref, qkv_ref, pf_ref):
    x = x_ref[...]
    ms = jnp.mean(x * x, axis=-1, keepdims=True)
    h = (x * lax.rsqrt(ms + EPS) * g_ref[...]).astype(BF16)
    qkv_ref[...] = jnp.dot(h, w_ref[:, :QKV_W], preferred_element_type=F32).astype(BF16)
    for lo in range(0, PF_W, PF_CHUNK):
        hi = min(lo + PF_CHUNK, PF_W)
        pf_ref[:, lo:hi] = jnp.dot(h, w_ref[:, QKV_W + lo:QKV_W + hi], preferred_element_type=F32)


def _inproj(x2, gain, w):
    t = x2.shape[0]
    return pl.pallas_call(
        _inproj_kernel,
        grid=(t // TM_PROJ,),
        in_specs=[
            pl.BlockSpec((TM_PROJ, D_MODEL), lambda i: (i, 0)),
            pl.BlockSpec((1, D_MODEL), lambda i: (0, 0)),
            pl.BlockSpec((D_MODEL, QKV_W + PF_W), lambda i: (0, 0)),
        ],
        out_specs=[
            pl.BlockSpec((TM_PROJ, QKV_W), lambda i: (i, 0)),
            pl.BlockSpec((TM_PROJ, PF_W), lambda i: (i, 0)),
        ],
        out_shape=[
            jax.ShapeDtypeStruct((t, QKV_W), BF16),
            jax.ShapeDtypeStruct((t, PF_W), F32),
        ],
        compiler_params=_cparams(("parallel",)),
        name="inproj",
    )(x2, gain, w)


def _attn_kernel(qi_ref, kj_ref, q_ref, k_ref, v_ref, bias_ref, gain_ref, lam_ref, o_ref,
                 qm_ref, m_ref, l_ref, acc_ref):
    p = pl.program_id(1)
    qi = qi_ref[p]
    kj = kj_ref[p]
    n_ht = 2 * DA_HEADS

    @pl.when(kj == 0)
    def _init():
        q = q_ref[...]
        for ht in range(n_ht):
            qm_ref[ht] = jnp.where(_group_mask(q.shape, 1, DA_HEAD_DIM, ht), q, jnp.zeros_like(q))
        m_ref[...] = jnp.full(m_ref.shape, NEG_BIG, F32)
        l_ref[...] = jnp.zeros(l_ref.shape, F32)
        acc_ref[...] = jnp.zeros(acc_ref.shape, F32)

    def step(with_bias):
        k = k_ref[...]
        for ht in range(n_ht):
            h = ht // 2
            s = lax.dot_general(qm_ref[ht], k, (((1,), (1,)), ((), ())), preferred_element_type=F32)
            if with_bias:
                s = s + bias_ref[0, h]
            m_prev = m_ref[ht]
            m_new = jnp.maximum(m_prev, jnp.max(s, axis=-1, keepdims=True))
            alpha = jnp.exp2(m_prev - m_new)
            e = jnp.exp2(s - _lane_tile(m_new, ATT_T // LANES))
            l_ref[ht] = alpha * l_ref[ht] + jnp.sum(e, axis=-1, keepdims=True)
            m_ref[ht] = m_new
            c0 = (h // 2) * LANES
            pv = jnp.dot(e.astype(BF16), v_ref[:, c0:c0 + LANES], preferred_element_type=F32)
            acc_ref[ht] = acc_ref[ht] * alpha + pv

    @pl.when(kj >= qi - 1)
    def _near():
        step(True)

    @pl.when(kj < qi - 1)
    def _far():
        step(False)

    @pl.when(kj == qi)
    def _finish():
        lam = lam_ref[...]
        lane_lo = lax.broadcasted_iota(jnp.int32, (ATT_T, LANES), 1) < DA_V_DIM
        for pair in range(DA_HEADS // 2):
            halves = []
            for sub in range(2):
                h = 2 * pair + sub
                o = acc_ref[2 * h] * (1.0 / l_ref[2 * h]) - lam * (acc_ref[2 * h + 1] * (1.0 / l_ref[2 * h + 1]))
                own = lane_lo if sub == 0 else jnp.logical_not(lane_lo)
                ms = jnp.sum(jnp.where(own, o * o, 0.0), axis=-1, keepdims=True) * (1.0 / DA_V_DIM)
                halves.append(o * lax.rsqrt(ms + EPS))
            blk = jnp.where(lane_lo, halves[0], halves[1])
            c0 = pair * LANES
            o_ref[:, c0:c0 + LANES] = (blk * gain_ref[:, c0:c0 + LANES]).astype(BF16)


def _attention(qkv, bias_tiles, gain, lam, batch, seq):
    nq = seq // ATT_T
    qi_l, kj_l = [], []
    for qi in range(nq):
        for kj in range(qi + 1):
            qi_l.append(qi)
            kj_l.append(kj)
    qi_tbl = jnp.asarray(np.array(qi_l, np.int32))
    kj_tbl = jnp.asarray(np.array(kj_l, np.int32))
    n_pairs = len(qi_l)
    n_ht = 2 * DA_HEADS

    grid_spec = pltpu.PrefetchScalarGridSpec(
        num_scalar_prefetch=2,
        grid=(batch, n_pairs),
        in_specs=[
            pl.BlockSpec((ATT_T, DA_WIDTH), lambda b, p, qi, kj: (b * nq + qi[p], 0)),
            pl.BlockSpec((ATT_T, DA_WIDTH), lambda b, p, qi, kj: (b * nq + kj[p], 1)),
            pl.BlockSpec((ATT_T, DA_WIDTH), lambda b, p, qi, kj: (b * nq + kj[p], 2)),
            pl.BlockSpec((1, DA_HEADS, ATT_T, ATT_T),
                         lambda b, p, qi, kj: (jnp.minimum(qi[p] - kj[p], 1), 0, 0, 0)),
            pl.BlockSpec((1, DA_WIDTH), lambda b, p, qi, kj: (0, 0)),
            pl.BlockSpec((1, LANES), lambda b, p, qi, kj: (0, 0)),
        ],
        out_specs=pl.BlockSpec((ATT_T, DA_WIDTH), lambda b, p, qi, kj: (b * nq + qi[p], 0)),
        scratch_shapes=[
            pltpu.VMEM((n_ht, ATT_T, DA_WIDTH), BF16),
            pltpu.VMEM((n_ht, ATT_T, LANES), F32),
            pltpu.VMEM((n_ht, ATT_T, LANES), F32),
            pltpu.VMEM((n_ht, ATT_T, LANES), F32),
        ],
    )
    return pl.pallas_call(
        _attn_kernel,
        grid_spec=grid_spec,
        out_shape=jax.ShapeDtypeStruct((batch * seq, DA_WIDTH), BF16),
        compiler_params=_cparams(("parallel", "arbitrary")),
        name="diff_attention",
    )(qi_tbl, kj_tbl, qkv, qkv, qkv, bias_tiles, gain, lam)


def _t5_bucket(rel):
    half = REL_BUCKETS // 2
    max_exact = half // 2
    ret = jnp.where(rel > 0, half, 0)
    n = jnp.abs(rel)
    nf = jnp.maximum(n, 1).astype(F32)
    large = max_exact + (jnp.log(nf / max_exact) / math.log(REL_MAX_DIST / max_exact)
                         * (half - max_exact)).astype(jnp.int32)
    large = jnp.minimum(large, half - 1)
    return ret + jnp.where(n < max_exact, n, large)


def _bias_tiles(rel_bias):
    i = jnp.arange(ATT_T)[:, None]
    j = jnp.arange(ATT_T)[None, :]
    far = rel_bias[_t5_bucket(jnp.int32(-2 * ATT_T))].astype(F32)
    tiles = []
    for off in range(2):
        bucket = _t5_bucket((j - off * ATT_T) - i)
        b = jnp.zeros((DA_HEADS, ATT_T, ATT_T), F32)
        for c in range(REL_BUCKETS):
            b = jnp.where((bucket == c)[None], rel_bias[c].astype(F32)[:, None, None], b)
        b = (b - far[:, None, None]) * LOG2E
        if off == 0:
            allowed = (j // CHUNK) <= (i // CHUNK)
            b = jnp.where(allowed[None], b, NEG_BIG)
        tiles.append(b)
    return jnp.stack(tiles)


def _cumsum_rows(x):
    n = x.shape[0]
    row = lax.broadcasted_iota(jnp.int32, x.shape, 0)
    sh = 1
    while sh < n:
        x = x + jnp.where(row >= sh, pltpu.roll(x, sh, axis=0), 0.0)
        sh *= 2
    return x


def _expand_heads(v, width, group):
    out = jnp.zeros((v.shape[0], width), F32)
    for h in range(width // group):
        col = jnp.broadcast_to(v[:, h:h + 1], (v.shape[0], width))
        out = jnp.where(_group_mask(out.shape, 1, group, h), col, out)
    return out


def _ssd_kernel(z_ref, xs_ref, bc_ref, dt_ref, cwx_ref, cwb_ref, cbx_ref, cbb_ref, dtb_ref, alog_ref,
                dskip_ref, ng_ref, o_ref, extx_ref, extb_ref, st_ref):
    j = pl.program_id(1)

    @pl.when(j == 0)
    def _reset():
        extx_ref[0:8, :] = jnp.zeros((8, SSM_WIDTH), F32)
        extb_ref[0:8, :] = jnp.zeros((8, 2 * LANES), F32)
        st_ref[...] = jnp.zeros(st_ref.shape, F32)

    extx_ref[8:8 + BLK, :] = xs_ref[...]
    extb_ref[8:8 + BLK, :] = bc_ref[...]
    yx = jnp.broadcast_to(cbx_ref[...], (BLK, SSM_WIDTH))
    yb = jnp.broadcast_to(cbb_ref[...], (BLK, 2 * LANES))
    for w in range(SSM_CONV):
        lo = 8 - (SSM_CONV - 1) + w
        yx = yx + extx_ref[lo:lo + BLK, :] * cwx_ref[w:w + 1, :]
        yb = yb + extb_ref[lo:lo + BLK, :] * cwb_ref[w:w + 1, :]
    tail_x = extx_ref[BLK:BLK + 8, :]
    tail_b = extb_ref[BLK:BLK + 8, :]
    extx_ref[0:8, :] = tail_x
    extb_ref[0:8, :] = tail_b
    xs = _silu(yx)
    bc = _silu(yb)
    bm = bc[:, :LANES]
    cm = bc[:, LANES:]

    dt = jax.nn.softplus(dt_ref[...] + dtb_ref[...])
    a = -jnp.exp(alog_ref[...]) * dt
    a_cs = _cumsum_rows(a)
    a_cs_t = a_cs.T
    a_last = a_cs[BLK - 1:BLK, :]

    dt_full = _expand_heads(dt, SSM_WIDTH, SSM_HEAD_DIM)
    ea_full = _expand_heads(jnp.exp(a_cs), SSM_WIDTH, SSM_HEAD_DIM)
    dec_full = _expand_heads(jnp.exp(a_last - a_cs), SSM_WIDTH, SSM_HEAD_DIM)
    cdec_full = _expand_heads(jnp.exp(a_last), SSM_WIDTH, SSM_HEAD_DIM)

    xdt = xs * dt_full
    xdt_b = xdt.astype(BF16)
    cm_b = cm.astype(BF16)
    bm_b = bm.astype(BF16)

    st = st_ref[...]
    y = jnp.dot(cm_b, st.astype(BF16), preferred_element_type=F32) * ea_full + xs * dskip_ref[...]

    row = lax.broadcasted_iota(jnp.int32, (BLK, BLK), 0)
    colio = lax.broadcasted_iota(jnp.int32, (BLK, BLK), 1)
    causal = colio <= row
    rep = SSM_HEADS // SSM_GROUPS
    lane_lo = lax.broadcasted_iota(jnp.int32, (BLK, LANES), 1) < SSM_HEAD_DIM
    pieces = []
    for g in range(SSM_GROUPS):
        cg = jnp.where(_group_mask(cm_b.shape, 1, SSM_STATE, g), cm_b, jnp.zeros_like(cm_b))
        cb = lax.dot_general(cg, bm_b, (((1,), (1,)), ((), ())), preferred_element_type=F32)
        for pair in range(rep // 2):
            halves = []
            for sub in range(2):
                h = g * rep + pair * 2 + sub
                seg = jnp.broadcast_to(a_cs[:, h:h + 1], (BLK, BLK)) - a_cs_t[h:h + 1, :]
                lmat = jnp.exp(jnp.where(causal, seg, NEG_BIG))
                mh = (cb * lmat).astype(BF16)
                c0 = (h // 2) * LANES
                halves.append(jnp.dot(mh, xdt_b[:, c0:c0 + LANES], preferred_element_type=F32))
            pieces.append(jnp.where(lane_lo, halves[0], halves[1]))
    y = y + jnp.concatenate(pieces, axis=1)

    upd = jnp.dot(bm.T.astype(BF16), (xdt * dec_full).astype(BF16), preferred_element_type=F32)
    r_io = lax.broadcasted_iota(jnp.int32, upd.shape, 0) // SSM_STATE
    c_io = lax.broadcasted_iota(jnp.int32, upd.shape, 1) // (SSM_HEAD_DIM * rep)
    st_ref[...] = st * cdec_full + jnp.where(r_io == c_io, upd, 0.0)

    y = y * _silu(z_ref[...])
    gw = SSM_WIDTH // SSM_GROUPS
    for g in range(SSM_GROUPS):
        yg = y[:, g * gw:(g + 1) * gw]
        ms = jnp.mean(yg * yg, axis=-1, keepdims=True)
        o_ref[:, g * gw:(g + 1) * gw] = (yg * lax.rsqrt(ms + EPS) * ng_ref[:, g * gw:(g + 1) * gw]).astype(BF16)


def _ssd(pf, cwx, cwb, cbx, cbb, dtb, alog, dskip, ng, batch, seq):
    nb = seq // BLK
    row = lambda w, c: pl.BlockSpec((BLK, w), lambda b, j: (b * nb + j, c))
    const = lambda r, w: pl.BlockSpec((r, w), lambda b, j: (0, 0))
    return pl.pallas_call(
        _ssd_kernel,
        grid=(batch, nb),
        in_specs=[
            row(SSM_WIDTH, C_Z // SSM_WIDTH), row(SSM_WIDTH, C_XS // SSM_WIDTH), row(2 * LANES, C_BC // (2 * LANES)),
            row(LANES, C_DT // LANES),
            const(SSM_CONV, SSM_WIDTH), const(SSM_CONV, 2 * LANES), const(1, SSM_WIDTH), const(1, 2 * LANES),
            const(1, LANES), const(1, LANES), const(1, SSM_WIDTH), const(1, SSM_WIDTH),
        ],
        out_specs=pl.BlockSpec((BLK, SSM_WIDTH), lambda b, j: (b * nb + j, 0)),
        out_shape=jax.ShapeDtypeStruct((batch * seq, SSM_WIDTH), BF16),
        scratch_shapes=[
            pltpu.VMEM((BLK + 8, SSM_WIDTH), F32),
            pltpu.VMEM((BLK + 8, 2 * LANES), F32),
            pltpu.VMEM((SSM_GROUPS * SSM_STATE, SSM_WIDTH), F32),
        ],
        compiler_params=_cparams(("parallel", "arbitrary")),
        name="ssd_mixer",
    )(pf, pf, pf, pf, cwx, cwb, cbx, cbb, dtb, alog, dskip, ng)


def _rotary(u, cos, sin_signed):
    first = (lax.broadcasted_iota(jnp.int32, u.shape, 1) % RET_KEY_DIM) < (RET_KEY_DIM // 2)
    half = RET_KEY_DIM // 2
    swapped = jnp.where(first, pltpu.roll(u, u.shape[1] - half, axis=1), pltpu.roll(u, half, axis=1))
    return u * cos + swapped * sin_signed


def _ret_kernel(q_ref, k_ref, v_ref, g_ref, cos_ref, sin_ref, dmat_ref, qdec_ref, kdec_ref, gdec_ref,
                bd_ref, gain_ref, o_ref, st_ref):
    j = pl.program_id(1)

    @pl.when(j == 0)
    def _reset():
        st_ref[...] = jnp.zeros(st_ref.shape, F32)

    cos = cos_ref[...]
    sin = sin_ref[...]
    q = _rotary(q_ref[...], cos, sin)
    k = _rotary(k_ref[...], cos, sin) * (RET_KEY_DIM ** -0.5)
    v_b = v_ref[...].astype(BF16)
    q_b = q.astype(BF16)
    k_b = k.astype(BF16)

    st = st_ref[...]
    o = jnp.dot((q * qdec_ref[...]).astype(BF16), st.astype(BF16), preferred_element_type=F32)
    for h in range(RET_HEADS):
        cm = _group_mask(q_b.shape, 1, RET_KEY_DIM, h)
        qh = jnp.where(cm, q_b, jnp.zeros_like(q_b))
        s = lax.dot_general(qh, k_b, (((1,), (1,)), ((), ())), preferred_element_type=F32) * dmat_ref[h]
        oh = jnp.dot(s.astype(BF16), v_b, preferred_element_type=F32)
        o = o + jnp.where(cm, oh, 0.0)

    upd = jnp.dot((k * kdec_ref[...]).T.astype(BF16), v_b, preferred_element_type=F32)
    st_ref[...] = st * gdec_ref[...] + upd * bd_ref[...]

    out = jnp.zeros(o.shape, F32)
    for h in range(RET_HEADS):
        cm = _group_mask(o.shape, 1, RET_KEY_DIM, h)
        ms = jnp.sum(jnp.where(cm, o * o, 0.0), axis=-1, keepdims=True) * (1.0 / RET_KEY_DIM)
        out = out + jnp.where(cm, o * lax.rsqrt(ms + EPS), 0.0)
    o_ref[...] = (_silu(g_ref[...]) * (out * gain_ref[...])).astype(BF16)


def _retention(pf, cos, sin, dmat, qdec, kdec, gdec, bd, gain, batch, seq):
    nb = seq // BLK
    row = lambda c: pl.BlockSpec((BLK, RET_WIDTH), lambda b, j: (b * nb + j, c))
    const2 = pl.BlockSpec((BLK, RET_WIDTH), lambda b, j: (0, 0))
    return pl.pallas_call(
        _ret_kernel,
        grid=(batch, nb),
        in_specs=[
            row(C_RQ // RET_WIDTH), row(C_RK // RET_WIDTH), row(C_RV // RET_WIDTH), row(C_RG // RET_WIDTH),
            pl.BlockSpec((BLK, RET_WIDTH), lambda b, j: (j, 0)),
            pl.BlockSpec((BLK, RET_WIDTH), lambda b, j: (j, 0)),
            pl.BlockSpec((RET_HEADS, BLK, BLK), lambda b, j: (0, 0, 0)),
            const2, const2, const2, const2,
            pl.BlockSpec((1, RET_WIDTH), lambda b, j: (0, 0)),
        ],
        out_specs=pl.BlockSpec((BLK, RET_WIDTH), lambda b, j: (b * nb + j, 0)),
        out_shape=jax.ShapeDtypeStruct((batch * seq, RET_WIDTH), BF16),
        scratch_shapes=[pltpu.VMEM((RET_WIDTH, RET_WIDTH), F32)],
        compiler_params=_cparams(("parallel", "arbitrary")),
        name="retention",
    )(pf, pf, pf, pf, cos, sin, dmat, qdec, kdec, gdec, bd, gain)


def _retention_tables(seq):
    half = RET_KEY_DIM // 2
    inv = 1.0 / (ROPE_BASE ** (jnp.arange(0, RET_KEY_DIM, 2, dtype=F32) / RET_KEY_DIM))
    ang = jnp.arange(seq, dtype=F32)[:, None] * inv[None, :]
    cos_h = jnp.concatenate([jnp.cos(ang), jnp.cos(ang)], axis=-1)
    sin_h = jnp.concatenate([-jnp.sin(ang), jnp.sin(ang)], axis=-1)
    cos = jnp.tile(cos_h, (1, RET_HEADS))
    sin = jnp.tile(sin_h, (1, RET_HEADS))
    log_gamma = jnp.log1p(-jnp.power(2.0, -5.0 - jnp.arange(RET_HEADS, dtype=F32)))
    idx = jnp.arange(BLK, dtype=F32)
    ii = jnp.arange(BLK)
    same_or_earlier_chunk = (ii[None, :] // CHUNK) <= (ii[:, None] // CHUNK)
    dmat = jnp.exp(log_gamma[:, None, None] * jnp.abs(idx[:, None] - idx[None, :]))
    dmat = jnp.where(same_or_earlier_chunk[None], dmat, 0.0)
    lg_cols = jnp.repeat(log_gamma, RET_KEY_DIM)
    qdec = jnp.exp(lg_cols[None, :] * (idx + 1.0)[:, None])
    kdec = jnp.exp(lg_cols[None, :] * (BLK - 1.0 - idx)[:, None])
    head_of = jnp.arange(RET_WIDTH) // RET_KEY_DIM
    bd = (head_of[:, None] == head_of[None, :]).astype(F32)
    gdec = jnp.exp(lg_cols * BLK)[:, None] * bd
    return cos, sin, dmat, qdec, kdec, gdec, bd


def _outproj_kernel(x_ref, a_ref, b_ref, c_ref, w_ref, g_ref, *rest, with_router):
    if with_router:
        rw_ref, xo_ref, h_ref, rinfo_ref = rest
    else:
        xo_ref, h_ref = rest
    acc = x_ref[...]
    acc = acc + jnp.dot(a_ref[...], w_ref[0:DA_WIDTH, :], preferred_element_type=F32)
    acc = acc + jnp.dot(b_ref[...], w_ref[DA_WIDTH:DA_WIDTH + SSM_WIDTH, :], preferred_element_type=F32)
    acc = acc + jnp.dot(c_ref[...], w_ref[DA_WIDTH + SSM_WIDTH:, :], preferred_element_type=F32)
    xo_ref[...] = acc
    ms = jnp.mean(acc * acc, axis=-1, keepdims=True)
    hf = acc * lax.rsqrt(ms + EPS) * g_ref[...]
    h = hf.astype(BF16)
    h_ref[...] = hf.astype(h_ref.dtype)
    if with_router:
        logits = jnp.dot(h, rw_ref[...], preferred_element_type=F32)
        lane = lax.broadcasted_iota(jnp.int32, logits.shape, 1).astype(F32)
        logits = jnp.where(lane < N_EXPERTS, logits, NEG_BIG)
        v1 = jnp.max(logits, axis=-1, keepdims=True)
        i1 = jnp.min(jnp.where(logits == v1, lane, float(LANES)), axis=-1, keepdims=True)
        rest_l = jnp.where(lane == i1, NEG_BIG, logits)
        v2 = jnp.max(rest_l, axis=-1, keepdims=True)
        i2 = jnp.min(jnp.where(rest_l == v2, lane, float(LANES)), axis=-1, keepdims=True)
        e2 = jnp.exp(v2 - v1)
        g1 = 1.0 / (1.0 + e2)
        g2 = e2 * g1
        rinfo_ref[...] = (jnp.where(lane == 0.0, g1, 0.0) + jnp.where(lane == 1.0, g2, 0.0)
                          + jnp.where(lane == 2.0, i1, 0.0) + jnp.where(lane == 3.0, i2, 0.0))


def _outproj(x2, oa, ob, oc, w, gain, router_w=None):
    t = x2.shape[0]
    with_router = router_w is not None
    tok = lambda w_: pl.BlockSpec((TM_PROJ, w_), lambda i: (i, 0))
    in_specs = [tok(D_MODEL), tok(DA_WIDTH), tok(SSM_WIDTH), tok(RET_WIDTH),
                pl.BlockSpec((D_MODEL, D_MODEL), lambda i: (0, 0)),
                pl.BlockSpec((1, D_MODEL), lambda i: (0, 0))]
    out_specs = [tok(D_MODEL), tok(D_MODEL)]
    h_dtype = F32 if with_router else BF16
    out_shape = [jax.ShapeDtypeStruct((t, D_MODEL), F32), jax.ShapeDtypeStruct((t, D_MODEL), h_dtype)]
    args = [x2, oa, ob, oc, w, gain]
    if with_router:
        in_specs.append(pl.BlockSpec((D_MODEL, LANES), lambda i: (0, 0)))
        out_specs.append(tok(LANES))
        out_shape.append(jax.ShapeDtypeStruct((t, LANES), F32))
        args.append(router_w)
    return pl.pallas_call(
        functools.partial(_outproj_kernel, with_router=with_router),
        grid=(t // TM_PROJ,),
        in_specs=in_specs,
        out_specs=out_specs,
        out_shape=out_shape,
        compiler_params=_cparams(("parallel",)),
        name="outproj_router" if with_router else "outproj",
    )(*args)


def _ffn_kernel(x_ref, h_ref, wg_ref, wu_ref, wd_ref, o_ref, acc_ref):
    f = pl.program_id(1)

    @pl.when(f == 0)
    def _init():
        acc_ref[...] = x_ref[...]

    h = h_ref[...]
    gate = jnp.dot(h, wg_ref[...], preferred_element_type=F32)
    up = jnp.dot(h, wu_ref[...], preferred_element_type=F32)
    act = (_silu(gate) * up).astype(BF16)
    acc_ref[...] += jnp.dot(act, wd_ref[...], preferred_element_type=F32)

    @pl.when(f == pl.num_programs(1) - 1)
    def _done():
        o_ref[...] = acc_ref[...]


def _dense_ffn(x2, h, wg, wu, wd):
    t = x2.shape[0]
    nf = wg.shape[1] // TF_FFN
    return pl.pallas_call(
        _ffn_kernel,
        grid=(t // TM_FFN, nf),
        in_specs=[
            pl.BlockSpec((TM_FFN, D_MODEL), lambda i, f: (i, 0)),
            pl.BlockSpec((TM_FFN, D_MODEL), lambda i, f: (i, 0)),
            pl.BlockSpec((D_MODEL, TF_FFN), lambda i, f: (0, f)),
            pl.BlockSpec((D_MODEL, TF_FFN), lambda i, f: (0, f)),
            pl.BlockSpec((TF_FFN, D_MODEL), lambda i, f: (f, 0)),
        ],
        out_specs=pl.BlockSpec((TM_FFN, D_MODEL), lambda i, f: (i, 0)),
        out_shape=jax.ShapeDtypeStruct((t, D_MODEL), F32),
        scratch_shapes=[pltpu.VMEM((TM_FFN, D_MODEL), F32)],
        compiler_params=_cparams(("parallel", "arbitrary")),
        name="dense_swiglu",
    )(x2, h, wg, wu, wd)


def _issue_row_gather(src_hbm, idx_ref, idx_base, dst, sem, n_rows):
    def body(r, carry):
        tok = idx_ref[idx_base + r]
        pltpu.make_async_copy(src_hbm.at[pl.ds(tok, 1)], dst.at[pl.ds(r, 1)], sem).start()
        return carry
    lax.fori_loop(0, n_rows, body, 0, unroll=8)


def _wait_row_gather(src_hbm, dst, sem, n_rows):
    pltpu.make_async_copy(src_hbm.at[pl.ds(0, n_rows)], dst, sem).wait()


def _moe_ffn_kernel(te_ref, nv_ref, rt_ref, h_hbm, wg_ref, wu_ref, wd_ref, y_ref, hbuf, xb_ref, acc_ref, sem):
    i = pl.program_id(0)
    f = pl.program_id(1)
    n_valid = nv_ref[0]
    valid = i < n_valid
    slot = i % 2

    @pl.when((f == 0) & (i == 0) & valid)
    def _first():
        _issue_row_gather(h_hbm, rt_ref, 0, hbuf.at[0], sem.at[0], TR_MOE)

    @pl.when((f == 0) & valid)
    def _arrive():
        _wait_row_gather(h_hbm, hbuf.at[slot], sem.at[slot], TR_MOE)
        xb_ref[...] = hbuf[slot].astype(BF16)

    @pl.when((f == 0) & (i + 1 < n_valid))
    def _prefetch():
        _issue_row_gather(h_hbm, rt_ref, (i + 1) * TR_MOE, hbuf.at[1 - slot], sem.at[1 - slot], TR_MOE)

    @pl.when(valid)
    def _compute():
        xb = xb_ref[...]
        gate = jnp.dot(xb, wg_ref[0], preferred_element_type=F32)
        up = jnp.dot(xb, wu_ref[0], preferred_element_type=F32)
        act = (_silu(gate) * up).astype(BF16)
        contrib = jnp.dot(act, wd_ref[0], preferred_element_type=F32)

        @pl.when(f == 0)
        def _():
            acc_ref[...] = contrib

        @pl.when(f > 0)
        def _():
            acc_ref[...] += contrib

    @pl.when(f == pl.num_programs(1) - 1)
    def _store():
        y_ref[...] = jnp.where(valid, acc_ref[...], 0.0)


def _moe_ffn(h, tile_expert, n_valid, row_token, wg, wu, wd):
    n_tiles = tile_expert.shape[0]
    nf = wg.shape[2] // TF_MOE

    def w_in_map(i, f, te, nv, rt):
        return (te[i], 0, jnp.where(i < nv[0], f, nf - 1))

    def w_out_map(i, f, te, nv, rt):
        return (te[i], jnp.where(i < nv[0], f, nf - 1), 0)

    grid_spec = pltpu.PrefetchScalarGridSpec(
        num_scalar_prefetch=3,
        grid=(n_tiles, nf),
        in_specs=[
            pl.BlockSpec(memory_space=pl.ANY),
            pl.BlockSpec((1, D_MODEL, TF_MOE), w_in_map),
            pl.BlockSpec((1, D_MODEL, TF_MOE), w_in_map),
            pl.BlockSpec((1, TF_MOE, D_MODEL), w_out_map),
        ],
        out_specs=pl.BlockSpec((TR_MOE, D_MODEL), lambda i, f, te, nv, rt: (i, 0)),
        scratch_shapes=[
            pltpu.VMEM((2, TR_MOE, D_MODEL), F32),
            pltpu.VMEM((TR_MOE, D_MODEL), BF16),
            pltpu.VMEM((TR_MOE, D_MODEL), F32),
            pltpu.SemaphoreType.DMA((2,)),
        ],
    )
    return pl.pallas_call(
        _moe_ffn_kernel,
        grid_spec=grid_spec,
        out_shape=jax.ShapeDtypeStruct((n_tiles * TR_MOE, D_MODEL), F32),
        compiler_params=_cparams(("arbitrary", "arbitrary")),
        name="moe_expert_ffn",
    )(tile_expert, n_valid, row_token, h, wg, wu, wd)


def _moe_combine_kernel(tr_ref, x_ref, rinfo_ref, y_hbm, fg_ref, o_ref, ybuf, sem):
    i = pl.program_id(0)
    slot = i % 2
    n_rows = TOP_K * TM_COMB

    @pl.when(i == 0)
    def _first():
        _issue_row_gather(y_hbm, tr_ref, 0, ybuf.at[0], sem.at[0], n_rows)

    _wait_row_gather(y_hbm, ybuf.at[slot], sem.at[slot], n_rows)

    @pl.when(i + 1 < pl.num_programs(0))
    def _prefetch():
        _issue_row_gather(y_hbm, tr_ref, (i + 1) * n_rows, ybuf.at[1 - slot], sem.at[1 - slot], n_rows)

    rinfo = rinfo_ref[...]
    g1 = jnp.broadcast_to(rinfo[:, 0:1], (TM_COMB, D_MODEL))
    g2 = jnp.broadcast_to(rinfo[:, 1:2], (TM_COMB, D_MODEL))
    y = x_ref[...] + g1 * ybuf[slot, 0:TM_COMB, :] + g2 * ybuf[slot, TM_COMB:n_rows, :]
    ms = jnp.mean(y * y, axis=-1, keepdims=True)
    o_ref[...] = y * lax.rsqrt(ms + EPS) * fg_ref[...]


def _moe_combine(x2, rinfo, y_rows, tok_rows, final_gain):
    t = x2.shape[0]
    grid_spec = pltpu.PrefetchScalarGridSpec(
        num_scalar_prefetch=1,
        grid=(t // TM_COMB,),
        in_specs=[
            pl.BlockSpec((TM_COMB, D_MODEL), lambda i, tr: (i, 0)),
            pl.BlockSpec((TM_COMB, LANES), lambda i, tr: (i, 0)),
            pl.BlockSpec(memory_space=pl.ANY),
            pl.BlockSpec((1, D_MODEL), lambda i, tr: (0, 0)),
        ],
        out_specs=pl.BlockSpec((TM_COMB, D_MODEL), lambda i, tr: (i, 0)),
        scratch_shapes=[
            pltpu.VMEM((2, TOP_K * TM_COMB, D_MODEL), F32),
            pltpu.SemaphoreType.DMA((2,)),
        ],
    )
    return pl.pallas_call(
        _moe_combine_kernel,
        grid_spec=grid_spec,
        out_shape=jax.ShapeDtypeStruct((t, D_MODEL), F32),
        compiler_params=_cparams(("arbitrary",)),
        name="moe_combine_norm",
    )(tok_rows, x2, rinfo, y_rows, final_gain)


def _routing_tables(rinfo):
    t = rinfo.shape[0]
    n_tiles = (TOP_K * t) // TR_MOE + N_EXPERTS
    flat_e = rinfo[:, 2:2 + TOP_K].astype(jnp.int32).reshape(-1)
    onehot = (flat_e[:, None] == jnp.arange(N_EXPERTS)[None, :]).astype(jnp.int32)
    csum = jnp.cumsum(onehot, axis=0)
    rank = jnp.sum((csum - onehot) * onehot, axis=1)
    counts = csum[-1]
    tiles_per_e = (counts + TR_MOE - 1) // TR_MOE
    tile_end = jnp.cumsum(tiles_per_e)
    row_off = (tile_end - tiles_per_e) * TR_MOE
    dest = row_off[flat_e] + rank
    token = jnp.arange(TOP_K * t, dtype=jnp.int32) // TOP_K
    row_token = jnp.zeros((n_tiles * TR_MOE,), jnp.int32).at[dest].set(token, unique_indices=True)
    n_valid = tile_end[-1:].astype(jnp.int32)
    tile_expert = jnp.searchsorted(tile_end, jnp.arange(n_tiles), side="right").astype(jnp.int32)
    last_e = jnp.searchsorted(tile_end, n_valid[0] - 1, side="right").astype(jnp.int32)
    tile_expert = jnp.where(jnp.arange(n_tiles) < n_valid[0], tile_expert, last_e)
    tok_rows = dest.reshape(t // TM_COMB, TM_COMB, TOP_K).transpose(0, 2, 1).reshape(-1).astype(jnp.int32)
    return tile_expert, n_valid, row_token, tok_rows


def _permute_w_in(w):
    sizes = (256, 256, 256, 512, 768, 8, 256, 256, 256, 256)
    offs = np.concatenate([[0], np.cumsum(sizes)])
    part = lambda n: w[:, offs[n]:offs[n + 1]]
    dt_pad = jnp.zeros((w.shape[0], LANES - SSM_HEADS), w.dtype)
    cols = [part(0) * (DA_HEAD_DIM ** -0.5 * LOG2E), part(1), part(2), part(3), part(4),
            part(6), part(7), part(8), part(9), part(5), dt_pad]
    return jnp.concatenate(cols, axis=1).astype(BF16)


def _pad_lanes(v, fill=0.0):
    return jnp.concatenate([v, jnp.full((LANES - v.shape[0],), fill, v.dtype)])[None, :]


def kernel(x, w_in, w_out, attn_norm, ffn_norm, final_norm, rel_bias, lambda_q1, lambda_k1, lambda_q2,
           lambda_k2, da_head_norm, conv_w, conv_b, dt_bias, a_log, d_skip, ssm_norm, ret_head_norm,
           w_gate, w_up, w_down, router_w, e_gate, e_up, e_down):
    batch, seq, _ = x.shape
    x2 = x.reshape(batch * seq, D_MODEL)
    bias_tiles = _bias_tiles(rel_bias)
    ret_tabs = _retention_tables(seq)

    for layer in range(DEPTH):
        lam_init = 0.8 - 0.6 * math.exp(-0.3 * layer)
        lam = (jnp.exp(jnp.sum(lambda_q1[layer] * lambda_k1[layer]))
               - jnp.exp(jnp.sum(lambda_q2[layer] * lambda_k2[layer])) + lam_init).astype(F32)
        lam = jnp.full((1, LANES), lam, F32)

        qkv, pf = _inproj(x2, attn_norm[layer][None, :], _permute_w_in(w_in[layer]))

        da_gain = (jnp.tile(da_head_norm[layer], DA_HEADS) * (1.0 - lam_init))[None, :]
        out_a = _attention(qkv, bias_tiles, da_gain, lam, batch, seq)

        cw, cb = conv_w[layer], conv_b[layer]
        out_b = _ssd(pf, cw[:, :SSM_WIDTH], cw[:, SSM_WIDTH:], cb[None, :SSM_WIDTH], cb[None, SSM_WIDTH:],
                     _pad_lanes(dt_bias[layer]), _pad_lanes(a_log[layer]),
                     jnp.repeat(d_skip[layer], SSM_HEAD_DIM)[None, :], ssm_norm[layer][None, :], batch, seq)

        ret_gain = jnp.tile(ret_head_norm[layer], RET_HEADS)[None, :]
        out_c = _retention(pf, *ret_tabs, ret_gain, batch, seq)

        w_o = w_out[layer].astype(BF16)
        if layer % 2 == 0:
            i = layer // 2
            x2, h = _outproj(x2, out_a, out_b, out_c, w_o, ffn_norm[layer][None, :])
            pad = FFN_PAD - FFN_DENSE
            wg = jnp.pad(w_gate[i], ((0, 0), (0, pad))).astype(BF16)
            wu = jnp.pad(w_up[i], ((0, 0), (0, pad))).astype(BF16)
            wd = jnp.pad(w_down[i], ((0, pad), (0, 0))).astype(BF16)
            x2 = _dense_ffn(x2, h, wg, wu, wd)
        else:
            i = layer // 2
            rw = jnp.pad(router_w[i], ((0, 0), (0, LANES - N_EXPERTS))).astype(BF16)
            x2, h, rinfo = _outproj(x2, out_a, out_b, out_c, w_o, ffn_norm[layer][None, :], rw)
            tile_expert, n_valid, row_token, tok_rows = _routing_tables(rinfo)
            y_rows = _moe_ffn(h, tile_expert, n_valid, row_token,
                              e_gate[i].astype(BF16), e_up[i].astype(BF16), e_down[i].astype(BF16))
            x2 = _moe_combine(x2, rinfo, y_rows, tok_rows, final_norm[None, :])
    return x2.reshape(batch, seq, D_MODEL)
```

```python
import functools
import math

import jax
import jax.numpy as jnp
import numpy as np
from jax import lax
from jax.experimental import pallas as pl
from jax.experimental.pallas import tpu as pltpu

F32 = jnp.float32
BF16 = jnp.bfloat16

D_MODEL = 1024
DEPTH = 2
CHUNK = 64
EPS = 1e-6
DA_HEADS = 4
DA_HEAD_DIM = 32
DA_V_DIM = 64
DA_WIDTH = 256
SSM_HEADS = 8
SSM_HEAD_DIM = 64
SSM_WIDTH = 512
SSM_STATE = 64
SSM_GROUPS = 2
SSM_CONV = 4
RET_HEADS = 4
RET_KEY_DIM = 64
RET_WIDTH = 256
ROPE_BASE = 10000.0
REL_BUCKETS = 32
REL_MAX_DIST = 128
FFN_DENSE = 2752
N_EXPERTS = 8
FFN_EXPERT = 3584

LANES = 128
VMEM_LIMIT = 48 * 1024 * 1024
NEG_BIG = -1e30

TM_PROJ = 512
ATT_T = 512
LOG2E = math.log2(math.e)
BLK = 256
FFN_PAD = 2816
TM_FFN = 1024
TF_FFN = 256
TF_MOE = 512
TR_MOE = 896
TM_COMB = 256
TOP_K = 2

C_Z, C_XS, C_BC, C_RQ, C_RK, C_RV, C_RG, C_DT = 0, 512, 1024, 1280, 1536, 1792, 2048, 2304
PF_W = 2432
PF_CHUNK = 640
QKV_W = 768


def _cparams(sem):
    return pltpu.CompilerParams(dimension_semantics=sem, vmem_limit_bytes=VMEM_LIMIT)


def _group_mask(shape, axis, group, idx):
    io = lax.broadcasted_iota(jnp.int32, shape, axis)
    return (io >= idx * group) & (io < (idx + 1) * group)


def _lane_tile(x, n, axis=1):
    return jnp.concatenate([x] * n, axis=axis)


def _silu(x):
    return x * (1.0 / (1.0 + jnp.exp(-x)))


def _inproj_kernel(x_ref, g_ref, w_ref, qkv_ref, kt_ref, pf_ref):
    x = x_ref[...]
    ms = jnp.mean(x * x, axis=-1, keepdims=True)
    h = (x * lax.rsqrt(ms + EPS) * g_ref[...]).astype(BF16)
    qkv = jnp.dot(h, w_ref[:, :QKV_W], preferred_element_type=F32)
    qkv_ref[...] = qkv.astype(BF16)
    kt_ref[...] = qkv[:, DA_WIDTH:2 * DA_WIDTH].T.astype(BF16)
    for lo in range(0, PF_W, PF_CHUNK):
        hi = min(lo + PF_CHUNK, PF_W)
        pf_ref[:, lo:hi] = jnp.dot(h, w_ref[:, QKV_W + lo:QKV_W + hi], preferred_element_type=F32)


def _inproj(x2, gain, w):
    t = x2.shape[0]
    return pl.pallas_call(
        _inproj_kernel,
        grid=(t // TM_PROJ,),
        in_specs=[
            pl.BlockSpec((TM_PROJ, D_MODEL), lambda i: (i, 0)),
            pl.BlockSpec((1, D_MODEL), lambda i: (0, 0)),
            pl.BlockSpec((D_MODEL, QKV_W + PF_W), lambda i: (0, 0)),
        ],
        out_specs=[
            pl.BlockSpec((TM_PROJ, QKV_W), lambda i: (i, 0)),
            pl.BlockSpec((DA_WIDTH, TM_PROJ), lambda i: (0, i)),
            pl.BlockSpec((TM_PROJ, PF_W), lambda i: (i, 0)),
        ],
        out_shape=[
            jax.ShapeDtypeStruct((t, QKV_W), BF16),
            jax.ShapeDtypeStruct((DA_WIDTH, t), BF16),
            jax.ShapeDtypeStruct((t, PF_W), F32),
        ],
        compiler_params=_cparams(("parallel",)),
        name="inproj",
    )(x2, gain, w)


def _attn_kernel(qi_ref, kj_ref, q_ref, kt_ref, v_ref, bias_ref, gain_ref, lam_ref, o_ref,
                 qm_ref, m_ref, l_ref, acc_ref):
    p = pl.program_id(1)
    qi = qi_ref[p]
    kj = kj_ref[p]
    n_ht = 2 * DA_HEADS

    @pl.when(kj == 0)
    def _init():
        q = q_ref[...]
        for ht in range(n_ht):
            qm_ref[ht] = jnp.where(_group_mask(q.shape, 1, DA_HEAD_DIM, ht), q, jnp.zeros_like(q))
        m_ref[...] = jnp.full(m_ref.shape, NEG_BIG, F32)
        l_ref[...] = jnp.zeros(l_ref.shape, F32)
        acc_ref[...] = jnp.zeros(acc_ref.shape, F32)

    def step(with_bias):
        kt = kt_ref[...]
        for ht in range(n_ht):
            h = ht // 2
            s = jnp.dot(qm_ref[ht], kt, preferred_element_type=F32)
            if with_bias:
                s = s + bias_ref[0, h]
            m_prev = m_ref[ht]
            m_new = jnp.maximum(m_prev, jnp.max(s, axis=-1, keepdims=True))
            alpha = jnp.exp2(m_prev - m_new)
            e = jnp.exp2(s - _lane_tile(m_new, ATT_T // LANES))
            l_ref[ht] = alpha * l_ref[ht] + jnp.sum(e, axis=-1, keepdims=True)
            m_ref[ht] = m_new
            c0 = (h // 2) * LANES
            pv = jnp.dot(e.astype(BF16), v_ref[:, c0:c0 + LANES], preferred_element_type=F32)
            acc_ref[ht] = acc_ref[ht] * alpha + pv

    @pl.when(kj >= qi - 1)
    def _near():
        step(True)

    @pl.when(kj < qi - 1)
    def _far():
        step(False)

    @pl.when(kj == qi)
    def _finish():
        lam = lam_ref[...]
        lane_lo = lax.broadcasted_iota(jnp.int32, (ATT_T, LANES), 1) < DA_V_DIM
        for pair in range(DA_HEADS // 2):
            halves = []
            for sub in range(2):
                h = 2 * pair + sub
                o = acc_ref[2 * h] * (1.0 / l_ref[2 * h]) - lam * (acc_ref[2 * h + 1] * (1.0 / l_ref[2 * h + 1]))
                own = lane_lo if sub == 0 else jnp.logical_not(lane_lo)
                ms = jnp.sum(jnp.where(own, o * o, 0.0), axis=-1, keepdims=True) * (1.0 / DA_V_DIM)
                halves.append(o * lax.rsqrt(ms + EPS))
            blk = jnp.where(lane_lo, halves[0], halves[1])
            c0 = pair * LANES
            o_ref[:, c0:c0 + LANES] = (blk * gain_ref[:, c0:c0 + LANES]).astype(BF16)


def _attention(qkv, kt, bias_tiles, gain, lam, batch, seq):
    nq = seq // ATT_T
    qi_l, kj_l = [], []
    for qi in range(nq):
        for kj in range(qi + 1):
            qi_l.append(qi)
            kj_l.append(kj)
    qi_tbl = jnp.asarray(np.array(qi_l, np.int32))
    kj_tbl = jnp.asarray(np.array(kj_l, np.int32))
    n_pairs = len(qi_l)
    n_ht = 2 * DA_HEADS

    grid_spec = pltpu.PrefetchScalarGridSpec(
        num_scalar_prefetch=2,
        grid=(batch, n_pairs),
        in_specs=[
            pl.BlockSpec((ATT_T, DA_WIDTH), lambda b, p, qi, kj: (b * nq + qi[p], 0)),
            pl.BlockSpec((DA_WIDTH, ATT_T), lambda b, p, qi, kj: (0, b * nq + kj[p])),
            pl.BlockSpec((ATT_T, DA_WIDTH), lambda b, p, qi, kj: (b * nq + kj[p], 2)),
            pl.BlockSpec((1, DA_HEADS, ATT_T, ATT_T),
                         lambda b, p, qi, kj: (jnp.minimum(qi[p] - kj[p], 1), 0, 0, 0)),
            pl.BlockSpec((1, DA_WIDTH), lambda b, p, qi, kj: (0, 0)),
            pl.BlockSpec((1, LANES), lambda b, p, qi, kj: (0, 0)),
        ],
        out_specs=pl.BlockSpec((ATT_T, DA_WIDTH), lambda b, p, qi, kj: (b * nq + qi[p], 0)),
        scratch_shapes=[
            pltpu.VMEM((n_ht, ATT_T, DA_WIDTH), BF16),
            pltpu.VMEM((n_ht, ATT_T, LANES), F32),
            pltpu.VMEM((n_ht, ATT_T, LANES), F32),
            pltpu.VMEM((n_ht, ATT_T, LANES), F32),
        ],
    )
    return pl.pallas_call(
        _attn_kernel,
        grid_spec=grid_spec,
        out_shape=jax.ShapeDtypeStruct((batch * seq, DA_WIDTH), BF16),
        compiler_params=_cparams(("parallel", "arbitrary")),
        name="diff_attention",
    )(qi_tbl, kj_tbl, qkv, kt, qkv, bias_tiles, gain, lam)


def _t5_bucket(rel):
    half = REL_BUCKETS // 2
    max_exact = half // 2
    ret = jnp.where(rel > 0, half, 0)
    n = jnp.abs(rel)
    nf = jnp.maximum(n, 1).astype(F32)
    large = max_exact + (jnp.log(nf / max_exact) / math.log(REL_MAX_DIST / max_exact)
                         * (half - max_exact)).astype(jnp.int32)
    large = jnp.minimum(large, half - 1)
    return ret + jnp.where(n < max_exact, n, large)


def _bias_tiles(rel_bias):
    i = jnp.arange(ATT_T)[:, None]
    j = jnp.arange(ATT_T)[None, :]
    far = rel_bias[_t5_bucket(jnp.int32(-2 * ATT_T))].astype(F32)
    tiles = []
    for off in range(2):
        bucket = _t5_bucket((j - off * ATT_T) - i)
        b = jnp.zeros((DA_HEADS, ATT_T, ATT_T), F32)
        for c in range(REL_BUCKETS):
            b = jnp.where((bucket == c)[None], rel_bias[c].astype(F32)[:, None, None], b)
        b = (b - far[:, None, None]) * LOG2E
        if off == 0:
            allowed = (j // CHUNK) <= (i // CHUNK)
            b = jnp.where(allowed[None], b, NEG_BIG)
        tiles.append(b)
    return jnp.stack(tiles)


def _cumsum_rows(x):
    n = x.shape[0]
    row = lax.broadcasted_iota(jnp.int32, x.shape, 0)
    sh = 1
    while sh < n:
        x = x + jnp.where(row >= sh, pltpu.roll(x, sh, axis=0), 0.0)
        sh *= 2
    return x


def _expand_heads(v, width, group):
    out = jnp.zeros((v.shape[0], width), F32)
    for h in range(width // group):
        col = jnp.broadcast_to(v[:, h:h + 1], (v.shape[0], width))
        out = jnp.where(_group_mask(out.shape, 1, group, h), col, out)
    return out


def _ssd_kernel(z_ref, xs_ref, bc_ref, dt_ref, cwx_ref, cwb_ref, cbx_ref, cbb_ref, dtb_ref, alog_ref,
                dskip_ref, ng_ref, o_ref, extx_ref, extb_ref, st_ref):
    j = pl.program_id(1)

    @pl.when(j == 0)
    def _reset():
        extx_ref[0:8, :] = jnp.zeros((8, SSM_WIDTH), F32)
        extb_ref[0:8, :] = jnp.zeros((8, 2 * LANES), F32)
        st_ref[...] = jnp.zeros(st_ref.shape, F32)

    extx_ref[8:8 + BLK, :] = xs_ref[...]
    extb_ref[8:8 + BLK, :] = bc_ref[...]
    yx = jnp.broadcast_to(cbx_ref[...], (BLK, SSM_WIDTH))
    yb = jnp.broadcast_to(cbb_ref[...], (BLK, 2 * LANES))
    for w in range(SSM_CONV):
        lo = 8 - (SSM_CONV - 1) + w
        yx = yx + extx_ref[lo:lo + BLK, :] * cwx_ref[w:w + 1, :]
        yb = yb + extb_ref[lo:lo + BLK, :] * cwb_ref[w:w + 1, :]
    tail_x = extx_ref[BLK:BLK + 8, :]
    tail_b = extb_ref[BLK:BLK + 8, :]
    extx_ref[0:8, :] = tail_x
    extb_ref[0:8, :] = tail_b
    xs = _silu(yx)
    bc = _silu(yb)
    bm = bc[:, :LANES]
    cm = bc[:, LANES:]

    dt = jax.nn.softplus(dt_ref[...] + dtb_ref[...])
    a = -jnp.exp(alog_ref[...]) * dt
    a_cs = _cumsum_rows(a)
    a_cs_t = a_cs.T
    a_last = a_cs[BLK - 1:BLK, :]

    dt_full = _expand_heads(dt, SSM_WIDTH, SSM_HEAD_DIM)
    ea_full = _expand_heads(jnp.exp(a_cs), SSM_WIDTH, SSM_HEAD_DIM)
    dec_full = _expand_heads(jnp.exp(a_last - a_cs), SSM_WIDTH, SSM_HEAD_DIM)
    cdec_full = _expand_heads(jnp.exp(a_last), SSM_WIDTH, SSM_HEAD_DIM)

    xdt = xs * dt_full
    xdt_b = xdt.astype(BF16)
    cm_b = cm.astype(BF16)
    bm_b = bm.astype(BF16)

    st = st_ref[...]
    y = jnp.dot(cm_b, st.astype(BF16), preferred_element_type=F32) * ea_full + xs * dskip_ref[...]

    row = lax.broadcasted_iota(jnp.int32, (BLK, BLK), 0)
    colio = lax.broadcasted_iota(jnp.int32, (BLK, BLK), 1)
    causal = colio <= row
    rep = SSM_HEADS // SSM_GROUPS
    lane_lo = lax.broadcasted_iota(jnp.int32, (BLK, LANES), 1) < SSM_HEAD_DIM
    pieces = []
    for g in range(SSM_GROUPS):
        cg = jnp.where(_group_mask(cm_b.shape, 1, SSM_STATE, g), cm_b, jnp.zeros_like(cm_b))
        cb = lax.dot_general(cg, bm_b, (((1,), (1,)), ((), ())), preferred_element_type=F32)
        for pair in range(rep // 2):
            halves = []
            for sub in range(2):
                h = g * rep + pair * 2 + sub
                seg = jnp.broadcast_to(a_cs[:, h:h + 1], (BLK, BLK)) - a_cs_t[h:h + 1, :]
                lmat = jnp.exp(jnp.where(causal, seg, NEG_BIG))
                mh = (cb * lmat).astype(BF16)
                c0 = (h // 2) * LANES
                halves.append(jnp.dot(mh, xdt_b[:, c0:c0 + LANES], preferred_element_type=F32))
            pieces.append(jnp.where(lane_lo, halves[0], halves[1]))
    y = y + jnp.concatenate(pieces, axis=1)

    upd = jnp.dot(bm.T.astype(BF16), (xdt * dec_full).astype(BF16), preferred_element_type=F32)
    r_io = lax.broadcasted_iota(jnp.int32, upd.shape, 0) // SSM_STATE
    c_io = lax.broadcasted_iota(jnp.int32, upd.shape, 1) // (SSM_HEAD_DIM * rep)
    st_ref[...] = st * cdec_full + jnp.where(r_io == c_io, upd, 0.0)

    y = y * _silu(z_ref[...])
    gw = SSM_WIDTH // SSM_GROUPS
    for g in range(SSM_GROUPS):
        yg = y[:, g * gw:(g + 1) * gw]
        ms = jnp.mean(yg * yg, axis=-1, keepdims=True)
        o_ref[:, g * gw:(g + 1) * gw] = (yg * lax.rsqrt(ms + EPS) * ng_ref[:, g * gw:(g + 1) * gw]).astype(BF16)


def _ssd(pf, cwx, cwb, cbx, cbb, dtb, alog, dskip, ng, batch, seq):
    nb = seq // BLK
    row = lambda w, c: pl.BlockSpec((BLK, w), lambda b, j: (b * nb + j, c))
    const = lambda r, w: pl.BlockSpec((r, w), lambda b, j: (0, 0))
    return pl.pallas_call(
        _ssd_kernel,
        grid=(batch, nb),
        in_specs=[
            row(SSM_WIDTH, C_Z // SSM_WIDTH), row(SSM_WIDTH, C_XS // SSM_WIDTH), row(2 * LANES, C_BC // (2 * LANES)),
            row(LANES, C_DT // LANES),
            const(SSM_CONV, SSM_WIDTH), const(SSM_CONV, 2 * LANES), const(1, SSM_WIDTH), const(1, 2 * LANES),
            const(1, LANES), const(1, LANES), const(1, SSM_WIDTH), const(1, SSM_WIDTH),
        ],
        out_specs=pl.BlockSpec((BLK, SSM_WIDTH), lambda b, j: (b * nb + j, 0)),
        out_shape=jax.ShapeDtypeStruct((batch * seq, SSM_WIDTH), BF16),
        scratch_shapes=[
            pltpu.VMEM((BLK + 8, SSM_WIDTH), F32),
            pltpu.VMEM((BLK + 8, 2 * LANES), F32),
            pltpu.VMEM((SSM_GROUPS * SSM_STATE, SSM_WIDTH), F32),
        ],
        compiler_params=_cparams(("parallel", "arbitrary")),
        name="ssd_mixer",
    )(pf, pf, pf, pf, cwx, cwb, cbx, cbb, dtb, alog, dskip, ng)


def _rotary(u, cos, sin_signed):
    first = (lax.broadcasted_iota(jnp.int32, u.shape, 1) % RET_KEY_DIM) < (RET_KEY_DIM // 2)
    half = RET_KEY_DIM // 2
    swapped = jnp.where(first, pltpu.roll(u, u.shape[1] - half, axis=1), pltpu.roll(u, half, axis=1))
    return u * cos + swapped * sin_signed


def _ret_kernel(q_ref, k_ref, v_ref, g_ref, cos_ref, sin_ref, dmat_ref, qdec_ref, kdec_ref, gdec_ref,
                bd_ref, gain_ref, o_ref, st_ref):
    j = pl.program_id(1)

    @pl.when(j == 0)
    def _reset():
        st_ref[...] = jnp.zeros(st_ref.shape, F32)

    cos = cos_ref[...]
    sin = sin_ref[...]
    q = _rotary(q_ref[...], cos, sin)
    k = _rotary(k_ref[...], cos, sin) * (RET_KEY_DIM ** -0.5)
    v_b = v_ref[...].astype(BF16)
    q_b = q.astype(BF16)
    k_b = k.astype(BF16)

    st = st_ref[...]
    o = jnp.dot((q * qdec_ref[...]).astype(BF16), st.astype(BF16), preferred_element_type=F32)
    for h in range(RET_HEADS):
        cm = _group_mask(q_b.shape, 1, RET_KEY_DIM, h)
        qh = jnp.where(cm, q_b, jnp.zeros_like(q_b))
        s = lax.dot_general(qh, k_b, (((1,), (1,)), ((), ())), preferred_element_type=F32) * dmat_ref[h]
        oh = jnp.dot(s.astype(BF16), v_b, preferred_element_type=F32)
        o = o + jnp.where(cm, oh, 0.0)

    upd = jnp.dot((k * kdec_ref[...]).T.astype(BF16), v_b, preferred_element_type=F32)
    st_ref[...] = st * gdec_ref[...] + upd * bd_ref[...]

    out = jnp.zeros(o.shape, F32)
    for h in range(RET_HEADS):
        cm = _group_mask(o.shape, 1, RET_KEY_DIM, h)
        ms = jnp.sum(jnp.where(cm, o * o, 0.0), axis=-1, keepdims=True) * (1.0 / RET_KEY_DIM)
        out = out + jnp.where(cm, o * lax.rsqrt(ms + EPS), 0.0)
    o_ref[...] = (_silu(g_ref[...]) * (out * gain_ref[...])).astype(BF16)


def _retention(pf, cos, sin, dmat, qdec, kdec, gdec, bd, gain, batch, seq):
    nb = seq // BLK
    row = lambda c: pl.BlockSpec((BLK, RET_WIDTH), lambda b, j: (b * nb + j, c))
    const2 = pl.BlockSpec((BLK, RET_WIDTH), lambda b, j: (0, 0))
    return pl.pallas_call(
        _ret_kernel,
        grid=(batch, nb),
        in_specs=[
            row(C_RQ // RET_WIDTH), row(C_RK // RET_WIDTH), row(C_RV // RET_WIDTH), row(C_RG // RET_WIDTH),
            pl.BlockSpec((BLK, RET_WIDTH), lambda b, j: (j, 0)),
            pl.BlockSpec((BLK, RET_WIDTH), lambda b, j: (j, 0)),
            pl.BlockSpec((RET_HEADS, BLK, BLK), lambda b, j: (0, 0, 0)),
            const2, const2, const2, const2,
            pl.BlockSpec((1, RET_WIDTH), lambda b, j: (0, 0)),
        ],
        out_specs=pl.BlockSpec((BLK, RET_WIDTH), lambda b, j: (b * nb + j, 0)),
        out_shape=jax.ShapeDtypeStruct((batch * seq, RET_WIDTH), BF16),
        scratch_shapes=[pltpu.VMEM((RET_WIDTH, RET_WIDTH), F32)],
        compiler_params=_cparams(("parallel", "arbitrary")),
        name="retention",
    )(pf, pf, pf, pf, cos, sin, dmat, qdec, kdec, gdec, bd, gain)


def _retention_tables(seq):
    half = RET_KEY_DIM // 2
    inv = 1.0 / (ROPE_BASE ** (jnp.arange(0, RET_KEY_DIM, 2, dtype=F32) / RET_KEY_DIM))
    ang = jnp.arange(seq, dtype=F32)[:, None] * inv[None, :]
    cos_h = jnp.concatenate([jnp.cos(ang), jnp.cos(ang)], axis=-1)
    sin_h = jnp.concatenate([-jnp.sin(ang), jnp.sin(ang)], axis=-1)
    cos = jnp.tile(cos_h, (1, RET_HEADS))
    sin = jnp.tile(sin_h, (1, RET_HEADS))
    log_gamma = jnp.log1p(-jnp.power(2.0, -5.0 - jnp.arange(RET_HEADS, dtype=F32)))
    idx = jnp.arange(BLK, dtype=F32)
    ii = jnp.arange(BLK)
    same_or_earlier_chunk = (ii[None, :] // CHUNK) <= (ii[:, None] // CHUNK)
    dmat = jnp.exp(log_gamma[:, None, None] * jnp.abs(idx[:, None] - idx[None, :]))
    dmat = jnp.where(same_or_earlier_chunk[None], dmat, 0.0)
    lg_cols = jnp.repeat(log_gamma, RET_KEY_DIM)
    qdec = jnp.exp(lg_cols[None, :] * (idx + 1.0)[:, None])
    kdec = jnp.exp(lg_cols[None, :] * (BLK - 1.0 - idx)[:, None])
    head_of = jnp.arange(RET_WIDTH) // RET_KEY_DIM
    bd = (head_of[:, None] == head_of[None, :]).astype(F32)
    gdec = jnp.exp(lg_cols * BLK)[:, None] * bd
    return cos, sin, dmat, qdec, kdec, gdec, bd


def _outproj_kernel(x_ref, a_ref, b_ref, c_ref, w_ref, g_ref, *rest, with_router):
    if with_router:
        rw_ref, xo_ref, h_ref, rinfo_ref = rest
    else:
        xo_ref, h_ref = rest
    acc = x_ref[...]
    acc = acc + jnp.dot(a_ref[...], w_ref[0:DA_WIDTH, :], preferred_element_type=F32)
    acc = acc + jnp.dot(b_ref[...], w_ref[DA_WIDTH:DA_WIDTH + SSM_WIDTH, :], preferred_element_type=F32)
    acc = acc + jnp.dot(c_ref[...], w_ref[DA_WIDTH + SSM_WIDTH:, :], preferred_element_type=F32)
    xo_ref[...] = acc
    ms = jnp.mean(acc * acc, axis=-1, keepdims=True)
    hf = acc * lax.rsqrt(ms + EPS) * g_ref[...]
    h = hf.astype(BF16)
    h_ref[...] = hf.astype(h_ref.dtype)
    if with_router:
        logits = jnp.dot(h, rw_ref[...], preferred_element_type=F32)
        lane = lax.broadcasted_iota(jnp.int32, logits.shape, 1).astype(F32)
        logits = jnp.where(lane < N_EXPERTS, logits, NEG_BIG)
        v1 = jnp.max(logits, axis=-1, keepdims=True)
        i1 = jnp.min(jnp.where(logits == v1, lane, float(LANES)), axis=-1, keepdims=True)
        rest_l = jnp.where(lane == i1, NEG_BIG, logits)
        v2 = jnp.max(rest_l, axis=-1, keepdims=True)
        i2 = jnp.min(jnp.where(rest_l == v2, lane, float(LANES)), axis=-1, keepdims=True)
        e2 = jnp.exp(v2 - v1)
        g1 = 1.0 / (1.0 + e2)
        g2 = e2 * g1
        rinfo_ref[...] = (jnp.where(lane == 0.0, g1, 0.0) + jnp.where(lane == 1.0, g2, 0.0)
                          + jnp.where(lane == 2.0, i1, 0.0) + jnp.where(lane == 3.0, i2, 0.0))


def _outproj(x2, oa, ob, oc, w, gain, router_w=None):
    t = x2.shape[0]
    with_router = router_w is not None
    tok = lambda w_: pl.BlockSpec((TM_PROJ, w_), lambda i: (i, 0))
    in_specs = [tok(D_MODEL), tok(DA_WIDTH), tok(SSM_WIDTH), tok(RET_WIDTH),
                pl.BlockSpec((D_MODEL, D_MODEL), lambda i: (0, 0)),
                pl.BlockSpec((1, D_MODEL), lambda i: (0, 0))]
    out_specs = [tok(D_MODEL), tok(D_MODEL)]
    h_dtype = F32 if with_router else BF16
    out_shape = [jax.ShapeDtypeStruct((t, D_MODEL), F32), jax.ShapeDtypeStruct((t, D_MODEL), h_dtype)]
    args = [x2, oa, ob, oc, w, gain]
    if with_router:
        in_specs.append(pl.BlockSpec((D_MODEL, LANES), lambda i: (0, 0)))
        out_specs.append(tok(LANES))
        out_shape.append(jax.ShapeDtypeStruct((t, LANES), F32))
        args.append(router_w)
    return pl.pallas_call(
        functools.partial(_outproj_kernel, with_router=with_router),
        grid=(t // TM_PROJ,),
        in_specs=in_specs,
        out_specs=out_specs,
        out_shape=out_shape,
        compiler_params=_cparams(("parallel",)),
        name="outproj_router" if with_router else "outproj",
    )(*args)


def _ffn_kernel(x_ref, h_ref, wg_ref, wu_ref, wd_ref, o_ref, acc_ref):
    f = pl.program_id(1)

    @pl.when(f == 0)
    def _init():
        acc_ref[...] = x_ref[...]

    h = h_ref[...]
    gate = jnp.dot(h, wg_ref[...], preferred_element_type=F32)
    up = jnp.dot(h, wu_ref[...], preferred_element_type=F32)
    act = (_silu(gate) * up).astype(BF16)
    acc_ref[...] += jnp.dot(act, wd_ref[...], preferred_element_type=F32)

    @pl.when(f == pl.num_programs(1) - 1)
    def _done():
        o_ref[...] = acc_ref[...]


def _dense_ffn(x2, h, wg, wu, wd):
    t = x2.shape[0]
    nf = wg.shape[1] // TF_FFN
    return pl.pallas_call(
        _ffn_kernel,
        grid=(t // TM_FFN, nf),
        in_specs=[
            pl.BlockSpec((TM_FFN, D_MODEL), lambda i, f: (i, 0)),
            pl.BlockSpec((TM_FFN, D_MODEL), lambda i, f: (i, 0)),
            pl.BlockSpec((D_MODEL, TF_FFN), lambda i, f: (0, f)),
            pl.BlockSpec((D_MODEL, TF_FFN), lambda i, f: (0, f)),
            pl.BlockSpec((TF_FFN, D_MODEL), lambda i, f: (f, 0)),
        ],
        out_specs=pl.BlockSpec((TM_FFN, D_MODEL), lambda i, f: (i, 0)),
        out_shape=jax.ShapeDtypeStruct((t, D_MODEL), F32),
        scratch_shapes=[pltpu.VMEM((TM_FFN, D_MODEL), F32)],
        compiler_params=_cparams(("parallel", "arbitrary")),
        name="dense_swiglu",
    )(x2, h, wg, wu, wd)


def _issue_row_gather(src_hbm, idx_ref, idx_base, dst, sem, n_rows):
    def body(r, carry):
        tok = idx_ref[idx_base + r]
        pltpu.make_async_copy(src_hbm.at[pl.ds(tok, 1)], dst.at[pl.ds(r, 1)], sem).start()
        return carry
    lax.fori_loop(0, n_rows, body, 0, unroll=8)


def _wait_row_gather(src_hbm, dst, sem, n_rows):
    pltpu.make_async_copy(src_hbm.at[pl.ds(0, n_rows)], dst, sem).wait()


def _moe_ffn_kernel(te_ref, nv_ref, rt_ref, h_hbm, wg_ref, wu_ref, wd_ref, y_ref, hbuf, xb_ref, acc_ref, sem):
    i = pl.program_id(0)
    f = pl.program_id(1)
    n_valid = nv_ref[0]
    valid = i < n_valid
    slot = i % 2

    @pl.when((f == 0) & (i == 0) & valid)
    def _first():
        _issue_row_gather(h_hbm, rt_ref, 0, hbuf.at[0], sem.at[0], TR_MOE)

    @pl.when((f == 0) & valid)
    def _arrive():
        _wait_row_gather(h_hbm, hbuf.at[slot], sem.at[slot], TR_MOE)
        xb_ref[...] = hbuf[slot].astype(BF16)

    def compute(prefetch_next):
        if prefetch_next:
            rows_per_step = TR_MOE // (FFN_EXPERT // TF_MOE)
            r0 = f * rows_per_step
            for j in range(rows_per_step):
                tok = rt_ref[(i + 1) * TR_MOE + r0 + j]
                pltpu.make_async_copy(h_hbm.at[pl.ds(tok, 1)], hbuf.at[1 - slot, pl.ds(r0 + j, 1)],
                                      sem.at[1 - slot]).start()
        xb = xb_ref[...]
        gate = jnp.dot(xb, wg_ref[0], preferred_element_type=F32)
        up = jnp.dot(xb, wu_ref[0], preferred_element_type=F32)
        act = (_silu(gate) * up).astype(BF16)
        contrib = jnp.dot(act, wd_ref[0], preferred_element_type=F32)

        @pl.when(f == 0)
        def _():
            acc_ref[...] = contrib

        @pl.when(f > 0)
        def _():
            acc_ref[...] += contrib

    has_next = i + 1 < n_valid

    @pl.when(valid & has_next)
    def _compute_and_prefetch():
        compute(True)

    @pl.when(valid & jnp.logical_not(has_next))
    def _compute_last():
        compute(False)

    @pl.when(f == pl.num_programs(1) - 1)
    def _store():
        y_ref[...] = jnp.where(valid, acc_ref[...], 0.0)


def _moe_ffn(h, tile_expert, n_valid, row_token, wg, wu, wd):
    n_tiles = tile_expert.shape[0]
    nf = wg.shape[2] // TF_MOE

    def w_in_map(i, f, te, nv, rt):
        return (te[i], 0, jnp.where(i < nv[0], f, nf - 1))

    def w_out_map(i, f, te, nv, rt):
        return (te[i], jnp.where(i < nv[0], f, nf - 1), 0)

    grid_spec = pltpu.PrefetchScalarGridSpec(
        num_scalar_prefetch=3,
        grid=(n_tiles, nf),
        in_specs=[
            pl.BlockSpec(memory_space=pl.ANY),
            pl.BlockSpec((1, D_MODEL, TF_MOE), w_in_map),
            pl.BlockSpec((1, D_MODEL, TF_MOE), w_in_map),
            pl.BlockSpec((1, TF_MOE, D_MODEL), w_out_map),
        ],
        out_specs=pl.BlockSpec((TR_MOE, D_MODEL), lambda i, f, te, nv, rt: (i, 0)),
        scratch_shapes=[
            pltpu.VMEM((2, TR_MOE, D_MODEL), F32),
            pltpu.VMEM((TR_MOE, D_MODEL), BF16),
            pltpu.VMEM((TR_MOE, D_MODEL), F32),
            pltpu.SemaphoreType.DMA((2,)),
        ],
    )
    return pl.pallas_call(
        _moe_ffn_kernel,
        grid_spec=grid_spec,
        out_shape=jax.ShapeDtypeStruct((n_tiles * TR_MOE, D_MODEL), F32),
        compiler_params=_cparams(("arbitrary", "arbitrary")),
        name="moe_expert_ffn",
    )(tile_expert, n_valid, row_token, h, wg, wu, wd)


def _moe_combine_kernel(tr_ref, x_ref, rinfo_ref, y_hbm, fg_ref, o_ref, ybuf, sem):
    i = pl.program_id(0)
    slot = i % 2
    n_rows = TOP_K * TM_COMB

    @pl.when(i == 0)
    def _first():
        _issue_row_gather(y_hbm, tr_ref, 0, ybuf.at[0], sem.at[0], n_rows)

    _wait_row_gather(y_hbm, ybuf.at[slot], sem.at[slot], n_rows)

    @pl.when(i + 1 < pl.num_programs(0))
    def _prefetch():
        _issue_row_gather(y_hbm, tr_ref, (i + 1) * n_rows, ybuf.at[1 - slot], sem.at[1 - slot], n_rows)

    rinfo = rinfo_ref[...]
    g1 = jnp.broadcast_to(rinfo[:, 0:1], (TM_COMB, D_MODEL))
    g2 = jnp.broadcast_to(rinfo[:, 1:2], (TM_COMB, D_MODEL))
    y = x_ref[...] + g1 * ybuf[slot, 0:TM_COMB, :] + g2 * ybuf[slot, TM_COMB:n_rows, :]
    ms = jnp.mean(y * y, axis=-1, keepdims=True)
    o_ref[...] = y * lax.rsqrt(ms + EPS) * fg_ref[...]


def _moe_combine(x2, rinfo, y_rows, tok_rows, final_gain):
    t = x2.shape[0]
    grid_spec = pltpu.PrefetchScalarGridSpec(
        num_scalar_prefetch=1,
        grid=(t // TM_COMB,),
        in_specs=[
            pl.BlockSpec((TM_COMB, D_MODEL), lambda i, tr: (i, 0)),
            pl.BlockSpec((TM_COMB, LANES), lambda i, tr: (i, 0)),
            pl.BlockSpec(memory_space=pl.ANY),
            pl.BlockSpec((1, D_MODEL), lambda i, tr: (0, 0)),
        ],
        out_specs=pl.BlockSpec((TM_COMB, D_MODEL), lambda i, tr: (i, 0)),
        scratch_shapes=[
            pltpu.VMEM((2, TOP_K * TM_COMB, D_MODEL), F32),
            pltpu.SemaphoreType.DMA((2,)),
        ],
    )
    return pl.pallas_call(
        _moe_combine_kernel,
        grid_spec=grid_spec,
        out_shape=jax.ShapeDtypeStruct((t, D_MODEL), F32),
        compiler_params=_cparams(("arbitrary",)),
        name="moe_combine_norm",
    )(tok_rows, x2, rinfo, y_rows, final_gain)


def _routing_tables(rinfo):
    t = rinfo.shape[0]
    n_tiles = -(-(TOP_K * t) // TR_MOE) + N_EXPERTS
    flat_e = rinfo[:, 2:2 + TOP_K].astype(jnp.int32).reshape(-1)
    onehot = (flat_e[:, None] == jnp.arange(N_EXPERTS)[None, :]).astype(jnp.int32)
    csum = jnp.cumsum(onehot, axis=0)
    rank = jnp.sum((csum - onehot) * onehot, axis=1)
    counts = csum[-1]
    tiles_per_e = (counts + TR_MOE - 1) // TR_MOE
    tile_end = jnp.cumsum(tiles_per_e)
    row_off = (tile_end - tiles_per_e) * TR_MOE
    dest = row_off[flat_e] + rank
    token = jnp.arange(TOP_K * t, dtype=jnp.int32) // TOP_K
    row_token = jnp.zeros((n_tiles * TR_MOE,), jnp.int32).at[dest].set(token, unique_indices=True)
    n_valid = tile_end[-1:].astype(jnp.int32)
    tile_expert = jnp.searchsorted(tile_end, jnp.arange(n_tiles), side="right").astype(jnp.int32)
    last_e = jnp.searchsorted(tile_end, n_valid[0] - 1, side="right").astype(jnp.int32)
    tile_expert = jnp.where(jnp.arange(n_tiles) < n_valid[0], tile_expert, last_e)
    tok_rows = dest.reshape(t // TM_COMB, TM_COMB, TOP_K).transpose(0, 2, 1).reshape(-1).astype(jnp.int32)
    return tile_expert, n_valid, row_token, tok_rows


def _permute_w_in(w):
    sizes = (256, 256, 256, 512, 768, 8, 256, 256, 256, 256)
    offs = np.concatenate([[0], np.cumsum(sizes)])
    part = lambda n: w[:, offs[n]:offs[n + 1]]
    dt_pad = jnp.zeros((w.shape[0], LANES - SSM_HEADS), w.dtype)
    cols = [part(0) * (DA_HEAD_DIM ** -0.5 * LOG2E), part(1), part(2), part(3), part(4),
            part(6), part(7), part(8), part(9), part(5), dt_pad]
    return jnp.concatenate(cols, axis=1).astype(BF16)


def _pad_lanes(v, fill=0.0):
    return jnp.concatenate([v, jnp.full((LANES - v.shape[0],), fill, v.dtype)])[None, :]


def kernel(x, w_in, w_out, attn_norm, ffn_norm, final_norm, rel_bias, lambda_q1, lambda_k1, lambda_q2,
           lambda_k2, da_head_norm, conv_w, conv_b, dt_bias, a_log, d_skip, ssm_norm, ret_head_norm,
           w_gate, w_up, w_down, router_w, e_gate, e_up, e_down):
    batch, seq, _ = x.shape
    x2 = x.reshape(batch * seq, D_MODEL)
    bias_tiles = _bias_tiles(rel_bias)
    ret_tabs = _retention_tables(seq)

    for layer in range(DEPTH):
        lam_init = 0.8 - 0.6 * math.exp(-0.3 * layer)
        lam = (jnp.exp(jnp.sum(lambda_q1[layer] * lambda_k1[layer]))
               - jnp.exp(jnp.sum(lambda_q2[layer] * lambda_k2[layer])) + lam_init).astype(F32)
        lam = jnp.full((1, LANES), lam, F32)

        qkv, kt, pf = _inproj(x2, attn_norm[layer][None, :], _permute_w_in(w_in[layer]))

        da_gain = (jnp.tile(da_head_norm[layer], DA_HEADS) * (1.0 - lam_init))[None, :]
        out_a = _attention(qkv, kt, bias_tiles, da_gain, lam, batch, seq)

        cw, cb = conv_w[layer], conv_b[layer]
        out_b = _ssd(pf, cw[:, :SSM_WIDTH], cw[:, SSM_WIDTH:], cb[None, :SSM_WIDTH], cb[None, SSM_WIDTH:],
                     _pad_lanes(dt_bias[layer]), _pad_lanes(a_log[layer]),
                     jnp.repeat(d_skip[layer], SSM_HEAD_DIM)[None, :], ssm_norm[layer][None, :], batch, seq)

        ret_gain = jnp.tile(ret_head_norm[layer], RET_HEADS)[None, :]
        out_c = _retention(pf, *ret_tabs, ret_gain, batch, seq)

        w_o = w_out[layer].astype(BF16)
        if layer % 2 == 0:
            i = layer // 2
            x2, h = _outproj(x2, out_a, out_b, out_c, w_o, ffn_norm[layer][None, :])
            pad = FFN_PAD - FFN_DENSE
            wg = jnp.pad(w_gate[i], ((0, 0), (0, pad))).astype(BF16)
            wu = jnp.pad(w_up[i], ((0, 0), (0, pad))).astype(BF16)
            wd = jnp.pad(w_down[i], ((0, pad), (0, 0))).astype(BF16)
            x2 = _dense_ffn(x2, h, wg, wu, wd)
        else:
            i = layer // 2
            rw = jnp.pad(router_w[i], ((0, 0), (0, LANES - N_EXPERTS))).astype(BF16)
            x2, h, rinfo = _outproj(x2, out_a, out_b, out_c, w_o, ffn_norm[layer][None, :], rw)
            tile_expert, n_valid, row_token, tok_rows = _routing_tables(rinfo)
            y_rows = _moe_ffn(h, tile_expert, n_valid, row_token,
                              e_gate[i].astype(BF16), e_up[i].astype(BF16), e_down[i].astype(BF16))
            x2 = _moe_combine(x2, rinfo, y_rows, tok_rows, final_norm[None, :])
    return x2.reshape(batch, seq, D_MODEL)
```

```python
import functools
import math

import jax
import jax.numpy as jnp
import numpy as np
from jax import lax
from jax.experimental import pallas as pl
from jax.experimental.pallas import tpu as pltpu

F32 = jnp.float32
BF16 = jnp.bfloat16

D_MODEL = 1024
DEPTH = 2
CHUNK = 64
EPS = 1e-6
DA_HEADS = 4
DA_HEAD_DIM = 32
DA_V_DIM = 64
DA_WIDTH = 256
SSM_HEADS = 8
SSM_HEAD_DIM = 64
SSM_WIDTH = 512
SSM_STATE = 64
SSM_GROUPS = 2
SSM_CONV = 4
RET_HEADS = 4
RET_KEY_DIM = 64
RET_WIDTH = 256
ROPE_BASE = 10000.0
REL_BUCKETS = 32
REL_MAX_DIST = 128
FFN_DENSE = 2752
N_EXPERTS = 8
FFN_EXPERT = 3584

LANES = 128
VMEM_LIMIT = 48 * 1024 * 1024
NEG_BIG = -1e30

TM_PROJ = 512
ATT_T = 512
LOG2E = math.log2(math.e)
BLK = 256
FFN_PAD = 2816
TM_FFN = 1024
TF_FFN = 256
TF_MOE = 512
TR_MOE = 896
TM_COMB = 256
TOP_K = 2

C_Z, C_XS, C_BC, C_RQ, C_RK, C_RV, C_RG, C_DT = 0, 512, 1024, 1280, 1536, 1792, 2048, 2304
PF_W = 2432
PF_CHUNK = 640
QKV_W = 768


def _cparams(sem):
    return pltpu.CompilerParams(dimension_semantics=sem, vmem_limit_bytes=VMEM_LIMIT)


def _group_mask(shape, axis, group, idx):
    io = lax.broadcasted_iota(jnp.int32, shape, axis)
    return (io >= idx * group) & (io < (idx + 1) * group)


def _lane_tile(x, n, axis=1):
    return jnp.concatenate([x] * n, axis=axis)


def _silu(x):
    return x * (1.0 / (1.0 + jnp.exp(-x)))


def _inproj_kernel(x_ref, g_ref, w_ref, qkv_ref, qvt_ref, pf_ref):
    x = x_ref[...]
    ms = jnp.mean(x * x, axis=-1, keepdims=True)
    h = (x * lax.rsqrt(ms + EPS) * g_ref[...]).astype(BF16)
    qkv = jnp.dot(h, w_ref[:, :QKV_W], preferred_element_type=F32)
    qkv_ref[...] = qkv.astype(BF16)
    qvt_ref[...] = qkv[:, :2 * DA_WIDTH].T.astype(BF16)
    for lo in range(0, PF_W, PF_CHUNK):
        hi = min(lo + PF_CHUNK, PF_W)
        pf_ref[:, lo:hi] = jnp.dot(h, w_ref[:, QKV_W + lo:QKV_W + hi], preferred_element_type=F32)


def _inproj(x2, gain, w):
    t = x2.shape[0]
    return pl.pallas_call(
        _inproj_kernel,
        grid=(t // TM_PROJ,),
        in_specs=[
            pl.BlockSpec((TM_PROJ, D_MODEL), lambda i: (i, 0)),
            pl.BlockSpec((1, D_MODEL), lambda i: (0, 0)),
            pl.BlockSpec((D_MODEL, QKV_W + PF_W), lambda i: (0, 0)),
        ],
        out_specs=[
            pl.BlockSpec((TM_PROJ, QKV_W), lambda i: (i, 0)),
            pl.BlockSpec((2 * DA_WIDTH, TM_PROJ), lambda i: (0, i)),
            pl.BlockSpec((TM_PROJ, PF_W), lambda i: (i, 0)),
        ],
        out_shape=[
            jax.ShapeDtypeStruct((t, QKV_W), BF16),
            jax.ShapeDtypeStruct((2 * DA_WIDTH, t), BF16),
            jax.ShapeDtypeStruct((t, PF_W), F32),
        ],
        compiler_params=_cparams(("parallel",)),
        name="inproj",
    )(x2, gain, w)


def _attn_kernel(qi_ref, kj_ref, qt_ref, k_ref, vt_ref, bias_ref, gain_ref, lam_ref, o_ref,
                 qm_ref, m_ref, l_ref, acc_ref, s_ref, e_ref, al_ref):
    p = pl.program_id(1)
    qi = qi_ref[p]
    kj = kj_ref[p]
    n_ht = 2 * DA_HEADS

    @pl.when(kj == 0)
    def _init():
        qt = qt_ref[...]
        for ht in range(n_ht):
            qm_ref[ht] = jnp.where(_group_mask(qt.shape, 0, DA_HEAD_DIM, ht), qt, jnp.zeros_like(qt))
        m_ref[...] = jnp.full(m_ref.shape, NEG_BIG, F32)
        l_ref[...] = jnp.zeros(l_ref.shape, F32)
        acc_ref[...] = jnp.zeros(acc_ref.shape, F32)

    def step(with_bias):
        k = k_ref[...]
        def scores(a, buf):
            s_ref[buf] = jnp.dot(k, qm_ref[a], preferred_element_type=F32)

        def softmax(b, buf):
            for c0 in range(0, ATT_T, LANES):
                cols = slice(c0, c0 + LANES)

                def block():
                    blk = s_ref[buf, :, cols]
                    return blk + bias_ref[0, b // 2, :, cols] if with_bias else blk

                m_prev = m_ref[b, :, cols]
                m_new = jnp.maximum(m_prev, jnp.max(block(), axis=0, keepdims=True))
                alpha = jnp.exp2(m_prev - m_new)
                e = jnp.exp2(block() - m_new)
                l_ref[b, :, cols] = alpha * l_ref[b, :, cols] + jnp.sum(e, axis=0, keepdims=True)
                m_ref[b, :, cols] = m_new
                al_ref[buf, :, cols] = alpha
                e_ref[buf, :, cols] = e.astype(BF16)

        def values(c, buf):
            r0 = (c // 4) * LANES
            if not isinstance(r0, int):
                r0 = pl.multiple_of(r0, LANES)
            pv = jnp.dot(vt_ref[pl.ds(r0, LANES), :], e_ref[buf], preferred_element_type=F32)
            acc_ref[c] = acc_ref[c] * al_ref[buf] + pv

        scores(0, 0)
        scores(1, 1)
        softmax(0, 0)

        def trip(u, carry):
            t = 2 * u + 1
            scores(t + 1, 0)
            softmax(t, 1)
            values(t - 1, 0)
            scores(t + 2, 1)
            softmax(t + 1, 0)
            values(t, 1)
            return carry

        for u in range((n_ht - 2) // 2):
            trip(u, 0)
        softmax(n_ht - 1, 1)
        values(n_ht - 2, 0)
        values(n_ht - 1, 1)

    @pl.when(kj >= qi - 1)
    def _near():
        step(True)

    @pl.when(kj < qi - 1)
    def _far():
        step(False)

    @pl.when(kj == qi)
    def _finish():
        lam = lam_ref[...]
        row_lo = lax.broadcasted_iota(jnp.int32, (LANES, ATT_T), 0) < DA_V_DIM
        for pair in range(DA_HEADS // 2):
            halves = []
            for sub in range(2):
                h = 2 * pair + sub
                o = acc_ref[2 * h] * (1.0 / l_ref[2 * h]) - lam * (acc_ref[2 * h + 1] * (1.0 / l_ref[2 * h + 1]))
                own = row_lo if sub == 0 else jnp.logical_not(row_lo)
                ms = jnp.sum(jnp.where(own, o * o, 0.0), axis=0, keepdims=True) * (1.0 / DA_V_DIM)
                halves.append(o * lax.rsqrt(ms + EPS))
            r0 = pair * LANES
            blk = jnp.where(row_lo, halves[0], halves[1]) * gain_ref[r0:r0 + LANES, :]
            o_ref[:, r0:r0 + LANES] = blk.T.astype(BF16)


def _attention(qkv, qvt, bias_tiles, gain, lam, batch, seq):
    nq = seq // ATT_T
    qi_l, kj_l = [], []
    for qi in range(nq):
        for kj in range(qi + 1):
            qi_l.append(qi)
            kj_l.append(kj)
    qi_tbl = jnp.asarray(np.array(qi_l, np.int32))
    kj_tbl = jnp.asarray(np.array(kj_l, np.int32))
    n_pairs = len(qi_l)
    n_ht = 2 * DA_HEADS

    grid_spec = pltpu.PrefetchScalarGridSpec(
        num_scalar_prefetch=2,
        grid=(batch, n_pairs),
        in_specs=[
            pl.BlockSpec((DA_WIDTH, ATT_T), lambda b, p, qi, kj: (0, b * nq + qi[p])),
            pl.BlockSpec((ATT_T, DA_WIDTH), lambda b, p, qi, kj: (b * nq + kj[p], 2)),
            pl.BlockSpec((DA_WIDTH, ATT_T), lambda b, p, qi, kj: (1, b * nq + kj[p])),
            pl.BlockSpec((1, DA_HEADS, ATT_T, ATT_T),
                         lambda b, p, qi, kj: (jnp.minimum(qi[p] - kj[p], 1), 0, 0, 0)),
            pl.BlockSpec((DA_WIDTH, 1), lambda b, p, qi, kj: (0, 0)),
            pl.BlockSpec((1, ATT_T), lambda b, p, qi, kj: (0, 0)),
        ],
        out_specs=pl.BlockSpec((ATT_T, DA_WIDTH), lambda b, p, qi, kj: (b * nq + qi[p], 0)),
        scratch_shapes=[
            pltpu.VMEM((n_ht, DA_WIDTH, ATT_T), BF16),
            pltpu.VMEM((n_ht, 1, ATT_T), F32),
            pltpu.VMEM((n_ht, 1, ATT_T), F32),
            pltpu.VMEM((n_ht, LANES, ATT_T), F32),
            pltpu.VMEM((2, ATT_T, ATT_T), F32),
            pltpu.VMEM((2, ATT_T, ATT_T), BF16),
            pltpu.VMEM((2, 1, ATT_T), F32),
        ],
    )
    return pl.pallas_call(
        _attn_kernel,
        grid_spec=grid_spec,
        out_shape=jax.ShapeDtypeStruct((batch * seq, DA_WIDTH), BF16),
        compiler_params=_cparams(("parallel", "arbitrary")),
        name="diff_attention",
    )(qi_tbl, kj_tbl, qvt, qkv, qvt, bias_tiles, gain, lam)


def _t5_bucket(rel):
    half = REL_BUCKETS // 2
    max_exact = half // 2
    ret = jnp.where(rel > 0, half, 0)
    n = jnp.abs(rel)
    nf = jnp.maximum(n, 1).astype(F32)
    large = max_exact + (jnp.log(nf / max_exact) / math.log(REL_MAX_DIST / max_exact)
                         * (half - max_exact)).astype(jnp.int32)
    large = jnp.minimum(large, half - 1)
    return ret + jnp.where(n < max_exact, n, large)


def _bias_tiles(rel_bias):
    i = jnp.arange(ATT_T)[None, :]
    j = jnp.arange(ATT_T)[:, None]
    far = rel_bias[_t5_bucket(jnp.int32(-2 * ATT_T))].astype(F32)
    tiles = []
    for off in range(2):
        bucket = _t5_bucket((j - off * ATT_T) - i)
        b = jnp.zeros((DA_HEADS, ATT_T, ATT_T), F32)
        for c in range(REL_BUCKETS):
            b = jnp.where((bucket == c)[None], rel_bias[c].astype(F32)[:, None, None], b)
        b = (b - far[:, None, None]) * LOG2E
        if off == 0:
            allowed = (j // CHUNK) <= (i // CHUNK)
            b = jnp.where(allowed[None], b, NEG_BIG)
        tiles.append(b)
    return jnp.stack(tiles)


def _cumsum_rows(x):
    n = x.shape[0]
    row = lax.broadcasted_iota(jnp.int32, x.shape, 0)
    sh = 1
    while sh < n:
        x = x + jnp.where(row >= sh, pltpu.roll(x, sh, axis=0), 0.0)
        sh *= 2
    return x


def _expand_heads(v, width, group):
    out = jnp.zeros((v.shape[0], width), F32)
    for h in range(width // group):
        col = jnp.broadcast_to(v[:, h:h + 1], (v.shape[0], width))
        out = jnp.where(_group_mask(out.shape, 1, group, h), col, out)
    return out


def _ssd_kernel(z_ref, xs_ref, bc_ref, dt_ref, cwx_ref, cwb_ref, cbx_ref, cbb_ref, dtb_ref, alog_ref,
                dskip_ref, ng_ref, o_ref, extx_ref, extb_ref, st_ref):
    j = pl.program_id(1)

    @pl.when(j == 0)
    def _reset():
        extx_ref[0:8, :] = jnp.zeros((8, SSM_WIDTH), F32)
        extb_ref[0:8, :] = jnp.zeros((8, 2 * LANES), F32)
        st_ref[...] = jnp.zeros(st_ref.shape, F32)

    extx_ref[8:8 + BLK, :] = xs_ref[...]
    extb_ref[8:8 + BLK, :] = bc_ref[...]
    yx = jnp.broadcast_to(cbx_ref[...], (BLK, SSM_WIDTH))
    yb = jnp.broadcast_to(cbb_ref[...], (BLK, 2 * LANES))
    for w in range(SSM_CONV):
        lo = 8 - (SSM_CONV - 1) + w
        yx = yx + extx_ref[lo:lo + BLK, :] * cwx_ref[w:w + 1, :]
        yb = yb + extb_ref[lo:lo + BLK, :] * cwb_ref[w:w + 1, :]
    tail_x = extx_ref[BLK:BLK + 8, :]
    tail_b = extb_ref[BLK:BLK + 8, :]
    extx_ref[0:8, :] = tail_x
    extb_ref[0:8, :] = tail_b
    xs = _silu(yx)
    bc = _silu(yb)
    bm = bc[:, :LANES]
    cm = bc[:, LANES:]

    dt = jax.nn.softplus(dt_ref[...] + dtb_ref[...])
    a = -jnp.exp(alog_ref[...]) * dt
    a_cs = _cumsum_rows(a)
    a_cs_t = a_cs.T
    a_last = a_cs[BLK - 1:BLK, :]

    dt_full = _expand_heads(dt, SSM_WIDTH, SSM_HEAD_DIM)
    ea_full = _expand_heads(jnp.exp(a_cs), SSM_WIDTH, SSM_HEAD_DIM)
    dec_full = _expand_heads(jnp.exp(a_last - a_cs), SSM_WIDTH, SSM_HEAD_DIM)
    cdec_full = _expand_heads(jnp.exp(a_last), SSM_WIDTH, SSM_HEAD_DIM)

    xdt = xs * dt_full
    xdt_b = xdt.astype(BF16)
    cm_b = cm.astype(BF16)
    bm_b = bm.astype(BF16)

    st = st_ref[...]
    y = jnp.dot(cm_b, st.astype(BF16), preferred_element_type=F32) * ea_full + xs * dskip_ref[...]

    row = lax.broadcasted_iota(jnp.int32, (BLK, BLK), 0)
    colio = lax.broadcasted_iota(jnp.int32, (BLK, BLK), 1)
    causal = colio <= row
    rep = SSM_HEADS // SSM_GROUPS
    lane_lo = lax.broadcasted_iota(jnp.int32, (BLK, LANES), 1) < SSM_HEAD_DIM
    pieces = []
    for g in range(SSM_GROUPS):
        cg = jnp.where(_group_mask(cm_b.shape, 1, SSM_STATE, g), cm_b, jnp.zeros_like(cm_b))
        cb = lax.dot_general(cg, bm_b, (((1,), (1,)), ((), ())), preferred_element_type=F32)
        for pair in range(rep // 2):
            halves = []
            for sub in range(2):
                h = g * rep + pair * 2 + sub
                seg = jnp.broadcast_to(a_cs[:, h:h + 1], (BLK, BLK)) - a_cs_t[h:h + 1, :]
                lmat = jnp.exp(jnp.where(causal, seg, NEG_BIG))
                mh = (cb * lmat).astype(BF16)
                c0 = (h // 2) * LANES
                halves.append(jnp.dot(mh, xdt_b[:, c0:c0 + LANES], preferred_element_type=F32))
            pieces.append(jnp.where(lane_lo, halves[0], halves[1]))
    y = y + jnp.concatenate(pieces, axis=1)

    upd = jnp.dot(bm.T.astype(BF16), (xdt * dec_full).astype(BF16), preferred_element_type=F32)
    r_io = lax.broadcasted_iota(jnp.int32, upd.shape, 0) // SSM_STATE
    c_io = lax.broadcasted_iota(jnp.int32, upd.shape, 1) // (SSM_HEAD_DIM * rep)
    st_ref[...] = st * cdec_full + jnp.where(r_io == c_io, upd, 0.0)

    y = y * _silu(z_ref[...])
    gw = SSM_WIDTH // SSM_GROUPS
    for g in range(SSM_GROUPS):
        yg = y[:, g * gw:(g + 1) * gw]
        ms = jnp.mean(yg * yg, axis=-1, keepdims=True)
        o_ref[:, g * gw:(g + 1) * gw] = (yg * lax.rsqrt(ms + EPS) * ng_ref[:, g * gw:(g + 1) * gw]).astype(BF16)


def _ssd(pf, cwx, cwb, cbx, cbb, dtb, alog, dskip, ng, batch, seq):
    nb = seq // BLK
    row = lambda w, c: pl.BlockSpec((BLK, w), lambda b, j: (b * nb + j, c))
    const = lambda r, w: pl.BlockSpec((r, w), lambda b, j: (0, 0))
    return pl.pallas_call(
        _ssd_kernel,
        grid=(batch, nb),
        in_specs=[
            row(SSM_WIDTH, C_Z // SSM_WIDTH), row(SSM_WIDTH, C_XS // SSM_WIDTH), row(2 * LANES, C_BC // (2 * LANES)),
            row(LANES, C_DT // LANES),
            const(SSM_CONV, SSM_WIDTH), const(SSM_CONV, 2 * LANES), const(1, SSM_WIDTH), const(1, 2 * LANES),
            const(1, LANES), const(1, LANES), const(1, SSM_WIDTH), const(1, SSM_WIDTH),
        ],
        out_specs=pl.BlockSpec((BLK, SSM_WIDTH), lambda b, j: (b * nb + j, 0)),
        out_shape=jax.ShapeDtypeStruct((batch * seq, SSM_WIDTH), BF16),
        scratch_shapes=[
            pltpu.VMEM((BLK + 8, SSM_WIDTH), F32),
            pltpu.VMEM((BLK + 8, 2 * LANES), F32),
            pltpu.VMEM((SSM_GROUPS * SSM_STATE, SSM_WIDTH), F32),
        ],
        compiler_params=_cparams(("parallel", "arbitrary")),
        name="ssd_mixer",
    )(pf, pf, pf, pf, cwx, cwb, cbx, cbb, dtb, alog, dskip, ng)


def _rotary(u, cos, sin_signed):
    first = (lax.broadcasted_iota(jnp.int32, u.shape, 1) % RET_KEY_DIM) < (RET_KEY_DIM // 2)
    half = RET_KEY_DIM // 2
    swapped = jnp.where(first, pltpu.roll(u, u.shape[1] - half, axis=1), pltpu.roll(u, half, axis=1))
    return u * cos + swapped * sin_signed


def _ret_kernel(q_ref, k_ref, v_ref, g_ref, cos_ref, sin_ref, dmat_ref, qdec_ref, kdec_ref, gdec_ref,
                bd_ref, gain_ref, o_ref, st_ref):
    j = pl.program_id(1)

    @pl.when(j == 0)
    def _reset():
        st_ref[...] = jnp.zeros(st_ref.shape, F32)

    cos = cos_ref[...]
    sin = sin_ref[...]
    q = _rotary(q_ref[...], cos, sin)
    k = _rotary(k_ref[...], cos, sin) * (RET_KEY_DIM ** -0.5)
    v_b = v_ref[...].astype(BF16)
    q_b = q.astype(BF16)
    k_b = k.astype(BF16)

    st = st_ref[...]
    o = jnp.dot((q * qdec_ref[...]).astype(BF16), st.astype(BF16), preferred_element_type=F32)
    for h in range(RET_HEADS):
        cm = _group_mask(q_b.shape, 1, RET_KEY_DIM, h)
        qh = jnp.where(cm, q_b, jnp.zeros_like(q_b))
        s = lax.dot_general(qh, k_b, (((1,), (1,)), ((), ())), preferred_element_type=F32) * dmat_ref[h]
        oh = jnp.dot(s.astype(BF16), v_b, preferred_element_type=F32)
        o = o + jnp.where(cm, oh, 0.0)

    upd = jnp.dot((k * kdec_ref[...]).T.astype(BF16), v_b, preferred_element_type=F32)
    st_ref[...] = st * gdec_ref[...] + upd * bd_ref[...]

    out = jnp.zeros(o.shape, F32)
    for h in range(RET_HEADS):
        cm = _group_mask(o.shape, 1, RET_KEY_DIM, h)
        ms = jnp.sum(jnp.where(cm, o * o, 0.0), axis=-1, keepdims=True) * (1.0 / RET_KEY_DIM)
        out = out + jnp.where(cm, o * lax.rsqrt(ms + EPS), 0.0)
    o_ref[...] = (_silu(g_ref[...]) * (out * gain_ref[...])).astype(BF16)


def _retention(pf, cos, sin, dmat, qdec, kdec, gdec, bd, gain, batch, seq):
    nb = seq // BLK
    row = lambda c: pl.BlockSpec((BLK, RET_WIDTH), lambda b, j: (b * nb + j, c))
    const2 = pl.BlockSpec((BLK, RET_WIDTH), lambda b, j: (0, 0))
    return pl.pallas_call(
        _ret_kernel,
        grid=(batch, nb),
        in_specs=[
            row(C_RQ // RET_WIDTH), row(C_RK // RET_WIDTH), row(C_RV // RET_WIDTH), row(C_RG // RET_WIDTH),
            pl.BlockSpec((BLK, RET_WIDTH), lambda b, j: (j, 0)),
            pl.BlockSpec((BLK, RET_WIDTH), lambda b, j: (j, 0)),
            pl.BlockSpec((RET_HEADS, BLK, BLK), lambda b, j: (0, 0, 0)),
            const2, const2, const2, const2,
            pl.BlockSpec((1, RET_WIDTH), lambda b, j: (0, 0)),
        ],
        out_specs=pl.BlockSpec((BLK, RET_WIDTH), lambda b, j: (b * nb + j, 0)),
        out_shape=jax.ShapeDtypeStruct((batch * seq, RET_WIDTH), BF16),
        scratch_shapes=[pltpu.VMEM((RET_WIDTH, RET_WIDTH), F32)],
        compiler_params=_cparams(("parallel", "arbitrary")),
        name="retention",
    )(pf, pf, pf, pf, cos, sin, dmat, qdec, kdec, gdec, bd, gain)


def _retention_tables(seq):
    half = RET_KEY_DIM // 2
    inv = 1.0 / (ROPE_BASE ** (jnp.arange(0, RET_KEY_DIM, 2, dtype=F32) / RET_KEY_DIM))
    ang = jnp.arange(seq, dtype=F32)[:, None] * inv[None, :]
    cos_h = jnp.concatenate([jnp.cos(ang), jnp.cos(ang)], axis=-1)
    sin_h = jnp.concatenate([-jnp.sin(ang), jnp.sin(ang)], axis=-1)
    cos = jnp.tile(cos_h, (1, RET_HEADS))
    sin = jnp.tile(sin_h, (1, RET_HEADS))
    log_gamma = jnp.log1p(-jnp.power(2.0, -5.0 - jnp.arange(RET_HEADS, dtype=F32)))
    idx = jnp.arange(BLK, dtype=F32)
    ii = jnp.arange(BLK)
    same_or_earlier_chunk = (ii[None, :] // CHUNK) <= (ii[:, None] // CHUNK)
    dmat = jnp.exp(log_gamma[:, None, None] * jnp.abs(idx[:, None] - idx[None, :]))
    dmat = jnp.where(same_or_earlier_chunk[None], dmat, 0.0)
    lg_cols = jnp.repeat(log_gamma, RET_KEY_DIM)
    qdec = jnp.exp(lg_cols[None, :] * (idx + 1.0)[:, None])
    kdec = jnp.exp(lg_cols[None, :] * (BLK - 1.0 - idx)[:, None])
    head_of = jnp.arange(RET_WIDTH) // RET_KEY_DIM
    bd = (head_of[:, None] == head_of[None, :]).astype(F32)
    gdec = jnp.exp(lg_cols * BLK)[:, None] * bd
    return cos, sin, dmat, qdec, kdec, gdec, bd


def _outproj_kernel(x_ref, a_ref, b_ref, c_ref, w_ref, g_ref, *rest, with_router):
    if with_router:
        rw_ref, xo_ref, h_ref, rinfo_ref = rest
    else:
        xo_ref, h_ref = rest
    acc = x_ref[...]
    acc = acc + jnp.dot(a_ref[...], w_ref[0:DA_WIDTH, :], preferred_element_type=F32)
    acc = acc + jnp.dot(b_ref[...], w_ref[DA_WIDTH:DA_WIDTH + SSM_WIDTH, :], preferred_element_type=F32)
    acc = acc + jnp.dot(c_ref[...], w_ref[DA_WIDTH + SSM_WIDTH:, :], preferred_element_type=F32)
    xo_ref[...] = acc
    ms = jnp.mean(acc * acc, axis=-1, keepdims=True)
    hf = acc * lax.rsqrt(ms + EPS) * g_ref[...]
    h = hf.astype(BF16)
    h_ref[...] = hf.astype(h_ref.dtype)
    if with_router:
        logits = jnp.dot(h, rw_ref[...], preferred_element_type=F32)
        lane = lax.broadcasted_iota(jnp.int32, logits.shape, 1).astype(F32)
        logits = jnp.where(lane < N_EXPERTS, logits, NEG_BIG)
        v1 = jnp.max(logits, axis=-1, keepdims=True)
        i1 = jnp.min(jnp.where(logits == v1, lane, float(LANES)), axis=-1, keepdims=True)
        rest_l = jnp.where(lane == i1, NEG_BIG, logits)
        v2 = jnp.max(rest_l, axis=-1, keepdims=True)
        i2 = jnp.min(jnp.where(rest_l == v2, lane, float(LANES)), axis=-1, keepdims=True)
        e2 = jnp.exp(v2 - v1)
        g1 = 1.0 / (1.0 + e2)
        g2 = e2 * g1
        rinfo_ref[...] = (jnp.where(lane == 0.0, g1, 0.0) + jnp.where(lane == 1.0, g2, 0.0)
                          + jnp.where(lane == 2.0, i1, 0.0) + jnp.where(lane == 3.0, i2, 0.0))


def _outproj(x2, oa, ob, oc, w, gain, router_w=None):
    t = x2.shape[0]
    with_router = router_w is not None
    tok = lambda w_: pl.BlockSpec((TM_PROJ, w_), lambda i: (i, 0))
    in_specs = [tok(D_MODEL), tok(DA_WIDTH), tok(SSM_WIDTH), tok(RET_WIDTH),
                pl.BlockSpec((D_MODEL, D_MODEL), lambda i: (0, 0)),
                pl.BlockSpec((1, D_MODEL), lambda i: (0, 0))]
    out_specs = [tok(D_MODEL), tok(D_MODEL)]
    h_dtype = F32 if with_router else BF16
    out_shape = [jax.ShapeDtypeStruct((t, D_MODEL), F32), jax.ShapeDtypeStruct((t, D_MODEL), h_dtype)]
    args = [x2, oa, ob, oc, w, gain]
    if with_router:
        in_specs.append(pl.BlockSpec((D_MODEL, LANES), lambda i: (0, 0)))
        out_specs.append(tok(LANES))
        out_shape.append(jax.ShapeDtypeStruct((t, LANES), F32))
        args.append(router_w)
    return pl.pallas_call(
        functools.partial(_outproj_kernel, with_router=with_router),
        grid=(t // TM_PROJ,),
        in_specs=in_specs,
        out_specs=out_specs,
        out_shape=out_shape,
        compiler_params=_cparams(("parallel",)),
        name="outproj_router" if with_router else "outproj",
    )(*args)


def _ffn_kernel(x_ref, h_ref, wg_ref, wu_ref, wd_ref, o_ref, acc_ref):
    f = pl.program_id(1)

    @pl.when(f == 0)
    def _init():
        acc_ref[...] = x_ref[...]

    h = h_ref[...]
    gate = jnp.dot(h, wg_ref[...], preferred_element_type=F32)
    up = jnp.dot(h, wu_ref[...], preferred_element_type=F32)
    act = (_silu(gate) * up).astype(BF16)
    acc_ref[...] += jnp.dot(act, wd_ref[...], preferred_element_type=F32)

    @pl.when(f == pl.num_programs(1) - 1)
    def _done():
        o_ref[...] = acc_ref[...]


def _dense_ffn(x2, h, wg, wu, wd):
    t = x2.shape[0]
    nf = wg.shape[1] // TF_FFN
    return pl.pallas_call(
        _ffn_kernel,
        grid=(t // TM_FFN, nf),
        in_specs=[
            pl.BlockSpec((TM_FFN, D_MODEL), lambda i, f: (i, 0)),
            pl.BlockSpec((TM_FFN, D_MODEL), lambda i, f: (i, 0)),
            pl.BlockSpec((D_MODEL, TF_FFN), lambda i, f: (0, f)),
            pl.BlockSpec((D_MODEL, TF_FFN), lambda i, f: (0, f)),
            pl.BlockSpec((TF_FFN, D_MODEL), lambda i, f: (f, 0)),
        ],
        out_specs=pl.BlockSpec((TM_FFN, D_MODEL), lambda i, f: (i, 0)),
        out_shape=jax.ShapeDtypeStruct((t, D_MODEL), F32),
        scratch_shapes=[pltpu.VMEM((TM_FFN, D_MODEL), F32)],
        compiler_params=_cparams(("parallel", "arbitrary")),
        name="dense_swiglu",
    )(x2, h, wg, wu, wd)


def _issue_row_gather(src_hbm, idx_ref, idx_base, dst, sem, n_rows):
    def body(r, carry):
        tok = idx_ref[idx_base + r]
        pltpu.make_async_copy(src_hbm.at[pl.ds(tok, 1)], dst.at[pl.ds(r, 1)], sem).start()
        return carry
    lax.fori_loop(0, n_rows, body, 0, unroll=8)


def _wait_row_gather(src_hbm, dst, sem, n_rows):
    pltpu.make_async_copy(src_hbm.at[pl.ds(0, n_rows)], dst, sem).wait()


def _moe_ffn_kernel(te_ref, nv_ref, rt_ref, h_hbm, wg_ref, wu_ref, wd_ref, y_ref, hbuf, xb_ref, acc_ref, sem):
    i = pl.program_id(0)
    f = pl.program_id(1)
    n_valid = nv_ref[0]
    valid = i < n_valid
    slot = i % 2

    @pl.when((f == 0) & (i == 0) & valid)
    def _first():
        _issue_row_gather(h_hbm, rt_ref, 0, hbuf.at[0], sem.at[0], TR_MOE)

    @pl.when((f == 0) & valid)
    def _arrive():
        _wait_row_gather(h_hbm, hbuf.at[slot], sem.at[slot], TR_MOE)
        xb_ref[...] = hbuf[slot].astype(BF16)

    def compute(prefetch_next):
        if prefetch_next:
            rows_per_step = TR_MOE // (FFN_EXPERT // TF_MOE)
            r0 = f * rows_per_step
            for j in range(rows_per_step):
                tok = rt_ref[(i + 1) * TR_MOE + r0 + j]
                pltpu.make_async_copy(h_hbm.at[pl.ds(tok, 1)], hbuf.at[1 - slot, pl.ds(r0 + j, 1)],
                                      sem.at[1 - slot]).start()
        xb = xb_ref[...]
        gate = jnp.dot(xb, wg_ref[0], preferred_element_type=F32)
        up = jnp.dot(xb, wu_ref[0], preferred_element_type=F32)
        act = (_silu(gate) * up).astype(BF16)
        contrib = jnp.dot(act, wd_ref[0], preferred_element_type=F32)

        @pl.when(f == 0)
        def _():
            acc_ref[...] = contrib

        @pl.when(f > 0)
        def _():
            acc_ref[...] += contrib

    has_next = i + 1 < n_valid

    @pl.when(valid & has_next)
    def _compute_and_prefetch():
        compute(True)

    @pl.when(valid & jnp.logical_not(has_next))
    def _compute_last():
        compute(False)

    @pl.when(f == pl.num_programs(1) - 1)
    def _store():
        y_ref[...] = jnp.where(valid, acc_ref[...], 0.0)


def _moe_ffn(h, tile_expert, n_valid, row_token, wg, wu, wd):
    n_tiles = tile_expert.shape[0]
    nf = wg.shape[2] // TF_MOE

    def w_in_map(i, f, te, nv, rt):
        return (te[i], 0, jnp.where(i < nv[0], f, nf - 1))

    def w_out_map(i, f, te, nv, rt):
        return (te[i], jnp.where(i < nv[0], f, nf - 1), 0)

    grid_spec = pltpu.PrefetchScalarGridSpec(
        num_scalar_prefetch=3,
        grid=(n_tiles, nf),
        in_specs=[
            pl.BlockSpec(memory_space=pl.ANY),
            pl.BlockSpec((1, D_MODEL, TF_MOE), w_in_map),
            pl.BlockSpec((1, D_MODEL, TF_MOE), w_in_map),
            pl.BlockSpec((1, TF_MOE, D_MODEL), w_out_map),
        ],
        out_specs=pl.BlockSpec((TR_MOE, D_MODEL), lambda i, f, te, nv, rt: (i, 0)),
        scratch_shapes=[
            pltpu.VMEM((2, TR_MOE, D_MODEL), F32),
            pltpu.VMEM((TR_MOE, D_MODEL), BF16),
            pltpu.VMEM((TR_MOE, D_MODEL), F32),
            pltpu.SemaphoreType.DMA((2,)),
        ],
    )
    return pl.pallas_call(
        _moe_ffn_kernel,
        grid_spec=grid_spec,
        out_shape=jax.ShapeDtypeStruct((n_tiles * TR_MOE, D_MODEL), F32),
        compiler_params=_cparams(("arbitrary", "arbitrary")),
        name="moe_expert_ffn",
    )(tile_expert, n_valid, row_token, h, wg, wu, wd)


def _moe_combine_kernel(tr_ref, x_ref, rinfo_ref, y_hbm, fg_ref, o_ref, ybuf, sem):
    i = pl.program_id(0)
    slot = i % 2
    n_rows = TOP_K * TM_COMB

    @pl.when(i == 0)
    def _first():
        _issue_row_gather(y_hbm, tr_ref, 0, ybuf.at[0], sem.at[0], n_rows)

    _wait_row_gather(y_hbm, ybuf.at[slot], sem.at[slot], n_rows)

    @pl.when(i + 1 < pl.num_programs(0))
    def _prefetch():
        _issue_row_gather(y_hbm, tr_ref, (i + 1) * n_rows, ybuf.at[1 - slot], sem.at[1 - slot], n_rows)

    rinfo = rinfo_ref[...]
    g1 = jnp.broadcast_to(rinfo[:, 0:1], (TM_COMB, D_MODEL))
    g2 = jnp.broadcast_to(rinfo[:, 1:2], (TM_COMB, D_MODEL))
    y = x_ref[...] + g1 * ybuf[slot, 0:TM_COMB, :] + g2 * ybuf[slot, TM_COMB:n_rows, :]
    ms = jnp.mean(y * y, axis=-1, keepdims=True)
    o_ref[...] = y * lax.rsqrt(ms + EPS) * fg_ref[...]


def _moe_combine(x2, rinfo, y_rows, tok_rows, final_gain):
    t = x2.shape[0]
    grid_spec = pltpu.PrefetchScalarGridSpec(
        num_scalar_prefetch=1,
        grid=(t // TM_COMB,),
        in_specs=[
            pl.BlockSpec((TM_COMB, D_MODEL), lambda i, tr: (i, 0)),
            pl.BlockSpec((TM_COMB, LANES), lambda i, tr: (i, 0)),
            pl.BlockSpec(memory_space=pl.ANY),
            pl.BlockSpec((1, D_MODEL), lambda i, tr: (0, 0)),
        ],
        out_specs=pl.BlockSpec((TM_COMB, D_MODEL), lambda i, tr: (i, 0)),
        scratch_shapes=[
            pltpu.VMEM((2, TOP_K * TM_COMB, D_MODEL), F32),
            pltpu.SemaphoreType.DMA((2,)),
        ],
    )
    return pl.pallas_call(
        _moe_combine_kernel,
        grid_spec=grid_spec,
        out_shape=jax.ShapeDtypeStruct((t, D_MODEL), F32),
        compiler_params=_cparams(("arbitrary",)),
        name="moe_combine_norm",
    )(tok_rows, x2, rinfo, y_rows, final_gain)


def _routing_tables(rinfo):
    t = rinfo.shape[0]
    n_tiles = -(-(TOP_K * t) // TR_MOE) + N_EXPERTS
    flat_e = rinfo[:, 2:2 + TOP_K].astype(jnp.int32).reshape(-1)
    onehot = (flat_e[:, None] == jnp.arange(N_EXPERTS)[None, :]).astype(jnp.int32)
    csum = jnp.cumsum(onehot, axis=0)
    rank = jnp.sum((csum - onehot) * onehot, axis=1)
    counts = csum[-1]
    tiles_per_e = (counts + TR_MOE - 1) // TR_MOE
    tile_end = jnp.cumsum(tiles_per_e)
    row_off = (tile_end - tiles_per_e) * TR_MOE
    dest = row_off[flat_e] + rank
    token = jnp.arange(TOP_K * t, dtype=jnp.int32) // TOP_K
    row_token = jnp.zeros((n_tiles * TR_MOE,), jnp.int32).at[dest].set(token, unique_indices=True)
    n_valid = tile_end[-1:].astype(jnp.int32)
    tile_expert = jnp.searchsorted(tile_end, jnp.arange(n_tiles), side="right").astype(jnp.int32)
    last_e = jnp.searchsorted(tile_end, n_valid[0] - 1, side="right").astype(jnp.int32)
    tile_expert = jnp.where(jnp.arange(n_tiles) < n_valid[0], tile_expert, last_e)
    tok_rows = dest.reshape(t // TM_COMB, TM_COMB, TOP_K).transpose(0, 2, 1).reshape(-1).astype(jnp.int32)
    return tile_expert, n_valid, row_token, tok_rows


def _permute_w_in(w):
    sizes = (256, 256, 256, 512, 768, 8, 256, 256, 256, 256)
    offs = np.concatenate([[0], np.cumsum(sizes)])
    part = lambda n: w[:, offs[n]:offs[n + 1]]
    dt_pad = jnp.zeros((w.shape[0], LANES - SSM_HEADS), w.dtype)
    cols = [part(0) * (DA_HEAD_DIM ** -0.5 * LOG2E), part(2), part(1), part(3), part(4),
            part(6), part(7), part(8), part(9), part(5), dt_pad]
    return jnp.concatenate(cols, axis=1).astype(BF16)


def _pad_lanes(v, fill=0.0):
    return jnp.concatenate([v, jnp.full((LANES - v.shape[0],), fill, v.dtype)])[None, :]


def kernel(x, w_in, w_out, attn_norm, ffn_norm, final_norm, rel_bias, lambda_q1, lambda_k1, lambda_q2,
           lambda_k2, da_head_norm, conv_w, conv_b, dt_bias, a_log, d_skip, ssm_norm, ret_head_norm,
           w_gate, w_up, w_down, router_w, e_gate, e_up, e_down):
    batch, seq, _ = x.shape
    x2 = x.reshape(batch * seq, D_MODEL)
    bias_tiles = _bias_tiles(rel_bias)
    ret_tabs = _retention_tables(seq)

    for layer in range(DEPTH):
        lam_init = 0.8 - 0.6 * math.exp(-0.3 * layer)
        lam = (jnp.exp(jnp.sum(lambda_q1[layer] * lambda_k1[layer]))
               - jnp.exp(jnp.sum(lambda_q2[layer] * lambda_k2[layer])) + lam_init).astype(F32)
        lam = jnp.full((1, ATT_T), lam, F32)

        qkv, qvt, pf = _inproj(x2, attn_norm[layer][None, :], _permute_w_in(w_in[layer]))

        da_gain = (jnp.tile(da_head_norm[layer], DA_HEADS) * (1.0 - lam_init))[:, None]
        out_a = _attention(qkv, qvt, bias_tiles, da_gain, lam, batch, seq)

        cw, cb = conv_w[layer], conv_b[layer]
        out_b = _ssd(pf, cw[:, :SSM_WIDTH], cw[:, SSM_WIDTH:], cb[None, :SSM_WIDTH], cb[None, SSM_WIDTH:],
                     _pad_lanes(dt_bias[layer]), _pad_lanes(a_log[layer]),
                     jnp.repeat(d_skip[layer], SSM_HEAD_DIM)[None, :], ssm_norm[layer][None, :], batch, seq)

        ret_gain = jnp.tile(ret_head_norm[layer], RET_HEADS)[None, :]
        out_c = _retention(pf, *ret_tabs, ret_gain, batch, seq)

        w_o = w_out[layer].astype(BF16)
        if layer % 2 == 0:
            i = layer // 2
            x2, h = _outproj(x2, out_a, out_b, out_c, w_o, ffn_norm[layer][None, :])
            pad = FFN_PAD - FFN_DENSE
            wg = jnp.pad(w_gate[i], ((0, 0), (0, pad))).astype(BF16)
            wu = jnp.pad(w_up[i], ((0, 0), (0, pad))).astype(BF16)
            wd = jnp.pad(w_down[i], ((0, pad), (0, 0))).astype(BF16)
            x2 = _dense_ffn(x2, h, wg, wu, wd)
        else:
            i = layer // 2
            rw = jnp.pad(router_w[i], ((0, 0), (0, LANES - N_EXPERTS))).astype(BF16)
            x2, h, rinfo = _outproj(x2, out_a, out_b, out_c, w_o, ffn_norm[layer][None, :], rw)
            tile_expert, n_valid, row_token, tok_rows = _routing_tables(rinfo)
            y_rows = _moe_ffn(h, tile_expert, n_valid, row_token,
                              e_gate[i].astype(BF16), e_up[i].astype(BF16), e_down[i].astype(BF16))
            x2 = _moe_combine(x2, rinfo, y_rows, tok_rows, final_norm[None, :])
    return x2.reshape(batch, seq, D_MODEL)
```

```python
import functools
import math

import jax
import jax.numpy as jnp
import numpy as np
from jax import lax
from jax.experimental import pallas as pl
from jax.experimental.pallas import tpu as pltpu

F32 = jnp.float32
BF16 = jnp.bfloat16

D_MODEL = 1024
DEPTH = 2
CHUNK = 64
EPS = 1e-6
DA_HEADS = 4
DA_HEAD_DIM = 32
DA_V_DIM = 64
DA_WIDTH = 256
SSM_HEADS = 8
SSM_HEAD_DIM = 64
SSM_WIDTH = 512
SSM_STATE = 64
SSM_GROUPS = 2
SSM_CONV = 4
RET_HEADS = 4
RET_KEY_DIM = 64
RET_WIDTH = 256
ROPE_BASE = 10000.0
REL_BUCKETS = 32
REL_MAX_DIST = 128
FFN_DENSE = 2752
N_EXPERTS = 8
FFN_EXPERT = 3584

LANES = 128
VMEM_LIMIT = 48 * 1024 * 1024
NEG_BIG = -1e30

TM_PROJ = 512
ATT_T = 512
LOG2E = math.log2(math.e)
BLK = 256
FFN_PAD = 2816
TM_FFN = 1024
TF_FFN = 256
TF_MOE = 512
TR_MOE = 896
TM_COMB = 256
TOP_K = 2

C_Z, C_XS, C_BC, C_RQ, C_RK, C_RV, C_RG, C_DT = 0, 512, 1024, 1280, 1536, 1792, 2048, 2304
PF_W = 2432
PF_CHUNK = 640
QKV_W = 768


def _cparams(sem):
    return pltpu.CompilerParams(dimension_semantics=sem, vmem_limit_bytes=VMEM_LIMIT)


def _group_mask(shape, axis, group, idx):
    io = lax.broadcasted_iota(jnp.int32, shape, axis)
    return (io >= idx * group) & (io < (idx + 1) * group)


def _lane_tile(x, n, axis=1):
    return jnp.concatenate([x] * n, axis=axis)


def _silu(x):
    return x * (1.0 / (1.0 + jnp.exp(-x)))


def _inproj_kernel(x_ref, g_ref, w_ref, qkv_ref, qvt_ref, pf_ref):
    x = x_ref[...]
    ms = jnp.mean(x * x, axis=-1, keepdims=True)
    h = (x * lax.rsqrt(ms + EPS) * g_ref[...]).astype(BF16)
    qkv = jnp.dot(h, w_ref[:, :QKV_W], preferred_element_type=F32)
    qkv_ref[...] = qkv.astype(BF16)
    qvt_ref[...] = qkv[:, :2 * DA_WIDTH].T.astype(BF16)
    for lo in range(0, PF_W, PF_CHUNK):
        hi = min(lo + PF_CHUNK, PF_W)
        pf_ref[:, lo:hi] = jnp.dot(h, w_ref[:, QKV_W + lo:QKV_W + hi], preferred_element_type=F32)


def _inproj(x2, gain, w):
    t = x2.shape[0]
    return pl.pallas_call(
        _inproj_kernel,
        grid=(t // TM_PROJ,),
        in_specs=[
            pl.BlockSpec((TM_PROJ, D_MODEL), lambda i: (i, 0)),
            pl.BlockSpec((1, D_MODEL), lambda i: (0, 0)),
            pl.BlockSpec((D_MODEL, QKV_W + PF_W), lambda i: (0, 0)),
        ],
        out_specs=[
            pl.BlockSpec((TM_PROJ, QKV_W), lambda i: (i, 0)),
            pl.BlockSpec((2 * DA_WIDTH, TM_PROJ), lambda i: (0, i)),
            pl.BlockSpec((TM_PROJ, PF_W), lambda i: (i, 0)),
        ],
        out_shape=[
            jax.ShapeDtypeStruct((t, QKV_W), BF16),
            jax.ShapeDtypeStruct((2 * DA_WIDTH, t), BF16),
            jax.ShapeDtypeStruct((t, PF_W), F32),
        ],
        compiler_params=_cparams(("parallel",)),
        name="inproj",
    )(x2, gain, w)


def _attn_kernel(qi_ref, kj_ref, qt_ref, k_ref, vt_ref, bias_ref, gain_ref, lam_ref, o_ref,
                 qm_ref, m_ref, l_ref, acc_ref, s_ref, e_ref, al_ref):
    p = pl.program_id(1)
    qi = qi_ref[p]
    kj = kj_ref[p]
    n_ht = 2 * DA_HEADS

    @pl.when(kj == 0)
    def _init():
        qt = qt_ref[...]
        for ht in range(n_ht):
            qm_ref[ht] = jnp.where(_group_mask(qt.shape, 0, DA_HEAD_DIM, ht), qt, jnp.zeros_like(qt))
        m_ref[...] = jnp.full(m_ref.shape, NEG_BIG, F32)
        l_ref[...] = jnp.zeros(l_ref.shape, F32)
        acc_ref[...] = jnp.zeros(acc_ref.shape, F32)

    def step(with_bias):
        k = k_ref[...]
        def scores(a, buf):
            s_ref[buf] = jnp.dot(k, qm_ref[a], preferred_element_type=F32)

        def softmax(b, buf):
            for c0 in range(0, ATT_T, LANES):
                cols = slice(c0, c0 + LANES)

                def block():
                    blk = s_ref[buf, :, cols]
                    return blk + bias_ref[0, b // 2, :, cols] if with_bias else blk

                m_prev = m_ref[b, :, cols]
                m_new = jnp.maximum(m_prev, jnp.max(block(), axis=0, keepdims=True))
                alpha = jnp.exp2(m_prev - m_new)
                e = jnp.exp2(block() - m_new)
                l_ref[b, :, cols] = alpha * l_ref[b, :, cols] + jnp.sum(e, axis=0, keepdims=True)
                m_ref[b, :, cols] = m_new
                al_ref[buf, :, cols] = alpha
                e_ref[buf, :, cols] = e.astype(BF16)

        def values(c, buf):
            r0 = (c // 4) * LANES
            if not isinstance(r0, int):
                r0 = pl.multiple_of(r0, LANES)
            pv = jnp.dot(vt_ref[pl.ds(r0, LANES), :], e_ref[buf], preferred_element_type=F32)
            acc_ref[c] = acc_ref[c] * al_ref[buf] + pv

        scores(0, 0)
        scores(1, 1)
        softmax(0, 0)

        def trip(u, carry):
            t = 2 * u + 1
            scores(t + 1, 0)
            softmax(t, 1)
            values(t - 1, 0)
            scores(t + 2, 1)
            softmax(t + 1, 0)
            values(t, 1)
            return carry

        for u in range((n_ht - 2) // 2):
            trip(u, 0)
        softmax(n_ht - 1, 1)
        values(n_ht - 2, 0)
        values(n_ht - 1, 1)

    @pl.when(kj >= qi - 1)
    def _near():
        step(True)

    @pl.when(kj < qi - 1)
    def _far():
        step(False)

    @pl.when(kj == qi)
    def _finish():
        lam = lam_ref[...]
        row_lo = lax.broadcasted_iota(jnp.int32, (LANES, ATT_T), 0) < DA_V_DIM
        for pair in range(DA_HEADS // 2):
            halves = []
            for sub in range(2):
                h = 2 * pair + sub
                o = acc_ref[2 * h] * (1.0 / l_ref[2 * h]) - lam * (acc_ref[2 * h + 1] * (1.0 / l_ref[2 * h + 1]))
                own = row_lo if sub == 0 else jnp.logical_not(row_lo)
                ms = jnp.sum(jnp.where(own, o * o, 0.0), axis=0, keepdims=True) * (1.0 / DA_V_DIM)
                halves.append(o * lax.rsqrt(ms + EPS))
            r0 = pair * LANES
            blk = jnp.where(row_lo, halves[0], halves[1]) * gain_ref[r0:r0 + LANES, :]
            o_ref[:, r0:r0 + LANES] = blk.T.astype(BF16)


def _attention(qkv, qvt, bias_tiles, gain, lam, batch, seq):
    nq = seq // ATT_T
    qi_l, kj_l = [], []
    for qi in range(nq):
        for kj in range(qi + 1):
            qi_l.append(qi)
            kj_l.append(kj)
    qi_tbl = jnp.asarray(np.array(qi_l, np.int32))
    kj_tbl = jnp.asarray(np.array(kj_l, np.int32))
    n_pairs = len(qi_l)
    n_ht = 2 * DA_HEADS

    grid_spec = pltpu.PrefetchScalarGridSpec(
        num_scalar_prefetch=2,
        grid=(batch, n_pairs),
        in_specs=[
            pl.BlockSpec((DA_WIDTH, ATT_T), lambda b, p, qi, kj: (0, b * nq + qi[p])),
            pl.BlockSpec((ATT_T, DA_WIDTH), lambda b, p, qi, kj: (b * nq + kj[p], 2)),
            pl.BlockSpec((DA_WIDTH, ATT_T), lambda b, p, qi, kj: (1, b * nq + kj[p])),
            pl.BlockSpec((1, DA_HEADS, ATT_T, ATT_T),
                         lambda b, p, qi, kj: (jnp.minimum(qi[p] - kj[p], 1), 0, 0, 0)),
            pl.BlockSpec((DA_WIDTH, 1), lambda b, p, qi, kj: (0, 0)),
            pl.BlockSpec((1, ATT_T), lambda b, p, qi, kj: (0, 0)),
        ],
        out_specs=pl.BlockSpec((ATT_T, DA_WIDTH), lambda b, p, qi, kj: (b * nq + qi[p], 0)),
        scratch_shapes=[
            pltpu.VMEM((n_ht, DA_WIDTH, ATT_T), BF16),
            pltpu.VMEM((n_ht, 1, ATT_T), F32),
            pltpu.VMEM((n_ht, 1, ATT_T), F32),
            pltpu.VMEM((n_ht, LANES, ATT_T), F32),
            pltpu.VMEM((2, ATT_T, ATT_T), F32),
            pltpu.VMEM((2, ATT_T, ATT_T), BF16),
            pltpu.VMEM((2, 1, ATT_T), F32),
        ],
    )
    return pl.pallas_call(
        _attn_kernel,
        grid_spec=grid_spec,
        out_shape=jax.ShapeDtypeStruct((batch * seq, DA_WIDTH), BF16),
        compiler_params=_cparams(("parallel", "arbitrary")),
        name="diff_attention",
    )(qi_tbl, kj_tbl, qvt, qkv, qvt, bias_tiles, gain, lam)


def _t5_bucket(rel):
    half = REL_BUCKETS // 2
    max_exact = half // 2
    ret = jnp.where(rel > 0, half, 0)
    n = jnp.abs(rel)
    nf = jnp.maximum(n, 1).astype(F32)
    large = max_exact + (jnp.log(nf / max_exact) / math.log(REL_MAX_DIST / max_exact)
                         * (half - max_exact)).astype(jnp.int32)
    large = jnp.minimum(large, half - 1)
    return ret + jnp.where(n < max_exact, n, large)


def _bias_tiles(rel_bias):
    i = jnp.arange(ATT_T)[None, :]
    j = jnp.arange(ATT_T)[:, None]
    far = rel_bias[_t5_bucket(jnp.int32(-2 * ATT_T))].astype(F32)
    tiles = []
    for off in range(2):
        bucket = _t5_bucket((j - off * ATT_T) - i)
        b = jnp.zeros((DA_HEADS, ATT_T, ATT_T), F32)
        for c in range(REL_BUCKETS):
            b = jnp.where((bucket == c)[None], rel_bias[c].astype(F32)[:, None, None], b)
        b = (b - far[:, None, None]) * LOG2E
        if off == 0:
            allowed = (j // CHUNK) <= (i // CHUNK)
            b = jnp.where(allowed[None], b, NEG_BIG)
        tiles.append(b)
    return jnp.stack(tiles)


def _cumsum_rows(x):
    n = x.shape[0]
    row = lax.broadcasted_iota(jnp.int32, x.shape, 0)
    sh = 1
    while sh < n:
        x = x + jnp.where(row >= sh, pltpu.roll(x, sh, axis=0), 0.0)
        sh *= 2
    return x


def _expand_heads(v, width, group):
    out = jnp.zeros((v.shape[0], width), F32)
    for h in range(width // group):
        col = jnp.broadcast_to(v[:, h:h + 1], (v.shape[0], width))
        out = jnp.where(_group_mask(out.shape, 1, group, h), col, out)
    return out


def _ssd_kernel(z_ref, xs_ref, bc_ref, dt_ref, cwx_ref, cwb_ref, cbx_ref, cbb_ref, dtb_ref, alog_ref,
                dskip_ref, ng_ref, o_ref, extx_ref, extb_ref, st_ref):
    j = pl.program_id(1)

    @pl.when(j == 0)
    def _reset():
        extx_ref[0:8, :] = jnp.zeros((8, SSM_WIDTH), F32)
        extb_ref[0:8, :] = jnp.zeros((8, 2 * LANES), F32)
        st_ref[...] = jnp.zeros(st_ref.shape, F32)

    extx_ref[8:8 + BLK, :] = xs_ref[...]
    extb_ref[8:8 + BLK, :] = bc_ref[...]
    yx = jnp.broadcast_to(cbx_ref[...], (BLK, SSM_WIDTH))
    yb = jnp.broadcast_to(cbb_ref[...], (BLK, 2 * LANES))
    for w in range(SSM_CONV):
        lo = 8 - (SSM_CONV - 1) + w
        yx = yx + extx_ref[lo:lo + BLK, :] * cwx_ref[w:w + 1, :]
        yb = yb + extb_ref[lo:lo + BLK, :] * cwb_ref[w:w + 1, :]
    tail_x = extx_ref[BLK:BLK + 8, :]
    tail_b = extb_ref[BLK:BLK + 8, :]
    extx_ref[0:8, :] = tail_x
    extb_ref[0:8, :] = tail_b
    xs = _silu(yx)
    bc = _silu(yb)
    bm = bc[:, :LANES]
    cm = bc[:, LANES:]

    dt = jax.nn.softplus(dt_ref[...] + dtb_ref[...])
    a = -jnp.exp(alog_ref[...]) * dt
    a_cs = _cumsum_rows(a)
    a_cs_t = a_cs.T
    a_last = a_cs[BLK - 1:BLK, :]

    dt_full = _expand_heads(dt, SSM_WIDTH, SSM_HEAD_DIM)
    ea_full = _expand_heads(jnp.exp(a_cs), SSM_WIDTH, SSM_HEAD_DIM)
    dec_full = _expand_heads(jnp.exp(a_last - a_cs), SSM_WIDTH, SSM_HEAD_DIM)
    cdec_full = _expand_heads(jnp.exp(a_last), SSM_WIDTH, SSM_HEAD_DIM)

    xdt = xs * dt_full
    xdt_b = xdt.astype(BF16)
    cm_b = cm.astype(BF16)
    bm_b = bm.astype(BF16)

    st = st_ref[...]
    y = jnp.dot(cm_b, st.astype(BF16), preferred_element_type=F32) * ea_full + xs * dskip_ref[...]

    row = lax.broadcasted_iota(jnp.int32, (BLK, BLK), 0)
    colio = lax.broadcasted_iota(jnp.int32, (BLK, BLK), 1)
    causal = colio <= row
    rep = SSM_HEADS // SSM_GROUPS
    lane_lo = lax.broadcasted_iota(jnp.int32, (BLK, LANES), 1) < SSM_HEAD_DIM
    pieces = []
    for g in range(SSM_GROUPS):
        cg = jnp.where(_group_mask(cm_b.shape, 1, SSM_STATE, g), cm_b, jnp.zeros_like(cm_b))
        cb = lax.dot_general(cg, bm_b, (((1,), (1,)), ((), ())), preferred_element_type=F32)
        for pair in range(rep // 2):
            halves = []
            for sub in range(2):
                h = g * rep + pair * 2 + sub
                seg = jnp.broadcast_to(a_cs[:, h:h + 1], (BLK, BLK)) - a_cs_t[h:h + 1, :]
                lmat = jnp.exp(jnp.where(causal, seg, NEG_BIG))
                mh = (cb * lmat).astype(BF16)
                c0 = (h // 2) * LANES
                halves.append(jnp.dot(mh, xdt_b[:, c0:c0 + LANES], preferred_element_type=F32))
            pieces.append(jnp.where(lane_lo, halves[0], halves[1]))
    y = y + jnp.concatenate(pieces, axis=1)

    upd = jnp.dot(bm.T.astype(BF16), (xdt * dec_full).astype(BF16), preferred_element_type=F32)
    r_io = lax.broadcasted_iota(jnp.int32, upd.shape, 0) // SSM_STATE
    c_io = lax.broadcasted_iota(jnp.int32, upd.shape, 1) // (SSM_HEAD_DIM * rep)
    st_ref[...] = st * cdec_full + jnp.where(r_io == c_io, upd, 0.0)

    y = y * _silu(z_ref[...])
    gw = SSM_WIDTH // SSM_GROUPS
    for g in range(SSM_GROUPS):
        yg = y[:, g * gw:(g + 1) * gw]
        ms = jnp.mean(yg * yg, axis=-1, keepdims=True)
        o_ref[:, g * gw:(g + 1) * gw] = (yg * lax.rsqrt(ms + EPS) * ng_ref[:, g * gw:(g + 1) * gw]).astype(BF16)


def _ssd(pf, cwx, cwb, cbx, cbb, dtb, alog, dskip, ng, batch, seq):
    nb = seq // BLK
    row = lambda w, c: pl.BlockSpec((BLK, w), lambda b, j: (b * nb + j, c))
    const = lambda r, w: pl.BlockSpec((r, w), lambda b, j: (0, 0))
    return pl.pallas_call(
        _ssd_kernel,
        grid=(batch, nb),
        in_specs=[
            row(SSM_WIDTH, C_Z // SSM_WIDTH), row(SSM_WIDTH, C_XS // SSM_WIDTH), row(2 * LANES, C_BC // (2 * LANES)),
            row(LANES, C_DT // LANES),
            const(SSM_CONV, SSM_WIDTH), const(SSM_CONV, 2 * LANES), const(1, SSM_WIDTH), const(1, 2 * LANES),
            const(1, LANES), const(1, LANES), const(1, SSM_WIDTH), const(1, SSM_WIDTH),
        ],
        out_specs=pl.BlockSpec((BLK, SSM_WIDTH), lambda b, j: (b * nb + j, 0)),
        out_shape=jax.ShapeDtypeStruct((batch * seq, SSM_WIDTH), BF16),
        scratch_shapes=[
            pltpu.VMEM((BLK + 8, SSM_WIDTH), F32),
            pltpu.VMEM((BLK + 8, 2 * LANES), F32),
            pltpu.VMEM((SSM_GROUPS * SSM_STATE, SSM_WIDTH), F32),
        ],
        compiler_params=_cparams(("parallel", "arbitrary")),
        name="ssd_mixer",
    )(pf, pf, pf, pf, cwx, cwb, cbx, cbb, dtb, alog, dskip, ng)


def _rotary(u, cos, sin_signed):
    first = (lax.broadcasted_iota(jnp.int32, u.shape, 1) % RET_KEY_DIM) < (RET_KEY_DIM // 2)
    half = RET_KEY_DIM // 2
    swapped = jnp.where(first, pltpu.roll(u, u.shape[1] - half, axis=1), pltpu.roll(u, half, axis=1))
    return u * cos + swapped * sin_signed


def _ret_kernel(q_ref, k_ref, v_ref, g_ref, cos_ref, sin_ref, dmat_ref, qdec_ref, kdec_ref, gdec_ref,
                bd_ref, gain_ref, o_ref, st_ref):
    j = pl.program_id(1)

    @pl.when(j == 0)
    def _reset():
        st_ref[...] = jnp.zeros(st_ref.shape, F32)

    cos = cos_ref[...]
    sin = sin_ref[...]
    q = _rotary(q_ref[...], cos, sin)
    k = _rotary(k_ref[...], cos, sin) * (RET_KEY_DIM ** -0.5)
    v_b = v_ref[...].astype(BF16)
    q_b = q.astype(BF16)
    k_b = k.astype(BF16)

    st = st_ref[...]
    o = jnp.dot((q * qdec_ref[...]).astype(BF16), st.astype(BF16), preferred_element_type=F32)
    for h in range(RET_HEADS):
        cm = _group_mask(q_b.shape, 1, RET_KEY_DIM, h)
        qh = jnp.where(cm, q_b, jnp.zeros_like(q_b))
        s = lax.dot_general(qh, k_b, (((1,), (1,)), ((), ())), preferred_element_type=F32) * dmat_ref[h]
        oh = jnp.dot(s.astype(BF16), v_b, preferred_element_type=F32)
        o = o + jnp.where(cm, oh, 0.0)

    upd = jnp.dot((k * kdec_ref[...]).T.astype(BF16), v_b, preferred_element_type=F32)
    st_ref[...] = st * gdec_ref[...] + upd * bd_ref[...]

    out = jnp.zeros(o.shape, F32)
    for h in range(RET_HEADS):
        cm = _group_mask(o.shape, 1, RET_KEY_DIM, h)
        ms = jnp.sum(jnp.where(cm, o * o, 0.0), axis=-1, keepdims=True) * (1.0 / RET_KEY_DIM)
        out = out + jnp.where(cm, o * lax.rsqrt(ms + EPS), 0.0)
    o_ref[...] = (_silu(g_ref[...]) * (out * gain_ref[...])).astype(BF16)


def _retention(pf, cos, sin, dmat, qdec, kdec, gdec, bd, gain, batch, seq):
    nb = seq // BLK
    row = lambda c: pl.BlockSpec((BLK, RET_WIDTH), lambda b, j: (b * nb + j, c))
    const2 = pl.BlockSpec((BLK, RET_WIDTH), lambda b, j: (0, 0))
    return pl.pallas_call(
        _ret_kernel,
        grid=(batch, nb),
        in_specs=[
            row(C_RQ // RET_WIDTH), row(C_RK // RET_WIDTH), row(C_RV // RET_WIDTH), row(C_RG // RET_WIDTH),
            pl.BlockSpec((BLK, RET_WIDTH), lambda b, j: (j, 0)),
            pl.BlockSpec((BLK, RET_WIDTH), lambda b, j: (j, 0)),
            pl.BlockSpec((RET_HEADS, BLK, BLK), lambda b, j: (0, 0, 0)),
            const2, const2, const2, const2,
            pl.BlockSpec((1, RET_WIDTH), lambda b, j: (0, 0)),
        ],
        out_specs=pl.BlockSpec((BLK, RET_WIDTH), lambda b, j: (b * nb + j, 0)),
        out_shape=jax.ShapeDtypeStruct((batch * seq, RET_WIDTH), BF16),
        scratch_shapes=[pltpu.VMEM((RET_WIDTH, RET_WIDTH), F32)],
        compiler_params=_cparams(("parallel", "arbitrary")),
        name="retention",
    )(pf, pf, pf, pf, cos, sin, dmat, qdec, kdec, gdec, bd, gain)


def _retention_tables(seq):
    half = RET_KEY_DIM // 2
    inv = 1.0 / (ROPE_BASE ** (jnp.arange(0, RET_KEY_DIM, 2, dtype=F32) / RET_KEY_DIM))
    ang = jnp.arange(seq, dtype=F32)[:, None] * inv[None, :]
    cos_h = jnp.concatenate([jnp.cos(ang), jnp.cos(ang)], axis=-1)
    sin_h = jnp.concatenate([-jnp.sin(ang), jnp.sin(ang)], axis=-1)
    cos = jnp.tile(cos_h, (1, RET_HEADS))
    sin = jnp.tile(sin_h, (1, RET_HEADS))
    log_gamma = jnp.log1p(-jnp.power(2.0, -5.0 - jnp.arange(RET_HEADS, dtype=F32)))
    idx = jnp.arange(BLK, dtype=F32)
    ii = jnp.arange(BLK)
    same_or_earlier_chunk = (ii[None, :] // CHUNK) <= (ii[:, None] // CHUNK)
    dmat = jnp.exp(log_gamma[:, None, None] * jnp.abs(idx[:, None] - idx[None, :]))
    dmat = jnp.where(same_or_earlier_chunk[None], dmat, 0.0)
    lg_cols = jnp.repeat(log_gamma, RET_KEY_DIM)
    qdec = jnp.exp(lg_cols[None, :] * (idx + 1.0)[:, None])
    kdec = jnp.exp(lg_cols[None, :] * (BLK - 1.0 - idx)[:, None])
    head_of = jnp.arange(RET_WIDTH) // RET_KEY_DIM
    bd = (head_of[:, None] == head_of[None, :]).astype(F32)
    gdec = jnp.exp(lg_cols * BLK)[:, None] * bd
    return cos, sin, dmat, qdec, kdec, gdec, bd


def _outproj_kernel(x_ref, a_ref, b_ref, c_ref, w_ref, g_ref, *rest, with_router):
    if with_router:
        rw_ref, xo_ref, h_ref, rinfo_ref = rest
    else:
        xo_ref, h_ref = rest
    acc = x_ref[...]
    acc = acc + jnp.dot(a_ref[...], w_ref[0:DA_WIDTH, :], preferred_element_type=F32)
    acc = acc + jnp.dot(b_ref[...], w_ref[DA_WIDTH:DA_WIDTH + SSM_WIDTH, :], preferred_element_type=F32)
    acc = acc + jnp.dot(c_ref[...], w_ref[DA_WIDTH + SSM_WIDTH:, :], preferred_element_type=F32)
    xo_ref[...] = acc
    ms = jnp.mean(acc * acc, axis=-1, keepdims=True)
    hf = acc * lax.rsqrt(ms + EPS) * g_ref[...]
    h = hf.astype(BF16)
    h_ref[...] = hf.astype(h_ref.dtype)
    if with_router:
        logits = jnp.dot(h, rw_ref[...], preferred_element_type=F32)
        lane = lax.broadcasted_iota(jnp.int32, logits.shape, 1).astype(F32)
        logits = jnp.where(lane < N_EXPERTS, logits, NEG_BIG)
        v1 = jnp.max(logits, axis=-1, keepdims=True)
        i1 = jnp.min(jnp.where(logits == v1, lane, float(LANES)), axis=-1, keepdims=True)
        rest_l = jnp.where(lane == i1, NEG_BIG, logits)
        v2 = jnp.max(rest_l, axis=-1, keepdims=True)
        i2 = jnp.min(jnp.where(rest_l == v2, lane, float(LANES)), axis=-1, keepdims=True)
        e2 = jnp.exp(v2 - v1)
        g1 = 1.0 / (1.0 + e2)
        g2 = e2 * g1
        rinfo_ref[...] = (jnp.where(lane == 0.0, g1, 0.0) + jnp.where(lane == 1.0, g2, 0.0)
                          + jnp.where(lane == 2.0, i1, 0.0) + jnp.where(lane == 3.0, i2, 0.0))


def _outproj(x2, oa, ob, oc, w, gain, router_w=None):
    t = x2.shape[0]
    with_router = router_w is not None
    tok = lambda w_: pl.BlockSpec((TM_PROJ, w_), lambda i: (i, 0))
    in_specs = [tok(D_MODEL), tok(DA_WIDTH), tok(SSM_WIDTH), tok(RET_WIDTH),
                pl.BlockSpec((D_MODEL, D_MODEL), lambda i: (0, 0)),
                pl.BlockSpec((1, D_MODEL), lambda i: (0, 0))]
    out_specs = [tok(D_MODEL), tok(D_MODEL)]
    h_dtype = F32 if with_router else BF16
    out_shape = [jax.ShapeDtypeStruct((t, D_MODEL), F32), jax.ShapeDtypeStruct((t, D_MODEL), h_dtype)]
    args = [x2, oa, ob, oc, w, gain]
    if with_router:
        in_specs.append(pl.BlockSpec((D_MODEL, LANES), lambda i: (0, 0)))
        out_specs.append(tok(LANES))
        out_shape.append(jax.ShapeDtypeStruct((t, LANES), F32))
        args.append(router_w)
    return pl.pallas_call(
        functools.partial(_outproj_kernel, with_router=with_router),
        grid=(t // TM_PROJ,),
        in_specs=in_specs,
        out_specs=out_specs,
        out_shape=out_shape,
        compiler_params=_cparams(("parallel",)),
        name="outproj_router" if with_router else "outproj",
    )(*args)


def _ffn_kernel(x_ref, h_ref, wg_ref, wu_ref, wd_ref, o_ref, acc_ref):
    f = pl.program_id(1)

    @pl.when(f == 0)
    def _init():
        acc_ref[...] = x_ref[...]

    h = h_ref[...]
    gate = jnp.dot(h, wg_ref[0], preferred_element_type=F32)
    up = jnp.dot(h, wu_ref[0], preferred_element_type=F32)
    act = (_silu(gate) * up).astype(BF16)
    acc_ref[...] += jnp.dot(act, wd_ref[...], preferred_element_type=F32)

    @pl.when(f == pl.num_programs(1) - 1)
    def _done():
        o_ref[...] = acc_ref[...]


def _dense_ffn(x2, h, wg, wu, wd):
    t = x2.shape[0]
    nf = wg.shape[0]
    return pl.pallas_call(
        _ffn_kernel,
        grid=(t // TM_FFN, nf),
        in_specs=[
            pl.BlockSpec((TM_FFN, D_MODEL), lambda i, f: (i, 0)),
            pl.BlockSpec((TM_FFN, D_MODEL), lambda i, f: (i, 0)),
            pl.BlockSpec((1, D_MODEL, TF_FFN), lambda i, f: (f, 0, 0)),
            pl.BlockSpec((1, D_MODEL, TF_FFN), lambda i, f: (f, 0, 0)),
            pl.BlockSpec((TF_FFN, D_MODEL), lambda i, f: (f, 0)),
        ],
        out_specs=pl.BlockSpec((TM_FFN, D_MODEL), lambda i, f: (i, 0)),
        out_shape=jax.ShapeDtypeStruct((t, D_MODEL), F32),
        scratch_shapes=[pltpu.VMEM((TM_FFN, D_MODEL), F32)],
        compiler_params=_cparams(("parallel", "arbitrary")),
        name="dense_swiglu",
    )(x2, h, wg, wu, wd)


def _issue_row_gather(src_hbm, idx_ref, idx_base, dst, sem, n_rows):
    def body(r, carry):
        tok = idx_ref[idx_base + r]
        pltpu.make_async_copy(src_hbm.at[pl.ds(tok, 1)], dst.at[pl.ds(r, 1)], sem).start()
        return carry
    lax.fori_loop(0, n_rows, body, 0, unroll=8)


def _wait_row_gather(src_hbm, dst, sem, n_rows):
    pltpu.make_async_copy(src_hbm.at[pl.ds(0, n_rows)], dst, sem).wait()


def _moe_ffn_kernel(te_ref, nv_ref, rt_ref, h_hbm, wg_ref, wu_ref, wd_ref, y_ref, hbuf, xb_ref, acc_ref, sem):
    i = pl.program_id(0)
    f = pl.program_id(1)
    n_valid = nv_ref[0]
    valid = i < n_valid
    slot = i % 2

    @pl.when((f == 0) & (i == 0) & valid)
    def _first():
        _issue_row_gather(h_hbm, rt_ref, 0, hbuf.at[0], sem.at[0], TR_MOE)

    @pl.when((f == 0) & valid)
    def _arrive():
        _wait_row_gather(h_hbm, hbuf.at[slot], sem.at[slot], TR_MOE)
        xb_ref[...] = hbuf[slot].astype(BF16)

    def compute(prefetch_next):
        if prefetch_next:
            rows_per_step = TR_MOE // (FFN_EXPERT // TF_MOE)
            r0 = f * rows_per_step
            for j in range(rows_per_step):
                tok = rt_ref[(i + 1) * TR_MOE + r0 + j]
                pltpu.make_async_copy(h_hbm.at[pl.ds(tok, 1)], hbuf.at[1 - slot, pl.ds(r0 + j, 1)],
                                      sem.at[1 - slot]).start()
        xb = xb_ref[...]
        gate = jnp.dot(xb, wg_ref[0, 0], preferred_element_type=F32)
        up = jnp.dot(xb, wu_ref[0, 0], preferred_element_type=F32)
        act = (_silu(gate) * up).astype(BF16)
        contrib = jnp.dot(act, wd_ref[0], preferred_element_type=F32)

        @pl.when(f == 0)
        def _():
            acc_ref[...] = contrib

        @pl.when(f > 0)
        def _():
            acc_ref[...] += contrib

    has_next = i + 1 < n_valid

    @pl.when(valid & has_next)
    def _compute_and_prefetch():
        compute(True)

    @pl.when(valid & jnp.logical_not(has_next))
    def _compute_last():
        compute(False)

    @pl.when(f == pl.num_programs(1) - 1)
    def _store():
        y_ref[...] = jnp.where(valid, acc_ref[...], 0.0)


def _moe_ffn(h, tile_expert, n_valid, row_token, wg, wu, wd):
    n_tiles = tile_expert.shape[0]
    nf = wg.shape[1]

    def w_in_map(i, f, te, nv, rt):
        return (te[i], jnp.where(i < nv[0], f, nf - 1), 0, 0)

    def w_out_map(i, f, te, nv, rt):
        return (te[i], jnp.where(i < nv[0], f, nf - 1), 0)

    grid_spec = pltpu.PrefetchScalarGridSpec(
        num_scalar_prefetch=3,
        grid=(n_tiles, nf),
        in_specs=[
            pl.BlockSpec(memory_space=pl.ANY),
            pl.BlockSpec((1, 1, D_MODEL, TF_MOE), w_in_map),
            pl.BlockSpec((1, 1, D_MODEL, TF_MOE), w_in_map),
            pl.BlockSpec((1, TF_MOE, D_MODEL), w_out_map),
        ],
        out_specs=pl.BlockSpec((TR_MOE, D_MODEL), lambda i, f, te, nv, rt: (i, 0)),
        scratch_shapes=[
            pltpu.VMEM((2, TR_MOE, D_MODEL), F32),
            pltpu.VMEM((TR_MOE, D_MODEL), BF16),
            pltpu.VMEM((TR_MOE, D_MODEL), F32),
            pltpu.SemaphoreType.DMA((2,)),
        ],
    )
    return pl.pallas_call(
        _moe_ffn_kernel,
        grid_spec=grid_spec,
        out_shape=jax.ShapeDtypeStruct((n_tiles * TR_MOE, D_MODEL), F32),
        compiler_params=_cparams(("arbitrary", "arbitrary")),
        name="moe_expert_ffn",
    )(tile_expert, n_valid, row_token, h, wg, wu, wd)


def _moe_combine_kernel(tr_ref, x_ref, rinfo_ref, y_hbm, fg_ref, o_ref, ybuf, sem):
    i = pl.program_id(0)
    slot = i % 2
    n_rows = TOP_K * TM_COMB

    @pl.when(i == 0)
    def _first():
        _issue_row_gather(y_hbm, tr_ref, 0, ybuf.at[0], sem.at[0], n_rows)

    _wait_row_gather(y_hbm, ybuf.at[slot], sem.at[slot], n_rows)

    @pl.when(i + 1 < pl.num_programs(0))
    def _prefetch():
        _issue_row_gather(y_hbm, tr_ref, (i + 1) * n_rows, ybuf.at[1 - slot], sem.at[1 - slot], n_rows)

    rinfo = rinfo_ref[...]
    g1 = jnp.broadcast_to(rinfo[:, 0:1], (TM_COMB, D_MODEL))
    g2 = jnp.broadcast_to(rinfo[:, 1:2], (TM_COMB, D_MODEL))
    y = x_ref[...] + g1 * ybuf[slot, 0:TM_COMB, :] + g2 * ybuf[slot, TM_COMB:n_rows, :]
    ms = jnp.mean(y * y, axis=-1, keepdims=True)
    o_ref[...] = y * lax.rsqrt(ms + EPS) * fg_ref[...]


def _moe_combine(x2, rinfo, y_rows, tok_rows, final_gain):
    t = x2.shape[0]
    grid_spec = pltpu.PrefetchScalarGridSpec(
        num_scalar_prefetch=1,
        grid=(t // TM_COMB,),
        in_specs=[
            pl.BlockSpec((TM_COMB, D_MODEL), lambda i, tr: (i, 0)),
            pl.BlockSpec((TM_COMB, LANES), lambda i, tr: (i, 0)),
            pl.BlockSpec(memory_space=pl.ANY),
            pl.BlockSpec((1, D_MODEL), lambda i, tr: (0, 0)),
        ],
        out_specs=pl.BlockSpec((TM_COMB, D_MODEL), lambda i, tr: (i, 0)),
        scratch_shapes=[
            pltpu.VMEM((2, TOP_K * TM_COMB, D_MODEL), F32),
            pltpu.SemaphoreType.DMA((2,)),
        ],
    )
    return pl.pallas_call(
        _moe_combine_kernel,
        grid_spec=grid_spec,
        out_shape=jax.ShapeDtypeStruct((t, D_MODEL), F32),
        compiler_params=_cparams(("arbitrary",)),
        name="moe_combine_norm",
    )(tok_rows, x2, rinfo, y_rows, final_gain)


def _routing_tables(rinfo):
    t = rinfo.shape[0]
    n_tiles = -(-(TOP_K * t) // TR_MOE) + N_EXPERTS
    flat_e = rinfo[:, 2:2 + TOP_K].astype(jnp.int32).reshape(-1)
    onehot = (flat_e[:, None] == jnp.arange(N_EXPERTS)[None, :]).astype(jnp.int32)
    csum = jnp.cumsum(onehot, axis=0)
    rank = jnp.sum((csum - onehot) * onehot, axis=1)
    counts = csum[-1]
    tiles_per_e = (counts + TR_MOE - 1) // TR_MOE
    tile_end = jnp.cumsum(tiles_per_e)
    row_off = (tile_end - tiles_per_e) * TR_MOE
    dest = row_off[flat_e] + rank
    token = jnp.arange(TOP_K * t, dtype=jnp.int32) // TOP_K
    row_token = jnp.zeros((n_tiles * TR_MOE,), jnp.int32).at[dest].set(token, unique_indices=True)
    n_valid = tile_end[-1:].astype(jnp.int32)
    tile_expert = jnp.searchsorted(tile_end, jnp.arange(n_tiles), side="right").astype(jnp.int32)
    last_e = jnp.searchsorted(tile_end, n_valid[0] - 1, side="right").astype(jnp.int32)
    tile_expert = jnp.where(jnp.arange(n_tiles) < n_valid[0], tile_expert, last_e)
    tok_rows = dest.reshape(t // TM_COMB, TM_COMB, TOP_K).transpose(0, 2, 1).reshape(-1).astype(jnp.int32)
    return tile_expert, n_valid, row_token, tok_rows


def _permute_w_in(w):
    sizes = (256, 256, 256, 512, 768, 8, 256, 256, 256, 256)
    offs = np.concatenate([[0], np.cumsum(sizes)])
    part = lambda n: w[:, offs[n]:offs[n + 1]]
    dt_pad = jnp.zeros((w.shape[0], LANES - SSM_HEADS), w.dtype)
    cols = [part(0) * (DA_HEAD_DIM ** -0.5 * LOG2E), part(2), part(1), part(3), part(4),
            part(6), part(7), part(8), part(9), part(5), dt_pad]
    return jnp.concatenate(cols, axis=1).astype(BF16)


def _column_tiles(w, tile):
    *lead, k, n = w.shape
    return jnp.moveaxis(w.reshape(*lead, k, n // tile, tile), -2, -3)


def _pad_lanes(v, fill=0.0):
    return jnp.concatenate([v, jnp.full((LANES - v.shape[0],), fill, v.dtype)])[None, :]


def kernel(x, w_in, w_out, attn_norm, ffn_norm, final_norm, rel_bias, lambda_q1, lambda_k1, lambda_q2,
           lambda_k2, da_head_norm, conv_w, conv_b, dt_bias, a_log, d_skip, ssm_norm, ret_head_norm,
           w_gate, w_up, w_down, router_w, e_gate, e_up, e_down):
    batch, seq, _ = x.shape
    x2 = x.reshape(batch * seq, D_MODEL)
    bias_tiles = _bias_tiles(rel_bias)
    ret_tabs = _retention_tables(seq)

    for layer in range(DEPTH):
        lam_init = 0.8 - 0.6 * math.exp(-0.3 * layer)
        lam = (jnp.exp(jnp.sum(lambda_q1[layer] * lambda_k1[layer]))
               - jnp.exp(jnp.sum(lambda_q2[layer] * lambda_k2[layer])) + lam_init).astype(F32)
        lam = jnp.full((1, ATT_T), lam, F32)

        qkv, qvt, pf = _inproj(x2, attn_norm[layer][None, :], _permute_w_in(w_in[layer]))

        da_gain = (jnp.tile(da_head_norm[layer], DA_HEADS) * (1.0 - lam_init))[:, None]
        out_a = _attention(qkv, qvt, bias_tiles, da_gain, lam, batch, seq)

        cw, cb = conv_w[layer], conv_b[layer]
        out_b = _ssd(pf, cw[:, :SSM_WIDTH], cw[:, SSM_WIDTH:], cb[None, :SSM_WIDTH], cb[None, SSM_WIDTH:],
                     _pad_lanes(dt_bias[layer]), _pad_lanes(a_log[layer]),
                     jnp.repeat(d_skip[layer], SSM_HEAD_DIM)[None, :], ssm_norm[layer][None, :], batch, seq)

        ret_gain = jnp.tile(ret_head_norm[layer], RET_HEADS)[None, :]
        out_c = _retention(pf, *ret_tabs, ret_gain, batch, seq)

        w_o = w_out[layer].astype(BF16)
        if layer % 2 == 0:
            i = layer // 2
            x2, h = _outproj(x2, out_a, out_b, out_c, w_o, ffn_norm[layer][None, :])
            pad = FFN_PAD - FFN_DENSE
            wg = _column_tiles(jnp.pad(w_gate[i], ((0, 0), (0, pad))).astype(BF16), TF_FFN)
            wu = _column_tiles(jnp.pad(w_up[i], ((0, 0), (0, pad))).astype(BF16), TF_FFN)
            wd = jnp.pad(w_down[i], ((0, pad), (0, 0))).astype(BF16)
            x2 = _dense_ffn(x2, h, wg, wu, wd)
        else:
            i = layer // 2
            rw = jnp.pad(router_w[i], ((0, 0), (0, LANES - N_EXPERTS))).astype(BF16)
            x2, h, rinfo = _outproj(x2, out_a, out_b, out_c, w_o, ffn_norm[layer][None, :], rw)
            tile_expert, n_valid, row_token, tok_rows = _routing_tables(rinfo)
            y_rows = _moe_ffn(h, tile_expert, n_valid, row_token,
                              _column_tiles(e_gate[i].astype(BF16), TF_MOE),
                              _column_tiles(e_up[i].astype(BF16), TF_MOE), e_down[i].astype(BF16))
            x2 = _moe_combine(x2, rinfo, y_rows, tok_rows, final_norm[None, :])
    return x2.reshape(batch, seq, D_MODEL)
```

```python
import functools
import math

import jax
import jax.numpy as jnp
import numpy as np
from jax import lax
from jax.experimental import pallas as pl
from jax.experimental.pallas import tpu as pltpu

F32 = jnp.float32
BF16 = jnp.bfloat16

D_MODEL = 1024
DEPTH = 2
CHUNK = 64
EPS = 1e-6
DA_HEADS = 4
DA_HEAD_DIM = 32
DA_V_DIM = 64
DA_WIDTH = 256
SSM_HEADS = 8
SSM_HEAD_DIM = 64
SSM_WIDTH = 512
SSM_STATE = 64
SSM_GROUPS = 2
SSM_CONV = 4
RET_HEADS = 4
RET_KEY_DIM = 64
RET_WIDTH = 256
ROPE_BASE = 10000.0
REL_BUCKETS = 32
REL_MAX_DIST = 128
FFN_DENSE = 2752
N_EXPERTS = 8
FFN_EXPERT = 3584

LANES = 128
VMEM_LIMIT = 48 * 1024 * 1024
NEG_BIG = -1e30

TM_PROJ = 512
ATT_T = 512
LOG2E = math.log2(math.e)
BLK = 256
FFN_PAD = 2816
TM_FFN = 512
TF_FFN = 1408
TF_MOE = 512
TR_MOE = 896
TM_COMB = 256
TOP_K = 2

C_Z, C_XS, C_BC, C_RQ, C_RK, C_RV, C_RG, C_DT = 0, 512, 1024, 1280, 1536, 1792, 2048, 2304
PF_W = 2432
PF_CHUNK = 640
QKV_W = 768


def _cparams(sem):
    return pltpu.CompilerParams(dimension_semantics=sem, vmem_limit_bytes=VMEM_LIMIT)


def _group_mask(shape, axis, group, idx):
    io = lax.broadcasted_iota(jnp.int32, shape, axis)
    return (io >= idx * group) & (io < (idx + 1) * group)


def _lane_tile(x, n, axis=1):
    return jnp.concatenate([x] * n, axis=axis)


def _silu(x):
    h = 0.5 * x
    return h + h * jnp.tanh(h)


def _inproj_kernel(x_ref, g_ref, w_ref, qkv_ref, qvt_ref, pf_ref):
    x = x_ref[...]
    ms = jnp.mean(x * x, axis=-1, keepdims=True)
    h = (x * lax.rsqrt(ms + EPS) * g_ref[...]).astype(BF16)
    qkv = jnp.dot(h, w_ref[:, :QKV_W], preferred_element_type=F32)
    qkv_ref[...] = qkv.astype(BF16)
    qvt_ref[...] = qkv[:, :2 * DA_WIDTH].T.astype(BF16)
    for lo in range(0, PF_W, PF_CHUNK):
        hi = min(lo + PF_CHUNK, PF_W)
        pf_ref[:, lo:hi] = jnp.dot(h, w_ref[:, QKV_W + lo:QKV_W + hi], preferred_element_type=F32)


def _inproj(x2, gain, w):
    t = x2.shape[0]
    return pl.pallas_call(
        _inproj_kernel,
        grid=(t // TM_PROJ,),
        in_specs=[
            pl.BlockSpec((TM_PROJ, D_MODEL), lambda i: (i, 0)),
            pl.BlockSpec((1, D_MODEL), lambda i: (0, 0)),
            pl.BlockSpec((D_MODEL, QKV_W + PF_W), lambda i: (0, 0)),
        ],
        out_specs=[
            pl.BlockSpec((TM_PROJ, QKV_W), lambda i: (i, 0)),
            pl.BlockSpec((2 * DA_WIDTH, TM_PROJ), lambda i: (0, i)),
            pl.BlockSpec((TM_PROJ, PF_W), lambda i: (i, 0)),
        ],
        out_shape=[
            jax.ShapeDtypeStruct((t, QKV_W), BF16),
            jax.ShapeDtypeStruct((2 * DA_WIDTH, t), BF16),
            jax.ShapeDtypeStruct((t, PF_W), F32),
        ],
        compiler_params=_cparams(("parallel",)),
        name="inproj",
    )(x2, gain, w)


def _attn_kernel(qi_ref, kj_ref, qt_ref, k_ref, vt_ref, bias_ref, gain_ref, lam_ref, o_ref,
                 qm_ref, m_ref, l_ref, acc_ref, s_ref, e_ref, al_ref):
    p = pl.program_id(1)
    qi = qi_ref[p]
    kj = kj_ref[p]
    n_ht = 2 * DA_HEADS

    @pl.when(kj == 0)
    def _init():
        qt = qt_ref[...]
        for ht in range(n_ht):
            qm_ref[ht] = jnp.where(_group_mask(qt.shape, 0, DA_HEAD_DIM, ht), qt, jnp.zeros_like(qt))
        m_ref[...] = jnp.full(m_ref.shape, NEG_BIG, F32)
        l_ref[...] = jnp.zeros(l_ref.shape, F32)
        acc_ref[...] = jnp.zeros(acc_ref.shape, F32)

    def step(with_bias):
        k = k_ref[...]
        def scores(a, buf):
            s_ref[buf] = jnp.dot(k, qm_ref[a], preferred_element_type=F32)

        def softmax(b, buf):
            for c0 in range(0, ATT_T, LANES):
                cols = slice(c0, c0 + LANES)

                def block():
                    blk = s_ref[buf, :, cols]
                    return blk + bias_ref[0, b // 2, :, cols] if with_bias else blk

                m_prev = m_ref[b, :, cols]
                m_new = jnp.maximum(m_prev, jnp.max(block(), axis=0, keepdims=True))
                alpha = jnp.exp2(m_prev - m_new)
                e = jnp.exp2(block() - m_new)
                l_ref[b, :, cols] = alpha * l_ref[b, :, cols] + jnp.sum(e, axis=0, keepdims=True)
                m_ref[b, :, cols] = m_new
                al_ref[buf, :, cols] = alpha
                e_ref[buf, :, cols] = e.astype(BF16)

        def values(c, buf):
            r0 = (c // 4) * LANES
            if not isinstance(r0, int):
                r0 = pl.multiple_of(r0, LANES)
            pv = jnp.dot(vt_ref[pl.ds(r0, LANES), :], e_ref[buf], preferred_element_type=F32)
            acc_ref[c] = acc_ref[c] * al_ref[buf] + pv

        scores(0, 0)
        scores(1, 1)
        softmax(0, 0)

        def trip(u, carry):
            t = 2 * u + 1
            scores(t + 1, 0)
            softmax(t, 1)
            values(t - 1, 0)
            scores(t + 2, 1)
            softmax(t + 1, 0)
            values(t, 1)
            return carry

        for u in range((n_ht - 2) // 2):
            trip(u, 0)
        softmax(n_ht - 1, 1)
        values(n_ht - 2, 0)
        values(n_ht - 1, 1)

    @pl.when(kj >= qi - 1)
    def _near():
        step(True)

    @pl.when(kj < qi - 1)
    def _far():
        step(False)

    @pl.when(kj == qi)
    def _finish():
        lam = lam_ref[...]
        row_lo = lax.broadcasted_iota(jnp.int32, (LANES, ATT_T), 0) < DA_V_DIM
        for pair in range(DA_HEADS // 2):
            halves = []
            for sub in range(2):
                h = 2 * pair + sub
                o = acc_ref[2 * h] * (1.0 / l_ref[2 * h]) - lam * (acc_ref[2 * h + 1] * (1.0 / l_ref[2 * h + 1]))
                own = row_lo if sub == 0 else jnp.logical_not(row_lo)
                ms = jnp.sum(jnp.where(own, o * o, 0.0), axis=0, keepdims=True) * (1.0 / DA_V_DIM)
                halves.append(o * lax.rsqrt(ms + EPS))
            r0 = pair * LANES
            blk = jnp.where(row_lo, halves[0], halves[1]) * gain_ref[r0:r0 + LANES, :]
            o_ref[:, r0:r0 + LANES] = blk.T.astype(BF16)


def _attention(qkv, qvt, bias_tiles, gain, lam, batch, seq):
    nq = seq // ATT_T
    qi_l, kj_l = [], []
    for qi in range(nq):
        for kj in range(qi + 1):
            qi_l.append(qi)
            kj_l.append(kj)
    qi_tbl = jnp.asarray(np.array(qi_l, np.int32))
    kj_tbl = jnp.asarray(np.array(kj_l, np.int32))
    n_pairs = len(qi_l)
    n_ht = 2 * DA_HEADS

    grid_spec = pltpu.PrefetchScalarGridSpec(
        num_scalar_prefetch=2,
        grid=(batch, n_pairs),
        in_specs=[
            pl.BlockSpec((DA_WIDTH, ATT_T), lambda b, p, qi, kj: (0, b * nq + qi[p])),
            pl.BlockSpec((ATT_T, DA_WIDTH), lambda b, p, qi, kj: (b * nq + kj[p], 2)),
            pl.BlockSpec((DA_WIDTH, ATT_T), lambda b, p, qi, kj: (1, b * nq + kj[p])),
            pl.BlockSpec((1, DA_HEADS, ATT_T, ATT_T),
                         lambda b, p, qi, kj: (jnp.minimum(qi[p] - kj[p], 1), 0, 0, 0)),
            pl.BlockSpec((DA_WIDTH, 1), lambda b, p, qi, kj: (0, 0)),
            pl.BlockSpec((1, ATT_T), lambda b, p, qi, kj: (0, 0)),
        ],
        out_specs=pl.BlockSpec((ATT_T, DA_WIDTH), lambda b, p, qi, kj: (b * nq + qi[p], 0)),
        scratch_shapes=[
            pltpu.VMEM((n_ht, DA_WIDTH, ATT_T), BF16),
            pltpu.VMEM((n_ht, 1, ATT_T), F32),
            pltpu.VMEM((n_ht, 1, ATT_T), F32),
            pltpu.VMEM((n_ht, LANES, ATT_T), F32),
            pltpu.VMEM((2, ATT_T, ATT_T), F32),
            pltpu.VMEM((2, ATT_T, ATT_T), BF16),
            pltpu.VMEM((2, 1, ATT_T), F32),
        ],
    )
    return pl.pallas_call(
        _attn_kernel,
        grid_spec=grid_spec,
        out_shape=jax.ShapeDtypeStruct((batch * seq, DA_WIDTH), BF16),
        compiler_params=_cparams(("parallel", "arbitrary")),
        name="diff_attention",
    )(qi_tbl, kj_tbl, qvt, qkv, qvt, bias_tiles, gain, lam)


def _t5_bucket(rel):
    half = REL_BUCKETS // 2
    max_exact = half // 2
    ret = jnp.where(rel > 0, half, 0)
    n = jnp.abs(rel)
    nf = jnp.maximum(n, 1).astype(F32)
    large = max_exact + (jnp.log(nf / max_exact) / math.log(REL_MAX_DIST / max_exact)
                         * (half - max_exact)).astype(jnp.int32)
    large = jnp.minimum(large, half - 1)
    return ret + jnp.where(n < max_exact, n, large)


def _bias_tiles(rel_bias):
    i = jnp.arange(ATT_T)[None, :]
    j = jnp.arange(ATT_T)[:, None]
    far = rel_bias[_t5_bucket(jnp.int32(-2 * ATT_T))].astype(F32)
    tiles = []
    for off in range(2):
        bucket = _t5_bucket((j - off * ATT_T) - i)
        b = jnp.zeros((DA_HEADS, ATT_T, ATT_T), F32)
        for c in range(REL_BUCKETS):
            b = jnp.where((bucket == c)[None], rel_bias[c].astype(F32)[:, None, None], b)
        b = (b - far[:, None, None]) * LOG2E
        if off == 0:
            allowed = (j // CHUNK) <= (i // CHUNK)
            b = jnp.where(allowed[None], b, NEG_BIG)
        tiles.append(b)
    return jnp.stack(tiles)


def _cumsum_rows(x):
    n = x.shape[0]
    row = lax.broadcasted_iota(jnp.int32, x.shape, 0)
    sh = 1
    while sh < n:
        x = x + jnp.where(row >= sh, pltpu.roll(x, sh, axis=0), 0.0)
        sh *= 2
    return x


def _expand_heads(v, width, group):
    out = jnp.zeros((v.shape[0], width), F32)
    for h in range(width // group):
        col = jnp.broadcast_to(v[:, h:h + 1], (v.shape[0], width))
        out = jnp.where(_group_mask(out.shape, 1, group, h), col, out)
    return out


def _ssd_kernel(z_ref, xs_ref, bc_ref, dt_ref, cwx_ref, cwb_ref, cbx_ref, cbb_ref, dtb_ref, alog_ref,
                dskip_ref, ng_ref, o_ref, extx_ref, extb_ref, st_ref):
    j = pl.program_id(1)

    @pl.when(j == 0)
    def _reset():
        extx_ref[0:8, :] = jnp.zeros((8, SSM_WIDTH), F32)
        extb_ref[0:8, :] = jnp.zeros((8, 2 * LANES), F32)
        st_ref[...] = jnp.zeros(st_ref.shape, F32)

    extx_ref[8:8 + BLK, :] = xs_ref[...]
    extb_ref[8:8 + BLK, :] = bc_ref[...]
    yx = jnp.broadcast_to(cbx_ref[...], (BLK, SSM_WIDTH))
    yb = jnp.broadcast_to(cbb_ref[...], (BLK, 2 * LANES))
    for w in range(SSM_CONV):
        lo = 8 - (SSM_CONV - 1) + w
        yx = yx + extx_ref[lo:lo + BLK, :] * cwx_ref[w:w + 1, :]
        yb = yb + extb_ref[lo:lo + BLK, :] * cwb_ref[w:w + 1, :]
    tail_x = extx_ref[BLK:BLK + 8, :]
    tail_b = extb_ref[BLK:BLK + 8, :]
    extx_ref[0:8, :] = tail_x
    extb_ref[0:8, :] = tail_b
    xs = _silu(yx)
    bc = _silu(yb)
    bm = bc[:, :LANES]
    cm = bc[:, LANES:]

    dt = jax.nn.softplus(dt_ref[...] + dtb_ref[...])
    a = -jnp.exp(alog_ref[...]) * dt
    a_cs = _cumsum_rows(a)
    a_cs_t = a_cs.T
    a_last = a_cs[BLK - 1:BLK, :]

    dt_full = _expand_heads(dt, SSM_WIDTH, SSM_HEAD_DIM)
    ea_full = _expand_heads(jnp.exp(a_cs), SSM_WIDTH, SSM_HEAD_DIM)
    dec_full = _expand_heads(jnp.exp(a_last - a_cs), SSM_WIDTH, SSM_HEAD_DIM)
    cdec_full = _expand_heads(jnp.exp(a_last), SSM_WIDTH, SSM_HEAD_DIM)

    xdt = xs * dt_full
    xdt_b = xdt.astype(BF16)
    cm_b = cm.astype(BF16)
    bm_b = bm.astype(BF16)

    st = st_ref[...]
    y = jnp.dot(cm_b, st.astype(BF16), preferred_element_type=F32) * ea_full + xs * dskip_ref[...]

    row = lax.broadcasted_iota(jnp.int32, (BLK, BLK), 0)
    colio = lax.broadcasted_iota(jnp.int32, (BLK, BLK), 1)
    causal = colio <= row
    rep = SSM_HEADS // SSM_GROUPS
    lane_lo = lax.broadcasted_iota(jnp.int32, (BLK, LANES), 1) < SSM_HEAD_DIM
    pieces = []
    for g in range(SSM_GROUPS):
        cg = jnp.where(_group_mask(cm_b.shape, 1, SSM_STATE, g), cm_b, jnp.zeros_like(cm_b))
        cb = lax.dot_general(cg, bm_b, (((1,), (1,)), ((), ())), preferred_element_type=F32)
        for pair in range(rep // 2):
            halves = []
            for sub in range(2):
                h = g * rep + pair * 2 + sub
                seg = jnp.broadcast_to(a_cs[:, h:h + 1], (BLK, BLK)) - a_cs_t[h:h + 1, :]
                lmat = jnp.exp(jnp.where(causal, seg, NEG_BIG))
                mh = (cb * lmat).astype(BF16)
                c0 = (h // 2) * LANES
                halves.append(jnp.dot(mh, xdt_b[:, c0:c0 + LANES], preferred_element_type=F32))
            pieces.append(jnp.where(lane_lo, halves[0], halves[1]))
    y = y + jnp.concatenate(pieces, axis=1)

    upd = jnp.dot(bm.T.astype(BF16), (xdt * dec_full).astype(BF16), preferred_element_type=F32)
    r_io = lax.broadcasted_iota(jnp.int32, upd.shape, 0) // SSM_STATE
    c_io = lax.broadcasted_iota(jnp.int32, upd.shape, 1) // (SSM_HEAD_DIM * rep)
    st_ref[...] = st * cdec_full + jnp.where(r_io == c_io, upd, 0.0)

    y = y * _silu(z_ref[...])
    gw = SSM_WIDTH // SSM_GROUPS
    for g in range(SSM_GROUPS):
        yg = y[:, g * gw:(g + 1) * gw]
        ms = jnp.mean(yg * yg, axis=-1, keepdims=True)
        o_ref[:, g * gw:(g + 1) * gw] = (yg * lax.rsqrt(ms + EPS) * ng_ref[:, g * gw:(g + 1) * gw]).astype(BF16)


def _ssd(pf, cwx, cwb, cbx, cbb, dtb, alog, dskip, ng, batch, seq):
    nb = seq // BLK
    row = lambda w, c: pl.BlockSpec((BLK, w), lambda b, j: (b * nb + j, c))
    const = lambda r, w: pl.BlockSpec((r, w), lambda b, j: (0, 0))
    return pl.pallas_call(
        _ssd_kernel,
        grid=(batch, nb),
        in_specs=[
            row(SSM_WIDTH, C_Z // SSM_WIDTH), row(SSM_WIDTH, C_XS // SSM_WIDTH), row(2 * LANES, C_BC // (2 * LANES)),
            row(LANES, C_DT // LANES),
            const(SSM_CONV, SSM_WIDTH), const(SSM_CONV, 2 * LANES), const(1, SSM_WIDTH), const(1, 2 * LANES),
            const(1, LANES), const(1, LANES), const(1, SSM_WIDTH), const(1, SSM_WIDTH),
        ],
        out_specs=pl.BlockSpec((BLK, SSM_WIDTH), lambda b, j: (b * nb + j, 0)),
        out_shape=jax.ShapeDtypeStruct((batch * seq, SSM_WIDTH), BF16),
        scratch_shapes=[
            pltpu.VMEM((BLK + 8, SSM_WIDTH), F32),
            pltpu.VMEM((BLK + 8, 2 * LANES), F32),
            pltpu.VMEM((SSM_GROUPS * SSM_STATE, SSM_WIDTH), F32),
        ],
        compiler_params=_cparams(("parallel", "arbitrary")),
        name="ssd_mixer",
    )(pf, pf, pf, pf, cwx, cwb, cbx, cbb, dtb, alog, dskip, ng)


def _rotary(u, cos, sin_signed):
    first = (lax.broadcasted_iota(jnp.int32, u.shape, 1) % RET_KEY_DIM) < (RET_KEY_DIM // 2)
    half = RET_KEY_DIM // 2
    swapped = jnp.where(first, pltpu.roll(u, u.shape[1] - half, axis=1), pltpu.roll(u, half, axis=1))
    return u * cos + swapped * sin_signed


def _ret_kernel(q_ref, k_ref, v_ref, g_ref, cos_ref, sin_ref, dmat_ref, qdec_ref, kdec_ref, gdec_ref,
                bd_ref, gain_ref, o_ref, st_ref):
    j = pl.program_id(1)

    @pl.when(j == 0)
    def _reset():
        st_ref[...] = jnp.zeros(st_ref.shape, F32)

    cos = _lane_tile(cos_ref[...], RET_WIDTH // LANES)
    sin = _lane_tile(sin_ref[...], RET_WIDTH // LANES)
    q = _rotary(q_ref[...], cos, sin)
    k = _rotary(k_ref[...], cos, sin) * (RET_KEY_DIM ** -0.5)
    v_b = v_ref[...].astype(BF16)
    q_b = q.astype(BF16)
    k_b = k.astype(BF16)

    st = st_ref[...]
    o = jnp.dot((q * qdec_ref[...]).astype(BF16), st.astype(BF16), preferred_element_type=F32)
    for h in range(RET_HEADS):
        cm = _group_mask(q_b.shape, 1, RET_KEY_DIM, h)
        qh = jnp.where(cm, q_b, jnp.zeros_like(q_b))
        s = lax.dot_general(qh, k_b, (((1,), (1,)), ((), ())), preferred_element_type=F32) * dmat_ref[h]
        oh = jnp.dot(s.astype(BF16), v_b, preferred_element_type=F32)
        o = o + jnp.where(cm, oh, 0.0)

    upd = jnp.dot((k * kdec_ref[...]).T.astype(BF16), v_b, preferred_element_type=F32)
    st_ref[...] = st * gdec_ref[...] + upd * bd_ref[...]

    out = jnp.zeros(o.shape, F32)
    for h in range(RET_HEADS):
        cm = _group_mask(o.shape, 1, RET_KEY_DIM, h)
        ms = jnp.sum(jnp.where(cm, o * o, 0.0), axis=-1, keepdims=True) * (1.0 / RET_KEY_DIM)
        out = out + jnp.where(cm, o * lax.rsqrt(ms + EPS), 0.0)
    o_ref[...] = (_silu(g_ref[...]) * (out * gain_ref[...])).astype(BF16)


def _retention(pf, cos, sin, dmat, qdec, kdec, gdec, bd, gain, batch, seq):
    nb = seq // BLK
    row = lambda c: pl.BlockSpec((BLK, RET_WIDTH), lambda b, j: (b * nb + j, c))
    const2 = pl.BlockSpec((BLK, RET_WIDTH), lambda b, j: (0, 0))
    return pl.pallas_call(
        _ret_kernel,
        grid=(batch, nb),
        in_specs=[
            row(C_RQ // RET_WIDTH), row(C_RK // RET_WIDTH), row(C_RV // RET_WIDTH), row(C_RG // RET_WIDTH),
            pl.BlockSpec((BLK, LANES), lambda b, j: (j, 0)),
            pl.BlockSpec((BLK, LANES), lambda b, j: (j, 0)),
            pl.BlockSpec((RET_HEADS, BLK, BLK), lambda b, j: (0, 0, 0)),
            const2, const2, const2, const2,
            pl.BlockSpec((1, RET_WIDTH), lambda b, j: (0, 0)),
        ],
        out_specs=pl.BlockSpec((BLK, RET_WIDTH), lambda b, j: (b * nb + j, 0)),
        out_shape=jax.ShapeDtypeStruct((batch * seq, RET_WIDTH), BF16),
        scratch_shapes=[pltpu.VMEM((RET_WIDTH, RET_WIDTH), F32)],
        compiler_params=_cparams(("parallel", "arbitrary")),
        name="retention",
    )(pf, pf, pf, pf, cos, sin, dmat, qdec, kdec, gdec, bd, gain)


def _retention_tables(seq):
    f32 = np.float32
    inv = (f32(1.0) / (f32(ROPE_BASE) ** (np.arange(0, RET_KEY_DIM, 2, dtype=f32) / f32(RET_KEY_DIM)))).astype(f32)
    ang = (np.arange(seq, dtype=f32)[:, None] * inv[None, :]).astype(f32)
    cos_h = np.concatenate([np.cos(ang), np.cos(ang)], axis=-1)
    sin_h = np.concatenate([-np.sin(ang), np.sin(ang)], axis=-1)
    cos = np.tile(cos_h, (1, 2)).astype(f32)
    sin = np.tile(sin_h, (1, 2)).astype(f32)
    log_gamma = np.log1p(-np.power(f32(2.0), f32(-5.0) - np.arange(RET_HEADS, dtype=f32))).astype(f32)
    idx = np.arange(BLK, dtype=f32)
    ii = np.arange(BLK)
    same_or_earlier_chunk = (ii[None, :] // CHUNK) <= (ii[:, None] // CHUNK)
    dmat = np.exp(log_gamma[:, None, None] * np.abs(idx[:, None] - idx[None, :]))
    dmat = np.where(same_or_earlier_chunk[None], dmat, 0.0).astype(f32)
    lg_cols = np.repeat(log_gamma, RET_KEY_DIM)
    qdec = np.exp(lg_cols[None, :] * (idx + 1.0)[:, None]).astype(f32)
    kdec = np.exp(lg_cols[None, :] * (BLK - 1.0 - idx)[:, None]).astype(f32)
    head_of = np.arange(RET_WIDTH) // RET_KEY_DIM
    bd = (head_of[:, None] == head_of[None, :]).astype(f32)
    gdec = (np.exp(lg_cols * f32(BLK))[:, None] * bd).astype(f32)
    return tuple(jnp.asarray(a) for a in (cos, sin, dmat, qdec, kdec, gdec, bd))


def _outproj_kernel(x_ref, a_ref, b_ref, c_ref, w_ref, g_ref, *rest, with_router):
    if with_router:
        rw_ref, xo_ref, h_ref, rinfo_ref = rest
    else:
        xo_ref, h_ref = rest
    acc = x_ref[...]
    acc = acc + jnp.dot(a_ref[...], w_ref[0:DA_WIDTH, :], preferred_element_type=F32)
    acc = acc + jnp.dot(b_ref[...], w_ref[DA_WIDTH:DA_WIDTH + SSM_WIDTH, :], preferred_element_type=F32)
    acc = acc + jnp.dot(c_ref[...], w_ref[DA_WIDTH + SSM_WIDTH:, :], preferred_element_type=F32)
    xo_ref[...] = acc
    ms = jnp.mean(acc * acc, axis=-1, keepdims=True)
    hf = acc * lax.rsqrt(ms + EPS) * g_ref[...]
    h = hf.astype(BF16)
    h_ref[...] = hf.astype(h_ref.dtype)
    if with_router:
        logits = jnp.dot(h, rw_ref[...], preferred_element_type=F32)
        lane = lax.broadcasted_iota(jnp.int32, logits.shape, 1).astype(F32)
        logits = jnp.where(lane < N_EXPERTS, logits, NEG_BIG)
        v1 = jnp.max(logits, axis=-1, keepdims=True)
        i1 = jnp.min(jnp.where(logits == v1, lane, float(LANES)), axis=-1, keepdims=True)
        rest_l = jnp.where(lane == i1, NEG_BIG, logits)
        v2 = jnp.max(rest_l, axis=-1, keepdims=True)
        i2 = jnp.min(jnp.where(rest_l == v2, lane, float(LANES)), axis=-1, keepdims=True)
        e2 = jnp.exp(v2 - v1)
        g1 = 1.0 / (1.0 + e2)
        g2 = e2 * g1
        rinfo_ref[...] = (jnp.where(lane == 0.0, g1, 0.0) + jnp.where(lane == 1.0, g2, 0.0)
                          + jnp.where(lane == 2.0, i1, 0.0) + jnp.where(lane == 3.0, i2, 0.0))


def _outproj(x2, oa, ob, oc, w, gain, router_w=None):
    t = x2.shape[0]
    with_router = router_w is not None
    tok = lambda w_: pl.BlockSpec((TM_PROJ, w_), lambda i: (i, 0))
    in_specs = [tok(D_MODEL), tok(DA_WIDTH), tok(SSM_WIDTH), tok(RET_WIDTH),
                pl.BlockSpec((D_MODEL, D_MODEL), lambda i: (0, 0)),
                pl.BlockSpec((1, D_MODEL), lambda i: (0, 0))]
    out_specs = [tok(D_MODEL), tok(D_MODEL)]
    h_dtype = F32 if with_router else BF16
    out_shape = [jax.ShapeDtypeStruct((t, D_MODEL), F32), jax.ShapeDtypeStruct((t, D_MODEL), h_dtype)]
    args = [x2, oa, ob, oc, w, gain]
    if with_router:
        in_specs.append(pl.BlockSpec((D_MODEL, LANES), lambda i: (0, 0)))
        out_specs.append(tok(LANES))
        out_shape.append(jax.ShapeDtypeStruct((t, LANES), F32))
        args.append(router_w)
    return pl.pallas_call(
        functools.partial(_outproj_kernel, with_router=with_router),
        grid=(t // TM_PROJ,),
        in_specs=in_specs,
        out_specs=out_specs,
        out_shape=out_shape,
        compiler_params=_cparams(("parallel",)),
        name="outproj_router" if with_router else "outproj",
    )(*args)


def _ffn_kernel(x_ref, h_ref, wg_ref, wu_ref, wd_ref, o_ref, acc_ref):
    f = pl.program_id(1)

    @pl.when(f == 0)
    def _init():
        acc_ref[...] = x_ref[...]

    h = h_ref[...]
    gate = jnp.dot(h, wg_ref[0], preferred_element_type=F32)
    up = jnp.dot(h, wu_ref[0], preferred_element_type=F32)
    act = (_silu(gate) * up).astype(BF16)
    acc_ref[...] += jnp.dot(act, wd_ref[...], preferred_element_type=F32)

    @pl.when(f == pl.num_programs(1) - 1)
    def _done():
        o_ref[...] = acc_ref[...]


def _dense_ffn(x2, h, wg, wu, wd):
    t = x2.shape[0]
    nf = wg.shape[0]
    return pl.pallas_call(
        _ffn_kernel,
        grid=(t // TM_FFN, nf),
        in_specs=[
            pl.BlockSpec((TM_FFN, D_MODEL), lambda i, f: (i, 0)),
            pl.BlockSpec((TM_FFN, D_MODEL), lambda i, f: (i, 0)),
            pl.BlockSpec((1, D_MODEL, TF_FFN), lambda i, f: (f, 0, 0)),
            pl.BlockSpec((1, D_MODEL, TF_FFN), lambda i, f: (f, 0, 0)),
            pl.BlockSpec((TF_FFN, D_MODEL), lambda i, f: (f, 0)),
        ],
        out_specs=pl.BlockSpec((TM_FFN, D_MODEL), lambda i, f: (i, 0)),
        out_shape=jax.ShapeDtypeStruct((t, D_MODEL), F32),
        scratch_shapes=[pltpu.VMEM((TM_FFN, D_MODEL), F32)],
        compiler_params=_cparams(("parallel", "arbitrary")),
        name="dense_swiglu",
    )(x2, h, wg, wu, wd)


def _issue_row_gather(src_hbm, idx_ref, idx_base, dst, sem, n_rows):
    def body(r, carry):
        tok = idx_ref[idx_base + r]
        pltpu.make_async_copy(src_hbm.at[pl.ds(tok, 1)], dst.at[pl.ds(r, 1)], sem).start()
        return carry
    lax.fori_loop(0, n_rows, body, 0, unroll=8)


def _wait_row_gather(src_hbm, dst, sem, n_rows):
    pltpu.make_async_copy(src_hbm.at[pl.ds(0, n_rows)], dst, sem).wait()


def _moe_ffn_kernel(te_ref, nv_ref, rt_ref, h_hbm, wg_ref, wu_ref, wd_ref, y_ref, hbuf, xb_ref, acc_ref, sem):
    i = pl.program_id(0)
    f = pl.program_id(1)
    n_valid = nv_ref[0]
    valid = i < n_valid
    slot = i % 2

    @pl.when((f == 0) & (i == 0) & valid)
    def _first():
        _issue_row_gather(h_hbm, rt_ref, 0, hbuf.at[0], sem.at[0], TR_MOE)

    @pl.when((f == 0) & valid)
    def _arrive():
        _wait_row_gather(h_hbm, hbuf.at[slot], sem.at[slot], TR_MOE)
        xb_ref[...] = hbuf[slot].astype(BF16)

    def compute(prefetch_next):
        if prefetch_next:
            rows_per_step = TR_MOE // (FFN_EXPERT // TF_MOE)
            r0 = f * rows_per_step
            for j in range(rows_per_step):
                tok = rt_ref[(i + 1) * TR_MOE + r0 + j]
                pltpu.make_async_copy(h_hbm.at[pl.ds(tok, 1)], hbuf.at[1 - slot, pl.ds(r0 + j, 1)],
                                      sem.at[1 - slot]).start()
        xb = xb_ref[...]
        gate = jnp.dot(xb, wg_ref[0], preferred_element_type=F32)
        up = jnp.dot(xb, wu_ref[0], preferred_element_type=F32)
        act = (_silu(gate) * up).astype(BF16)
        contrib = jnp.dot(act, wd_ref[0], preferred_element_type=F32)

        @pl.when(f == 0)
        def _():
            acc_ref[...] = contrib

        @pl.when(f > 0)
        def _():
            acc_ref[...] += contrib

    has_next = i + 1 < n_valid

    @pl.when(valid & has_next)
    def _compute_and_prefetch():
        compute(True)

    @pl.when(valid & jnp.logical_not(has_next))
    def _compute_last():
        compute(False)

    @pl.when(f == pl.num_programs(1) - 1)
    def _store():
        y_ref[...] = jnp.where(valid, acc_ref[...], 0.0)


def _moe_ffn(h, tile_expert, n_valid, row_token, wg, wu, wd):
    n_tiles = tile_expert.shape[0]
    nf = wg.shape[2] // TF_MOE

    def w_in_map(i, f, te, nv, rt):
        return (te[i], 0, jnp.where(i < nv[0], f, nf - 1))

    def w_out_map(i, f, te, nv, rt):
        return (te[i], jnp.where(i < nv[0], f, nf - 1), 0)

    grid_spec = pltpu.PrefetchScalarGridSpec(
        num_scalar_prefetch=3,
        grid=(n_tiles, nf),
        in_specs=[
            pl.BlockSpec(memory_space=pl.ANY),
            pl.BlockSpec((1, D_MODEL, TF_MOE), w_in_map),
            pl.BlockSpec((1, D_MODEL, TF_MOE), w_in_map),
            pl.BlockSpec((1, TF_MOE, D_MODEL), w_out_map),
        ],
        out_specs=pl.BlockSpec((TR_MOE, D_MODEL), lambda i, f, te, nv, rt: (i, 0)),
        scratch_shapes=[
            pltpu.VMEM((2, TR_MOE, D_MODEL), F32),
            pltpu.VMEM((TR_MOE, D_MODEL), BF16),
            pltpu.VMEM((TR_MOE, D_MODEL), F32),
            pltpu.SemaphoreType.DMA((2,)),
        ],
    )
    return pl.pallas_call(
        _moe_ffn_kernel,
        grid_spec=grid_spec,
        out_shape=jax.ShapeDtypeStruct((n_tiles * TR_MOE, D_MODEL), F32),
        compiler_params=_cparams(("arbitrary", "arbitrary")),
        name="moe_expert_ffn",
    )(tile_expert, n_valid, row_token, h, wg, wu, wd)


def _moe_combine_kernel(tr_ref, x_ref, rinfo_ref, y_hbm, fg_ref, o_ref, ybuf, sem):
    i = pl.program_id(0)
    slot = i % 2
    n_rows = TOP_K * TM_COMB

    @pl.when(i == 0)
    def _first():
        _issue_row_gather(y_hbm, tr_ref, 0, ybuf.at[0], sem.at[0], n_rows)

    _wait_row_gather(y_hbm, ybuf.at[slot], sem.at[slot], n_rows)

    @pl.when(i + 1 < pl.num_programs(0))
    def _prefetch():
        _issue_row_gather(y_hbm, tr_ref, (i + 1) * n_rows, ybuf.at[1 - slot], sem.at[1 - slot], n_rows)

    rinfo = rinfo_ref[...]
    g1 = jnp.broadcast_to(rinfo[:, 0:1], (TM_COMB, D_MODEL))
    g2 = jnp.broadcast_to(rinfo[:, 1:2], (TM_COMB, D_MODEL))
    y = x_ref[...] + g1 * ybuf[slot, 0:TM_COMB, :] + g2 * ybuf[slot, TM_COMB:n_rows, :]
    ms = jnp.mean(y * y, axis=-1, keepdims=True)
    o_ref[...] = y * lax.rsqrt(ms + EPS) * fg_ref[...]


def _moe_combine(x2, rinfo, y_rows, tok_rows, final_gain):
    t = x2.shape[0]
    grid_spec = pltpu.PrefetchScalarGridSpec(
        num_scalar_prefetch=1,
        grid=(t // TM_COMB,),
        in_specs=[
            pl.BlockSpec((TM_COMB, D_MODEL), lambda i, tr: (i, 0)),
            pl.BlockSpec((TM_COMB, LANES), lambda i, tr: (i, 0)),
            pl.BlockSpec(memory_space=pl.ANY),
            pl.BlockSpec((1, D_MODEL), lambda i, tr: (0, 0)),
        ],
        out_specs=pl.BlockSpec((TM_COMB, D_MODEL), lambda i, tr: (i, 0)),
        scratch_shapes=[
            pltpu.VMEM((2, TOP_K * TM_COMB, D_MODEL), F32),
            pltpu.SemaphoreType.DMA((2,)),
        ],
    )
    return pl.pallas_call(
        _moe_combine_kernel,
        grid_spec=grid_spec,
        out_shape=jax.ShapeDtypeStruct((t, D_MODEL), F32),
        compiler_params=_cparams(("arbitrary",)),
        name="moe_combine_norm",
    )(tok_rows, x2, rinfo, y_rows, final_gain)


def _routing_tables(rinfo):
    t = rinfo.shape[0]
    n_tiles = -(-(TOP_K * t) // TR_MOE) + N_EXPERTS
    flat_e = rinfo[:, 2:2 + TOP_K].astype(jnp.int32).reshape(-1)
    onehot = (flat_e[:, None] == jnp.arange(N_EXPERTS)[None, :]).astype(jnp.int32)
    csum = jnp.cumsum(onehot, axis=0)
    rank = jnp.sum((csum - onehot) * onehot, axis=1)
    counts = csum[-1]
    tiles_per_e = (counts + TR_MOE - 1) // TR_MOE
    tile_end = jnp.cumsum(tiles_per_e)
    row_off = (tile_end - tiles_per_e) * TR_MOE
    dest = row_off[flat_e] + rank
    token = jnp.arange(TOP_K * t, dtype=jnp.int32) // TOP_K
    row_token = jnp.zeros((n_tiles * TR_MOE,), jnp.int32).at[dest].set(token, unique_indices=True)
    n_valid = tile_end[-1:].astype(jnp.int32)
    tile_expert = jnp.searchsorted(tile_end, jnp.arange(n_tiles), side="right").astype(jnp.int32)
    last_e = jnp.searchsorted(tile_end, n_valid[0] - 1, side="right").astype(jnp.int32)
    tile_expert = jnp.where(jnp.arange(n_tiles) < n_valid[0], tile_expert, last_e)
    tok_rows = dest.reshape(t // TM_COMB, TM_COMB, TOP_K).transpose(0, 2, 1).reshape(-1).astype(jnp.int32)
    return tile_expert, n_valid, row_token, tok_rows


def _permute_w_in(w):
    sizes = (256, 256, 256, 512, 768, 8, 256, 256, 256, 256)
    offs = np.concatenate([[0], np.cumsum(sizes)])
    part = lambda n: w[:, offs[n]:offs[n + 1]]
    dt_pad = jnp.zeros((w.shape[0], LANES - SSM_HEADS), w.dtype)
    cols = [part(0) * (DA_HEAD_DIM ** -0.5 * LOG2E), part(2), part(1), part(3), part(4),
            part(6), part(7), part(8), part(9), part(5), dt_pad]
    return jnp.concatenate(cols, axis=1).astype(BF16)


def _column_tiles(w, tile):
    *lead, k, n = w.shape
    return jnp.moveaxis(w.reshape(*lead, k, n // tile, tile), -2, -3)


def _pad_lanes(v, fill=0.0):
    return jnp.concatenate([v, jnp.full((LANES - v.shape[0],), fill, v.dtype)])[None, :]


def kernel(x, w_in, w_out, attn_norm, ffn_norm, final_norm, rel_bias, lambda_q1, lambda_k1, lambda_q2,
           lambda_k2, da_head_norm, conv_w, conv_b, dt_bias, a_log, d_skip, ssm_norm, ret_head_norm,
           w_gate, w_up, w_down, router_w, e_gate, e_up, e_down):
    batch, seq, _ = x.shape
    x2 = x.reshape(batch * seq, D_MODEL)
    bias_tiles = _bias_tiles(rel_bias)
    ret_tabs = _retention_tables(seq)

    for layer in range(DEPTH):
        lam_init = 0.8 - 0.6 * math.exp(-0.3 * layer)
        lam = (jnp.exp(jnp.sum(lambda_q1[layer] * lambda_k1[layer]))
               - jnp.exp(jnp.sum(lambda_q2[layer] * lambda_k2[layer])) + lam_init).astype(F32)
        lam = jnp.full((1, ATT_T), lam, F32)

        qkv, qvt, pf = _inproj(x2, attn_norm[layer][None, :], _permute_w_in(w_in[layer]))

        da_gain = (jnp.tile(da_head_norm[layer], DA_HEADS) * (1.0 - lam_init))[:, None]
        out_a = _attention(qkv, qvt, bias_tiles, da_gain, lam, batch, seq)

        cw, cb = conv_w[layer], conv_b[layer]
        out_b = _ssd(pf, cw[:, :SSM_WIDTH], cw[:, SSM_WIDTH:], cb[None, :SSM_WIDTH], cb[None, SSM_WIDTH:],
                     _pad_lanes(dt_bias[layer]), _pad_lanes(a_log[layer]),
                     jnp.repeat(d_skip[layer], SSM_HEAD_DIM)[None, :], ssm_norm[layer][None, :], batch, seq)

        ret_gain = jnp.tile(ret_head_norm[layer], RET_HEADS)[None, :]
        out_c = _retention(pf, *ret_tabs, ret_gain, batch, seq)

        w_o = w_out[layer].astype(BF16)
        if layer % 2 == 0:
            i = layer // 2
            x2, h = _outproj(x2, out_a, out_b, out_c, w_o, ffn_norm[layer][None, :])
            pad = FFN_PAD - FFN_DENSE
            wg = _column_tiles(jnp.pad(w_gate[i], ((0, 0), (0, pad))).astype(BF16), TF_FFN)
            wu = _column_tiles(jnp.pad(w_up[i], ((0, 0), (0, pad))).astype(BF16), TF_FFN)
            wd = jnp.pad(w_down[i], ((0, pad), (0, 0))).astype(BF16)
            x2 = _dense_ffn(x2, h, wg, wu, wd)
        else:
            i = layer // 2
            rw = jnp.pad(router_w[i], ((0, 0), (0, LANES - N_EXPERTS))).astype(BF16)
            x2, h, rinfo = _outproj(x2, out_a, out_b, out_c, w_o, ffn_norm[layer][None, :], rw)
            tile_expert, n_valid, row_token, tok_rows = _routing_tables(rinfo)
            y_rows = _moe_ffn(h, tile_expert, n_valid, row_token,
                              e_gate[i].astype(BF16), e_up[i].astype(BF16), e_down[i].astype(BF16))
            x2 = _moe_combine(x2, rinfo, y_rows, tok_rows, final_norm[None, :])
    return x2.reshape(batch, seq, D_MODEL)
```

```python
import functools
import math

import jax
import jax.numpy as jnp
import numpy as np
from jax import lax
from jax.experimental import pallas as pl
from jax.experimental.pallas import tpu as pltpu

F32 = jnp.float32
BF16 = jnp.bfloat16

D_MODEL = 1024
DEPTH = 2
CHUNK = 64
EPS = 1e-6
DA_HEADS = 4
DA_HEAD_DIM = 32
DA_V_DIM = 64
DA_WIDTH = 256
SSM_HEADS = 8
SSM_HEAD_DIM = 64
SSM_WIDTH = 512
SSM_STATE = 64
SSM_GROUPS = 2
SSM_CONV = 4
RET_HEADS = 4
RET_KEY_DIM = 64
RET_WIDTH = 256
ROPE_BASE = 10000.0
REL_BUCKETS = 32
REL_MAX_DIST = 128
FFN_DENSE = 2752
N_EXPERTS = 8
FFN_EXPERT = 3584

LANES = 128
VMEM_LIMIT = 48 * 1024 * 1024
NEG_BIG = -1e30

TM_PROJ = 512
ATT_T = 512
LOG2E = math.log2(math.e)
BLK = 256
FFN_PAD = 2816
TM_FFN = 512
TF_FFN = 1408
TF_MOE = 896
TR_MOE = 896
TM_COMB = 256
TOP_K = 2

C_Z, C_XS, C_BC, C_RQ, C_RK, C_RV, C_RG, C_DT = 0, 512, 1024, 1280, 1536, 1792, 2048, 2304
PF_W = 2432
PF_CHUNK = 640
QKV_W = 768


def _cparams(sem):
    return pltpu.CompilerParams(dimension_semantics=sem, vmem_limit_bytes=VMEM_LIMIT)


def _group_mask(shape, axis, group, idx):
    io = lax.broadcasted_iota(jnp.int32, shape, axis)
    return (io >= idx * group) & (io < (idx + 1) * group)


def _lane_tile(x, n, axis=1):
    return jnp.concatenate([x] * n, axis=axis)


def _silu(x):
    h = 0.5 * x
    return h + h * jnp.tanh(h)


def _inproj_kernel(x_ref, g_ref, w_ref, qkv_ref, qvt_ref, pf_ref):
    x = x_ref[...]
    ms = jnp.mean(x * x, axis=-1, keepdims=True)
    h = (x * lax.rsqrt(ms + EPS) * g_ref[...]).astype(BF16)
    qkv = jnp.dot(h, w_ref[:, :QKV_W], preferred_element_type=F32)
    qkv_ref[...] = qkv.astype(BF16)
    qvt_ref[...] = qkv[:, :2 * DA_WIDTH].T.astype(BF16)
    for lo in range(0, PF_W, PF_CHUNK):
        hi = min(lo + PF_CHUNK, PF_W)
        pf_ref[:, lo:hi] = jnp.dot(h, w_ref[:, QKV_W + lo:QKV_W + hi], preferred_element_type=F32)


def _inproj(x2, gain, w):
    t = x2.shape[0]
    return pl.pallas_call(
        _inproj_kernel,
        grid=(t // TM_PROJ,),
        in_specs=[
            pl.BlockSpec((TM_PROJ, D_MODEL), lambda i: (i, 0)),
            pl.BlockSpec((1, D_MODEL), lambda i: (0, 0)),
            pl.BlockSpec((D_MODEL, QKV_W + PF_W), lambda i: (0, 0)),
        ],
        out_specs=[
            pl.BlockSpec((TM_PROJ, QKV_W), lambda i: (i, 0)),
            pl.BlockSpec((2 * DA_WIDTH, TM_PROJ), lambda i: (0, i)),
            pl.BlockSpec((TM_PROJ, PF_W), lambda i: (i, 0)),
        ],
        out_shape=[
            jax.ShapeDtypeStruct((t, QKV_W), BF16),
            jax.ShapeDtypeStruct((2 * DA_WIDTH, t), BF16),
            jax.ShapeDtypeStruct((t, PF_W), F32),
        ],
        compiler_params=_cparams(("parallel",)),
        name="inproj",
    )(x2, gain, w)


def _attn_kernel(qi_ref, kj_ref, qt_ref, k_ref, vt_ref, bias_ref, gain_ref, lam_ref, o_ref,
                 qm_ref, m_ref, l_ref, acc_ref, s_ref, e_ref, al_ref):
    p = pl.program_id(1)
    qi = qi_ref[p]
    kj = kj_ref[p]
    n_ht = 2 * DA_HEADS

    @pl.when(kj == 0)
    def _init():
        qt = qt_ref[...]
        for ht in range(n_ht):
            qm_ref[ht] = jnp.where(_group_mask(qt.shape, 0, DA_HEAD_DIM, ht), qt, jnp.zeros_like(qt))
        m_ref[...] = jnp.full(m_ref.shape, NEG_BIG, F32)
        l_ref[...] = jnp.zeros(l_ref.shape, F32)
        acc_ref[...] = jnp.zeros(acc_ref.shape, F32)

    def step(with_bias):
        k = k_ref[...]
        def scores(a, buf):
            s_ref[buf] = jnp.dot(k, qm_ref[a], preferred_element_type=F32)

        def softmax(b, buf):
            for c0 in range(0, ATT_T, LANES):
                cols = slice(c0, c0 + LANES)

                def block():
                    blk = s_ref[buf, :, cols]
                    return blk + bias_ref[0, b // 2, :, cols] if with_bias else blk

                m_prev = m_ref[b, :, cols]
                m_new = jnp.maximum(m_prev, jnp.max(block(), axis=0, keepdims=True))
                alpha = jnp.exp2(m_prev - m_new)
                e = jnp.exp2(block() - m_new)
                l_ref[b, :, cols] = alpha * l_ref[b, :, cols] + jnp.sum(e, axis=0, keepdims=True)
                m_ref[b, :, cols] = m_new
                al_ref[buf, :, cols] = alpha
                e_ref[buf, :, cols] = e.astype(BF16)

        def values(c, buf):
            r0 = (c // 4) * LANES
            if not isinstance(r0, int):
                r0 = pl.multiple_of(r0, LANES)
            pv = jnp.dot(vt_ref[pl.ds(r0, LANES), :], e_ref[buf], preferred_element_type=F32)
            acc_ref[c] = acc_ref[c] * al_ref[buf] + pv

        scores(0, 0)
        scores(1, 1)
        softmax(0, 0)

        def trip(u, carry):
            t = 2 * u + 1
            scores(t + 1, 0)
            softmax(t, 1)
            values(t - 1, 0)
            scores(t + 2, 1)
            softmax(t + 1, 0)
            values(t, 1)
            return carry

        for u in range((n_ht - 2) // 2):
            trip(u, 0)
        softmax(n_ht - 1, 1)
        values(n_ht - 2, 0)
        values(n_ht - 1, 1)

    @pl.when(kj >= qi - 1)
    def _near():
        step(True)

    @pl.when(kj < qi - 1)
    def _far():
        step(False)

    @pl.when(kj == qi)
    def _finish():
        lam = lam_ref[...]
        row_lo = lax.broadcasted_iota(jnp.int32, (LANES, ATT_T), 0) < DA_V_DIM
        for pair in range(DA_HEADS // 2):
            halves = []
            for sub in range(2):
                h = 2 * pair + sub
                o = acc_ref[2 * h] * (1.0 / l_ref[2 * h]) - lam * (acc_ref[2 * h + 1] * (1.0 / l_ref[2 * h + 1]))
                own = row_lo if sub == 0 else jnp.logical_not(row_lo)
                ms = jnp.sum(jnp.where(own, o * o, 0.0), axis=0, keepdims=True) * (1.0 / DA_V_DIM)
                halves.append(o * lax.rsqrt(ms + EPS))
            r0 = pair * LANES
            blk = jnp.where(row_lo, halves[0], halves[1]) * gain_ref[r0:r0 + LANES, :]
            o_ref[:, r0:r0 + LANES] = blk.T.astype(BF16)


def _attention(qkv, qvt, bias_tiles, gain, lam, batch, seq):
    nq = seq // ATT_T
    qi_l, kj_l = [], []
    for qi in range(nq):
        for kj in range(qi + 1):
            qi_l.append(qi)
            kj_l.append(kj)
    qi_tbl = jnp.asarray(np.array(qi_l, np.int32))
    kj_tbl = jnp.asarray(np.array(kj_l, np.int32))
    n_pairs = len(qi_l)
    n_ht = 2 * DA_HEADS

    grid_spec = pltpu.PrefetchScalarGridSpec(
        num_scalar_prefetch=2,
        grid=(batch, n_pairs),
        in_specs=[
            pl.BlockSpec((DA_WIDTH, ATT_T), lambda b, p, qi, kj: (0, b * nq + qi[p])),
            pl.BlockSpec((ATT_T, DA_WIDTH), lambda b, p, qi, kj: (b * nq + kj[p], 2)),
            pl.BlockSpec((DA_WIDTH, ATT_T), lambda b, p, qi, kj: (1, b * nq + kj[p])),
            pl.BlockSpec((1, DA_HEADS, ATT_T, ATT_T),
                         lambda b, p, qi, kj: (jnp.minimum(qi[p] - kj[p], 1), 0, 0, 0)),
            pl.BlockSpec((DA_WIDTH, 1), lambda b, p, qi, kj: (0, 0)),
            pl.BlockSpec((1, ATT_T), lambda b, p, qi, kj: (0, 0)),
        ],
        out_specs=pl.BlockSpec((ATT_T, DA_WIDTH), lambda b, p, qi, kj: (b * nq + qi[p], 0)),
        scratch_shapes=[
            pltpu.VMEM((n_ht, DA_WIDTH, ATT_T), BF16),
            pltpu.VMEM((n_ht, 1, ATT_T), F32),
            pltpu.VMEM((n_ht, 1, ATT_T), F32),
            pltpu.VMEM((n_ht, LANES, ATT_T), F32),
            pltpu.VMEM((2, ATT_T, ATT_T), F32),
            pltpu.VMEM((2, ATT_T, ATT_T), BF16),
            pltpu.VMEM((2, 1, ATT_T), F32),
        ],
    )
    return pl.pallas_call(
        _attn_kernel,
        grid_spec=grid_spec,
        out_shape=jax.ShapeDtypeStruct((batch * seq, DA_WIDTH), BF16),
        compiler_params=_cparams(("parallel", "arbitrary")),
        name="diff_attention",
    )(qi_tbl, kj_tbl, qvt, qkv, qvt, bias_tiles, gain, lam)


def _t5_bucket(rel):
    half = REL_BUCKETS // 2
    max_exact = half // 2
    ret = jnp.where(rel > 0, half, 0)
    n = jnp.abs(rel)
    nf = jnp.maximum(n, 1).astype(F32)
    large = max_exact + (jnp.log(nf / max_exact) / math.log(REL_MAX_DIST / max_exact)
                         * (half - max_exact)).astype(jnp.int32)
    large = jnp.minimum(large, half - 1)
    return ret + jnp.where(n < max_exact, n, large)


def _bias_tiles(rel_bias):
    i = jnp.arange(ATT_T)[None, :]
    j = jnp.arange(ATT_T)[:, None]
    far = rel_bias[_t5_bucket(jnp.int32(-2 * ATT_T))].astype(F32)
    tiles = []
    for off in range(2):
        bucket = _t5_bucket((j - off * ATT_T) - i)
        b = jnp.zeros((DA_HEADS, ATT_T, ATT_T), F32)
        for c in range(REL_BUCKETS):
            b = jnp.where((bucket == c)[None], rel_bias[c].astype(F32)[:, None, None], b)
        b = (b - far[:, None, None]) * LOG2E
        if off == 0:
            allowed = (j // CHUNK) <= (i // CHUNK)
            b = jnp.where(allowed[None], b, NEG_BIG)
        tiles.append(b)
    return jnp.stack(tiles)


def _cumsum_rows(x):
    n = x.shape[0]
    row = lax.broadcasted_iota(jnp.int32, x.shape, 0)
    sh = 1
    while sh < n:
        x = x + jnp.where(row >= sh, pltpu.roll(x, sh, axis=0), 0.0)
        sh *= 2
    return x


def _expand_heads(v, width, group):
    out = jnp.zeros((v.shape[0], width), F32)
    for h in range(width // group):
        col = jnp.broadcast_to(v[:, h:h + 1], (v.shape[0], width))
        out = jnp.where(_group_mask(out.shape, 1, group, h), col, out)
    return out


def _ssd_kernel(z_ref, xs_ref, bc_ref, dt_ref, cwx_ref, cwb_ref, cbx_ref, cbb_ref, dtb_ref, alog_ref,
                dskip_ref, ng_ref, o_ref, extx_ref, extb_ref, st_ref):
    j = pl.program_id(1)

    @pl.when(j == 0)
    def _reset():
        extx_ref[0:8, :] = jnp.zeros((8, SSM_WIDTH), F32)
        extb_ref[0:8, :] = jnp.zeros((8, 2 * LANES), F32)
        st_ref[...] = jnp.zeros(st_ref.shape, F32)

    extx_ref[8:8 + BLK, :] = xs_ref[...]
    extb_ref[8:8 + BLK, :] = bc_ref[...]
    yx = jnp.broadcast_to(cbx_ref[...], (BLK, SSM_WIDTH))
    yb = jnp.broadcast_to(cbb_ref[...], (BLK, 2 * LANES))
    for w in range(SSM_CONV):
        lo = 8 - (SSM_CONV - 1) + w
        yx = yx + extx_ref[lo:lo + BLK, :] * cwx_ref[w:w + 1, :]
        yb = yb + extb_ref[lo:lo + BLK, :] * cwb_ref[w:w + 1, :]
    tail_x = extx_ref[BLK:BLK + 8, :]
    tail_b = extb_ref[BLK:BLK + 8, :]
    extx_ref[0:8, :] = tail_x
    extb_ref[0:8, :] = tail_b
    xs = _silu(yx)
    bc = _silu(yb)
    bm = bc[:, :LANES]
    cm = bc[:, LANES:]

    dt = jax.nn.softplus(dt_ref[...] + dtb_ref[...])
    a = -jnp.exp(alog_ref[...]) * dt
    a_cs = _cumsum_rows(a)
    a_cs_t = a_cs.T
    a_last = a_cs[BLK - 1:BLK, :]

    dt_full = _expand_heads(dt, SSM_WIDTH, SSM_HEAD_DIM)
    ea_full = _expand_heads(jnp.exp(a_cs), SSM_WIDTH, SSM_HEAD_DIM)
    dec_full = _expand_heads(jnp.exp(a_last - a_cs), SSM_WIDTH, SSM_HEAD_DIM)
    cdec_full = _expand_heads(jnp.exp(a_last), SSM_WIDTH, SSM_HEAD_DIM)

    xdt = xs * dt_full
    xdt_b = xdt.astype(BF16)
    cm_b = cm.astype(BF16)
    bm_b = bm.astype(BF16)

    st = st_ref[...]
    y = jnp.dot(cm_b, st.astype(BF16), preferred_element_type=F32) * ea_full + xs * dskip_ref[...]

    row = lax.broadcasted_iota(jnp.int32, (BLK, BLK), 0)
    colio = lax.broadcasted_iota(jnp.int32, (BLK, BLK), 1)
    causal = colio <= row
    rep = SSM_HEADS // SSM_GROUPS
    lane_lo = lax.broadcasted_iota(jnp.int32, (BLK, LANES), 1) < SSM_HEAD_DIM
    pieces = []
    for g in range(SSM_GROUPS):
        cg = jnp.where(_group_mask(cm_b.shape, 1, SSM_STATE, g), cm_b, jnp.zeros_like(cm_b))
        cb = lax.dot_general(cg, bm_b, (((1,), (1,)), ((), ())), preferred_element_type=F32)
        for pair in range(rep // 2):
            halves = []
            for sub in range(2):
                h = g * rep + pair * 2 + sub
                seg = jnp.broadcast_to(a_cs[:, h:h + 1], (BLK, BLK)) - a_cs_t[h:h + 1, :]
                lmat = jnp.exp(jnp.where(causal, seg, NEG_BIG))
                mh = (cb * lmat).astype(BF16)
                c0 = (h // 2) * LANES
                halves.append(jnp.dot(mh, xdt_b[:, c0:c0 + LANES], preferred_element_type=F32))
            pieces.append(jnp.where(lane_lo, halves[0], halves[1]))
    y = y + jnp.concatenate(pieces, axis=1)

    upd = jnp.dot(bm.T.astype(BF16), (xdt * dec_full).astype(BF16), preferred_element_type=F32)
    r_io = lax.broadcasted_iota(jnp.int32, upd.shape, 0) // SSM_STATE
    c_io = lax.broadcasted_iota(jnp.int32, upd.shape, 1) // (SSM_HEAD_DIM * rep)
    st_ref[...] = st * cdec_full + jnp.where(r_io == c_io, upd, 0.0)

    y = y * _silu(z_ref[...])
    gw = SSM_WIDTH // SSM_GROUPS
    for g in range(SSM_GROUPS):
        yg = y[:, g * gw:(g + 1) * gw]
        ms = jnp.mean(yg * yg, axis=-1, keepdims=True)
        o_ref[:, g * gw:(g + 1) * gw] = (yg * lax.rsqrt(ms + EPS) * ng_ref[:, g * gw:(g + 1) * gw]).astype(BF16)


def _ssd(pf, cwx, cwb, cbx, cbb, dtb, alog, dskip, ng, batch, seq):
    nb = seq // BLK
    row = lambda w, c: pl.BlockSpec((BLK, w), lambda b, j: (b * nb + j, c))
    const = lambda r, w: pl.BlockSpec((r, w), lambda b, j: (0, 0))
    return pl.pallas_call(
        _ssd_kernel,
        grid=(batch, nb),
        in_specs=[
            row(SSM_WIDTH, C_Z // SSM_WIDTH), row(SSM_WIDTH, C_XS // SSM_WIDTH), row(2 * LANES, C_BC // (2 * LANES)),
            row(LANES, C_DT // LANES),
            const(SSM_CONV, SSM_WIDTH), const(SSM_CONV, 2 * LANES), const(1, SSM_WIDTH), const(1, 2 * LANES),
            const(1, LANES), const(1, LANES), const(1, SSM_WIDTH), const(1, SSM_WIDTH),
        ],
        out_specs=pl.BlockSpec((BLK, SSM_WIDTH), lambda b, j: (b * nb + j, 0)),
        out_shape=jax.ShapeDtypeStruct((batch * seq, SSM_WIDTH), BF16),
        scratch_shapes=[
            pltpu.VMEM((BLK + 8, SSM_WIDTH), F32),
            pltpu.VMEM((BLK + 8, 2 * LANES), F32),
            pltpu.VMEM((SSM_GROUPS * SSM_STATE, SSM_WIDTH), F32),
        ],
        compiler_params=_cparams(("parallel", "arbitrary")),
        name="ssd_mixer",
    )(pf, pf, pf, pf, cwx, cwb, cbx, cbb, dtb, alog, dskip, ng)


def _rotary(u, cos, sin_signed):
    first = (lax.broadcasted_iota(jnp.int32, u.shape, 1) % RET_KEY_DIM) < (RET_KEY_DIM // 2)
    half = RET_KEY_DIM // 2
    swapped = jnp.where(first, pltpu.roll(u, u.shape[1] - half, axis=1), pltpu.roll(u, half, axis=1))
    return u * cos + swapped * sin_signed


def _ret_kernel(q_ref, k_ref, v_ref, g_ref, cos_ref, sin_ref, dmat_ref, qdec_ref, kdec_ref, gdec_ref,
                bd_ref, gain_ref, o_ref, st_ref):
    j = pl.program_id(1)

    @pl.when(j == 0)
    def _reset():
        st_ref[...] = jnp.zeros(st_ref.shape, F32)

    cos = _lane_tile(cos_ref[...], RET_WIDTH // LANES)
    sin = _lane_tile(sin_ref[...], RET_WIDTH // LANES)
    q = _rotary(q_ref[...], cos, sin)
    k = _rotary(k_ref[...], cos, sin) * (RET_KEY_DIM ** -0.5)
    v_b = v_ref[...].astype(BF16)
    q_b = q.astype(BF16)
    k_b = k.astype(BF16)

    st = st_ref[...]
    o = jnp.dot((q * qdec_ref[...]).astype(BF16), st.astype(BF16), preferred_element_type=F32)
    for h in range(RET_HEADS):
        cm = _group_mask(q_b.shape, 1, RET_KEY_DIM, h)
        qh = jnp.where(cm, q_b, jnp.zeros_like(q_b))
        s = lax.dot_general(qh, k_b, (((1,), (1,)), ((), ())), preferred_element_type=F32) * dmat_ref[h]
        oh = jnp.dot(s.astype(BF16), v_b, preferred_element_type=F32)
        o = o + jnp.where(cm, oh, 0.0)

    upd = jnp.dot((k * kdec_ref[...]).T.astype(BF16), v_b, preferred_element_type=F32)
    st_ref[...] = st * gdec_ref[...] + upd * bd_ref[...]

    out = jnp.zeros(o.shape, F32)
    for h in range(RET_HEADS):
        cm = _group_mask(o.shape, 1, RET_KEY_DIM, h)
        ms = jnp.sum(jnp.where(cm, o * o, 0.0), axis=-1, keepdims=True) * (1.0 / RET_KEY_DIM)
        out = out + jnp.where(cm, o * lax.rsqrt(ms + EPS), 0.0)
    o_ref[...] = (_silu(g_ref[...]) * (out * gain_ref[...])).astype(BF16)


def _retention(pf, cos, sin, dmat, qdec, kdec, gdec, bd, gain, batch, seq):
    nb = seq // BLK
    row = lambda c: pl.BlockSpec((BLK, RET_WIDTH), lambda b, j: (b * nb + j, c))
    const2 = pl.BlockSpec((BLK, RET_WIDTH), lambda b, j: (0, 0))
    return pl.pallas_call(
        _ret_kernel,
        grid=(batch, nb),
        in_specs=[
            row(C_RQ // RET_WIDTH), row(C_RK // RET_WIDTH), row(C_RV // RET_WIDTH), row(C_RG // RET_WIDTH),
            pl.BlockSpec((BLK, LANES), lambda b, j: (j, 0)),
            pl.BlockSpec((BLK, LANES), lambda b, j: (j, 0)),
            pl.BlockSpec((RET_HEADS, BLK, BLK), lambda b, j: (0, 0, 0)),
            const2, const2, const2, const2,
            pl.BlockSpec((1, RET_WIDTH), lambda b, j: (0, 0)),
        ],
        out_specs=pl.BlockSpec((BLK, RET_WIDTH), lambda b, j: (b * nb + j, 0)),
        out_shape=jax.ShapeDtypeStruct((batch * seq, RET_WIDTH), BF16),
        scratch_shapes=[pltpu.VMEM((RET_WIDTH, RET_WIDTH), F32)],
        compiler_params=_cparams(("parallel", "arbitrary")),
        name="retention",
    )(pf, pf, pf, pf, cos, sin, dmat, qdec, kdec, gdec, bd, gain)


def _retention_tables(seq):
    f32 = np.float32
    inv = (f32(1.0) / (f32(ROPE_BASE) ** (np.arange(0, RET_KEY_DIM, 2, dtype=f32) / f32(RET_KEY_DIM)))).astype(f32)
    ang = (np.arange(seq, dtype=f32)[:, None] * inv[None, :]).astype(f32)
    cos_h = np.concatenate([np.cos(ang), np.cos(ang)], axis=-1)
    sin_h = np.concatenate([-np.sin(ang), np.sin(ang)], axis=-1)
    cos = np.tile(cos_h, (1, 2)).astype(f32)
    sin = np.tile(sin_h, (1, 2)).astype(f32)
    log_gamma = np.log1p(-np.power(f32(2.0), f32(-5.0) - np.arange(RET_HEADS, dtype=f32))).astype(f32)
    idx = np.arange(BLK, dtype=f32)
    ii = np.arange(BLK)
    same_or_earlier_chunk = (ii[None, :] // CHUNK) <= (ii[:, None] // CHUNK)
    dmat = np.exp(log_gamma[:, None, None] * np.abs(idx[:, None] - idx[None, :]))
    dmat = np.where(same_or_earlier_chunk[None], dmat, 0.0).astype(f32)
    lg_cols = np.repeat(log_gamma, RET_KEY_DIM)
    qdec = np.exp(lg_cols[None, :] * (idx + 1.0)[:, None]).astype(f32)
    kdec = np.exp(lg_cols[None, :] * (BLK - 1.0 - idx)[:, None]).astype(f32)
    head_of = np.arange(RET_WIDTH) // RET_KEY_DIM
    bd = (head_of[:, None] == head_of[None, :]).astype(f32)
    gdec = (np.exp(lg_cols * f32(BLK))[:, None] * bd).astype(f32)
    return tuple(jnp.asarray(a) for a in (cos, sin, dmat, qdec, kdec, gdec, bd))


def _outproj_kernel(x_ref, a_ref, b_ref, c_ref, w_ref, g_ref, *rest, with_router):
    if with_router:
        rw_ref, xo_ref, h_ref, rinfo_ref = rest
    else:
        xo_ref, h_ref = rest
    acc = x_ref[...]
    acc = acc + jnp.dot(a_ref[...], w_ref[0:DA_WIDTH, :], preferred_element_type=F32)
    acc = acc + jnp.dot(b_ref[...], w_ref[DA_WIDTH:DA_WIDTH + SSM_WIDTH, :], preferred_element_type=F32)
    acc = acc + jnp.dot(c_ref[...], w_ref[DA_WIDTH + SSM_WIDTH:, :], preferred_element_type=F32)
    xo_ref[...] = acc
    ms = jnp.mean(acc * acc, axis=-1, keepdims=True)
    hf = acc * lax.rsqrt(ms + EPS) * g_ref[...]
    h = hf.astype(BF16)
    h_ref[...] = hf.astype(h_ref.dtype)
    if with_router:
        logits = jnp.dot(h, rw_ref[...], preferred_element_type=F32)
        lane = lax.broadcasted_iota(jnp.int32, logits.shape, 1).astype(F32)
        logits = jnp.where(lane < N_EXPERTS, logits, NEG_BIG)
        v1 = jnp.max(logits, axis=-1, keepdims=True)
        i1 = jnp.min(jnp.where(logits == v1, lane, float(LANES)), axis=-1, keepdims=True)
        rest_l = jnp.where(lane == i1, NEG_BIG, logits)
        v2 = jnp.max(rest_l, axis=-1, keepdims=True)
        i2 = jnp.min(jnp.where(rest_l == v2, lane, float(LANES)), axis=-1, keepdims=True)
        e2 = jnp.exp(v2 - v1)
        g1 = 1.0 / (1.0 + e2)
        g2 = e2 * g1
        rinfo_ref[...] = (jnp.where(lane == 0.0, g1, 0.0) + jnp.where(lane == 1.0, g2, 0.0)
                          + jnp.where(lane == 2.0, i1, 0.0) + jnp.where(lane == 3.0, i2, 0.0))


def _outproj(x2, oa, ob, oc, w, gain, router_w=None):
    t = x2.shape[0]
    with_router = router_w is not None
    tok = lambda w_: pl.BlockSpec((TM_PROJ, w_), lambda i: (i, 0))
    in_specs = [tok(D_MODEL), tok(DA_WIDTH), tok(SSM_WIDTH), tok(RET_WIDTH),
                pl.BlockSpec((D_MODEL, D_MODEL), lambda i: (0, 0)),
                pl.BlockSpec((1, D_MODEL), lambda i: (0, 0))]
    out_specs = [tok(D_MODEL), tok(D_MODEL)]
    h_dtype = F32 if with_router else BF16
    out_shape = [jax.ShapeDtypeStruct((t, D_MODEL), F32), jax.ShapeDtypeStruct((t, D_MODEL), h_dtype)]
    args = [x2, oa, ob, oc, w, gain]
    if with_router:
        in_specs.append(pl.BlockSpec((D_MODEL, LANES), lambda i: (0, 0)))
        out_specs.append(tok(LANES))
        out_shape.append(jax.ShapeDtypeStruct((t, LANES), F32))
        args.append(router_w)
    return pl.pallas_call(
        functools.partial(_outproj_kernel, with_router=with_router),
        grid=(t // TM_PROJ,),
        in_specs=in_specs,
        out_specs=out_specs,
        out_shape=out_shape,
        compiler_params=_cparams(("parallel",)),
        name="outproj_router" if with_router else "outproj",
    )(*args)


def _ffn_kernel(x_ref, h_ref, wg_ref, wu_ref, wd_ref, o_ref, acc_ref):
    f = pl.program_id(1)

    @pl.when(f == 0)
    def _init():
        acc_ref[...] = x_ref[...]

    h = h_ref[...]
    gate = jnp.dot(h, wg_ref[0], preferred_element_type=F32)
    up = jnp.dot(h, wu_ref[0], preferred_element_type=F32)
    act = (_silu(gate) * up).astype(BF16)
    acc_ref[...] += jnp.dot(act, wd_ref[...], preferred_element_type=F32)

    @pl.when(f == pl.num_programs(1) - 1)
    def _done():
        o_ref[...] = acc_ref[...]


def _dense_ffn(x2, h, wg, wu, wd):
    t = x2.shape[0]
    nf = wg.shape[0]
    return pl.pallas_call(
        _ffn_kernel,
        grid=(t // TM_FFN, nf),
        in_specs=[
            pl.BlockSpec((TM_FFN, D_MODEL), lambda i, f: (i, 0)),
            pl.BlockSpec((TM_FFN, D_MODEL), lambda i, f: (i, 0)),
            pl.BlockSpec((1, D_MODEL, TF_FFN), lambda i, f: (f, 0, 0)),
            pl.BlockSpec((1, D_MODEL, TF_FFN), lambda i, f: (f, 0, 0)),
            pl.BlockSpec((TF_FFN, D_MODEL), lambda i, f: (f, 0)),
        ],
        out_specs=pl.BlockSpec((TM_FFN, D_MODEL), lambda i, f: (i, 0)),
        out_shape=jax.ShapeDtypeStruct((t, D_MODEL), F32),
        scratch_shapes=[pltpu.VMEM((TM_FFN, D_MODEL), F32)],
        compiler_params=_cparams(("parallel", "arbitrary")),
        name="dense_swiglu",
    )(x2, h, wg, wu, wd)


def _issue_row_gather(src_hbm, idx_ref, idx_base, dst, sem, n_rows):
    def body(r, carry):
        tok = idx_ref[idx_base + r]
        pltpu.make_async_copy(src_hbm.at[pl.ds(tok, 1)], dst.at[pl.ds(r, 1)], sem).start()
        return carry
    lax.fori_loop(0, n_rows, body, 0, unroll=8)


def _wait_row_gather(src_hbm, dst, sem, n_rows):
    pltpu.make_async_copy(src_hbm.at[pl.ds(0, n_rows)], dst, sem).wait()


def _moe_ffn_kernel(te_ref, nv_ref, rt_ref, h_hbm, wg_ref, wu_ref, wd_ref, y_ref, hbuf, xb_ref, acc_ref, sem):
    i = pl.program_id(0)
    f = pl.program_id(1)
    n_valid = nv_ref[0]
    valid = i < n_valid
    slot = i % 2

    @pl.when((f == 0) & (i == 0) & valid)
    def _first():
        _issue_row_gather(h_hbm, rt_ref, 0, hbuf.at[0], sem.at[0], TR_MOE)

    @pl.when((f == 0) & valid)
    def _arrive():
        _wait_row_gather(h_hbm, hbuf.at[slot], sem.at[slot], TR_MOE)
        xb_ref[...] = hbuf[slot].astype(BF16)

    def compute(prefetch_next):
        if prefetch_next:
            rows_per_step = TR_MOE // (FFN_EXPERT // TF_MOE)
            r0 = f * rows_per_step
            for j in range(rows_per_step):
                tok = rt_ref[(i + 1) * TR_MOE + r0 + j]
                pltpu.make_async_copy(h_hbm.at[pl.ds(tok, 1)], hbuf.at[1 - slot, pl.ds(r0 + j, 1)],
                                      sem.at[1 - slot]).start()
        xb = xb_ref[...]
        gate = jnp.dot(xb, wg_ref[0], preferred_element_type=F32)
        up = jnp.dot(xb, wu_ref[0], preferred_element_type=F32)
        act = (_silu(gate) * up).astype(BF16)
        contrib = jnp.dot(act, wd_ref[0], preferred_element_type=F32)

        @pl.when(f == 0)
        def _():
            acc_ref[...] = contrib

        @pl.when(f > 0)
        def _():
            acc_ref[...] += contrib

    has_next = i + 1 < n_valid

    @pl.when(valid & has_next)
    def _compute_and_prefetch():
        compute(True)

    @pl.when(valid & jnp.logical_not(has_next))
    def _compute_last():
        compute(False)

    @pl.when(f == pl.num_programs(1) - 1)
    def _store():
        y_ref[...] = jnp.where(valid, acc_ref[...], 0.0)


def _moe_ffn(h, tile_expert, n_valid, row_token, wg, wu, wd):
    n_tiles = tile_expert.shape[0]
    nf = wg.shape[2] // TF_MOE

    def w_in_map(i, f, te, nv, rt):
        return (te[i], 0, jnp.where(i < nv[0], f, nf - 1))

    def w_out_map(i, f, te, nv, rt):
        return (te[i], jnp.where(i < nv[0], f, nf - 1), 0)

    grid_spec = pltpu.PrefetchScalarGridSpec(
        num_scalar_prefetch=3,
        grid=(n_tiles, nf),
        in_specs=[
            pl.BlockSpec(memory_space=pl.ANY),
            pl.BlockSpec((1, D_MODEL, TF_MOE), w_in_map),
            pl.BlockSpec((1, D_MODEL, TF_MOE), w_in_map),
            pl.BlockSpec((1, TF_MOE, D_MODEL), w_out_map),
        ],
        out_specs=pl.BlockSpec((TR_MOE, D_MODEL), lambda i, f, te, nv, rt: (i, 0)),
        scratch_shapes=[
            pltpu.VMEM((2, TR_MOE, D_MODEL), F32),
            pltpu.VMEM((TR_MOE, D_MODEL), BF16),
            pltpu.VMEM((TR_MOE, D_MODEL), F32),
            pltpu.SemaphoreType.DMA((2,)),
        ],
    )
    return pl.pallas_call(
        _moe_ffn_kernel,
        grid_spec=grid_spec,
        out_shape=jax.ShapeDtypeStruct((n_tiles * TR_MOE, D_MODEL), F32),
        compiler_params=_cparams(("arbitrary", "arbitrary")),
        name="moe_expert_ffn",
    )(tile_expert, n_valid, row_token, h, wg, wu, wd)


def _moe_combine_kernel(tr_ref, x_ref, rinfo_ref, y_hbm, fg_ref, o_ref, ybuf, sem):
    i = pl.program_id(0)
    slot = i % 2
    n_rows = TOP_K * TM_COMB

    @pl.when(i == 0)
    def _first():
        _issue_row_gather(y_hbm, tr_ref, 0, ybuf.at[0], sem.at[0], n_rows)

    _wait_row_gather(y_hbm, ybuf.at[slot], sem.at[slot], n_rows)

    @pl.when(i + 1 < pl.num_programs(0))
    def _prefetch():
        _issue_row_gather(y_hbm, tr_ref, (i + 1) * n_rows, ybuf.at[1 - slot], sem.at[1 - slot], n_rows)

    rinfo = rinfo_ref[...]
    g1 = jnp.broadcast_to(rinfo[:, 0:1], (TM_COMB, D_MODEL))
    g2 = jnp.broadcast_to(rinfo[:, 1:2], (TM_COMB, D_MODEL))
    y = x_ref[...] + g1 * ybuf[slot, 0:TM_COMB, :] + g2 * ybuf[slot, TM_COMB:n_rows, :]
    ms = jnp.mean(y * y, axis=-1, keepdims=True)
    o_ref[...] = y * lax.rsqrt(ms + EPS) * fg_ref[...]


def _moe_combine(x2, rinfo, y_rows, tok_rows, final_gain):
    t = x2.shape[0]
    grid_spec = pltpu.PrefetchScalarGridSpec(
        num_scalar_prefetch=1,
        grid=(t // TM_COMB,),
        in_specs=[
            pl.BlockSpec((TM_COMB, D_MODEL), lambda i, tr: (i, 0)),
            pl.BlockSpec((TM_COMB, LANES), lambda i, tr: (i, 0)),
            pl.BlockSpec(memory_space=pl.ANY),
            pl.BlockSpec((1, D_MODEL), lambda i, tr: (0, 0)),
        ],
        out_specs=pl.BlockSpec((TM_COMB, D_MODEL), lambda i, tr: (i, 0)),
        scratch_shapes=[
            pltpu.VMEM((2, TOP_K * TM_COMB, D_MODEL), F32),
            pltpu.SemaphoreType.DMA((2,)),
        ],
    )
    return pl.pallas_call(
        _moe_combine_kernel,
        grid_spec=grid_spec,
        out_shape=jax.ShapeDtypeStruct((t, D_MODEL), F32),
        compiler_params=_cparams(("arbitrary",)),
        name="moe_combine_norm",
    )(tok_rows, x2, rinfo, y_rows, final_gain)


def _routing_tables(rinfo):
    t = rinfo.shape[0]
    n_tiles = -(-(TOP_K * t) // TR_MOE) + N_EXPERTS
    flat_e = rinfo[:, 2:2 + TOP_K].astype(jnp.int32).reshape(-1)
    onehot = (flat_e[:, None] == jnp.arange(N_EXPERTS)[None, :]).astype(jnp.int32)
    csum = jnp.cumsum(onehot, axis=0)
    rank = jnp.sum((csum - onehot) * onehot, axis=1)
    counts = csum[-1]
    tiles_per_e = (counts + TR_MOE - 1) // TR_MOE
    tile_end = jnp.cumsum(tiles_per_e)
    row_off = (tile_end - tiles_per_e) * TR_MOE
    dest = row_off[flat_e] + rank
    token = jnp.arange(TOP_K * t, dtype=jnp.int32) // TOP_K
    row_token = jnp.zeros((n_tiles * TR_MOE,), jnp.int32).at[dest].set(token, unique_indices=True)
    n_valid = tile_end[-1:].astype(jnp.int32)
    tile_expert = jnp.searchsorted(tile_end, jnp.arange(n_tiles), side="right").astype(jnp.int32)
    last_e = jnp.searchsorted(tile_end, n_valid[0] - 1, side="right").astype(jnp.int32)
    tile_expert = jnp.where(jnp.arange(n_tiles) < n_valid[0], tile_expert, last_e)
    tok_rows = dest.reshape(t // TM_COMB, TM_COMB, TOP_K).transpose(0, 2, 1).reshape(-1).astype(jnp.int32)
    return tile_expert, n_valid, row_token, tok_rows


def _permute_w_in(w):
    sizes = (256, 256, 256, 512, 768, 8, 256, 256, 256, 256)
    offs = np.concatenate([[0], np.cumsum(sizes)])
    part = lambda n: w[:, offs[n]:offs[n + 1]]
    dt_pad = jnp.zeros((w.shape[0], LANES - SSM_HEADS), w.dtype)
    cols = [part(0) * (DA_HEAD_DIM ** -0.5 * LOG2E), part(2), part(1), part(3), part(4),
            part(6), part(7), part(8), part(9), part(5), dt_pad]
    return jnp.concatenate(cols, axis=1).astype(BF16)


def _column_tiles(w, tile):
    *lead, k, n = w.shape
    return jnp.moveaxis(w.reshape(*lead, k, n // tile, tile), -2, -3)


def _pad_lanes(v, fill=0.0):
    return jnp.concatenate([v, jnp.full((LANES - v.shape[0],), fill, v.dtype)])[None, :]


def kernel(x, w_in, w_out, attn_norm, ffn_norm, final_norm, rel_bias, lambda_q1, lambda_k1, lambda_q2,
           lambda_k2, da_head_norm, conv_w, conv_b, dt_bias, a_log, d_skip, ssm_norm, ret_head_norm,
           w_gate, w_up, w_down, router_w, e_gate, e_up, e_down):
    batch, seq, _ = x.shape
    x2 = x.reshape(batch * seq, D_MODEL)
    bias_tiles = _bias_tiles(rel_bias)
    ret_tabs = _retention_tables(seq)

    for layer in range(DEPTH):
        lam_init = 0.8 - 0.6 * math.exp(-0.3 * layer)
        lam = (jnp.exp(jnp.sum(lambda_q1[layer] * lambda_k1[layer]))
               - jnp.exp(jnp.sum(lambda_q2[layer] * lambda_k2[layer])) + lam_init).astype(F32)
        lam = jnp.full((1, ATT_T), lam, F32)

        qkv, qvt, pf = _inproj(x2, attn_norm[layer][None, :], _permute_w_in(w_in[layer]))

        da_gain = (jnp.tile(da_head_norm[layer], DA_HEADS) * (1.0 - lam_init))[:, None]
        out_a = _attention(qkv, qvt, bias_tiles, da_gain, lam, batch, seq)

        cw, cb = conv_w[layer], conv_b[layer]
        out_b = _ssd(pf, cw[:, :SSM_WIDTH], cw[:, SSM_WIDTH:], cb[None, :SSM_WIDTH], cb[None, SSM_WIDTH:],
                     _pad_lanes(dt_bias[layer]), _pad_lanes(a_log[layer]),
                     jnp.repeat(d_skip[layer], SSM_HEAD_DIM)[None, :], ssm_norm[layer][None, :], batch, seq)

        ret_gain = jnp.tile(ret_head_norm[layer], RET_HEADS)[None, :]
        out_c = _retention(pf, *ret_tabs, ret_gain, batch, seq)

        w_o = w_out[layer].astype(BF16)
        if layer % 2 == 0:
            i = layer // 2
            x2, h = _outproj(x2, out_a, out_b, out_c, w_o, ffn_norm[layer][None, :])
            pad = FFN_PAD - FFN_DENSE
            wg = _column_tiles(jnp.pad(w_gate[i], ((0, 0), (0, pad))).astype(BF16), TF_FFN)
            wu = _column_tiles(jnp.pad(w_up[i], ((0, 0), (0, pad))).astype(BF16), TF_FFN)
            wd = jnp.pad(w_down[i], ((0, pad), (0, 0))).astype(BF16)
            x2 = _dense_ffn(x2, h, wg, wu, wd)
        else:
            i = layer // 2
            rw = jnp.pad(router_w[i], ((0, 0), (0, LANES - N_EXPERTS))).astype(BF16)
            x2, h, rinfo = _outproj(x2, out_a, out_b, out_c, w_o, ffn_norm[layer][None, :], rw)
            tile_expert, n_valid, row_token, tok_rows = _routing_tables(rinfo)
            y_rows = _moe_ffn(h, tile_expert, n_valid, row_token,
                              e_gate[i].astype(BF16), e_up[i].astype(BF16), e_down[i].astype(BF16))
            x2 = _moe_combine(x2, rinfo, y_rows, tok_rows, final_norm[None, :])
    return x2.reshape(batch, seq, D_MODEL)
```

```python
import functools
import math

import jax
import jax.numpy as jnp
import numpy as np
from jax import lax
from jax.experimental import pallas as pl
from jax.experimental.pallas import tpu as pltpu

F32 = jnp.float32
BF16 = jnp.bfloat16

D_MODEL = 1024
DEPTH = 2
CHUNK = 64
EPS = 1e-6
DA_HEADS = 4
DA_HEAD_DIM = 32
DA_V_DIM = 64
DA_WIDTH = 256
SSM_HEADS = 8
SSM_HEAD_DIM = 64
SSM_WIDTH = 512
SSM_STATE = 64
SSM_GROUPS = 2
SSM_CONV = 4
RET_HEADS = 4
RET_KEY_DIM = 64
RET_WIDTH = 256
ROPE_BASE = 10000.0
REL_BUCKETS = 32
REL_MAX_DIST = 128
FFN_DENSE = 2752
N_EXPERTS = 8
FFN_EXPERT = 3584

LANES = 128
VMEM_LIMIT = 48 * 1024 * 1024
NEG_BIG = -1e30

TM_PROJ = 512
ATT_T = 512
LOG2E = math.log2(math.e)
ATT_SLOTS = 4
BLK = 256
FFN_PAD = 2816
TM_FFN = 512
TF_FFN = 1408
TF_MOE = 512
TR_MOE = 896
TM_COMB = 256
TOP_K = 2

C_Z, C_XS, C_BC, C_RQ, C_RK, C_RV, C_RG, C_DT = 0, 512, 1024, 1280, 1536, 1792, 2048, 2304
PF_W = 2432
PF_CHUNK = 640
QKV_W = 768


def _cparams(sem):
    return pltpu.CompilerParams(dimension_semantics=sem, vmem_limit_bytes=VMEM_LIMIT)


def _group_mask(shape, axis, group, idx):
    io = lax.broadcasted_iota(jnp.int32, shape, axis)
    return (io >= idx * group) & (io < (idx + 1) * group)


def _lane_tile(x, n, axis=1):
    return jnp.concatenate([x] * n, axis=axis)


def _silu(x):
    h = 0.5 * x
    return h + h * jnp.tanh(h)


def _inproj_kernel(x_ref, g_ref, w_ref, qkv_ref, qvt_ref, pf_ref):
    x = x_ref[...]
    ms = jnp.mean(x * x, axis=-1, keepdims=True)
    h = (x * lax.rsqrt(ms + EPS) * g_ref[...]).astype(BF16)
    qkv = jnp.dot(h, w_ref[:, :QKV_W], preferred_element_type=F32)
    qkv_ref[...] = qkv.astype(BF16)
    qvt_ref[...] = qkv[:, :2 * DA_WIDTH].T.astype(BF16)
    for lo in range(0, PF_W, PF_CHUNK):
        hi = min(lo + PF_CHUNK, PF_W)
        pf_ref[:, lo:hi] = jnp.dot(h, w_ref[:, QKV_W + lo:QKV_W + hi], preferred_element_type=F32)


def _inproj(x2, gain, w):
    t = x2.shape[0]
    return pl.pallas_call(
        _inproj_kernel,
        grid=(t // TM_PROJ,),
        in_specs=[
            pl.BlockSpec((TM_PROJ, D_MODEL), lambda i: (i, 0)),
            pl.BlockSpec((1, D_MODEL), lambda i: (0, 0)),
            pl.BlockSpec((D_MODEL, QKV_W + PF_W), lambda i: (0, 0)),
        ],
        out_specs=[
            pl.BlockSpec((TM_PROJ, QKV_W), lambda i: (i, 0)),
            pl.BlockSpec((2 * DA_WIDTH, TM_PROJ), lambda i: (0, i)),
            pl.BlockSpec((TM_PROJ, PF_W), lambda i: (i, 0)),
        ],
        out_shape=[
            jax.ShapeDtypeStruct((t, QKV_W), BF16),
            jax.ShapeDtypeStruct((2 * DA_WIDTH, t), BF16),
            jax.ShapeDtypeStruct((t, PF_W), F32),
        ],
        compiler_params=_cparams(("parallel",)),
        name="inproj",
    )(x2, gain, w)


def _attn_kernel(qt_ref, k_hbm, vt_hbm, bias_ref, gain_ref, lam_ref, o_ref,
                 qm_ref, m_ref, l_ref, acc_ref, s_ref, e_ref, al_ref, kbuf, vbuf, sem):
    b = pl.program_id(0)
    qi = pl.program_id(1)
    nq = pl.num_programs(1)
    n_ht = 2 * DA_HEADS

    def kv_copies(j, slot):
        row0 = pl.multiple_of((b * nq + j) * ATT_T, ATT_T)
        kc = pltpu.make_async_copy(k_hbm.at[pl.ds(row0, ATT_T), pl.ds(2 * DA_WIDTH, DA_WIDTH)],
                                   kbuf.at[slot], sem.at[0, slot])
        vc = pltpu.make_async_copy(vt_hbm.at[pl.ds(DA_WIDTH, DA_WIDTH), pl.ds(row0, ATT_T)],
                                   vbuf.at[slot], sem.at[1, slot])
        return kc, vc

    def start_kv(j):
        for c in kv_copies(j, j % ATT_SLOTS):
            c.start()

    def wait_kv(j):
        for c in kv_copies(j, j % ATT_SLOTS):
            c.wait()

    start_kv(0)

    @pl.when(qi >= 1)
    def _second_tile():
        start_kv(1)

    qt = qt_ref[...]
    for ht in range(n_ht):
        qm_ref[ht] = jnp.where(_group_mask(qt.shape, 0, DA_HEAD_DIM, ht), qt, jnp.zeros_like(qt))
    m_ref[...] = jnp.full(m_ref.shape, NEG_BIG, F32)
    l_ref[...] = jnp.zeros(l_ref.shape, F32)
    acc_ref[...] = jnp.zeros(acc_ref.shape, F32)
    e_ref[1] = jnp.zeros(e_ref.shape[1:], BF16)
    al_ref[1] = jnp.ones(al_ref.shape[1:], F32)
    vbuf[ATT_SLOTS - 1] = jnp.zeros(vbuf.shape[1:], BF16)

    def scores(slot, a):
        s_ref[a % 2] = jnp.dot(kbuf[slot], qm_ref[a], preferred_element_type=F32)

    def softmax(t, bias_idx):
        buf = t % 2
        for c0 in range(0, ATT_T, LANES):
            cols = slice(c0, c0 + LANES)

            def block():
                blk = s_ref[buf, :, cols]
                return blk if bias_idx is None else blk + bias_ref[bias_idx, t // 2, :, cols]

            m_prev = m_ref[t, :, cols]
            m_new = jnp.maximum(m_prev, jnp.max(block(), axis=0, keepdims=True))
            alpha = jnp.exp2(m_prev - m_new)
            e = jnp.exp2(block() - m_new)
            l_ref[t, :, cols] = alpha * l_ref[t, :, cols] + jnp.sum(e, axis=0, keepdims=True)
            m_ref[t, :, cols] = m_new
            al_ref[buf, :, cols] = alpha
            e_ref[buf, :, cols] = e.astype(BF16)

    def values(slot, c):
        r0 = (c // 4) * LANES
        pv = jnp.dot(vbuf[slot, r0:r0 + LANES, :], e_ref[c % 2], preferred_element_type=F32)
        acc_ref[c] = acc_ref[c] * al_ref[c % 2] + pv

    def tile(j, bias_idx, last):
        slot = j % ATT_SLOTS
        if not last:
            wait_kv(j + 1)
        for t in range(n_ht):
            if t + 1 < n_ht:
                scores(slot, t + 1)
            elif not last:
                scores((j + 1) % ATT_SLOTS, 0)
            softmax(t, bias_idx)
            if t >= 1:
                values(slot, t - 1)
            else:
                values((j + ATT_SLOTS - 1) % ATT_SLOTS, n_ht - 1)

    wait_kv(0)
    scores(0, 0)

    def far_tile(j, carry):
        start_kv(j + 2)
        tile(j, None, False)
        return carry

    lax.fori_loop(0, jnp.maximum(qi - 1, 0), far_tile, 0)

    @pl.when(qi >= 1)
    def _adjacent():
        tile(qi - 1, 1, False)

    tile(qi, 0, True)
    values(qi % ATT_SLOTS, n_ht - 1)

    lam = lam_ref[...]
    row_lo = lax.broadcasted_iota(jnp.int32, (LANES, ATT_T), 0) < DA_V_DIM
    for pair in range(DA_HEADS // 2):
        halves = []
        for sub in range(2):
            h = 2 * pair + sub
            o = acc_ref[2 * h] * (1.0 / l_ref[2 * h]) - lam * (acc_ref[2 * h + 1] * (1.0 / l_ref[2 * h + 1]))
            own = row_lo if sub == 0 else jnp.logical_not(row_lo)
            ms = jnp.sum(jnp.where(own, o * o, 0.0), axis=0, keepdims=True) * (1.0 / DA_V_DIM)
            halves.append(o * lax.rsqrt(ms + EPS))
        r0 = pair * LANES
        blk = jnp.where(row_lo, halves[0], halves[1]) * gain_ref[r0:r0 + LANES, :]
        o_ref[:, r0:r0 + LANES] = blk.T.astype(BF16)


def _attention(qkv, qvt, bias_tiles, gain, lam, batch, seq):
    nq = seq // ATT_T
    n_ht = 2 * DA_HEADS
    return pl.pallas_call(
        _attn_kernel,
        grid=(batch, nq),
        in_specs=[
            pl.BlockSpec((DA_WIDTH, ATT_T), lambda b, qi: (0, b * nq + qi)),
            pl.BlockSpec(memory_space=pl.ANY),
            pl.BlockSpec(memory_space=pl.ANY),
            pl.BlockSpec((2, DA_HEADS, ATT_T, ATT_T), lambda b, qi: (0, 0, 0, 0)),
            pl.BlockSpec((DA_WIDTH, 1), lambda b, qi: (0, 0)),
            pl.BlockSpec((1, ATT_T), lambda b, qi: (0, 0)),
        ],
        out_specs=pl.BlockSpec((ATT_T, DA_WIDTH), lambda b, qi: (b * nq + qi, 0)),
        out_shape=jax.ShapeDtypeStruct((batch * seq, DA_WIDTH), BF16),
        scratch_shapes=[
            pltpu.VMEM((n_ht, DA_WIDTH, ATT_T), BF16),
            pltpu.VMEM((n_ht, 1, ATT_T), F32),
            pltpu.VMEM((n_ht, 1, ATT_T), F32),
            pltpu.VMEM((n_ht, LANES, ATT_T), F32),
            pltpu.VMEM((2, ATT_T, ATT_T), F32),
            pltpu.VMEM((2, ATT_T, ATT_T), BF16),
            pltpu.VMEM((2, 1, ATT_T), F32),
            pltpu.VMEM((ATT_SLOTS, ATT_T, DA_WIDTH), BF16),
            pltpu.VMEM((ATT_SLOTS, DA_WIDTH, ATT_T), BF16),
            pltpu.SemaphoreType.DMA((2, ATT_SLOTS)),
        ],
        compiler_params=_cparams(("parallel", "arbitrary")),
        name="diff_attention",
    )(qvt, qkv, qvt, bias_tiles, gain, lam)


def _t5_bucket(rel):
    half = REL_BUCKETS // 2
    max_exact = half // 2
    ret = jnp.where(rel > 0, half, 0)
    n = jnp.abs(rel)
    nf = jnp.maximum(n, 1).astype(F32)
    large = max_exact + (jnp.log(nf / max_exact) / math.log(REL_MAX_DIST / max_exact)
                         * (half - max_exact)).astype(jnp.int32)
    large = jnp.minimum(large, half - 1)
    return ret + jnp.where(n < max_exact, n, large)


def _bias_tiles(rel_bias):
    i = jnp.arange(ATT_T)[None, :]
    j = jnp.arange(ATT_T)[:, None]
    far = rel_bias[_t5_bucket(jnp.int32(-2 * ATT_T))].astype(F32)
    tiles = []
    for off in range(2):
        bucket = _t5_bucket((j - off * ATT_T) - i)
        b = jnp.zeros((DA_HEADS, ATT_T, ATT_T), F32)
        for c in range(REL_BUCKETS):
            b = jnp.where((bucket == c)[None], rel_bias[c].astype(F32)[:, None, None], b)
        b = (b - far[:, None, None]) * LOG2E
        if off == 0:
            allowed = (j // CHUNK) <= (i // CHUNK)
            b = jnp.where(allowed[None], b, NEG_BIG)
        tiles.append(b)
    return jnp.stack(tiles)


def _cumsum_rows(x):
    n = x.shape[0]
    row = lax.broadcasted_iota(jnp.int32, x.shape, 0)
    sh = 1
    while sh < n:
        x = x + jnp.where(row >= sh, pltpu.roll(x, sh, axis=0), 0.0)
        sh *= 2
    return x


def _expand_heads(v, width, group):
    out = jnp.zeros((v.shape[0], width), F32)
    for h in range(width // group):
        col = jnp.broadcast_to(v[:, h:h + 1], (v.shape[0], width))
        out = jnp.where(_group_mask(out.shape, 1, group, h), col, out)
    return out


def _ssd_kernel(z_ref, xs_ref, bc_ref, dt_ref, cwx_ref, cwb_ref, cbx_ref, cbb_ref, dtb_ref, alog_ref,
                dskip_ref, ng_ref, o_ref, extx_ref, extb_ref, st_ref):
    j = pl.program_id(1)

    @pl.when(j == 0)
    def _reset():
        extx_ref[0:8, :] = jnp.zeros((8, SSM_WIDTH), F32)
        extb_ref[0:8, :] = jnp.zeros((8, 2 * LANES), F32)
        st_ref[...] = jnp.zeros(st_ref.shape, F32)

    extx_ref[8:8 + BLK, :] = xs_ref[...]
    extb_ref[8:8 + BLK, :] = bc_ref[...]
    yx = jnp.broadcast_to(cbx_ref[...], (BLK, SSM_WIDTH))
    yb = jnp.broadcast_to(cbb_ref[...], (BLK, 2 * LANES))
    for w in range(SSM_CONV):
        lo = 8 - (SSM_CONV - 1) + w
        yx = yx + extx_ref[lo:lo + BLK, :] * cwx_ref[w:w + 1, :]
        yb = yb + extb_ref[lo:lo + BLK, :] * cwb_ref[w:w + 1, :]
    tail_x = extx_ref[BLK:BLK + 8, :]
    tail_b = extb_ref[BLK:BLK + 8, :]
    extx_ref[0:8, :] = tail_x
    extb_ref[0:8, :] = tail_b
    xs = _silu(yx)
    bc = _silu(yb)
    bm = bc[:, :LANES]
    cm = bc[:, LANES:]

    dt = jax.nn.softplus(dt_ref[...] + dtb_ref[...])
    a = -jnp.exp(alog_ref[...]) * dt
    a_cs = _cumsum_rows(a)
    a_cs_t = a_cs.T
    a_last = a_cs[BLK - 1:BLK, :]

    dt_full = _expand_heads(dt, SSM_WIDTH, SSM_HEAD_DIM)
    ea_full = _expand_heads(jnp.exp(a_cs), SSM_WIDTH, SSM_HEAD_DIM)
    dec_full = _expand_heads(jnp.exp(a_last - a_cs), SSM_WIDTH, SSM_HEAD_DIM)
    cdec_full = _expand_heads(jnp.exp(a_last), SSM_WIDTH, SSM_HEAD_DIM)

    xdt = xs * dt_full
    xdt_b = xdt.astype(BF16)
    cm_b = cm.astype(BF16)
    bm_b = bm.astype(BF16)

    st = st_ref[...]
    y = jnp.dot(cm_b, st.astype(BF16), preferred_element_type=F32) * ea_full + xs * dskip_ref[...]

    row = lax.broadcasted_iota(jnp.int32, (BLK, BLK), 0)
    colio = lax.broadcasted_iota(jnp.int32, (BLK, BLK), 1)
    causal = colio <= row
    rep = SSM_HEADS // SSM_GROUPS
    lane_lo = lax.broadcasted_iota(jnp.int32, (BLK, LANES), 1) < SSM_HEAD_DIM
    pieces = []
    for g in range(SSM_GROUPS):
        cg = jnp.where(_group_mask(cm_b.shape, 1, SSM_STATE, g), cm_b, jnp.zeros_like(cm_b))
        cb = lax.dot_general(cg, bm_b, (((1,), (1,)), ((), ())), preferred_element_type=F32)
        for pair in range(rep // 2):
            halves = []
            for sub in range(2):
                h = g * rep + pair * 2 + sub
                seg = jnp.broadcast_to(a_cs[:, h:h + 1], (BLK, BLK)) - a_cs_t[h:h + 1, :]
                lmat = jnp.exp(jnp.where(causal, seg, NEG_BIG))
                mh = (cb * lmat).astype(BF16)
                c0 = (h // 2) * LANES
                halves.append(jnp.dot(mh, xdt_b[:, c0:c0 + LANES], preferred_element_type=F32))
            pieces.append(jnp.where(lane_lo, halves[0], halves[1]))
    y = y + jnp.concatenate(pieces, axis=1)

    upd = jnp.dot(bm.T.astype(BF16), (xdt * dec_full).astype(BF16), preferred_element_type=F32)
    r_io = lax.broadcasted_iota(jnp.int32, upd.shape, 0) // SSM_STATE
    c_io = lax.broadcasted_iota(jnp.int32, upd.shape, 1) // (SSM_HEAD_DIM * rep)
    st_ref[...] = st * cdec_full + jnp.where(r_io == c_io, upd, 0.0)

    y = y * _silu(z_ref[...])
    gw = SSM_WIDTH // SSM_GROUPS
    for g in range(SSM_GROUPS):
        yg = y[:, g * gw:(g + 1) * gw]
        ms = jnp.mean(yg * yg, axis=-1, keepdims=True)
        o_ref[:, g * gw:(g + 1) * gw] = (yg * lax.rsqrt(ms + EPS) * ng_ref[:, g * gw:(g + 1) * gw]).astype(BF16)


def _ssd(pf, cwx, cwb, cbx, cbb, dtb, alog, dskip, ng, batch, seq):
    nb = seq // BLK
    row = lambda w, c: pl.BlockSpec((BLK, w), lambda b, j: (b * nb + j, c))
    const = lambda r, w: pl.BlockSpec((r, w), lambda b, j: (0, 0))
    return pl.pallas_call(
        _ssd_kernel,
        grid=(batch, nb),
        in_specs=[
            row(SSM_WIDTH, C_Z // SSM_WIDTH), row(SSM_WIDTH, C_XS // SSM_WIDTH), row(2 * LANES, C_BC // (2 * LANES)),
            row(LANES, C_DT // LANES),
            const(SSM_CONV, SSM_WIDTH), const(SSM_CONV, 2 * LANES), const(1, SSM_WIDTH), const(1, 2 * LANES),
            const(1, LANES), const(1, LANES), const(1, SSM_WIDTH), const(1, SSM_WIDTH),
        ],
        out_specs=pl.BlockSpec((BLK, SSM_WIDTH), lambda b, j: (b * nb + j, 0)),
        out_shape=jax.ShapeDtypeStruct((batch * seq, SSM_WIDTH), BF16),
        scratch_shapes=[
            pltpu.VMEM((BLK + 8, SSM_WIDTH), F32),
            pltpu.VMEM((BLK + 8, 2 * LANES), F32),
            pltpu.VMEM((SSM_GROUPS * SSM_STATE, SSM_WIDTH), F32),
        ],
        compiler_params=_cparams(("parallel", "arbitrary")),
        name="ssd_mixer",
    )(pf, pf, pf, pf, cwx, cwb, cbx, cbb, dtb, alog, dskip, ng)


def _rotary(u, cos, sin_signed):
    first = (lax.broadcasted_iota(jnp.int32, u.shape, 1) % RET_KEY_DIM) < (RET_KEY_DIM // 2)
    half = RET_KEY_DIM // 2
    swapped = jnp.where(first, pltpu.roll(u, u.shape[1] - half, axis=1), pltpu.roll(u, half, axis=1))
    return u * cos + swapped * sin_signed


def _ret_kernel(q_ref, k_ref, v_ref, g_ref, cos_ref, sin_ref, dmat_ref, qdec_ref, kdec_ref, gdec_ref,
                bd_ref, gain_ref, o_ref, st_ref):
    j = pl.program_id(1)

    @pl.when(j == 0)
    def _reset():
        st_ref[...] = jnp.zeros(st_ref.shape, F32)

    cos = _lane_tile(cos_ref[...], RET_WIDTH // LANES)
    sin = _lane_tile(sin_ref[...], RET_WIDTH // LANES)
    q = _rotary(q_ref[...], cos, sin)
    k = _rotary(k_ref[...], cos, sin) * (RET_KEY_DIM ** -0.5)
    v_b = v_ref[...].astype(BF16)
    q_b = q.astype(BF16)
    k_b = k.astype(BF16)

    st = st_ref[...]
    o = jnp.dot((q * qdec_ref[...]).astype(BF16), st.astype(BF16), preferred_element_type=F32)
    for h in range(RET_HEADS):
        cm = _group_mask(q_b.shape, 1, RET_KEY_DIM, h)
        qh = jnp.where(cm, q_b, jnp.zeros_like(q_b))
        s = lax.dot_general(qh, k_b, (((1,), (1,)), ((), ())), preferred_element_type=F32) * dmat_ref[h]
        oh = jnp.dot(s.astype(BF16), v_b, preferred_element_type=F32)
        o = o + jnp.where(cm, oh, 0.0)

    upd = jnp.dot((k * kdec_ref[...]).T.astype(BF16), v_b, preferred_element_type=F32)
    st_ref[...] = st * gdec_ref[...] + upd * bd_ref[...]

    out = jnp.zeros(o.shape, F32)
    for h in range(RET_HEADS):
        cm = _group_mask(o.shape, 1, RET_KEY_DIM, h)
        ms = jnp.sum(jnp.where(cm, o * o, 0.0), axis=-1, keepdims=True) * (1.0 / RET_KEY_DIM)
        out = out + jnp.where(cm, o * lax.rsqrt(ms + EPS), 0.0)
    o_ref[...] = (_silu(g_ref[...]) * (out * gain_ref[...])).astype(BF16)


def _retention(pf, cos, sin, dmat, qdec, kdec, gdec, bd, gain, batch, seq):
    nb = seq // BLK
    row = lambda c: pl.BlockSpec((BLK, RET_WIDTH), lambda b, j: (b * nb + j, c))
    const2 = pl.BlockSpec((BLK, RET_WIDTH), lambda b, j: (0, 0))
    return pl.pallas_call(
        _ret_kernel,
        grid=(batch, nb),
        in_specs=[
            row(C_RQ // RET_WIDTH), row(C_RK // RET_WIDTH), row(C_RV // RET_WIDTH), row(C_RG // RET_WIDTH),
            pl.BlockSpec((BLK, LANES), lambda b, j: (j, 0)),
            pl.BlockSpec((BLK, LANES), lambda b, j: (j, 0)),
            pl.BlockSpec((RET_HEADS, BLK, BLK), lambda b, j: (0, 0, 0)),
            const2, const2, const2, const2,
            pl.BlockSpec((1, RET_WIDTH), lambda b, j: (0, 0)),
        ],
        out_specs=pl.BlockSpec((BLK, RET_WIDTH), lambda b, j: (b * nb + j, 0)),
        out_shape=jax.ShapeDtypeStruct((batch * seq, RET_WIDTH), BF16),
        scratch_shapes=[pltpu.VMEM((RET_WIDTH, RET_WIDTH), F32)],
        compiler_params=_cparams(("parallel", "arbitrary")),
        name="retention",
    )(pf, pf, pf, pf, cos, sin, dmat, qdec, kdec, gdec, bd, gain)


def _retention_tables(seq):
    f32 = np.float32
    inv = (f32(1.0) / (f32(ROPE_BASE) ** (np.arange(0, RET_KEY_DIM, 2, dtype=f32) / f32(RET_KEY_DIM)))).astype(f32)
    ang = (np.arange(seq, dtype=f32)[:, None] * inv[None, :]).astype(f32)
    cos_h = np.concatenate([np.cos(ang), np.cos(ang)], axis=-1)
    sin_h = np.concatenate([-np.sin(ang), np.sin(ang)], axis=-1)
    cos = np.tile(cos_h, (1, 2)).astype(f32)
    sin = np.tile(sin_h, (1, 2)).astype(f32)
    log_gamma = np.log1p(-np.power(f32(2.0), f32(-5.0) - np.arange(RET_HEADS, dtype=f32))).astype(f32)
    idx = np.arange(BLK, dtype=f32)
    ii = np.arange(BLK)
    same_or_earlier_chunk = (ii[None, :] // CHUNK) <= (ii[:, None] // CHUNK)
    dmat = np.exp(log_gamma[:, None, None] * np.abs(idx[:, None] - idx[None, :]))
    dmat = np.where(same_or_earlier_chunk[None], dmat, 0.0).astype(f32)
    lg_cols = np.repeat(log_gamma, RET_KEY_DIM)
    qdec = np.exp(lg_cols[None, :] * (idx + 1.0)[:, None]).astype(f32)
    kdec = np.exp(lg_cols[None, :] * (BLK - 1.0 - idx)[:, None]).astype(f32)
    head_of = np.arange(RET_WIDTH) // RET_KEY_DIM
    bd = (head_of[:, None] == head_of[None, :]).astype(f32)
    gdec = (np.exp(lg_cols * f32(BLK))[:, None] * bd).astype(f32)
    return tuple(jnp.asarray(a) for a in (cos, sin, dmat, qdec, kdec, gdec, bd))


def _outproj_kernel(x_ref, a_ref, b_ref, c_ref, w_ref, g_ref, *rest, with_router):
    if with_router:
        rw_ref, xo_ref, h_ref, rinfo_ref = rest
    else:
        xo_ref, h_ref = rest
    acc = x_ref[...]
    acc = acc + jnp.dot(a_ref[...], w_ref[0:DA_WIDTH, :], preferred_element_type=F32)
    acc = acc + jnp.dot(b_ref[...], w_ref[DA_WIDTH:DA_WIDTH + SSM_WIDTH, :], preferred_element_type=F32)
    acc = acc + jnp.dot(c_ref[...], w_ref[DA_WIDTH + SSM_WIDTH:, :], preferred_element_type=F32)
    xo_ref[...] = acc
    ms = jnp.mean(acc * acc, axis=-1, keepdims=True)
    hf = acc * lax.rsqrt(ms + EPS) * g_ref[...]
    h = hf.astype(BF16)
    h_ref[...] = hf.astype(h_ref.dtype)
    if with_router:
        logits = jnp.dot(h, rw_ref[...], preferred_element_type=F32)
        lane = lax.broadcasted_iota(jnp.int32, logits.shape, 1).astype(F32)
        logits = jnp.where(lane < N_EXPERTS, logits, NEG_BIG)
        v1 = jnp.max(logits, axis=-1, keepdims=True)
        i1 = jnp.min(jnp.where(logits == v1, lane, float(LANES)), axis=-1, keepdims=True)
        rest_l = jnp.where(lane == i1, NEG_BIG, logits)
        v2 = jnp.max(rest_l, axis=-1, keepdims=True)
        i2 = jnp.min(jnp.where(rest_l == v2, lane, float(LANES)), axis=-1, keepdims=True)
        e2 = jnp.exp(v2 - v1)
        g1 = 1.0 / (1.0 + e2)
        g2 = e2 * g1
        rinfo_ref[...] = (jnp.where(lane == 0.0, g1, 0.0) + jnp.where(lane == 1.0, g2, 0.0)
                          + jnp.where(lane == 2.0, i1, 0.0) + jnp.where(lane == 3.0, i2, 0.0))


def _outproj(x2, oa, ob, oc, w, gain, router_w=None):
    t = x2.shape[0]
    with_router = router_w is not None
    tok = lambda w_: pl.BlockSpec((TM_PROJ, w_), lambda i: (i, 0))
    in_specs = [tok(D_MODEL), tok(DA_WIDTH), tok(SSM_WIDTH), tok(RET_WIDTH),
                pl.BlockSpec((D_MODEL, D_MODEL), lambda i: (0, 0)),
                pl.BlockSpec((1, D_MODEL), lambda i: (0, 0))]
    out_specs = [tok(D_MODEL), tok(D_MODEL)]
    h_dtype = F32 if with_router else BF16
    out_shape = [jax.ShapeDtypeStruct((t, D_MODEL), F32), jax.ShapeDtypeStruct((t, D_MODEL), h_dtype)]
    args = [x2, oa, ob, oc, w, gain]
    if with_router:
        in_specs.append(pl.BlockSpec((D_MODEL, LANES), lambda i: (0, 0)))
        out_specs.append(tok(LANES))
        out_shape.append(jax.ShapeDtypeStruct((t, LANES), F32))
        args.append(router_w)
    return pl.pallas_call(
        functools.partial(_outproj_kernel, with_router=with_router),
        grid=(t // TM_PROJ,),
        in_specs=in_specs,
        out_specs=out_specs,
        out_shape=out_shape,
        compiler_params=_cparams(("parallel",)),
        name="outproj_router" if with_router else "outproj",
    )(*args)


def _ffn_kernel(x_ref, h_ref, wg_ref, wu_ref, wd_ref, o_ref, acc_ref):
    f = pl.program_id(1)

    @pl.when(f == 0)
    def _init():
        acc_ref[...] = x_ref[...]

    h = h_ref[...]
    gate = jnp.dot(h, wg_ref[0], preferred_element_type=F32)
    up = jnp.dot(h, wu_ref[0], preferred_element_type=F32)
    act = (_silu(gate) * up).astype(BF16)
    acc_ref[...] += jnp.dot(act, wd_ref[...], preferred_element_type=F32)

    @pl.when(f == pl.num_programs(1) - 1)
    def _done():
        o_ref[...] = acc_ref[...]


def _dense_ffn(x2, h, wg, wu, wd):
    t = x2.shape[0]
    nf = wg.shape[0]
    return pl.pallas_call(
        _ffn_kernel,
        grid=(t // TM_FFN, nf),
        in_specs=[
            pl.BlockSpec((TM_FFN, D_MODEL), lambda i, f: (i, 0)),
            pl.BlockSpec((TM_FFN, D_MODEL), lambda i, f: (i, 0)),
            pl.BlockSpec((1, D_MODEL, TF_FFN), lambda i, f: (f, 0, 0)),
            pl.BlockSpec((1, D_MODEL, TF_FFN), lambda i, f: (f, 0, 0)),
            pl.BlockSpec((TF_FFN, D_MODEL), lambda i, f: (f, 0)),
        ],
        out_specs=pl.BlockSpec((TM_FFN, D_MODEL), lambda i, f: (i, 0)),
        out_shape=jax.ShapeDtypeStruct((t, D_MODEL), F32),
        scratch_shapes=[pltpu.VMEM((TM_FFN, D_MODEL), F32)],
        compiler_params=_cparams(("parallel", "arbitrary")),
        name="dense_swiglu",
    )(x2, h, wg, wu, wd)


def _issue_row_gather(src_hbm, idx_ref, idx_base, dst, sem, n_rows):
    def body(r, carry):
        tok = idx_ref[idx_base + r]
        pltpu.make_async_copy(src_hbm.at[pl.ds(tok, 1)], dst.at[pl.ds(r, 1)], sem).start()
        return carry
    lax.fori_loop(0, n_rows, body, 0, unroll=8)


def _wait_row_gather(src_hbm, dst, sem, n_rows):
    pltpu.make_async_copy(src_hbm.at[pl.ds(0, n_rows)], dst, sem).wait()


def _moe_ffn_kernel(te_ref, nv_ref, rt_ref, h_hbm, wg_ref, wu_ref, wd_ref, y_ref, hbuf, xb_ref, acc_ref, sem):
    i = pl.program_id(0)
    f = pl.program_id(1)
    n_valid = nv_ref[0]
    valid = i < n_valid
    slot = i % 2

    @pl.when((f == 0) & (i == 0) & valid)
    def _first():
        _issue_row_gather(h_hbm, rt_ref, 0, hbuf.at[0], sem.at[0], TR_MOE)

    @pl.when((f == 0) & valid)
    def _arrive():
        _wait_row_gather(h_hbm, hbuf.at[slot], sem.at[slot], TR_MOE)
        xb_ref[...] = hbuf[slot].astype(BF16)

    def compute(prefetch_next):
        if prefetch_next:
            rows_per_step = TR_MOE // (FFN_EXPERT // TF_MOE)
            r0 = f * rows_per_step
            for j in range(rows_per_step):
                tok = rt_ref[(i + 1) * TR_MOE + r0 + j]
                pltpu.make_async_copy(h_hbm.at[pl.ds(tok, 1)], hbuf.at[1 - slot, pl.ds(r0 + j, 1)],
                                      sem.at[1 - slot]).start()
        xb = xb_ref[...]
        gate = jnp.dot(xb, wg_ref[0], preferred_element_type=F32)
        up = jnp.dot(xb, wu_ref[0], preferred_element_type=F32)
        act = (_silu(gate) * up).astype(BF16)
        contrib = jnp.dot(act, wd_ref[0], preferred_element_type=F32)

        @pl.when(f == 0)
        def _():
            acc_ref[...] = contrib

        @pl.when(f > 0)
        def _():
            acc_ref[...] += contrib

    has_next = i + 1 < n_valid

    @pl.when(valid & has_next)
    def _compute_and_prefetch():
        compute(True)

    @pl.when(valid & jnp.logical_not(has_next))
    def _compute_last():
        compute(False)

    @pl.when(f == pl.num_programs(1) - 1)
    def _store():
        y_ref[...] = jnp.where(valid, acc_ref[...], 0.0)


def _moe_ffn(h, tile_expert, n_valid, row_token, wg, wu, wd):
    n_tiles = tile_expert.shape[0]
    nf = wg.shape[2] // TF_MOE

    def w_in_map(i, f, te, nv, rt):
        return (te[i], 0, jnp.where(i < nv[0], f, nf - 1))

    def w_out_map(i, f, te, nv, rt):
        return (te[i], jnp.where(i < nv[0], f, nf - 1), 0)

    grid_spec = pltpu.PrefetchScalarGridSpec(
        num_scalar_prefetch=3,
        grid=(n_tiles, nf),
        in_specs=[
            pl.BlockSpec(memory_space=pl.ANY),
            pl.BlockSpec((1, D_MODEL, TF_MOE), w_in_map),
            pl.BlockSpec((1, D_MODEL, TF_MOE), w_in_map),
            pl.BlockSpec((1, TF_MOE, D_MODEL), w_out_map),
        ],
        out_specs=pl.BlockSpec((TR_MOE, D_MODEL), lambda i, f, te, nv, rt: (i, 0)),
        scratch_shapes=[
            pltpu.VMEM((2, TR_MOE, D_MODEL), F32),
            pltpu.VMEM((TR_MOE, D_MODEL), BF16),
            pltpu.VMEM((TR_MOE, D_MODEL), F32),
            pltpu.SemaphoreType.DMA((2,)),
        ],
    )
    return pl.pallas_call(
        _moe_ffn_kernel,
        grid_spec=grid_spec,
        out_shape=jax.ShapeDtypeStruct((n_tiles * TR_MOE, D_MODEL), F32),
        compiler_params=_cparams(("arbitrary", "arbitrary")),
        name="moe_expert_ffn",
    )(tile_expert, n_valid, row_token, h, wg, wu, wd)


def _moe_combine_kernel(tr_ref, x_ref, rinfo_ref, y_hbm, fg_ref, o_ref, ybuf, sem):
    i = pl.program_id(0)
    slot = i % 2
    n_rows = TOP_K * TM_COMB

    @pl.when(i == 0)
    def _first():
        _issue_row_gather(y_hbm, tr_ref, 0, ybuf.at[0], sem.at[0], n_rows)

    _wait_row_gather(y_hbm, ybuf.at[slot], sem.at[slot], n_rows)

    @pl.when(i + 1 < pl.num_programs(0))
    def _prefetch():
        _issue_row_gather(y_hbm, tr_ref, (i + 1) * n_rows, ybuf.at[1 - slot], sem.at[1 - slot], n_rows)

    rinfo = rinfo_ref[...]
    g1 = jnp.broadcast_to(rinfo[:, 0:1], (TM_COMB, D_MODEL))
    g2 = jnp.broadcast_to(rinfo[:, 1:2], (TM_COMB, D_MODEL))
    y = x_ref[...] + g1 * ybuf[slot, 0:TM_COMB, :] + g2 * ybuf[slot, TM_COMB:n_rows, :]
    ms = jnp.mean(y * y, axis=-1, keepdims=True)
    o_ref[...] = y * lax.rsqrt(ms + EPS) * fg_ref[...]


def _moe_combine(x2, rinfo, y_rows, tok_rows, final_gain):
    t = x2.shape[0]
    grid_spec = pltpu.PrefetchScalarGridSpec(
        num_scalar_prefetch=1,
        grid=(t // TM_COMB,),
        in_specs=[
            pl.BlockSpec((TM_COMB, D_MODEL), lambda i, tr: (i, 0)),
            pl.BlockSpec((TM_COMB, LANES), lambda i, tr: (i, 0)),
            pl.BlockSpec(memory_space=pl.ANY),
            pl.BlockSpec((1, D_MODEL), lambda i, tr: (0, 0)),
        ],
        out_specs=pl.BlockSpec((TM_COMB, D_MODEL), lambda i, tr: (i, 0)),
        scratch_shapes=[
            pltpu.VMEM((2, TOP_K * TM_COMB, D_MODEL), F32),
            pltpu.SemaphoreType.DMA((2,)),
        ],
    )
    return pl.pallas_call(
        _moe_combine_kernel,
        grid_spec=grid_spec,
        out_shape=jax.ShapeDtypeStruct((t, D_MODEL), F32),
        compiler_params=_cparams(("arbitrary",)),
        name="moe_combine_norm",
    )(tok_rows, x2, rinfo, y_rows, final_gain)


def _routing_tables(rinfo):
    t = rinfo.shape[0]
    n_tiles = -(-(TOP_K * t) // TR_MOE) + N_EXPERTS
    flat_e = rinfo[:, 2:2 + TOP_K].astype(jnp.int32).reshape(-1)
    onehot = (flat_e[:, None] == jnp.arange(N_EXPERTS)[None, :]).astype(jnp.int32)
    csum = jnp.cumsum(onehot, axis=0)
    rank = jnp.sum((csum - onehot) * onehot, axis=1)
    counts = csum[-1]
    tiles_per_e = (counts + TR_MOE - 1) // TR_MOE
    tile_end = jnp.cumsum(tiles_per_e)
    row_off = (tile_end - tiles_per_e) * TR_MOE
    dest = row_off[flat_e] + rank
    token = jnp.arange(TOP_K * t, dtype=jnp.int32) // TOP_K
    row_token = jnp.zeros((n_tiles * TR_MOE,), jnp.int32).at[dest].set(token, unique_indices=True)
    n_valid = tile_end[-1:].astype(jnp.int32)
    tile_expert = jnp.searchsorted(tile_end, jnp.arange(n_tiles), side="right").astype(jnp.int32)
    last_e = jnp.searchsorted(tile_end, n_valid[0] - 1, side="right").astype(jnp.int32)
    tile_expert = jnp.where(jnp.arange(n_tiles) < n_valid[0], tile_expert, last_e)
    tok_rows = dest.reshape(t // TM_COMB, TM_COMB, TOP_K).transpose(0, 2, 1).reshape(-1).astype(jnp.int32)
    return tile_expert, n_valid, row_token, tok_rows


def _permute_w_in(w):
    sizes = (256, 256, 256, 512, 768, 8, 256, 256, 256, 256)
    offs = np.concatenate([[0], np.cumsum(sizes)])
    part = lambda n: w[:, offs[n]:offs[n + 1]]
    dt_pad = jnp.zeros((w.shape[0], LANES - SSM_HEADS), w.dtype)
    cols = [part(0) * (DA_HEAD_DIM ** -0.5 * LOG2E), part(2), part(1), part(3), part(4),
            part(6), part(7), part(8), part(9), part(5), dt_pad]
    return jnp.concatenate(cols, axis=1).astype(BF16)


def _column_tiles(w, tile):
    *lead, k, n = w.shape
    return jnp.moveaxis(w.reshape(*lead, k, n // tile, tile), -2, -3)


def _pad_lanes(v, fill=0.0):
    return jnp.concatenate([v, jnp.full((LANES - v.shape[0],), fill, v.dtype)])[None, :]


def kernel(x, w_in, w_out, attn_norm, ffn_norm, final_norm, rel_bias, lambda_q1, lambda_k1, lambda_q2,
           lambda_k2, da_head_norm, conv_w, conv_b, dt_bias, a_log, d_skip, ssm_norm, ret_head_norm,
           w_gate, w_up, w_down, router_w, e_gate, e_up, e_down):
    batch, seq, _ = x.shape
    x2 = x.reshape(batch * seq, D_MODEL)
    bias_tiles = _bias_tiles(rel_bias)
    ret_tabs = _retention_tables(seq)

    for layer in range(DEPTH):
        lam_init = 0.8 - 0.6 * math.exp(-0.3 * layer)
        lam = (jnp.exp(jnp.sum(lambda_q1[layer] * lambda_k1[layer]))
               - jnp.exp(jnp.sum(lambda_q2[layer] * lambda_k2[layer])) + lam_init).astype(F32)
        lam = jnp.full((1, ATT_T), lam, F32)

        qkv, qvt, pf = _inproj(x2, attn_norm[layer][None, :], _permute_w_in(w_in[layer]))

        da_gain = (jnp.tile(da_head_norm[layer], DA_HEADS) * (1.0 - lam_init))[:, None]
        out_a = _attention(qkv, qvt, bias_tiles, da_gain, lam, batch, seq)

        cw, cb = conv_w[layer], conv_b[layer]
        out_b = _ssd(pf, cw[:, :SSM_WIDTH], cw[:, SSM_WIDTH:], cb[None, :SSM_WIDTH], cb[None, SSM_WIDTH:],
                     _pad_lanes(dt_bias[layer]), _pad_lanes(a_log[layer]),
                     jnp.repeat(d_skip[layer], SSM_HEAD_DIM)[None, :], ssm_norm[layer][None, :], batch, seq)

        ret_gain = jnp.tile(ret_head_norm[layer], RET_HEADS)[None, :]
        out_c = _retention(pf, *ret_tabs, ret_gain, batch, seq)

        w_o = w_out[layer].astype(BF16)
        if layer % 2 == 0:
            i = layer // 2
            x2, h = _outproj(x2, out_a, out_b, out_c, w_o, ffn_norm[layer][None, :])
            pad = FFN_PAD - FFN_DENSE
            wg = _column_tiles(jnp.pad(w_gate[i], ((0, 0), (0, pad))).astype(BF16), TF_FFN)
            wu = _column_tiles(jnp.pad(w_up[i], ((0, 0), (0, pad))).astype(BF16), TF_FFN)
            wd = jnp.pad(w_down[i], ((0, pad), (0, 0))).astype(BF16)
            x2 = _dense_ffn(x2, h, wg, wu, wd)
        else:
            i = layer // 2
            rw = jnp.pad(router_w[i], ((0, 0), (0, LANES - N_EXPERTS))).astype(BF16)
            x2, h, rinfo = _outproj(x2, out_a, out_b, out_c, w_o, ffn_norm[layer][None, :], rw)
            tile_expert, n_valid, row_token, tok_rows = _routing_tables(rinfo)
            y_rows = _moe_ffn(h, tile_expert, n_valid, row_token,
                              e_gate[i].astype(BF16), e_up[i].astype(BF16), e_down[i].astype(BF16))
            x2 = _moe_combine(x2, rinfo, y_rows, tok_rows, final_norm[None, :])
    return x2.reshape(batch, seq, D_MODEL)
```

```python
import functools
import math

import jax
import jax.numpy as jnp
import numpy as np
from jax import lax
from jax.experimental import pallas as pl
from jax.experimental.pallas import tpu as pltpu

F32 = jnp.float32
BF16 = jnp.bfloat16

D_MODEL = 1024
DEPTH = 2
CHUNK = 64
EPS = 1e-6
DA_HEADS = 4
DA_HEAD_DIM = 32
DA_V_DIM = 64
DA_WIDTH = 256
SSM_HEADS = 8
SSM_HEAD_DIM = 64
SSM_WIDTH = 512
SSM_STATE = 64
SSM_GROUPS = 2
SSM_CONV = 4
RET_HEADS = 4
RET_KEY_DIM = 64
RET_WIDTH = 256
ROPE_BASE = 10000.0
REL_BUCKETS = 32
REL_MAX_DIST = 128
FFN_DENSE = 2752
N_EXPERTS = 8
FFN_EXPERT = 3584

LANES = 128
VMEM_LIMIT = 48 * 1024 * 1024
NEG_BIG = -1e30

TM_PROJ = 512
ATT_T = 512
LOG2E = math.log2(math.e)
BLK = 256
FFN_PAD = 2816
TM_FFN = 512
TF_FFN = 1408
TF_MOE = 512
TR_MOE = 896
TM_COMB = 256
TOP_K = 2

C_Z, C_XS, C_BC, C_RQ, C_RK, C_RV, C_RG, C_DT = 0, 512, 1024, 1280, 1536, 1792, 2048, 2304
PF_W = 2432
PF_CHUNK = 640
QKV_W = 768


def _cparams(sem):
    return pltpu.CompilerParams(dimension_semantics=sem, vmem_limit_bytes=VMEM_LIMIT)


def _group_mask(shape, axis, group, idx):
    io = lax.broadcasted_iota(jnp.int32, shape, axis)
    return (io >= idx * group) & (io < (idx + 1) * group)


def _lane_tile(x, n, axis=1):
    return jnp.concatenate([x] * n, axis=axis)


def _silu(x):
    h = 0.5 * x
    return h + h * jnp.tanh(h)


def _inproj_kernel(x_ref, g_ref, w_ref, qkv_ref, qvt_ref, pf_ref):
    x = x_ref[...]
    ms = jnp.mean(x * x, axis=-1, keepdims=True)
    h = (x * lax.rsqrt(ms + EPS) * g_ref[...]).astype(BF16)
    qkv = jnp.dot(h, w_ref[:, :QKV_W], preferred_element_type=F32)
    qkv_ref[...] = qkv.astype(BF16)
    qvt_ref[...] = qkv[:, :2 * DA_WIDTH].T.astype(BF16)
    for lo in range(0, PF_W, PF_CHUNK):
        hi = min(lo + PF_CHUNK, PF_W)
        pf_ref[:, lo:hi] = jnp.dot(h, w_ref[:, QKV_W + lo:QKV_W + hi], preferred_element_type=F32)


def _inproj(x2, gain, w):
    t = x2.shape[0]
    return pl.pallas_call(
        _inproj_kernel,
        grid=(t // TM_PROJ,),
        in_specs=[
            pl.BlockSpec((TM_PROJ, D_MODEL), lambda i: (i, 0)),
            pl.BlockSpec((1, D_MODEL), lambda i: (0, 0)),
            pl.BlockSpec((D_MODEL, QKV_W + PF_W), lambda i: (0, 0)),
        ],
        out_specs=[
            pl.BlockSpec((TM_PROJ, QKV_W), lambda i: (i, 0)),
            pl.BlockSpec((2 * DA_WIDTH, TM_PROJ), lambda i: (0, i)),
            pl.BlockSpec((TM_PROJ, PF_W), lambda i: (i, 0)),
        ],
        out_shape=[
            jax.ShapeDtypeStruct((t, QKV_W), BF16),
            jax.ShapeDtypeStruct((2 * DA_WIDTH, t), BF16),
            jax.ShapeDtypeStruct((t, PF_W), F32),
        ],
        compiler_params=_cparams(("parallel",)),
        name="inproj",
    )(x2, gain, w)


def _attn_kernel(qi_ref, kj_ref, qt_ref, k_ref, vt_ref, bias_ref, gain_ref, lam_ref, o_ref,
                 qm_ref, m_ref, l_ref, acc_ref, s_ref, e_ref, al_ref):
    p = pl.program_id(1)
    qi = qi_ref[p]
    kj = kj_ref[p]
    n_ht = 2 * DA_HEADS

    @pl.when(kj == 0)
    def _init():
        qt = qt_ref[...]
        for ht in range(n_ht):
            qm_ref[ht] = jnp.where(_group_mask(qt.shape, 0, DA_HEAD_DIM, ht), qt, jnp.zeros_like(qt))
        m_ref[...] = jnp.full(m_ref.shape, NEG_BIG, F32)
        l_ref[...] = jnp.zeros(l_ref.shape, F32)
        acc_ref[...] = jnp.zeros(acc_ref.shape, F32)

    def step(with_bias):
        k = k_ref[...]
        def scores(a, buf):
            s_ref[buf] = jnp.dot(k, qm_ref[a], preferred_element_type=F32)

        def softmax(b, buf):
            for c0 in range(0, ATT_T, LANES):
                cols = slice(c0, c0 + LANES)

                def block():
                    blk = s_ref[buf, :, cols]
                    return blk + bias_ref[0, b // 2, :, cols] if with_bias else blk

                m_prev = m_ref[b, :, cols]
                m_new = jnp.maximum(m_prev, jnp.max(block(), axis=0, keepdims=True))
                alpha = jnp.exp2(m_prev - m_new)
                e = jnp.exp2(block() - m_new)
                l_ref[b, :, cols] = alpha * l_ref[b, :, cols] + jnp.sum(e, axis=0, keepdims=True)
                m_ref[b, :, cols] = m_new
                al_ref[buf, :, cols] = alpha
                e_ref[buf, :, cols] = e.astype(BF16)

        def values(c, buf):
            r0 = (c // 4) * LANES
            if not isinstance(r0, int):
                r0 = pl.multiple_of(r0, LANES)
            pv = jnp.dot(vt_ref[pl.ds(r0, LANES), :], e_ref[buf], preferred_element_type=F32)
            acc_ref[c] = acc_ref[c] * al_ref[buf] + pv

        scores(0, 0)
        scores(1, 1)
        softmax(0, 0)

        def trip(u, carry):
            t = 2 * u + 1
            scores(t + 1, 0)
            softmax(t, 1)
            values(t - 1, 0)
            scores(t + 2, 1)
            softmax(t + 1, 0)
            values(t, 1)
            return carry

        for u in range((n_ht - 2) // 2):
            trip(u, 0)
        softmax(n_ht - 1, 1)
        values(n_ht - 2, 0)
        values(n_ht - 1, 1)

    @pl.when(kj >= qi - 1)
    def _near():
        step(True)

    @pl.when(kj < qi - 1)
    def _far():
        step(False)

    @pl.when(kj == qi)
    def _finish():
        lam = lam_ref[...]
        row_lo = lax.broadcasted_iota(jnp.int32, (LANES, ATT_T), 0) < DA_V_DIM
        for pair in range(DA_HEADS // 2):
            halves = []
            for sub in range(2):
                h = 2 * pair + sub
                o = acc_ref[2 * h] * (1.0 / l_ref[2 * h]) - lam * (acc_ref[2 * h + 1] * (1.0 / l_ref[2 * h + 1]))
                own = row_lo if sub == 0 else jnp.logical_not(row_lo)
                ms = jnp.sum(jnp.where(own, o * o, 0.0), axis=0, keepdims=True) * (1.0 / DA_V_DIM)
                halves.append(o * lax.rsqrt(ms + EPS))
            r0 = pair * LANES
            blk = jnp.where(row_lo, halves[0], halves[1]) * gain_ref[r0:r0 + LANES, :]
            o_ref[:, r0:r0 + LANES] = blk.T.astype(BF16)


def _attention(qkv, qvt, bias_tiles, gain, lam, batch, seq):
    nq = seq // ATT_T
    qi_l, kj_l = [], []
    for qi in range(nq):
        for kj in range(qi + 1):
            qi_l.append(qi)
            kj_l.append(kj)
    qi_tbl = jnp.asarray(np.array(qi_l, np.int32))
    kj_tbl = jnp.asarray(np.array(kj_l, np.int32))
    n_pairs = len(qi_l)
    n_ht = 2 * DA_HEADS

    grid_spec = pltpu.PrefetchScalarGridSpec(
        num_scalar_prefetch=2,
        grid=(batch, n_pairs),
        in_specs=[
            pl.BlockSpec((DA_WIDTH, ATT_T), lambda b, p, qi, kj: (0, b * nq + qi[p])),
            pl.BlockSpec((ATT_T, DA_WIDTH), lambda b, p, qi, kj: (b * nq + kj[p], 2)),
            pl.BlockSpec((DA_WIDTH, ATT_T), lambda b, p, qi, kj: (1, b * nq + kj[p])),
            pl.BlockSpec((1, DA_HEADS, ATT_T, ATT_T),
                         lambda b, p, qi, kj: (jnp.minimum(qi[p] - kj[p], 1), 0, 0, 0)),
            pl.BlockSpec((DA_WIDTH, 1), lambda b, p, qi, kj: (0, 0)),
            pl.BlockSpec((1, ATT_T), lambda b, p, qi, kj: (0, 0)),
        ],
        out_specs=pl.BlockSpec((ATT_T, DA_WIDTH), lambda b, p, qi, kj: (b * nq + qi[p], 0)),
        scratch_shapes=[
            pltpu.VMEM((n_ht, DA_WIDTH, ATT_T), BF16),
            pltpu.VMEM((n_ht, 1, ATT_T), F32),
            pltpu.VMEM((n_ht, 1, ATT_T), F32),
            pltpu.VMEM((n_ht, LANES, ATT_T), F32),
            pltpu.VMEM((2, ATT_T, ATT_T), F32),
            pltpu.VMEM((2, ATT_T, ATT_T), BF16),
            pltpu.VMEM((2, 1, ATT_T), F32),
        ],
    )
    return pl.pallas_call(
        _attn_kernel,
        grid_spec=grid_spec,
        out_shape=jax.ShapeDtypeStruct((batch * seq, DA_WIDTH), BF16),
        compiler_params=_cparams(("parallel", "arbitrary")),
        name="diff_attention",
    )(qi_tbl, kj_tbl, qvt, qkv, qvt, bias_tiles, gain, lam)


def _t5_bucket(rel):
    half = REL_BUCKETS // 2
    max_exact = half // 2
    ret = jnp.where(rel > 0, half, 0)
    n = jnp.abs(rel)
    nf = jnp.maximum(n, 1).astype(F32)
    large = max_exact + (jnp.log(nf / max_exact) / math.log(REL_MAX_DIST / max_exact)
                         * (half - max_exact)).astype(jnp.int32)
    large = jnp.minimum(large, half - 1)
    return ret + jnp.where(n < max_exact, n, large)


def _bias_tiles(rel_bias):
    i = jnp.arange(ATT_T)[None, :]
    j = jnp.arange(ATT_T)[:, None]
    far = rel_bias[_t5_bucket(jnp.int32(-2 * ATT_T))].astype(F32)
    tiles = []
    for off in range(2):
        bucket = _t5_bucket((j - off * ATT_T) - i)
        b = jnp.zeros((DA_HEADS, ATT_T, ATT_T), F32)
        for c in range(REL_BUCKETS):
            b = jnp.where((bucket == c)[None], rel_bias[c].astype(F32)[:, None, None], b)
        b = (b - far[:, None, None]) * LOG2E
        if off == 0:
            allowed = (j // CHUNK) <= (i // CHUNK)
            b = jnp.where(allowed[None], b, NEG_BIG)
        tiles.append(b)
    return jnp.stack(tiles)


def _cumsum_rows(x):
    n = x.shape[0]
    row = lax.broadcasted_iota(jnp.int32, x.shape, 0)
    sh = 1
    while sh < n:
        x = x + jnp.where(row >= sh, pltpu.roll(x, sh, axis=0), 0.0)
        sh *= 2
    return x


def _expand_heads(v, width, group):
    out = jnp.zeros((v.shape[0], width), F32)
    for h in range(width // group):
        col = jnp.broadcast_to(v[:, h:h + 1], (v.shape[0], width))
        out = jnp.where(_group_mask(out.shape, 1, group, h), col, out)
    return out


def _ssd_kernel(z_ref, xs_ref, bc_ref, dt_ref, cwx_ref, cwb_ref, cbx_ref, cbb_ref, dtb_ref, alog_ref,
                dskip_ref, ng_ref, o_ref, extx_ref, extb_ref, st_ref):
    j = pl.program_id(1)

    @pl.when(j == 0)
    def _reset():
        extx_ref[0:8, :] = jnp.zeros((8, SSM_WIDTH), F32)
        extb_ref[0:8, :] = jnp.zeros((8, 2 * LANES), F32)
        st_ref[...] = jnp.zeros(st_ref.shape, F32)

    extx_ref[8:8 + BLK, :] = xs_ref[...]
    extb_ref[8:8 + BLK, :] = bc_ref[...]
    yx = jnp.broadcast_to(cbx_ref[...], (BLK, SSM_WIDTH))
    yb = jnp.broadcast_to(cbb_ref[...], (BLK, 2 * LANES))
    for w in range(SSM_CONV):
        lo = 8 - (SSM_CONV - 1) + w
        yx = yx + extx_ref[lo:lo + BLK, :] * cwx_ref[w:w + 1, :]
        yb = yb + extb_ref[lo:lo + BLK, :] * cwb_ref[w:w + 1, :]
    tail_x = extx_ref[BLK:BLK + 8, :]
    tail_b = extb_ref[BLK:BLK + 8, :]
    extx_ref[0:8, :] = tail_x
    extb_ref[0:8, :] = tail_b
    xs = _silu(yx)
    bc = _silu(yb)
    bm = bc[:, :LANES]
    cm = bc[:, LANES:]

    dt = jax.nn.softplus(dt_ref[...] + dtb_ref[...])
    a = -jnp.exp(alog_ref[...]) * dt
    a_cs = _cumsum_rows(a)
    a_cs_t = a_cs.T
    a_last = a_cs[BLK - 1:BLK, :]

    dt_full = _expand_heads(dt, SSM_WIDTH, SSM_HEAD_DIM)
    ea_full = _expand_heads(jnp.exp(a_cs), SSM_WIDTH, SSM_HEAD_DIM)
    dec_full = _expand_heads(jnp.exp(a_last - a_cs), SSM_WIDTH, SSM_HEAD_DIM)
    cdec_full = _expand_heads(jnp.exp(a_last), SSM_WIDTH, SSM_HEAD_DIM)

    xdt = xs * dt_full
    xdt_b = xdt.astype(BF16)
    cm_b = cm.astype(BF16)
    bm_b = bm.astype(BF16)

    st = st_ref[...]
    y = jnp.dot(cm_b, st.astype(BF16), preferred_element_type=F32) * ea_full + xs * dskip_ref[...]

    row = lax.broadcasted_iota(jnp.int32, (BLK, BLK), 0)
    colio = lax.broadcasted_iota(jnp.int32, (BLK, BLK), 1)
    causal = colio <= row
    rep = SSM_HEADS // SSM_GROUPS
    lane_lo = lax.broadcasted_iota(jnp.int32, (BLK, LANES), 1) < SSM_HEAD_DIM
    pieces = []
    for g in range(SSM_GROUPS):
        cg = jnp.where(_group_mask(cm_b.shape, 1, SSM_STATE, g), cm_b, jnp.zeros_like(cm_b))
        cb = lax.dot_general(cg, bm_b, (((1,), (1,)), ((), ())), preferred_element_type=F32)
        for pair in range(rep // 2):
            halves = []
            for sub in range(2):
                h = g * rep + pair * 2 + sub
                seg = jnp.broadcast_to(a_cs[:, h:h + 1], (BLK, BLK)) - a_cs_t[h:h + 1, :]
                lmat = jnp.exp(jnp.where(causal, seg, NEG_BIG))
                mh = (cb * lmat).astype(BF16)
                c0 = (h // 2) * LANES
                halves.append(jnp.dot(mh, xdt_b[:, c0:c0 + LANES], preferred_element_type=F32))
            pieces.append(jnp.where(lane_lo, halves[0], halves[1]))
    y = y + jnp.concatenate(pieces, axis=1)

    upd = jnp.dot(bm.T.astype(BF16), (xdt * dec_full).astype(BF16), preferred_element_type=F32)
    r_io = lax.broadcasted_iota(jnp.int32, upd.shape, 0) // SSM_STATE
    c_io = lax.broadcasted_iota(jnp.int32, upd.shape, 1) // (SSM_HEAD_DIM * rep)
    st_ref[...] = st * cdec_full + jnp.where(r_io == c_io, upd, 0.0)

    y = y * _silu(z_ref[...])
    gw = SSM_WIDTH // SSM_GROUPS
    for g in range(SSM_GROUPS):
        yg = y[:, g * gw:(g + 1) * gw]
        ms = jnp.mean(yg * yg, axis=-1, keepdims=True)
        o_ref[:, g * gw:(g + 1) * gw] = (yg * lax.rsqrt(ms + EPS) * ng_ref[:, g * gw:(g + 1) * gw]).astype(BF16)


def _ssd(pf, cwx, cwb, cbx, cbb, dtb, alog, dskip, ng, batch, seq):
    nb = seq // BLK
    row = lambda w, c: pl.BlockSpec((BLK, w), lambda b, j: (b * nb + j, c))
    const = lambda r, w: pl.BlockSpec((r, w), lambda b, j: (0, 0))
    return pl.pallas_call(
        _ssd_kernel,
        grid=(batch, nb),
        in_specs=[
            row(SSM_WIDTH, C_Z // SSM_WIDTH), row(SSM_WIDTH, C_XS // SSM_WIDTH), row(2 * LANES, C_BC // (2 * LANES)),
            row(LANES, C_DT // LANES),
            const(SSM_CONV, SSM_WIDTH), const(SSM_CONV, 2 * LANES), const(1, SSM_WIDTH), const(1, 2 * LANES),
            const(1, LANES), const(1, LANES), const(1, SSM_WIDTH), const(1, SSM_WIDTH),
        ],
        out_specs=pl.BlockSpec((BLK, SSM_WIDTH), lambda b, j: (b * nb + j, 0)),
        out_shape=jax.ShapeDtypeStruct((batch * seq, SSM_WIDTH), BF16),
        scratch_shapes=[
            pltpu.VMEM((BLK + 8, SSM_WIDTH), F32),
            pltpu.VMEM((BLK + 8, 2 * LANES), F32),
            pltpu.VMEM((SSM_GROUPS * SSM_STATE, SSM_WIDTH), F32),
        ],
        compiler_params=_cparams(("parallel", "arbitrary")),
        name="ssd_mixer",
    )(pf, pf, pf, pf, cwx, cwb, cbx, cbb, dtb, alog, dskip, ng)


def _rotary(u, cos, sin_signed):
    first = (lax.broadcasted_iota(jnp.int32, u.shape, 1) % RET_KEY_DIM) < (RET_KEY_DIM // 2)
    half = RET_KEY_DIM // 2
    swapped = jnp.where(first, pltpu.roll(u, u.shape[1] - half, axis=1), pltpu.roll(u, half, axis=1))
    return u * cos + swapped * sin_signed


def _ret_kernel(q_ref, k_ref, v_ref, g_ref, cos_ref, sin_ref, dmat_ref, qdec_ref, kdec_ref, gdec_ref,
                bd_ref, gain_ref, o_ref, st_ref):
    j = pl.program_id(1)

    @pl.when(j == 0)
    def _reset():
        st_ref[...] = jnp.zeros(st_ref.shape, F32)

    cos = _lane_tile(cos_ref[...], RET_WIDTH // LANES)
    sin = _lane_tile(sin_ref[...], RET_WIDTH // LANES)
    q = _rotary(q_ref[...], cos, sin)
    k = _rotary(k_ref[...], cos, sin) * (RET_KEY_DIM ** -0.5)
    v_b = v_ref[...].astype(BF16)
    q_b = q.astype(BF16)
    k_b = k.astype(BF16)

    st = st_ref[...]
    o = jnp.dot((q * qdec_ref[...]).astype(BF16), st.astype(BF16), preferred_element_type=F32)
    for h in range(RET_HEADS):
        cm = _group_mask(q_b.shape, 1, RET_KEY_DIM, h)
        qh = jnp.where(cm, q_b, jnp.zeros_like(q_b))
        s = lax.dot_general(qh, k_b, (((1,), (1,)), ((), ())), preferred_element_type=F32) * dmat_ref[h]
        oh = jnp.dot(s.astype(BF16), v_b, preferred_element_type=F32)
        o = o + jnp.where(cm, oh, 0.0)

    upd = jnp.dot((k * kdec_ref[...]).T.astype(BF16), v_b, preferred_element_type=F32)
    st_ref[...] = st * gdec_ref[...] + upd * bd_ref[...]

    out = jnp.zeros(o.shape, F32)
    for h in range(RET_HEADS):
        cm = _group_mask(o.shape, 1, RET_KEY_DIM, h)
        ms = jnp.sum(jnp.where(cm, o * o, 0.0), axis=-1, keepdims=True) * (1.0 / RET_KEY_DIM)
        out = out + jnp.where(cm, o * lax.rsqrt(ms + EPS), 0.0)
    o_ref[...] = (_silu(g_ref[...]) * (out * gain_ref[...])).astype(BF16)


def _retention(pf, cos, sin, dmat, qdec, kdec, gdec, bd, gain, batch, seq):
    nb = seq // BLK
    row = lambda c: pl.BlockSpec((BLK, RET_WIDTH), lambda b, j: (b * nb + j, c))
    const2 = pl.BlockSpec((BLK, RET_WIDTH), lambda b, j: (0, 0))
    return pl.pallas_call(
        _ret_kernel,
        grid=(batch, nb),
        in_specs=[
            row(C_RQ // RET_WIDTH), row(C_RK // RET_WIDTH), row(C_RV // RET_WIDTH), row(C_RG // RET_WIDTH),
            pl.BlockSpec((BLK, LANES), lambda b, j: (j, 0)),
            pl.BlockSpec((BLK, LANES), lambda b, j: (j, 0)),
            pl.BlockSpec((RET_HEADS, BLK, BLK), lambda b, j: (0, 0, 0)),
            const2, const2, const2, const2,
            pl.BlockSpec((1, RET_WIDTH), lambda b, j: (0, 0)),
        ],
        out_specs=pl.BlockSpec((BLK, RET_WIDTH), lambda b, j: (b * nb + j, 0)),
        out_shape=jax.ShapeDtypeStruct((batch * seq, RET_WIDTH), BF16),
        scratch_shapes=[pltpu.VMEM((RET_WIDTH, RET_WIDTH), F32)],
        compiler_params=_cparams(("parallel", "arbitrary")),
        name="retention",
    )(pf, pf, pf, pf, cos, sin, dmat, qdec, kdec, gdec, bd, gain)


def _retention_tables(seq):
    f32 = np.float32
    inv = (f32(1.0) / (f32(ROPE_BASE) ** (np.arange(0, RET_KEY_DIM, 2, dtype=f32) / f32(RET_KEY_DIM)))).astype(f32)
    ang = (np.arange(seq, dtype=f32)[:, None] * inv[None, :]).astype(f32)
    cos_h = np.concatenate([np.cos(ang), np.cos(ang)], axis=-1)
    sin_h = np.concatenate([-np.sin(ang), np.sin(ang)], axis=-1)
    cos = np.tile(cos_h, (1, 2)).astype(f32)
    sin = np.tile(sin_h, (1, 2)).astype(f32)
    log_gamma = np.log1p(-np.power(f32(2.0), f32(-5.0) - np.arange(RET_HEADS, dtype=f32))).astype(f32)
    idx = np.arange(BLK, dtype=f32)
    ii = np.arange(BLK)
    same_or_earlier_chunk = (ii[None, :] // CHUNK) <= (ii[:, None] // CHUNK)
    dmat = np.exp(log_gamma[:, None, None] * np.abs(idx[:, None] - idx[None, :]))
    dmat = np.where(same_or_earlier_chunk[None], dmat, 0.0).astype(f32)
    lg_cols = np.repeat(log_gamma, RET_KEY_DIM)
    qdec = np.exp(lg_cols[None, :] * (idx + 1.0)[:, None]).astype(f32)
    kdec = np.exp(lg_cols[None, :] * (BLK - 1.0 - idx)[:, None]).astype(f32)
    head_of = np.arange(RET_WIDTH) // RET_KEY_DIM
    bd = (head_of[:, None] == head_of[None, :]).astype(f32)
    gdec = (np.exp(lg_cols * f32(BLK))[:, None] * bd).astype(f32)
    return tuple(jnp.asarray(a) for a in (cos, sin, dmat, qdec, kdec, gdec, bd))


def _outproj_kernel(x_ref, a_ref, b_ref, c_ref, w_ref, g_ref, *rest, with_router):
    if with_router:
        rw_ref, xo_ref, h_ref, rinfo_ref = rest
    else:
        xo_ref, h_ref = rest
    acc = x_ref[...]
    acc = acc + jnp.dot(a_ref[...], w_ref[0:DA_WIDTH, :], preferred_element_type=F32)
    acc = acc + jnp.dot(b_ref[...], w_ref[DA_WIDTH:DA_WIDTH + SSM_WIDTH, :], preferred_element_type=F32)
    acc = acc + jnp.dot(c_ref[...], w_ref[DA_WIDTH + SSM_WIDTH:, :], preferred_element_type=F32)
    xo_ref[...] = acc
    ms = jnp.mean(acc * acc, axis=-1, keepdims=True)
    hf = acc * lax.rsqrt(ms + EPS) * g_ref[...]
    h = hf.astype(BF16)
    h_ref[...] = hf.astype(h_ref.dtype)
    if with_router:
        logits = jnp.dot(h, rw_ref[...], preferred_element_type=F32)
        lane = lax.broadcasted_iota(jnp.int32, logits.shape, 1).astype(F32)
        logits = jnp.where(lane < N_EXPERTS, logits, NEG_BIG)
        v1 = jnp.max(logits, axis=-1, keepdims=True)
        i1 = jnp.min(jnp.where(logits == v1, lane, float(LANES)), axis=-1, keepdims=True)
        rest_l = jnp.where(lane == i1, NEG_BIG, logits)
        v2 = jnp.max(rest_l, axis=-1, keepdims=True)
        i2 = jnp.min(jnp.where(rest_l == v2, lane, float(LANES)), axis=-1, keepdims=True)
        e2 = jnp.exp(v2 - v1)
        g1 = 1.0 / (1.0 + e2)
        g2 = e2 * g1
        rinfo_ref[...] = (jnp.where(lane == 0.0, g1, 0.0) + jnp.where(lane == 1.0, g2, 0.0)
                          + jnp.where(lane == 2.0, i1, 0.0) + jnp.where(lane == 3.0, i2, 0.0))


def _outproj(x2, oa, ob, oc, w, gain, router_w=None):
    t = x2.shape[0]
    with_router = router_w is not None
    tok = lambda w_: pl.BlockSpec((TM_PROJ, w_), lambda i: (i, 0))
    in_specs = [tok(D_MODEL), tok(DA_WIDTH), tok(SSM_WIDTH), tok(RET_WIDTH),
                pl.BlockSpec((D_MODEL, D_MODEL), lambda i: (0, 0)),
                pl.BlockSpec((1, D_MODEL), lambda i: (0, 0))]
    out_specs = [tok(D_MODEL), tok(D_MODEL)]
    h_dtype = F32 if with_router else BF16
    out_shape = [jax.ShapeDtypeStruct((t, D_MODEL), F32), jax.ShapeDtypeStruct((t, D_MODEL), h_dtype)]
    args = [x2, oa, ob, oc, w, gain]
    if with_router:
        in_specs.append(pl.BlockSpec((D_MODEL, LANES), lambda i: (0, 0)))
        out_specs.append(tok(LANES))
        out_shape.append(jax.ShapeDtypeStruct((t, LANES), F32))
        args.append(router_w)
    return pl.pallas_call(
        functools.partial(_outproj_kernel, with_router=with_router),
        grid=(t // TM_PROJ,),
        in_specs=in_specs,
        out_specs=out_specs,
        out_shape=out_shape,
        compiler_params=_cparams(("parallel",)),
        name="outproj_router" if with_router else "outproj",
    )(*args)


def _ffn_kernel(x_ref, h_ref, wg_ref, wu_ref, wd_ref, o_ref, acc_ref):
    f = pl.program_id(1)

    @pl.when(f == 0)
    def _init():
        acc_ref[...] = x_ref[...]

    h = h_ref[...]
    gate = jnp.dot(h, wg_ref[0], preferred_element_type=F32)
    up = jnp.dot(h, wu_ref[0], preferred_element_type=F32)
    act = (_silu(gate) * up).astype(BF16)
    acc_ref[...] += jnp.dot(act, wd_ref[...], preferred_element_type=F32)

    @pl.when(f == pl.num_programs(1) - 1)
    def _done():
        o_ref[...] = acc_ref[...]


def _dense_ffn(x2, h, wg, wu, wd):
    t = x2.shape[0]
    nf = wg.shape[0]
    return pl.pallas_call(
        _ffn_kernel,
        grid=(t // TM_FFN, nf),
        in_specs=[
            pl.BlockSpec((TM_FFN, D_MODEL), lambda i, f: (i, 0)),
            pl.BlockSpec((TM_FFN, D_MODEL), lambda i, f: (i, 0)),
            pl.BlockSpec((1, D_MODEL, TF_FFN), lambda i, f: (f, 0, 0)),
            pl.BlockSpec((1, D_MODEL, TF_FFN), lambda i, f: (f, 0, 0)),
            pl.BlockSpec((TF_FFN, D_MODEL), lambda i, f: (f, 0)),
        ],
        out_specs=pl.BlockSpec((TM_FFN, D_MODEL), lambda i, f: (i, 0)),
        out_shape=jax.ShapeDtypeStruct((t, D_MODEL), F32),
        scratch_shapes=[pltpu.VMEM((TM_FFN, D_MODEL), F32)],
        compiler_params=_cparams(("parallel", "arbitrary")),
        name="dense_swiglu",
    )(x2, h, wg, wu, wd)


def _issue_row_gather(src_hbm, idx_ref, idx_base, dst, sem, n_rows):
    def body(r, carry):
        tok = idx_ref[idx_base + r]
        pltpu.make_async_copy(src_hbm.at[pl.ds(tok, 1)], dst.at[pl.ds(r, 1)], sem).start()
        return carry
    lax.fori_loop(0, n_rows, body, 0, unroll=8)


def _wait_row_gather(src_hbm, dst, sem, n_rows):
    pltpu.make_async_copy(src_hbm.at[pl.ds(0, n_rows)], dst, sem).wait()


def _moe_ffn_kernel(te_ref, nv_ref, tb_ref, rt_ref, h_hbm, wg_ref, wu_ref, wd_ref, y_ref, hbuf, xb_ref, acc_ref, sem):
    i = pl.program_id(0)
    f = pl.program_id(1)
    n_valid = nv_ref[0]
    valid = i < n_valid
    slot = i % 2

    @pl.when((f == 0) & (i == 0) & valid)
    def _first():
        _issue_row_gather(h_hbm, rt_ref, tb_ref[0], hbuf.at[0], sem.at[0], TR_MOE)

    @pl.when((f == 0) & valid)
    def _arrive():
        _wait_row_gather(h_hbm, hbuf.at[slot], sem.at[slot], TR_MOE)
        xb_ref[...] = hbuf[slot].astype(BF16)

    def compute(prefetch_next):
        if prefetch_next:
            rows_per_step = TR_MOE // (FFN_EXPERT // TF_MOE)
            r0 = f * rows_per_step
            src0 = tb_ref[i + 1] + r0
            for j in range(rows_per_step):
                tok = rt_ref[src0 + j]
                pltpu.make_async_copy(h_hbm.at[pl.ds(tok, 1)], hbuf.at[1 - slot, pl.ds(r0 + j, 1)],
                                      sem.at[1 - slot]).start()
        xb = xb_ref[...]
        gate = jnp.dot(xb, wg_ref[0], preferred_element_type=F32)
        up = jnp.dot(xb, wu_ref[0], preferred_element_type=F32)
        act = (_silu(gate) * up).astype(BF16)
        contrib = jnp.dot(act, wd_ref[0], preferred_element_type=F32)

        @pl.when(f == 0)
        def _():
            acc_ref[...] = contrib

        @pl.when(f > 0)
        def _():
            acc_ref[...] += contrib

    has_next = i + 1 < n_valid

    @pl.when(valid & has_next)
    def _compute_and_prefetch():
        compute(True)

    @pl.when(valid & jnp.logical_not(has_next))
    def _compute_last():
        compute(False)

    @pl.when(f == pl.num_programs(1) - 1)
    def _store():
        y_ref[...] = jnp.where(valid, acc_ref[...], 0.0)


def _moe_ffn(h, tile_expert, n_valid, tile_base, sorted_token, wg, wu, wd):
    n_tiles = tile_expert.shape[0]
    nf = wg.shape[2] // TF_MOE

    def w_in_map(i, f, te, nv, tb, rt):
        return (te[i], 0, jnp.where(i < nv[0], f, nf - 1))

    def w_out_map(i, f, te, nv, tb, rt):
        return (te[i], jnp.where(i < nv[0], f, nf - 1), 0)

    grid_spec = pltpu.PrefetchScalarGridSpec(
        num_scalar_prefetch=4,
        grid=(n_tiles, nf),
        in_specs=[
            pl.BlockSpec(memory_space=pl.ANY),
            pl.BlockSpec((1, D_MODEL, TF_MOE), w_in_map),
            pl.BlockSpec((1, D_MODEL, TF_MOE), w_in_map),
            pl.BlockSpec((1, TF_MOE, D_MODEL), w_out_map),
        ],
        out_specs=pl.BlockSpec((TR_MOE, D_MODEL), lambda i, f, te, nv, tb, rt: (i, 0)),
        scratch_shapes=[
            pltpu.VMEM((2, TR_MOE, D_MODEL), F32),
            pltpu.VMEM((TR_MOE, D_MODEL), BF16),
            pltpu.VMEM((TR_MOE, D_MODEL), F32),
            pltpu.SemaphoreType.DMA((2,)),
        ],
    )
    return pl.pallas_call(
        _moe_ffn_kernel,
        grid_spec=grid_spec,
        out_shape=jax.ShapeDtypeStruct((n_tiles * TR_MOE, D_MODEL), F32),
        compiler_params=_cparams(("arbitrary", "arbitrary")),
        name="moe_expert_ffn",
    )(tile_expert, n_valid, tile_base, sorted_token, h, wg, wu, wd)


def _moe_combine_kernel(tr_ref, x_ref, rinfo_ref, y_hbm, fg_ref, o_ref, ybuf, sem):
    i = pl.program_id(0)
    slot = i % 2
    n_rows = TOP_K * TM_COMB

    @pl.when(i == 0)
    def _first():
        _issue_row_gather(y_hbm, tr_ref, 0, ybuf.at[0], sem.at[0], n_rows)

    _wait_row_gather(y_hbm, ybuf.at[slot], sem.at[slot], n_rows)

    @pl.when(i + 1 < pl.num_programs(0))
    def _prefetch():
        src0 = (i + 1) * n_rows
        dst = ybuf.at[1 - slot]
        for r in range(n_rows):
            pltpu.make_async_copy(y_hbm.at[pl.ds(tr_ref[src0 + r], 1)], dst.at[pl.ds(r, 1)], sem.at[1 - slot]).start()

    rinfo = rinfo_ref[...]
    g1 = jnp.broadcast_to(rinfo[:, 0:1], (TM_COMB, D_MODEL))
    g2 = jnp.broadcast_to(rinfo[:, 1:2], (TM_COMB, D_MODEL))
    y = x_ref[...] + g1 * ybuf[slot, 0:TM_COMB, :] + g2 * ybuf[slot, TM_COMB:n_rows, :]
    ms = jnp.mean(y * y, axis=-1, keepdims=True)
    o_ref[...] = y * lax.rsqrt(ms + EPS) * fg_ref[...]


def _moe_combine(x2, rinfo, y_rows, tok_rows, final_gain):
    t = x2.shape[0]
    grid_spec = pltpu.PrefetchScalarGridSpec(
        num_scalar_prefetch=1,
        grid=(t // TM_COMB,),
        in_specs=[
            pl.BlockSpec((TM_COMB, D_MODEL), lambda i, tr: (i, 0)),
            pl.BlockSpec((TM_COMB, LANES), lambda i, tr: (i, 0)),
            pl.BlockSpec(memory_space=pl.ANY),
            pl.BlockSpec((1, D_MODEL), lambda i, tr: (0, 0)),
        ],
        out_specs=pl.BlockSpec((TM_COMB, D_MODEL), lambda i, tr: (i, 0)),
        scratch_shapes=[
            pltpu.VMEM((2, TOP_K * TM_COMB, D_MODEL), F32),
            pltpu.SemaphoreType.DMA((2,)),
        ],
    )
    return pl.pallas_call(
        _moe_combine_kernel,
        grid_spec=grid_spec,
        out_shape=jax.ShapeDtypeStruct((t, D_MODEL), F32),
        compiler_params=_cparams(("arbitrary",)),
        name="moe_combine_norm",
    )(tok_rows, x2, rinfo, y_rows, final_gain)


def _routing_tables(rinfo):
    t = rinfo.shape[0]
    n_assign = TOP_K * t
    n_tiles = -(-n_assign // TR_MOE) + N_EXPERTS
    flat_e = rinfo[:, 2:2 + TOP_K].astype(jnp.int32).reshape(-1)
    onehot = (flat_e[:, None] == jnp.arange(N_EXPERTS)[None, :]).astype(jnp.int32)
    csum = jnp.cumsum(onehot, axis=0)
    rank = jnp.sum((csum - onehot) * onehot, axis=1)
    counts = csum[-1]
    tiles_per_e = (counts + TR_MOE - 1) // TR_MOE
    tile_end = jnp.cumsum(tiles_per_e)
    first_tile = tile_end - tiles_per_e
    dest = (first_tile * TR_MOE)[flat_e] + rank
    order = jnp.sort(flat_e * n_assign + jnp.arange(n_assign, dtype=jnp.int32))
    sorted_token = jnp.concatenate([(order % n_assign) // TOP_K, jnp.zeros((TR_MOE,), jnp.int32)])
    start = jnp.cumsum(counts) - counts
    n_valid = tile_end[-1:].astype(jnp.int32)
    tile_ids = jnp.arange(n_tiles)
    tile_expert = jnp.searchsorted(tile_end, tile_ids, side="right").astype(jnp.int32)
    last_e = jnp.searchsorted(tile_end, n_valid[0] - 1, side="right").astype(jnp.int32)
    tile_expert = jnp.where(tile_ids < n_valid[0], tile_expert, last_e)
    tile_base = start[tile_expert] + (tile_ids - first_tile[tile_expert]) * TR_MOE
    tile_base = jnp.clip(tile_base, 0, n_assign).astype(jnp.int32)
    tok_rows = dest.reshape(t // TM_COMB, TM_COMB, TOP_K).transpose(0, 2, 1).reshape(-1).astype(jnp.int32)
    return tile_expert, n_valid, tile_base, sorted_token, tok_rows


def _permute_w_in(w):
    sizes = (256, 256, 256, 512, 768, 8, 256, 256, 256, 256)
    offs = np.concatenate([[0], np.cumsum(sizes)])
    part = lambda n: w[:, offs[n]:offs[n + 1]]
    dt_pad = jnp.zeros((w.shape[0], LANES - SSM_HEADS), w.dtype)
    cols = [part(0) * (DA_HEAD_DIM ** -0.5 * LOG2E), part(2), part(1), part(3), part(4),
            part(6), part(7), part(8), part(9), part(5), dt_pad]
    return jnp.concatenate(cols, axis=1).astype(BF16)


def _column_tiles(w, tile):
    *lead, k, n = w.shape
    return jnp.moveaxis(w.reshape(*lead, k, n // tile, tile), -2, -3)


def _pad_lanes(v, fill=0.0):
    return jnp.concatenate([v, jnp.full((LANES - v.shape[0],), fill, v.dtype)])[None, :]


def kernel(x, w_in, w_out, attn_norm, ffn_norm, final_norm, rel_bias, lambda_q1, lambda_k1, lambda_q2,
           lambda_k2, da_head_norm, conv_w, conv_b, dt_bias, a_log, d_skip, ssm_norm, ret_head_norm,
           w_gate, w_up, w_down, router_w, e_gate, e_up, e_down):
    batch, seq, _ = x.shape
    x2 = x.reshape(batch * seq, D_MODEL)
    bias_tiles = _bias_tiles(rel_bias)
    ret_tabs = _retention_tables(seq)

    for layer in range(DEPTH):
        lam_init = 0.8 - 0.6 * math.exp(-0.3 * layer)
        lam = (jnp.exp(jnp.sum(lambda_q1[layer] * lambda_k1[layer]))
               - jnp.exp(jnp.sum(lambda_q2[layer] * lambda_k2[layer])) + lam_init).astype(F32)
        lam = jnp.full((1, ATT_T), lam, F32)

        qkv, qvt, pf = _inproj(x2, attn_norm[layer][None, :], _permute_w_in(w_in[layer]))

        da_gain = (jnp.tile(da_head_norm[layer], DA_HEADS) * (1.0 - lam_init))[:, None]
        out_a = _attention(qkv, qvt, bias_tiles, da_gain, lam, batch, seq)

        cw, cb = conv_w[layer], conv_b[layer]
        out_b = _ssd(pf, cw[:, :SSM_WIDTH], cw[:, SSM_WIDTH:], cb[None, :SSM_WIDTH], cb[None, SSM_WIDTH:],
                     _pad_lanes(dt_bias[layer]), _pad_lanes(a_log[layer]),
                     jnp.repeat(d_skip[layer], SSM_HEAD_DIM)[None, :], ssm_norm[layer][None, :], batch, seq)

        ret_gain = jnp.tile(ret_head_norm[layer], RET_HEADS)[None, :]
        out_c = _retention(pf, *ret_tabs, ret_gain, batch, seq)

        w_o = w_out[layer].astype(BF16)
        if layer % 2 == 0:
            i = layer // 2
            x2, h = _outproj(x2, out_a, out_b, out_c, w_o, ffn_norm[layer][None, :])
            pad = FFN_PAD - FFN_DENSE
            wg = _column_tiles(jnp.pad(w_gate[i], ((0, 0), (0, pad))).astype(BF16), TF_FFN)
            wu = _column_tiles(jnp.pad(w_up[i], ((0, 0), (0, pad))).astype(BF16), TF_FFN)
            wd = jnp.pad(w_down[i], ((0, pad), (0, 0))).astype(BF16)
            x2 = _dense_ffn(x2, h, wg, wu, wd)
        else:
            i = layer // 2
            rw = jnp.pad(router_w[i], ((0, 0), (0, LANES - N_EXPERTS))).astype(BF16)
            x2, h, rinfo = _outproj(x2, out_a, out_b, out_c, w_o, ffn_norm[layer][None, :], rw)
            tile_expert, n_valid, tile_base, sorted_token, tok_rows = _routing_tables(rinfo)
            y_rows = _moe_ffn(h, tile_expert, n_valid, tile_base, sorted_token,
                              e_gate[i].astype(BF16), e_up[i].astype(BF16), e_down[i].astype(BF16))
            x2 = _moe_combine(x2, rinfo, y_rows, tok_rows, final_norm[None, :])
    return x2.reshape(batch, seq, D_MODEL)
```

```python
import functools
import math

import jax
import jax.numpy as jnp
import numpy as np
from jax import lax
from jax.experimental import pallas as pl
from jax.experimental.pallas import tpu as pltpu

F32 = jnp.float32
BF16 = jnp.bfloat16

D_MODEL = 1024
DEPTH = 2
CHUNK = 64
EPS = 1e-6
DA_HEADS = 4
DA_HEAD_DIM = 32
DA_V_DIM = 64
DA_WIDTH = 256
SSM_HEADS = 8
SSM_HEAD_DIM = 64
SSM_WIDTH = 512
SSM_STATE = 64
SSM_GROUPS = 2
SSM_CONV = 4
RET_HEADS = 4
RET_KEY_DIM = 64
RET_WIDTH = 256
ROPE_BASE = 10000.0
REL_BUCKETS = 32
REL_MAX_DIST = 128
FFN_DENSE = 2752
N_EXPERTS = 8
FFN_EXPERT = 3584

LANES = 128
VMEM_LIMIT = 48 * 1024 * 1024
NEG_BIG = -1e30

TM_PROJ = 512
ATT_T = 512
LOG2E = math.log2(math.e)
BLK = 256
FFN_PAD = 2816
TM_FFN = 512
TF_FFN = 1408
TF_MOE = 512
TR_MOE = 896
TM_COMB = 256
TOP_K = 2

C_Z, C_XS, C_BC, C_RQ, C_RK, C_RV, C_RG, C_DT = 0, 512, 1024, 1280, 1536, 1792, 2048, 2304
PF_W = 2432
PF_CHUNK = 640
QKV_W = 768


def _cparams(sem):
    return pltpu.CompilerParams(dimension_semantics=sem, vmem_limit_bytes=VMEM_LIMIT)


def _group_mask(shape, axis, group, idx):
    io = lax.broadcasted_iota(jnp.int32, shape, axis)
    return (io >= idx * group) & (io < (idx + 1) * group)


def _lane_tile(x, n, axis=1):
    return jnp.concatenate([x] * n, axis=axis)


def _silu(x):
    h = 0.5 * x
    return h + h * jnp.tanh(h)


def _inproj_kernel(x_ref, g_ref, w_ref, qkv_ref, qvt_ref, pf_ref):
    x = x_ref[...]
    ms = jnp.mean(x * x, axis=-1, keepdims=True)
    h = (x * lax.rsqrt(ms + EPS) * g_ref[...]).astype(BF16)
    qkv = jnp.dot(h, w_ref[:, :QKV_W], preferred_element_type=F32)
    qkv_ref[...] = qkv.astype(BF16)
    qvt_ref[...] = qkv[:, :2 * DA_WIDTH].T.astype(BF16)
    for lo in range(0, PF_W, PF_CHUNK):
        hi = min(lo + PF_CHUNK, PF_W)
        pf_ref[:, lo:hi] = jnp.dot(h, w_ref[:, QKV_W + lo:QKV_W + hi], preferred_element_type=F32)


def _inproj(x2, gain, w):
    t = x2.shape[0]
    return pl.pallas_call(
        _inproj_kernel,
        grid=(t // TM_PROJ,),
        in_specs=[
            pl.BlockSpec((TM_PROJ, D_MODEL), lambda i: (i, 0)),
            pl.BlockSpec((1, D_MODEL), lambda i: (0, 0)),
            pl.BlockSpec((D_MODEL, QKV_W + PF_W), lambda i: (0, 0)),
        ],
        out_specs=[
            pl.BlockSpec((TM_PROJ, QKV_W), lambda i: (i, 0)),
            pl.BlockSpec((2 * DA_WIDTH, TM_PROJ), lambda i: (0, i)),
            pl.BlockSpec((TM_PROJ, PF_W), lambda i: (i, 0)),
        ],
        out_shape=[
            jax.ShapeDtypeStruct((t, QKV_W), BF16),
            jax.ShapeDtypeStruct((2 * DA_WIDTH, t), BF16),
            jax.ShapeDtypeStruct((t, PF_W), F32),
        ],
        compiler_params=_cparams(("parallel",)),
        name="inproj",
    )(x2, gain, w)


def _attn_kernel(qi_ref, kj_ref, qt_ref, k_ref, vt_ref, bias_ref, gain_ref, lam_ref, o_ref,
                 qm_ref, m_ref, l_ref, acc_ref, s_ref, e_ref, al_ref):
    p = pl.program_id(1)
    qi = qi_ref[p]
    kj = kj_ref[p]
    n_ht = 2 * DA_HEADS

    @pl.when(kj == 0)
    def _init():
        qt = qt_ref[...]
        for ht in range(n_ht):
            qm_ref[ht] = jnp.where(_group_mask(qt.shape, 0, DA_HEAD_DIM, ht), qt, jnp.zeros_like(qt))
        m_ref[...] = jnp.full(m_ref.shape, NEG_BIG, F32)
        l_ref[...] = jnp.zeros(l_ref.shape, F32)
        acc_ref[...] = jnp.zeros(acc_ref.shape, F32)

    def step(with_bias):
        k = k_ref[...]
        def scores(a, buf):
            s_ref[buf] = jnp.dot(k, qm_ref[a], preferred_element_type=F32)

        def softmax(b, buf):
            for c0 in range(0, ATT_T, LANES):
                cols = slice(c0, c0 + LANES)

                def block():
                    blk = s_ref[buf, :, cols]
                    return blk + bias_ref[0, b // 2, :, cols] if with_bias else blk

                m_prev = m_ref[b, :, cols]
                m_new = jnp.maximum(m_prev, jnp.max(block(), axis=0, keepdims=True))
                alpha = jnp.exp2(m_prev - m_new)
                e = jnp.exp2(block() - m_new)
                l_ref[b, :, cols] = alpha * l_ref[b, :, cols] + jnp.sum(e, axis=0, keepdims=True)
                m_ref[b, :, cols] = m_new
                al_ref[buf, :, cols] = alpha
                e_ref[buf, :, cols] = e.astype(BF16)

        def values(c, buf):
            r0 = (c // 4) * LANES
            if not isinstance(r0, int):
                r0 = pl.multiple_of(r0, LANES)
            pv = jnp.dot(vt_ref[pl.ds(r0, LANES), :], e_ref[buf], preferred_element_type=F32)
            acc_ref[c] = acc_ref[c] * al_ref[buf] + pv

        scores(0, 0)
        scores(1, 1)
        softmax(0, 0)

        def trip(u, carry):
            t = 2 * u + 1
            scores(t + 1, 0)
            softmax(t, 1)
            values(t - 1, 0)
            scores(t + 2, 1)
            softmax(t + 1, 0)
            values(t, 1)
            return carry

        for u in range((n_ht - 2) // 2):
            trip(u, 0)
        softmax(n_ht - 1, 1)
        values(n_ht - 2, 0)
        values(n_ht - 1, 1)

    @pl.when(kj >= qi - 1)
    def _near():
        step(True)

    @pl.when(kj < qi - 1)
    def _far():
        step(False)

    @pl.when(kj == qi)
    def _finish():
        lam = lam_ref[...]
        row_lo = lax.broadcasted_iota(jnp.int32, (LANES, ATT_T), 0) < DA_V_DIM
        for pair in range(DA_HEADS // 2):
            halves = []
            for sub in range(2):
                h = 2 * pair + sub
                o = acc_ref[2 * h] * (1.0 / l_ref[2 * h]) - lam * (acc_ref[2 * h + 1] * (1.0 / l_ref[2 * h + 1]))
                own = row_lo if sub == 0 else jnp.logical_not(row_lo)
                ms = jnp.sum(jnp.where(own, o * o, 0.0), axis=0, keepdims=True) * (1.0 / DA_V_DIM)
                halves.append(o * lax.rsqrt(ms + EPS))
            r0 = pair * LANES
            blk = jnp.where(row_lo, halves[0], halves[1]) * gain_ref[r0:r0 + LANES, :]
            o_ref[:, r0:r0 + LANES] = blk.T.astype(BF16)


def _attention(qkv, qvt, bias_tiles, gain, lam, batch, seq):
    nq = seq // ATT_T
    qi_l, kj_l = [], []
    for qi in range(nq):
        for kj in range(qi + 1):
            qi_l.append(qi)
            kj_l.append(kj)
    qi_tbl = jnp.asarray(np.array(qi_l, np.int32))
    kj_tbl = jnp.asarray(np.array(kj_l, np.int32))
    n_pairs = len(qi_l)
    n_ht = 2 * DA_HEADS

    grid_spec = pltpu.PrefetchScalarGridSpec(
        num_scalar_prefetch=2,
        grid=(batch, n_pairs),
        in_specs=[
            pl.BlockSpec((DA_WIDTH, ATT_T), lambda b, p, qi, kj: (0, b * nq + qi[p])),
            pl.BlockSpec((ATT_T, DA_WIDTH), lambda b, p, qi, kj: (b * nq + kj[p], 2)),
            pl.BlockSpec((DA_WIDTH, ATT_T), lambda b, p, qi, kj: (1, b * nq + kj[p])),
            pl.BlockSpec((1, DA_HEADS, ATT_T, ATT_T),
                         lambda b, p, qi, kj: (jnp.minimum(qi[p] - kj[p], 1), 0, 0, 0)),
            pl.BlockSpec((DA_WIDTH, 1), lambda b, p, qi, kj: (0, 0)),
            pl.BlockSpec((1, ATT_T), lambda b, p, qi, kj: (0, 0)),
        ],
        out_specs=pl.BlockSpec((ATT_T, DA_WIDTH), lambda b, p, qi, kj: (b * nq + qi[p], 0)),
        scratch_shapes=[
            pltpu.VMEM((n_ht, DA_WIDTH, ATT_T), BF16),
            pltpu.VMEM((n_ht, 1, ATT_T), F32),
            pltpu.VMEM((n_ht, 1, ATT_T), F32),
            pltpu.VMEM((n_ht, LANES, ATT_T), F32),
            pltpu.VMEM((2, ATT_T, ATT_T), F32),
            pltpu.VMEM((2, ATT_T, ATT_T), BF16),
            pltpu.VMEM((2, 1, ATT_T), F32),
        ],
    )
    return pl.pallas_call(
        _attn_kernel,
        grid_spec=grid_spec,
        out_shape=jax.ShapeDtypeStruct((batch * seq, DA_WIDTH), BF16),
        compiler_params=_cparams(("parallel", "arbitrary")),
        name="diff_attention",
    )(qi_tbl, kj_tbl, qvt, qkv, qvt, bias_tiles, gain, lam)


def _t5_bucket(rel):
    half = REL_BUCKETS // 2
    max_exact = half // 2
    ret = jnp.where(rel > 0, half, 0)
    n = jnp.abs(rel)
    nf = jnp.maximum(n, 1).astype(F32)
    large = max_exact + (jnp.log(nf / max_exact) / math.log(REL_MAX_DIST / max_exact)
                         * (half - max_exact)).astype(jnp.int32)
    large = jnp.minimum(large, half - 1)
    return ret + jnp.where(n < max_exact, n, large)


def _bias_tiles(rel_bias):
    n = ATT_T
    i = jnp.arange(n)[None, :]
    j = jnp.arange(n)[:, None]
    far = rel_bias[_t5_bucket(jnp.int32(-2 * n))].astype(F32)
    tiles = []
    for off in range(2):
        rel = (n - 1 - off * n) - jnp.arange(2 * n)
        bucket = _t5_bucket(rel)
        g = jnp.zeros((DA_HEADS, 2 * n), F32)
        for c in range(REL_BUCKETS):
            g = jnp.where((bucket == c)[None], rel_bias[c].astype(F32)[:, None], g)
        g = (g - far[:, None]) * LOG2E
        skew = jnp.broadcast_to(g[:, None, :], (DA_HEADS, n, 2 * n)).reshape(DA_HEADS, 2 * n * n)
        skew = skew[:, :n * (2 * n - 1)].reshape(DA_HEADS, n, 2 * n - 1)
        b = skew[:, :, n - 1:]
        if off == 0:
            allowed = (j // CHUNK) <= (i // CHUNK)
            b = jnp.where(allowed[None], b, NEG_BIG)
        tiles.append(b)
    return jnp.stack(tiles)


def _cumsum_rows(x):
    n = x.shape[0]
    row = lax.broadcasted_iota(jnp.int32, x.shape, 0)
    sh = 1
    while sh < n:
        x = x + jnp.where(row >= sh, pltpu.roll(x, sh, axis=0), 0.0)
        sh *= 2
    return x


def _expand_heads(v, width, group):
    out = jnp.zeros((v.shape[0], width), F32)
    for h in range(width // group):
        col = jnp.broadcast_to(v[:, h:h + 1], (v.shape[0], width))
        out = jnp.where(_group_mask(out.shape, 1, group, h), col, out)
    return out


def _ssd_kernel(z_ref, xs_ref, bc_ref, dt_ref, cwx_ref, cwb_ref, cbx_ref, cbb_ref, dtb_ref, alog_ref,
                dskip_ref, ng_ref, o_ref, extx_ref, extb_ref, st_ref):
    j = pl.program_id(1)

    @pl.when(j == 0)
    def _reset():
        extx_ref[0:8, :] = jnp.zeros((8, SSM_WIDTH), F32)
        extb_ref[0:8, :] = jnp.zeros((8, 2 * LANES), F32)
        st_ref[...] = jnp.zeros(st_ref.shape, F32)

    extx_ref[8:8 + BLK, :] = xs_ref[...]
    extb_ref[8:8 + BLK, :] = bc_ref[...]
    yx = jnp.broadcast_to(cbx_ref[...], (BLK, SSM_WIDTH))
    yb = jnp.broadcast_to(cbb_ref[...], (BLK, 2 * LANES))
    for w in range(SSM_CONV):
        lo = 8 - (SSM_CONV - 1) + w
        yx = yx + extx_ref[lo:lo + BLK, :] * cwx_ref[w:w + 1, :]
        yb = yb + extb_ref[lo:lo + BLK, :] * cwb_ref[w:w + 1, :]
    tail_x = extx_ref[BLK:BLK + 8, :]
    tail_b = extb_ref[BLK:BLK + 8, :]
    extx_ref[0:8, :] = tail_x
    extb_ref[0:8, :] = tail_b
    xs = _silu(yx)
    bc = _silu(yb)
    bm = bc[:, :LANES]
    cm = bc[:, LANES:]

    dt = jax.nn.softplus(dt_ref[...] + dtb_ref[...])
    a = -jnp.exp(alog_ref[...]) * dt
    a_cs = _cumsum_rows(a)
    a_cs_t = a_cs.T
    a_last = a_cs[BLK - 1:BLK, :]

    dt_full = _expand_heads(dt, SSM_WIDTH, SSM_HEAD_DIM)
    ea_full = _expand_heads(jnp.exp(a_cs), SSM_WIDTH, SSM_HEAD_DIM)
    dec_full = _expand_heads(jnp.exp(a_last - a_cs), SSM_WIDTH, SSM_HEAD_DIM)
    cdec_full = _expand_heads(jnp.exp(a_last), SSM_WIDTH, SSM_HEAD_DIM)

    xdt = xs * dt_full
    xdt_b = xdt.astype(BF16)
    cm_b = cm.astype(BF16)
    bm_b = bm.astype(BF16)

    st = st_ref[...]
    y = jnp.dot(cm_b, st.astype(BF16), preferred_element_type=F32) * ea_full + xs * dskip_ref[...]

    row = lax.broadcasted_iota(jnp.int32, (BLK, BLK), 0)
    colio = lax.broadcasted_iota(jnp.int32, (BLK, BLK), 1)
    causal = colio <= row
    rep = SSM_HEADS // SSM_GROUPS
    lane_lo = lax.broadcasted_iota(jnp.int32, (BLK, LANES), 1) < SSM_HEAD_DIM
    pieces = []
    for g in range(SSM_GROUPS):
        cg = jnp.where(_group_mask(cm_b.shape, 1, SSM_STATE, g), cm_b, jnp.zeros_like(cm_b))
        cb = lax.dot_general(cg, bm_b, (((1,), (1,)), ((), ())), preferred_element_type=F32)
        for pair in range(rep // 2):
            halves = []
            for sub in range(2):
                h = g * rep + pair * 2 + sub
                seg = jnp.broadcast_to(a_cs[:, h:h + 1], (BLK, BLK)) - a_cs_t[h:h + 1, :]
                lmat = jnp.exp(jnp.where(causal, seg, NEG_BIG))
                mh = (cb * lmat).astype(BF16)
                c0 = (h // 2) * LANES
                halves.append(jnp.dot(mh, xdt_b[:, c0:c0 + LANES], preferred_element_type=F32))
            pieces.append(jnp.where(lane_lo, halves[0], halves[1]))
    y = y + jnp.concatenate(pieces, axis=1)

    upd = jnp.dot(bm.T.astype(BF16), (xdt * dec_full).astype(BF16), preferred_element_type=F32)
    r_io = lax.broadcasted_iota(jnp.int32, upd.shape, 0) // SSM_STATE
    c_io = lax.broadcasted_iota(jnp.int32, upd.shape, 1) // (SSM_HEAD_DIM * rep)
    st_ref[...] = st * cdec_full + jnp.where(r_io == c_io, upd, 0.0)

    y = y * _silu(z_ref[...])
    gw = SSM_WIDTH // SSM_GROUPS
    for g in range(SSM_GROUPS):
        yg = y[:, g * gw:(g + 1) * gw]
        ms = jnp.mean(yg * yg, axis=-1, keepdims=True)
        o_ref[:, g * gw:(g + 1) * gw] = (yg * lax.rsqrt(ms + EPS) * ng_ref[:, g * gw:(g + 1) * gw]).astype(BF16)


def _ssd(pf, cwx, cwb, cbx, cbb, dtb, alog, dskip, ng, batch, seq):
    nb = seq // BLK
    row = lambda w, c: pl.BlockSpec((BLK, w), lambda b, j: (b * nb + j, c))
    const = lambda r, w: pl.BlockSpec((r, w), lambda b, j: (0, 0))
    return pl.pallas_call(
        _ssd_kernel,
        grid=(batch, nb),
        in_specs=[
            row(SSM_WIDTH, C_Z // SSM_WIDTH), row(SSM_WIDTH, C_XS // SSM_WIDTH), row(2 * LANES, C_BC // (2 * LANES)),
            row(LANES, C_DT // LANES),
            const(SSM_CONV, SSM_WIDTH), const(SSM_CONV, 2 * LANES), const(1, SSM_WIDTH), const(1, 2 * LANES),
            const(1, LANES), const(1, LANES), const(1, SSM_WIDTH), const(1, SSM_WIDTH),
        ],
        out_specs=pl.BlockSpec((BLK, SSM_WIDTH), lambda b, j: (b * nb + j, 0)),
        out_shape=jax.ShapeDtypeStruct((batch * seq, SSM_WIDTH), BF16),
        scratch_shapes=[
            pltpu.VMEM((BLK + 8, SSM_WIDTH), F32),
            pltpu.VMEM((BLK + 8, 2 * LANES), F32),
            pltpu.VMEM((SSM_GROUPS * SSM_STATE, SSM_WIDTH), F32),
        ],
        compiler_params=_cparams(("parallel", "arbitrary")),
        name="ssd_mixer",
    )(pf, pf, pf, pf, cwx, cwb, cbx, cbb, dtb, alog, dskip, ng)


def _rotary(u, cos, sin_signed):
    first = (lax.broadcasted_iota(jnp.int32, u.shape, 1) % RET_KEY_DIM) < (RET_KEY_DIM // 2)
    half = RET_KEY_DIM // 2
    swapped = jnp.where(first, pltpu.roll(u, u.shape[1] - half, axis=1), pltpu.roll(u, half, axis=1))
    return u * cos + swapped * sin_signed


def _ret_kernel(q_ref, k_ref, v_ref, g_ref, cos_ref, sin_ref, dmat_ref, qdec_ref, kdec_ref, gdec_ref,
                bd_ref, gain_ref, o_ref, st_ref):
    j = pl.program_id(1)

    @pl.when(j == 0)
    def _reset():
        st_ref[...] = jnp.zeros(st_ref.shape, F32)

    cos = _lane_tile(cos_ref[...], RET_WIDTH // LANES)
    sin = _lane_tile(sin_ref[...], RET_WIDTH // LANES)
    q = _rotary(q_ref[...], cos, sin)
    k = _rotary(k_ref[...], cos, sin) * (RET_KEY_DIM ** -0.5)
    v_b = v_ref[...].astype(BF16)
    q_b = q.astype(BF16)
    k_b = k.astype(BF16)

    st = st_ref[...]
    o = jnp.dot((q * qdec_ref[...]).astype(BF16), st.astype(BF16), preferred_element_type=F32)
    for h in range(RET_HEADS):
        cm = _group_mask(q_b.shape, 1, RET_KEY_DIM, h)
        qh = jnp.where(cm, q_b, jnp.zeros_like(q_b))
        s = lax.dot_general(qh, k_b, (((1,), (1,)), ((), ())), preferred_element_type=F32) * dmat_ref[h]
        oh = jnp.dot(s.astype(BF16), v_b, preferred_element_type=F32)
        o = o + jnp.where(cm, oh, 0.0)

    upd = jnp.dot((k * kdec_ref[...]).T.astype(BF16), v_b, preferred_element_type=F32)
    st_ref[...] = st * gdec_ref[...] + upd * bd_ref[...]

    out = jnp.zeros(o.shape, F32)
    for h in range(RET_HEADS):
        cm = _group_mask(o.shape, 1, RET_KEY_DIM, h)
        ms = jnp.sum(jnp.where(cm, o * o, 0.0), axis=-1, keepdims=True) * (1.0 / RET_KEY_DIM)
        out = out + jnp.where(cm, o * lax.rsqrt(ms + EPS), 0.0)
    o_ref[...] = (_silu(g_ref[...]) * (out * gain_ref[...])).astype(BF16)


def _retention(pf, cos, sin, dmat, qdec, kdec, gdec, bd, gain, batch, seq):
    nb = seq // BLK
    row = lambda c: pl.BlockSpec((BLK, RET_WIDTH), lambda b, j: (b * nb + j, c))
    const2 = pl.BlockSpec((BLK, RET_WIDTH), lambda b, j: (0, 0))
    return pl.pallas_call(
        _ret_kernel,
        grid=(batch, nb),
        in_specs=[
            row(C_RQ // RET_WIDTH), row(C_RK // RET_WIDTH), row(C_RV // RET_WIDTH), row(C_RG // RET_WIDTH),
            pl.BlockSpec((BLK, LANES), lambda b, j: (j, 0)),
            pl.BlockSpec((BLK, LANES), lambda b, j: (j, 0)),
            pl.BlockSpec((RET_HEADS, BLK, BLK), lambda b, j: (0, 0, 0)),
            const2, const2, const2, const2,
            pl.BlockSpec((1, RET_WIDTH), lambda b, j: (0, 0)),
        ],
        out_specs=pl.BlockSpec((BLK, RET_WIDTH), lambda b, j: (b * nb + j, 0)),
        out_shape=jax.ShapeDtypeStruct((batch * seq, RET_WIDTH), BF16),
        scratch_shapes=[pltpu.VMEM((RET_WIDTH, RET_WIDTH), F32)],
        compiler_params=_cparams(("parallel", "arbitrary")),
        name="retention",
    )(pf, pf, pf, pf, cos, sin, dmat, qdec, kdec, gdec, bd, gain)


def _retention_tables(seq):
    f32 = np.float32
    inv = (f32(1.0) / (f32(ROPE_BASE) ** (np.arange(0, RET_KEY_DIM, 2, dtype=f32) / f32(RET_KEY_DIM)))).astype(f32)
    ang = (np.arange(seq, dtype=f32)[:, None] * inv[None, :]).astype(f32)
    cos_h = np.concatenate([np.cos(ang), np.cos(ang)], axis=-1)
    sin_h = np.concatenate([-np.sin(ang), np.sin(ang)], axis=-1)
    cos = np.tile(cos_h, (1, 2)).astype(f32)
    sin = np.tile(sin_h, (1, 2)).astype(f32)
    log_gamma = np.log1p(-np.power(f32(2.0), f32(-5.0) - np.arange(RET_HEADS, dtype=f32))).astype(f32)
    idx = np.arange(BLK, dtype=f32)
    ii = np.arange(BLK)
    same_or_earlier_chunk = (ii[None, :] // CHUNK) <= (ii[:, None] // CHUNK)
    dmat = np.exp(log_gamma[:, None, None] * np.abs(idx[:, None] - idx[None, :]))
    dmat = np.where(same_or_earlier_chunk[None], dmat, 0.0).astype(f32)
    lg_cols = np.repeat(log_gamma, RET_KEY_DIM)
    qdec = np.exp(lg_cols[None, :] * (idx + 1.0)[:, None]).astype(f32)
    kdec = np.exp(lg_cols[None, :] * (BLK - 1.0 - idx)[:, None]).astype(f32)
    head_of = np.arange(RET_WIDTH) // RET_KEY_DIM
    bd = (head_of[:, None] == head_of[None, :]).astype(f32)
    gdec = (np.exp(lg_cols * f32(BLK))[:, None] * bd).astype(f32)
    return tuple(jnp.asarray(a) for a in (cos, sin, dmat, qdec, kdec, gdec, bd))


def _outproj_kernel(x_ref, a_ref, b_ref, c_ref, w_ref, g_ref, *rest, with_router):
    if with_router:
        rw_ref, xo_ref, h_ref, rinfo_ref = rest
    else:
        xo_ref, h_ref = rest
    acc = x_ref[...]
    acc = acc + jnp.dot(a_ref[...], w_ref[0:DA_WIDTH, :], preferred_element_type=F32)
    acc = acc + jnp.dot(b_ref[...], w_ref[DA_WIDTH:DA_WIDTH + SSM_WIDTH, :], preferred_element_type=F32)
    acc = acc + jnp.dot(c_ref[...], w_ref[DA_WIDTH + SSM_WIDTH:, :], preferred_element_type=F32)
    xo_ref[...] = acc
    ms = jnp.mean(acc * acc, axis=-1, keepdims=True)
    hf = acc * lax.rsqrt(ms + EPS) * g_ref[...]
    h = hf.astype(BF16)
    h_ref[...] = hf.astype(h_ref.dtype)
    if with_router:
        logits = jnp.dot(h, rw_ref[...], preferred_element_type=F32)
        lane = lax.broadcasted_iota(jnp.int32, logits.shape, 1).astype(F32)
        logits = jnp.where(lane < N_EXPERTS, logits, NEG_BIG)
        v1 = jnp.max(logits, axis=-1, keepdims=True)
        i1 = jnp.min(jnp.where(logits == v1, lane, float(LANES)), axis=-1, keepdims=True)
        rest_l = jnp.where(lane == i1, NEG_BIG, logits)
        v2 = jnp.max(rest_l, axis=-1, keepdims=True)
        i2 = jnp.min(jnp.where(rest_l == v2, lane, float(LANES)), axis=-1, keepdims=True)
        e2 = jnp.exp(v2 - v1)
        g1 = 1.0 / (1.0 + e2)
        g2 = e2 * g1
        rinfo_ref[...] = (jnp.where(lane == 0.0, g1, 0.0) + jnp.where(lane == 1.0, g2, 0.0)
                          + jnp.where(lane == 2.0, i1, 0.0) + jnp.where(lane == 3.0, i2, 0.0))


def _outproj(x2, oa, ob, oc, w, gain, router_w=None):
    t = x2.shape[0]
    with_router = router_w is not None
    tok = lambda w_: pl.BlockSpec((TM_PROJ, w_), lambda i: (i, 0))
    in_specs = [tok(D_MODEL), tok(DA_WIDTH), tok(SSM_WIDTH), tok(RET_WIDTH),
                pl.BlockSpec((D_MODEL, D_MODEL), lambda i: (0, 0)),
                pl.BlockSpec((1, D_MODEL), lambda i: (0, 0))]
    out_specs = [tok(D_MODEL), tok(D_MODEL)]
    h_dtype = F32 if with_router else BF16
    out_shape = [jax.ShapeDtypeStruct((t, D_MODEL), F32), jax.ShapeDtypeStruct((t, D_MODEL), h_dtype)]
    args = [x2, oa, ob, oc, w, gain]
    if with_router:
        in_specs.append(pl.BlockSpec((D_MODEL, LANES), lambda i: (0, 0)))
        out_specs.append(tok(LANES))
        out_shape.append(jax.ShapeDtypeStruct((t, LANES), F32))
        args.append(router_w)
    return pl.pallas_call(
        functools.partial(_outproj_kernel, with_router=with_router),
        grid=(t // TM_PROJ,),
        in_specs=in_specs,
        out_specs=out_specs,
        out_shape=out_shape,
        compiler_params=_cparams(("parallel",)),
        name="outproj_router" if with_router else "outproj",
    )(*args)


def _ffn_kernel(x_ref, h_ref, wg_ref, wu_ref, wd_ref, o_ref, acc_ref):
    f = pl.program_id(1)

    @pl.when(f == 0)
    def _init():
        acc_ref[...] = x_ref[...]

    h = h_ref[...]
    gate = jnp.dot(h, wg_ref[0], preferred_element_type=F32)
    up = jnp.dot(h, wu_ref[0], preferred_element_type=F32)
    act = (_silu(gate) * up).astype(BF16)
    acc_ref[...] += jnp.dot(act, wd_ref[...], preferred_element_type=F32)

    @pl.when(f == pl.num_programs(1) - 1)
    def _done():
        o_ref[...] = acc_ref[...]


def _dense_ffn(x2, h, wg, wu, wd):
    t = x2.shape[0]
    nf = wg.shape[0]
    return pl.pallas_call(
        _ffn_kernel,
        grid=(t // TM_FFN, nf),
        in_specs=[
            pl.BlockSpec((TM_FFN, D_MODEL), lambda i, f: (i, 0)),
            pl.BlockSpec((TM_FFN, D_MODEL), lambda i, f: (i, 0)),
            pl.BlockSpec((1, D_MODEL, TF_FFN), lambda i, f: (f, 0, 0)),
            pl.BlockSpec((1, D_MODEL, TF_FFN), lambda i, f: (f, 0, 0)),
            pl.BlockSpec((TF_FFN, D_MODEL), lambda i, f: (f, 0)),
        ],
        out_specs=pl.BlockSpec((TM_FFN, D_MODEL), lambda i, f: (i, 0)),
        out_shape=jax.ShapeDtypeStruct((t, D_MODEL), F32),
        scratch_shapes=[pltpu.VMEM((TM_FFN, D_MODEL), F32)],
        compiler_params=_cparams(("parallel", "arbitrary")),
        name="dense_swiglu",
    )(x2, h, wg, wu, wd)


def _issue_row_gather(src_hbm, idx_ref, idx_base, dst, sem, n_rows):
    def body(r, carry):
        tok = idx_ref[idx_base + r]
        pltpu.make_async_copy(src_hbm.at[pl.ds(tok, 1)], dst.at[pl.ds(r, 1)], sem).start()
        return carry
    lax.fori_loop(0, n_rows, body, 0, unroll=8)


def _wait_row_gather(src_hbm, dst, sem, n_rows):
    pltpu.make_async_copy(src_hbm.at[pl.ds(0, n_rows)], dst, sem).wait()


def _moe_ffn_kernel(te_ref, nv_ref, tb_ref, rt_ref, h_hbm, wg_ref, wu_ref, wd_ref, y_ref, hbuf, xb_ref, acc_ref, sem):
    i = pl.program_id(0)
    f = pl.program_id(1)
    n_valid = nv_ref[0]
    valid = i < n_valid
    slot = i % 2

    @pl.when((f == 0) & (i == 0) & valid)
    def _first():
        _issue_row_gather(h_hbm, rt_ref, tb_ref[0], hbuf.at[0], sem.at[0], TR_MOE)

    @pl.when((f == 0) & valid)
    def _arrive():
        _wait_row_gather(h_hbm, hbuf.at[slot], sem.at[slot], TR_MOE)
        xb_ref[...] = hbuf[slot].astype(BF16)

    def compute(prefetch_next):
        if prefetch_next:
            rows_per_step = TR_MOE // (FFN_EXPERT // TF_MOE)
            r0 = f * rows_per_step
            src0 = tb_ref[i + 1] + r0
            for j in range(rows_per_step):
                tok = rt_ref[src0 + j]
                pltpu.make_async_copy(h_hbm.at[pl.ds(tok, 1)], hbuf.at[1 - slot, pl.ds(r0 + j, 1)],
                                      sem.at[1 - slot]).start()
        xb = xb_ref[...]
        gate = jnp.dot(xb, wg_ref[0].astype(BF16), preferred_element_type=F32)
        up = jnp.dot(xb, wu_ref[0].astype(BF16), preferred_element_type=F32)
        act = (_silu(gate) * up).astype(BF16)
        contrib = jnp.dot(act, wd_ref[0].astype(BF16), preferred_element_type=F32)

        @pl.when(f == 0)
        def _():
            acc_ref[...] = contrib

        @pl.when(f > 0)
        def _():
            acc_ref[...] += contrib

    has_next = i + 1 < n_valid

    @pl.when(valid & has_next)
    def _compute_and_prefetch():
        compute(True)

    @pl.when(valid & jnp.logical_not(has_next))
    def _compute_last():
        compute(False)

    @pl.when(f == pl.num_programs(1) - 1)
    def _store():
        y_ref[...] = jnp.where(valid, acc_ref[...], 0.0)


def _moe_ffn(h, tile_expert, n_valid, tile_base, sorted_token, wg, wu, wd):
    n_tiles = tile_expert.shape[0]
    nf = wg.shape[2] // TF_MOE

    def w_in_map(i, f, te, nv, tb, rt):
        return (te[i], 0, jnp.where(i < nv[0], f, nf - 1))

    def w_out_map(i, f, te, nv, tb, rt):
        return (te[i], jnp.where(i < nv[0], f, nf - 1), 0)

    grid_spec = pltpu.PrefetchScalarGridSpec(
        num_scalar_prefetch=4,
        grid=(n_tiles, nf),
        in_specs=[
            pl.BlockSpec(memory_space=pl.ANY),
            pl.BlockSpec((1, D_MODEL, TF_MOE), w_in_map),
            pl.BlockSpec((1, D_MODEL, TF_MOE), w_in_map),
            pl.BlockSpec((1, TF_MOE, D_MODEL), w_out_map),
        ],
        out_specs=pl.BlockSpec((TR_MOE, D_MODEL), lambda i, f, te, nv, tb, rt: (i, 0)),
        scratch_shapes=[
            pltpu.VMEM((2, TR_MOE, D_MODEL), F32),
            pltpu.VMEM((TR_MOE, D_MODEL), BF16),
            pltpu.VMEM((TR_MOE, D_MODEL), F32),
            pltpu.SemaphoreType.DMA((2,)),
        ],
    )
    return pl.pallas_call(
        _moe_ffn_kernel,
        grid_spec=grid_spec,
        out_shape=jax.ShapeDtypeStruct((n_tiles * TR_MOE, D_MODEL), F32),
        compiler_params=_cparams(("arbitrary", "arbitrary")),
        name="moe_expert_ffn",
    )(tile_expert, n_valid, tile_base, sorted_token, h, wg, wu, wd)


def _moe_combine_kernel(tr_ref, x_ref, rinfo_ref, y_hbm, fg_ref, o_ref, ybuf, sem):
    i = pl.program_id(0)
    slot = i % 2
    n_rows = TOP_K * TM_COMB

    @pl.when(i == 0)
    def _first():
        _issue_row_gather(y_hbm, tr_ref, 0, ybuf.at[0], sem.at[0], n_rows)

    _wait_row_gather(y_hbm, ybuf.at[slot], sem.at[slot], n_rows)

    @pl.when(i + 1 < pl.num_programs(0))
    def _prefetch():
        src0 = (i + 1) * n_rows
        dst = ybuf.at[1 - slot]
        for r in range(n_rows):
            pltpu.make_async_copy(y_hbm.at[pl.ds(tr_ref[src0 + r], 1)], dst.at[pl.ds(r, 1)], sem.at[1 - slot]).start()

    rinfo = rinfo_ref[...]
    g1 = jnp.broadcast_to(rinfo[:, 0:1], (TM_COMB, D_MODEL))
    g2 = jnp.broadcast_to(rinfo[:, 1:2], (TM_COMB, D_MODEL))
    y = x_ref[...] + g1 * ybuf[slot, 0:TM_COMB, :] + g2 * ybuf[slot, TM_COMB:n_rows, :]
    ms = jnp.mean(y * y, axis=-1, keepdims=True)
    o_ref[...] = y * lax.rsqrt(ms + EPS) * fg_ref[...]


def _moe_combine(x2, rinfo, y_rows, tok_rows, final_gain):
    t = x2.shape[0]
    grid_spec = pltpu.PrefetchScalarGridSpec(
        num_scalar_prefetch=1,
        grid=(t // TM_COMB,),
        in_specs=[
            pl.BlockSpec((TM_COMB, D_MODEL), lambda i, tr: (i, 0)),
            pl.BlockSpec((TM_COMB, LANES), lambda i, tr: (i, 0)),
            pl.BlockSpec(memory_space=pl.ANY),
            pl.BlockSpec((1, D_MODEL), lambda i, tr: (0, 0)),
        ],
        out_specs=pl.BlockSpec((TM_COMB, D_MODEL), lambda i, tr: (i, 0)),
        scratch_shapes=[
            pltpu.VMEM((2, TOP_K * TM_COMB, D_MODEL), F32),
            pltpu.SemaphoreType.DMA((2,)),
        ],
    )
    return pl.pallas_call(
        _moe_combine_kernel,
        grid_spec=grid_spec,
        out_shape=jax.ShapeDtypeStruct((t, D_MODEL), F32),
        compiler_params=_cparams(("arbitrary",)),
        name="moe_combine_norm",
    )(tok_rows, x2, rinfo, y_rows, final_gain)


def _routing_tables(rinfo):
    t = rinfo.shape[0]
    n_assign = TOP_K * t
    n_tiles = -(-n_assign // TR_MOE) + N_EXPERTS
    flat_e = rinfo[:, 2:2 + TOP_K].astype(jnp.int32).reshape(-1)
    onehot = (flat_e[:, None] == jnp.arange(N_EXPERTS)[None, :]).astype(jnp.int32)
    csum = jnp.cumsum(onehot, axis=0)
    rank = jnp.sum((csum - onehot) * onehot, axis=1)
    counts = csum[-1]
    tiles_per_e = (counts + TR_MOE - 1) // TR_MOE
    tile_end = jnp.cumsum(tiles_per_e)
    first_tile = tile_end - tiles_per_e
    dest = (first_tile * TR_MOE)[flat_e] + rank
    order = jnp.sort(flat_e * n_assign + jnp.arange(n_assign, dtype=jnp.int32))
    sorted_token = jnp.concatenate([(order % n_assign) // TOP_K, jnp.zeros((TR_MOE,), jnp.int32)])
    start = jnp.cumsum(counts) - counts
    n_valid = tile_end[-1:].astype(jnp.int32)
    tile_ids = jnp.arange(n_tiles)
    tile_expert = jnp.searchsorted(tile_end, tile_ids, side="right").astype(jnp.int32)
    last_e = jnp.searchsorted(tile_end, n_valid[0] - 1, side="right").astype(jnp.int32)
    tile_expert = jnp.where(tile_ids < n_valid[0], tile_expert, last_e)
    tile_base = start[tile_expert] + (tile_ids - first_tile[tile_expert]) * TR_MOE
    tile_base = jnp.clip(tile_base, 0, n_assign).astype(jnp.int32)
    tok_rows = dest.reshape(t // TM_COMB, TM_COMB, TOP_K).transpose(0, 2, 1).reshape(-1).astype(jnp.int32)
    return tile_expert, n_valid, tile_base, sorted_token, tok_rows


def _permute_w_in(w):
    sizes = (256, 256, 256, 512, 768, 8, 256, 256, 256, 256)
    offs = np.concatenate([[0], np.cumsum(sizes)])
    part = lambda n: w[:, offs[n]:offs[n + 1]]
    dt_pad = jnp.zeros((w.shape[0], LANES - SSM_HEADS), w.dtype)
    cols = [part(0) * (DA_HEAD_DIM ** -0.5 * LOG2E), part(2), part(1), part(3), part(4),
            part(6), part(7), part(8), part(9), part(5), dt_pad]
    return jnp.concatenate(cols, axis=1).astype(BF16)


def _column_tiles(w, tile):
    *lead, k, n = w.shape
    return jnp.moveaxis(w.reshape(*lead, k, n // tile, tile), -2, -3)


def _pad_lanes(v, fill=0.0):
    return jnp.concatenate([v, jnp.full((LANES - v.shape[0],), fill, v.dtype)])[None, :]


def kernel(x, w_in, w_out, attn_norm, ffn_norm, final_norm, rel_bias, lambda_q1, lambda_k1, lambda_q2,
           lambda_k2, da_head_norm, conv_w, conv_b, dt_bias, a_log, d_skip, ssm_norm, ret_head_norm,
           w_gate, w_up, w_down, router_w, e_gate, e_up, e_down):
    batch, seq, _ = x.shape
    x2 = x.reshape(batch * seq, D_MODEL)
    bias_tiles = _bias_tiles(rel_bias)
    ret_tabs = _retention_tables(seq)

    for layer in range(DEPTH):
        lam_init = 0.8 - 0.6 * math.exp(-0.3 * layer)
        lam = (jnp.exp(jnp.sum(lambda_q1[layer] * lambda_k1[layer]))
               - jnp.exp(jnp.sum(lambda_q2[layer] * lambda_k2[layer])) + lam_init).astype(F32)
        lam = jnp.full((1, ATT_T), lam, F32)

        qkv, qvt, pf = _inproj(x2, attn_norm[layer][None, :], _permute_w_in(w_in[layer]))

        da_gain = (jnp.tile(da_head_norm[layer], DA_HEADS) * (1.0 - lam_init))[:, None]
        out_a = _attention(qkv, qvt, bias_tiles, da_gain, lam, batch, seq)

        cw, cb = conv_w[layer], conv_b[layer]
        out_b = _ssd(pf, cw[:, :SSM_WIDTH], cw[:, SSM_WIDTH:], cb[None, :SSM_WIDTH], cb[None, SSM_WIDTH:],
                     _pad_lanes(dt_bias[layer]), _pad_lanes(a_log[layer]),
                     jnp.repeat(d_skip[layer], SSM_HEAD_DIM)[None, :], ssm_norm[layer][None, :], batch, seq)

        ret_gain = jnp.tile(ret_head_norm[layer], RET_HEADS)[None, :]
        out_c = _retention(pf, *ret_tabs, ret_gain, batch, seq)

        w_o = w_out[layer].astype(BF16)
        if layer % 2 == 0:
            i = layer // 2
            x2, h = _outproj(x2, out_a, out_b, out_c, w_o, ffn_norm[layer][None, :])
            pad = FFN_PAD - FFN_DENSE
            wg = _column_tiles(jnp.pad(w_gate[i], ((0, 0), (0, pad))).astype(BF16), TF_FFN)
            wu = _column_tiles(jnp.pad(w_up[i], ((0, 0), (0, pad))).astype(BF16), TF_FFN)
            wd = jnp.pad(w_down[i], ((0, pad), (0, 0))).astype(BF16)
            x2 = _dense_ffn(x2, h, wg, wu, wd)
        else:
            i = layer // 2
            rw = jnp.pad(router_w[i], ((0, 0), (0, LANES - N_EXPERTS))).astype(BF16)
            x2, h, rinfo = _outproj(x2, out_a, out_b, out_c, w_o, ffn_norm[layer][None, :], rw)
            tile_expert, n_valid, tile_base, sorted_token, tok_rows = _routing_tables(rinfo)
            y_rows = _moe_ffn(h, tile_expert, n_valid, tile_base, sorted_token,
                              e_gate[i], e_up[i], e_down[i])
            x2 = _moe_combine(x2, rinfo, y_rows, tok_rows, final_norm[None, :])
    return x2.reshape(batch, seq, D_MODEL)
```

```python
import functools
import math

import jax
import jax.numpy as jnp
import numpy as np
from jax import lax
from jax.experimental import pallas as pl
from jax.experimental.pallas import tpu as pltpu

F32 = jnp.float32
BF16 = jnp.bfloat16

D_MODEL = 1024
DEPTH = 2
CHUNK = 64
EPS = 1e-6
DA_HEADS = 4
DA_HEAD_DIM = 32
DA_V_DIM = 64
DA_WIDTH = 256
SSM_HEADS = 8
SSM_HEAD_DIM = 64
SSM_WIDTH = 512
SSM_STATE = 64
SSM_GROUPS = 2
SSM_CONV = 4
RET_HEADS = 4
RET_KEY_DIM = 64
RET_WIDTH = 256
ROPE_BASE = 10000.0
REL_BUCKETS = 32
REL_MAX_DIST = 128
FFN_DENSE = 2752
N_EXPERTS = 8
FFN_EXPERT = 3584

LANES = 128
VMEM_LIMIT = 48 * 1024 * 1024
NEG_BIG = -1e30

TM_PROJ = 512
ATT_T = 512
LOG2E = math.log2(math.e)
ATT_FAR2, ATT_FAR1, ATT_NEAR = 0, 1, 2
BLK = 256
FFN_PAD = 2816
TM_FFN = 512
TF_FFN = 1408
TF_MOE = 512
TR_MOE = 896
TM_COMB = 256
TOP_K = 2

C_Z, C_XS, C_BC, C_RQ, C_RK, C_RV, C_RG, C_DT = 0, 512, 1024, 1280, 1536, 1792, 2048, 2304
PF_W = 2432
PF_CHUNK = 640
QKV_W = 768


def _cparams(sem):
    return pltpu.CompilerParams(dimension_semantics=sem, vmem_limit_bytes=VMEM_LIMIT)


def _group_mask(shape, axis, group, idx):
    io = lax.broadcasted_iota(jnp.int32, shape, axis)
    return (io >= idx * group) & (io < (idx + 1) * group)


def _lane_tile(x, n, axis=1):
    return jnp.concatenate([x] * n, axis=axis)


def _silu(x):
    h = 0.5 * x
    return h + h * jnp.tanh(h)


def _inproj_kernel(x_ref, g_ref, w_ref, qkv_ref, qvt_ref, pf_ref):
    x = x_ref[...]
    ms = jnp.mean(x * x, axis=-1, keepdims=True)
    h = (x * lax.rsqrt(ms + EPS) * g_ref[...]).astype(BF16)
    qkv = jnp.dot(h, w_ref[:, :QKV_W], preferred_element_type=F32)
    qkv_ref[...] = qkv.astype(BF16)
    qvt_ref[...] = qkv[:, :2 * DA_WIDTH].T.astype(BF16)
    for lo in range(0, PF_W, PF_CHUNK):
        hi = min(lo + PF_CHUNK, PF_W)
        pf_ref[:, lo:hi] = jnp.dot(h, w_ref[:, QKV_W + lo:QKV_W + hi], preferred_element_type=F32)


def _inproj(x2, gain, w):
    t = x2.shape[0]
    return pl.pallas_call(
        _inproj_kernel,
        grid=(t // TM_PROJ,),
        in_specs=[
            pl.BlockSpec((TM_PROJ, D_MODEL), lambda i: (i, 0)),
            pl.BlockSpec((1, D_MODEL), lambda i: (0, 0)),
            pl.BlockSpec((D_MODEL, QKV_W + PF_W), lambda i: (0, 0)),
        ],
        out_specs=[
            pl.BlockSpec((TM_PROJ, QKV_W), lambda i: (i, 0)),
            pl.BlockSpec((2 * DA_WIDTH, TM_PROJ), lambda i: (0, i)),
            pl.BlockSpec((TM_PROJ, PF_W), lambda i: (i, 0)),
        ],
        out_shape=[
            jax.ShapeDtypeStruct((t, QKV_W), BF16),
            jax.ShapeDtypeStruct((2 * DA_WIDTH, t), BF16),
            jax.ShapeDtypeStruct((t, PF_W), F32),
        ],
        compiler_params=_cparams(("parallel",)),
        name="inproj",
    )(x2, gain, w)


def _attn_kernel(qi_ref, kj_ref, kind_ref, qt_ref, k_ref, vt_ref, k2_ref, vt2_ref, bias_ref, gain_ref, lam_ref,
                 o_ref, qm_ref, m_ref, l_ref, acc_ref, s_ref, e_ref, al_ref):
    p = pl.program_id(1)
    kj = kj_ref[p]
    kind = kind_ref[p]
    n_ht = 2 * DA_HEADS

    @pl.when(kj == 0)
    def _init():
        qt = qt_ref[...]
        for ht in range(n_ht):
            qm_ref[ht] = jnp.where(_group_mask(qt.shape, 0, DA_HEAD_DIM, ht), qt, jnp.zeros_like(qt))
        m_ref[...] = jnp.full(m_ref.shape, NEG_BIG, F32)
        l_ref[...] = jnp.zeros(l_ref.shape, F32)
        acc_ref[...] = jnp.zeros(acc_ref.shape, F32)

    def run(tiles):
        n_pos = n_ht * len(tiles)

        def scores(n):
            k_r = tiles[n // n_ht][0]
            s_ref[n % 2] = jnp.dot(k_r[...], qm_ref[n % n_ht], preferred_element_type=F32)

        def softmax(n):
            t, buf, with_bias = n % n_ht, n % 2, tiles[n // n_ht][2]
            for c0 in range(0, ATT_T, LANES):
                cols = slice(c0, c0 + LANES)

                def block():
                    blk = s_ref[buf, :, cols]
                    return blk + bias_ref[0, t // 2, :, cols] if with_bias else blk

                m_prev = m_ref[t, :, cols]
                m_new = jnp.maximum(m_prev, jnp.max(block(), axis=0, keepdims=True))
                alpha = jnp.exp2(m_prev - m_new)
                e = jnp.exp2(block() - m_new)
                l_ref[t, :, cols] = alpha * l_ref[t, :, cols] + jnp.sum(e, axis=0, keepdims=True)
                m_ref[t, :, cols] = m_new
                al_ref[buf, :, cols] = alpha
                e_ref[buf, :, cols] = e.astype(BF16)

        def values(n):
            t, buf, vt_r = n % n_ht, n % 2, tiles[n // n_ht][1]
            r0 = (t // 4) * LANES
            pv = jnp.dot(vt_r[r0:r0 + LANES, :], e_ref[buf], preferred_element_type=F32)
            acc_ref[t] = acc_ref[t] * al_ref[buf] + pv

        scores(0)
        scores(1)
        softmax(0)
        for n in range(1, n_pos - 1):
            scores(n + 1)
            softmax(n)
            values(n - 1)
        softmax(n_pos - 1)
        values(n_pos - 2)
        values(n_pos - 1)

    @pl.when(kind == ATT_FAR2)
    def _far_pair():
        run([(k_ref, vt_ref, False), (k2_ref, vt2_ref, False)])

    @pl.when(kind == ATT_FAR1)
    def _far_single():
        run([(k_ref, vt_ref, False)])

    @pl.when(kind == ATT_NEAR)
    def _near():
        run([(k_ref, vt_ref, True)])

    @pl.when(kj == qi_ref[p])
    def _finish():
        lam = lam_ref[...]
        row_lo = lax.broadcasted_iota(jnp.int32, (LANES, ATT_T), 0) < DA_V_DIM
        for pair in range(DA_HEADS // 2):
            halves = []
            for sub in range(2):
                h = 2 * pair + sub
                o = acc_ref[2 * h] * (1.0 / l_ref[2 * h]) - lam * (acc_ref[2 * h + 1] * (1.0 / l_ref[2 * h + 1]))
                own = row_lo if sub == 0 else jnp.logical_not(row_lo)
                ms = jnp.sum(jnp.where(own, o * o, 0.0), axis=0, keepdims=True) * (1.0 / DA_V_DIM)
                halves.append(o * lax.rsqrt(ms + EPS))
            r0 = pair * LANES
            blk = jnp.where(row_lo, halves[0], halves[1]) * gain_ref[r0:r0 + LANES, :]
            o_ref[:, r0:r0 + LANES] = blk.T.astype(BF16)


def _attention(qkv, qvt, bias_tiles, gain, lam, batch, seq):
    nq = seq // ATT_T
    qi_l, kj_l, kind_l = [], [], []
    for qi in range(nq):
        n_far = max(qi - 1, 0)
        for kj in range(0, n_far - 1, 2):
            qi_l.append(qi), kj_l.append(kj), kind_l.append(ATT_FAR2)
        if n_far % 2:
            qi_l.append(qi), kj_l.append(n_far - 1), kind_l.append(ATT_FAR1)
        for kj in range(n_far, qi + 1):
            qi_l.append(qi), kj_l.append(kj), kind_l.append(ATT_NEAR)
    qi_tbl = jnp.asarray(np.array(qi_l, np.int32))
    kj_tbl = jnp.asarray(np.array(kj_l, np.int32))
    kind_tbl = jnp.asarray(np.array(kind_l, np.int32))
    n_steps = len(qi_l)
    n_ht = 2 * DA_HEADS

    def second(kj, kind, p):
        return kj[p] + jnp.where(kind[p] == ATT_FAR2, 1, 0)

    grid_spec = pltpu.PrefetchScalarGridSpec(
        num_scalar_prefetch=3,
        grid=(batch, n_steps),
        in_specs=[
            pl.BlockSpec((DA_WIDTH, ATT_T), lambda b, p, qi, kj, kind: (0, b * nq + qi[p])),
            pl.BlockSpec((ATT_T, DA_WIDTH), lambda b, p, qi, kj, kind: (b * nq + kj[p], 2)),
            pl.BlockSpec((DA_WIDTH, ATT_T), lambda b, p, qi, kj, kind: (1, b * nq + kj[p])),
            pl.BlockSpec((ATT_T, DA_WIDTH), lambda b, p, qi, kj, kind: (b * nq + second(kj, kind, p), 2)),
            pl.BlockSpec((DA_WIDTH, ATT_T), lambda b, p, qi, kj, kind: (1, b * nq + second(kj, kind, p))),
            pl.BlockSpec((1, DA_HEADS, ATT_T, ATT_T),
                         lambda b, p, qi, kj, kind: (jnp.minimum(qi[p] - kj[p], 1), 0, 0, 0)),
            pl.BlockSpec((DA_WIDTH, 1), lambda b, p, qi, kj, kind: (0, 0)),
            pl.BlockSpec((1, ATT_T), lambda b, p, qi, kj, kind: (0, 0)),
        ],
        out_specs=pl.BlockSpec((ATT_T, DA_WIDTH), lambda b, p, qi, kj, kind: (b * nq + qi[p], 0)),
        scratch_shapes=[
            pltpu.VMEM((n_ht, DA_WIDTH, ATT_T), BF16),
            pltpu.VMEM((n_ht, 1, ATT_T), F32),
            pltpu.VMEM((n_ht, 1, ATT_T), F32),
            pltpu.VMEM((n_ht, LANES, ATT_T), F32),
            pltpu.VMEM((2, ATT_T, ATT_T), F32),
            pltpu.VMEM((2, ATT_T, ATT_T), BF16),
            pltpu.VMEM((2, 1, ATT_T), F32),
        ],
    )
    return pl.pallas_call(
        _attn_kernel,
        grid_spec=grid_spec,
        out_shape=jax.ShapeDtypeStruct((batch * seq, DA_WIDTH), BF16),
        compiler_params=_cparams(("parallel", "arbitrary")),
        name="diff_attention",
    )(qi_tbl, kj_tbl, kind_tbl, qvt, qkv, qvt, qkv, qvt, bias_tiles, gain, lam)


def _t5_bucket(rel):
    half = REL_BUCKETS // 2
    max_exact = half // 2
    ret = jnp.where(rel > 0, half, 0)
    n = jnp.abs(rel)
    nf = jnp.maximum(n, 1).astype(F32)
    large = max_exact + (jnp.log(nf / max_exact) / math.log(REL_MAX_DIST / max_exact)
                         * (half - max_exact)).astype(jnp.int32)
    large = jnp.minimum(large, half - 1)
    return ret + jnp.where(n < max_exact, n, large)


def _bias_tiles(rel_bias):
    n = ATT_T
    i = jnp.arange(n)[None, :]
    j = jnp.arange(n)[:, None]
    far = rel_bias[_t5_bucket(jnp.int32(-2 * n))].astype(F32)
    tiles = []
    for off in range(2):
        rel = (n - 1 - off * n) - jnp.arange(2 * n)
        bucket = _t5_bucket(rel)
        g = jnp.zeros((DA_HEADS, 2 * n), F32)
        for c in range(REL_BUCKETS):
            g = jnp.where((bucket == c)[None], rel_bias[c].astype(F32)[:, None], g)
        g = (g - far[:, None]) * LOG2E
        skew = jnp.broadcast_to(g[:, None, :], (DA_HEADS, n, 2 * n)).reshape(DA_HEADS, 2 * n * n)
        skew = skew[:, :n * (2 * n - 1)].reshape(DA_HEADS, n, 2 * n - 1)
        b = skew[:, :, n - 1:]
        if off == 0:
            allowed = (j // CHUNK) <= (i // CHUNK)
            b = jnp.where(allowed[None], b, NEG_BIG)
        tiles.append(b)
    return jnp.stack(tiles)


def _cumsum_rows(x):
    n = x.shape[0]
    row = lax.broadcasted_iota(jnp.int32, x.shape, 0)
    sh = 1
    while sh < n:
        x = x + jnp.where(row >= sh, pltpu.roll(x, sh, axis=0), 0.0)
        sh *= 2
    return x


def _expand_heads(v, width, group):
    out = jnp.zeros((v.shape[0], width), F32)
    for h in range(width // group):
        col = jnp.broadcast_to(v[:, h:h + 1], (v.shape[0], width))
        out = jnp.where(_group_mask(out.shape, 1, group, h), col, out)
    return out


def _ssd_kernel(z_ref, xs_ref, bc_ref, dt_ref, cwx_ref, cwb_ref, cbx_ref, cbb_ref, dtb_ref, alog_ref,
                dskip_ref, ng_ref, o_ref, extx_ref, extb_ref, st_ref):
    j = pl.program_id(1)

    @pl.when(j == 0)
    def _reset():
        extx_ref[0:8, :] = jnp.zeros((8, SSM_WIDTH), F32)
        extb_ref[0:8, :] = jnp.zeros((8, 2 * LANES), F32)
        st_ref[...] = jnp.zeros(st_ref.shape, F32)

    extx_ref[8:8 + BLK, :] = xs_ref[...]
    extb_ref[8:8 + BLK, :] = bc_ref[...]
    yx = jnp.broadcast_to(cbx_ref[...], (BLK, SSM_WIDTH))
    yb = jnp.broadcast_to(cbb_ref[...], (BLK, 2 * LANES))
    for w in range(SSM_CONV):
        lo = 8 - (SSM_CONV - 1) + w
        yx = yx + extx_ref[lo:lo + BLK, :] * cwx_ref[w:w + 1, :]
        yb = yb + extb_ref[lo:lo + BLK, :] * cwb_ref[w:w + 1, :]
    tail_x = extx_ref[BLK:BLK + 8, :]
    tail_b = extb_ref[BLK:BLK + 8, :]
    extx_ref[0:8, :] = tail_x
    extb_ref[0:8, :] = tail_b
    xs = _silu(yx)
    bc = _silu(yb)
    bm = bc[:, :LANES]
    cm = bc[:, LANES:]

    dt = jax.nn.softplus(dt_ref[...] + dtb_ref[...])
    a = -jnp.exp(alog_ref[...]) * dt
    a_cs = _cumsum_rows(a)
    a_cs_t = a_cs.T
    a_last = a_cs[BLK - 1:BLK, :]

    dt_full = _expand_heads(dt, SSM_WIDTH, SSM_HEAD_DIM)
    ea_full = _expand_heads(jnp.exp(a_cs), SSM_WIDTH, SSM_HEAD_DIM)
    dec_full = _expand_heads(jnp.exp(a_last - a_cs), SSM_WIDTH, SSM_HEAD_DIM)
    cdec_full = _expand_heads(jnp.exp(a_last), SSM_WIDTH, SSM_HEAD_DIM)

    xdt = xs * dt_full
    xdt_b = xdt.astype(BF16)
    cm_b = cm.astype(BF16)
    bm_b = bm.astype(BF16)

    st = st_ref[...]
    y = jnp.dot(cm_b, st.astype(BF16), preferred_element_type=F32) * ea_full + xs * dskip_ref[...]

    row = lax.broadcasted_iota(jnp.int32, (BLK, BLK), 0)
    colio = lax.broadcasted_iota(jnp.int32, (BLK, BLK), 1)
    causal = colio <= row
    rep = SSM_HEADS // SSM_GROUPS
    lane_lo = lax.broadcasted_iota(jnp.int32, (BLK, LANES), 1) < SSM_HEAD_DIM
    pieces = []
    for g in range(SSM_GROUPS):
        cg = jnp.where(_group_mask(cm_b.shape, 1, SSM_STATE, g), cm_b, jnp.zeros_like(cm_b))
        cb = lax.dot_general(cg, bm_b, (((1,), (1,)), ((), ())), preferred_element_type=F32)
        for pair in range(rep // 2):
            halves = []
            for sub in range(2):
                h = g * rep + pair * 2 + sub
                seg = jnp.broadcast_to(a_cs[:, h:h + 1], (BLK, BLK)) - a_cs_t[h:h + 1, :]
                lmat = jnp.exp(jnp.where(causal, seg, NEG_BIG))
                mh = (cb * lmat).astype(BF16)
                c0 = (h // 2) * LANES
                halves.append(jnp.dot(mh, xdt_b[:, c0:c0 + LANES], preferred_element_type=F32))
            pieces.append(jnp.where(lane_lo, halves[0], halves[1]))
    y = y + jnp.concatenate(pieces, axis=1)

    upd = jnp.dot(bm.T.astype(BF16), (xdt * dec_full).astype(BF16), preferred_element_type=F32)
    r_io = lax.broadcasted_iota(jnp.int32, upd.shape, 0) // SSM_STATE
    c_io = lax.broadcasted_iota(jnp.int32, upd.shape, 1) // (SSM_HEAD_DIM * rep)
    st_ref[...] = st * cdec_full + jnp.where(r_io == c_io, upd, 0.0)

    y = y * _silu(z_ref[...])
    gw = SSM_WIDTH // SSM_GROUPS
    for g in range(SSM_GROUPS):
        yg = y[:, g * gw:(g + 1) * gw]
        ms = jnp.mean(yg * yg, axis=-1, keepdims=True)
        o_ref[:, g * gw:(g + 1) * gw] = (yg * lax.rsqrt(ms + EPS) * ng_ref[:, g * gw:(g + 1) * gw]).astype(BF16)


def _ssd(pf, cwx, cwb, cbx, cbb, dtb, alog, dskip, ng, batch, seq):
    nb = seq // BLK
    row = lambda w, c: pl.BlockSpec((BLK, w), lambda b, j: (b * nb + j, c))
    const = lambda r, w: pl.BlockSpec((r, w), lambda b, j: (0, 0))
    return pl.pallas_call(
        _ssd_kernel,
        grid=(batch, nb),
        in_specs=[
            row(SSM_WIDTH, C_Z // SSM_WIDTH), row(SSM_WIDTH, C_XS // SSM_WIDTH), row(2 * LANES, C_BC // (2 * LANES)),
            row(LANES, C_DT // LANES),
            const(SSM_CONV, SSM_WIDTH), const(SSM_CONV, 2 * LANES), const(1, SSM_WIDTH), const(1, 2 * LANES),
            const(1, LANES), const(1, LANES), const(1, SSM_WIDTH), const(1, SSM_WIDTH),
        ],
        out_specs=pl.BlockSpec((BLK, SSM_WIDTH), lambda b, j: (b * nb + j, 0)),
        out_shape=jax.ShapeDtypeStruct((batch * seq, SSM_WIDTH), BF16),
        scratch_shapes=[
            pltpu.VMEM((BLK + 8, SSM_WIDTH), F32),
            pltpu.VMEM((BLK + 8, 2 * LANES), F32),
            pltpu.VMEM((SSM_GROUPS * SSM_STATE, SSM_WIDTH), F32),
        ],
        compiler_params=_cparams(("parallel", "arbitrary")),
        name="ssd_mixer",
    )(pf, pf, pf, pf, cwx, cwb, cbx, cbb, dtb, alog, dskip, ng)


def _rotary(u, cos, sin_signed):
    first = (lax.broadcasted_iota(jnp.int32, u.shape, 1) % RET_KEY_DIM) < (RET_KEY_DIM // 2)
    half = RET_KEY_DIM // 2
    swapped = jnp.where(first, pltpu.roll(u, u.shape[1] - half, axis=1), pltpu.roll(u, half, axis=1))
    return u * cos + swapped * sin_signed


def _ret_kernel(q_ref, k_ref, v_ref, g_ref, cos_ref, sin_ref, dmat_ref, qdec_ref, kdec_ref, gdec_ref,
                bd_ref, gain_ref, o_ref, st_ref):
    j = pl.program_id(1)

    @pl.when(j == 0)
    def _reset():
        st_ref[...] = jnp.zeros(st_ref.shape, F32)

    cos = _lane_tile(cos_ref[...], RET_WIDTH // LANES)
    sin = _lane_tile(sin_ref[...], RET_WIDTH // LANES)
    q = _rotary(q_ref[...], cos, sin)
    k = _rotary(k_ref[...], cos, sin) * (RET_KEY_DIM ** -0.5)
    v_b = v_ref[...].astype(BF16)
    q_b = q.astype(BF16)
    k_b = k.astype(BF16)

    st = st_ref[...]
    o = jnp.dot((q * qdec_ref[...]).astype(BF16), st.astype(BF16), preferred_element_type=F32)
    for h in range(RET_HEADS):
        cm = _group_mask(q_b.shape, 1, RET_KEY_DIM, h)
        qh = jnp.where(cm, q_b, jnp.zeros_like(q_b))
        s = lax.dot_general(qh, k_b, (((1,), (1,)), ((), ())), preferred_element_type=F32) * dmat_ref[h]
        oh = jnp.dot(s.astype(BF16), v_b, preferred_element_type=F32)
        o = o + jnp.where(cm, oh, 0.0)

    upd = jnp.dot((k * kdec_ref[...]).T.astype(BF16), v_b, preferred_element_type=F32)
    st_ref[...] = st * gdec_ref[...] + upd * bd_ref[...]

    out = jnp.zeros(o.shape, F32)
    for h in range(RET_HEADS):
        cm = _group_mask(o.shape, 1, RET_KEY_DIM, h)
        ms = jnp.sum(jnp.where(cm, o * o, 0.0), axis=-1, keepdims=True) * (1.0 / RET_KEY_DIM)
        out = out + jnp.where(cm, o * lax.rsqrt(ms + EPS), 0.0)
    o_ref[...] = (_silu(g_ref[...]) * (out * gain_ref[...])).astype(BF16)


def _retention(pf, cos, sin, dmat, qdec, kdec, gdec, bd, gain, batch, seq):
    nb = seq // BLK
    row = lambda c: pl.BlockSpec((BLK, RET_WIDTH), lambda b, j: (b * nb + j, c))
    const2 = pl.BlockSpec((BLK, RET_WIDTH), lambda b, j: (0, 0))
    return pl.pallas_call(
        _ret_kernel,
        grid=(batch, nb),
        in_specs=[
            row(C_RQ // RET_WIDTH), row(C_RK // RET_WIDTH), row(C_RV // RET_WIDTH), row(C_RG // RET_WIDTH),
            pl.BlockSpec((BLK, LANES), lambda b, j: (j, 0)),
            pl.BlockSpec((BLK, LANES), lambda b, j: (j, 0)),
            pl.BlockSpec((RET_HEADS, BLK, BLK), lambda b, j: (0, 0, 0)),
            const2, const2, const2, const2,
            pl.BlockSpec((1, RET_WIDTH), lambda b, j: (0, 0)),
        ],
        out_specs=pl.BlockSpec((BLK, RET_WIDTH), lambda b, j: (b * nb + j, 0)),
        out_shape=jax.ShapeDtypeStruct((batch * seq, RET_WIDTH), BF16),
        scratch_shapes=[pltpu.VMEM((RET_WIDTH, RET_WIDTH), F32)],
        compiler_params=_cparams(("parallel", "arbitrary")),
        name="retention",
    )(pf, pf, pf, pf, cos, sin, dmat, qdec, kdec, gdec, bd, gain)


def _retention_tables(seq):
    f32 = np.float32
    inv = (f32(1.0) / (f32(ROPE_BASE) ** (np.arange(0, RET_KEY_DIM, 2, dtype=f32) / f32(RET_KEY_DIM)))).astype(f32)
    ang = (np.arange(seq, dtype=f32)[:, None] * inv[None, :]).astype(f32)
    cos_h = np.concatenate([np.cos(ang), np.cos(ang)], axis=-1)
    sin_h = np.concatenate([-np.sin(ang), np.sin(ang)], axis=-1)
    cos = np.tile(cos_h, (1, 2)).astype(f32)
    sin = np.tile(sin_h, (1, 2)).astype(f32)
    log_gamma = np.log1p(-np.power(f32(2.0), f32(-5.0) - np.arange(RET_HEADS, dtype=f32))).astype(f32)
    idx = np.arange(BLK, dtype=f32)
    ii = np.arange(BLK)
    same_or_earlier_chunk = (ii[None, :] // CHUNK) <= (ii[:, None] // CHUNK)
    dmat = np.exp(log_gamma[:, None, None] * np.abs(idx[:, None] - idx[None, :]))
    dmat = np.where(same_or_earlier_chunk[None], dmat, 0.0).astype(f32)
    lg_cols = np.repeat(log_gamma, RET_KEY_DIM)
    qdec = np.exp(lg_cols[None, :] * (idx + 1.0)[:, None]).astype(f32)
    kdec = np.exp(lg_cols[None, :] * (BLK - 1.0 - idx)[:, None]).astype(f32)
    head_of = np.arange(RET_WIDTH) // RET_KEY_DIM
    bd = (head_of[:, None] == head_of[None, :]).astype(f32)
    gdec = (np.exp(lg_cols * f32(BLK))[:, None] * bd).astype(f32)
    return tuple(jnp.asarray(a) for a in (cos, sin, dmat, qdec, kdec, gdec, bd))


def _outproj_kernel(x_ref, a_ref, b_ref, c_ref, w_ref, g_ref, *rest, with_router):
    if with_router:
        rw_ref, xo_ref, h_ref, rinfo_ref = rest
    else:
        xo_ref, h_ref = rest
    acc = x_ref[...]
    acc = acc + jnp.dot(a_ref[...], w_ref[0:DA_WIDTH, :], preferred_element_type=F32)
    acc = acc + jnp.dot(b_ref[...], w_ref[DA_WIDTH:DA_WIDTH + SSM_WIDTH, :], preferred_element_type=F32)
    acc = acc + jnp.dot(c_ref[...], w_ref[DA_WIDTH + SSM_WIDTH:, :], preferred_element_type=F32)
    xo_ref[...] = acc
    ms = jnp.mean(acc * acc, axis=-1, keepdims=True)
    hf = acc * lax.rsqrt(ms + EPS) * g_ref[...]
    h = hf.astype(BF16)
    h_ref[...] = hf.astype(h_ref.dtype)
    if with_router:
        logits = jnp.dot(h, rw_ref[...], preferred_element_type=F32)
        lane = lax.broadcasted_iota(jnp.int32, logits.shape, 1).astype(F32)
        logits = jnp.where(lane < N_EXPERTS, logits, NEG_BIG)
        v1 = jnp.max(logits, axis=-1, keepdims=True)
        i1 = jnp.min(jnp.where(logits == v1, lane, float(LANES)), axis=-1, keepdims=True)
        rest_l = jnp.where(lane == i1, NEG_BIG, logits)
        v2 = jnp.max(rest_l, axis=-1, keepdims=True)
        i2 = jnp.min(jnp.where(rest_l == v2, lane, float(LANES)), axis=-1, keepdims=True)
        e2 = jnp.exp(v2 - v1)
        g1 = 1.0 / (1.0 + e2)
        g2 = e2 * g1
        rinfo_ref[...] = (jnp.where(lane == 0.0, g1, 0.0) + jnp.where(lane == 1.0, g2, 0.0)
                          + jnp.where(lane == 2.0, i1, 0.0) + jnp.where(lane == 3.0, i2, 0.0))


def _outproj(x2, oa, ob, oc, w, gain, router_w=None):
    t = x2.shape[0]
    with_router = router_w is not None
    tok = lambda w_: pl.BlockSpec((TM_PROJ, w_), lambda i: (i, 0))
    in_specs = [tok(D_MODEL), tok(DA_WIDTH), tok(SSM_WIDTH), tok(RET_WIDTH),
                pl.BlockSpec((D_MODEL, D_MODEL), lambda i: (0, 0)),
                pl.BlockSpec((1, D_MODEL), lambda i: (0, 0))]
    out_specs = [tok(D_MODEL), tok(D_MODEL)]
    h_dtype = F32 if with_router else BF16
    out_shape = [jax.ShapeDtypeStruct((t, D_MODEL), F32), jax.ShapeDtypeStruct((t, D_MODEL), h_dtype)]
    args = [x2, oa, ob, oc, w, gain]
    if with_router:
        in_specs.append(pl.BlockSpec((D_MODEL, LANES), lambda i: (0, 0)))
        out_specs.append(tok(LANES))
        out_shape.append(jax.ShapeDtypeStruct((t, LANES), F32))
        args.append(router_w)
    return pl.pallas_call(
        functools.partial(_outproj_kernel, with_router=with_router),
        grid=(t // TM_PROJ,),
        in_specs=in_specs,
        out_specs=out_specs,
        out_shape=out_shape,
        compiler_params=_cparams(("parallel",)),
        name="outproj_router" if with_router else "outproj",
    )(*args)


def _ffn_kernel(x_ref, h_ref, wg_ref, wu_ref, wd_ref, o_ref, acc_ref):
    f = pl.program_id(1)

    @pl.when(f == 0)
    def _init():
        acc_ref[...] = x_ref[...]

    h = h_ref[...]
    gate = jnp.dot(h, wg_ref[0], preferred_element_type=F32)
    up = jnp.dot(h, wu_ref[0], preferred_element_type=F32)
    act = (_silu(gate) * up).astype(BF16)
    acc_ref[...] += jnp.dot(act, wd_ref[...], preferred_element_type=F32)

    @pl.when(f == pl.num_programs(1) - 1)
    def _done():
        o_ref[...] = acc_ref[...]


def _dense_ffn(x2, h, wg, wu, wd):
    t = x2.shape[0]
    nf = wg.shape[0]
    return pl.pallas_call(
        _ffn_kernel,
        grid=(t // TM_FFN, nf),
        in_specs=[
            pl.BlockSpec((TM_FFN, D_MODEL), lambda i, f: (i, 0)),
            pl.BlockSpec((TM_FFN, D_MODEL), lambda i, f: (i, 0)),
            pl.BlockSpec((1, D_MODEL, TF_FFN), lambda i, f: (f, 0, 0)),
            pl.BlockSpec((1, D_MODEL, TF_FFN), lambda i, f: (f, 0, 0)),
            pl.BlockSpec((TF_FFN, D_MODEL), lambda i, f: (f, 0)),
        ],
        out_specs=pl.BlockSpec((TM_FFN, D_MODEL), lambda i, f: (i, 0)),
        out_shape=jax.ShapeDtypeStruct((t, D_MODEL), F32),
        scratch_shapes=[pltpu.VMEM((TM_FFN, D_MODEL), F32)],
        compiler_params=_cparams(("parallel", "arbitrary")),
        name="dense_swiglu",
    )(x2, h, wg, wu, wd)


def _issue_row_gather(src_hbm, idx_ref, idx_base, dst, sem, n_rows):
    def body(r, carry):
        tok = idx_ref[idx_base + r]
        pltpu.make_async_copy(src_hbm.at[pl.ds(tok, 1)], dst.at[pl.ds(r, 1)], sem).start()
        return carry
    lax.fori_loop(0, n_rows, body, 0, unroll=8)


def _wait_row_gather(src_hbm, dst, sem, n_rows):
    pltpu.make_async_copy(src_hbm.at[pl.ds(0, n_rows)], dst, sem).wait()


def _moe_ffn_kernel(te_ref, nv_ref, tb_ref, rt_ref, h_hbm, wg_ref, wu_ref, wd_ref, y_ref, hbuf, xb_ref, acc_ref, sem):
    i = pl.program_id(0)
    f = pl.program_id(1)
    n_valid = nv_ref[0]
    valid = i < n_valid
    slot = i % 2

    @pl.when((f == 0) & (i == 0) & valid)
    def _first():
        _issue_row_gather(h_hbm, rt_ref, tb_ref[0], hbuf.at[0], sem.at[0], TR_MOE)

    @pl.when((f == 0) & valid)
    def _arrive():
        _wait_row_gather(h_hbm, hbuf.at[slot], sem.at[slot], TR_MOE)
        xb_ref[...] = hbuf[slot].astype(BF16)

    def compute(prefetch_next):
        if prefetch_next:
            rows_per_step = TR_MOE // (FFN_EXPERT // TF_MOE)
            r0 = f * rows_per_step
            src0 = tb_ref[i + 1] + r0
            for j in range(rows_per_step):
                tok = rt_ref[src0 + j]
                pltpu.make_async_copy(h_hbm.at[pl.ds(tok, 1)], hbuf.at[1 - slot, pl.ds(r0 + j, 1)],
                                      sem.at[1 - slot]).start()
        xb = xb_ref[...]
        gate = jnp.dot(xb, wg_ref[0].astype(BF16), preferred_element_type=F32)
        up = jnp.dot(xb, wu_ref[0].astype(BF16), preferred_element_type=F32)
        act = (_silu(gate) * up).astype(BF16)
        contrib = jnp.dot(act, wd_ref[0].astype(BF16), preferred_element_type=F32)

        @pl.when(f == 0)
        def _():
            acc_ref[...] = contrib

        @pl.when(f > 0)
        def _():
            acc_ref[...] += contrib

    has_next = i + 1 < n_valid

    @pl.when(valid & has_next)
    def _compute_and_prefetch():
        compute(True)

    @pl.when(valid & jnp.logical_not(has_next))
    def _compute_last():
        compute(False)

    @pl.when(f == pl.num_programs(1) - 1)
    def _store():
        y_ref[...] = jnp.where(valid, acc_ref[...], 0.0)


def _moe_ffn(h, tile_expert, n_valid, tile_base, sorted_token, wg, wu, wd):
    n_tiles = tile_expert.shape[0]
    nf = wg.shape[2] // TF_MOE

    def w_in_map(i, f, te, nv, tb, rt):
        return (te[i], 0, jnp.where(i < nv[0], f, nf - 1))

    def w_out_map(i, f, te, nv, tb, rt):
        return (te[i], jnp.where(i < nv[0], f, nf - 1), 0)

    grid_spec = pltpu.PrefetchScalarGridSpec(
        num_scalar_prefetch=4,
        grid=(n_tiles, nf),
        in_specs=[
            pl.BlockSpec(memory_space=pl.ANY),
            pl.BlockSpec((1, D_MODEL, TF_MOE), w_in_map),
            pl.BlockSpec((1, D_MODEL, TF_MOE), w_in_map),
            pl.BlockSpec((1, TF_MOE, D_MODEL), w_out_map),
        ],
        out_specs=pl.BlockSpec((TR_MOE, D_MODEL), lambda i, f, te, nv, tb, rt: (i, 0)),
        scratch_shapes=[
            pltpu.VMEM((2, TR_MOE, D_MODEL), F32),
            pltpu.VMEM((TR_MOE, D_MODEL), BF16),
            pltpu.VMEM((TR_MOE, D_MODEL), F32),
            pltpu.SemaphoreType.DMA((2,)),
        ],
    )
    return pl.pallas_call(
        _moe_ffn_kernel,
        grid_spec=grid_spec,
        out_shape=jax.ShapeDtypeStruct((n_tiles * TR_MOE, D_MODEL), F32),
        compiler_params=_cparams(("arbitrary", "arbitrary")),
        name="moe_expert_ffn",
    )(tile_expert, n_valid, tile_base, sorted_token, h, wg, wu, wd)


def _moe_combine_kernel(tr_ref, x_ref, rinfo_ref, y_hbm, fg_ref, o_ref, ybuf, sem):
    i = pl.program_id(0)
    slot = i % 2
    n_rows = TOP_K * TM_COMB

    @pl.when(i == 0)
    def _first():
        _issue_row_gather(y_hbm, tr_ref, 0, ybuf.at[0], sem.at[0], n_rows)

    _wait_row_gather(y_hbm, ybuf.at[slot], sem.at[slot], n_rows)

    @pl.when(i + 1 < pl.num_programs(0))
    def _prefetch():
        src0 = (i + 1) * n_rows
        dst = ybuf.at[1 - slot]
        for r in range(n_rows):
            pltpu.make_async_copy(y_hbm.at[pl.ds(tr_ref[src0 + r], 1)], dst.at[pl.ds(r, 1)], sem.at[1 - slot]).start()

    rinfo = rinfo_ref[...]
    g1 = jnp.broadcast_to(rinfo[:, 0:1], (TM_COMB, D_MODEL))
    g2 = jnp.broadcast_to(rinfo[:, 1:2], (TM_COMB, D_MODEL))
    y = x_ref[...] + g1 * ybuf[slot, 0:TM_COMB, :] + g2 * ybuf[slot, TM_COMB:n_rows, :]
    ms = jnp.mean(y * y, axis=-1, keepdims=True)
    o_ref[...] = y * lax.rsqrt(ms + EPS) * fg_ref[...]


def _moe_combine(x2, rinfo, y_rows, tok_rows, final_gain):
    t = x2.shape[0]
    grid_spec = pltpu.PrefetchScalarGridSpec(
        num_scalar_prefetch=1,
        grid=(t // TM_COMB,),
        in_specs=[
            pl.BlockSpec((TM_COMB, D_MODEL), lambda i, tr: (i, 0)),
            pl.BlockSpec((TM_COMB, LANES), lambda i, tr: (i, 0)),
            pl.BlockSpec(memory_space=pl.ANY),
            pl.BlockSpec((1, D_MODEL), lambda i, tr: (0, 0)),
        ],
        out_specs=pl.BlockSpec((TM_COMB, D_MODEL), lambda i, tr: (i, 0)),
        scratch_shapes=[
            pltpu.VMEM((2, TOP_K * TM_COMB, D_MODEL), F32),
            pltpu.SemaphoreType.DMA((2,)),
        ],
    )
    return pl.pallas_call(
        _moe_combine_kernel,
        grid_spec=grid_spec,
        out_shape=jax.ShapeDtypeStruct((t, D_MODEL), F32),
        compiler_params=_cparams(("arbitrary",)),
        name="moe_combine_norm",
    )(tok_rows, x2, rinfo, y_rows, final_gain)


def _routing_tables(rinfo):
    t = rinfo.shape[0]
    n_assign = TOP_K * t
    n_tiles = -(-n_assign // TR_MOE) + N_EXPERTS
    flat_e = rinfo[:, 2:2 + TOP_K].astype(jnp.int32).reshape(-1)
    onehot = (flat_e[:, None] == jnp.arange(N_EXPERTS)[None, :]).astype(jnp.int32)
    csum = jnp.cumsum(onehot, axis=0)
    rank = jnp.sum((csum - onehot) * onehot, axis=1)
    counts = csum[-1]
    tiles_per_e = (counts + TR_MOE - 1) // TR_MOE
    tile_end = jnp.cumsum(tiles_per_e)
    first_tile = tile_end - tiles_per_e
    dest = (first_tile * TR_MOE)[flat_e] + rank
    order = jnp.sort(flat_e * n_assign + jnp.arange(n_assign, dtype=jnp.int32))
    sorted_token = jnp.concatenate([(order % n_assign) // TOP_K, jnp.zeros((TR_MOE,), jnp.int32)])
    start = jnp.cumsum(counts) - counts
    n_valid = tile_end[-1:].astype(jnp.int32)
    tile_ids = jnp.arange(n_tiles)
    tile_expert = jnp.searchsorted(tile_end, tile_ids, side="right").astype(jnp.int32)
    last_e = jnp.searchsorted(tile_end, n_valid[0] - 1, side="right").astype(jnp.int32)
    tile_expert = jnp.where(tile_ids < n_valid[0], tile_expert, last_e)
    tile_base = start[tile_expert] + (tile_ids - first_tile[tile_expert]) * TR_MOE
    tile_base = jnp.clip(tile_base, 0, n_assign).astype(jnp.int32)
    tok_rows = dest.reshape(t // TM_COMB, TM_COMB, TOP_K).transpose(0, 2, 1).reshape(-1).astype(jnp.int32)
    return tile_expert, n_valid, tile_base, sorted_token, tok_rows


def _permute_w_in(w):
    sizes = (256, 256, 256, 512, 768, 8, 256, 256, 256, 256)
    offs = np.concatenate([[0], np.cumsum(sizes)])
    part = lambda n: w[:, offs[n]:offs[n + 1]]
    dt_pad = jnp.zeros((w.shape[0], LANES - SSM_HEADS), w.dtype)
    cols = [part(0) * (DA_HEAD_DIM ** -0.5 * LOG2E), part(2), part(1), part(3), part(4),
            part(6), part(7), part(8), part(9), part(5), dt_pad]
    return jnp.concatenate(cols, axis=1).astype(BF16)


def _column_tiles(w, tile):
    *lead, k, n = w.shape
    return jnp.moveaxis(w.reshape(*lead, k, n // tile, tile), -2, -3)


def _pad_lanes(v, fill=0.0):
    return jnp.concatenate([v, jnp.full((LANES - v.shape[0],), fill, v.dtype)])[None, :]


def kernel(x, w_in, w_out, attn_norm, ffn_norm, final_norm, rel_bias, lambda_q1, lambda_k1, lambda_q2,
           lambda_k2, da_head_norm, conv_w, conv_b, dt_bias, a_log, d_skip, ssm_norm, ret_head_norm,
           w_gate, w_up, w_down, router_w, e_gate, e_up, e_down):
    batch, seq, _ = x.shape
    x2 = x.reshape(batch * seq, D_MODEL)
    bias_tiles = _bias_tiles(rel_bias)
    ret_tabs = _retention_tables(seq)

    for layer in range(DEPTH):
        lam_init = 0.8 - 0.6 * math.exp(-0.3 * layer)
        lam = (jnp.exp(jnp.sum(lambda_q1[layer] * lambda_k1[layer]))
               - jnp.exp(jnp.sum(lambda_q2[layer] * lambda_k2[layer])) + lam_init).astype(F32)
        lam = jnp.full((1, ATT_T), lam, F32)

        qkv, qvt, pf = _inproj(x2, attn_norm[layer][None, :], _permute_w_in(w_in[layer]))

        da_gain = (jnp.tile(da_head_norm[layer], DA_HEADS) * (1.0 - lam_init))[:, None]
        out_a = _attention(qkv, qvt, bias_tiles, da_gain, lam, batch, seq)

        cw, cb = conv_w[layer], conv_b[layer]
        out_b = _ssd(pf, cw[:, :SSM_WIDTH], cw[:, SSM_WIDTH:], cb[None, :SSM_WIDTH], cb[None, SSM_WIDTH:],
                     _pad_lanes(dt_bias[layer]), _pad_lanes(a_log[layer]),
                     jnp.repeat(d_skip[layer], SSM_HEAD_DIM)[None, :], ssm_norm[layer][None, :], batch, seq)

        ret_gain = jnp.tile(ret_head_norm[layer], RET_HEADS)[None, :]
        out_c = _retention(pf, *ret_tabs, ret_gain, batch, seq)

        w_o = w_out[layer].astype(BF16)
        if layer % 2 == 0:
            i = layer // 2
            x2, h = _outproj(x2, out_a, out_b, out_c, w_o, ffn_norm[layer][None, :])
            pad = FFN_PAD - FFN_DENSE
            wg = _column_tiles(jnp.pad(w_gate[i], ((0, 0), (0, pad))).astype(BF16), TF_FFN)
            wu = _column_tiles(jnp.pad(w_up[i], ((0, 0), (0, pad))).astype(BF16), TF_FFN)
            wd = jnp.pad(w_down[i], ((0, pad), (0, 0))).astype(BF16)
            x2 = _dense_ffn(x2, h, wg, wu, wd)
        else:
            i = layer // 2
            rw = jnp.pad(router_w[i], ((0, 0), (0, LANES - N_EXPERTS))).astype(BF16)
            x2, h, rinfo = _outproj(x2, out_a, out_b, out_c, w_o, ffn_norm[layer][None, :], rw)
            tile_expert, n_valid, tile_base, sorted_token, tok_rows = _routing_tables(rinfo)
            y_rows = _moe_ffn(h, tile_expert, n_valid, tile_base, sorted_token,
                              e_gate[i], e_up[i], e_down[i])
            x2 = _moe_combine(x2, rinfo, y_rows, tok_rows, final_norm[None, :])
    return x2.reshape(batch, seq, D_MODEL)
```

```python
import functools
import math

import jax
import jax.numpy as jnp
import numpy as np
from jax import lax
from jax.experimental import pallas as pl
from jax.experimental.pallas import tpu as pltpu

F32 = jnp.float32
BF16 = jnp.bfloat16

D_MODEL = 1024
DEPTH = 2
CHUNK = 64
EPS = 1e-6
DA_HEADS = 4
DA_HEAD_DIM = 32
DA_V_DIM = 64
DA_WIDTH = 256
SSM_HEADS = 8
SSM_HEAD_DIM = 64
SSM_WIDTH = 512
SSM_STATE = 64
SSM_GROUPS = 2
SSM_CONV = 4
RET_HEADS = 4
RET_KEY_DIM = 64
RET_WIDTH = 256
ROPE_BASE = 10000.0
REL_BUCKETS = 32
REL_MAX_DIST = 128
FFN_DENSE = 2752
N_EXPERTS = 8
FFN_EXPERT = 3584

LANES = 128
VMEM_LIMIT = 48 * 1024 * 1024
NEG_BIG = -1e30

TM_PROJ = 512
ATT_T = 512
LOG2E = math.log2(math.e)
BLK = 256
FFN_PAD = 2816
TM_FFN = 512
TF_FFN = 1408
TF_MOE = 512
TR_MOE = 896
TM_COMB = 256
TOP_K = 2

C_Z, C_XS, C_BC, C_RQ, C_RK, C_RV, C_RG, C_DT = 0, 512, 1024, 1280, 1536, 1792, 2048, 2304
PF_W = 2432
PF_CHUNK = 640
QKV_W = 768


def _cparams(sem):
    return pltpu.CompilerParams(dimension_semantics=sem, vmem_limit_bytes=VMEM_LIMIT)


def _group_mask(shape, axis, group, idx):
    io = lax.broadcasted_iota(jnp.int32, shape, axis)
    return (io >= idx * group) & (io < (idx + 1) * group)


def _lane_tile(x, n, axis=1):
    return jnp.concatenate([x] * n, axis=axis)


def _silu(x):
    h = 0.5 * x
    return h + h * jnp.tanh(h)


def _inproj_kernel(x_ref, g_ref, w_ref, qkv_ref, qvt_ref, pf_ref):
    x = x_ref[...]
    ms = jnp.mean(x * x, axis=-1, keepdims=True)
    h = (x * lax.rsqrt(ms + EPS) * g_ref[...]).astype(BF16)
    qkv = jnp.dot(h, w_ref[:, :QKV_W], preferred_element_type=F32)
    qkv_ref[...] = qkv.astype(BF16)
    qvt_ref[...] = qkv[:, :2 * DA_WIDTH].T.astype(BF16)
    for lo in range(0, PF_W, PF_CHUNK):
        hi = min(lo + PF_CHUNK, PF_W)
        pf_ref[:, lo:hi] = jnp.dot(h, w_ref[:, QKV_W + lo:QKV_W + hi], preferred_element_type=F32)


def _inproj(x2, gain, w):
    t = x2.shape[0]
    return pl.pallas_call(
        _inproj_kernel,
        grid=(t // TM_PROJ,),
        in_specs=[
            pl.BlockSpec((TM_PROJ, D_MODEL), lambda i: (i, 0)),
            pl.BlockSpec((1, D_MODEL), lambda i: (0, 0)),
            pl.BlockSpec((D_MODEL, QKV_W + PF_W), lambda i: (0, 0)),
        ],
        out_specs=[
            pl.BlockSpec((TM_PROJ, QKV_W), lambda i: (i, 0)),
            pl.BlockSpec((2 * DA_WIDTH, TM_PROJ), lambda i: (0, i)),
            pl.BlockSpec((TM_PROJ, PF_W), lambda i: (i, 0)),
        ],
        out_shape=[
            jax.ShapeDtypeStruct((t, QKV_W), BF16),
            jax.ShapeDtypeStruct((2 * DA_WIDTH, t), BF16),
            jax.ShapeDtypeStruct((t, PF_W), F32),
        ],
        compiler_params=_cparams(("parallel",)),
        name="inproj",
    )(x2, gain, w)


def _attn_kernel(qi_ref, kj_ref, qt_ref, k_ref, vt_ref, bias_ref, gain_ref, lam_ref, o_ref,
                 qm_ref, m_ref, l_ref, acc_ref, s_ref, e_ref, al_ref, mx_ref):
    p = pl.program_id(1)
    qi = qi_ref[p]
    kj = kj_ref[p]
    n_ht = 2 * DA_HEADS

    @pl.when(kj == 0)
    def _init():
        qt = qt_ref[...]
        for ht in range(n_ht):
            qm_ref[ht] = jnp.where(_group_mask(qt.shape, 0, DA_HEAD_DIM, ht), qt, jnp.zeros_like(qt))
        m_ref[...] = jnp.full(m_ref.shape, NEG_BIG, F32)
        l_ref[...] = jnp.zeros(l_ref.shape, F32)
        acc_ref[...] = jnp.zeros(acc_ref.shape, F32)

    def step(with_bias):
        k = k_ref[...]
        def scores(a, buf):
            s = jnp.dot(k, qm_ref[a], preferred_element_type=F32)
            if with_bias:
                s = s + bias_ref[0, a // 2]
            s_ref[buf] = s
            mx_ref[buf] = jnp.max(s, axis=0, keepdims=True)

        def softmax(b, buf):
            for c0 in range(0, ATT_T, LANES):
                cols = slice(c0, c0 + LANES)
                m_prev = m_ref[b, :, cols]
                m_new = jnp.maximum(m_prev, mx_ref[buf, :, cols])
                alpha = jnp.exp2(m_prev - m_new)
                e = jnp.exp2(s_ref[buf, :, cols] - m_new)
                l_ref[b, :, cols] = alpha * l_ref[b, :, cols] + jnp.sum(e, axis=0, keepdims=True)
                m_ref[b, :, cols] = m_new
                al_ref[buf, :, cols] = alpha
                e_ref[buf, :, cols] = e.astype(BF16)

        def values(c, buf):
            r0 = (c // 4) * LANES
            if not isinstance(r0, int):
                r0 = pl.multiple_of(r0, LANES)
            pv = jnp.dot(vt_ref[pl.ds(r0, LANES), :], e_ref[buf], preferred_element_type=F32)
            acc_ref[c] = acc_ref[c] * al_ref[buf] + pv

        scores(0, 0)
        scores(1, 1)
        softmax(0, 0)

        def trip(u, carry):
            t = 2 * u + 1
            scores(t + 1, 0)
            softmax(t, 1)
            values(t - 1, 0)
            scores(t + 2, 1)
            softmax(t + 1, 0)
            values(t, 1)
            return carry

        for u in range((n_ht - 2) // 2):
            trip(u, 0)
        softmax(n_ht - 1, 1)
        values(n_ht - 2, 0)
        values(n_ht - 1, 1)

    @pl.when(kj >= qi - 1)
    def _near():
        step(True)

    @pl.when(kj < qi - 1)
    def _far():
        step(False)

    @pl.when(kj == qi)
    def _finish():
        lam = lam_ref[...]
        row_lo = lax.broadcasted_iota(jnp.int32, (LANES, ATT_T), 0) < DA_V_DIM
        for pair in range(DA_HEADS // 2):
            halves = []
            for sub in range(2):
                h = 2 * pair + sub
                o = acc_ref[2 * h] * (1.0 / l_ref[2 * h]) - lam * (acc_ref[2 * h + 1] * (1.0 / l_ref[2 * h + 1]))
                own = row_lo if sub == 0 else jnp.logical_not(row_lo)
                ms = jnp.sum(jnp.where(own, o * o, 0.0), axis=0, keepdims=True) * (1.0 / DA_V_DIM)
                halves.append(o * lax.rsqrt(ms + EPS))
            r0 = pair * LANES
            blk = jnp.where(row_lo, halves[0], halves[1]) * gain_ref[r0:r0 + LANES, :]
            o_ref[:, r0:r0 + LANES] = blk.T.astype(BF16)


def _attention(qkv, qvt, bias_tiles, gain, lam, batch, seq):
    nq = seq // ATT_T
    qi_l, kj_l = [], []
    for qi in range(nq):
        for kj in range(qi + 1):
            qi_l.append(qi)
            kj_l.append(kj)
    qi_tbl = jnp.asarray(np.array(qi_l, np.int32))
    kj_tbl = jnp.asarray(np.array(kj_l, np.int32))
    n_pairs = len(qi_l)
    n_ht = 2 * DA_HEADS

    grid_spec = pltpu.PrefetchScalarGridSpec(
        num_scalar_prefetch=2,
        grid=(batch, n_pairs),
        in_specs=[
            pl.BlockSpec((DA_WIDTH, ATT_T), lambda b, p, qi, kj: (0, b * nq + qi[p])),
            pl.BlockSpec((ATT_T, DA_WIDTH), lambda b, p, qi, kj: (b * nq + kj[p], 2)),
            pl.BlockSpec((DA_WIDTH, ATT_T), lambda b, p, qi, kj: (1, b * nq + kj[p])),
            pl.BlockSpec((1, DA_HEADS, ATT_T, ATT_T),
                         lambda b, p, qi, kj: (jnp.minimum(qi[p] - kj[p], 1), 0, 0, 0)),
            pl.BlockSpec((DA_WIDTH, 1), lambda b, p, qi, kj: (0, 0)),
            pl.BlockSpec((1, ATT_T), lambda b, p, qi, kj: (0, 0)),
        ],
        out_specs=pl.BlockSpec((ATT_T, DA_WIDTH), lambda b, p, qi, kj: (b * nq + qi[p], 0)),
        scratch_shapes=[
            pltpu.VMEM((n_ht, DA_WIDTH, ATT_T), BF16),
            pltpu.VMEM((n_ht, 1, ATT_T), F32),
            pltpu.VMEM((n_ht, 1, ATT_T), F32),
            pltpu.VMEM((n_ht, LANES, ATT_T), F32),
            pltpu.VMEM((2, ATT_T, ATT_T), F32),
            pltpu.VMEM((2, ATT_T, ATT_T), BF16),
            pltpu.VMEM((2, 1, ATT_T), F32),
            pltpu.VMEM((2, 1, ATT_T), F32),
        ],
    )
    return pl.pallas_call(
        _attn_kernel,
        grid_spec=grid_spec,
        out_shape=jax.ShapeDtypeStruct((batch * seq, DA_WIDTH), BF16),
        compiler_params=_cparams(("parallel", "arbitrary")),
        name="diff_attention",
    )(qi_tbl, kj_tbl, qvt, qkv, qvt, bias_tiles, gain, lam)


def _t5_bucket(rel):
    half = REL_BUCKETS // 2
    max_exact = half // 2
    ret = jnp.where(rel > 0, half, 0)
    n = jnp.abs(rel)
    nf = jnp.maximum(n, 1).astype(F32)
    large = max_exact + (jnp.log(nf / max_exact) / math.log(REL_MAX_DIST / max_exact)
                         * (half - max_exact)).astype(jnp.int32)
    large = jnp.minimum(large, half - 1)
    return ret + jnp.where(n < max_exact, n, large)


def _bias_tiles(rel_bias):
    n = ATT_T
    i = jnp.arange(n)[None, :]
    j = jnp.arange(n)[:, None]
    far = rel_bias[_t5_bucket(jnp.int32(-2 * n))].astype(F32)
    tiles = []
    for off in range(2):
        rel = (n - 1 - off * n) - jnp.arange(2 * n)
        bucket = _t5_bucket(rel)
        g = jnp.zeros((DA_HEADS, 2 * n), F32)
        for c in range(REL_BUCKETS):
            g = jnp.where((bucket == c)[None], rel_bias[c].astype(F32)[:, None], g)
        g = (g - far[:, None]) * LOG2E
        skew = jnp.broadcast_to(g[:, None, :], (DA_HEADS, n, 2 * n)).reshape(DA_HEADS, 2 * n * n)
        skew = skew[:, :n * (2 * n - 1)].reshape(DA_HEADS, n, 2 * n - 1)
        b = skew[:, :, n - 1:]
        if off == 0:
            allowed = (j // CHUNK) <= (i // CHUNK)
            b = jnp.where(allowed[None], b, NEG_BIG)
        tiles.append(b)
    return jnp.stack(tiles)


def _cumsum_rows(x):
    n = x.shape[0]
    row = lax.broadcasted_iota(jnp.int32, x.shape, 0)
    sh = 1
    while sh < n:
        x = x + jnp.where(row >= sh, pltpu.roll(x, sh, axis=0), 0.0)
        sh *= 2
    return x


def _expand_heads(v, width, group):
    out = jnp.zeros((v.shape[0], width), F32)
    for h in range(width // group):
        col = jnp.broadcast_to(v[:, h:h + 1], (v.shape[0], width))
        out = jnp.where(_group_mask(out.shape, 1, group, h), col, out)
    return out


def _ssd_kernel(z_ref, xs_ref, bc_ref, dt_ref, cwx_ref, cwb_ref, cbx_ref, cbb_ref, dtb_ref, alog_ref,
                dskip_ref, ng_ref, o_ref, extx_ref, extb_ref, st_ref):
    j = pl.program_id(1)

    @pl.when(j == 0)
    def _reset():
        extx_ref[0:8, :] = jnp.zeros((8, SSM_WIDTH), F32)
        extb_ref[0:8, :] = jnp.zeros((8, 2 * LANES), F32)
        st_ref[...] = jnp.zeros(st_ref.shape, F32)

    extx_ref[8:8 + BLK, :] = xs_ref[...]
    extb_ref[8:8 + BLK, :] = bc_ref[...]
    yx = jnp.broadcast_to(cbx_ref[...], (BLK, SSM_WIDTH))
    yb = jnp.broadcast_to(cbb_ref[...], (BLK, 2 * LANES))
    for w in range(SSM_CONV):
        lo = 8 - (SSM_CONV - 1) + w
        yx = yx + extx_ref[lo:lo + BLK, :] * cwx_ref[w:w + 1, :]
        yb = yb + extb_ref[lo:lo + BLK, :] * cwb_ref[w:w + 1, :]
    tail_x = extx_ref[BLK:BLK + 8, :]
    tail_b = extb_ref[BLK:BLK + 8, :]
    extx_ref[0:8, :] = tail_x
    extb_ref[0:8, :] = tail_b
    xs = _silu(yx)
    bc = _silu(yb)
    bm = bc[:, :LANES]
    cm = bc[:, LANES:]

    dt = jax.nn.softplus(dt_ref[...] + dtb_ref[...])
    a = -jnp.exp(alog_ref[...]) * dt
    a_cs = _cumsum_rows(a)
    a_cs_t = a_cs.T
    a_last = a_cs[BLK - 1:BLK, :]

    dt_full = _expand_heads(dt, SSM_WIDTH, SSM_HEAD_DIM)
    ea_full = _expand_heads(jnp.exp(a_cs), SSM_WIDTH, SSM_HEAD_DIM)
    dec_full = _expand_heads(jnp.exp(a_last - a_cs), SSM_WIDTH, SSM_HEAD_DIM)
    cdec_full = _expand_heads(jnp.exp(a_last), SSM_WIDTH, SSM_HEAD_DIM)

    xdt = xs * dt_full
    xdt_b = xdt.astype(BF16)
    cm_b = cm.astype(BF16)
    bm_b = bm.astype(BF16)

    st = st_ref[...]
    y = jnp.dot(cm_b, st.astype(BF16), preferred_element_type=F32) * ea_full + xs * dskip_ref[...]

    row = lax.broadcasted_iota(jnp.int32, (BLK, BLK), 0)
    colio = lax.broadcasted_iota(jnp.int32, (BLK, BLK), 1)
    causal = colio <= row
    rep = SSM_HEADS // SSM_GROUPS
    lane_lo = lax.broadcasted_iota(jnp.int32, (BLK, LANES), 1) < SSM_HEAD_DIM
    pieces = []
    for g in range(SSM_GROUPS):
        cg = jnp.where(_group_mask(cm_b.shape, 1, SSM_STATE, g), cm_b, jnp.zeros_like(cm_b))
        cb = lax.dot_general(cg, bm_b, (((1,), (1,)), ((), ())), preferred_element_type=F32)
        for pair in range(rep // 2):
            halves = []
            for sub in range(2):
                h = g * rep + pair * 2 + sub
                seg = jnp.broadcast_to(a_cs[:, h:h + 1], (BLK, BLK)) - a_cs_t[h:h + 1, :]
                lmat = jnp.exp(jnp.where(causal, seg, NEG_BIG))
                mh = (cb * lmat).astype(BF16)
                c0 = (h // 2) * LANES
                halves.append(jnp.dot(mh, xdt_b[:, c0:c0 + LANES], preferred_element_type=F32))
            pieces.append(jnp.where(lane_lo, halves[0], halves[1]))
    y = y + jnp.concatenate(pieces, axis=1)

    upd = jnp.dot(bm.T.astype(BF16), (xdt * dec_full).astype(BF16), preferred_element_type=F32)
    r_io = lax.broadcasted_iota(jnp.int32, upd.shape, 0) // SSM_STATE
    c_io = lax.broadcasted_iota(jnp.int32, upd.shape, 1) // (SSM_HEAD_DIM * rep)
    st_ref[...] = st * cdec_full + jnp.where(r_io == c_io, upd, 0.0)

    y = y * _silu(z_ref[...])
    gw = SSM_WIDTH // SSM_GROUPS
    for g in range(SSM_GROUPS):
        yg = y[:, g * gw:(g + 1) * gw]
        ms = jnp.mean(yg * yg, axis=-1, keepdims=True)
        o_ref[:, g * gw:(g + 1) * gw] = (yg * lax.rsqrt(ms + EPS) * ng_ref[:, g * gw:(g + 1) * gw]).astype(BF16)


def _ssd(pf, cwx, cwb, cbx, cbb, dtb, alog, dskip, ng, batch, seq):
    nb = seq // BLK
    row = lambda w, c: pl.BlockSpec((BLK, w), lambda b, j: (b * nb + j, c))
    const = lambda r, w: pl.BlockSpec((r, w), lambda b, j: (0, 0))
    return pl.pallas_call(
        _ssd_kernel,
        grid=(batch, nb),
        in_specs=[
            row(SSM_WIDTH, C_Z // SSM_WIDTH), row(SSM_WIDTH, C_XS // SSM_WIDTH), row(2 * LANES, C_BC // (2 * LANES)),
            row(LANES, C_DT // LANES),
            const(SSM_CONV, SSM_WIDTH), const(SSM_CONV, 2 * LANES), const(1, SSM_WIDTH), const(1, 2 * LANES),
            const(1, LANES), const(1, LANES), const(1, SSM_WIDTH), const(1, SSM_WIDTH),
        ],
        out_specs=pl.BlockSpec((BLK, SSM_WIDTH), lambda b, j: (b * nb + j, 0)),
        out_shape=jax.ShapeDtypeStruct((batch * seq, SSM_WIDTH), BF16),
        scratch_shapes=[
            pltpu.VMEM((BLK + 8, SSM_WIDTH), F32),
            pltpu.VMEM((BLK + 8, 2 * LANES), F32),
            pltpu.VMEM((SSM_GROUPS * SSM_STATE, SSM_WIDTH), F32),
        ],
        compiler_params=_cparams(("parallel", "arbitrary")),
        name="ssd_mixer",
    )(pf, pf, pf, pf, cwx, cwb, cbx, cbb, dtb, alog, dskip, ng)


def _rotary(u, cos, sin_signed):
    first = (lax.broadcasted_iota(jnp.int32, u.shape, 1) % RET_KEY_DIM) < (RET_KEY_DIM // 2)
    half = RET_KEY_DIM // 2
    swapped = jnp.where(first, pltpu.roll(u, u.shape[1] - half, axis=1), pltpu.roll(u, half, axis=1))
    return u * cos + swapped * sin_signed


def _ret_kernel(q_ref, k_ref, v_ref, g_ref, cos_ref, sin_ref, dmat_ref, qdec_ref, kdec_ref, gdec_ref,
                bd_ref, gain_ref, o_ref, st_ref):
    j = pl.program_id(1)

    @pl.when(j == 0)
    def _reset():
        st_ref[...] = jnp.zeros(st_ref.shape, F32)

    cos = _lane_tile(cos_ref[...], RET_WIDTH // LANES)
    sin = _lane_tile(sin_ref[...], RET_WIDTH // LANES)
    q = _rotary(q_ref[...], cos, sin)
    k = _rotary(k_ref[...], cos, sin) * (RET_KEY_DIM ** -0.5)
    v_b = v_ref[...].astype(BF16)
    q_b = q.astype(BF16)
    k_b = k.astype(BF16)

    st = st_ref[...]
    o = jnp.dot((q * qdec_ref[...]).astype(BF16), st.astype(BF16), preferred_element_type=F32)
    for h in range(RET_HEADS):
        cm = _group_mask(q_b.shape, 1, RET_KEY_DIM, h)
        qh = jnp.where(cm, q_b, jnp.zeros_like(q_b))
        s = lax.dot_general(qh, k_b, (((1,), (1,)), ((), ())), preferred_element_type=F32) * dmat_ref[h]
        oh = jnp.dot(s.astype(BF16), v_b, preferred_element_type=F32)
        o = o + jnp.where(cm, oh, 0.0)

    upd = jnp.dot((k * kdec_ref[...]).T.astype(BF16), v_b, preferred_element_type=F32)
    st_ref[...] = st * gdec_ref[...] + upd * bd_ref[...]

    out = jnp.zeros(o.shape, F32)
    for h in range(RET_HEADS):
        cm = _group_mask(o.shape, 1, RET_KEY_DIM, h)
        ms = jnp.sum(jnp.where(cm, o * o, 0.0), axis=-1, keepdims=True) * (1.0 / RET_KEY_DIM)
        out = out + jnp.where(cm, o * lax.rsqrt(ms + EPS), 0.0)
    o_ref[...] = (_silu(g_ref[...]) * (out * gain_ref[...])).astype(BF16)


def _retention(pf, cos, sin, dmat, qdec, kdec, gdec, bd, gain, batch, seq):
    nb = seq // BLK
    row = lambda c: pl.BlockSpec((BLK, RET_WIDTH), lambda b, j: (b * nb + j, c))
    const2 = pl.BlockSpec((BLK, RET_WIDTH), lambda b, j: (0, 0))
    return pl.pallas_call(
        _ret_kernel,
        grid=(batch, nb),
        in_specs=[
            row(C_RQ // RET_WIDTH), row(C_RK // RET_WIDTH), row(C_RV // RET_WIDTH), row(C_RG // RET_WIDTH),
            pl.BlockSpec((BLK, LANES), lambda b, j: (j, 0)),
            pl.BlockSpec((BLK, LANES), lambda b, j: (j, 0)),
            pl.BlockSpec((RET_HEADS, BLK, BLK), lambda b, j: (0, 0, 0)),
            const2, const2, const2, const2,
            pl.BlockSpec((1, RET_WIDTH), lambda b, j: (0, 0)),
        ],
        out_specs=pl.BlockSpec((BLK, RET_WIDTH), lambda b, j: (b * nb + j, 0)),
        out_shape=jax.ShapeDtypeStruct((batch * seq, RET_WIDTH), BF16),
        scratch_shapes=[pltpu.VMEM((RET_WIDTH, RET_WIDTH), F32)],
        compiler_params=_cparams(("parallel", "arbitrary")),
        name="retention",
    )(pf, pf, pf, pf, cos, sin, dmat, qdec, kdec, gdec, bd, gain)


def _retention_tables(seq):
    f32 = np.float32
    inv = (f32(1.0) / (f32(ROPE_BASE) ** (np.arange(0, RET_KEY_DIM, 2, dtype=f32) / f32(RET_KEY_DIM)))).astype(f32)
    ang = (np.arange(seq, dtype=f32)[:, None] * inv[None, :]).astype(f32)
    cos_h = np.concatenate([np.cos(ang), np.cos(ang)], axis=-1)
    sin_h = np.concatenate([-np.sin(ang), np.sin(ang)], axis=-1)
    cos = np.tile(cos_h, (1, 2)).astype(f32)
    sin = np.tile(sin_h, (1, 2)).astype(f32)
    log_gamma = np.log1p(-np.power(f32(2.0), f32(-5.0) - np.arange(RET_HEADS, dtype=f32))).astype(f32)
    idx = np.arange(BLK, dtype=f32)
    ii = np.arange(BLK)
    same_or_earlier_chunk = (ii[None, :] // CHUNK) <= (ii[:, None] // CHUNK)
    dmat = np.exp(log_gamma[:, None, None] * np.abs(idx[:, None] - idx[None, :]))
    dmat = np.where(same_or_earlier_chunk[None], dmat, 0.0).astype(f32)
    lg_cols = np.repeat(log_gamma, RET_KEY_DIM)
    qdec = np.exp(lg_cols[None, :] * (idx + 1.0)[:, None]).astype(f32)
    kdec = np.exp(lg_cols[None, :] * (BLK - 1.0 - idx)[:, None]).astype(f32)
    head_of = np.arange(RET_WIDTH) // RET_KEY_DIM
    bd = (head_of[:, None] == head_of[None, :]).astype(f32)
    gdec = (np.exp(lg_cols * f32(BLK))[:, None] * bd).astype(f32)
    return tuple(jnp.asarray(a) for a in (cos, sin, dmat, qdec, kdec, gdec, bd))


def _outproj_kernel(x_ref, a_ref, b_ref, c_ref, w_ref, g_ref, *rest, with_router):
    if with_router:
        rw_ref, xo_ref, h_ref, rinfo_ref = rest
    else:
        xo_ref, h_ref = rest
    acc = x_ref[...]
    acc = acc + jnp.dot(a_ref[...], w_ref[0:DA_WIDTH, :], preferred_element_type=F32)
    acc = acc + jnp.dot(b_ref[...], w_ref[DA_WIDTH:DA_WIDTH + SSM_WIDTH, :], preferred_element_type=F32)
    acc = acc + jnp.dot(c_ref[...], w_ref[DA_WIDTH + SSM_WIDTH:, :], preferred_element_type=F32)
    xo_ref[...] = acc
    ms = jnp.mean(acc * acc, axis=-1, keepdims=True)
    hf = acc * lax.rsqrt(ms + EPS) * g_ref[...]
    h = hf.astype(BF16)
    h_ref[...] = hf.astype(h_ref.dtype)
    if with_router:
        logits = jnp.dot(h, rw_ref[...], preferred_element_type=F32)
        lane = lax.broadcasted_iota(jnp.int32, logits.shape, 1).astype(F32)
        logits = jnp.where(lane < N_EXPERTS, logits, NEG_BIG)
        v1 = jnp.max(logits, axis=-1, keepdims=True)
        i1 = jnp.min(jnp.where(logits == v1, lane, float(LANES)), axis=-1, keepdims=True)
        rest_l = jnp.where(lane == i1, NEG_BIG, logits)
        v2 = jnp.max(rest_l, axis=-1, keepdims=True)
        i2 = jnp.min(jnp.where(rest_l == v2, lane, float(LANES)), axis=-1, keepdims=True)
        e2 = jnp.exp(v2 - v1)
        g1 = 1.0 / (1.0 + e2)
        g2 = e2 * g1
        rinfo_ref[...] = (jnp.where(lane == 0.0, g1, 0.0) + jnp.where(lane == 1.0, g2, 0.0)
                          + jnp.where(lane == 2.0, i1, 0.0) + jnp.where(lane == 3.0, i2, 0.0))


def _outproj(x2, oa, ob, oc, w, gain, router_w=None):
    t = x2.shape[0]
    with_router = router_w is not None
    tok = lambda w_: pl.BlockSpec((TM_PROJ, w_), lambda i: (i, 0))
    in_specs = [tok(D_MODEL), tok(DA_WIDTH), tok(SSM_WIDTH), tok(RET_WIDTH),
                pl.BlockSpec((D_MODEL, D_MODEL), lambda i: (0, 0)),
                pl.BlockSpec((1, D_MODEL), lambda i: (0, 0))]
    out_specs = [tok(D_MODEL), tok(D_MODEL)]
    h_dtype = F32 if with_router else BF16
    out_shape = [jax.ShapeDtypeStruct((t, D_MODEL), F32), jax.ShapeDtypeStruct((t, D_MODEL), h_dtype)]
    args = [x2, oa, ob, oc, w, gain]
    if with_router:
        in_specs.append(pl.BlockSpec((D_MODEL, LANES), lambda i: (0, 0)))
        out_specs.append(tok(LANES))
        out_shape.append(jax.ShapeDtypeStruct((t, LANES), F32))
        args.append(router_w)
    return pl.pallas_call(
        functools.partial(_outproj_kernel, with_router=with_router),
        grid=(t // TM_PROJ,),
        in_specs=in_specs,
        out_specs=out_specs,
        out_shape=out_shape,
        compiler_params=_cparams(("parallel",)),
        name="outproj_router" if with_router else "outproj",
    )(*args)


def _ffn_kernel(x_ref, h_ref, wg_ref, wu_ref, wd_ref, o_ref, acc_ref):
    f = pl.program_id(1)

    @pl.when(f == 0)
    def _init():
        acc_ref[...] = x_ref[...]

    h = h_ref[...]
    gate = jnp.dot(h, wg_ref[0], preferred_element_type=F32)
    up = jnp.dot(h, wu_ref[0], preferred_element_type=F32)
    act = (_silu(gate) * up).astype(BF16)
    acc_ref[...] += jnp.dot(act, wd_ref[...], preferred_element_type=F32)

    @pl.when(f == pl.num_programs(1) - 1)
    def _done():
        o_ref[...] = acc_ref[...]


def _dense_ffn(x2, h, wg, wu, wd):
    t = x2.shape[0]
    nf = wg.shape[0]
    return pl.pallas_call(
        _ffn_kernel,
        grid=(t // TM_FFN, nf),
        in_specs=[
            pl.BlockSpec((TM_FFN, D_MODEL), lambda i, f: (i, 0)),
            pl.BlockSpec((TM_FFN, D_MODEL), lambda i, f: (i, 0)),
            pl.BlockSpec((1, D_MODEL, TF_FFN), lambda i, f: (f, 0, 0)),
            pl.BlockSpec((1, D_MODEL, TF_FFN), lambda i, f: (f, 0, 0)),
            pl.BlockSpec((TF_FFN, D_MODEL), lambda i, f: (f, 0)),
        ],
        out_specs=pl.BlockSpec((TM_FFN, D_MODEL), lambda i, f: (i, 0)),
        out_shape=jax.ShapeDtypeStruct((t, D_MODEL), F32),
        scratch_shapes=[pltpu.VMEM((TM_FFN, D_MODEL), F32)],
        compiler_params=_cparams(("parallel", "arbitrary")),
        name="dense_swiglu",
    )(x2, h, wg, wu, wd)


def _issue_row_gather(src_hbm, idx_ref, idx_base, dst, sem, n_rows):
    def body(r, carry):
        tok = idx_ref[idx_base + r]
        pltpu.make_async_copy(src_hbm.at[pl.ds(tok, 1)], dst.at[pl.ds(r, 1)], sem).start()
        return carry
    lax.fori_loop(0, n_rows, body, 0, unroll=8)


def _wait_row_gather(src_hbm, dst, sem, n_rows):
    pltpu.make_async_copy(src_hbm.at[pl.ds(0, n_rows)], dst, sem).wait()


def _moe_ffn_kernel(te_ref, nv_ref, tb_ref, rt_ref, h_hbm, wg_ref, wu_ref, wd_ref, y_ref, hbuf, xb_ref, acc_ref, sem):
    i = pl.program_id(0)
    f = pl.program_id(1)
    n_valid = nv_ref[0]
    valid = i < n_valid
    slot = i % 2

    @pl.when((f == 0) & (i == 0) & valid)
    def _first():
        _issue_row_gather(h_hbm, rt_ref, tb_ref[0], hbuf.at[0], sem.at[0], TR_MOE)

    @pl.when((f == 0) & valid)
    def _arrive():
        _wait_row_gather(h_hbm, hbuf.at[slot], sem.at[slot], TR_MOE)
        xb_ref[...] = hbuf[slot].astype(BF16)

    def compute(prefetch_next):
        if prefetch_next:
            rows_per_step = TR_MOE // (FFN_EXPERT // TF_MOE)
            r0 = f * rows_per_step
            src0 = tb_ref[i + 1] + r0
            for j in range(rows_per_step):
                tok = rt_ref[src0 + j]
                pltpu.make_async_copy(h_hbm.at[pl.ds(tok, 1)], hbuf.at[1 - slot, pl.ds(r0 + j, 1)],
                                      sem.at[1 - slot]).start()
        xb = xb_ref[...]
        gate = jnp.dot(xb, wg_ref[0].astype(BF16), preferred_element_type=F32)
        up = jnp.dot(xb, wu_ref[0].astype(BF16), preferred_element_type=F32)
        act = (_silu(gate) * up).astype(BF16)
        contrib = jnp.dot(act, wd_ref[0].astype(BF16), preferred_element_type=F32)

        @pl.when(f == 0)
        def _():
            acc_ref[...] = contrib

        @pl.when(f > 0)
        def _():
            acc_ref[...] += contrib

    has_next = i + 1 < n_valid

    @pl.when(valid & has_next)
    def _compute_and_prefetch():
        compute(True)

    @pl.when(valid & jnp.logical_not(has_next))
    def _compute_last():
        compute(False)

    @pl.when(f == pl.num_programs(1) - 1)
    def _store():
        y_ref[...] = jnp.where(valid, acc_ref[...], 0.0)


def _moe_ffn(h, tile_expert, n_valid, tile_base, sorted_token, wg, wu, wd):
    n_tiles = tile_expert.shape[0]
    nf = wg.shape[2] // TF_MOE

    def w_in_map(i, f, te, nv, tb, rt):
        return (te[i], 0, jnp.where(i < nv[0], f, nf - 1))

    def w_out_map(i, f, te, nv, tb, rt):
        return (te[i], jnp.where(i < nv[0], f, nf - 1), 0)

    grid_spec = pltpu.PrefetchScalarGridSpec(
        num_scalar_prefetch=4,
        grid=(n_tiles, nf),
        in_specs=[
            pl.BlockSpec(memory_space=pl.ANY),
            pl.BlockSpec((1, D_MODEL, TF_MOE), w_in_map),
            pl.BlockSpec((1, D_MODEL, TF_MOE), w_in_map),
            pl.BlockSpec((1, TF_MOE, D_MODEL), w_out_map),
        ],
        out_specs=pl.BlockSpec((TR_MOE, D_MODEL), lambda i, f, te, nv, tb, rt: (i, 0)),
        scratch_shapes=[
            pltpu.VMEM((2, TR_MOE, D_MODEL), F32),
            pltpu.VMEM((TR_MOE, D_MODEL), BF16),
            pltpu.VMEM((TR_MOE, D_MODEL), F32),
            pltpu.SemaphoreType.DMA((2,)),
        ],
    )
    return pl.pallas_call(
        _moe_ffn_kernel,
        grid_spec=grid_spec,
        out_shape=jax.ShapeDtypeStruct((n_tiles * TR_MOE, D_MODEL), F32),
        compiler_params=_cparams(("arbitrary", "arbitrary")),
        name="moe_expert_ffn",
    )(tile_expert, n_valid, tile_base, sorted_token, h, wg, wu, wd)


def _moe_combine_kernel(tr_ref, x_ref, rinfo_ref, y_hbm, fg_ref, o_ref, ybuf, sem):
    i = pl.program_id(0)
    slot = i % 2
    n_rows = TOP_K * TM_COMB

    @pl.when(i == 0)
    def _first():
        _issue_row_gather(y_hbm, tr_ref, 0, ybuf.at[0], sem.at[0], n_rows)

    _wait_row_gather(y_hbm, ybuf.at[slot], sem.at[slot], n_rows)

    @pl.when(i + 1 < pl.num_programs(0))
    def _prefetch():
        src0 = (i + 1) * n_rows
        dst = ybuf.at[1 - slot]
        for r in range(n_rows):
            pltpu.make_async_copy(y_hbm.at[pl.ds(tr_ref[src0 + r], 1)], dst.at[pl.ds(r, 1)], sem.at[1 - slot]).start()

    rinfo = rinfo_ref[...]
    g1 = jnp.broadcast_to(rinfo[:, 0:1], (TM_COMB, D_MODEL))
    g2 = jnp.broadcast_to(rinfo[:, 1:2], (TM_COMB, D_MODEL))
    y = x_ref[...] + g1 * ybuf[slot, 0:TM_COMB, :] + g2 * ybuf[slot, TM_COMB:n_rows, :]
    ms = jnp.mean(y * y, axis=-1, keepdims=True)
    o_ref[...] = y * lax.rsqrt(ms + EPS) * fg_ref[...]


def _moe_combine(x2, rinfo, y_rows, tok_rows, final_gain):
    t = x2.shape[0]
    grid_spec = pltpu.PrefetchScalarGridSpec(
        num_scalar_prefetch=1,
        grid=(t // TM_COMB,),
        in_specs=[
            pl.BlockSpec((TM_COMB, D_MODEL), lambda i, tr: (i, 0)),
            pl.BlockSpec((TM_COMB, LANES), lambda i, tr: (i, 0)),
            pl.BlockSpec(memory_space=pl.ANY),
            pl.BlockSpec((1, D_MODEL), lambda i, tr: (0, 0)),
        ],
        out_specs=pl.BlockSpec((TM_COMB, D_MODEL), lambda i, tr: (i, 0)),
        scratch_shapes=[
            pltpu.VMEM((2, TOP_K * TM_COMB, D_MODEL), F32),
            pltpu.SemaphoreType.DMA((2,)),
        ],
    )
    return pl.pallas_call(
        _moe_combine_kernel,
        grid_spec=grid_spec,
        out_shape=jax.ShapeDtypeStruct((t, D_MODEL), F32),
        compiler_params=_cparams(("arbitrary",)),
        name="moe_combine_norm",
    )(tok_rows, x2, rinfo, y_rows, final_gain)


def _routing_tables(rinfo):
    t = rinfo.shape[0]
    n_assign = TOP_K * t
    n_tiles = -(-n_assign // TR_MOE) + N_EXPERTS
    flat_e = rinfo[:, 2:2 + TOP_K].astype(jnp.int32).reshape(-1)
    onehot = (flat_e[:, None] == jnp.arange(N_EXPERTS)[None, :]).astype(jnp.int32)
    csum = jnp.cumsum(onehot, axis=0)
    rank = jnp.sum((csum - onehot) * onehot, axis=1)
    counts = csum[-1]
    tiles_per_e = (counts + TR_MOE - 1) // TR_MOE
    tile_end = jnp.cumsum(tiles_per_e)
    first_tile = tile_end - tiles_per_e
    dest = (first_tile * TR_MOE)[flat_e] + rank
    order = jnp.sort(flat_e * n_assign + jnp.arange(n_assign, dtype=jnp.int32))
    sorted_token = jnp.concatenate([(order % n_assign) // TOP_K, jnp.zeros((TR_MOE,), jnp.int32)])
    start = jnp.cumsum(counts) - counts
    n_valid = tile_end[-1:].astype(jnp.int32)
    tile_ids = jnp.arange(n_tiles)
    tile_expert = jnp.searchsorted(tile_end, tile_ids, side="right").astype(jnp.int32)
    last_e = jnp.searchsorted(tile_end, n_valid[0] - 1, side="right").astype(jnp.int32)
    tile_expert = jnp.where(tile_ids < n_valid[0], tile_expert, last_e)
    tile_base = start[tile_expert] + (tile_ids - first_tile[tile_expert]) * TR_MOE
    tile_base = jnp.clip(tile_base, 0, n_assign).astype(jnp.int32)
    tok_rows = dest.reshape(t // TM_COMB, TM_COMB, TOP_K).transpose(0, 2, 1).reshape(-1).astype(jnp.int32)
    return tile_expert, n_valid, tile_base, sorted_token, tok_rows


def _permute_w_in(w):
    sizes = (256, 256, 256, 512, 768, 8, 256, 256, 256, 256)
    offs = np.concatenate([[0], np.cumsum(sizes)])
    part = lambda n: w[:, offs[n]:offs[n + 1]]
    dt_pad = jnp.zeros((w.shape[0], LANES - SSM_HEADS), w.dtype)
    cols = [part(0) * (DA_HEAD_DIM ** -0.5 * LOG2E), part(2), part(1), part(3), part(4),
            part(6), part(7), part(8), part(9), part(5), dt_pad]
    return jnp.concatenate(cols, axis=1).astype(BF16)


def _column_tiles(w, tile):
    *lead, k, n = w.shape
    return jnp.moveaxis(w.reshape(*lead, k, n // tile, tile), -2, -3)


def _pad_lanes(v, fill=0.0):
    return jnp.concatenate([v, jnp.full((LANES - v.shape[0],), fill, v.dtype)])[None, :]


def kernel(x, w_in, w_out, attn_norm, ffn_norm, final_norm, rel_bias, lambda_q1, lambda_k1, lambda_q2,
           lambda_k2, da_head_norm, conv_w, conv_b, dt_bias, a_log, d_skip, ssm_norm, ret_head_norm,
           w_gate, w_up, w_down, router_w, e_gate, e_up, e_down):
    batch, seq, _ = x.shape
    x2 = x.reshape(batch * seq, D_MODEL)
    bias_tiles = _bias_tiles(rel_bias)
    ret_tabs = _retention_tables(seq)

    for layer in range(DEPTH):
        lam_init = 0.8 - 0.6 * math.exp(-0.3 * layer)
        lam = (jnp.exp(jnp.sum(lambda_q1[layer] * lambda_k1[layer]))
               - jnp.exp(jnp.sum(lambda_q2[layer] * lambda_k2[layer])) + lam_init).astype(F32)
        lam = jnp.full((1, ATT_T), lam, F32)

        qkv, qvt, pf = _inproj(x2, attn_norm[layer][None, :], _permute_w_in(w_in[layer]))

        da_gain = (jnp.tile(da_head_norm[layer], DA_HEADS) * (1.0 - lam_init))[:, None]
        out_a = _attention(qkv, qvt, bias_tiles, da_gain, lam, batch, seq)

        cw, cb = conv_w[layer], conv_b[layer]
        out_b = _ssd(pf, cw[:, :SSM_WIDTH], cw[:, SSM_WIDTH:], cb[None, :SSM_WIDTH], cb[None, SSM_WIDTH:],
                     _pad_lanes(dt_bias[layer]), _pad_lanes(a_log[layer]),
                     jnp.repeat(d_skip[layer], SSM_HEAD_DIM)[None, :], ssm_norm[layer][None, :], batch, seq)

        ret_gain = jnp.tile(ret_head_norm[layer], RET_HEADS)[None, :]
        out_c = _retention(pf, *ret_tabs, ret_gain, batch, seq)

        w_o = w_out[layer].astype(BF16)
        if layer % 2 == 0:
            i = layer // 2
            x2, h = _outproj(x2, out_a, out_b, out_c, w_o, ffn_norm[layer][None, :])
            pad = FFN_PAD - FFN_DENSE
            wg = _column_tiles(jnp.pad(w_gate[i], ((0, 0), (0, pad))).astype(BF16), TF_FFN)
            wu = _column_tiles(jnp.pad(w_up[i], ((0, 0), (0, pad))).astype(BF16), TF_FFN)
            wd = jnp.pad(w_down[i], ((0, pad), (0, 0))).astype(BF16)
            x2 = _dense_ffn(x2, h, wg, wu, wd)
        else:
            i = layer // 2
            rw = jnp.pad(router_w[i], ((0, 0), (0, LANES - N_EXPERTS))).astype(BF16)
            x2, h, rinfo = _outproj(x2, out_a, out_b, out_c, w_o, ffn_norm[layer][None, :], rw)
            tile_expert, n_valid, tile_base, sorted_token, tok_rows = _routing_tables(rinfo)
            y_rows = _moe_ffn(h, tile_expert, n_valid, tile_base, sorted_token,
                              e_gate[i], e_up[i], e_down[i])
            x2 = _moe_combine(x2, rinfo, y_rows, tok_rows, final_norm[None, :])
    return x2.reshape(batch, seq, D_MODEL)
```

```python
import functools
import math

import jax
import jax.numpy as jnp
import numpy as np
from jax import lax
from jax.experimental import pallas as pl
from jax.experimental.pallas import tpu as pltpu

F32 = jnp.float32
BF16 = jnp.bfloat16

D_MODEL = 1024
DEPTH = 2
CHUNK = 64
EPS = 1e-6
DA_HEADS = 4
DA_HEAD_DIM = 32
DA_V_DIM = 64
DA_WIDTH = 256
SSM_HEADS = 8
SSM_HEAD_DIM = 64
SSM_WIDTH = 512
SSM_STATE = 64
SSM_GROUPS = 2
SSM_CONV = 4
RET_HEADS = 4
RET_KEY_DIM = 64
RET_WIDTH = 256
ROPE_BASE = 10000.0
REL_BUCKETS = 32
REL_MAX_DIST = 128
FFN_DENSE = 2752
N_EXPERTS = 8
FFN_EXPERT = 3584

LANES = 128
VMEM_LIMIT = 48 * 1024 * 1024
NEG_BIG = -1e30

TM_PROJ = 512
ATT_T = 512
LOG2E = math.log2(math.e)
BLK = 256
FFN_PAD = 2816
TM_FFN = 512
TF_FFN = 1408
TF_MOE = 512
TR_MOE = 896
TM_COMB = 256
TOP_K = 2

C_Z, C_XS, C_BC, C_RQ, C_RK, C_RV, C_RG, C_DT = 0, 512, 1024, 1280, 1536, 1792, 2048, 2304
PF_W = 2432
PF_CHUNK = 640
QKV_W = 768


def _cparams(sem):
    return pltpu.CompilerParams(dimension_semantics=sem, vmem_limit_bytes=VMEM_LIMIT)


def _group_mask(shape, axis, group, idx):
    io = lax.broadcasted_iota(jnp.int32, shape, axis)
    return (io >= idx * group) & (io < (idx + 1) * group)


def _lane_tile(x, n, axis=1):
    return jnp.concatenate([x] * n, axis=axis)


def _silu(x):
    h = 0.5 * x
    return h + h * jnp.tanh(h)


def _inproj_kernel(x_ref, g_ref, w_ref, qkv_ref, qvt_ref, pf_ref):
    x = x_ref[...]
    ms = jnp.mean(x * x, axis=-1, keepdims=True)
    h = (x * lax.rsqrt(ms + EPS) * g_ref[...]).astype(BF16)
    qkv = jnp.dot(h, w_ref[:, :QKV_W], preferred_element_type=F32)
    qkv_ref[...] = qkv.astype(BF16)
    qvt_ref[...] = qkv[:, :2 * DA_WIDTH].T.astype(BF16)
    for lo in range(0, PF_W, PF_CHUNK):
        hi = min(lo + PF_CHUNK, PF_W)
        pf_ref[:, lo:hi] = jnp.dot(h, w_ref[:, QKV_W + lo:QKV_W + hi], preferred_element_type=F32)


def _inproj(x2, gain, w):
    t = x2.shape[0]
    return pl.pallas_call(
        _inproj_kernel,
        grid=(t // TM_PROJ,),
        in_specs=[
            pl.BlockSpec((TM_PROJ, D_MODEL), lambda i: (i, 0)),
            pl.BlockSpec((1, D_MODEL), lambda i: (0, 0)),
            pl.BlockSpec((D_MODEL, QKV_W + PF_W), lambda i: (0, 0)),
        ],
        out_specs=[
            pl.BlockSpec((TM_PROJ, QKV_W), lambda i: (i, 0)),
            pl.BlockSpec((2 * DA_WIDTH, TM_PROJ), lambda i: (0, i)),
            pl.BlockSpec((TM_PROJ, PF_W), lambda i: (i, 0)),
        ],
        out_shape=[
            jax.ShapeDtypeStruct((t, QKV_W), BF16),
            jax.ShapeDtypeStruct((2 * DA_WIDTH, t), BF16),
            jax.ShapeDtypeStruct((t, PF_W), F32),
        ],
        compiler_params=_cparams(("parallel",)),
        name="inproj",
    )(x2, gain, w)


def _attn_kernel(qi_ref, kj_ref, qt_ref, k_ref, vt_ref, bias_ref, gain_ref, lam_ref, o_ref,
                 qm_ref, m_ref, l_ref, acc_ref, s_ref, e_ref, al_ref, mx_ref):
    p = pl.program_id(1)
    qi = qi_ref[p]
    kj = kj_ref[p]
    n_ht = 2 * DA_HEADS

    @pl.when(kj == 0)
    def _init():
        qt = qt_ref[...]
        for ht in range(n_ht):
            qm_ref[ht] = jnp.where(_group_mask(qt.shape, 0, DA_HEAD_DIM, ht), qt, jnp.zeros_like(qt))
        m_ref[...] = jnp.full(m_ref.shape, NEG_BIG, F32)
        l_ref[...] = jnp.zeros(l_ref.shape, F32)
        acc_ref[...] = jnp.zeros(acc_ref.shape, F32)

    def step(with_bias):
        k = k_ref[...]
        def scores(a, buf):
            s = jnp.dot(k, qm_ref[a], preferred_element_type=F32)
            if with_bias:
                s = s + bias_ref[0, a // 2]
            s_ref[buf] = s
            mx_ref[buf] = jnp.max(s, axis=0, keepdims=True)

        def softmax(b, buf):
            for c0 in range(0, ATT_T, LANES):
                cols = slice(c0, c0 + LANES)
                m_prev = m_ref[b, :, cols]
                m_new = jnp.maximum(m_prev, mx_ref[buf, :, cols])
                alpha = jnp.exp2(m_prev - m_new)
                e = jnp.exp2(s_ref[buf, :, cols] - m_new)
                l_ref[b, :, cols] = alpha * l_ref[b, :, cols] + jnp.sum(e, axis=0, keepdims=True)
                m_ref[b, :, cols] = m_new
                al_ref[buf, :, cols] = alpha
                e_ref[buf, :, cols] = e.astype(BF16)

        def values(c, buf):
            r0 = (c // 4) * LANES
            if not isinstance(r0, int):
                r0 = pl.multiple_of(r0, LANES)
            pv = jnp.dot(vt_ref[pl.ds(r0, LANES), :], e_ref[buf], preferred_element_type=F32)
            acc_ref[c] = acc_ref[c] * al_ref[buf] + pv

        scores(0, 0)
        scores(1, 1)
        softmax(0, 0)

        def trip(u, carry):
            t = 2 * u + 1
            scores(t + 1, 0)
            softmax(t, 1)
            values(t - 1, 0)
            scores(t + 2, 1)
            softmax(t + 1, 0)
            values(t, 1)
            return carry

        for u in range((n_ht - 2) // 2):
            trip(u, 0)
        softmax(n_ht - 1, 1)
        values(n_ht - 2, 0)
        values(n_ht - 1, 1)

    @pl.when(kj >= qi - 1)
    def _near():
        step(True)

    @pl.when(kj < qi - 1)
    def _far():
        step(False)

    @pl.when(kj == qi)
    def _finish():
        lam = lam_ref[...]
        row_lo = lax.broadcasted_iota(jnp.int32, (LANES, ATT_T), 0) < DA_V_DIM
        for pair in range(DA_HEADS // 2):
            halves = []
            for sub in range(2):
                h = 2 * pair + sub
                o = acc_ref[2 * h] * (1.0 / l_ref[2 * h]) - lam * (acc_ref[2 * h + 1] * (1.0 / l_ref[2 * h + 1]))
                own = row_lo if sub == 0 else jnp.logical_not(row_lo)
                ms = jnp.sum(jnp.where(own, o * o, 0.0), axis=0, keepdims=True) * (1.0 / DA_V_DIM)
                halves.append(o * lax.rsqrt(ms + EPS))
            r0 = pair * LANES
            blk = jnp.where(row_lo, halves[0], halves[1]) * gain_ref[r0:r0 + LANES, :]
            o_ref[:, r0:r0 + LANES] = blk.T.astype(BF16)


def _attention(qkv, qvt, bias_tiles, gain, lam, batch, seq):
    nq = seq // ATT_T
    qi_l, kj_l = [], []
    for qi in range(nq):
        for kj in range(qi + 1):
            qi_l.append(qi)
            kj_l.append(kj)
    qi_tbl = jnp.asarray(np.array(qi_l, np.int32))
    kj_tbl = jnp.asarray(np.array(kj_l, np.int32))
    n_pairs = len(qi_l)
    n_ht = 2 * DA_HEADS

    grid_spec = pltpu.PrefetchScalarGridSpec(
        num_scalar_prefetch=2,
        grid=(batch, n_pairs),
        in_specs=[
            pl.BlockSpec((DA_WIDTH, ATT_T), lambda b, p, qi, kj: (0, b * nq + qi[p])),
            pl.BlockSpec((ATT_T, DA_WIDTH), lambda b, p, qi, kj: (b * nq + kj[p], 2)),
            pl.BlockSpec((DA_WIDTH, ATT_T), lambda b, p, qi, kj: (1, b * nq + kj[p])),
            pl.BlockSpec((1, DA_HEADS, ATT_T, ATT_T),
                         lambda b, p, qi, kj: (jnp.minimum(qi[p] - kj[p], 1), 0, 0, 0)),
            pl.BlockSpec((DA_WIDTH, 1), lambda b, p, qi, kj: (0, 0)),
            pl.BlockSpec((1, ATT_T), lambda b, p, qi, kj: (0, 0)),
        ],
        out_specs=pl.BlockSpec((ATT_T, DA_WIDTH), lambda b, p, qi, kj: (b * nq + qi[p], 0)),
        scratch_shapes=[
            pltpu.VMEM((n_ht, DA_WIDTH, ATT_T), BF16),
            pltpu.VMEM((n_ht, 1, ATT_T), F32),
            pltpu.VMEM((n_ht, 1, ATT_T), F32),
            pltpu.VMEM((n_ht, LANES, ATT_T), F32),
            pltpu.VMEM((2, ATT_T, ATT_T), F32),
            pltpu.VMEM((2, ATT_T, ATT_T), BF16),
            pltpu.VMEM((2, 1, ATT_T), F32),
            pltpu.VMEM((2, 1, ATT_T), F32),
        ],
    )
    return pl.pallas_call(
        _attn_kernel,
        grid_spec=grid_spec,
        out_shape=jax.ShapeDtypeStruct((batch * seq, DA_WIDTH), BF16),
        compiler_params=_cparams(("parallel", "arbitrary")),
        name="diff_attention",
    )(qi_tbl, kj_tbl, qvt, qkv, qvt, bias_tiles, gain, lam)


def _t5_bucket(rel):
    half = REL_BUCKETS // 2
    max_exact = half // 2
    ret = jnp.where(rel > 0, half, 0)
    n = jnp.abs(rel)
    nf = jnp.maximum(n, 1).astype(F32)
    large = max_exact + (jnp.log(nf / max_exact) / math.log(REL_MAX_DIST / max_exact)
                         * (half - max_exact)).astype(jnp.int32)
    large = jnp.minimum(large, half - 1)
    return ret + jnp.where(n < max_exact, n, large)


def _bias_tiles(rel_bias):
    n = ATT_T
    i = jnp.arange(n)[None, :]
    j = jnp.arange(n)[:, None]
    far = rel_bias[_t5_bucket(jnp.int32(-2 * n))].astype(F32)
    tiles = []
    for off in range(2):
        rel = (n - 1 - off * n) - jnp.arange(2 * n)
        bucket = _t5_bucket(rel)
        g = jnp.zeros((DA_HEADS, 2 * n), F32)
        for c in range(REL_BUCKETS):
            g = jnp.where((bucket == c)[None], rel_bias[c].astype(F32)[:, None], g)
        g = (g - far[:, None]) * LOG2E
        skew = jnp.broadcast_to(g[:, None, :], (DA_HEADS, n, 2 * n)).reshape(DA_HEADS, 2 * n * n)
        skew = skew[:, :n * (2 * n - 1)].reshape(DA_HEADS, n, 2 * n - 1)
        b = skew[:, :, n - 1:]
        if off == 0:
            allowed = (j // CHUNK) <= (i // CHUNK)
            b = jnp.where(allowed[None], b, NEG_BIG)
        tiles.append(b)
    return jnp.stack(tiles)


def _cumsum_rows(x):
    n = x.shape[0]
    row = lax.broadcasted_iota(jnp.int32, x.shape, 0)
    sh = 1
    while sh < n:
        x = x + jnp.where(row >= sh, pltpu.roll(x, sh, axis=0), 0.0)
        sh *= 2
    return x


def _expand_heads(v, width, group):
    out = jnp.zeros((v.shape[0], width), F32)
    for h in range(width // group):
        col = jnp.broadcast_to(v[:, h:h + 1], (v.shape[0], width))
        out = jnp.where(_group_mask(out.shape, 1, group, h), col, out)
    return out


def _ssd_kernel(z_ref, xs_ref, bc_ref, dt_ref, cwx_ref, cwb_ref, cbx_ref, cbb_ref, dtb_ref, alog_ref,
                dskip_ref, ng_ref, o_ref, extx_ref, extb_ref, st_ref):
    j = pl.program_id(1)

    @pl.when(j == 0)
    def _reset():
        extx_ref[0:8, :] = jnp.zeros((8, SSM_WIDTH), F32)
        extb_ref[0:8, :] = jnp.zeros((8, 2 * LANES), F32)
        st_ref[...] = jnp.zeros(st_ref.shape, F32)

    extx_ref[8:8 + BLK, :] = xs_ref[...]
    extb_ref[8:8 + BLK, :] = bc_ref[...]
    yx = jnp.broadcast_to(cbx_ref[...], (BLK, SSM_WIDTH))
    yb = jnp.broadcast_to(cbb_ref[...], (BLK, 2 * LANES))
    for w in range(SSM_CONV):
        lo = 8 - (SSM_CONV - 1) + w
        yx = yx + extx_ref[lo:lo + BLK, :] * cwx_ref[w:w + 1, :]
        yb = yb + extb_ref[lo:lo + BLK, :] * cwb_ref[w:w + 1, :]
    tail_x = extx_ref[BLK:BLK + 8, :]
    tail_b = extb_ref[BLK:BLK + 8, :]
    extx_ref[0:8, :] = tail_x
    extb_ref[0:8, :] = tail_b
    xs = _silu(yx)
    bc = _silu(yb)
    bm = bc[:, :LANES]
    cm = bc[:, LANES:]

    dt = jax.nn.softplus(dt_ref[...] + dtb_ref[...])
    a = -jnp.exp(alog_ref[...]) * dt
    a_cs = _cumsum_rows(a)
    a_cs_t = a_cs.T
    a_last = a_cs[BLK - 1:BLK, :]

    dt_full = _expand_heads(dt, SSM_WIDTH, SSM_HEAD_DIM)
    ea_full = _expand_heads(jnp.exp(a_cs), SSM_WIDTH, SSM_HEAD_DIM)
    dec_full = _expand_heads(jnp.exp(a_last - a_cs), SSM_WIDTH, SSM_HEAD_DIM)
    cdec_full = _expand_heads(jnp.exp(a_last), SSM_WIDTH, SSM_HEAD_DIM)

    xdt = xs * dt_full
    xdt_b = xdt.astype(BF16)
    cm_b = cm.astype(BF16)
    bm_b = bm.astype(BF16)

    st = st_ref[...]
    y = jnp.dot(cm_b, st.astype(BF16), preferred_element_type=F32) * ea_full + xs * dskip_ref[...]

    row = lax.broadcasted_iota(jnp.int32, (BLK, BLK), 0)
    colio = lax.broadcasted_iota(jnp.int32, (BLK, BLK), 1)
    causal = colio <= row
    rep = SSM_HEADS // SSM_GROUPS
    lane_lo = lax.broadcasted_iota(jnp.int32, (BLK, LANES), 1) < SSM_HEAD_DIM
    pieces = []
    for g in range(SSM_GROUPS):
        cg = jnp.where(_group_mask(cm_b.shape, 1, SSM_STATE, g), cm_b, jnp.zeros_like(cm_b))
        cb = lax.dot_general(cg, bm_b, (((1,), (1,)), ((), ())), preferred_element_type=F32)
        for pair in range(rep // 2):
            halves = []
            for sub in range(2):
                h = g * rep + pair * 2 + sub
                seg = jnp.broadcast_to(a_cs[:, h:h + 1], (BLK, BLK)) - a_cs_t[h:h + 1, :]
                lmat = jnp.exp(jnp.where(causal, seg, NEG_BIG))
                mh = (cb * lmat).astype(BF16)
                c0 = (h // 2) * LANES
                halves.append(jnp.dot(mh, xdt_b[:, c0:c0 + LANES], preferred_element_type=F32))
            pieces.append(jnp.where(lane_lo, halves[0], halves[1]))
    y = y + jnp.concatenate(pieces, axis=1)

    upd = jnp.dot(bm.T.astype(BF16), (xdt * dec_full).astype(BF16), preferred_element_type=F32)
    r_io = lax.broadcasted_iota(jnp.int32, upd.shape, 0) // SSM_STATE
    c_io = lax.broadcasted_iota(jnp.int32, upd.shape, 1) // (SSM_HEAD_DIM * rep)
    st_ref[...] = st * cdec_full + jnp.where(r_io == c_io, upd, 0.0)

    y = y * _silu(z_ref[...])
    gw = SSM_WIDTH // SSM_GROUPS
    for g in range(SSM_GROUPS):
        yg = y[:, g * gw:(g + 1) * gw]
        ms = jnp.mean(yg * yg, axis=-1, keepdims=True)
        o_ref[:, g * gw:(g + 1) * gw] = (yg * lax.rsqrt(ms + EPS) * ng_ref[:, g * gw:(g + 1) * gw]).astype(BF16)


def _ssd(pf, cwx, cwb, cbx, cbb, dtb, alog, dskip, ng, batch, seq):
    nb = seq // BLK
    row = lambda w, c: pl.BlockSpec((BLK, w), lambda b, j: (b * nb + j, c))
    const = lambda r, w: pl.BlockSpec((r, w), lambda b, j: (0, 0))
    return pl.pallas_call(
        _ssd_kernel,
        grid=(batch, nb),
        in_specs=[
            row(SSM_WIDTH, C_Z // SSM_WIDTH), row(SSM_WIDTH, C_XS // SSM_WIDTH), row(2 * LANES, C_BC // (2 * LANES)),
            row(LANES, C_DT // LANES),
            const(SSM_CONV, SSM_WIDTH), const(SSM_CONV, 2 * LANES), const(1, SSM_WIDTH), const(1, 2 * LANES),
            const(1, LANES), const(1, LANES), const(1, SSM_WIDTH), const(1, SSM_WIDTH),
        ],
        out_specs=pl.BlockSpec((BLK, SSM_WIDTH), lambda b, j: (b * nb + j, 0)),
        out_shape=jax.ShapeDtypeStruct((batch * seq, SSM_WIDTH), BF16),
        scratch_shapes=[
            pltpu.VMEM((BLK + 8, SSM_WIDTH), F32),
            pltpu.VMEM((BLK + 8, 2 * LANES), F32),
            pltpu.VMEM((SSM_GROUPS * SSM_STATE, SSM_WIDTH), F32),
        ],
        compiler_params=_cparams(("parallel", "arbitrary")),
        name="ssd_mixer",
    )(pf, pf, pf, pf, cwx, cwb, cbx, cbb, dtb, alog, dskip, ng)


def _rotary(u, cos, sin_signed):
    first = (lax.broadcasted_iota(jnp.int32, u.shape, 1) % RET_KEY_DIM) < (RET_KEY_DIM // 2)
    half = RET_KEY_DIM // 2
    swapped = jnp.where(first, pltpu.roll(u, u.shape[1] - half, axis=1), pltpu.roll(u, half, axis=1))
    return u * cos + swapped * sin_signed


def _ret_kernel(q_ref, k_ref, v_ref, g_ref, cos_ref, sin_ref, dmat_ref, qdec_ref, kdec_ref, gdec_ref,
                bd_ref, gain_ref, o_ref, st_ref):
    j = pl.program_id(1)

    @pl.when(j == 0)
    def _reset():
        st_ref[...] = jnp.zeros(st_ref.shape, F32)

    cos = _lane_tile(cos_ref[...], RET_WIDTH // LANES)
    sin = _lane_tile(sin_ref[...], RET_WIDTH // LANES)
    q = _rotary(q_ref[...], cos, sin)
    k = _rotary(k_ref[...], cos, sin) * (RET_KEY_DIM ** -0.5)
    v_b = v_ref[...].astype(BF16)
    q_b = q.astype(BF16)
    k_b = k.astype(BF16)

    st = st_ref[...]
    o = jnp.dot((q * qdec_ref[...]).astype(BF16), st.astype(BF16), preferred_element_type=F32)
    for h in range(RET_HEADS):
        cm = _group_mask(q_b.shape, 1, RET_KEY_DIM, h)
        qh = jnp.where(cm, q_b, jnp.zeros_like(q_b))
        s = lax.dot_general(qh, k_b, (((1,), (1,)), ((), ())), preferred_element_type=F32) * dmat_ref[h]
        oh = jnp.dot(s.astype(BF16), v_b, preferred_element_type=F32)
        o = o + jnp.where(cm, oh, 0.0)

    upd = jnp.dot((k * kdec_ref[...]).T.astype(BF16), v_b, preferred_element_type=F32)
    st_ref[...] = st * gdec_ref[...] + upd * bd_ref[...]

    out = jnp.zeros(o.shape, F32)
    for h in range(RET_HEADS):
        cm = _group_mask(o.shape, 1, RET_KEY_DIM, h)
        ms = jnp.sum(jnp.where(cm, o * o, 0.0), axis=-1, keepdims=True) * (1.0 / RET_KEY_DIM)
        out = out + jnp.where(cm, o * lax.rsqrt(ms + EPS), 0.0)
    o_ref[...] = (_silu(g_ref[...]) * (out * gain_ref[...])).astype(BF16)


def _retention(pf, cos, sin, dmat, qdec, kdec, gdec, bd, gain, batch, seq):
    nb = seq // BLK
    row = lambda c: pl.BlockSpec((BLK, RET_WIDTH), lambda b, j: (b * nb + j, c))
    const2 = pl.BlockSpec((BLK, RET_WIDTH), lambda b, j: (0, 0))
    return pl.pallas_call(
        _ret_kernel,
        grid=(batch, nb),
        in_specs=[
            row(C_RQ // RET_WIDTH), row(C_RK // RET_WIDTH), row(C_RV // RET_WIDTH), row(C_RG // RET_WIDTH),
            pl.BlockSpec((BLK, LANES), lambda b, j: (j, 0)),
            pl.BlockSpec((BLK, LANES), lambda b, j: (j, 0)),
            pl.BlockSpec((RET_HEADS, BLK, BLK), lambda b, j: (0, 0, 0)),
            const2, const2, const2, const2,
            pl.BlockSpec((1, RET_WIDTH), lambda b, j: (0, 0)),
        ],
        out_specs=pl.BlockSpec((BLK, RET_WIDTH), lambda b, j: (b * nb + j, 0)),
        out_shape=jax.ShapeDtypeStruct((batch * seq, RET_WIDTH), BF16),
        scratch_shapes=[pltpu.VMEM((RET_WIDTH, RET_WIDTH), F32)],
        compiler_params=_cparams(("parallel", "arbitrary")),
        name="retention",
    )(pf, pf, pf, pf, cos, sin, dmat, qdec, kdec, gdec, bd, gain)


def _retention_tables(seq):
    f32 = np.float32
    inv = (f32(1.0) / (f32(ROPE_BASE) ** (np.arange(0, RET_KEY_DIM, 2, dtype=f32) / f32(RET_KEY_DIM)))).astype(f32)
    ang = (np.arange(seq, dtype=f32)[:, None] * inv[None, :]).astype(f32)
    cos_h = np.concatenate([np.cos(ang), np.cos(ang)], axis=-1)
    sin_h = np.concatenate([-np.sin(ang), np.sin(ang)], axis=-1)
    cos = np.tile(cos_h, (1, 2)).astype(f32)
    sin = np.tile(sin_h, (1, 2)).astype(f32)
    log_gamma = np.log1p(-np.power(f32(2.0), f32(-5.0) - np.arange(RET_HEADS, dtype=f32))).astype(f32)
    idx = np.arange(BLK, dtype=f32)
    ii = np.arange(BLK)
    same_or_earlier_chunk = (ii[None, :] // CHUNK) <= (ii[:, None] // CHUNK)
    dmat = np.exp(log_gamma[:, None, None] * np.abs(idx[:, None] - idx[None, :]))
    dmat = np.where(same_or_earlier_chunk[None], dmat, 0.0).astype(f32)
    lg_cols = np.repeat(log_gamma, RET_KEY_DIM)
    qdec = np.exp(lg_cols[None, :] * (idx + 1.0)[:, None]).astype(f32)
    kdec = np.exp(lg_cols[None, :] * (BLK - 1.0 - idx)[:, None]).astype(f32)
    head_of = np.arange(RET_WIDTH) // RET_KEY_DIM
    bd = (head_of[:, None] == head_of[None, :]).astype(f32)
    gdec = (np.exp(lg_cols * f32(BLK))[:, None] * bd).astype(f32)
    return tuple(jnp.asarray(a) for a in (cos, sin, dmat, qdec, kdec, gdec, bd))


def _outproj_kernel(x_ref, a_ref, b_ref, c_ref, w_ref, g_ref, *rest, with_router):
    if with_router:
        rw_ref, xo_ref, h_ref, rinfo_ref = rest
    else:
        xo_ref, h_ref = rest
    acc = x_ref[...]
    acc = acc + jnp.dot(a_ref[...], w_ref[0:DA_WIDTH, :], preferred_element_type=F32)
    acc = acc + jnp.dot(b_ref[...], w_ref[DA_WIDTH:DA_WIDTH + SSM_WIDTH, :], preferred_element_type=F32)
    acc = acc + jnp.dot(c_ref[...], w_ref[DA_WIDTH + SSM_WIDTH:, :], preferred_element_type=F32)
    xo_ref[...] = acc
    ms = jnp.mean(acc * acc, axis=-1, keepdims=True)
    hf = acc * lax.rsqrt(ms + EPS) * g_ref[...]
    h = hf.astype(BF16)
    h_ref[...] = hf.astype(h_ref.dtype)
    if with_router:
        logits = jnp.dot(h, rw_ref[...], preferred_element_type=F32)
        lane = lax.broadcasted_iota(jnp.int32, logits.shape, 1).astype(F32)
        logits = jnp.where(lane < N_EXPERTS, logits, NEG_BIG)
        v1 = jnp.max(logits, axis=-1, keepdims=True)
        i1 = jnp.min(jnp.where(logits == v1, lane, float(LANES)), axis=-1, keepdims=True)
        rest_l = jnp.where(lane == i1, NEG_BIG, logits)
        v2 = jnp.max(rest_l, axis=-1, keepdims=True)
        i2 = jnp.min(jnp.where(rest_l == v2, lane, float(LANES)), axis=-1, keepdims=True)
        e2 = jnp.exp(v2 - v1)
        g1 = 1.0 / (1.0 + e2)
        g2 = e2 * g1
        rinfo_ref[...] = (jnp.where(lane == 0.0, g1, 0.0) + jnp.where(lane == 1.0, g2, 0.0)
                          + jnp.where(lane == 2.0, i1, 0.0) + jnp.where(lane == 3.0, i2, 0.0))


def _outproj(x2, oa, ob, oc, w, gain, router_w=None):
    t = x2.shape[0]
    with_router = router_w is not None
    tok = lambda w_: pl.BlockSpec((TM_PROJ, w_), lambda i: (i, 0))
    in_specs = [tok(D_MODEL), tok(DA_WIDTH), tok(SSM_WIDTH), tok(RET_WIDTH),
                pl.BlockSpec((D_MODEL, D_MODEL), lambda i: (0, 0)),
                pl.BlockSpec((1, D_MODEL), lambda i: (0, 0))]
    out_specs = [tok(D_MODEL), tok(D_MODEL)]
    h_dtype = F32 if with_router else BF16
    out_shape = [jax.ShapeDtypeStruct((t, D_MODEL), F32), jax.ShapeDtypeStruct((t, D_MODEL), h_dtype)]
    args = [x2, oa, ob, oc, w, gain]
    if with_router:
        in_specs.append(pl.BlockSpec((D_MODEL, LANES), lambda i: (0, 0)))
        out_specs.append(tok(LANES))
        out_shape.append(jax.ShapeDtypeStruct((t, LANES), F32))
        args.append(router_w)
    return pl.pallas_call(
        functools.partial(_outproj_kernel, with_router=with_router),
        grid=(t // TM_PROJ,),
        in_specs=in_specs,
        out_specs=out_specs,
        out_shape=out_shape,
        compiler_params=_cparams(("parallel",)),
        name="outproj_router" if with_router else "outproj",
    )(*args)


def _ffn_kernel(x_ref, a_ref, b_ref, c_ref, wo_ref, g_ref, wg_ref, wu_ref, wd_ref, o_ref, acc_ref, h_ref):
    f = pl.program_id(1)

    @pl.when(f == 0)
    def _init():
        acc = x_ref[...]
        acc = acc + jnp.dot(a_ref[...], wo_ref[0:DA_WIDTH, :], preferred_element_type=F32)
        acc = acc + jnp.dot(b_ref[...], wo_ref[DA_WIDTH:DA_WIDTH + SSM_WIDTH, :], preferred_element_type=F32)
        acc = acc + jnp.dot(c_ref[...], wo_ref[DA_WIDTH + SSM_WIDTH:, :], preferred_element_type=F32)
        acc_ref[...] = acc
        ms = jnp.mean(acc * acc, axis=-1, keepdims=True)
        h_ref[...] = (acc * lax.rsqrt(ms + EPS) * g_ref[...]).astype(BF16)

    h = h_ref[...]
    gate = jnp.dot(h, wg_ref[0], preferred_element_type=F32)
    up = jnp.dot(h, wu_ref[0], preferred_element_type=F32)
    act = (_silu(gate) * up).astype(BF16)
    acc_ref[...] += jnp.dot(act, wd_ref[...], preferred_element_type=F32)

    @pl.when(f == pl.num_programs(1) - 1)
    def _done():
        o_ref[...] = acc_ref[...]


def _outproj_dense_ffn(x2, oa, ob, oc, w_o, gain, wg, wu, wd):
    t = x2.shape[0]
    nf = wg.shape[0]
    tok = lambda w_: pl.BlockSpec((TM_FFN, w_), lambda i, f: (i, 0))
    return pl.pallas_call(
        _ffn_kernel,
        grid=(t // TM_FFN, nf),
        in_specs=[
            tok(D_MODEL), tok(DA_WIDTH), tok(SSM_WIDTH), tok(RET_WIDTH),
            pl.BlockSpec((D_MODEL, D_MODEL), lambda i, f: (0, 0)),
            pl.BlockSpec((1, D_MODEL), lambda i, f: (0, 0)),
            pl.BlockSpec((1, D_MODEL, TF_FFN), lambda i, f: (f, 0, 0)),
            pl.BlockSpec((1, D_MODEL, TF_FFN), lambda i, f: (f, 0, 0)),
            pl.BlockSpec((TF_FFN, D_MODEL), lambda i, f: (f, 0)),
        ],
        out_specs=pl.BlockSpec((TM_FFN, D_MODEL), lambda i, f: (i, 0)),
        out_shape=jax.ShapeDtypeStruct((t, D_MODEL), F32),
        scratch_shapes=[pltpu.VMEM((TM_FFN, D_MODEL), F32), pltpu.VMEM((TM_FFN, D_MODEL), BF16)],
        compiler_params=_cparams(("parallel", "arbitrary")),
        name="outproj_dense_swiglu",
    )(x2, oa, ob, oc, w_o, gain, wg, wu, wd)


def _issue_row_gather(src_hbm, idx_ref, idx_base, dst, sem, n_rows):
    def body(r, carry):
        tok = idx_ref[idx_base + r]
        pltpu.make_async_copy(src_hbm.at[pl.ds(tok, 1)], dst.at[pl.ds(r, 1)], sem).start()
        return carry
    lax.fori_loop(0, n_rows, body, 0, unroll=8)


def _wait_row_gather(src_hbm, dst, sem, n_rows):
    pltpu.make_async_copy(src_hbm.at[pl.ds(0, n_rows)], dst, sem).wait()


def _moe_ffn_kernel(te_ref, nv_ref, tb_ref, rt_ref, h_hbm, wg_ref, wu_ref, wd_ref, y_ref, hbuf, xb_ref, acc_ref, sem):
    i = pl.program_id(0)
    f = pl.program_id(1)
    n_valid = nv_ref[0]
    valid = i < n_valid
    slot = i % 2

    @pl.when((f == 0) & (i == 0) & valid)
    def _first():
        _issue_row_gather(h_hbm, rt_ref, tb_ref[0], hbuf.at[0], sem.at[0], TR_MOE)

    @pl.when((f == 0) & valid)
    def _arrive():
        _wait_row_gather(h_hbm, hbuf.at[slot], sem.at[slot], TR_MOE)
        xb_ref[...] = hbuf[slot].astype(BF16)

    def compute(prefetch_next):
        if prefetch_next:
            rows_per_step = TR_MOE // (FFN_EXPERT // TF_MOE)
            r0 = f * rows_per_step
            src0 = tb_ref[i + 1] + r0
            for j in range(rows_per_step):
                tok = rt_ref[src0 + j]
                pltpu.make_async_copy(h_hbm.at[pl.ds(tok, 1)], hbuf.at[1 - slot, pl.ds(r0 + j, 1)],
                                      sem.at[1 - slot]).start()
        xb = xb_ref[...]
        gate = jnp.dot(xb, wg_ref[0].astype(BF16), preferred_element_type=F32)
        up = jnp.dot(xb, wu_ref[0].astype(BF16), preferred_element_type=F32)
        act = (_silu(gate) * up).astype(BF16)
        contrib = jnp.dot(act, wd_ref[0].astype(BF16), preferred_element_type=F32)

        @pl.when(f == 0)
        def _():
            acc_ref[...] = contrib

        @pl.when(f > 0)
        def _():
            acc_ref[...] += contrib

    has_next = i + 1 < n_valid

    @pl.when(valid & has_next)
    def _compute_and_prefetch():
        compute(True)

    @pl.when(valid & jnp.logical_not(has_next))
    def _compute_last():
        compute(False)

    @pl.when(f == pl.num_programs(1) - 1)
    def _store():
        y_ref[...] = jnp.where(valid, acc_ref[...], 0.0)


def _moe_ffn(h, tile_expert, n_valid, tile_base, sorted_token, wg, wu, wd):
    n_tiles = tile_expert.shape[0]
    nf = wg.shape[2] // TF_MOE

    def w_in_map(i, f, te, nv, tb, rt):
        return (te[i], 0, jnp.where(i < nv[0], f, nf - 1))

    def w_out_map(i, f, te, nv, tb, rt):
        return (te[i], jnp.where(i < nv[0], f, nf - 1), 0)

    grid_spec = pltpu.PrefetchScalarGridSpec(
        num_scalar_prefetch=4,
        grid=(n_tiles, nf),
        in_specs=[
            pl.BlockSpec(memory_space=pl.ANY),
            pl.BlockSpec((1, D_MODEL, TF_MOE), w_in_map),
            pl.BlockSpec((1, D_MODEL, TF_MOE), w_in_map),
            pl.BlockSpec((1, TF_MOE, D_MODEL), w_out_map),
        ],
        out_specs=pl.BlockSpec((TR_MOE, D_MODEL), lambda i, f, te, nv, tb, rt: (i, 0)),
        scratch_shapes=[
            pltpu.VMEM((2, TR_MOE, D_MODEL), F32),
            pltpu.VMEM((TR_MOE, D_MODEL), BF16),
            pltpu.VMEM((TR_MOE, D_MODEL), F32),
            pltpu.SemaphoreType.DMA((2,)),
        ],
    )
    return pl.pallas_call(
        _moe_ffn_kernel,
        grid_spec=grid_spec,
        out_shape=jax.ShapeDtypeStruct((n_tiles * TR_MOE, D_MODEL), F32),
        compiler_params=_cparams(("arbitrary", "arbitrary")),
        name="moe_expert_ffn",
    )(tile_expert, n_valid, tile_base, sorted_token, h, wg, wu, wd)


def _moe_combine_kernel(tr_ref, x_ref, rinfo_ref, y_hbm, fg_ref, o_ref, ybuf, sem):
    i = pl.program_id(0)
    slot = i % 2
    n_rows = TOP_K * TM_COMB

    @pl.when(i == 0)
    def _first():
        _issue_row_gather(y_hbm, tr_ref, 0, ybuf.at[0], sem.at[0], n_rows)

    _wait_row_gather(y_hbm, ybuf.at[slot], sem.at[slot], n_rows)

    @pl.when(i + 1 < pl.num_programs(0))
    def _prefetch():
        src0 = (i + 1) * n_rows
        dst = ybuf.at[1 - slot]
        for r in range(n_rows):
            pltpu.make_async_copy(y_hbm.at[pl.ds(tr_ref[src0 + r], 1)], dst.at[pl.ds(r, 1)], sem.at[1 - slot]).start()

    rinfo = rinfo_ref[...]
    g1 = jnp.broadcast_to(rinfo[:, 0:1], (TM_COMB, D_MODEL))
    g2 = jnp.broadcast_to(rinfo[:, 1:2], (TM_COMB, D_MODEL))
    y = x_ref[...] + g1 * ybuf[slot, 0:TM_COMB, :] + g2 * ybuf[slot, TM_COMB:n_rows, :]
    ms = jnp.mean(y * y, axis=-1, keepdims=True)
    o_ref[...] = y * lax.rsqrt(ms + EPS) * fg_ref[...]


def _moe_combine(x2, rinfo, y_rows, tok_rows, final_gain):
    t = x2.shape[0]
    grid_spec = pltpu.PrefetchScalarGridSpec(
        num_scalar_prefetch=1,
        grid=(t // TM_COMB,),
        in_specs=[
            pl.BlockSpec((TM_COMB, D_MODEL), lambda i, tr: (i, 0)),
            pl.BlockSpec((TM_COMB, LANES), lambda i, tr: (i, 0)),
            pl.BlockSpec(memory_space=pl.ANY),
            pl.BlockSpec((1, D_MODEL), lambda i, tr: (0, 0)),
        ],
        out_specs=pl.BlockSpec((TM_COMB, D_MODEL), lambda i, tr: (i, 0)),
        scratch_shapes=[
            pltpu.VMEM((2, TOP_K * TM_COMB, D_MODEL), F32),
            pltpu.SemaphoreType.DMA((2,)),
        ],
    )
    return pl.pallas_call(
        _moe_combine_kernel,
        grid_spec=grid_spec,
        out_shape=jax.ShapeDtypeStruct((t, D_MODEL), F32),
        compiler_params=_cparams(("arbitrary",)),
        name="moe_combine_norm",
    )(tok_rows, x2, rinfo, y_rows, final_gain)


def _routing_tables(rinfo):
    t = rinfo.shape[0]
    n_assign = TOP_K * t
    n_tiles = -(-n_assign // TR_MOE) + N_EXPERTS
    flat_e = rinfo[:, 2:2 + TOP_K].astype(jnp.int32).reshape(-1)
    onehot = (flat_e[:, None] == jnp.arange(N_EXPERTS)[None, :]).astype(jnp.int32)
    csum = jnp.cumsum(onehot, axis=0)
    rank = jnp.sum((csum - onehot) * onehot, axis=1)
    counts = csum[-1]
    tiles_per_e = (counts + TR_MOE - 1) // TR_MOE
    tile_end = jnp.cumsum(tiles_per_e)
    first_tile = tile_end - tiles_per_e
    dest = (first_tile * TR_MOE)[flat_e] + rank
    order = jnp.sort(flat_e * n_assign + jnp.arange(n_assign, dtype=jnp.int32))
    sorted_token = jnp.concatenate([(order % n_assign) // TOP_K, jnp.zeros((TR_MOE,), jnp.int32)])
    start = jnp.cumsum(counts) - counts
    n_valid = tile_end[-1:].astype(jnp.int32)
    tile_ids = jnp.arange(n_tiles)
    tile_expert = jnp.searchsorted(tile_end, tile_ids, side="right").astype(jnp.int32)
    last_e = jnp.searchsorted(tile_end, n_valid[0] - 1, side="right").astype(jnp.int32)
    tile_expert = jnp.where(tile_ids < n_valid[0], tile_expert, last_e)
    tile_base = start[tile_expert] + (tile_ids - first_tile[tile_expert]) * TR_MOE
    tile_base = jnp.clip(tile_base, 0, n_assign).astype(jnp.int32)
    tok_rows = dest.reshape(t // TM_COMB, TM_COMB, TOP_K).transpose(0, 2, 1).reshape(-1).astype(jnp.int32)
    return tile_expert, n_valid, tile_base, sorted_token, tok_rows


def _permute_w_in(w):
    sizes = (256, 256, 256, 512, 768, 8, 256, 256, 256, 256)
    offs = np.concatenate([[0], np.cumsum(sizes)])
    part = lambda n: w[:, offs[n]:offs[n + 1]]
    dt_pad = jnp.zeros((w.shape[0], LANES - SSM_HEADS), w.dtype)
    cols = [part(0) * (DA_HEAD_DIM ** -0.5 * LOG2E), part(2), part(1), part(3), part(4),
            part(6), part(7), part(8), part(9), part(5), dt_pad]
    return jnp.concatenate(cols, axis=1).astype(BF16)


def _column_tiles(w, tile):
    *lead, k, n = w.shape
    return jnp.moveaxis(w.reshape(*lead, k, n // tile, tile), -2, -3)


def _pad_lanes(v, fill=0.0):
    return jnp.concatenate([v, jnp.full((LANES - v.shape[0],), fill, v.dtype)])[None, :]


def kernel(x, w_in, w_out, attn_norm, ffn_norm, final_norm, rel_bias, lambda_q1, lambda_k1, lambda_q2,
           lambda_k2, da_head_norm, conv_w, conv_b, dt_bias, a_log, d_skip, ssm_norm, ret_head_norm,
           w_gate, w_up, w_down, router_w, e_gate, e_up, e_down):
    batch, seq, _ = x.shape
    x2 = x.reshape(batch * seq, D_MODEL)
    bias_tiles = _bias_tiles(rel_bias)
    ret_tabs = _retention_tables(seq)

    for layer in range(DEPTH):
        lam_init = 0.8 - 0.6 * math.exp(-0.3 * layer)
        lam = (jnp.exp(jnp.sum(lambda_q1[layer] * lambda_k1[layer]))
               - jnp.exp(jnp.sum(lambda_q2[layer] * lambda_k2[layer])) + lam_init).astype(F32)
        lam = jnp.full((1, ATT_T), lam, F32)

        qkv, qvt, pf = _inproj(x2, attn_norm[layer][None, :], _permute_w_in(w_in[layer]))

        da_gain = (jnp.tile(da_head_norm[layer], DA_HEADS) * (1.0 - lam_init))[:, None]
        out_a = _attention(qkv, qvt, bias_tiles, da_gain, lam, batch, seq)

        cw, cb = conv_w[layer], conv_b[layer]
        out_b = _ssd(pf, cw[:, :SSM_WIDTH], cw[:, SSM_WIDTH:], cb[None, :SSM_WIDTH], cb[None, SSM_WIDTH:],
                     _pad_lanes(dt_bias[layer]), _pad_lanes(a_log[layer]),
                     jnp.repeat(d_skip[layer], SSM_HEAD_DIM)[None, :], ssm_norm[layer][None, :], batch, seq)

        ret_gain = jnp.tile(ret_head_norm[layer], RET_HEADS)[None, :]
        out_c = _retention(pf, *ret_tabs, ret_gain, batch, seq)

        w_o = w_out[layer].astype(BF16)
        if layer % 2 == 0:
            i = layer // 2
            pad = FFN_PAD - FFN_DENSE
            wg = _column_tiles(jnp.pad(w_gate[i], ((0, 0), (0, pad))).astype(BF16), TF_FFN)
            wu = _column_tiles(jnp.pad(w_up[i], ((0, 0), (0, pad))).astype(BF16), TF_FFN)
            wd = jnp.pad(w_down[i], ((0, pad), (0, 0))).astype(BF16)
            x2 = _outproj_dense_ffn(x2, out_a, out_b, out_c, w_o, ffn_norm[layer][None, :], wg, wu, wd)
        else:
            i = layer // 2
            rw = jnp.pad(router_w[i], ((0, 0), (0, LANES - N_EXPERTS))).astype(BF16)
            x2, h, rinfo = _outproj(x2, out_a, out_b, out_c, w_o, ffn_norm[layer][None, :], rw)
            tile_expert, n_valid, tile_base, sorted_token, tok_rows = _routing_tables(rinfo)
            y_rows = _moe_ffn(h, tile_expert, n_valid, tile_base, sorted_token,
                              e_gate[i], e_up[i], e_down[i])
            x2 = _moe_combine(x2, rinfo, y_rows, tok_rows, final_norm[None, :])
    return x2.reshape(batch, seq, D_MODEL)
```

```python
import math

import jax
import jax.numpy as jnp
import numpy as np
from jax import lax
from jax.experimental import pallas as pl
from jax.experimental.pallas import tpu as pltpu

F32 = jnp.float32
BF16 = jnp.bfloat16

D_MODEL = 1024
DEPTH = 2
CHUNK = 64
EPS = 1e-6
DA_HEADS = 4
DA_HEAD_DIM = 32
DA_V_DIM = 64
DA_WIDTH = 256
SSM_HEADS = 8
SSM_HEAD_DIM = 64
SSM_WIDTH = 512
SSM_STATE = 64
SSM_GROUPS = 2
SSM_CONV = 4
RET_HEADS = 4
RET_KEY_DIM = 64
RET_WIDTH = 256
ROPE_BASE = 10000.0
REL_BUCKETS = 32
REL_MAX_DIST = 128
FFN_DENSE = 2752
N_EXPERTS = 8
FFN_EXPERT = 3584

LANES = 128
VMEM_LIMIT = 48 * 1024 * 1024
NEG_BIG = -1e30

TM_PROJ = 512
ATT_T = 512
LOG2E = math.log2(math.e)
BLK = 256
FFN_PAD = 2816
TM_FFN = 512
TF_FFN = 1408
TF_MOE = 512
TR_MOE = 896
TM_COMB = 256
TOP_K = 2

C_Z, C_XS, C_BC, C_RQ, C_RK, C_RV, C_RG, C_DT = 0, 512, 1024, 1280, 1536, 1792, 2048, 2304
PF_W = 2432
PF_CHUNK = 640
QKV_W = 768


def _cparams(sem):
    return pltpu.CompilerParams(dimension_semantics=sem, vmem_limit_bytes=VMEM_LIMIT)


def _group_mask(shape, axis, group, idx):
    io = lax.broadcasted_iota(jnp.int32, shape, axis)
    return (io >= idx * group) & (io < (idx + 1) * group)


def _lane_tile(x, n, axis=1):
    return jnp.concatenate([x] * n, axis=axis)


def _silu(x):
    h = 0.5 * x
    return h + h * jnp.tanh(h)


def _inproj_kernel(x_ref, g_ref, w_ref, qkv_ref, qvt_ref, pf_ref):
    x = x_ref[...]
    ms = jnp.mean(x * x, axis=-1, keepdims=True)
    h = (x * lax.rsqrt(ms + EPS) * g_ref[...]).astype(BF16)
    qkv = jnp.dot(h, w_ref[:, :QKV_W], preferred_element_type=F32)
    qkv_ref[...] = qkv.astype(BF16)
    qvt_ref[...] = qkv[:, :2 * DA_WIDTH].T.astype(BF16)
    for lo in range(0, PF_W, PF_CHUNK):
        hi = min(lo + PF_CHUNK, PF_W)
        pf_ref[:, lo:hi] = jnp.dot(h, w_ref[:, QKV_W + lo:QKV_W + hi], preferred_element_type=F32)


def _inproj(x2, gain, w):
    t = x2.shape[0]
    return pl.pallas_call(
        _inproj_kernel,
        grid=(t // TM_PROJ,),
        in_specs=[
            pl.BlockSpec((TM_PROJ, D_MODEL), lambda i: (i, 0)),
            pl.BlockSpec((1, D_MODEL), lambda i: (0, 0)),
            pl.BlockSpec((D_MODEL, QKV_W + PF_W), lambda i: (0, 0)),
        ],
        out_specs=[
            pl.BlockSpec((TM_PROJ, QKV_W), lambda i: (i, 0)),
            pl.BlockSpec((2 * DA_WIDTH, TM_PROJ), lambda i: (0, i)),
            pl.BlockSpec((TM_PROJ, PF_W), lambda i: (i, 0)),
        ],
        out_shape=[
            jax.ShapeDtypeStruct((t, QKV_W), BF16),
            jax.ShapeDtypeStruct((2 * DA_WIDTH, t), BF16),
            jax.ShapeDtypeStruct((t, PF_W), F32),
        ],
        compiler_params=_cparams(("parallel",)),
        name="inproj",
    )(x2, gain, w)


def _attn_kernel(qi_ref, kj_ref, qt_ref, k_ref, vt_ref, bias_ref, gain_ref, lam_ref, o_ref,
                 qm_ref, m_ref, l_ref, acc_ref, s_ref, e_ref, al_ref):
    p = pl.program_id(1)
    qi = qi_ref[p]
    kj = kj_ref[p]
    n_ht = 2 * DA_HEADS

    @pl.when(kj == 0)
    def _init():
        qt = qt_ref[...]
        for ht in range(n_ht):
            qm_ref[ht] = jnp.where(_group_mask(qt.shape, 0, DA_HEAD_DIM, ht), qt, jnp.zeros_like(qt))
        m_ref[...] = jnp.full(m_ref.shape, NEG_BIG, F32)
        l_ref[...] = jnp.zeros(l_ref.shape, F32)
        acc_ref[...] = jnp.zeros(acc_ref.shape, F32)

    def step(with_bias):
        k = k_ref[...]
        def scores(a, buf):
            s_ref[buf] = jnp.dot(k, qm_ref[a], preferred_element_type=F32)

        def softmax(b, buf):
            for c0 in range(0, ATT_T, LANES):
                cols = slice(c0, c0 + LANES)

                def block():
                    blk = s_ref[buf, :, cols]
                    return blk + bias_ref[0, b // 2, :, cols] if with_bias else blk

                m_prev = m_ref[b, :, cols]
                m_new = jnp.maximum(m_prev, jnp.max(block(), axis=0, keepdims=True))
                alpha = jnp.exp2(m_prev - m_new)
                e = jnp.exp2(block() - m_new)
                l_ref[b, :, cols] = alpha * l_ref[b, :, cols] + jnp.sum(e, axis=0, keepdims=True)
                m_ref[b, :, cols] = m_new
                al_ref[buf, :, cols] = alpha
                e_ref[buf, :, cols] = e.astype(BF16)

        def values(c, buf):
            r0 = (c // 4) * LANES
            if not isinstance(r0, int):
                r0 = pl.multiple_of(r0, LANES)
            pv = jnp.dot(vt_ref[pl.ds(r0, LANES), :], e_ref[buf], preferred_element_type=F32)
            acc_ref[c] = acc_ref[c] * al_ref[buf] + pv

        scores(0, 0)
        scores(1, 1)
        softmax(0, 0)

        def trip(u, carry):
            t = 2 * u + 1
            scores(t + 1, 0)
            softmax(t, 1)
            values(t - 1, 0)
            scores(t + 2, 1)
            softmax(t + 1, 0)
            values(t, 1)
            return carry

        for u in range((n_ht - 2) // 2):
            trip(u, 0)
        softmax(n_ht - 1, 1)
        values(n_ht - 2, 0)
        values(n_ht - 1, 1)

    @pl.when(kj >= qi - 1)
    def _near():
        step(True)

    @pl.when(kj < qi - 1)
    def _far():
        step(False)

    @pl.when(kj == qi)
    def _finish():
        lam = lam_ref[...]
        row_lo = lax.broadcasted_iota(jnp.int32, (LANES, ATT_T), 0) < DA_V_DIM
        for pair in range(DA_HEADS // 2):
            halves = []
            for sub in range(2):
                h = 2 * pair + sub
                o = acc_ref[2 * h] * (1.0 / l_ref[2 * h]) - lam * (acc_ref[2 * h + 1] * (1.0 / l_ref[2 * h + 1]))
                own = row_lo if sub == 0 else jnp.logical_not(row_lo)
                ms = jnp.sum(jnp.where(own, o * o, 0.0), axis=0, keepdims=True) * (1.0 / DA_V_DIM)
                halves.append(o * lax.rsqrt(ms + EPS))
            r0 = pair * LANES
            blk = jnp.where(row_lo, halves[0], halves[1]) * gain_ref[r0:r0 + LANES, :]
            o_ref[:, r0:r0 + LANES] = blk.T.astype(BF16)


def _attention(qkv, qvt, bias_tiles, gain, lam, batch, seq):
    nq = seq // ATT_T
    qi_l, kj_l = [], []
    for qi in range(nq):
        for kj in range(qi + 1):
            qi_l.append(qi)
            kj_l.append(kj)
    qi_tbl = jnp.asarray(np.array(qi_l, np.int32))
    kj_tbl = jnp.asarray(np.array(kj_l, np.int32))
    n_pairs = len(qi_l)
    n_ht = 2 * DA_HEADS

    grid_spec = pltpu.PrefetchScalarGridSpec(
        num_scalar_prefetch=2,
        grid=(batch, n_pairs),
        in_specs=[
            pl.BlockSpec((DA_WIDTH, ATT_T), lambda b, p, qi, kj: (0, b * nq + qi[p])),
            pl.BlockSpec((ATT_T, DA_WIDTH), lambda b, p, qi, kj: (b * nq + kj[p], 2)),
            pl.BlockSpec((DA_WIDTH, ATT_T), lambda b, p, qi, kj: (1, b * nq + kj[p])),
            pl.BlockSpec((1, DA_HEADS, ATT_T, ATT_T),
                         lambda b, p, qi, kj: (jnp.minimum(qi[p] - kj[p], 1), 0, 0, 0)),
            pl.BlockSpec((DA_WIDTH, 1), lambda b, p, qi, kj: (0, 0)),
            pl.BlockSpec((1, ATT_T), lambda b, p, qi, kj: (0, 0)),
        ],
        out_specs=pl.BlockSpec((ATT_T, DA_WIDTH), lambda b, p, qi, kj: (b * nq + qi[p], 0)),
        scratch_shapes=[
            pltpu.VMEM((n_ht, DA_WIDTH, ATT_T), BF16),
            pltpu.VMEM((n_ht, 1, ATT_T), F32),
            pltpu.VMEM((n_ht, 1, ATT_T), F32),
            pltpu.VMEM((n_ht, LANES, ATT_T), F32),
            pltpu.VMEM((2, ATT_T, ATT_T), F32),
            pltpu.VMEM((2, ATT_T, ATT_T), BF16),
            pltpu.VMEM((2, 1, ATT_T), F32),
        ],
    )
    return pl.pallas_call(
        _attn_kernel,
        grid_spec=grid_spec,
        out_shape=jax.ShapeDtypeStruct((batch * seq, DA_WIDTH), BF16),
        compiler_params=_cparams(("parallel", "arbitrary")),
        name="diff_attention",
    )(qi_tbl, kj_tbl, qvt, qkv, qvt, bias_tiles, gain, lam)


def _t5_bucket(rel):
    half = REL_BUCKETS // 2
    max_exact = half // 2
    ret = jnp.where(rel > 0, half, 0)
    n = jnp.abs(rel)
    nf = jnp.maximum(n, 1).astype(F32)
    large = max_exact + (jnp.log(nf / max_exact) / math.log(REL_MAX_DIST / max_exact)
                         * (half - max_exact)).astype(jnp.int32)
    large = jnp.minimum(large, half - 1)
    return ret + jnp.where(n < max_exact, n, large)


def _bias_tiles(rel_bias):
    n = ATT_T
    i = jnp.arange(n)[None, :]
    j = jnp.arange(n)[:, None]
    far = rel_bias[_t5_bucket(jnp.int32(-2 * n))].astype(F32)
    tiles = []
    for off in range(2):
        rel = (n - 1 - off * n) - jnp.arange(2 * n)
        bucket = _t5_bucket(rel)
        g = jnp.zeros((DA_HEADS, 2 * n), F32)
        for c in range(REL_BUCKETS):
            g = jnp.where((bucket == c)[None], rel_bias[c].astype(F32)[:, None], g)
        g = (g - far[:, None]) * LOG2E
        skew = jnp.broadcast_to(g[:, None, :], (DA_HEADS, n, 2 * n)).reshape(DA_HEADS, 2 * n * n)
        skew = skew[:, :n * (2 * n - 1)].reshape(DA_HEADS, n, 2 * n - 1)
        b = skew[:, :, n - 1:]
        if off == 0:
            allowed = (j // CHUNK) <= (i // CHUNK)
            b = jnp.where(allowed[None], b, NEG_BIG)
        tiles.append(b)
    return jnp.stack(tiles)


def _cumsum_rows(x):
    n = x.shape[0]
    row = lax.broadcasted_iota(jnp.int32, x.shape, 0)
    sh = 1
    while sh < n:
        x = x + jnp.where(row >= sh, pltpu.roll(x, sh, axis=0), 0.0)
        sh *= 2
    return x


def _expand_heads(v, width, group):
    out = jnp.zeros((v.shape[0], width), F32)
    for h in range(width // group):
        col = jnp.broadcast_to(v[:, h:h + 1], (v.shape[0], width))
        out = jnp.where(_group_mask(out.shape, 1, group, h), col, out)
    return out


def _ssd_kernel(z_ref, xs_ref, bc_ref, dt_ref, cwx_ref, cwb_ref, cbx_ref, cbb_ref, dtb_ref, alog_ref,
                dskip_ref, ng_ref, o_ref, extx_ref, extb_ref, st_ref):
    j = pl.program_id(1)

    @pl.when(j == 0)
    def _reset():
        extx_ref[0:8, :] = jnp.zeros((8, SSM_WIDTH), F32)
        extb_ref[0:8, :] = jnp.zeros((8, 2 * LANES), F32)
        st_ref[...] = jnp.zeros(st_ref.shape, F32)

    extx_ref[8:8 + BLK, :] = xs_ref[...]
    extb_ref[8:8 + BLK, :] = bc_ref[...]
    yx = jnp.broadcast_to(cbx_ref[...], (BLK, SSM_WIDTH))
    yb = jnp.broadcast_to(cbb_ref[...], (BLK, 2 * LANES))
    for w in range(SSM_CONV):
        lo = 8 - (SSM_CONV - 1) + w
        yx = yx + extx_ref[lo:lo + BLK, :] * cwx_ref[w:w + 1, :]
        yb = yb + extb_ref[lo:lo + BLK, :] * cwb_ref[w:w + 1, :]
    tail_x = extx_ref[BLK:BLK + 8, :]
    tail_b = extb_ref[BLK:BLK + 8, :]
    extx_ref[0:8, :] = tail_x
    extb_ref[0:8, :] = tail_b
    xs = _silu(yx)
    bc = _silu(yb)
    bm = bc[:, :LANES]
    cm = bc[:, LANES:]

    dt = jax.nn.softplus(dt_ref[...] + dtb_ref[...])
    a = -jnp.exp(alog_ref[...]) * dt
    a_cs = _cumsum_rows(a)
    a_cs_t = a_cs.T
    a_last = a_cs[BLK - 1:BLK, :]

    dt_full = _expand_heads(dt, SSM_WIDTH, SSM_HEAD_DIM)
    ea_full = _expand_heads(jnp.exp(a_cs), SSM_WIDTH, SSM_HEAD_DIM)
    dec_full = _expand_heads(jnp.exp(a_last - a_cs), SSM_WIDTH, SSM_HEAD_DIM)
    cdec_full = _expand_heads(jnp.exp(a_last), SSM_WIDTH, SSM_HEAD_DIM)

    xdt = xs * dt_full
    xdt_b = xdt.astype(BF16)
    cm_b = cm.astype(BF16)
    bm_b = bm.astype(BF16)

    st = st_ref[...]
    y = jnp.dot(cm_b, st.astype(BF16), preferred_element_type=F32) * ea_full + xs * dskip_ref[...]

    row = lax.broadcasted_iota(jnp.int32, (BLK, BLK), 0)
    colio = lax.broadcasted_iota(jnp.int32, (BLK, BLK), 1)
    causal = colio <= row
    rep = SSM_HEADS // SSM_GROUPS
    lane_lo = lax.broadcasted_iota(jnp.int32, (BLK, LANES), 1) < SSM_HEAD_DIM
    pieces = []
    for g in range(SSM_GROUPS):
        cg = jnp.where(_group_mask(cm_b.shape, 1, SSM_STATE, g), cm_b, jnp.zeros_like(cm_b))
        cb = lax.dot_general(cg, bm_b, (((1,), (1,)), ((), ())), preferred_element_type=F32)
        for pair in range(rep // 2):
            halves = []
            for sub in range(2):
                h = g * rep + pair * 2 + sub
                seg = jnp.broadcast_to(a_cs[:, h:h + 1], (BLK, BLK)) - a_cs_t[h:h + 1, :]
                lmat = jnp.exp(jnp.where(causal, seg, NEG_BIG))
                mh = (cb * lmat).astype(BF16)
                c0 = (h // 2) * LANES
                halves.append(jnp.dot(mh, xdt_b[:, c0:c0 + LANES], preferred_element_type=F32))
            pieces.append(jnp.where(lane_lo, halves[0], halves[1]))
    y = y + jnp.concatenate(pieces, axis=1)

    upd = jnp.dot(bm.T.astype(BF16), (xdt * dec_full).astype(BF16), preferred_element_type=F32)
    r_io = lax.broadcasted_iota(jnp.int32, upd.shape, 0) // SSM_STATE
    c_io = lax.broadcasted_iota(jnp.int32, upd.shape, 1) // (SSM_HEAD_DIM * rep)
    st_ref[...] = st * cdec_full + jnp.where(r_io == c_io, upd, 0.0)

    y = y * _silu(z_ref[...])
    gw = SSM_WIDTH // SSM_GROUPS
    for g in range(SSM_GROUPS):
        yg = y[:, g * gw:(g + 1) * gw]
        ms = jnp.mean(yg * yg, axis=-1, keepdims=True)
        o_ref[:, g * gw:(g + 1) * gw] = (yg * lax.rsqrt(ms + EPS) * ng_ref[:, g * gw:(g + 1) * gw]).astype(BF16)


def _ssd(pf, cwx, cwb, cbx, cbb, dtb, alog, dskip, ng, batch, seq):
    nb = seq // BLK
    row = lambda w, c: pl.BlockSpec((BLK, w), lambda b, j: (b * nb + j, c))
    const = lambda r, w: pl.BlockSpec((r, w), lambda b, j: (0, 0))
    return pl.pallas_call(
        _ssd_kernel,
        grid=(batch, nb),
        in_specs=[
            row(SSM_WIDTH, C_Z // SSM_WIDTH), row(SSM_WIDTH, C_XS // SSM_WIDTH), row(2 * LANES, C_BC // (2 * LANES)),
            row(LANES, C_DT // LANES),
            const(SSM_CONV, SSM_WIDTH), const(SSM_CONV, 2 * LANES), const(1, SSM_WIDTH), const(1, 2 * LANES),
            const(1, LANES), const(1, LANES), const(1, SSM_WIDTH), const(1, SSM_WIDTH),
        ],
        out_specs=pl.BlockSpec((BLK, SSM_WIDTH), lambda b, j: (b * nb + j, 0)),
        out_shape=jax.ShapeDtypeStruct((batch * seq, SSM_WIDTH), BF16),
        scratch_shapes=[
            pltpu.VMEM((BLK + 8, SSM_WIDTH), F32),
            pltpu.VMEM((BLK + 8, 2 * LANES), F32),
            pltpu.VMEM((SSM_GROUPS * SSM_STATE, SSM_WIDTH), F32),
        ],
        compiler_params=_cparams(("parallel", "arbitrary")),
        name="ssd_mixer",
    )(pf, pf, pf, pf, cwx, cwb, cbx, cbb, dtb, alog, dskip, ng)


def _rotary(u, cos, sin_signed):
    first = (lax.broadcasted_iota(jnp.int32, u.shape, 1) % RET_KEY_DIM) < (RET_KEY_DIM // 2)
    half = RET_KEY_DIM // 2
    swapped = jnp.where(first, pltpu.roll(u, u.shape[1] - half, axis=1), pltpu.roll(u, half, axis=1))
    return u * cos + swapped * sin_signed


def _ret_kernel(q_ref, k_ref, v_ref, g_ref, cos_ref, sin_ref, dmat_ref, qdec_ref, kdec_ref, gdec_ref,
                bd_ref, gain_ref, o_ref, st_ref):
    j = pl.program_id(1)

    @pl.when(j == 0)
    def _reset():
        st_ref[...] = jnp.zeros(st_ref.shape, F32)

    cos = _lane_tile(cos_ref[...], RET_WIDTH // LANES)
    sin = _lane_tile(sin_ref[...], RET_WIDTH // LANES)
    q = _rotary(q_ref[...], cos, sin)
    k = _rotary(k_ref[...], cos, sin) * (RET_KEY_DIM ** -0.5)
    v_b = v_ref[...].astype(BF16)
    q_b = q.astype(BF16)
    k_b = k.astype(BF16)

    st = st_ref[...]
    o = jnp.dot((q * qdec_ref[...]).astype(BF16), st.astype(BF16), preferred_element_type=F32)
    for h in range(RET_HEADS):
        cm = _group_mask(q_b.shape, 1, RET_KEY_DIM, h)
        qh = jnp.where(cm, q_b, jnp.zeros_like(q_b))
        s = lax.dot_general(qh, k_b, (((1,), (1,)), ((), ())), preferred_element_type=F32) * dmat_ref[h]
        oh = jnp.dot(s.astype(BF16), v_b, preferred_element_type=F32)
        o = o + jnp.where(cm, oh, 0.0)

    upd = jnp.dot((k * kdec_ref[...]).T.astype(BF16), v_b, preferred_element_type=F32)
    st_ref[...] = st * gdec_ref[...] + upd * bd_ref[...]

    out = jnp.zeros(o.shape, F32)
    for h in range(RET_HEADS):
        cm = _group_mask(o.shape, 1, RET_KEY_DIM, h)
        ms = jnp.sum(jnp.where(cm, o * o, 0.0), axis=-1, keepdims=True) * (1.0 / RET_KEY_DIM)
        out = out + jnp.where(cm, o * lax.rsqrt(ms + EPS), 0.0)
    o_ref[...] = (_silu(g_ref[...]) * (out * gain_ref[...])).astype(BF16)


def _retention(pf, cos, sin, dmat, qdec, kdec, gdec, bd, gain, batch, seq):
    nb = seq // BLK
    row = lambda c: pl.BlockSpec((BLK, RET_WIDTH), lambda b, j: (b * nb + j, c))
    const2 = pl.BlockSpec((BLK, RET_WIDTH), lambda b, j: (0, 0))
    return pl.pallas_call(
        _ret_kernel,
        grid=(batch, nb),
        in_specs=[
            row(C_RQ // RET_WIDTH), row(C_RK // RET_WIDTH), row(C_RV // RET_WIDTH), row(C_RG // RET_WIDTH),
            pl.BlockSpec((BLK, LANES), lambda b, j: (j, 0)),
            pl.BlockSpec((BLK, LANES), lambda b, j: (j, 0)),
            pl.BlockSpec((RET_HEADS, BLK, BLK), lambda b, j: (0, 0, 0)),
            const2, const2, const2, const2,
            pl.BlockSpec((1, RET_WIDTH), lambda b, j: (0, 0)),
        ],
        out_specs=pl.BlockSpec((BLK, RET_WIDTH), lambda b, j: (b * nb + j, 0)),
        out_shape=jax.ShapeDtypeStruct((batch * seq, RET_WIDTH), BF16),
        scratch_shapes=[pltpu.VMEM((RET_WIDTH, RET_WIDTH), F32)],
        compiler_params=_cparams(("parallel", "arbitrary")),
        name="retention",
    )(pf, pf, pf, pf, cos, sin, dmat, qdec, kdec, gdec, bd, gain)


def _retention_tables(seq):
    f32 = np.float32
    inv = (f32(1.0) / (f32(ROPE_BASE) ** (np.arange(0, RET_KEY_DIM, 2, dtype=f32) / f32(RET_KEY_DIM)))).astype(f32)
    ang = (np.arange(seq, dtype=f32)[:, None] * inv[None, :]).astype(f32)
    cos_h = np.concatenate([np.cos(ang), np.cos(ang)], axis=-1)
    sin_h = np.concatenate([-np.sin(ang), np.sin(ang)], axis=-1)
    cos = np.tile(cos_h, (1, 2)).astype(f32)
    sin = np.tile(sin_h, (1, 2)).astype(f32)
    log_gamma = np.log1p(-np.power(f32(2.0), f32(-5.0) - np.arange(RET_HEADS, dtype=f32))).astype(f32)
    idx = np.arange(BLK, dtype=f32)
    ii = np.arange(BLK)
    same_or_earlier_chunk = (ii[None, :] // CHUNK) <= (ii[:, None] // CHUNK)
    dmat = np.exp(log_gamma[:, None, None] * np.abs(idx[:, None] - idx[None, :]))
    dmat = np.where(same_or_earlier_chunk[None], dmat, 0.0).astype(f32)
    lg_cols = np.repeat(log_gamma, RET_KEY_DIM)
    qdec = np.exp(lg_cols[None, :] * (idx + 1.0)[:, None]).astype(f32)
    kdec = np.exp(lg_cols[None, :] * (BLK - 1.0 - idx)[:, None]).astype(f32)
    head_of = np.arange(RET_WIDTH) // RET_KEY_DIM
    bd = (head_of[:, None] == head_of[None, :]).astype(f32)
    gdec = (np.exp(lg_cols * f32(BLK))[:, None] * bd).astype(f32)
    return tuple(jnp.asarray(a) for a in (cos, sin, dmat, qdec, kdec, gdec, bd))


def _outproj_router_kernel(x_ref, a_ref, b_ref, c_ref, w_ref, g_ref, rw_ref, xo_ref, h_ref, rinfo_ref):
    acc = x_ref[...]
    acc = acc + jnp.dot(a_ref[...], w_ref[0:DA_WIDTH, :], preferred_element_type=F32)
    acc = acc + jnp.dot(b_ref[...], w_ref[DA_WIDTH:DA_WIDTH + SSM_WIDTH, :], preferred_element_type=F32)
    acc = acc + jnp.dot(c_ref[...], w_ref[DA_WIDTH + SSM_WIDTH:, :], preferred_element_type=F32)
    xo_ref[...] = acc
    ms = jnp.mean(acc * acc, axis=-1, keepdims=True)
    hf = acc * lax.rsqrt(ms + EPS) * g_ref[...]
    h = hf.astype(BF16)
    h_ref[...] = hf
    logits = jnp.dot(h, rw_ref[...], preferred_element_type=F32)
    lane = lax.broadcasted_iota(jnp.int32, logits.shape, 1).astype(F32)
    logits = jnp.where(lane < N_EXPERTS, logits, NEG_BIG)
    v1 = jnp.max(logits, axis=-1, keepdims=True)
    i1 = jnp.min(jnp.where(logits == v1, lane, float(LANES)), axis=-1, keepdims=True)
    rest_l = jnp.where(lane == i1, NEG_BIG, logits)
    v2 = jnp.max(rest_l, axis=-1, keepdims=True)
    i2 = jnp.min(jnp.where(rest_l == v2, lane, float(LANES)), axis=-1, keepdims=True)
    e2 = jnp.exp(v2 - v1)
    g1 = 1.0 / (1.0 + e2)
    g2 = e2 * g1
    rinfo_ref[...] = (jnp.where(lane == 0.0, g1, 0.0) + jnp.where(lane == 1.0, g2, 0.0)
                      + jnp.where(lane == 2.0, i1, 0.0) + jnp.where(lane == 3.0, i2, 0.0))


def _outproj_router(x2, oa, ob, oc, w, gain, router_w):
    t = x2.shape[0]
    tok = lambda w_: pl.BlockSpec((TM_PROJ, w_), lambda i: (i, 0))
    return pl.pallas_call(
        _outproj_router_kernel,
        grid=(t // TM_PROJ,),
        in_specs=[tok(D_MODEL), tok(DA_WIDTH), tok(SSM_WIDTH), tok(RET_WIDTH),
                  pl.BlockSpec((D_MODEL, D_MODEL), lambda i: (0, 0)),
                  pl.BlockSpec((1, D_MODEL), lambda i: (0, 0)),
                  pl.BlockSpec((D_MODEL, LANES), lambda i: (0, 0))],
        out_specs=[tok(D_MODEL), tok(D_MODEL), tok(LANES)],
        out_shape=[jax.ShapeDtypeStruct((t, D_MODEL), F32), jax.ShapeDtypeStruct((t, D_MODEL), F32),
                   jax.ShapeDtypeStruct((t, LANES), F32)],
        compiler_params=_cparams(("parallel",)),
        name="outproj_router",
    )(x2, oa, ob, oc, w, gain, router_w)


def _ffn_kernel(x_ref, a_ref, b_ref, c_ref, wo_ref, g_ref, wg_ref, wu_ref, wd_ref, o_ref, acc_ref, h_ref):
    f = pl.program_id(1)

    @pl.when(f == 0)
    def _init():
        acc = x_ref[...]
        acc = acc + jnp.dot(a_ref[...], wo_ref[0:DA_WIDTH, :], preferred_element_type=F32)
        acc = acc + jnp.dot(b_ref[...], wo_ref[DA_WIDTH:DA_WIDTH + SSM_WIDTH, :], preferred_element_type=F32)
        acc = acc + jnp.dot(c_ref[...], wo_ref[DA_WIDTH + SSM_WIDTH:, :], preferred_element_type=F32)
        acc_ref[...] = acc
        ms = jnp.mean(acc * acc, axis=-1, keepdims=True)
        h_ref[...] = (acc * lax.rsqrt(ms + EPS) * g_ref[...]).astype(BF16)

    h = h_ref[...]
    gate = jnp.dot(h, wg_ref[0], preferred_element_type=F32)
    up = jnp.dot(h, wu_ref[0], preferred_element_type=F32)
    act = (_silu(gate) * up).astype(BF16)
    acc_ref[...] += jnp.dot(act, wd_ref[...], preferred_element_type=F32)

    @pl.when(f == pl.num_programs(1) - 1)
    def _done():
        o_ref[...] = acc_ref[...]


def _outproj_dense_ffn(x2, oa, ob, oc, w_o, gain, wg, wu, wd):
    t = x2.shape[0]
    nf = wg.shape[0]
    tok = lambda w_: pl.BlockSpec((TM_FFN, w_), lambda i, f: (i, 0))
    return pl.pallas_call(
        _ffn_kernel,
        grid=(t // TM_FFN, nf),
        in_specs=[
            tok(D_MODEL), tok(DA_WIDTH), tok(SSM_WIDTH), tok(RET_WIDTH),
            pl.BlockSpec((D_MODEL, D_MODEL), lambda i, f: (0, 0)),
            pl.BlockSpec((1, D_MODEL), lambda i, f: (0, 0)),
            pl.BlockSpec((1, D_MODEL, TF_FFN), lambda i, f: (f, 0, 0)),
            pl.BlockSpec((1, D_MODEL, TF_FFN), lambda i, f: (f, 0, 0)),
            pl.BlockSpec((TF_FFN, D_MODEL), lambda i, f: (f, 0)),
        ],
        out_specs=pl.BlockSpec((TM_FFN, D_MODEL), lambda i, f: (i, 0)),
        out_shape=jax.ShapeDtypeStruct((t, D_MODEL), F32),
        scratch_shapes=[pltpu.VMEM((TM_FFN, D_MODEL), F32), pltpu.VMEM((TM_FFN, D_MODEL), BF16)],
        compiler_params=_cparams(("parallel", "arbitrary")),
        name="outproj_dense_swiglu",
    )(x2, oa, ob, oc, w_o, gain, wg, wu, wd)


def _issue_row_gather(src_hbm, idx_ref, idx_base, dst, sem, n_rows):
    def body(r, carry):
        tok = idx_ref[idx_base + r]
        pltpu.make_async_copy(src_hbm.at[pl.ds(tok, 1)], dst.at[pl.ds(r, 1)], sem).start()
        return carry
    lax.fori_loop(0, n_rows, body, 0, unroll=8)


def _wait_row_gather(src_hbm, dst, sem, n_rows):
    pltpu.make_async_copy(src_hbm.at[pl.ds(0, n_rows)], dst, sem).wait()


def _moe_ffn_kernel(te_ref, nv_ref, tb_ref, rt_ref, h_hbm, wg_ref, wu_ref, wd_ref, y_ref, hbuf, xb_ref, acc_ref, sem):
    i = pl.program_id(0)
    f = pl.program_id(1)
    n_valid = nv_ref[0]
    valid = i < n_valid
    slot = i % 2

    @pl.when((f == 0) & (i == 0) & valid)
    def _first():
        _issue_row_gather(h_hbm, rt_ref, tb_ref[0], hbuf.at[0], sem.at[0], TR_MOE)

    @pl.when((f == 0) & valid)
    def _arrive():
        _wait_row_gather(h_hbm, hbuf.at[slot], sem.at[slot], TR_MOE)
        xb_ref[...] = hbuf[slot].astype(BF16)

    def compute(prefetch_next):
        if prefetch_next:
            rows_per_step = TR_MOE // (FFN_EXPERT // TF_MOE)
            r0 = f * rows_per_step
            src0 = tb_ref[i + 1] + r0
            for j in range(rows_per_step):
                tok = rt_ref[src0 + j]
                pltpu.make_async_copy(h_hbm.at[pl.ds(tok, 1)], hbuf.at[1 - slot, pl.ds(r0 + j, 1)],
                                      sem.at[1 - slot]).start()
        xb = xb_ref[...]
        gate = jnp.dot(xb, wg_ref[0].astype(BF16), preferred_element_type=F32)
        up = jnp.dot(xb, wu_ref[0].astype(BF16), preferred_element_type=F32)
        act = (_silu(gate) * up).astype(BF16)
        contrib = jnp.dot(act, wd_ref[0].astype(BF16), preferred_element_type=F32)

        @pl.when(f == 0)
        def _():
            acc_ref[...] = contrib

        @pl.when(f > 0)
        def _():
            acc_ref[...] += contrib

    has_next = i + 1 < n_valid

    @pl.when(valid & has_next)
    def _compute_and_prefetch():
        compute(True)

    @pl.when(valid & jnp.logical_not(has_next))
    def _compute_last():
        compute(False)

    @pl.when(f == pl.num_programs(1) - 1)
    def _store():
        y_ref[...] = jnp.where(valid, acc_ref[...], 0.0)


def _moe_ffn(h, tile_expert, n_valid, tile_base, sorted_token, wg, wu, wd):
    n_tiles = tile_expert.shape[0]
    nf = wg.shape[2] // TF_MOE

    def w_in_map(i, f, te, nv, tb, rt):
        return (te[i], 0, jnp.where(i < nv[0], f, nf - 1))

    def w_out_map(i, f, te, nv, tb, rt):
        return (te[i], jnp.where(i < nv[0], f, nf - 1), 0)

    grid_spec = pltpu.PrefetchScalarGridSpec(
        num_scalar_prefetch=4,
        grid=(n_tiles, nf),
        in_specs=[
            pl.BlockSpec(memory_space=pl.ANY),
            pl.BlockSpec((1, D_MODEL, TF_MOE), w_in_map),
            pl.BlockSpec((1, D_MODEL, TF_MOE), w_in_map),
            pl.BlockSpec((1, TF_MOE, D_MODEL), w_out_map),
        ],
        out_specs=pl.BlockSpec((TR_MOE, D_MODEL), lambda i, f, te, nv, tb, rt: (i, 0)),
        scratch_shapes=[
            pltpu.VMEM((2, TR_MOE, D_MODEL), F32),
            pltpu.VMEM((TR_MOE, D_MODEL), BF16),
            pltpu.VMEM((TR_MOE, D_MODEL), F32),
            pltpu.SemaphoreType.DMA((2,)),
        ],
    )
    return pl.pallas_call(
        _moe_ffn_kernel,
        grid_spec=grid_spec,
        out_shape=jax.ShapeDtypeStruct((n_tiles * TR_MOE, D_MODEL), F32),
        compiler_params=_cparams(("arbitrary", "arbitrary")),
        name="moe_expert_ffn",
    )(tile_expert, n_valid, tile_base, sorted_token, h, wg, wu, wd)


def _moe_combine_kernel(tr_ref, x_ref, rinfo_ref, y_hbm, fg_ref, o_ref, ybuf, sem):
    i = pl.program_id(0)
    slot = i % 2
    n_rows = TOP_K * TM_COMB

    @pl.when(i == 0)
    def _first():
        _issue_row_gather(y_hbm, tr_ref, 0, ybuf.at[0], sem.at[0], n_rows)

    _wait_row_gather(y_hbm, ybuf.at[slot], sem.at[slot], n_rows)

    @pl.when(i + 1 < pl.num_programs(0))
    def _prefetch():
        src0 = (i + 1) * n_rows
        dst = ybuf.at[1 - slot]
        for r in range(n_rows):
            pltpu.make_async_copy(y_hbm.at[pl.ds(tr_ref[src0 + r], 1)], dst.at[pl.ds(r, 1)], sem.at[1 - slot]).start()

    rinfo = rinfo_ref[...]
    g1 = jnp.broadcast_to(rinfo[:, 0:1], (TM_COMB, D_MODEL))
    g2 = jnp.broadcast_to(rinfo[:, 1:2], (TM_COMB, D_MODEL))
    y = x_ref[...] + g1 * ybuf[slot, 0:TM_COMB, :] + g2 * ybuf[slot, TM_COMB:n_rows, :]
    ms = jnp.mean(y * y, axis=-1, keepdims=True)
    o_ref[...] = y * lax.rsqrt(ms + EPS) * fg_ref[...]


def _moe_combine(x2, rinfo, y_rows, tok_rows, final_gain):
    t = x2.shape[0]
    grid_spec = pltpu.PrefetchScalarGridSpec(
        num_scalar_prefetch=1,
        grid=(t // TM_COMB,),
        in_specs=[
            pl.BlockSpec((TM_COMB, D_MODEL), lambda i, tr: (i, 0)),
            pl.BlockSpec((TM_COMB, LANES), lambda i, tr: (i, 0)),
            pl.BlockSpec(memory_space=pl.ANY),
            pl.BlockSpec((1, D_MODEL), lambda i, tr: (0, 0)),
        ],
        out_specs=pl.BlockSpec((TM_COMB, D_MODEL), lambda i, tr: (i, 0)),
        scratch_shapes=[
            pltpu.VMEM((2, TOP_K * TM_COMB, D_MODEL), F32),
            pltpu.SemaphoreType.DMA((2,)),
        ],
    )
    return pl.pallas_call(
        _moe_combine_kernel,
        grid_spec=grid_spec,
        out_shape=jax.ShapeDtypeStruct((t, D_MODEL), F32),
        compiler_params=_cparams(("arbitrary",)),
        name="moe_combine_norm",
    )(tok_rows, x2, rinfo, y_rows, final_gain)


def _routing_tables(rinfo):
    t = rinfo.shape[0]
    n_assign = TOP_K * t
    n_tiles = -(-n_assign // TR_MOE) + N_EXPERTS
    flat_e = rinfo[:, 2:2 + TOP_K].astype(jnp.int32).reshape(-1)
    onehot = (flat_e[:, None] == jnp.arange(N_EXPERTS)[None, :]).astype(jnp.int32)
    blocks = onehot.astype(BF16).reshape(n_assign // LANES, LANES, N_EXPERTS)
    tri = (jnp.arange(LANES)[:, None] >= jnp.arange(LANES)[None, :]).astype(BF16)
    within = jnp.einsum("ij,bjk->bik", tri, blocks, preferred_element_type=F32)
    block_end = jnp.cumsum(within[:, -1, :], axis=0)
    csum = (within + (block_end - within[:, -1, :])[:, None, :]).reshape(n_assign, N_EXPERTS).astype(jnp.int32)
    rank = jnp.sum((csum - onehot) * onehot, axis=1)
    counts = csum[-1]
    tiles_per_e = (counts + TR_MOE - 1) // TR_MOE
    tile_end = jnp.cumsum(tiles_per_e)
    first_tile = tile_end - tiles_per_e
    dest = (first_tile * TR_MOE)[flat_e] + rank
    order = jnp.sort(flat_e * n_assign + jnp.arange(n_assign, dtype=jnp.int32))
    sorted_token = jnp.concatenate([(order % n_assign) // TOP_K, jnp.zeros((TR_MOE,), jnp.int32)])
    start = jnp.cumsum(counts) - counts
    n_valid = tile_end[-1:].astype(jnp.int32)
    tile_ids = jnp.arange(n_tiles)
    tile_expert = jnp.searchsorted(tile_end, tile_ids, side="right").astype(jnp.int32)
    last_e = jnp.searchsorted(tile_end, n_valid[0] - 1, side="right").astype(jnp.int32)
    tile_expert = jnp.where(tile_ids < n_valid[0], tile_expert, last_e)
    tile_base = start[tile_expert] + (tile_ids - first_tile[tile_expert]) * TR_MOE
    tile_base = jnp.clip(tile_base, 0, n_assign).astype(jnp.int32)
    tok_rows = dest.reshape(t // TM_COMB, TM_COMB, TOP_K).transpose(0, 2, 1).reshape(-1).astype(jnp.int32)
    return tile_expert, n_valid, tile_base, sorted_token, tok_rows


def _permute_w_in(w):
    sizes = (256, 256, 256, 512, 768, 8, 256, 256, 256, 256)
    offs = np.concatenate([[0], np.cumsum(sizes)])
    part = lambda n: w[:, offs[n]:offs[n + 1]]
    dt_pad = jnp.zeros((w.shape[0], LANES - SSM_HEADS), w.dtype)
    cols = [part(0) * (DA_HEAD_DIM ** -0.5 * LOG2E), part(2), part(1), part(3), part(4),
            part(6), part(7), part(8), part(9), part(5), dt_pad]
    return jnp.concatenate(cols, axis=1).astype(BF16)


def _column_tiles(w, tile):
    *lead, k, n = w.shape
    return jnp.moveaxis(w.reshape(*lead, k, n // tile, tile), -2, -3)


def _pad_lanes(v, fill=0.0):
    return jnp.concatenate([v, jnp.full((LANES - v.shape[0],), fill, v.dtype)])[None, :]


def kernel(x, w_in, w_out, attn_norm, ffn_norm, final_norm, rel_bias, lambda_q1, lambda_k1, lambda_q2,
           lambda_k2, da_head_norm, conv_w, conv_b, dt_bias, a_log, d_skip, ssm_norm, ret_head_norm,
           w_gate, w_up, w_down, router_w, e_gate, e_up, e_down):
    batch, seq, _ = x.shape
    assert DEPTH % 2 == 0, "the final RMSNorm is fused into the last (expert) layer's combine kernel"
    assert seq % max(ATT_T, TM_PROJ, BLK) == 0 and (batch * seq) % TM_FFN == 0
    x2 = x.reshape(batch * seq, D_MODEL)
    bias_tiles = _bias_tiles(rel_bias)
    ret_tabs = _retention_tables(seq)

    for layer in range(DEPTH):
        lam_init = 0.8 - 0.6 * math.exp(-0.3 * layer)
        lam = (jnp.exp(jnp.sum(lambda_q1[layer] * lambda_k1[layer]))
               - jnp.exp(jnp.sum(lambda_q2[layer] * lambda_k2[layer])) + lam_init).astype(F32)
        lam = jnp.full((1, ATT_T), lam, F32)

        qkv, qvt, pf = _inproj(x2, attn_norm[layer][None, :], _permute_w_in(w_in[layer]))

        da_gain = (jnp.tile(da_head_norm[layer], DA_HEADS) * (1.0 - lam_init))[:, None]
        out_a = _attention(qkv, qvt, bias_tiles, da_gain, lam, batch, seq)

        cw, cb = conv_w[layer], conv_b[layer]
        out_b = _ssd(pf, cw[:, :SSM_WIDTH], cw[:, SSM_WIDTH:], cb[None, :SSM_WIDTH], cb[None, SSM_WIDTH:],
                     _pad_lanes(dt_bias[layer]), _pad_lanes(a_log[layer]),
                     jnp.repeat(d_skip[layer], SSM_HEAD_DIM)[None, :], ssm_norm[layer][None, :], batch, seq)

        ret_gain = jnp.tile(ret_head_norm[layer], RET_HEADS)[None, :]
        out_c = _retention(pf, *ret_tabs, ret_gain, batch, seq)

        w_o = w_out[layer].astype(BF16)
        if layer % 2 == 0:
            i = layer // 2
            pad = FFN_PAD - FFN_DENSE
            wg = _column_tiles(jnp.pad(w_gate[i], ((0, 0), (0, pad))).astype(BF16), TF_FFN)
            wu = _column_tiles(jnp.pad(w_up[i], ((0, 0), (0, pad))).astype(BF16), TF_FFN)
            wd = jnp.pad(w_down[i], ((0, pad), (0, 0))).astype(BF16)
            x2 = _outproj_dense_ffn(x2, out_a, out_b, out_c, w_o, ffn_norm[layer][None, :], wg, wu, wd)
        else:
            i = layer // 2
            rw = jnp.pad(router_w[i], ((0, 0), (0, LANES - N_EXPERTS))).astype(BF16)
            x2, h, rinfo = _outproj_router(x2, out_a, out_b, out_c, w_o, ffn_norm[layer][None, :], rw)
            tile_expert, n_valid, tile_base, sorted_token, tok_rows = _routing_tables(rinfo)
            y_rows = _moe_ffn(h, tile_expert, n_valid, tile_base, sorted_token,
                              e_gate[i], e_up[i], e_down[i])
            x2 = _moe_combine(x2, rinfo, y_rows, tok_rows, final_norm[None, :])
    return x2.reshape(batch, seq, D_MODEL)
```

```python
import math

import jax
import jax.numpy as jnp
import numpy as np
from jax import lax
from jax.experimental import pallas as pl
from jax.experimental.pallas import tpu as pltpu

F32 = jnp.float32
BF16 = jnp.bfloat16

D_MODEL = 1024
DEPTH = 2
CHUNK = 64
EPS = 1e-6
DA_HEADS = 4
DA_HEAD_DIM = 32
DA_V_DIM = 64
DA_WIDTH = 256
SSM_HEADS = 8
SSM_HEAD_DIM = 64
SSM_WIDTH = 512
SSM_STATE = 64
SSM_GROUPS = 2
SSM_CONV = 4
RET_HEADS = 4
RET_KEY_DIM = 64
RET_WIDTH = 256
ROPE_BASE = 10000.0
REL_BUCKETS = 32
REL_MAX_DIST = 128
FFN_DENSE = 2752
N_EXPERTS = 8
FFN_EXPERT = 3584

LANES = 128
VMEM_LIMIT = 48 * 1024 * 1024
NEG_BIG = -1e30

TM_PROJ = 512
ATT_T = 512
LOG2E = math.log2(math.e)
BLK = 256
FFN_PAD = 2816
TM_FFN = 512
TF_FFN = 1408
TF_MOE = 512
TR_MOE = 896
TM_COMB = 256
TOP_K = 2

C_Z, C_XS, C_BC, C_RQ, C_RK, C_RV, C_RG, C_DT = 0, 512, 1024, 1280, 1536, 1792, 2048, 2304
PF_W = 2432
PF_CHUNK = 640
QKV_W = 768


def _cparams(sem):
    return pltpu.CompilerParams(dimension_semantics=sem, vmem_limit_bytes=VMEM_LIMIT)


def _group_mask(shape, axis, group, idx):
    io = lax.broadcasted_iota(jnp.int32, shape, axis)
    return (io >= idx * group) & (io < (idx + 1) * group)


def _lane_tile(x, n, axis=1):
    return jnp.concatenate([x] * n, axis=axis)


def _silu(x):
    h = 0.5 * x
    return h + h * jnp.tanh(h)


def _inproj_kernel(x_ref, g_ref, w_ref, qkv_ref, qvt_ref, pf_ref):
    x = x_ref[...]
    ms = jnp.mean(x * x, axis=-1, keepdims=True)
    h = (x * lax.rsqrt(ms + EPS) * g_ref[...]).astype(BF16)
    qkv = jnp.dot(h, w_ref[:, :QKV_W], preferred_element_type=F32)
    qkv_ref[...] = qkv.astype(BF16)
    qvt_ref[...] = qkv[:, :2 * DA_WIDTH].T.astype(BF16)
    for lo in range(0, PF_W, PF_CHUNK):
        hi = min(lo + PF_CHUNK, PF_W)
        pf_ref[:, lo:hi] = jnp.dot(h, w_ref[:, QKV_W + lo:QKV_W + hi], preferred_element_type=F32)


def _inproj(x2, gain, w):
    t = x2.shape[0]
    return pl.pallas_call(
        _inproj_kernel,
        grid=(t // TM_PROJ,),
        in_specs=[
            pl.BlockSpec((TM_PROJ, D_MODEL), lambda i: (i, 0)),
            pl.BlockSpec((1, D_MODEL), lambda i: (0, 0)),
            pl.BlockSpec((D_MODEL, QKV_W + PF_W), lambda i: (0, 0)),
        ],
        out_specs=[
            pl.BlockSpec((TM_PROJ, QKV_W), lambda i: (i, 0)),
            pl.BlockSpec((2 * DA_WIDTH, TM_PROJ), lambda i: (0, i)),
            pl.BlockSpec((TM_PROJ, PF_W), lambda i: (i, 0)),
        ],
        out_shape=[
            jax.ShapeDtypeStruct((t, QKV_W), BF16),
            jax.ShapeDtypeStruct((2 * DA_WIDTH, t), BF16),
            jax.ShapeDtypeStruct((t, PF_W), F32),
        ],
        compiler_params=_cparams(("parallel",)),
        name="inproj",
    )(x2, gain, w)


def _attn_kernel(qi_ref, kj_ref, qt_ref, k_ref, vt_ref, bias_ref, gain_ref, lam_ref, o_ref,
                 qm_ref, m_ref, l_ref, acc_ref, s_ref, e_ref, al_ref):
    p = pl.program_id(1)
    qi = qi_ref[p]
    kj = kj_ref[p]
    n_ht = 2 * DA_HEADS

    @pl.when(kj == 0)
    def _init():
        qt = qt_ref[...]
        for ht in range(n_ht):
            qm_ref[ht] = jnp.where(_group_mask(qt.shape, 0, DA_HEAD_DIM, ht), qt, jnp.zeros_like(qt))
        m_ref[...] = jnp.full(m_ref.shape, NEG_BIG, F32)
        l_ref[...] = jnp.zeros(l_ref.shape, F32)
        acc_ref[...] = jnp.zeros(acc_ref.shape, F32)

    def step(with_bias):
        k = k_ref[...]
        def scores(a, buf):
            s_ref[buf] = jnp.dot(k, qm_ref[a], preferred_element_type=F32)

        def softmax(b, buf):
            for c0 in range(0, ATT_T, LANES):
                cols = slice(c0, c0 + LANES)

                def block():
                    blk = s_ref[buf, :, cols]
                    return blk + bias_ref[0, b // 2, :, cols] if with_bias else blk

                m_prev = m_ref[b, :, cols]
                m_new = jnp.maximum(m_prev, jnp.max(block(), axis=0, keepdims=True))
                alpha = jnp.exp2(m_prev - m_new)
                e = jnp.exp2(block() - m_new)
                l_ref[b, :, cols] = alpha * l_ref[b, :, cols] + jnp.sum(e, axis=0, keepdims=True)
                m_ref[b, :, cols] = m_new
                al_ref[buf, :, cols] = alpha
                e_ref[buf, :, cols] = e.astype(BF16)

        def values(c, buf):
            r0 = (c // 4) * LANES
            if not isinstance(r0, int):
                r0 = pl.multiple_of(r0, LANES)
            pv = jnp.dot(vt_ref[pl.ds(r0, LANES), :], e_ref[buf], preferred_element_type=F32)
            acc_ref[c] = acc_ref[c] * al_ref[buf] + pv

        scores(0, 0)
        scores(1, 1)
        softmax(0, 0)

        def trip(u, carry):
            t = 2 * u + 1
            scores(t + 1, 0)
            softmax(t, 1)
            values(t - 1, 0)
            scores(t + 2, 1)
            softmax(t + 1, 0)
            values(t, 1)
            return carry

        for u in range((n_ht - 2) // 2):
            trip(u, 0)
        softmax(n_ht - 1, 1)
        values(n_ht - 2, 0)
        values(n_ht - 1, 1)

    @pl.when(kj >= qi - 1)
    def _near():
        step(True)

    @pl.when(kj < qi - 1)
    def _far():
        step(False)

    @pl.when(kj == qi)
    def _finish():
        lam = lam_ref[...]
        row_lo = lax.broadcasted_iota(jnp.int32, (LANES, ATT_T), 0) < DA_V_DIM
        for pair in range(DA_HEADS // 2):
            halves = []
            for sub in range(2):
                h = 2 * pair + sub
                o = acc_ref[2 * h] * (1.0 / l_ref[2 * h]) - lam * (acc_ref[2 * h + 1] * (1.0 / l_ref[2 * h + 1]))
                own = row_lo if sub == 0 else jnp.logical_not(row_lo)
                ms = jnp.sum(jnp.where(own, o * o, 0.0), axis=0, keepdims=True) * (1.0 / DA_V_DIM)
                halves.append(o * lax.rsqrt(ms + EPS))
            r0 = pair * LANES
            blk = jnp.where(row_lo, halves[0], halves[1]) * gain_ref[r0:r0 + LANES, :]
            o_ref[:, r0:r0 + LANES] = blk.T.astype(BF16)


def _attention(qkv, qvt, bias_tiles, gain, lam, batch, seq):
    nq = seq // ATT_T
    qi_l, kj_l = [], []
    for qi in range(nq):
        for kj in range(qi + 1):
            qi_l.append(qi)
            kj_l.append(kj)
    qi_tbl = jnp.asarray(np.array(qi_l, np.int32))
    kj_tbl = jnp.asarray(np.array(kj_l, np.int32))
    n_pairs = len(qi_l)
    n_ht = 2 * DA_HEADS

    grid_spec = pltpu.PrefetchScalarGridSpec(
        num_scalar_prefetch=2,
        grid=(batch, n_pairs),
        in_specs=[
            pl.BlockSpec((DA_WIDTH, ATT_T), lambda b, p, qi, kj: (0, b * nq + qi[p])),
            pl.BlockSpec((ATT_T, DA_WIDTH), lambda b, p, qi, kj: (b * nq + kj[p], 2)),
            pl.BlockSpec((DA_WIDTH, ATT_T), lambda b, p, qi, kj: (1, b * nq + kj[p])),
            pl.BlockSpec((1, DA_HEADS, ATT_T, ATT_T),
                         lambda b, p, qi, kj: (jnp.minimum(qi[p] - kj[p], 1), 0, 0, 0)),
            pl.BlockSpec((DA_WIDTH, 1), lambda b, p, qi, kj: (0, 0)),
            pl.BlockSpec((1, ATT_T), lambda b, p, qi, kj: (0, 0)),
        ],
        out_specs=pl.BlockSpec((ATT_T, DA_WIDTH), lambda b, p, qi, kj: (b * nq + qi[p], 0)),
        scratch_shapes=[
            pltpu.VMEM((n_ht, DA_WIDTH, ATT_T), BF16),
            pltpu.VMEM((n_ht, 1, ATT_T), F32),
            pltpu.VMEM((n_ht, 1, ATT_T), F32),
            pltpu.VMEM((n_ht, LANES, ATT_T), F32),
            pltpu.VMEM((2, ATT_T, ATT_T), F32),
            pltpu.VMEM((2, ATT_T, ATT_T), BF16),
            pltpu.VMEM((2, 1, ATT_T), F32),
        ],
    )
    return pl.pallas_call(
        _attn_kernel,
        grid_spec=grid_spec,
        out_shape=jax.ShapeDtypeStruct((batch * seq, DA_WIDTH), BF16),
        compiler_params=_cparams(("parallel", "arbitrary")),
        name="diff_attention",
    )(qi_tbl, kj_tbl, qvt, qkv, qvt, bias_tiles, gain, lam)


def _t5_bucket(rel):
    half = REL_BUCKETS // 2
    max_exact = half // 2
    ret = jnp.where(rel > 0, half, 0)
    n = jnp.abs(rel)
    nf = jnp.maximum(n, 1).astype(F32)
    large = max_exact + (jnp.log(nf / max_exact) / math.log(REL_MAX_DIST / max_exact)
                         * (half - max_exact)).astype(jnp.int32)
    large = jnp.minimum(large, half - 1)
    return ret + jnp.where(n < max_exact, n, large)


def _bias_tiles(rel_bias):
    n = ATT_T
    i = jnp.arange(n)[None, :]
    j = jnp.arange(n)[:, None]
    far = rel_bias[_t5_bucket(jnp.int32(-2 * n))].astype(F32)
    tiles = []
    for off in range(2):
        rel = (n - 1 - off * n) - jnp.arange(2 * n)
        bucket = _t5_bucket(rel)
        g = jnp.zeros((DA_HEADS, 2 * n), F32)
        for c in range(REL_BUCKETS):
            g = jnp.where((bucket == c)[None], rel_bias[c].astype(F32)[:, None], g)
        g = (g - far[:, None]) * LOG2E
        skew = jnp.broadcast_to(g[:, None, :], (DA_HEADS, n, 2 * n)).reshape(DA_HEADS, 2 * n * n)
        skew = skew[:, :n * (2 * n - 1)].reshape(DA_HEADS, n, 2 * n - 1)
        b = skew[:, :, n - 1:]
        if off == 0:
            allowed = (j // CHUNK) <= (i // CHUNK)
            b = jnp.where(allowed[None], b, NEG_BIG)
        tiles.append(b)
    return jnp.stack(tiles)


def _cumsum_rows(x):
    n = x.shape[0]
    row = lax.broadcasted_iota(jnp.int32, x.shape, 0)
    sh = 1
    while sh < n:
        x = x + jnp.where(row >= sh, pltpu.roll(x, sh, axis=0), 0.0)
        sh *= 2
    return x


def _expand_heads(v, width, group):
    out = jnp.zeros((v.shape[0], width), F32)
    for h in range(width // group):
        col = jnp.broadcast_to(v[:, h:h + 1], (v.shape[0], width))
        out = jnp.where(_group_mask(out.shape, 1, group, h), col, out)
    return out


def _ssd_kernel(z_ref, xs_ref, bc_ref, dt_ref, cwx_ref, cwb_ref, cbx_ref, cbb_ref, dtb_ref, alog_ref,
                dskip_ref, ng_ref, o_ref, extx_ref, extb_ref, st_ref):
    j = pl.program_id(1)

    @pl.when(j == 0)
    def _reset():
        extx_ref[0:8, :] = jnp.zeros((8, SSM_WIDTH), F32)
        extb_ref[0:8, :] = jnp.zeros((8, 2 * LANES), F32)
        st_ref[...] = jnp.zeros(st_ref.shape, F32)

    extx_ref[8:8 + BLK, :] = xs_ref[...]
    extb_ref[8:8 + BLK, :] = bc_ref[...]
    yx = jnp.broadcast_to(cbx_ref[...], (BLK, SSM_WIDTH))
    yb = jnp.broadcast_to(cbb_ref[...], (BLK, 2 * LANES))
    for w in range(SSM_CONV):
        lo = 8 - (SSM_CONV - 1) + w
        yx = yx + extx_ref[lo:lo + BLK, :] * cwx_ref[w:w + 1, :]
        yb = yb + extb_ref[lo:lo + BLK, :] * cwb_ref[w:w + 1, :]
    tail_x = extx_ref[BLK:BLK + 8, :]
    tail_b = extb_ref[BLK:BLK + 8, :]
    extx_ref[0:8, :] = tail_x
    extb_ref[0:8, :] = tail_b
    xs = _silu(yx)
    bc = _silu(yb)
    bm = bc[:, :LANES]
    cm = bc[:, LANES:]

    dt = jax.nn.softplus(dt_ref[...] + dtb_ref[...])
    a = -jnp.exp(alog_ref[...]) * dt
    a_cs = _cumsum_rows(a)
    a_cs_t = a_cs.T
    a_last = a_cs[BLK - 1:BLK, :]

    dt_full = _expand_heads(dt, SSM_WIDTH, SSM_HEAD_DIM)
    ea_full = _expand_heads(jnp.exp(a_cs), SSM_WIDTH, SSM_HEAD_DIM)
    dec_full = _expand_heads(jnp.exp(a_last - a_cs), SSM_WIDTH, SSM_HEAD_DIM)
    cdec_full = _expand_heads(jnp.exp(a_last), SSM_WIDTH, SSM_HEAD_DIM)

    xdt = xs * dt_full
    xdt_b = xdt.astype(BF16)
    cm_b = cm.astype(BF16)
    bm_b = bm.astype(BF16)

    st = st_ref[...]
    y = jnp.dot(cm_b, st.astype(BF16), preferred_element_type=F32) * ea_full + xs * dskip_ref[...]

    row = lax.broadcasted_iota(jnp.int32, (BLK, BLK), 0)
    colio = lax.broadcasted_iota(jnp.int32, (BLK, BLK), 1)
    causal = colio <= row
    rep = SSM_HEADS // SSM_GROUPS
    lane_lo = lax.broadcasted_iota(jnp.int32, (BLK, LANES), 1) < SSM_HEAD_DIM
    pieces = []
    for g in range(SSM_GROUPS):
        cg = jnp.where(_group_mask(cm_b.shape, 1, SSM_STATE, g), cm_b, jnp.zeros_like(cm_b))
        cb = lax.dot_general(cg, bm_b, (((1,), (1,)), ((), ())), preferred_element_type=F32)
        for pair in range(rep // 2):
            halves = []
            for sub in range(2):
                h = g * rep + pair * 2 + sub
                seg = jnp.broadcast_to(a_cs[:, h:h + 1], (BLK, BLK)) - a_cs_t[h:h + 1, :]
                lmat = jnp.exp(jnp.where(causal, seg, NEG_BIG))
                mh = (cb * lmat).astype(BF16)
                c0 = (h // 2) * LANES
                halves.append(jnp.dot(mh, xdt_b[:, c0:c0 + LANES], preferred_element_type=F32))
            pieces.append(jnp.where(lane_lo, halves[0], halves[1]))
    y = y + jnp.concatenate(pieces, axis=1)

    upd = jnp.dot(bm.T.astype(BF16), (xdt * dec_full).astype(BF16), preferred_element_type=F32)
    r_io = lax.broadcasted_iota(jnp.int32, upd.shape, 0) // SSM_STATE
    c_io = lax.broadcasted_iota(jnp.int32, upd.shape, 1) // (SSM_HEAD_DIM * rep)
    st_ref[...] = st * cdec_full + jnp.where(r_io == c_io, upd, 0.0)

    y = y * _silu(z_ref[...])
    gw = SSM_WIDTH // SSM_GROUPS
    for g in range(SSM_GROUPS):
        yg = y[:, g * gw:(g + 1) * gw]
        ms = jnp.mean(yg * yg, axis=-1, keepdims=True)
        o_ref[:, g * gw:(g + 1) * gw] = (yg * lax.rsqrt(ms + EPS) * ng_ref[:, g * gw:(g + 1) * gw]).astype(BF16)


def _ssd(pf, cwx, cwb, cbx, cbb, dtb, alog, dskip, ng, batch, seq):
    nb = seq // BLK
    row = lambda w, c: pl.BlockSpec((BLK, w), lambda b, j: (b * nb + j, c))
    const = lambda r, w: pl.BlockSpec((r, w), lambda b, j: (0, 0))
    return pl.pallas_call(
        _ssd_kernel,
        grid=(batch, nb),
        in_specs=[
            row(SSM_WIDTH, C_Z // SSM_WIDTH), row(SSM_WIDTH, C_XS // SSM_WIDTH), row(2 * LANES, C_BC // (2 * LANES)),
            row(LANES, C_DT // LANES),
            const(SSM_CONV, SSM_WIDTH), const(SSM_CONV, 2 * LANES), const(1, SSM_WIDTH), const(1, 2 * LANES),
            const(1, LANES), const(1, LANES), const(1, SSM_WIDTH), const(1, SSM_WIDTH),
        ],
        out_specs=pl.BlockSpec((BLK, SSM_WIDTH), lambda b, j: (b * nb + j, 0)),
        out_shape=jax.ShapeDtypeStruct((batch * seq, SSM_WIDTH), BF16),
        scratch_shapes=[
            pltpu.VMEM((BLK + 8, SSM_WIDTH), F32),
            pltpu.VMEM((BLK + 8, 2 * LANES), F32),
            pltpu.VMEM((SSM_GROUPS * SSM_STATE, SSM_WIDTH), F32),
        ],
        compiler_params=_cparams(("parallel", "arbitrary")),
        name="ssd_mixer",
    )(pf, pf, pf, pf, cwx, cwb, cbx, cbb, dtb, alog, dskip, ng)


def _rotary(u, cos, sin_signed):
    first = (lax.broadcasted_iota(jnp.int32, u.shape, 1) % RET_KEY_DIM) < (RET_KEY_DIM // 2)
    half = RET_KEY_DIM // 2
    swapped = jnp.where(first, pltpu.roll(u, u.shape[1] - half, axis=1), pltpu.roll(u, half, axis=1))
    return u * cos + swapped * sin_signed


def _ret_kernel(q_ref, k_ref, v_ref, g_ref, cos_ref, sin_ref, dmat_ref, qdec_ref, kdec_ref, gdec_ref,
                bd_ref, gain_ref, o_ref, st_ref):
    j = pl.program_id(1)

    @pl.when(j == 0)
    def _reset():
        st_ref[...] = jnp.zeros(st_ref.shape, F32)

    cos = _lane_tile(cos_ref[...], RET_WIDTH // LANES)
    sin = _lane_tile(sin_ref[...], RET_WIDTH // LANES)
    q = _rotary(q_ref[...], cos, sin)
    k = _rotary(k_ref[...], cos, sin) * (RET_KEY_DIM ** -0.5)
    v_b = v_ref[...].astype(BF16)
    q_b = q.astype(BF16)
    k_b = k.astype(BF16)

    st = st_ref[...]
    o = jnp.dot((q * qdec_ref[...]).astype(BF16), st.astype(BF16), preferred_element_type=F32)
    for h in range(RET_HEADS):
        cm = _group_mask(q_b.shape, 1, RET_KEY_DIM, h)
        qh = jnp.where(cm, q_b, jnp.zeros_like(q_b))
        s = lax.dot_general(qh, k_b, (((1,), (1,)), ((), ())), preferred_element_type=F32) * dmat_ref[h]
        oh = jnp.dot(s.astype(BF16), v_b, preferred_element_type=F32)
        o = o + jnp.where(cm, oh, 0.0)

    upd = jnp.dot((k * kdec_ref[...]).T.astype(BF16), v_b, preferred_element_type=F32)
    st_ref[...] = st * gdec_ref[...] + upd * bd_ref[...]

    out = jnp.zeros(o.shape, F32)
    for h in range(RET_HEADS):
        cm = _group_mask(o.shape, 1, RET_KEY_DIM, h)
        ms = jnp.sum(jnp.where(cm, o * o, 0.0), axis=-1, keepdims=True) * (1.0 / RET_KEY_DIM)
        out = out + jnp.where(cm, o * lax.rsqrt(ms + EPS), 0.0)
    o_ref[...] = (_silu(g_ref[...]) * (out * gain_ref[...])).astype(BF16)


def _retention(pf, cos, sin, dmat, qdec, kdec, gdec, bd, gain, batch, seq):
    nb = seq // BLK
    row = lambda c: pl.BlockSpec((BLK, RET_WIDTH), lambda b, j: (b * nb + j, c))
    const2 = pl.BlockSpec((BLK, RET_WIDTH), lambda b, j: (0, 0))
    return pl.pallas_call(
        _ret_kernel,
        grid=(batch, nb),
        in_specs=[
            row(C_RQ // RET_WIDTH), row(C_RK // RET_WIDTH), row(C_RV // RET_WIDTH), row(C_RG // RET_WIDTH),
            pl.BlockSpec((BLK, LANES), lambda b, j: (j, 0)),
            pl.BlockSpec((BLK, LANES), lambda b, j: (j, 0)),
            pl.BlockSpec((RET_HEADS, BLK, BLK), lambda b, j: (0, 0, 0)),
            const2, const2, const2, const2,
            pl.BlockSpec((1, RET_WIDTH), lambda b, j: (0, 0)),
        ],
        out_specs=pl.BlockSpec((BLK, RET_WIDTH), lambda b, j: (b * nb + j, 0)),
        out_shape=jax.ShapeDtypeStruct((batch * seq, RET_WIDTH), BF16),
        scratch_shapes=[pltpu.VMEM((RET_WIDTH, RET_WIDTH), F32)],
        compiler_params=_cparams(("parallel", "arbitrary")),
        name="retention",
    )(pf, pf, pf, pf, cos, sin, dmat, qdec, kdec, gdec, bd, gain)


def _retention_tables(seq):
    f32 = np.float32
    inv = (f32(1.0) / (f32(ROPE_BASE) ** (np.arange(0, RET_KEY_DIM, 2, dtype=f32) / f32(RET_KEY_DIM)))).astype(f32)
    ang = (np.arange(seq, dtype=f32)[:, None] * inv[None, :]).astype(f32)
    cos_h = np.concatenate([np.cos(ang), np.cos(ang)], axis=-1)
    sin_h = np.concatenate([-np.sin(ang), np.sin(ang)], axis=-1)
    cos = np.tile(cos_h, (1, 2)).astype(f32)
    sin = np.tile(sin_h, (1, 2)).astype(f32)
    log_gamma = np.log1p(-np.power(f32(2.0), f32(-5.0) - np.arange(RET_HEADS, dtype=f32))).astype(f32)
    idx = np.arange(BLK, dtype=f32)
    ii = np.arange(BLK)
    same_or_earlier_chunk = (ii[None, :] // CHUNK) <= (ii[:, None] // CHUNK)
    dmat = np.exp(log_gamma[:, None, None] * np.abs(idx[:, None] - idx[None, :]))
    dmat = np.where(same_or_earlier_chunk[None], dmat, 0.0).astype(f32)
    lg_cols = np.repeat(log_gamma, RET_KEY_DIM)
    qdec = np.exp(lg_cols[None, :] * (idx + 1.0)[:, None]).astype(f32)
    kdec = np.exp(lg_cols[None, :] * (BLK - 1.0 - idx)[:, None]).astype(f32)
    head_of = np.arange(RET_WIDTH) // RET_KEY_DIM
    bd = (head_of[:, None] == head_of[None, :]).astype(f32)
    gdec = (np.exp(lg_cols * f32(BLK))[:, None] * bd).astype(f32)
    return tuple(jnp.asarray(a) for a in (cos, sin, dmat, qdec, kdec, gdec, bd))


def _outproj_router_kernel(x_ref, a_ref, b_ref, c_ref, w_ref, g_ref, rw_ref, xo_ref, h_ref, rinfo_ref):
    acc = x_ref[...]
    acc = acc + jnp.dot(a_ref[...], w_ref[0:DA_WIDTH, :], preferred_element_type=F32)
    acc = acc + jnp.dot(b_ref[...], w_ref[DA_WIDTH:DA_WIDTH + SSM_WIDTH, :], preferred_element_type=F32)
    acc = acc + jnp.dot(c_ref[...], w_ref[DA_WIDTH + SSM_WIDTH:, :], preferred_element_type=F32)
    xo_ref[...] = acc
    ms = jnp.mean(acc * acc, axis=-1, keepdims=True)
    hf = acc * lax.rsqrt(ms + EPS) * g_ref[...]
    h = hf.astype(BF16)
    h_ref[...] = hf
    logits = jnp.dot(h, rw_ref[...], preferred_element_type=F32)
    lane = lax.broadcasted_iota(jnp.int32, logits.shape, 1).astype(F32)
    logits = jnp.where(lane < N_EXPERTS, logits, NEG_BIG)
    v1 = jnp.max(logits, axis=-1, keepdims=True)
    i1 = jnp.min(jnp.where(logits == v1, lane, float(LANES)), axis=-1, keepdims=True)
    rest_l = jnp.where(lane == i1, NEG_BIG, logits)
    v2 = jnp.max(rest_l, axis=-1, keepdims=True)
    i2 = jnp.min(jnp.where(rest_l == v2, lane, float(LANES)), axis=-1, keepdims=True)
    e2 = jnp.exp(v2 - v1)
    g1 = 1.0 / (1.0 + e2)
    g2 = e2 * g1
    rinfo_ref[...] = (jnp.where(lane == 0.0, g1, 0.0) + jnp.where(lane == 1.0, g2, 0.0)
                      + jnp.where(lane == 2.0, i1, 0.0) + jnp.where(lane == 3.0, i2, 0.0))


def _outproj_router(x2, oa, ob, oc, w, gain, router_w):
    t = x2.shape[0]
    tok = lambda w_: pl.BlockSpec((TM_PROJ, w_), lambda i: (i, 0))
    return pl.pallas_call(
        _outproj_router_kernel,
        grid=(t // TM_PROJ,),
        in_specs=[tok(D_MODEL), tok(DA_WIDTH), tok(SSM_WIDTH), tok(RET_WIDTH),
                  pl.BlockSpec((D_MODEL, D_MODEL), lambda i: (0, 0)),
                  pl.BlockSpec((1, D_MODEL), lambda i: (0, 0)),
                  pl.BlockSpec((D_MODEL, LANES), lambda i: (0, 0))],
        out_specs=[tok(D_MODEL), tok(D_MODEL), tok(LANES)],
        out_shape=[jax.ShapeDtypeStruct((t, D_MODEL), F32), jax.ShapeDtypeStruct((t, D_MODEL), F32),
                   jax.ShapeDtypeStruct((t, LANES), F32)],
        compiler_params=_cparams(("parallel",)),
        name="outproj_router",
    )(x2, oa, ob, oc, w, gain, router_w)


def _ffn_kernel(x_ref, a_ref, b_ref, c_ref, wo_ref, g_ref, wg_ref, wu_ref, wd_ref, o_ref, acc_ref, h_ref):
    f = pl.program_id(1)

    @pl.when(f == 0)
    def _init():
        acc = x_ref[...]
        acc = acc + jnp.dot(a_ref[...], wo_ref[0:DA_WIDTH, :], preferred_element_type=F32)
        acc = acc + jnp.dot(b_ref[...], wo_ref[DA_WIDTH:DA_WIDTH + SSM_WIDTH, :], preferred_element_type=F32)
        acc = acc + jnp.dot(c_ref[...], wo_ref[DA_WIDTH + SSM_WIDTH:, :], preferred_element_type=F32)
        acc_ref[...] = acc
        ms = jnp.mean(acc * acc, axis=-1, keepdims=True)
        h_ref[...] = (acc * lax.rsqrt(ms + EPS) * g_ref[...]).astype(BF16)

    h = h_ref[...]
    gate = jnp.dot(h, wg_ref[0], preferred_element_type=F32)
    up = jnp.dot(h, wu_ref[0], preferred_element_type=F32)
    act = (_silu(gate) * up).astype(BF16)
    acc_ref[...] += jnp.dot(act, wd_ref[...], preferred_element_type=F32)

    @pl.when(f == pl.num_programs(1) - 1)
    def _done():
        o_ref[...] = acc_ref[...]


def _outproj_dense_ffn(x2, oa, ob, oc, w_o, gain, wg, wu, wd):
    t = x2.shape[0]
    nf = wg.shape[0]
    tok = lambda w_: pl.BlockSpec((TM_FFN, w_), lambda i, f: (i, 0))
    return pl.pallas_call(
        _ffn_kernel,
        grid=(t // TM_FFN, nf),
        in_specs=[
            tok(D_MODEL), tok(DA_WIDTH), tok(SSM_WIDTH), tok(RET_WIDTH),
            pl.BlockSpec((D_MODEL, D_MODEL), lambda i, f: (0, 0)),
            pl.BlockSpec((1, D_MODEL), lambda i, f: (0, 0)),
            pl.BlockSpec((1, D_MODEL, TF_FFN), lambda i, f: (f, 0, 0)),
            pl.BlockSpec((1, D_MODEL, TF_FFN), lambda i, f: (f, 0, 0)),
            pl.BlockSpec((TF_FFN, D_MODEL), lambda i, f: (f, 0)),
        ],
        out_specs=pl.BlockSpec((TM_FFN, D_MODEL), lambda i, f: (i, 0)),
        out_shape=jax.ShapeDtypeStruct((t, D_MODEL), F32),
        scratch_shapes=[pltpu.VMEM((TM_FFN, D_MODEL), F32), pltpu.VMEM((TM_FFN, D_MODEL), BF16)],
        compiler_params=_cparams(("parallel", "arbitrary")),
        name="outproj_dense_swiglu",
    )(x2, oa, ob, oc, w_o, gain, wg, wu, wd)


def _issue_row_gather(src_hbm, idx_ref, idx_base, dst, sem, n_rows):
    def body(r, carry):
        tok = idx_ref[idx_base + r]
        pltpu.make_async_copy(src_hbm.at[pl.ds(tok, 1)], dst.at[pl.ds(r, 1)], sem).start()
        return carry
    lax.fori_loop(0, n_rows, body, 0, unroll=8)


def _wait_row_gather(src_hbm, dst, sem, n_rows):
    pltpu.make_async_copy(src_hbm.at[pl.ds(0, n_rows)], dst, sem).wait()


def _moe_ffn_kernel(te_ref, nv_ref, tb_ref, rt_ref, h_hbm, wg_ref, wu_ref, wd_ref, y_ref, hbuf, xb_ref, acc_ref, sem):
    i = pl.program_id(0)
    f = pl.program_id(1)
    n_valid = nv_ref[0]
    valid = i < n_valid
    slot = i % 2

    @pl.when((f == 0) & (i == 0) & valid)
    def _first():
        _issue_row_gather(h_hbm, rt_ref, tb_ref[0], hbuf.at[0], sem.at[0], TR_MOE)

    @pl.when((f == 0) & valid)
    def _arrive():
        _wait_row_gather(h_hbm, hbuf.at[slot], sem.at[slot], TR_MOE)
        xb_ref[...] = hbuf[slot].astype(BF16)

    def compute(prefetch_next):
        if prefetch_next:
            rows_per_step = TR_MOE // (FFN_EXPERT // TF_MOE)
            r0 = f * rows_per_step
            src0 = tb_ref[i + 1] + r0
            for j in range(rows_per_step):
                tok = rt_ref[src0 + j]
                pltpu.make_async_copy(h_hbm.at[pl.ds(tok, 1)], hbuf.at[1 - slot, pl.ds(r0 + j, 1)],
                                      sem.at[1 - slot]).start()
        xb = xb_ref[...]
        gate = jnp.dot(xb, wg_ref[0].astype(BF16), preferred_element_type=F32)
        up = jnp.dot(xb, wu_ref[0].astype(BF16), preferred_element_type=F32)
        act = (_silu(gate) * up).astype(BF16)
        contrib = jnp.dot(act, wd_ref[0].astype(BF16), preferred_element_type=F32)

        @pl.when(f == 0)
        def _():
            acc_ref[...] = contrib

        @pl.when(f > 0)
        def _():
            acc_ref[...] += contrib

    has_next = i + 1 < n_valid

    @pl.when(valid & has_next)
    def _compute_and_prefetch():
        compute(True)

    @pl.when(valid & jnp.logical_not(has_next))
    def _compute_last():
        compute(False)

    @pl.when(f == pl.num_programs(1) - 1)
    def _store():
        y_ref[...] = jnp.where(valid, acc_ref[...], 0.0)


def _moe_ffn(h, tile_expert, n_valid, tile_base, sorted_token, wg, wu, wd):
    n_tiles = tile_expert.shape[0]
    nf = wg.shape[2] // TF_MOE

    def w_in_map(i, f, te, nv, tb, rt):
        return (te[i], 0, jnp.where(i < nv[0], f, nf - 1))

    def w_out_map(i, f, te, nv, tb, rt):
        return (te[i], jnp.where(i < nv[0], f, nf - 1), 0)

    grid_spec = pltpu.PrefetchScalarGridSpec(
        num_scalar_prefetch=4,
        grid=(n_tiles, nf),
        in_specs=[
            pl.BlockSpec(memory_space=pl.ANY),
            pl.BlockSpec((1, D_MODEL, TF_MOE), w_in_map),
            pl.BlockSpec((1, D_MODEL, TF_MOE), w_in_map),
            pl.BlockSpec((1, TF_MOE, D_MODEL), w_out_map),
        ],
        out_specs=pl.BlockSpec((TR_MOE, D_MODEL), lambda i, f, te, nv, tb, rt: (i, 0)),
        scratch_shapes=[
            pltpu.VMEM((2, TR_MOE, D_MODEL), F32),
            pltpu.VMEM((TR_MOE, D_MODEL), BF16),
            pltpu.VMEM((TR_MOE, D_MODEL), F32),
            pltpu.SemaphoreType.DMA((2,)),
        ],
    )
    return pl.pallas_call(
        _moe_ffn_kernel,
        grid_spec=grid_spec,
        out_shape=jax.ShapeDtypeStruct((n_tiles * TR_MOE, D_MODEL), F32),
        compiler_params=_cparams(("arbitrary", "arbitrary")),
        name="moe_expert_ffn",
    )(tile_expert, n_valid, tile_base, sorted_token, h, wg, wu, wd)


def _moe_combine_kernel(tr_ref, x_ref, rinfo_ref, y_hbm, fg_ref, o_ref, ybuf, sem):
    i = pl.program_id(0)
    slot = i % 2
    n_rows = TOP_K * TM_COMB

    @pl.when(i == 0)
    def _first():
        _issue_row_gather(y_hbm, tr_ref, 0, ybuf.at[0], sem.at[0], n_rows)

    _wait_row_gather(y_hbm, ybuf.at[slot], sem.at[slot], n_rows)

    @pl.when(i + 1 < pl.num_programs(0))
    def _prefetch():
        src0 = (i + 1) * n_rows
        dst = ybuf.at[1 - slot]
        for r in range(n_rows):
            pltpu.make_async_copy(y_hbm.at[pl.ds(tr_ref[src0 + r], 1)], dst.at[pl.ds(r, 1)], sem.at[1 - slot]).start()

    rinfo = rinfo_ref[...]
    g1 = jnp.broadcast_to(rinfo[:, 0:1], (TM_COMB, D_MODEL))
    g2 = jnp.broadcast_to(rinfo[:, 1:2], (TM_COMB, D_MODEL))
    y = x_ref[...] + g1 * ybuf[slot, 0:TM_COMB, :] + g2 * ybuf[slot, TM_COMB:n_rows, :]
    ms = jnp.mean(y * y, axis=-1, keepdims=True)
    o_ref[...] = y * lax.rsqrt(ms + EPS) * fg_ref[...]


def _moe_combine(x2, rinfo, y_rows, tok_rows, final_gain):
    t = x2.shape[0]
    grid_spec = pltpu.PrefetchScalarGridSpec(
        num_scalar_prefetch=1,
        grid=(t // TM_COMB,),
        in_specs=[
            pl.BlockSpec((TM_COMB, D_MODEL), lambda i, tr: (i, 0)),
            pl.BlockSpec((TM_COMB, LANES), lambda i, tr: (i, 0)),
            pl.BlockSpec(memory_space=pl.ANY),
            pl.BlockSpec((1, D_MODEL), lambda i, tr: (0, 0)),
        ],
        out_specs=pl.BlockSpec((TM_COMB, D_MODEL), lambda i, tr: (i, 0)),
        scratch_shapes=[
            pltpu.VMEM((2, TOP_K * TM_COMB, D_MODEL), F32),
            pltpu.SemaphoreType.DMA((2,)),
        ],
    )
    return pl.pallas_call(
        _moe_combine_kernel,
        grid_spec=grid_spec,
        out_shape=jax.ShapeDtypeStruct((t, D_MODEL), F32),
        compiler_params=_cparams(("arbitrary",)),
        name="moe_combine_norm",
    )(tok_rows, x2, rinfo, y_rows, final_gain)


def _routing_tables(rinfo):
    t = rinfo.shape[0]
    n_assign = TOP_K * t
    n_tiles = -(-n_assign // TR_MOE) + N_EXPERTS
    flat_e = rinfo[:, 2:2 + TOP_K].astype(jnp.int32).reshape(-1)
    onehot = (flat_e[:, None] == jnp.arange(N_EXPERTS)[None, :]).astype(jnp.int32)
    blocks = onehot.astype(BF16).reshape(n_assign // LANES, LANES, N_EXPERTS)
    tri = (jnp.arange(LANES)[:, None] >= jnp.arange(LANES)[None, :]).astype(BF16)
    within = jnp.einsum("ij,bjk->bik", tri, blocks, preferred_element_type=F32)
    block_end = jnp.cumsum(within[:, -1, :], axis=0)
    csum = (within + (block_end - within[:, -1, :])[:, None, :]).reshape(n_assign, N_EXPERTS).astype(jnp.int32)
    rank = jnp.sum((csum - onehot) * onehot, axis=1)
    counts = csum[-1]
    tiles_per_e = (counts + TR_MOE - 1) // TR_MOE
    tile_end = jnp.cumsum(tiles_per_e)
    first_tile = tile_end - tiles_per_e
    dest = (first_tile * TR_MOE)[flat_e] + rank
    order = jnp.sort(flat_e * n_assign + jnp.arange(n_assign, dtype=jnp.int32))
    sorted_token = jnp.concatenate([(order % n_assign) // TOP_K, jnp.zeros((TR_MOE,), jnp.int32)])
    start = jnp.cumsum(counts) - counts
    n_valid = tile_end[-1:].astype(jnp.int32)
    tile_ids = jnp.arange(n_tiles)
    tile_expert = jnp.sum(tile_ids[:, None] >= tile_end[None, :], axis=1).astype(jnp.int32)
    last_e = jnp.sum(n_valid[0] - 1 >= tile_end).astype(jnp.int32)
    tile_expert = jnp.where(tile_ids < n_valid[0], tile_expert, last_e)
    tile_base = start[tile_expert] + (tile_ids - first_tile[tile_expert]) * TR_MOE
    tile_base = jnp.clip(tile_base, 0, n_assign).astype(jnp.int32)
    tok_rows = dest.reshape(t // TM_COMB, TM_COMB, TOP_K).transpose(0, 2, 1).reshape(-1).astype(jnp.int32)
    return tile_expert, n_valid, tile_base, sorted_token, tok_rows


def _permute_w_in(w):
    sizes = (256, 256, 256, 512, 768, 8, 256, 256, 256, 256)
    offs = np.concatenate([[0], np.cumsum(sizes)])
    part = lambda n: w[:, offs[n]:offs[n + 1]]
    dt_pad = jnp.zeros((w.shape[0], LANES - SSM_HEADS), w.dtype)
    cols = [part(0) * (DA_HEAD_DIM ** -0.5 * LOG2E), part(2), part(1), part(3), part(4),
            part(6), part(7), part(8), part(9), part(5), dt_pad]
    return jnp.concatenate(cols, axis=1).astype(BF16)


def _column_tiles(w, tile):
    *lead, k, n = w.shape
    return jnp.moveaxis(w.reshape(*lead, k, n // tile, tile), -2, -3)


def _pad_lanes(v, fill=0.0):
    return jnp.concatenate([v, jnp.full((LANES - v.shape[0],), fill, v.dtype)])[None, :]


def kernel(x, w_in, w_out, attn_norm, ffn_norm, final_norm, rel_bias, lambda_q1, lambda_k1, lambda_q2,
           lambda_k2, da_head_norm, conv_w, conv_b, dt_bias, a_log, d_skip, ssm_norm, ret_head_norm,
           w_gate, w_up, w_down, router_w, e_gate, e_up, e_down):
    batch, seq, _ = x.shape
    assert DEPTH % 2 == 0, "the final RMSNorm is fused into the last (expert) layer's combine kernel"
    assert seq % max(ATT_T, TM_PROJ, BLK) == 0 and (batch * seq) % TM_FFN == 0
    x2 = x.reshape(batch * seq, D_MODEL)
    bias_tiles = _bias_tiles(rel_bias)
    ret_tabs = _retention_tables(seq)

    for layer in range(DEPTH):
        lam_init = 0.8 - 0.6 * math.exp(-0.3 * layer)
        lam = (jnp.exp(jnp.sum(lambda_q1[layer] * lambda_k1[layer]))
               - jnp.exp(jnp.sum(lambda_q2[layer] * lambda_k2[layer])) + lam_init).astype(F32)
        lam = jnp.full((1, ATT_T), lam, F32)

        qkv, qvt, pf = _inproj(x2, attn_norm[layer][None, :], _permute_w_in(w_in[layer]))

        da_gain = (jnp.tile(da_head_norm[layer], DA_HEADS) * (1.0 - lam_init))[:, None]
        out_a = _attention(qkv, qvt, bias_tiles, da_gain, lam, batch, seq)

        cw, cb = conv_w[layer], conv_b[layer]
        out_b = _ssd(pf, cw[:, :SSM_WIDTH], cw[:, SSM_WIDTH:], cb[None, :SSM_WIDTH], cb[None, SSM_WIDTH:],
                     _pad_lanes(dt_bias[layer]), _pad_lanes(a_log[layer]),
                     jnp.repeat(d_skip[layer], SSM_HEAD_DIM)[None, :], ssm_norm[layer][None, :], batch, seq)

        ret_gain = jnp.tile(ret_head_norm[layer], RET_HEADS)[None, :]
        out_c = _retention(pf, *ret_tabs, ret_gain, batch, seq)

        w_o = w_out[layer].astype(BF16)
        if layer % 2 == 0:
            i = layer // 2
            pad = FFN_PAD - FFN_DENSE
            wg = _column_tiles(jnp.pad(w_gate[i], ((0, 0), (0, pad))).astype(BF16), TF_FFN)
            wu = _column_tiles(jnp.pad(w_up[i], ((0, 0), (0, pad))).astype(BF16), TF_FFN)
            wd = jnp.pad(w_down[i], ((0, pad), (0, 0))).astype(BF16)
            x2 = _outproj_dense_ffn(x2, out_a, out_b, out_c, w_o, ffn_norm[layer][None, :], wg, wu, wd)
        else:
            i = layer // 2
            rw = jnp.pad(router_w[i], ((0, 0), (0, LANES - N_EXPERTS))).astype(BF16)
            x2, h, rinfo = _outproj_router(x2, out_a, out_b, out_c, w_o, ffn_norm[layer][None, :], rw)
            tile_expert, n_valid, tile_base, sorted_token, tok_rows = _routing_tables(rinfo)
            y_rows = _moe_ffn(h, tile_expert, n_valid, tile_base, sorted_token,
                              e_gate[i], e_up[i], e_down[i])
            x2 = _moe_combine(x2, rinfo, y_rows, tok_rows, final_norm[None, :])
    return x2.reshape(batch, seq, D_MODEL)
```

```python
import math

import jax
import jax.numpy as jnp
import numpy as np
from jax import lax
from jax.experimental import pallas as pl
from jax.experimental.pallas import tpu as pltpu

F32 = jnp.float32
BF16 = jnp.bfloat16

D_MODEL = 1024
DEPTH = 2
CHUNK = 64
EPS = 1e-6
DA_HEADS = 4
DA_HEAD_DIM = 32
DA_V_DIM = 64
DA_WIDTH = 256
SSM_HEADS = 8
SSM_HEAD_DIM = 64
SSM_WIDTH = 512
SSM_STATE = 64
SSM_GROUPS = 2
SSM_CONV = 4
RET_HEADS = 4
RET_KEY_DIM = 64
RET_WIDTH = 256
ROPE_BASE = 10000.0
REL_BUCKETS = 32
REL_MAX_DIST = 128
FFN_DENSE = 2752
N_EXPERTS = 8
FFN_EXPERT = 3584

LANES = 128
VMEM_LIMIT = 48 * 1024 * 1024
NEG_BIG = -1e30

TM_PROJ = 512
ATT_T = 512
LOG2E = math.log2(math.e)
BLK = 256
FFN_PAD = 2816
TM_FFN = 512
TF_FFN = 1408
TF_MOE = 512
TR_MOE = 672
TM_COMB = 256
TOP_K = 2

C_Z, C_XS, C_BC, C_RQ, C_RK, C_RV, C_RG, C_DT = 0, 512, 1024, 1280, 1536, 1792, 2048, 2304
PF_W = 2432
PF_CHUNK = 640
QKV_W = 768


def _cparams(sem):
    return pltpu.CompilerParams(dimension_semantics=sem, vmem_limit_bytes=VMEM_LIMIT)


def _group_mask(shape, axis, group, idx):
    io = lax.broadcasted_iota(jnp.int32, shape, axis)
    return (io >= idx * group) & (io < (idx + 1) * group)


def _lane_tile(x, n, axis=1):
    return jnp.concatenate([x] * n, axis=axis)


def _silu(x):
    h = 0.5 * x
    return h + h * jnp.tanh(h)


def _inproj_kernel(x_ref, g_ref, w_ref, qkv_ref, qvt_ref, pf_ref):
    x = x_ref[...]
    ms = jnp.mean(x * x, axis=-1, keepdims=True)
    h = (x * lax.rsqrt(ms + EPS) * g_ref[...]).astype(BF16)
    qkv = jnp.dot(h, w_ref[:, :QKV_W], preferred_element_type=F32)
    qkv_ref[...] = qkv.astype(BF16)
    qvt_ref[...] = qkv[:, :2 * DA_WIDTH].T.astype(BF16)
    for lo in range(0, PF_W, PF_CHUNK):
        hi = min(lo + PF_CHUNK, PF_W)
        pf_ref[:, lo:hi] = jnp.dot(h, w_ref[:, QKV_W + lo:QKV_W + hi], preferred_element_type=F32)


def _inproj(x2, gain, w):
    t = x2.shape[0]
    return pl.pallas_call(
        _inproj_kernel,
        grid=(t // TM_PROJ,),
        in_specs=[
            pl.BlockSpec((TM_PROJ, D_MODEL), lambda i: (i, 0)),
            pl.BlockSpec((1, D_MODEL), lambda i: (0, 0)),
            pl.BlockSpec((D_MODEL, QKV_W + PF_W), lambda i: (0, 0)),
        ],
        out_specs=[
            pl.BlockSpec((TM_PROJ, QKV_W), lambda i: (i, 0)),
            pl.BlockSpec((2 * DA_WIDTH, TM_PROJ), lambda i: (0, i)),
            pl.BlockSpec((TM_PROJ, PF_W), lambda i: (i, 0)),
        ],
        out_shape=[
            jax.ShapeDtypeStruct((t, QKV_W), BF16),
            jax.ShapeDtypeStruct((2 * DA_WIDTH, t), BF16),
            jax.ShapeDtypeStruct((t, PF_W), F32),
        ],
        compiler_params=_cparams(("parallel",)),
        name="inproj",
    )(x2, gain, w)


def _attn_kernel(qi_ref, kj_ref, qt_ref, k_ref, vt_ref, bias_ref, gain_ref, lam_ref, o_ref,
                 qm_ref, m_ref, l_ref, acc_ref, s_ref, e_ref, al_ref):
    p = pl.program_id(1)
    qi = qi_ref[p]
    kj = kj_ref[p]
    n_ht = 2 * DA_HEADS

    @pl.when(kj == 0)
    def _init():
        qt = qt_ref[...]
        for ht in range(n_ht):
            qm_ref[ht] = jnp.where(_group_mask(qt.shape, 0, DA_HEAD_DIM, ht), qt, jnp.zeros_like(qt))
        m_ref[...] = jnp.full(m_ref.shape, NEG_BIG, F32)
        l_ref[...] = jnp.zeros(l_ref.shape, F32)
        acc_ref[...] = jnp.zeros(acc_ref.shape, F32)

    def step(with_bias):
        k = k_ref[...]
        def scores(a, buf):
            s_ref[buf] = jnp.dot(k, qm_ref[a], preferred_element_type=F32)

        def softmax(b, buf):
            for c0 in range(0, ATT_T, LANES):
                cols = slice(c0, c0 + LANES)

                def block():
                    blk = s_ref[buf, :, cols]
                    return blk + bias_ref[0, b // 2, :, cols] if with_bias else blk

                m_prev = m_ref[b, :, cols]
                m_new = jnp.maximum(m_prev, jnp.max(block(), axis=0, keepdims=True))
                alpha = jnp.exp2(m_prev - m_new)
                e = jnp.exp2(block() - m_new)
                l_ref[b, :, cols] = alpha * l_ref[b, :, cols] + jnp.sum(e, axis=0, keepdims=True)
                m_ref[b, :, cols] = m_new
                al_ref[buf, :, cols] = alpha
                e_ref[buf, :, cols] = e.astype(BF16)

        def values(c, buf):
            r0 = (c // 4) * LANES
            if not isinstance(r0, int):
                r0 = pl.multiple_of(r0, LANES)
            pv = jnp.dot(vt_ref[pl.ds(r0, LANES), :], e_ref[buf], preferred_element_type=F32)
            acc_ref[c] = acc_ref[c] * al_ref[buf] + pv

        scores(0, 0)
        scores(1, 1)
        softmax(0, 0)

        def trip(u, carry):
            t = 2 * u + 1
            scores(t + 1, 0)
            softmax(t, 1)
            values(t - 1, 0)
            scores(t + 2, 1)
            softmax(t + 1, 0)
            values(t, 1)
            return carry

        for u in range((n_ht - 2) // 2):
            trip(u, 0)
        softmax(n_ht - 1, 1)
        values(n_ht - 2, 0)
        values(n_ht - 1, 1)

    @pl.when(kj >= qi - 1)
    def _near():
        step(True)

    @pl.when(kj < qi - 1)
    def _far():
        step(False)

    @pl.when(kj == qi)
    def _finish():
        lam = lam_ref[...]
        row_lo = lax.broadcasted_iota(jnp.int32, (LANES, ATT_T), 0) < DA_V_DIM
        for pair in range(DA_HEADS // 2):
            halves = []
            for sub in range(2):
                h = 2 * pair + sub
                o = acc_ref[2 * h] * (1.0 / l_ref[2 * h]) - lam * (acc_ref[2 * h + 1] * (1.0 / l_ref[2 * h + 1]))
                own = row_lo if sub == 0 else jnp.logical_not(row_lo)
                ms = jnp.sum(jnp.where(own, o * o, 0.0), axis=0, keepdims=True) * (1.0 / DA_V_DIM)
                halves.append(o * lax.rsqrt(ms + EPS))
            r0 = pair * LANES
            blk = jnp.where(row_lo, halves[0], halves[1]) * gain_ref[r0:r0 + LANES, :]
            o_ref[:, r0:r0 + LANES] = blk.T.astype(BF16)


def _attention(qkv, qvt, bias_tiles, gain, lam, batch, seq):
    nq = seq // ATT_T
    qi_l, kj_l = [], []
    for qi in range(nq):
        for kj in range(qi + 1):
            qi_l.append(qi)
            kj_l.append(kj)
    qi_tbl = jnp.asarray(np.array(qi_l, np.int32))
    kj_tbl = jnp.asarray(np.array(kj_l, np.int32))
    n_pairs = len(qi_l)
    n_ht = 2 * DA_HEADS

    grid_spec = pltpu.PrefetchScalarGridSpec(
        num_scalar_prefetch=2,
        grid=(batch, n_pairs),
        in_specs=[
            pl.BlockSpec((DA_WIDTH, ATT_T), lambda b, p, qi, kj: (0, b * nq + qi[p])),
            pl.BlockSpec((ATT_T, DA_WIDTH), lambda b, p, qi, kj: (b * nq + kj[p], 2)),
            pl.BlockSpec((DA_WIDTH, ATT_T), lambda b, p, qi, kj: (1, b * nq + kj[p])),
            pl.BlockSpec((1, DA_HEADS, ATT_T, ATT_T),
                         lambda b, p, qi, kj: (jnp.minimum(qi[p] - kj[p], 1), 0, 0, 0)),
            pl.BlockSpec((DA_WIDTH, 1), lambda b, p, qi, kj: (0, 0)),
            pl.BlockSpec((1, ATT_T), lambda b, p, qi, kj: (0, 0)),
        ],
        out_specs=pl.BlockSpec((ATT_T, DA_WIDTH), lambda b, p, qi, kj: (b * nq + qi[p], 0)),
        scratch_shapes=[
            pltpu.VMEM((n_ht, DA_WIDTH, ATT_T), BF16),
            pltpu.VMEM((n_ht, 1, ATT_T), F32),
            pltpu.VMEM((n_ht, 1, ATT_T), F32),
            pltpu.VMEM((n_ht, LANES, ATT_T), F32),
            pltpu.VMEM((2, ATT_T, ATT_T), F32),
            pltpu.VMEM((2, ATT_T, ATT_T), BF16),
            pltpu.VMEM((2, 1, ATT_T), F32),
        ],
    )
    return pl.pallas_call(
        _attn_kernel,
        grid_spec=grid_spec,
        out_shape=jax.ShapeDtypeStruct((batch * seq, DA_WIDTH), BF16),
        compiler_params=_cparams(("parallel", "arbitrary")),
        name="diff_attention",
    )(qi_tbl, kj_tbl, qvt, qkv, qvt, bias_tiles, gain, lam)


def _t5_bucket(rel):
    half = REL_BUCKETS // 2
    max_exact = half // 2
    ret = jnp.where(rel > 0, half, 0)
    n = jnp.abs(rel)
    nf = jnp.maximum(n, 1).astype(F32)
    large = max_exact + (jnp.log(nf / max_exact) / math.log(REL_MAX_DIST / max_exact)
                         * (half - max_exact)).astype(jnp.int32)
    large = jnp.minimum(large, half - 1)
    return ret + jnp.where(n < max_exact, n, large)


def _bias_tiles(rel_bias):
    n = ATT_T
    i = jnp.arange(n)[None, :]
    j = jnp.arange(n)[:, None]
    far = rel_bias[_t5_bucket(jnp.int32(-2 * n))].astype(F32)
    tiles = []
    for off in range(2):
        rel = (n - 1 - off * n) - jnp.arange(2 * n)
        bucket = _t5_bucket(rel)
        g = jnp.zeros((DA_HEADS, 2 * n), F32)
        for c in range(REL_BUCKETS):
            g = jnp.where((bucket == c)[None], rel_bias[c].astype(F32)[:, None], g)
        g = (g - far[:, None]) * LOG2E
        skew = jnp.broadcast_to(g[:, None, :], (DA_HEADS, n, 2 * n)).reshape(DA_HEADS, 2 * n * n)
        skew = skew[:, :n * (2 * n - 1)].reshape(DA_HEADS, n, 2 * n - 1)
        b = skew[:, :, n - 1:]
        if off == 0:
            allowed = (j // CHUNK) <= (i // CHUNK)
            b = jnp.where(allowed[None], b, NEG_BIG)
        tiles.append(b)
    return jnp.stack(tiles)


def _cumsum_rows(x):
    n = x.shape[0]
    row = lax.broadcasted_iota(jnp.int32, x.shape, 0)
    sh = 1
    while sh < n:
        x = x + jnp.where(row >= sh, pltpu.roll(x, sh, axis=0), 0.0)
        sh *= 2
    return x


def _expand_heads(v, width, group):
    assert 2 * group == LANES
    rows = v.shape[0]
    first = lax.broadcasted_iota(jnp.int32, (rows, LANES), 1) < group
    pieces = []
    for pair in range(width // LANES):
        a = jnp.broadcast_to(v[:, 2 * pair:2 * pair + 1], (rows, LANES))
        b = jnp.broadcast_to(v[:, 2 * pair + 1:2 * pair + 2], (rows, LANES))
        pieces.append(jnp.where(first, a, b))
    return jnp.concatenate(pieces, axis=1)


def _ssd_kernel(z_ref, xs_ref, bc_ref, dt_ref, cwx_ref, cwb_ref, cbx_ref, cbb_ref, dtb_ref, alog_ref,
                dskip_ref, ng_ref, o_ref, extx_ref, extb_ref, st_ref):
    j = pl.program_id(1)

    @pl.when(j == 0)
    def _reset():
        extx_ref[0:8, :] = jnp.zeros((8, SSM_WIDTH), F32)
        extb_ref[0:8, :] = jnp.zeros((8, 2 * LANES), F32)
        st_ref[...] = jnp.zeros(st_ref.shape, F32)

    extx_ref[8:8 + BLK, :] = xs_ref[...]
    extb_ref[8:8 + BLK, :] = bc_ref[...]
    yx = jnp.broadcast_to(cbx_ref[...], (BLK, SSM_WIDTH))
    yb = jnp.broadcast_to(cbb_ref[...], (BLK, 2 * LANES))
    for w in range(SSM_CONV):
        lo = 8 - (SSM_CONV - 1) + w
        yx = yx + extx_ref[lo:lo + BLK, :] * cwx_ref[w:w + 1, :]
        yb = yb + extb_ref[lo:lo + BLK, :] * cwb_ref[w:w + 1, :]
    tail_x = extx_ref[BLK:BLK + 8, :]
    tail_b = extb_ref[BLK:BLK + 8, :]
    extx_ref[0:8, :] = tail_x
    extb_ref[0:8, :] = tail_b
    xs = _silu(yx)
    bc = _silu(yb)
    bm = bc[:, :LANES]
    cm = bc[:, LANES:]

    dt = jax.nn.softplus(dt_ref[...] + dtb_ref[...])
    a = -jnp.exp(alog_ref[...]) * dt
    a_cs = _cumsum_rows(a)
    a_cs_t = a_cs.T
    a_last = a_cs[BLK - 1:BLK, :]

    dt_full = _expand_heads(dt, SSM_WIDTH, SSM_HEAD_DIM)
    ea_full = _expand_heads(jnp.exp(a_cs), SSM_WIDTH, SSM_HEAD_DIM)
    dec_full = _expand_heads(jnp.exp(a_last - a_cs), SSM_WIDTH, SSM_HEAD_DIM)
    cdec_full = _expand_heads(jnp.exp(a_last), SSM_WIDTH, SSM_HEAD_DIM)

    xdt = xs * dt_full
    xdt_b = xdt.astype(BF16)
    cm_b = cm.astype(BF16)
    bm_b = bm.astype(BF16)

    st = st_ref[...]
    y = jnp.dot(cm_b, st.astype(BF16), preferred_element_type=F32) * ea_full + xs * dskip_ref[...]

    row = lax.broadcasted_iota(jnp.int32, (BLK, BLK), 0)
    colio = lax.broadcasted_iota(jnp.int32, (BLK, BLK), 1)
    causal = colio <= row
    rep = SSM_HEADS // SSM_GROUPS
    lane_lo = lax.broadcasted_iota(jnp.int32, (BLK, LANES), 1) < SSM_HEAD_DIM
    pieces = []
    for g in range(SSM_GROUPS):
        cg = jnp.where(_group_mask(cm_b.shape, 1, SSM_STATE, g), cm_b, jnp.zeros_like(cm_b))
        cb = lax.dot_general(cg, bm_b, (((1,), (1,)), ((), ())), preferred_element_type=F32)
        for pair in range(rep // 2):
            halves = []
            for sub in range(2):
                h = g * rep + pair * 2 + sub
                seg = jnp.broadcast_to(a_cs[:, h:h + 1], (BLK, BLK)) - a_cs_t[h:h + 1, :]
                lmat = jnp.exp(jnp.where(causal, seg, NEG_BIG))
                mh = (cb * lmat).astype(BF16)
                c0 = (h // 2) * LANES
                halves.append(jnp.dot(mh, xdt_b[:, c0:c0 + LANES], preferred_element_type=F32))
            pieces.append(jnp.where(lane_lo, halves[0], halves[1]))
    y = y + jnp.concatenate(pieces, axis=1)

    upd = jnp.dot(bm.T.astype(BF16), (xdt * dec_full).astype(BF16), preferred_element_type=F32)
    r_io = lax.broadcasted_iota(jnp.int32, upd.shape, 0) // SSM_STATE
    c_io = lax.broadcasted_iota(jnp.int32, upd.shape, 1) // (SSM_HEAD_DIM * rep)
    st_ref[...] = st * cdec_full + jnp.where(r_io == c_io, upd, 0.0)

    y = y * _silu(z_ref[...])
    gw = SSM_WIDTH // SSM_GROUPS
    for g in range(SSM_GROUPS):
        yg = y[:, g * gw:(g + 1) * gw]
        ms = jnp.mean(yg * yg, axis=-1, keepdims=True)
        o_ref[:, g * gw:(g + 1) * gw] = (yg * lax.rsqrt(ms + EPS) * ng_ref[:, g * gw:(g + 1) * gw]).astype(BF16)


def _ssd(pf, cwx, cwb, cbx, cbb, dtb, alog, dskip, ng, batch, seq):
    nb = seq // BLK
    row = lambda w, c: pl.BlockSpec((BLK, w), lambda b, j: (b * nb + j, c))
    const = lambda r, w: pl.BlockSpec((r, w), lambda b, j: (0, 0))
    return pl.pallas_call(
        _ssd_kernel,
        grid=(batch, nb),
        in_specs=[
            row(SSM_WIDTH, C_Z // SSM_WIDTH), row(SSM_WIDTH, C_XS // SSM_WIDTH), row(2 * LANES, C_BC // (2 * LANES)),
            row(LANES, C_DT // LANES),
            const(SSM_CONV, SSM_WIDTH), const(SSM_CONV, 2 * LANES), const(1, SSM_WIDTH), const(1, 2 * LANES),
            const(1, LANES), const(1, LANES), const(1, SSM_WIDTH), const(1, SSM_WIDTH),
        ],
        out_specs=pl.BlockSpec((BLK, SSM_WIDTH), lambda b, j: (b * nb + j, 0)),
        out_shape=jax.ShapeDtypeStruct((batch * seq, SSM_WIDTH), BF16),
        scratch_shapes=[
            pltpu.VMEM((BLK + 8, SSM_WIDTH), F32),
            pltpu.VMEM((BLK + 8, 2 * LANES), F32),
            pltpu.VMEM((SSM_GROUPS * SSM_STATE, SSM_WIDTH), F32),
        ],
        compiler_params=_cparams(("parallel", "arbitrary")),
        name="ssd_mixer",
    )(pf, pf, pf, pf, cwx, cwb, cbx, cbb, dtb, alog, dskip, ng)


def _rotary(u, cos, sin_signed):
    first = (lax.broadcasted_iota(jnp.int32, u.shape, 1) % RET_KEY_DIM) < (RET_KEY_DIM // 2)
    half = RET_KEY_DIM // 2
    swapped = jnp.where(first, pltpu.roll(u, u.shape[1] - half, axis=1), pltpu.roll(u, half, axis=1))
    return u * cos + swapped * sin_signed


def _ret_kernel(q_ref, k_ref, v_ref, g_ref, cos_ref, sin_ref, dmat_ref, qdec_ref, kdec_ref, gdec_ref,
                bd_ref, gain_ref, o_ref, st_ref):
    j = pl.program_id(1)

    @pl.when(j == 0)
    def _reset():
        st_ref[...] = jnp.zeros(st_ref.shape, F32)

    cos = _lane_tile(cos_ref[...], RET_WIDTH // LANES)
    sin = _lane_tile(sin_ref[...], RET_WIDTH // LANES)
    q = _rotary(q_ref[...], cos, sin)
    k = _rotary(k_ref[...], cos, sin) * (RET_KEY_DIM ** -0.5)
    v_b = v_ref[...].astype(BF16)
    q_b = q.astype(BF16)
    k_b = k.astype(BF16)

    st = st_ref[...]
    o = jnp.dot((q * qdec_ref[...]).astype(BF16), st.astype(BF16), preferred_element_type=F32)
    for h in range(RET_HEADS):
        cm = _group_mask(q_b.shape, 1, RET_KEY_DIM, h)
        qh = jnp.where(cm, q_b, jnp.zeros_like(q_b))
        s = lax.dot_general(qh, k_b, (((1,), (1,)), ((), ())), preferred_element_type=F32) * dmat_ref[h]
        oh = jnp.dot(s.astype(BF16), v_b, preferred_element_type=F32)
        o = o + jnp.where(cm, oh, 0.0)

    upd = jnp.dot((k * kdec_ref[...]).T.astype(BF16), v_b, preferred_element_type=F32)
    st_ref[...] = st * gdec_ref[...] + upd * bd_ref[...]

    out = jnp.zeros(o.shape, F32)
    for h in range(RET_HEADS):
        cm = _group_mask(o.shape, 1, RET_KEY_DIM, h)
        ms = jnp.sum(jnp.where(cm, o * o, 0.0), axis=-1, keepdims=True) * (1.0 / RET_KEY_DIM)
        out = out + jnp.where(cm, o * lax.rsqrt(ms + EPS), 0.0)
    o_ref[...] = (_silu(g_ref[...]) * (out * gain_ref[...])).astype(BF16)


def _retention(pf, cos, sin, dmat, qdec, kdec, gdec, bd, gain, batch, seq):
    nb = seq // BLK
    row = lambda c: pl.BlockSpec((BLK, RET_WIDTH), lambda b, j: (b * nb + j, c))
    const2 = pl.BlockSpec((BLK, RET_WIDTH), lambda b, j: (0, 0))
    return pl.pallas_call(
        _ret_kernel,
        grid=(batch, nb),
        in_specs=[
            row(C_RQ // RET_WIDTH), row(C_RK // RET_WIDTH), row(C_RV // RET_WIDTH), row(C_RG // RET_WIDTH),
            pl.BlockSpec((BLK, LANES), lambda b, j: (j, 0)),
            pl.BlockSpec((BLK, LANES), lambda b, j: (j, 0)),
            pl.BlockSpec((RET_HEADS, BLK, BLK), lambda b, j: (0, 0, 0)),
            const2, const2, const2, const2,
            pl.BlockSpec((1, RET_WIDTH), lambda b, j: (0, 0)),
        ],
        out_specs=pl.BlockSpec((BLK, RET_WIDTH), lambda b, j: (b * nb + j, 0)),
        out_shape=jax.ShapeDtypeStruct((batch * seq, RET_WIDTH), BF16),
        scratch_shapes=[pltpu.VMEM((RET_WIDTH, RET_WIDTH), F32)],
        compiler_params=_cparams(("parallel", "arbitrary")),
        name="retention",
    )(pf, pf, pf, pf, cos, sin, dmat, qdec, kdec, gdec, bd, gain)


def _retention_tables(seq):
    f32 = np.float32
    inv = (f32(1.0) / (f32(ROPE_BASE) ** (np.arange(0, RET_KEY_DIM, 2, dtype=f32) / f32(RET_KEY_DIM)))).astype(f32)
    ang = (np.arange(seq, dtype=f32)[:, None] * inv[None, :]).astype(f32)
    cos_h = np.concatenate([np.cos(ang), np.cos(ang)], axis=-1)
    sin_h = np.concatenate([-np.sin(ang), np.sin(ang)], axis=-1)
    cos = np.tile(cos_h, (1, 2)).astype(f32)
    sin = np.tile(sin_h, (1, 2)).astype(f32)
    log_gamma = np.log1p(-np.power(f32(2.0), f32(-5.0) - np.arange(RET_HEADS, dtype=f32))).astype(f32)
    idx = np.arange(BLK, dtype=f32)
    ii = np.arange(BLK)
    same_or_earlier_chunk = (ii[None, :] // CHUNK) <= (ii[:, None] // CHUNK)
    dmat = np.exp(log_gamma[:, None, None] * np.abs(idx[:, None] - idx[None, :]))
    dmat = np.where(same_or_earlier_chunk[None], dmat, 0.0).astype(f32)
    lg_cols = np.repeat(log_gamma, RET_KEY_DIM)
    qdec = np.exp(lg_cols[None, :] * (idx + 1.0)[:, None]).astype(f32)
    kdec = np.exp(lg_cols[None, :] * (BLK - 1.0 - idx)[:, None]).astype(f32)
    head_of = np.arange(RET_WIDTH) // RET_KEY_DIM
    bd = (head_of[:, None] == head_of[None, :]).astype(f32)
    gdec = (np.exp(lg_cols * f32(BLK))[:, None] * bd).astype(f32)
    return tuple(jnp.asarray(a) for a in (cos, sin, dmat, qdec, kdec, gdec, bd))


def _outproj_router_kernel(x_ref, a_ref, b_ref, c_ref, w_ref, g_ref, rw_ref, xo_ref, h_ref, rinfo_ref):
    acc = x_ref[...]
    acc = acc + jnp.dot(a_ref[...], w_ref[0:DA_WIDTH, :], preferred_element_type=F32)
    acc = acc + jnp.dot(b_ref[...], w_ref[DA_WIDTH:DA_WIDTH + SSM_WIDTH, :], preferred_element_type=F32)
    acc = acc + jnp.dot(c_ref[...], w_ref[DA_WIDTH + SSM_WIDTH:, :], preferred_element_type=F32)
    xo_ref[...] = acc
    ms = jnp.mean(acc * acc, axis=-1, keepdims=True)
    hf = acc * lax.rsqrt(ms + EPS) * g_ref[...]
    h = hf.astype(BF16)
    h_ref[...] = hf
    logits = jnp.dot(h, rw_ref[...], preferred_element_type=F32)
    lane = lax.broadcasted_iota(jnp.int32, logits.shape, 1).astype(F32)
    logits = jnp.where(lane < N_EXPERTS, logits, NEG_BIG)
    v1 = jnp.max(logits, axis=-1, keepdims=True)
    i1 = jnp.min(jnp.where(logits == v1, lane, float(LANES)), axis=-1, keepdims=True)
    rest_l = jnp.where(lane == i1, NEG_BIG, logits)
    v2 = jnp.max(rest_l, axis=-1, keepdims=True)
    i2 = jnp.min(jnp.where(rest_l == v2, lane, float(LANES)), axis=-1, keepdims=True)
    e2 = jnp.exp(v2 - v1)
    g1 = 1.0 / (1.0 + e2)
    g2 = e2 * g1
    rinfo_ref[...] = (jnp.where(lane == 0.0, g1, 0.0) + jnp.where(lane == 1.0, g2, 0.0)
                      + jnp.where(lane == 2.0, i1, 0.0) + jnp.where(lane == 3.0, i2, 0.0))


def _outproj_router(x2, oa, ob, oc, w, gain, router_w):
    t = x2.shape[0]
    tok = lambda w_: pl.BlockSpec((TM_PROJ, w_), lambda i: (i, 0))
    return pl.pallas_call(
        _outproj_router_kernel,
        grid=(t // TM_PROJ,),
        in_specs=[tok(D_MODEL), tok(DA_WIDTH), tok(SSM_WIDTH), tok(RET_WIDTH),
                  pl.BlockSpec((D_MODEL, D_MODEL), lambda i: (0, 0)),
                  pl.BlockSpec((1, D_MODEL), lambda i: (0, 0)),
                  pl.BlockSpec((D_MODEL, LANES), lambda i: (0, 0))],
        out_specs=[tok(D_MODEL), tok(D_MODEL), tok(LANES)],
        out_shape=[jax.ShapeDtypeStruct((t, D_MODEL), F32), jax.ShapeDtypeStruct((t, D_MODEL), F32),
                   jax.ShapeDtypeStruct((t, LANES), F32)],
        compiler_params=_cparams(("parallel",)),
        name="outproj_router",
    )(x2, oa, ob, oc, w, gain, router_w)


def _ffn_kernel(x_ref, a_ref, b_ref, c_ref, wo_ref, g_ref, wg_ref, wu_ref, wd_ref, o_ref, acc_ref, h_ref):
    f = pl.program_id(1)

    @pl.when(f == 0)
    def _init():
        acc = x_ref[...]
        acc = acc + jnp.dot(a_ref[...], wo_ref[0:DA_WIDTH, :], preferred_element_type=F32)
        acc = acc + jnp.dot(b_ref[...], wo_ref[DA_WIDTH:DA_WIDTH + SSM_WIDTH, :], preferred_element_type=F32)
        acc = acc + jnp.dot(c_ref[...], wo_ref[DA_WIDTH + SSM_WIDTH:, :], preferred_element_type=F32)
        acc_ref[...] = acc
        ms = jnp.mean(acc * acc, axis=-1, keepdims=True)
        h_ref[...] = (acc * lax.rsqrt(ms + EPS) * g_ref[...]).astype(BF16)

    h = h_ref[...]
    gate = jnp.dot(h, wg_ref[0], preferred_element_type=F32)
    up = jnp.dot(h, wu_ref[0], preferred_element_type=F32)
    act = (_silu(gate) * up).astype(BF16)
    acc_ref[...] += jnp.dot(act, wd_ref[...], preferred_element_type=F32)

    @pl.when(f == pl.num_programs(1) - 1)
    def _done():
        o_ref[...] = acc_ref[...]


def _outproj_dense_ffn(x2, oa, ob, oc, w_o, gain, wg, wu, wd):
    t = x2.shape[0]
    nf = wg.shape[0]
    tok = lambda w_: pl.BlockSpec((TM_FFN, w_), lambda i, f: (i, 0))
    return pl.pallas_call(
        _ffn_kernel,
        grid=(t // TM_FFN, nf),
        in_specs=[
            tok(D_MODEL), tok(DA_WIDTH), tok(SSM_WIDTH), tok(RET_WIDTH),
            pl.BlockSpec((D_MODEL, D_MODEL), lambda i, f: (0, 0)),
            pl.BlockSpec((1, D_MODEL), lambda i, f: (0, 0)),
            pl.BlockSpec((1, D_MODEL, TF_FFN), lambda i, f: (f, 0, 0)),
            pl.BlockSpec((1, D_MODEL, TF_FFN), lambda i, f: (f, 0, 0)),
            pl.BlockSpec((TF_FFN, D_MODEL), lambda i, f: (f, 0)),
        ],
        out_specs=pl.BlockSpec((TM_FFN, D_MODEL), lambda i, f: (i, 0)),
        out_shape=jax.ShapeDtypeStruct((t, D_MODEL), F32),
        scratch_shapes=[pltpu.VMEM((TM_FFN, D_MODEL), F32), pltpu.VMEM((TM_FFN, D_MODEL), BF16)],
        compiler_params=_cparams(("parallel", "arbitrary")),
        name="outproj_dense_swiglu",
    )(x2, oa, ob, oc, w_o, gain, wg, wu, wd)


def _issue_row_gather(src_hbm, idx_ref, idx_base, dst, sem, n_rows):
    def body(r, carry):
        tok = idx_ref[idx_base + r]
        pltpu.make_async_copy(src_hbm.at[pl.ds(tok, 1)], dst.at[pl.ds(r, 1)], sem).start()
        return carry
    lax.fori_loop(0, n_rows, body, 0, unroll=8)


def _wait_row_gather(src_hbm, dst, sem, n_rows):
    pltpu.make_async_copy(src_hbm.at[pl.ds(0, n_rows)], dst, sem).wait()


def _moe_ffn_kernel(te_ref, nv_ref, tb_ref, rt_ref, h_hbm, wg_ref, wu_ref, wd_ref, y_ref, hbuf, xb_ref, acc_ref, sem):
    i = pl.program_id(0)
    f = pl.program_id(1)
    n_valid = nv_ref[0]
    valid = i < n_valid
    slot = i % 2

    @pl.when((f == 0) & (i == 0) & valid)
    def _first():
        _issue_row_gather(h_hbm, rt_ref, tb_ref[0], hbuf.at[0], sem.at[0], TR_MOE)

    @pl.when((f == 0) & valid)
    def _arrive():
        _wait_row_gather(h_hbm, hbuf.at[slot], sem.at[slot], TR_MOE)
        xb_ref[...] = hbuf[slot].astype(BF16)

    def compute(prefetch_next):
        if prefetch_next:
            rows_per_step = TR_MOE // (FFN_EXPERT // TF_MOE)
            r0 = f * rows_per_step
            src0 = tb_ref[i + 1] + r0
            for j in range(rows_per_step):
                tok = rt_ref[src0 + j]
                pltpu.make_async_copy(h_hbm.at[pl.ds(tok, 1)], hbuf.at[1 - slot, pl.ds(r0 + j, 1)],
                                      sem.at[1 - slot]).start()
        xb = xb_ref[...]
        gate = jnp.dot(xb, wg_ref[0].astype(BF16), preferred_element_type=F32)
        up = jnp.dot(xb, wu_ref[0].astype(BF16), preferred_element_type=F32)
        act = (_silu(gate) * up).astype(BF16)
        contrib = jnp.dot(act, wd_ref[0].astype(BF16), preferred_element_type=F32)

        @pl.when(f == 0)
        def _():
            acc_ref[...] = contrib

        @pl.when(f > 0)
        def _():
            acc_ref[...] += contrib

    has_next = i + 1 < n_valid

    @pl.when(valid & has_next)
    def _compute_and_prefetch():
        compute(True)

    @pl.when(valid & jnp.logical_not(has_next))
    def _compute_last():
        compute(False)

    @pl.when(f == pl.num_programs(1) - 1)
    def _store():
        y_ref[...] = jnp.where(valid, acc_ref[...], 0.0)


def _moe_ffn(h, tile_expert, n_valid, tile_base, sorted_token, wg, wu, wd):
    n_tiles = tile_expert.shape[0]
    nf = wg.shape[2] // TF_MOE

    def w_in_map(i, f, te, nv, tb, rt):
        return (te[i], 0, jnp.where(i < nv[0], f, nf - 1))

    def w_out_map(i, f, te, nv, tb, rt):
        return (te[i], jnp.where(i < nv[0], f, nf - 1), 0)

    grid_spec = pltpu.PrefetchScalarGridSpec(
        num_scalar_prefetch=4,
        grid=(n_tiles, nf),
        in_specs=[
            pl.BlockSpec(memory_space=pl.ANY),
            pl.BlockSpec((1, D_MODEL, TF_MOE), w_in_map),
            pl.BlockSpec((1, D_MODEL, TF_MOE), w_in_map),
            pl.BlockSpec((1, TF_MOE, D_MODEL), w_out_map),
        ],
        out_specs=pl.BlockSpec((TR_MOE, D_MODEL), lambda i, f, te, nv, tb, rt: (i, 0)),
        scratch_shapes=[
            pltpu.VMEM((2, TR_MOE, D_MODEL), F32),
            pltpu.VMEM((TR_MOE, D_MODEL), BF16),
            pltpu.VMEM((TR_MOE, D_MODEL), F32),
            pltpu.SemaphoreType.DMA((2,)),
        ],
    )
    return pl.pallas_call(
        _moe_ffn_kernel,
        grid_spec=grid_spec,
        out_shape=jax.ShapeDtypeStruct((n_tiles * TR_MOE, D_MODEL), F32),
        compiler_params=_cparams(("arbitrary", "arbitrary")),
        name="moe_expert_ffn",
    )(tile_expert, n_valid, tile_base, sorted_token, h, wg, wu, wd)


def _moe_combine_kernel(tr_ref, x_ref, rinfo_ref, y_hbm, fg_ref, o_ref, ybuf, sem):
    i = pl.program_id(0)
    slot = i % 2
    n_rows = TOP_K * TM_COMB

    @pl.when(i == 0)
    def _first():
        _issue_row_gather(y_hbm, tr_ref, 0, ybuf.at[0], sem.at[0], n_rows)

    _wait_row_gather(y_hbm, ybuf.at[slot], sem.at[slot], n_rows)

    @pl.when(i + 1 < pl.num_programs(0))
    def _prefetch():
        src0 = (i + 1) * n_rows
        dst = ybuf.at[1 - slot]
        for r in range(n_rows):
            pltpu.make_async_copy(y_hbm.at[pl.ds(tr_ref[src0 + r], 1)], dst.at[pl.ds(r, 1)], sem.at[1 - slot]).start()

    rinfo = rinfo_ref[...]
    g1 = jnp.broadcast_to(rinfo[:, 0:1], (TM_COMB, D_MODEL))
    g2 = jnp.broadcast_to(rinfo[:, 1:2], (TM_COMB, D_MODEL))
    y = x_ref[...] + g1 * ybuf[slot, 0:TM_COMB, :] + g2 * ybuf[slot, TM_COMB:n_rows, :]
    ms = jnp.mean(y * y, axis=-1, keepdims=True)
    o_ref[...] = y * lax.rsqrt(ms + EPS) * fg_ref[...]


def _moe_combine(x2, rinfo, y_rows, tok_rows, final_gain):
    t = x2.shape[0]
    grid_spec = pltpu.PrefetchScalarGridSpec(
        num_scalar_prefetch=1,
        grid=(t // TM_COMB,),
        in_specs=[
            pl.BlockSpec((TM_COMB, D_MODEL), lambda i, tr: (i, 0)),
            pl.BlockSpec((TM_COMB, LANES), lambda i, tr: (i, 0)),
            pl.BlockSpec(memory_space=pl.ANY),
            pl.BlockSpec((1, D_MODEL), lambda i, tr: (0, 0)),
        ],
        out_specs=pl.BlockSpec((TM_COMB, D_MODEL), lambda i, tr: (i, 0)),
        scratch_shapes=[
            pltpu.VMEM((2, TOP_K * TM_COMB, D_MODEL), F32),
            pltpu.SemaphoreType.DMA((2,)),
        ],
    )
    return pl.pallas_call(
        _moe_combine_kernel,
        grid_spec=grid_spec,
        out_shape=jax.ShapeDtypeStruct((t, D_MODEL), F32),
        compiler_params=_cparams(("arbitrary",)),
        name="moe_combine_norm",
    )(tok_rows, x2, rinfo, y_rows, final_gain)


def _routing_tables(rinfo):
    t = rinfo.shape[0]
    n_assign = TOP_K * t
    n_tiles = -(-n_assign // TR_MOE) + N_EXPERTS
    flat_e = rinfo[:, 2:2 + TOP_K].astype(jnp.int32).reshape(-1)
    onehot = (flat_e[:, None] == jnp.arange(N_EXPERTS)[None, :]).astype(jnp.int32)
    blocks = onehot.astype(BF16).reshape(n_assign // LANES, LANES, N_EXPERTS)
    tri = (jnp.arange(LANES)[:, None] >= jnp.arange(LANES)[None, :]).astype(BF16)
    within = jnp.einsum("ij,bjk->bik", tri, blocks, preferred_element_type=F32)
    block_end = jnp.cumsum(within[:, -1, :], axis=0)
    csum = (within + (block_end - within[:, -1, :])[:, None, :]).reshape(n_assign, N_EXPERTS).astype(jnp.int32)
    rank = jnp.sum((csum - onehot) * onehot, axis=1)
    counts = csum[-1]
    tiles_per_e = (counts + TR_MOE - 1) // TR_MOE
    tile_end = jnp.cumsum(tiles_per_e)
    first_tile = tile_end - tiles_per_e
    dest = (first_tile * TR_MOE)[flat_e] + rank
    order = jnp.sort(flat_e * n_assign + jnp.arange(n_assign, dtype=jnp.int32))
    sorted_token = jnp.concatenate([(order % n_assign) // TOP_K, jnp.zeros((TR_MOE,), jnp.int32)])
    start = jnp.cumsum(counts) - counts
    n_valid = tile_end[-1:].astype(jnp.int32)
    tile_ids = jnp.arange(n_tiles)
    tile_expert = jnp.sum(tile_ids[:, None] >= tile_end[None, :], axis=1).astype(jnp.int32)
    last_e = jnp.sum(n_valid[0] - 1 >= tile_end).astype(jnp.int32)
    tile_expert = jnp.where(tile_ids < n_valid[0], tile_expert, last_e)
    tile_base = start[tile_expert] + (tile_ids - first_tile[tile_expert]) * TR_MOE
    tile_base = jnp.clip(tile_base, 0, n_assign).astype(jnp.int32)
    tok_rows = dest.reshape(t // TM_COMB, TM_COMB, TOP_K).transpose(0, 2, 1).reshape(-1).astype(jnp.int32)
    return tile_expert, n_valid, tile_base, sorted_token, tok_rows


def _permute_w_in(w):
    sizes = (256, 256, 256, 512, 768, 8, 256, 256, 256, 256)
    offs = np.concatenate([[0], np.cumsum(sizes)])
    part = lambda n: w[:, offs[n]:offs[n + 1]]
    dt_pad = jnp.zeros((w.shape[0], LANES - SSM_HEADS), w.dtype)
    cols = [part(0) * (DA_HEAD_DIM ** -0.5 * LOG2E), part(2), part(1), part(3), part(4),
            part(6), part(7), part(8), part(9), part(5), dt_pad]
    return jnp.concatenate(cols, axis=1).astype(BF16)


def _column_tiles(w, tile):
    *lead, k, n = w.shape
    return jnp.moveaxis(w.reshape(*lead, k, n // tile, tile), -2, -3)


def _pad_lanes(v, fill=0.0):
    return jnp.concatenate([v, jnp.full((LANES - v.shape[0],), fill, v.dtype)])[None, :]


def kernel(x, w_in, w_out, attn_norm, ffn_norm, final_norm, rel_bias, lambda_q1, lambda_k1, lambda_q2,
           lambda_k2, da_head_norm, conv_w, conv_b, dt_bias, a_log, d_skip, ssm_norm, ret_head_norm,
           w_gate, w_up, w_down, router_w, e_gate, e_up, e_down):
    batch, seq, _ = x.shape
    assert DEPTH % 2 == 0, "the final RMSNorm is fused into the last (expert) layer's combine kernel"
    assert seq % max(ATT_T, TM_PROJ, BLK) == 0 and (batch * seq) % TM_FFN == 0
    x2 = x.reshape(batch * seq, D_MODEL)
    bias_tiles = _bias_tiles(rel_bias)
    ret_tabs = _retention_tables(seq)

    for layer in range(DEPTH):
        lam_init = 0.8 - 0.6 * math.exp(-0.3 * layer)
        lam = (jnp.exp(jnp.sum(lambda_q1[layer] * lambda_k1[layer]))
               - jnp.exp(jnp.sum(lambda_q2[layer] * lambda_k2[layer])) + lam_init).astype(F32)
        lam = jnp.full((1, ATT_T), lam, F32)

        qkv, qvt, pf = _inproj(x2, attn_norm[layer][None, :], _permute_w_in(w_in[layer]))

        da_gain = (jnp.tile(da_head_norm[layer], DA_HEADS) * (1.0 - lam_init))[:, None]
        out_a = _attention(qkv, qvt, bias_tiles, da_gain, lam, batch, seq)

        cw, cb = conv_w[layer], conv_b[layer]
        out_b = _ssd(pf, cw[:, :SSM_WIDTH], cw[:, SSM_WIDTH:], cb[None, :SSM_WIDTH], cb[None, SSM_WIDTH:],
                     _pad_lanes(dt_bias[layer]), _pad_lanes(a_log[layer]),
                     jnp.repeat(d_skip[layer], SSM_HEAD_DIM)[None, :], ssm_norm[layer][None, :], batch, seq)

        ret_gain = jnp.tile(ret_head_norm[layer], RET_HEADS)[None, :]
        out_c = _retention(pf, *ret_tabs, ret_gain, batch, seq)

        w_o = w_out[layer].astype(BF16)
        if layer % 2 == 0:
            i = layer // 2
            pad = FFN_PAD - FFN_DENSE
            wg = _column_tiles(jnp.pad(w_gate[i], ((0, 0), (0, pad))).astype(BF16), TF_FFN)
            wu = _column_tiles(jnp.pad(w_up[i], ((0, 0), (0, pad))).astype(BF16), TF_FFN)
            wd = jnp.pad(w_down[i], ((0, pad), (0, 0))).astype(BF16)
            x2 = _outproj_dense_ffn(x2, out_a, out_b, out_c, w_o, ffn_norm[layer][None, :], wg, wu, wd)
        else:
            i = layer // 2
            rw = jnp.pad(router_w[i], ((0, 0), (0, LANES - N_EXPERTS))).astype(BF16)
            x2, h, rinfo = _outproj_router(x2, out_a, out_b, out_c, w_o, ffn_norm[layer][None, :], rw)
            tile_expert, n_valid, tile_base, sorted_token, tok_rows = _routing_tables(rinfo)
            y_rows = _moe_ffn(h, tile_expert, n_valid, tile_base, sorted_token,
                              e_gate[i], e_up[i], e_down[i])
            x2 = _moe_combine(x2, rinfo, y_rows, tok_rows, final_norm[None, :])
    return x2.reshape(batch, seq, D_MODEL)
```

```python
import math

import jax
import jax.numpy as jnp
import numpy as np
from jax import lax
from jax.experimental import pallas as pl
from jax.experimental.pallas import tpu as pltpu

F32 = jnp.float32
BF16 = jnp.bfloat16

D_MODEL = 1024
DEPTH = 2
CHUNK = 64
EPS = 1e-6
DA_HEADS = 4
DA_HEAD_DIM = 32
DA_V_DIM = 64
DA_WIDTH = 256
SSM_HEADS = 8
SSM_HEAD_DIM = 64
SSM_WIDTH = 512
SSM_STATE = 64
SSM_GROUPS = 2
SSM_CONV = 4
RET_HEADS = 4
RET_KEY_DIM = 64
RET_WIDTH = 256
ROPE_BASE = 10000.0
REL_BUCKETS = 32
REL_MAX_DIST = 128
FFN_DENSE = 2752
N_EXPERTS = 8
FFN_EXPERT = 3584

LANES = 128
VMEM_LIMIT = 48 * 1024 * 1024
NEG_BIG = -1e30

TM_PROJ = 512
ATT_T = 512
LOG2E = math.log2(math.e)
BLK = 256
FFN_PAD = 2816
TM_FFN = 512
TF_FFN = 1408
TF_MOE = 512
TR_MOE = 1008
TM_COMB = 256
TOP_K = 2

C_Z, C_XS, C_BC, C_RQ, C_RK, C_RV, C_RG, C_DT = 0, 512, 1024, 1280, 1536, 1792, 2048, 2304
PF_W = 2432
PF_CHUNK = 640
QKV_W = 768


def _cparams(sem):
    return pltpu.CompilerParams(dimension_semantics=sem, vmem_limit_bytes=VMEM_LIMIT)


def _group_mask(shape, axis, group, idx):
    io = lax.broadcasted_iota(jnp.int32, shape, axis)
    return (io >= idx * group) & (io < (idx + 1) * group)


def _lane_tile(x, n, axis=1):
    return jnp.concatenate([x] * n, axis=axis)


def _silu(x):
    h = 0.5 * x
    return h + h * jnp.tanh(h)


def _inproj_kernel(x_ref, g_ref, w_ref, qkv_ref, qvt_ref, pf_ref):
    x = x_ref[...]
    ms = jnp.mean(x * x, axis=-1, keepdims=True)
    h = (x * lax.rsqrt(ms + EPS) * g_ref[...]).astype(BF16)
    qkv = jnp.dot(h, w_ref[:, :QKV_W], preferred_element_type=F32)
    qkv_ref[...] = qkv.astype(BF16)
    qvt_ref[...] = qkv[:, :2 * DA_WIDTH].T.astype(BF16)
    for lo in range(0, PF_W, PF_CHUNK):
        hi = min(lo + PF_CHUNK, PF_W)
        pf_ref[:, lo:hi] = jnp.dot(h, w_ref[:, QKV_W + lo:QKV_W + hi], preferred_element_type=F32)


def _inproj(x2, gain, w):
    t = x2.shape[0]
    return pl.pallas_call(
        _inproj_kernel,
        grid=(t // TM_PROJ,),
        in_specs=[
            pl.BlockSpec((TM_PROJ, D_MODEL), lambda i: (i, 0)),
            pl.BlockSpec((1, D_MODEL), lambda i: (0, 0)),
            pl.BlockSpec((D_MODEL, QKV_W + PF_W), lambda i: (0, 0)),
        ],
        out_specs=[
            pl.BlockSpec((TM_PROJ, QKV_W), lambda i: (i, 0)),
            pl.BlockSpec((2 * DA_WIDTH, TM_PROJ), lambda i: (0, i)),
            pl.BlockSpec((TM_PROJ, PF_W), lambda i: (i, 0)),
        ],
        out_shape=[
            jax.ShapeDtypeStruct((t, QKV_W), BF16),
            jax.ShapeDtypeStruct((2 * DA_WIDTH, t), BF16),
            jax.ShapeDtypeStruct((t, PF_W), F32),
        ],
        compiler_params=_cparams(("parallel",)),
        name="inproj",
    )(x2, gain, w)


def _attn_kernel(qi_ref, kj_ref, qt_ref, k_ref, vt_ref, bias_ref, gain_ref, lam_ref, o_ref,
                 qm_ref, m_ref, l_ref, acc_ref, s_ref, e_ref, al_ref):
    p = pl.program_id(1)
    qi = qi_ref[p]
    kj = kj_ref[p]
    n_ht = 2 * DA_HEADS

    @pl.when(kj == 0)
    def _init():
        qt = qt_ref[...]
        for ht in range(n_ht):
            qm_ref[ht] = jnp.where(_group_mask(qt.shape, 0, DA_HEAD_DIM, ht), qt, jnp.zeros_like(qt))
        m_ref[...] = jnp.full(m_ref.shape, NEG_BIG, F32)
        l_ref[...] = jnp.zeros(l_ref.shape, F32)
        acc_ref[...] = jnp.zeros(acc_ref.shape, F32)

    def step(with_bias):
        k = k_ref[...]
        def scores(a, buf):
            s_ref[buf] = jnp.dot(k, qm_ref[a], preferred_element_type=F32)

        def softmax(b, buf):
            for c0 in range(0, ATT_T, LANES):
                cols = slice(c0, c0 + LANES)

                def block():
                    blk = s_ref[buf, :, cols]
                    return blk + bias_ref[0, b // 2, :, cols] if with_bias else blk

                m_prev = m_ref[b, :, cols]
                m_new = jnp.maximum(m_prev, jnp.max(block(), axis=0, keepdims=True))
                alpha = jnp.exp2(m_prev - m_new)
                e = jnp.exp2(block() - m_new)
                l_ref[b, :, cols] = alpha * l_ref[b, :, cols] + jnp.sum(e, axis=0, keepdims=True)
                m_ref[b, :, cols] = m_new
                al_ref[buf, :, cols] = alpha
                e_ref[buf, :, cols] = e.astype(BF16)

        def values(c, buf):
            r0 = (c // 4) * LANES
            if not isinstance(r0, int):
                r0 = pl.multiple_of(r0, LANES)
            pv = jnp.dot(vt_ref[pl.ds(r0, LANES), :], e_ref[buf], preferred_element_type=F32)
            acc_ref[c] = acc_ref[c] * al_ref[buf] + pv

        scores(0, 0)
        scores(1, 1)
        softmax(0, 0)

        def trip(u, carry):
            t = 2 * u + 1
            scores(t + 1, 0)
            softmax(t, 1)
            values(t - 1, 0)
            scores(t + 2, 1)
            softmax(t + 1, 0)
            values(t, 1)
            return carry

        for u in range((n_ht - 2) // 2):
            trip(u, 0)
        softmax(n_ht - 1, 1)
        values(n_ht - 2, 0)
        values(n_ht - 1, 1)

    @pl.when(kj >= qi - 1)
    def _near():
        step(True)

    @pl.when(kj < qi - 1)
    def _far():
        step(False)

    @pl.when(kj == qi)
    def _finish():
        lam = lam_ref[...]
        row_lo = lax.broadcasted_iota(jnp.int32, (LANES, ATT_T), 0) < DA_V_DIM
        for pair in range(DA_HEADS // 2):
            halves = []
            for sub in range(2):
                h = 2 * pair + sub
                o = acc_ref[2 * h] * (1.0 / l_ref[2 * h]) - lam * (acc_ref[2 * h + 1] * (1.0 / l_ref[2 * h + 1]))
                own = row_lo if sub == 0 else jnp.logical_not(row_lo)
                ms = jnp.sum(jnp.where(own, o * o, 0.0), axis=0, keepdims=True) * (1.0 / DA_V_DIM)
                halves.append(o * lax.rsqrt(ms + EPS))
            r0 = pair * LANES
            blk = jnp.where(row_lo, halves[0], halves[1]) * gain_ref[r0:r0 + LANES, :]
            o_ref[:, r0:r0 + LANES] = blk.T.astype(BF16)


def _attention(qkv, qvt, bias_tiles, gain, lam, batch, seq):
    nq = seq // ATT_T
    qi_l, kj_l = [], []
    for qi in range(nq):
        for kj in range(qi + 1):
            qi_l.append(qi)
            kj_l.append(kj)
    qi_tbl = jnp.asarray(np.array(qi_l, np.int32))
    kj_tbl = jnp.asarray(np.array(kj_l, np.int32))
    n_pairs = len(qi_l)
    n_ht = 2 * DA_HEADS

    grid_spec = pltpu.PrefetchScalarGridSpec(
        num_scalar_prefetch=2,
        grid=(batch, n_pairs),
        in_specs=[
            pl.BlockSpec((DA_WIDTH, ATT_T), lambda b, p, qi, kj: (0, b * nq + qi[p])),
            pl.BlockSpec((ATT_T, DA_WIDTH), lambda b, p, qi, kj: (b * nq + kj[p], 2)),
            pl.BlockSpec((DA_WIDTH, ATT_T), lambda b, p, qi, kj: (1, b * nq + kj[p])),
            pl.BlockSpec((1, DA_HEADS, ATT_T, ATT_T),
                         lambda b, p, qi, kj: (jnp.minimum(qi[p] - kj[p], 1), 0, 0, 0)),
            pl.BlockSpec((DA_WIDTH, 1), lambda b, p, qi, kj: (0, 0)),
            pl.BlockSpec((1, ATT_T), lambda b, p, qi, kj: (0, 0)),
        ],
        out_specs=pl.BlockSpec((ATT_T, DA_WIDTH), lambda b, p, qi, kj: (b * nq + qi[p], 0)),
        scratch_shapes=[
            pltpu.VMEM((n_ht, DA_WIDTH, ATT_T), BF16),
            pltpu.VMEM((n_ht, 1, ATT_T), F32),
            pltpu.VMEM((n_ht, 1, ATT_T), F32),
            pltpu.VMEM((n_ht, LANES, ATT_T), F32),
            pltpu.VMEM((2, ATT_T, ATT_T), F32),
            pltpu.VMEM((2, ATT_T, ATT_T), BF16),
            pltpu.VMEM((2, 1, ATT_T), F32),
        ],
    )
    return pl.pallas_call(
        _attn_kernel,
        grid_spec=grid_spec,
        out_shape=jax.ShapeDtypeStruct((batch * seq, DA_WIDTH), BF16),
        compiler_params=_cparams(("parallel", "arbitrary")),
        name="diff_attention",
    )(qi_tbl, kj_tbl, qvt, qkv, qvt, bias_tiles, gain, lam)


def _t5_bucket(rel):
    half = REL_BUCKETS // 2
    max_exact = half // 2
    ret = jnp.where(rel > 0, half, 0)
    n = jnp.abs(rel)
    nf = jnp.maximum(n, 1).astype(F32)
    large = max_exact + (jnp.log(nf / max_exact) / math.log(REL_MAX_DIST / max_exact)
                         * (half - max_exact)).astype(jnp.int32)
    large = jnp.minimum(large, half - 1)
    return ret + jnp.where(n < max_exact, n, large)


def _bias_tiles(rel_bias):
    n = ATT_T
    i = jnp.arange(n)[None, :]
    j = jnp.arange(n)[:, None]
    far = rel_bias[_t5_bucket(jnp.int32(-2 * n))].astype(F32)
    tiles = []
    for off in range(2):
        rel = (n - 1 - off * n) - jnp.arange(2 * n)
        bucket = _t5_bucket(rel)
        g = jnp.zeros((DA_HEADS, 2 * n), F32)
        for c in range(REL_BUCKETS):
            g = jnp.where((bucket == c)[None], rel_bias[c].astype(F32)[:, None], g)
        g = (g - far[:, None]) * LOG2E
        skew = jnp.broadcast_to(g[:, None, :], (DA_HEADS, n, 2 * n)).reshape(DA_HEADS, 2 * n * n)
        skew = skew[:, :n * (2 * n - 1)].reshape(DA_HEADS, n, 2 * n - 1)
        b = skew[:, :, n - 1:]
        if off == 0:
            allowed = (j // CHUNK) <= (i // CHUNK)
            b = jnp.where(allowed[None], b, NEG_BIG)
        tiles.append(b)
    return jnp.stack(tiles)


def _cumsum_rows(x):
    n = x.shape[0]
    row = lax.broadcasted_iota(jnp.int32, x.shape, 0)
    sh = 1
    while sh < n:
        x = x + jnp.where(row >= sh, pltpu.roll(x, sh, axis=0), 0.0)
        sh *= 2
    return x


def _expand_heads(v, width, group):
    assert 2 * group == LANES
    rows = v.shape[0]
    first = lax.broadcasted_iota(jnp.int32, (rows, LANES), 1) < group
    pieces = []
    for pair in range(width // LANES):
        a = jnp.broadcast_to(v[:, 2 * pair:2 * pair + 1], (rows, LANES))
        b = jnp.broadcast_to(v[:, 2 * pair + 1:2 * pair + 2], (rows, LANES))
        pieces.append(jnp.where(first, a, b))
    return jnp.concatenate(pieces, axis=1)


def _ssd_kernel(z_ref, xs_ref, bc_ref, dt_ref, cwx_ref, cwb_ref, cbx_ref, cbb_ref, dtb_ref, alog_ref,
                dskip_ref, ng_ref, o_ref, extx_ref, extb_ref, st_ref):
    j = pl.program_id(1)

    @pl.when(j == 0)
    def _reset():
        extx_ref[0:8, :] = jnp.zeros((8, SSM_WIDTH), F32)
        extb_ref[0:8, :] = jnp.zeros((8, 2 * LANES), F32)
        st_ref[...] = jnp.zeros(st_ref.shape, F32)

    extx_ref[8:8 + BLK, :] = xs_ref[...]
    extb_ref[8:8 + BLK, :] = bc_ref[...]
    yx = jnp.broadcast_to(cbx_ref[...], (BLK, SSM_WIDTH))
    yb = jnp.broadcast_to(cbb_ref[...], (BLK, 2 * LANES))
    for w in range(SSM_CONV):
        lo = 8 - (SSM_CONV - 1) + w
        yx = yx + extx_ref[lo:lo + BLK, :] * cwx_ref[w:w + 1, :]
        yb = yb + extb_ref[lo:lo + BLK, :] * cwb_ref[w:w + 1, :]
    tail_x = extx_ref[BLK:BLK + 8, :]
    tail_b = extb_ref[BLK:BLK + 8, :]
    extx_ref[0:8, :] = tail_x
    extb_ref[0:8, :] = tail_b
    xs = _silu(yx)
    bc = _silu(yb)
    bm = bc[:, :LANES]
    cm = bc[:, LANES:]

    dt = jax.nn.softplus(dt_ref[...] + dtb_ref[...])
    a = -jnp.exp(alog_ref[...]) * dt
    a_cs = _cumsum_rows(a)
    a_cs_t = a_cs.T
    a_last = a_cs[BLK - 1:BLK, :]

    dt_full = _expand_heads(dt, SSM_WIDTH, SSM_HEAD_DIM)
    ea_full = _expand_heads(jnp.exp(a_cs), SSM_WIDTH, SSM_HEAD_DIM)
    dec_full = _expand_heads(jnp.exp(a_last - a_cs), SSM_WIDTH, SSM_HEAD_DIM)
    cdec_full = _expand_heads(jnp.exp(a_last), SSM_WIDTH, SSM_HEAD_DIM)

    xdt = xs * dt_full
    xdt_b = xdt.astype(BF16)
    cm_b = cm.astype(BF16)
    bm_b = bm.astype(BF16)

    st = st_ref[...]
    y = jnp.dot(cm_b, st.astype(BF16), preferred_element_type=F32) * ea_full + xs * dskip_ref[...]

    row = lax.broadcasted_iota(jnp.int32, (BLK, BLK), 0)
    colio = lax.broadcasted_iota(jnp.int32, (BLK, BLK), 1)
    causal = colio <= row
    rep = SSM_HEADS // SSM_GROUPS
    lane_lo = lax.broadcasted_iota(jnp.int32, (BLK, LANES), 1) < SSM_HEAD_DIM
    pieces = []
    for g in range(SSM_GROUPS):
        cg = jnp.where(_group_mask(cm_b.shape, 1, SSM_STATE, g), cm_b, jnp.zeros_like(cm_b))
        cb = lax.dot_general(cg, bm_b, (((1,), (1,)), ((), ())), preferred_element_type=F32)
        for pair in range(rep // 2):
            halves = []
            for sub in range(2):
                h = g * rep + pair * 2 + sub
                seg = jnp.broadcast_to(a_cs[:, h:h + 1], (BLK, BLK)) - a_cs_t[h:h + 1, :]
                lmat = jnp.exp(jnp.where(causal, seg, NEG_BIG))
                mh = (cb * lmat).astype(BF16)
                c0 = (h // 2) * LANES
                halves.append(jnp.dot(mh, xdt_b[:, c0:c0 + LANES], preferred_element_type=F32))
            pieces.append(jnp.where(lane_lo, halves[0], halves[1]))
    y = y + jnp.concatenate(pieces, axis=1)

    upd = jnp.dot(bm.T.astype(BF16), (xdt * dec_full).astype(BF16), preferred_element_type=F32)
    r_io = lax.broadcasted_iota(jnp.int32, upd.shape, 0) // SSM_STATE
    c_io = lax.broadcasted_iota(jnp.int32, upd.shape, 1) // (SSM_HEAD_DIM * rep)
    st_ref[...] = st * cdec_full + jnp.where(r_io == c_io, upd, 0.0)

    y = y * _silu(z_ref[...])
    gw = SSM_WIDTH // SSM_GROUPS
    for g in range(SSM_GROUPS):
        yg = y[:, g * gw:(g + 1) * gw]
        ms = jnp.mean(yg * yg, axis=-1, keepdims=True)
        o_ref[:, g * gw:(g + 1) * gw] = (yg * lax.rsqrt(ms + EPS) * ng_ref[:, g * gw:(g + 1) * gw]).astype(BF16)


def _ssd(pf, cwx, cwb, cbx, cbb, dtb, alog, dskip, ng, batch, seq):
    nb = seq // BLK
    row = lambda w, c: pl.BlockSpec((BLK, w), lambda b, j: (b * nb + j, c))
    const = lambda r, w: pl.BlockSpec((r, w), lambda b, j: (0, 0))
    return pl.pallas_call(
        _ssd_kernel,
        grid=(batch, nb),
        in_specs=[
            row(SSM_WIDTH, C_Z // SSM_WIDTH), row(SSM_WIDTH, C_XS // SSM_WIDTH), row(2 * LANES, C_BC // (2 * LANES)),
            row(LANES, C_DT // LANES),
            const(SSM_CONV, SSM_WIDTH), const(SSM_CONV, 2 * LANES), const(1, SSM_WIDTH), const(1, 2 * LANES),
            const(1, LANES), const(1, LANES), const(1, SSM_WIDTH), const(1, SSM_WIDTH),
        ],
        out_specs=pl.BlockSpec((BLK, SSM_WIDTH), lambda b, j: (b * nb + j, 0)),
        out_shape=jax.ShapeDtypeStruct((batch * seq, SSM_WIDTH), BF16),
        scratch_shapes=[
            pltpu.VMEM((BLK + 8, SSM_WIDTH), F32),
            pltpu.VMEM((BLK + 8, 2 * LANES), F32),
            pltpu.VMEM((SSM_GROUPS * SSM_STATE, SSM_WIDTH), F32),
        ],
        compiler_params=_cparams(("parallel", "arbitrary")),
        name="ssd_mixer",
    )(pf, pf, pf, pf, cwx, cwb, cbx, cbb, dtb, alog, dskip, ng)


def _rotary(u, cos, sin_signed):
    first = (lax.broadcasted_iota(jnp.int32, u.shape, 1) % RET_KEY_DIM) < (RET_KEY_DIM // 2)
    half = RET_KEY_DIM // 2
    swapped = jnp.where(first, pltpu.roll(u, u.shape[1] - half, axis=1), pltpu.roll(u, half, axis=1))
    return u * cos + swapped * sin_signed


def _ret_kernel(q_ref, k_ref, v_ref, g_ref, cos_ref, sin_ref, dmat_ref, qdec_ref, kdec_ref, gdec_ref,
                bd_ref, gain_ref, o_ref, st_ref):
    j = pl.program_id(1)

    @pl.when(j == 0)
    def _reset():
        st_ref[...] = jnp.zeros(st_ref.shape, F32)

    cos = _lane_tile(cos_ref[...], RET_WIDTH // LANES)
    sin = _lane_tile(sin_ref[...], RET_WIDTH // LANES)
    q = _rotary(q_ref[...], cos, sin)
    k = _rotary(k_ref[...], cos, sin) * (RET_KEY_DIM ** -0.5)
    v_b = v_ref[...].astype(BF16)
    q_b = q.astype(BF16)
    k_b = k.astype(BF16)

    st = st_ref[...]
    o = jnp.dot((q * qdec_ref[...]).astype(BF16), st.astype(BF16), preferred_element_type=F32)
    for h in range(RET_HEADS):
        cm = _group_mask(q_b.shape, 1, RET_KEY_DIM, h)
        qh = jnp.where(cm, q_b, jnp.zeros_like(q_b))
        s = lax.dot_general(qh, k_b, (((1,), (1,)), ((), ())), preferred_element_type=F32) * dmat_ref[h]
        oh = jnp.dot(s.astype(BF16), v_b, preferred_element_type=F32)
        o = o + jnp.where(cm, oh, 0.0)

    upd = jnp.dot((k * kdec_ref[...]).T.astype(BF16), v_b, preferred_element_type=F32)
    st_ref[...] = st * gdec_ref[...] + upd * bd_ref[...]

    out = jnp.zeros(o.shape, F32)
    for h in range(RET_HEADS):
        cm = _group_mask(o.shape, 1, RET_KEY_DIM, h)
        ms = jnp.sum(jnp.where(cm, o * o, 0.0), axis=-1, keepdims=True) * (1.0 / RET_KEY_DIM)
        out = out + jnp.where(cm, o * lax.rsqrt(ms + EPS), 0.0)
    o_ref[...] = (_silu(g_ref[...]) * (out * gain_ref[...])).astype(BF16)


def _retention(pf, cos, sin, dmat, qdec, kdec, gdec, bd, gain, batch, seq):
    nb = seq // BLK
    row = lambda c: pl.BlockSpec((BLK, RET_WIDTH), lambda b, j: (b * nb + j, c))
    const2 = pl.BlockSpec((BLK, RET_WIDTH), lambda b, j: (0, 0))
    return pl.pallas_call(
        _ret_kernel,
        grid=(batch, nb),
        in_specs=[
            row(C_RQ // RET_WIDTH), row(C_RK // RET_WIDTH), row(C_RV // RET_WIDTH), row(C_RG // RET_WIDTH),
            pl.BlockSpec((BLK, LANES), lambda b, j: (j, 0)),
            pl.BlockSpec((BLK, LANES), lambda b, j: (j, 0)),
            pl.BlockSpec((RET_HEADS, BLK, BLK), lambda b, j: (0, 0, 0)),
            const2, const2, const2, const2,
            pl.BlockSpec((1, RET_WIDTH), lambda b, j: (0, 0)),
        ],
        out_specs=pl.BlockSpec((BLK, RET_WIDTH), lambda b, j: (b * nb + j, 0)),
        out_shape=jax.ShapeDtypeStruct((batch * seq, RET_WIDTH), BF16),
        scratch_shapes=[pltpu.VMEM((RET_WIDTH, RET_WIDTH), F32)],
        compiler_params=_cparams(("parallel", "arbitrary")),
        name="retention",
    )(pf, pf, pf, pf, cos, sin, dmat, qdec, kdec, gdec, bd, gain)


def _retention_tables(seq):
    f32 = np.float32
    inv = (f32(1.0) / (f32(ROPE_BASE) ** (np.arange(0, RET_KEY_DIM, 2, dtype=f32) / f32(RET_KEY_DIM)))).astype(f32)
    ang = (np.arange(seq, dtype=f32)[:, None] * inv[None, :]).astype(f32)
    cos_h = np.concatenate([np.cos(ang), np.cos(ang)], axis=-1)
    sin_h = np.concatenate([-np.sin(ang), np.sin(ang)], axis=-1)
    cos = np.tile(cos_h, (1, 2)).astype(f32)
    sin = np.tile(sin_h, (1, 2)).astype(f32)
    log_gamma = np.log1p(-np.power(f32(2.0), f32(-5.0) - np.arange(RET_HEADS, dtype=f32))).astype(f32)
    idx = np.arange(BLK, dtype=f32)
    ii = np.arange(BLK)
    same_or_earlier_chunk = (ii[None, :] // CHUNK) <= (ii[:, None] // CHUNK)
    dmat = np.exp(log_gamma[:, None, None] * np.abs(idx[:, None] - idx[None, :]))
    dmat = np.where(same_or_earlier_chunk[None], dmat, 0.0).astype(f32)
    lg_cols = np.repeat(log_gamma, RET_KEY_DIM)
    qdec = np.exp(lg_cols[None, :] * (idx + 1.0)[:, None]).astype(f32)
    kdec = np.exp(lg_cols[None, :] * (BLK - 1.0 - idx)[:, None]).astype(f32)
    head_of = np.arange(RET_WIDTH) // RET_KEY_DIM
    bd = (head_of[:, None] == head_of[None, :]).astype(f32)
    gdec = (np.exp(lg_cols * f32(BLK))[:, None] * bd).astype(f32)
    return tuple(jnp.asarray(a) for a in (cos, sin, dmat, qdec, kdec, gdec, bd))


def _outproj_router_kernel(x_ref, a_ref, b_ref, c_ref, w_ref, g_ref, rw_ref, xo_ref, h_ref, rinfo_ref):
    acc = x_ref[...]
    acc = acc + jnp.dot(a_ref[...], w_ref[0:DA_WIDTH, :], preferred_element_type=F32)
    acc = acc + jnp.dot(b_ref[...], w_ref[DA_WIDTH:DA_WIDTH + SSM_WIDTH, :], preferred_element_type=F32)
    acc = acc + jnp.dot(c_ref[...], w_ref[DA_WIDTH + SSM_WIDTH:, :], preferred_element_type=F32)
    xo_ref[...] = acc
    ms = jnp.mean(acc * acc, axis=-1, keepdims=True)
    hf = acc * lax.rsqrt(ms + EPS) * g_ref[...]
    h = hf.astype(BF16)
    h_ref[...] = hf
    logits = jnp.dot(h, rw_ref[...], preferred_element_type=F32)
    lane = lax.broadcasted_iota(jnp.int32, logits.shape, 1).astype(F32)
    logits = jnp.where(lane < N_EXPERTS, logits, NEG_BIG)
    v1 = jnp.max(logits, axis=-1, keepdims=True)
    i1 = jnp.min(jnp.where(logits == v1, lane, float(LANES)), axis=-1, keepdims=True)
    rest_l = jnp.where(lane == i1, NEG_BIG, logits)
    v2 = jnp.max(rest_l, axis=-1, keepdims=True)
    i2 = jnp.min(jnp.where(rest_l == v2, lane, float(LANES)), axis=-1, keepdims=True)
    e2 = jnp.exp(v2 - v1)
    g1 = 1.0 / (1.0 + e2)
    g2 = e2 * g1
    rinfo_ref[...] = (jnp.where(lane == 0.0, g1, 0.0) + jnp.where(lane == 1.0, g2, 0.0)
                      + jnp.where(lane == 2.0, i1, 0.0) + jnp.where(lane == 3.0, i2, 0.0))


def _outproj_router(x2, oa, ob, oc, w, gain, router_w):
    t = x2.shape[0]
    tok = lambda w_: pl.BlockSpec((TM_PROJ, w_), lambda i: (i, 0))
    return pl.pallas_call(
        _outproj_router_kernel,
        grid=(t // TM_PROJ,),
        in_specs=[tok(D_MODEL), tok(DA_WIDTH), tok(SSM_WIDTH), tok(RET_WIDTH),
                  pl.BlockSpec((D_MODEL, D_MODEL), lambda i: (0, 0)),
                  pl.BlockSpec((1, D_MODEL), lambda i: (0, 0)),
                  pl.BlockSpec((D_MODEL, LANES), lambda i: (0, 0))],
        out_specs=[tok(D_MODEL), tok(D_MODEL), tok(LANES)],
        out_shape=[jax.ShapeDtypeStruct((t, D_MODEL), F32), jax.ShapeDtypeStruct((t, D_MODEL), F32),
                   jax.ShapeDtypeStruct((t, LANES), F32)],
        compiler_params=_cparams(("parallel",)),
        name="outproj_router",
    )(x2, oa, ob, oc, w, gain, router_w)


def _ffn_kernel(x_ref, a_ref, b_ref, c_ref, wo_ref, g_ref, wg_ref, wu_ref, wd_ref, o_ref, acc_ref, h_ref):
    f = pl.program_id(1)

    @pl.when(f == 0)
    def _init():
        acc = x_ref[...]
        acc = acc + jnp.dot(a_ref[...], wo_ref[0:DA_WIDTH, :], preferred_element_type=F32)
        acc = acc + jnp.dot(b_ref[...], wo_ref[DA_WIDTH:DA_WIDTH + SSM_WIDTH, :], preferred_element_type=F32)
        acc = acc + jnp.dot(c_ref[...], wo_ref[DA_WIDTH + SSM_WIDTH:, :], preferred_element_type=F32)
        acc_ref[...] = acc
        ms = jnp.mean(acc * acc, axis=-1, keepdims=True)
        h_ref[...] = (acc * lax.rsqrt(ms + EPS) * g_ref[...]).astype(BF16)

    h = h_ref[...]
    gate = jnp.dot(h, wg_ref[0], preferred_element_type=F32)
    up = jnp.dot(h, wu_ref[0], preferred_element_type=F32)
    act = (_silu(gate) * up).astype(BF16)
    acc_ref[...] += jnp.dot(act, wd_ref[...], preferred_element_type=F32)

    @pl.when(f == pl.num_programs(1) - 1)
    def _done():
        o_ref[...] = acc_ref[...]


def _outproj_dense_ffn(x2, oa, ob, oc, w_o, gain, wg, wu, wd):
    t = x2.shape[0]
    nf = wg.shape[0]
    tok = lambda w_: pl.BlockSpec((TM_FFN, w_), lambda i, f: (i, 0))
    return pl.pallas_call(
        _ffn_kernel,
        grid=(t // TM_FFN, nf),
        in_specs=[
            tok(D_MODEL), tok(DA_WIDTH), tok(SSM_WIDTH), tok(RET_WIDTH),
            pl.BlockSpec((D_MODEL, D_MODEL), lambda i, f: (0, 0)),
            pl.BlockSpec((1, D_MODEL), lambda i, f: (0, 0)),
            pl.BlockSpec((1, D_MODEL, TF_FFN), lambda i, f: (f, 0, 0)),
            pl.BlockSpec((1, D_MODEL, TF_FFN), lambda i, f: (f, 0, 0)),
            pl.BlockSpec((TF_FFN, D_MODEL), lambda i, f: (f, 0)),
        ],
        out_specs=pl.BlockSpec((TM_FFN, D_MODEL), lambda i, f: (i, 0)),
        out_shape=jax.ShapeDtypeStruct((t, D_MODEL), F32),
        scratch_shapes=[pltpu.VMEM((TM_FFN, D_MODEL), F32), pltpu.VMEM((TM_FFN, D_MODEL), BF16)],
        compiler_params=_cparams(("parallel", "arbitrary")),
        name="outproj_dense_swiglu",
    )(x2, oa, ob, oc, w_o, gain, wg, wu, wd)


def _issue_row_gather(src_hbm, idx_ref, idx_base, dst, sem, n_rows):
    def body(r, carry):
        tok = idx_ref[idx_base + r]
        pltpu.make_async_copy(src_hbm.at[pl.ds(tok, 1)], dst.at[pl.ds(r, 1)], sem).start()
        return carry
    lax.fori_loop(0, n_rows, body, 0, unroll=8)


def _wait_row_gather(src_hbm, dst, sem, n_rows):
    pltpu.make_async_copy(src_hbm.at[pl.ds(0, n_rows)], dst, sem).wait()


def _moe_ffn_kernel(te_ref, nv_ref, tb_ref, rt_ref, h_hbm, wg_ref, wu_ref, wd_ref, y_ref, hbuf, xb_ref, acc_ref, sem):
    i = pl.program_id(0)
    f = pl.program_id(1)
    n_valid = nv_ref[0]
    valid = i < n_valid
    slot = i % 2

    @pl.when((f == 0) & (i == 0) & valid)
    def _first():
        _issue_row_gather(h_hbm, rt_ref, tb_ref[0], hbuf.at[0], sem.at[0], TR_MOE)

    @pl.when((f == 0) & valid)
    def _arrive():
        _wait_row_gather(h_hbm, hbuf.at[slot], sem.at[slot], TR_MOE)
        xb_ref[...] = hbuf[slot].astype(BF16)

    def compute(prefetch_next):
        if prefetch_next:
            rows_per_step = TR_MOE // (FFN_EXPERT // TF_MOE)
            r0 = f * rows_per_step
            src0 = tb_ref[i + 1] + r0
            for j in range(rows_per_step):
                tok = rt_ref[src0 + j]
                pltpu.make_async_copy(h_hbm.at[pl.ds(tok, 1)], hbuf.at[1 - slot, pl.ds(r0 + j, 1)],
                                      sem.at[1 - slot]).start()
        xb = xb_ref[...]
        gate = jnp.dot(xb, wg_ref[0].astype(BF16), preferred_element_type=F32)
        up = jnp.dot(xb, wu_ref[0].astype(BF16), preferred_element_type=F32)
        act = (_silu(gate) * up).astype(BF16)
        contrib = jnp.dot(act, wd_ref[0].astype(BF16), preferred_element_type=F32)

        @pl.when(f == 0)
        def _():
            acc_ref[...] = contrib

        @pl.when(f > 0)
        def _():
            acc_ref[...] += contrib

    has_next = i + 1 < n_valid

    @pl.when(valid & has_next)
    def _compute_and_prefetch():
        compute(True)

    @pl.when(valid & jnp.logical_not(has_next))
    def _compute_last():
        compute(False)

    @pl.when(f == pl.num_programs(1) - 1)
    def _store():
        y_ref[...] = jnp.where(valid, acc_ref[...], 0.0)


def _moe_ffn(h, tile_expert, n_valid, tile_base, sorted_token, wg, wu, wd):
    n_tiles = tile_expert.shape[0]
    nf = wg.shape[2] // TF_MOE

    def w_in_map(i, f, te, nv, tb, rt):
        return (te[i], 0, jnp.where(i < nv[0], f, nf - 1))

    def w_out_map(i, f, te, nv, tb, rt):
        return (te[i], jnp.where(i < nv[0], f, nf - 1), 0)

    grid_spec = pltpu.PrefetchScalarGridSpec(
        num_scalar_prefetch=4,
        grid=(n_tiles, nf),
        in_specs=[
            pl.BlockSpec(memory_space=pl.ANY),
            pl.BlockSpec((1, D_MODEL, TF_MOE), w_in_map),
            pl.BlockSpec((1, D_MODEL, TF_MOE), w_in_map),
            pl.BlockSpec((1, TF_MOE, D_MODEL), w_out_map),
        ],
        out_specs=pl.BlockSpec((TR_MOE, D_MODEL), lambda i, f, te, nv, tb, rt: (i, 0)),
        scratch_shapes=[
            pltpu.VMEM((2, TR_MOE, D_MODEL), F32),
            pltpu.VMEM((TR_MOE, D_MODEL), BF16),
            pltpu.VMEM((TR_MOE, D_MODEL), F32),
            pltpu.SemaphoreType.DMA((2,)),
        ],
    )
    return pl.pallas_call(
        _moe_ffn_kernel,
        grid_spec=grid_spec,
        out_shape=jax.ShapeDtypeStruct((n_tiles * TR_MOE, D_MODEL), F32),
        compiler_params=_cparams(("arbitrary", "arbitrary")),
        name="moe_expert_ffn",
    )(tile_expert, n_valid, tile_base, sorted_token, h, wg, wu, wd)


def _moe_combine_kernel(tr_ref, x_ref, rinfo_ref, y_hbm, fg_ref, o_ref, ybuf, sem):
    i = pl.program_id(0)
    slot = i % 2
    n_rows = TOP_K * TM_COMB

    @pl.when(i == 0)
    def _first():
        _issue_row_gather(y_hbm, tr_ref, 0, ybuf.at[0], sem.at[0], n_rows)

    _wait_row_gather(y_hbm, ybuf.at[slot], sem.at[slot], n_rows)

    @pl.when(i + 1 < pl.num_programs(0))
    def _prefetch():
        src0 = (i + 1) * n_rows
        dst = ybuf.at[1 - slot]
        for r in range(n_rows):
            pltpu.make_async_copy(y_hbm.at[pl.ds(tr_ref[src0 + r], 1)], dst.at[pl.ds(r, 1)], sem.at[1 - slot]).start()

    rinfo = rinfo_ref[...]
    g1 = jnp.broadcast_to(rinfo[:, 0:1], (TM_COMB, D_MODEL))
    g2 = jnp.broadcast_to(rinfo[:, 1:2], (TM_COMB, D_MODEL))
    y = x_ref[...] + g1 * ybuf[slot, 0:TM_COMB, :] + g2 * ybuf[slot, TM_COMB:n_rows, :]
    ms = jnp.mean(y * y, axis=-1, keepdims=True)
    o_ref[...] = y * lax.rsqrt(ms + EPS) * fg_ref[...]


def _moe_combine(x2, rinfo, y_rows, tok_rows, final_gain):
    t = x2.shape[0]
    grid_spec = pltpu.PrefetchScalarGridSpec(
        num_scalar_prefetch=1,
        grid=(t // TM_COMB,),
        in_specs=[
            pl.BlockSpec((TM_COMB, D_MODEL), lambda i, tr: (i, 0)),
            pl.BlockSpec((TM_COMB, LANES), lambda i, tr: (i, 0)),
            pl.BlockSpec(memory_space=pl.ANY),
            pl.BlockSpec((1, D_MODEL), lambda i, tr: (0, 0)),
        ],
        out_specs=pl.BlockSpec((TM_COMB, D_MODEL), lambda i, tr: (i, 0)),
        scratch_shapes=[
            pltpu.VMEM((2, TOP_K * TM_COMB, D_MODEL), F32),
            pltpu.SemaphoreType.DMA((2,)),
        ],
    )
    return pl.pallas_call(
        _moe_combine_kernel,
        grid_spec=grid_spec,
        out_shape=jax.ShapeDtypeStruct((t, D_MODEL), F32),
        compiler_params=_cparams(("arbitrary",)),
        name="moe_combine_norm",
    )(tok_rows, x2, rinfo, y_rows, final_gain)


def _routing_tables(rinfo):
    t = rinfo.shape[0]
    n_assign = TOP_K * t
    n_tiles = -(-n_assign // TR_MOE) + N_EXPERTS
    flat_e = rinfo[:, 2:2 + TOP_K].astype(jnp.int32).reshape(-1)
    onehot = (flat_e[:, None] == jnp.arange(N_EXPERTS)[None, :]).astype(jnp.int32)
    blocks = onehot.astype(BF16).reshape(n_assign // LANES, LANES, N_EXPERTS)
    tri = (jnp.arange(LANES)[:, None] >= jnp.arange(LANES)[None, :]).astype(BF16)
    within = jnp.einsum("ij,bjk->bik", tri, blocks, preferred_element_type=F32)
    block_end = jnp.cumsum(within[:, -1, :], axis=0)
    csum = (within + (block_end - within[:, -1, :])[:, None, :]).reshape(n_assign, N_EXPERTS).astype(jnp.int32)
    rank = jnp.sum((csum - onehot) * onehot, axis=1)
    counts = csum[-1]
    tiles_per_e = (counts + TR_MOE - 1) // TR_MOE
    tile_end = jnp.cumsum(tiles_per_e)
    first_tile = tile_end - tiles_per_e
    dest = (first_tile * TR_MOE)[flat_e] + rank
    order = jnp.sort(flat_e * n_assign + jnp.arange(n_assign, dtype=jnp.int32))
    sorted_token = jnp.concatenate([(order % n_assign) // TOP_K, jnp.zeros((TR_MOE,), jnp.int32)])
    start = jnp.cumsum(counts) - counts
    n_valid = tile_end[-1:].astype(jnp.int32)
    tile_ids = jnp.arange(n_tiles)
    tile_expert = jnp.sum(tile_ids[:, None] >= tile_end[None, :], axis=1).astype(jnp.int32)
    last_e = jnp.sum(n_valid[0] - 1 >= tile_end).astype(jnp.int32)
    tile_expert = jnp.where(tile_ids < n_valid[0], tile_expert, last_e)
    tile_base = start[tile_expert] + (tile_ids - first_tile[tile_expert]) * TR_MOE
    tile_base = jnp.clip(tile_base, 0, n_assign).astype(jnp.int32)
    tok_rows = dest.reshape(t // TM_COMB, TM_COMB, TOP_K).transpose(0, 2, 1).reshape(-1).astype(jnp.int32)
    return tile_expert, n_valid, tile_base, sorted_token, tok_rows


def _permute_w_in(w):
    sizes = (256, 256, 256, 512, 768, 8, 256, 256, 256, 256)
    offs = np.concatenate([[0], np.cumsum(sizes)])
    part = lambda n: w[:, offs[n]:offs[n + 1]]
    dt_pad = jnp.zeros((w.shape[0], LANES - SSM_HEADS), w.dtype)
    cols = [part(0) * (DA_HEAD_DIM ** -0.5 * LOG2E), part(2), part(1), part(3), part(4),
            part(6), part(7), part(8), part(9), part(5), dt_pad]
    return jnp.concatenate(cols, axis=1).astype(BF16)


def _column_tiles(w, tile):
    *lead, k, n = w.shape
    return jnp.moveaxis(w.reshape(*lead, k, n // tile, tile), -2, -3)


def _pad_lanes(v, fill=0.0):
    return jnp.concatenate([v, jnp.full((LANES - v.shape[0],), fill, v.dtype)])[None, :]


def kernel(x, w_in, w_out, attn_norm, ffn_norm, final_norm, rel_bias, lambda_q1, lambda_k1, lambda_q2,
           lambda_k2, da_head_norm, conv_w, conv_b, dt_bias, a_log, d_skip, ssm_norm, ret_head_norm,
           w_gate, w_up, w_down, router_w, e_gate, e_up, e_down):
    batch, seq, _ = x.shape
    assert DEPTH % 2 == 0, "the final RMSNorm is fused into the last (expert) layer's combine kernel"
    assert seq % max(ATT_T, TM_PROJ, BLK) == 0 and (batch * seq) % TM_FFN == 0
    x2 = x.reshape(batch * seq, D_MODEL)
    bias_tiles = _bias_tiles(rel_bias)
    ret_tabs = _retention_tables(seq)

    for layer in range(DEPTH):
        lam_init = 0.8 - 0.6 * math.exp(-0.3 * layer)
        lam = (jnp.exp(jnp.sum(lambda_q1[layer] * lambda_k1[layer]))
               - jnp.exp(jnp.sum(lambda_q2[layer] * lambda_k2[layer])) + lam_init).astype(F32)
        lam = jnp.full((1, ATT_T), lam, F32)

        qkv, qvt, pf = _inproj(x2, attn_norm[layer][None, :], _permute_w_in(w_in[layer]))

        da_gain = (jnp.tile(da_head_norm[layer], DA_HEADS) * (1.0 - lam_init))[:, None]
        out_a = _attention(qkv, qvt, bias_tiles, da_gain, lam, batch, seq)

        cw, cb = conv_w[layer], conv_b[layer]
        out_b = _ssd(pf, cw[:, :SSM_WIDTH], cw[:, SSM_WIDTH:], cb[None, :SSM_WIDTH], cb[None, SSM_WIDTH:],
                     _pad_lanes(dt_bias[layer]), _pad_lanes(a_log[layer]),
                     jnp.repeat(d_skip[layer], SSM_HEAD_DIM)[None, :], ssm_norm[layer][None, :], batch, seq)

        ret_gain = jnp.tile(ret_head_norm[layer], RET_HEADS)[None, :]
        out_c = _retention(pf, *ret_tabs, ret_gain, batch, seq)

        w_o = w_out[layer].astype(BF16)
        if layer % 2 == 0:
            i = layer // 2
            pad = FFN_PAD - FFN_DENSE
            wg = _column_tiles(jnp.pad(w_gate[i], ((0, 0), (0, pad))).astype(BF16), TF_FFN)
            wu = _column_tiles(jnp.pad(w_up[i], ((0, 0), (0, pad))).astype(BF16), TF_FFN)
            wd = jnp.pad(w_down[i], ((0, pad), (0, 0))).astype(BF16)
            x2 = _outproj_dense_ffn(x2, out_a, out_b, out_c, w_o, ffn_norm[layer][None, :], wg, wu, wd)
        else:
            i = layer // 2
            rw = jnp.pad(router_w[i], ((0, 0), (0, LANES - N_EXPERTS))).astype(BF16)
            x2, h, rinfo = _outproj_router(x2, out_a, out_b, out_c, w_o, ffn_norm[layer][None, :], rw)
            tile_expert, n_valid, tile_base, sorted_token, tok_rows = _routing_tables(rinfo)
            y_rows = _moe_ffn(h, tile_expert, n_valid, tile_base, sorted_token,
                              e_gate[i], e_up[i], e_down[i])
            x2 = _moe_combine(x2, rinfo, y_rows, tok_rows, final_norm[None, :])
    return x2.reshape(batch, seq, D_MODEL)
```

```python
import math

import jax
import jax.numpy as jnp
import numpy as np
from jax import lax
from jax.experimental import pallas as pl
from jax.experimental.pallas import tpu as pltpu

F32 = jnp.float32
BF16 = jnp.bfloat16

D_MODEL = 1024
DEPTH = 2
CHUNK = 64
EPS = 1e-6
DA_HEADS = 4
DA_HEAD_DIM = 32
DA_V_DIM = 64
DA_WIDTH = 256
SSM_HEADS = 8
SSM_HEAD_DIM = 64
SSM_WIDTH = 512
SSM_STATE = 64
SSM_GROUPS = 2
SSM_CONV = 4
RET_HEADS = 4
RET_KEY_DIM = 64
RET_WIDTH = 256
ROPE_BASE = 10000.0
REL_BUCKETS = 32
REL_MAX_DIST = 128
FFN_DENSE = 2752
N_EXPERTS = 8
FFN_EXPERT = 3584

LANES = 128
VMEM_LIMIT = 48 * 1024 * 1024
NEG_BIG = -1e30

TM_PROJ = 512
ATT_T = 512
LOG2E = math.log2(math.e)
BLK = 256
FFN_PAD = 2816
TM_FFN = 512
TF_FFN = 1408
TF_MOE = 512
TR_MOE = 896
TM_COMB = 256
TOP_K = 2

C_Z, C_XS, C_BC, C_RQ, C_RK, C_RV, C_RG, C_DT = 0, 512, 1024, 1280, 1536, 1792, 2048, 2304
PF_W = 2432
PF_CHUNK = 640
QKV_W = 768


def _cparams(sem):
    return pltpu.CompilerParams(dimension_semantics=sem, vmem_limit_bytes=VMEM_LIMIT)


def _group_mask(shape, axis, group, idx):
    io = lax.broadcasted_iota(jnp.int32, shape, axis)
    return (io >= idx * group) & (io < (idx + 1) * group)


def _lane_tile(x, n, axis=1):
    return jnp.concatenate([x] * n, axis=axis)


def _silu(x):
    h = 0.5 * x
    return h + h * jnp.tanh(h)


def _inproj_kernel(x_ref, g_ref, w_ref, qkv_ref, qvt_ref, pf_ref):
    x = x_ref[...]
    ms = jnp.mean(x * x, axis=-1, keepdims=True)
    h = (x * lax.rsqrt(ms + EPS) * g_ref[...]).astype(BF16)
    qkv = jnp.dot(h, w_ref[:, :QKV_W], preferred_element_type=F32)
    qkv_ref[...] = qkv.astype(BF16)
    qvt_ref[...] = qkv[:, :2 * DA_WIDTH].T.astype(BF16)
    for lo in range(0, PF_W, PF_CHUNK):
        hi = min(lo + PF_CHUNK, PF_W)
        pf_ref[:, lo:hi] = jnp.dot(h, w_ref[:, QKV_W + lo:QKV_W + hi], preferred_element_type=F32)


def _inproj(x2, gain, w):
    t = x2.shape[0]
    return pl.pallas_call(
        _inproj_kernel,
        grid=(t // TM_PROJ,),
        in_specs=[
            pl.BlockSpec((TM_PROJ, D_MODEL), lambda i: (i, 0)),
            pl.BlockSpec((1, D_MODEL), lambda i: (0, 0)),
            pl.BlockSpec((D_MODEL, QKV_W + PF_W), lambda i: (0, 0)),
        ],
        out_specs=[
            pl.BlockSpec((TM_PROJ, QKV_W), lambda i: (i, 0)),
            pl.BlockSpec((2 * DA_WIDTH, TM_PROJ), lambda i: (0, i)),
            pl.BlockSpec((TM_PROJ, PF_W), lambda i: (i, 0)),
        ],
        out_shape=[
            jax.ShapeDtypeStruct((t, QKV_W), BF16),
            jax.ShapeDtypeStruct((2 * DA_WIDTH, t), BF16),
            jax.ShapeDtypeStruct((t, PF_W), F32),
        ],
        compiler_params=_cparams(("parallel",)),
        name="inproj",
    )(x2, gain, w)


def _attn_kernel(qi_ref, kj_ref, qt_ref, k_ref, vt_ref, bias_ref, gain_ref, lam_ref, o_ref,
                 qm_ref, m_ref, l_ref, acc_ref, s_ref, e_ref, al_ref):
    p = pl.program_id(1)
    qi = qi_ref[p]
    kj = kj_ref[p]
    n_ht = 2 * DA_HEADS

    @pl.when(kj == 0)
    def _init():
        qt = qt_ref[...]
        for ht in range(n_ht):
            qm_ref[ht] = jnp.where(_group_mask(qt.shape, 0, DA_HEAD_DIM, ht), qt, jnp.zeros_like(qt))
        m_ref[...] = jnp.full(m_ref.shape, NEG_BIG, F32)
        l_ref[...] = jnp.zeros(l_ref.shape, F32)
        acc_ref[...] = jnp.zeros(acc_ref.shape, F32)

    def step(with_bias):
        k = k_ref[...]
        def scores(a, buf):
            s_ref[buf] = jnp.dot(k, qm_ref[a], preferred_element_type=F32)

        def softmax(b, buf):
            for c0 in range(0, ATT_T, LANES):
                cols = slice(c0, c0 + LANES)

                def block():
                    blk = s_ref[buf, :, cols]
                    return blk + bias_ref[0, b // 2, :, cols] if with_bias else blk

                m_prev = m_ref[b, :, cols]
                m_new = jnp.maximum(m_prev, jnp.max(block(), axis=0, keepdims=True))
                alpha = jnp.exp2(m_prev - m_new)
                e = jnp.exp2(block() - m_new)
                l_ref[b, :, cols] = alpha * l_ref[b, :, cols] + jnp.sum(e, axis=0, keepdims=True)
                m_ref[b, :, cols] = m_new
                al_ref[buf, :, cols] = alpha
                e_ref[buf, :, cols] = e.astype(BF16)

        def values(c, buf):
            r0 = (c // 4) * LANES
            if not isinstance(r0, int):
                r0 = pl.multiple_of(r0, LANES)
            pv = jnp.dot(vt_ref[pl.ds(r0, LANES), :], e_ref[buf], preferred_element_type=F32)
            acc_ref[c] = acc_ref[c] * al_ref[buf] + pv

        scores(0, 0)
        scores(1, 1)
        softmax(0, 0)

        def trip(u, carry):
            t = 2 * u + 1
            scores(t + 1, 0)
            softmax(t, 1)
            values(t - 1, 0)
            scores(t + 2, 1)
            softmax(t + 1, 0)
            values(t, 1)
            return carry

        for u in range((n_ht - 2) // 2):
            trip(u, 0)
        softmax(n_ht - 1, 1)
        values(n_ht - 2, 0)
        values(n_ht - 1, 1)

    @pl.when(kj >= qi - 1)
    def _near():
        step(True)

    @pl.when(kj < qi - 1)
    def _far():
        step(False)

    @pl.when(kj == qi)
    def _finish():
        lam = lam_ref[...]
        row_lo = lax.broadcasted_iota(jnp.int32, (LANES, ATT_T), 0) < DA_V_DIM
        for pair in range(DA_HEADS // 2):
            halves = []
            for sub in range(2):
                h = 2 * pair + sub
                o = acc_ref[2 * h] * (1.0 / l_ref[2 * h]) - lam * (acc_ref[2 * h + 1] * (1.0 / l_ref[2 * h + 1]))
                own = row_lo if sub == 0 else jnp.logical_not(row_lo)
                ms = jnp.sum(jnp.where(own, o * o, 0.0), axis=0, keepdims=True) * (1.0 / DA_V_DIM)
                halves.append(o * lax.rsqrt(ms + EPS))
            r0 = pair * LANES
            blk = jnp.where(row_lo, halves[0], halves[1]) * gain_ref[r0:r0 + LANES, :]
            o_ref[:, r0:r0 + LANES] = blk.T.astype(BF16)


def _attention(qkv, qvt, bias_tiles, gain, lam, batch, seq):
    nq = seq // ATT_T
    qi_l, kj_l = [], []
    for qi in range(nq):
        for kj in range(qi + 1):
            qi_l.append(qi)
            kj_l.append(kj)
    qi_tbl = jnp.asarray(np.array(qi_l, np.int32))
    kj_tbl = jnp.asarray(np.array(kj_l, np.int32))
    n_pairs = len(qi_l)
    n_ht = 2 * DA_HEADS

    grid_spec = pltpu.PrefetchScalarGridSpec(
        num_scalar_prefetch=2,
        grid=(batch, n_pairs),
        in_specs=[
            pl.BlockSpec((DA_WIDTH, ATT_T), lambda b, p, qi, kj: (0, b * nq + qi[p])),
            pl.BlockSpec((ATT_T, DA_WIDTH), lambda b, p, qi, kj: (b * nq + kj[p], 2)),
            pl.BlockSpec((DA_WIDTH, ATT_T), lambda b, p, qi, kj: (1, b * nq + kj[p])),
            pl.BlockSpec((1, DA_HEADS, ATT_T, ATT_T),
                         lambda b, p, qi, kj: (jnp.minimum(qi[p] - kj[p], 1), 0, 0, 0)),
            pl.BlockSpec((DA_WIDTH, 1), lambda b, p, qi, kj: (0, 0)),
            pl.BlockSpec((1, ATT_T), lambda b, p, qi, kj: (0, 0)),
        ],
        out_specs=pl.BlockSpec((ATT_T, DA_WIDTH), lambda b, p, qi, kj: (b * nq + qi[p], 0)),
        scratch_shapes=[
            pltpu.VMEM((n_ht, DA_WIDTH, ATT_T), BF16),
            pltpu.VMEM((n_ht, 1, ATT_T), F32),
            pltpu.VMEM((n_ht, 1, ATT_T), F32),
            pltpu.VMEM((n_ht, LANES, ATT_T), F32),
            pltpu.VMEM((2, ATT_T, ATT_T), F32),
            pltpu.VMEM((2, ATT_T, ATT_T), BF16),
            pltpu.VMEM((2, 1, ATT_T), F32),
        ],
    )
    return pl.pallas_call(
        _attn_kernel,
        grid_spec=grid_spec,
        out_shape=jax.ShapeDtypeStruct((batch * seq, DA_WIDTH), BF16),
        compiler_params=_cparams(("parallel", "arbitrary")),
        name="diff_attention",
    )(qi_tbl, kj_tbl, qvt, qkv, qvt, bias_tiles, gain, lam)


def _t5_bucket(rel):
    half = REL_BUCKETS // 2
    max_exact = half // 2
    ret = jnp.where(rel > 0, half, 0)
    n = jnp.abs(rel)
    nf = jnp.maximum(n, 1).astype(F32)
    large = max_exact + (jnp.log(nf / max_exact) / math.log(REL_MAX_DIST / max_exact)
                         * (half - max_exact)).astype(jnp.int32)
    large = jnp.minimum(large, half - 1)
    return ret + jnp.where(n < max_exact, n, large)


def _bias_tiles(rel_bias):
    n = ATT_T
    i = jnp.arange(n)[None, :]
    j = jnp.arange(n)[:, None]
    far = rel_bias[_t5_bucket(jnp.int32(-2 * n))].astype(F32)
    tiles = []
    for off in range(2):
        rel = (n - 1 - off * n) - jnp.arange(2 * n)
        bucket = _t5_bucket(rel)
        g = jnp.zeros((DA_HEADS, 2 * n), F32)
        for c in range(REL_BUCKETS):
            g = jnp.where((bucket == c)[None], rel_bias[c].astype(F32)[:, None], g)
        g = (g - far[:, None]) * LOG2E
        skew = jnp.broadcast_to(g[:, None, :], (DA_HEADS, n, 2 * n)).reshape(DA_HEADS, 2 * n * n)
        skew = skew[:, :n * (2 * n - 1)].reshape(DA_HEADS, n, 2 * n - 1)
        b = skew[:, :, n - 1:]
        if off == 0:
            allowed = (j // CHUNK) <= (i // CHUNK)
            b = jnp.where(allowed[None], b, NEG_BIG)
        tiles.append(b)
    return jnp.stack(tiles)


def _cumsum_rows(x):
    n = x.shape[0]
    row = lax.broadcasted_iota(jnp.int32, x.shape, 0)
    sh = 1
    while sh < n:
        x = x + jnp.where(row >= sh, pltpu.roll(x, sh, axis=0), 0.0)
        sh *= 2
    return x


def _expand_heads(v, width, group):
    assert 2 * group == LANES
    rows = v.shape[0]
    first = lax.broadcasted_iota(jnp.int32, (rows, LANES), 1) < group
    pieces = []
    for pair in range(width // LANES):
        a = jnp.broadcast_to(v[:, 2 * pair:2 * pair + 1], (rows, LANES))
        b = jnp.broadcast_to(v[:, 2 * pair + 1:2 * pair + 2], (rows, LANES))
        pieces.append(jnp.where(first, a, b))
    return jnp.concatenate(pieces, axis=1)


def _ssd_kernel(z_ref, xs_ref, bc_ref, dt_ref, cwx_ref, cwb_ref, cbx_ref, cbb_ref, dtb_ref, alog_ref,
                dskip_ref, ng_ref, o_ref, extx_ref, extb_ref, st_ref):
    j = pl.program_id(1)

    @pl.when(j == 0)
    def _reset():
        extx_ref[0:8, :] = jnp.zeros((8, SSM_WIDTH), F32)
        extb_ref[0:8, :] = jnp.zeros((8, 2 * LANES), F32)
        st_ref[...] = jnp.zeros(st_ref.shape, F32)

    extx_ref[8:8 + BLK, :] = xs_ref[...]
    extb_ref[8:8 + BLK, :] = bc_ref[...]
    yx = jnp.broadcast_to(cbx_ref[...], (BLK, SSM_WIDTH))
    yb = jnp.broadcast_to(cbb_ref[...], (BLK, 2 * LANES))
    for w in range(SSM_CONV):
        lo = 8 - (SSM_CONV - 1) + w
        yx = yx + extx_ref[lo:lo + BLK, :] * cwx_ref[w:w + 1, :]
        yb = yb + extb_ref[lo:lo + BLK, :] * cwb_ref[w:w + 1, :]
    tail_x = extx_ref[BLK:BLK + 8, :]
    tail_b = extb_ref[BLK:BLK + 8, :]
    extx_ref[0:8, :] = tail_x
    extb_ref[0:8, :] = tail_b
    xs = _silu(yx)
    bc = _silu(yb)
    bm = bc[:, :LANES]
    cm = bc[:, LANES:]

    dt = jax.nn.softplus(dt_ref[...] + dtb_ref[...])
    a = -jnp.exp(alog_ref[...]) * dt
    a_cs = _cumsum_rows(a)
    a_cs_t = a_cs.T
    a_last = a_cs[BLK - 1:BLK, :]

    dt_full = _expand_heads(dt, SSM_WIDTH, SSM_HEAD_DIM)
    ea_full = _expand_heads(jnp.exp(a_cs), SSM_WIDTH, SSM_HEAD_DIM)
    dec_full = _expand_heads(jnp.exp(a_last - a_cs), SSM_WIDTH, SSM_HEAD_DIM)
    cdec_full = _expand_heads(jnp.exp(a_last), SSM_WIDTH, SSM_HEAD_DIM)

    xdt = xs * dt_full
    xdt_b = xdt.astype(BF16)
    cm_b = cm.astype(BF16)
    bm_b = bm.astype(BF16)

    st = st_ref[...]
    y = jnp.dot(cm_b, st.astype(BF16), preferred_element_type=F32) * ea_full + xs * dskip_ref[...]

    row = lax.broadcasted_iota(jnp.int32, (BLK, BLK), 0)
    colio = lax.broadcasted_iota(jnp.int32, (BLK, BLK), 1)
    causal = colio <= row
    rep = SSM_HEADS // SSM_GROUPS
    lane_lo = lax.broadcasted_iota(jnp.int32, (BLK, LANES), 1) < SSM_HEAD_DIM
    pieces = []
    for g in range(SSM_GROUPS):
        cg = jnp.where(_group_mask(cm_b.shape, 1, SSM_STATE, g), cm_b, jnp.zeros_like(cm_b))
        cb = lax.dot_general(cg, bm_b, (((1,), (1,)), ((), ())), preferred_element_type=F32)
        for pair in range(rep // 2):
            halves = []
            for sub in range(2):
                h = g * rep + pair * 2 + sub
                seg = jnp.broadcast_to(a_cs[:, h:h + 1], (BLK, BLK)) - a_cs_t[h:h + 1, :]
                lmat = jnp.exp(jnp.where(causal, seg, NEG_BIG))
                mh = (cb * lmat).astype(BF16)
                c0 = (h // 2) * LANES
                halves.append(jnp.dot(mh, xdt_b[:, c0:c0 + LANES], preferred_element_type=F32))
            pieces.append(jnp.where(lane_lo, halves[0], halves[1]))
    y = y + jnp.concatenate(pieces, axis=1)

    upd = jnp.dot(bm.T.astype(BF16), (xdt * dec_full).astype(BF16), preferred_element_type=F32)
    r_io = lax.broadcasted_iota(jnp.int32, upd.shape, 0) // SSM_STATE
    c_io = lax.broadcasted_iota(jnp.int32, upd.shape, 1) // (SSM_HEAD_DIM * rep)
    st_ref[...] = st * cdec_full + jnp.where(r_io == c_io, upd, 0.0)

    y = y * _silu(z_ref[...])
    gw = SSM_WIDTH // SSM_GROUPS
    for g in range(SSM_GROUPS):
        yg = y[:, g * gw:(g + 1) * gw]
        ms = jnp.mean(yg * yg, axis=-1, keepdims=True)
        o_ref[:, g * gw:(g + 1) * gw] = (yg * lax.rsqrt(ms + EPS) * ng_ref[:, g * gw:(g + 1) * gw]).astype(BF16)


def _ssd(pf, cwx, cwb, cbx, cbb, dtb, alog, dskip, ng, batch, seq):
    nb = seq // BLK
    row = lambda w, c: pl.BlockSpec((BLK, w), lambda b, j: (b * nb + j, c))
    const = lambda r, w: pl.BlockSpec((r, w), lambda b, j: (0, 0))
    return pl.pallas_call(
        _ssd_kernel,
        grid=(batch, nb),
        in_specs=[
            row(SSM_WIDTH, C_Z // SSM_WIDTH), row(SSM_WIDTH, C_XS // SSM_WIDTH), row(2 * LANES, C_BC // (2 * LANES)),
            row(LANES, C_DT // LANES),
            const(SSM_CONV, SSM_WIDTH), const(SSM_CONV, 2 * LANES), const(1, SSM_WIDTH), const(1, 2 * LANES),
            const(1, LANES), const(1, LANES), const(1, SSM_WIDTH), const(1, SSM_WIDTH),
        ],
        out_specs=pl.BlockSpec((BLK, SSM_WIDTH), lambda b, j: (b * nb + j, 0)),
        out_shape=jax.ShapeDtypeStruct((batch * seq, SSM_WIDTH), BF16),
        scratch_shapes=[
            pltpu.VMEM((BLK + 8, SSM_WIDTH), F32),
            pltpu.VMEM((BLK + 8, 2 * LANES), F32),
            pltpu.VMEM((SSM_GROUPS * SSM_STATE, SSM_WIDTH), F32),
        ],
        compiler_params=_cparams(("parallel", "arbitrary")),
        name="ssd_mixer",
    )(pf, pf, pf, pf, cwx, cwb, cbx, cbb, dtb, alog, dskip, ng)


def _rotary(u, cos, sin_signed):
    first = (lax.broadcasted_iota(jnp.int32, u.shape, 1) % RET_KEY_DIM) < (RET_KEY_DIM // 2)
    half = RET_KEY_DIM // 2
    swapped = jnp.where(first, pltpu.roll(u, u.shape[1] - half, axis=1), pltpu.roll(u, half, axis=1))
    return u * cos + swapped * sin_signed


def _ret_kernel(q_ref, k_ref, v_ref, g_ref, cos_ref, sin_ref, dmat_ref, qdec_ref, kdec_ref, gdec_ref,
                bd_ref, gain_ref, o_ref, st_ref):
    j = pl.program_id(1)

    @pl.when(j == 0)
    def _reset():
        st_ref[...] = jnp.zeros(st_ref.shape, F32)

    cos = _lane_tile(cos_ref[...], RET_WIDTH // LANES)
    sin = _lane_tile(sin_ref[...], RET_WIDTH // LANES)
    q = _rotary(q_ref[...], cos, sin)
    k = _rotary(k_ref[...], cos, sin) * (RET_KEY_DIM ** -0.5)
    v_b = v_ref[...].astype(BF16)
    q_b = q.astype(BF16)
    k_b = k.astype(BF16)

    st = st_ref[...]
    o = jnp.dot((q * qdec_ref[...]).astype(BF16), st.astype(BF16), preferred_element_type=F32)
    for h in range(RET_HEADS):
        cm = _group_mask(q_b.shape, 1, RET_KEY_DIM, h)
        qh = jnp.where(cm, q_b, jnp.zeros_like(q_b))
        s = lax.dot_general(qh, k_b, (((1,), (1,)), ((), ())), preferred_element_type=F32) * dmat_ref[h]
        oh = jnp.dot(s.astype(BF16), v_b, preferred_element_type=F32)
        o = o + jnp.where(cm, oh, 0.0)

    upd = jnp.dot((k * kdec_ref[...]).T.astype(BF16), v_b, preferred_element_type=F32)
    st_ref[...] = st * gdec_ref[...] + upd * bd_ref[...]

    out = jnp.zeros(o.shape, F32)
    for h in range(RET_HEADS):
        cm = _group_mask(o.shape, 1, RET_KEY_DIM, h)
        ms = jnp.sum(jnp.where(cm, o * o, 0.0), axis=-1, keepdims=True) * (1.0 / RET_KEY_DIM)
        out = out + jnp.where(cm, o * lax.rsqrt(ms + EPS), 0.0)
    o_ref[...] = (_silu(g_ref[...]) * (out * gain_ref[...])).astype(BF16)


def _retention(pf, cos, sin, dmat, qdec, kdec, gdec, bd, gain, batch, seq):
    nb = seq // BLK
    row = lambda c: pl.BlockSpec((BLK, RET_WIDTH), lambda b, j: (b * nb + j, c))
    const2 = pl.BlockSpec((BLK, RET_WIDTH), lambda b, j: (0, 0))
    return pl.pallas_call(
        _ret_kernel,
        grid=(batch, nb),
        in_specs=[
            row(C_RQ // RET_WIDTH), row(C_RK // RET_WIDTH), row(C_RV // RET_WIDTH), row(C_RG // RET_WIDTH),
            pl.BlockSpec((BLK, LANES), lambda b, j: (j, 0)),
            pl.BlockSpec((BLK, LANES), lambda b, j: (j, 0)),
            pl.BlockSpec((RET_HEADS, BLK, BLK), lambda b, j: (0, 0, 0)),
            const2, const2, const2, const2,
            pl.BlockSpec((1, RET_WIDTH), lambda b, j: (0, 0)),
        ],
        out_specs=pl.BlockSpec((BLK, RET_WIDTH), lambda b, j: (b * nb + j, 0)),
        out_shape=jax.ShapeDtypeStruct((batch * seq, RET_WIDTH), BF16),
        scratch_shapes=[pltpu.VMEM((RET_WIDTH, RET_WIDTH), F32)],
        compiler_params=_cparams(("parallel", "arbitrary")),
        name="retention",
    )(pf, pf, pf, pf, cos, sin, dmat, qdec, kdec, gdec, bd, gain)


def _retention_tables(seq):
    f32 = np.float32
    inv = (f32(1.0) / (f32(ROPE_BASE) ** (np.arange(0, RET_KEY_DIM, 2, dtype=f32) / f32(RET_KEY_DIM)))).astype(f32)
    ang = (np.arange(seq, dtype=f32)[:, None] * inv[None, :]).astype(f32)
    cos_h = np.concatenate([np.cos(ang), np.cos(ang)], axis=-1)
    sin_h = np.concatenate([-np.sin(ang), np.sin(ang)], axis=-1)
    cos = np.tile(cos_h, (1, 2)).astype(f32)
    sin = np.tile(sin_h, (1, 2)).astype(f32)
    log_gamma = np.log1p(-np.power(f32(2.0), f32(-5.0) - np.arange(RET_HEADS, dtype=f32))).astype(f32)
    idx = np.arange(BLK, dtype=f32)
    ii = np.arange(BLK)
    same_or_earlier_chunk = (ii[None, :] // CHUNK) <= (ii[:, None] // CHUNK)
    dmat = np.exp(log_gamma[:, None, None] * np.abs(idx[:, None] - idx[None, :]))
    dmat = np.where(same_or_earlier_chunk[None], dmat, 0.0).astype(f32)
    lg_cols = np.repeat(log_gamma, RET_KEY_DIM)
    qdec = np.exp(lg_cols[None, :] * (idx + 1.0)[:, None]).astype(f32)
    kdec = np.exp(lg_cols[None, :] * (BLK - 1.0 - idx)[:, None]).astype(f32)
    head_of = np.arange(RET_WIDTH) // RET_KEY_DIM
    bd = (head_of[:, None] == head_of[None, :]).astype(f32)
    gdec = (np.exp(lg_cols * f32(BLK))[:, None] * bd).astype(f32)
    return tuple(jnp.asarray(a) for a in (cos, sin, dmat, qdec, kdec, gdec, bd))


def _outproj_router_kernel(x_ref, a_ref, b_ref, c_ref, w_ref, g_ref, rw_ref, xo_ref, h_ref, rinfo_ref):
    acc = x_ref[...]
    acc = acc + jnp.dot(a_ref[...], w_ref[0:DA_WIDTH, :], preferred_element_type=F32)
    acc = acc + jnp.dot(b_ref[...], w_ref[DA_WIDTH:DA_WIDTH + SSM_WIDTH, :], preferred_element_type=F32)
    acc = acc + jnp.dot(c_ref[...], w_ref[DA_WIDTH + SSM_WIDTH:, :], preferred_element_type=F32)
    xo_ref[...] = acc
    ms = jnp.mean(acc * acc, axis=-1, keepdims=True)
    hf = acc * lax.rsqrt(ms + EPS) * g_ref[...]
    h = hf.astype(BF16)
    h_ref[...] = hf
    logits = jnp.dot(h, rw_ref[...], preferred_element_type=F32)
    lane = lax.broadcasted_iota(jnp.int32, logits.shape, 1).astype(F32)
    logits = jnp.where(lane < N_EXPERTS, logits, NEG_BIG)
    v1 = jnp.max(logits, axis=-1, keepdims=True)
    i1 = jnp.min(jnp.where(logits == v1, lane, float(LANES)), axis=-1, keepdims=True)
    rest_l = jnp.where(lane == i1, NEG_BIG, logits)
    v2 = jnp.max(rest_l, axis=-1, keepdims=True)
    i2 = jnp.min(jnp.where(rest_l == v2, lane, float(LANES)), axis=-1, keepdims=True)
    e2 = jnp.exp(v2 - v1)
    g1 = 1.0 / (1.0 + e2)
    g2 = e2 * g1
    rinfo_ref[...] = (jnp.where(lane == 0.0, g1, 0.0) + jnp.where(lane == 1.0, g2, 0.0)
                      + jnp.where(lane == 2.0, i1, 0.0) + jnp.where(lane == 3.0, i2, 0.0))


def _outproj_router(x2, oa, ob, oc, w, gain, router_w):
    t = x2.shape[0]
    tok = lambda w_: pl.BlockSpec((TM_PROJ, w_), lambda i: (i, 0))
    return pl.pallas_call(
        _outproj_router_kernel,
        grid=(t // TM_PROJ,),
        in_specs=[tok(D_MODEL), tok(DA_WIDTH), tok(SSM_WIDTH), tok(RET_WIDTH),
                  pl.BlockSpec((D_MODEL, D_MODEL), lambda i: (0, 0)),
                  pl.BlockSpec((1, D_MODEL), lambda i: (0, 0)),
                  pl.BlockSpec((D_MODEL, LANES), lambda i: (0, 0))],
        out_specs=[tok(D_MODEL), tok(D_MODEL), tok(LANES)],
        out_shape=[jax.ShapeDtypeStruct((t, D_MODEL), F32), jax.ShapeDtypeStruct((t, D_MODEL), F32),
                   jax.ShapeDtypeStruct((t, LANES), F32)],
        compiler_params=_cparams(("parallel",)),
        name="outproj_router",
    )(x2, oa, ob, oc, w, gain, router_w)


def _ffn_kernel(x_ref, a_ref, b_ref, c_ref, wo_ref, g_ref, wg_ref, wu_ref, wd_ref, o_ref, acc_ref, h_ref):
    f = pl.program_id(1)

    @pl.when(f == 0)
    def _init():
        acc = x_ref[...]
        acc = acc + jnp.dot(a_ref[...], wo_ref[0:DA_WIDTH, :], preferred_element_type=F32)
        acc = acc + jnp.dot(b_ref[...], wo_ref[DA_WIDTH:DA_WIDTH + SSM_WIDTH, :], preferred_element_type=F32)
        acc = acc + jnp.dot(c_ref[...], wo_ref[DA_WIDTH + SSM_WIDTH:, :], preferred_element_type=F32)
        acc_ref[...] = acc
        ms = jnp.mean(acc * acc, axis=-1, keepdims=True)
        h_ref[...] = (acc * lax.rsqrt(ms + EPS) * g_ref[...]).astype(BF16)

    h = h_ref[...]
    gate = jnp.dot(h, wg_ref[0], preferred_element_type=F32)
    up = jnp.dot(h, wu_ref[0], preferred_element_type=F32)
    act = (_silu(gate) * up).astype(BF16)
    acc_ref[...] += jnp.dot(act, wd_ref[...], preferred_element_type=F32)

    @pl.when(f == pl.num_programs(1) - 1)
    def _done():
        o_ref[...] = acc_ref[...]


def _outproj_dense_ffn(x2, oa, ob, oc, w_o, gain, wg, wu, wd):
    t = x2.shape[0]
    nf = wg.shape[0]
    tok = lambda w_: pl.BlockSpec((TM_FFN, w_), lambda i, f: (i, 0))
    return pl.pallas_call(
        _ffn_kernel,
        grid=(t // TM_FFN, nf),
        in_specs=[
            tok(D_MODEL), tok(DA_WIDTH), tok(SSM_WIDTH), tok(RET_WIDTH),
            pl.BlockSpec((D_MODEL, D_MODEL), lambda i, f: (0, 0)),
            pl.BlockSpec((1, D_MODEL), lambda i, f: (0, 0)),
            pl.BlockSpec((1, D_MODEL, TF_FFN), lambda i, f: (f, 0, 0)),
            pl.BlockSpec((1, D_MODEL, TF_FFN), lambda i, f: (f, 0, 0)),
            pl.BlockSpec((TF_FFN, D_MODEL), lambda i, f: (f, 0)),
        ],
        out_specs=pl.BlockSpec((TM_FFN, D_MODEL), lambda i, f: (i, 0)),
        out_shape=jax.ShapeDtypeStruct((t, D_MODEL), F32),
        scratch_shapes=[pltpu.VMEM((TM_FFN, D_MODEL), F32), pltpu.VMEM((TM_FFN, D_MODEL), BF16)],
        compiler_params=_cparams(("parallel", "arbitrary")),
        name="outproj_dense_swiglu",
    )(x2, oa, ob, oc, w_o, gain, wg, wu, wd)


def _issue_row_gather(src_hbm, idx_ref, idx_base, dst, sem, n_rows):
    def body(r, carry):
        tok = idx_ref[idx_base + r]
        pltpu.make_async_copy(src_hbm.at[pl.ds(tok, 1)], dst.at[pl.ds(r, 1)], sem).start()
        return carry
    lax.fori_loop(0, n_rows, body, 0, unroll=8)


def _wait_row_gather(src_hbm, dst, sem, n_rows):
    pltpu.make_async_copy(src_hbm.at[pl.ds(0, n_rows)], dst, sem).wait()


def _moe_ffn_kernel(te_ref, nv_ref, tb_ref, rt_ref, h_hbm, wg_ref, wu_ref, wd_ref, y_ref, hbuf, xb_ref, acc_ref, sem):
    i = pl.program_id(0)
    f = pl.program_id(1)
    n_valid = nv_ref[0]
    valid = i < n_valid
    slot = i % 2

    @pl.when((f == 0) & (i == 0) & valid)
    def _first():
        _issue_row_gather(h_hbm, rt_ref, tb_ref[0], hbuf.at[0], sem.at[0], TR_MOE)

    @pl.when((f == 0) & valid)
    def _arrive():
        _wait_row_gather(h_hbm, hbuf.at[slot], sem.at[slot], TR_MOE)
        xb_ref[...] = hbuf[slot].astype(BF16)

    def compute(prefetch_next):
        if prefetch_next:
            rows_per_step = TR_MOE // (FFN_EXPERT // TF_MOE)
            r0 = f * rows_per_step
            src0 = tb_ref[i + 1] + r0
            for j in range(rows_per_step):
                tok = rt_ref[src0 + j]
                pltpu.make_async_copy(h_hbm.at[pl.ds(tok, 1)], hbuf.at[1 - slot, pl.ds(r0 + j, 1)],
                                      sem.at[1 - slot]).start()
        xb = xb_ref[...]
        gate = jnp.dot(xb, wg_ref[0].astype(BF16), preferred_element_type=F32)
        up = jnp.dot(xb, wu_ref[0].astype(BF16), preferred_element_type=F32)
        act = (_silu(gate) * up).astype(BF16)
        contrib = jnp.dot(act, wd_ref[0].astype(BF16), preferred_element_type=F32)

        @pl.when(f == 0)
        def _():
            acc_ref[...] = contrib

        @pl.when(f > 0)
        def _():
            acc_ref[...] += contrib

    has_next = i + 1 < n_valid

    @pl.when(valid & has_next)
    def _compute_and_prefetch():
        compute(True)

    @pl.when(valid & jnp.logical_not(has_next))
    def _compute_last():
        compute(False)

    @pl.when(f == pl.num_programs(1) - 1)
    def _store():
        y_ref[...] = jnp.where(valid, acc_ref[...], 0.0)


def _moe_ffn(h, tile_expert, n_valid, tile_base, sorted_token, wg, wu, wd):
    n_tiles = tile_expert.shape[0]
    nf = wg.shape[2] // TF_MOE

    def w_in_map(i, f, te, nv, tb, rt):
        return (te[i], 0, jnp.where(i < nv[0], f, nf - 1))

    def w_out_map(i, f, te, nv, tb, rt):
        return (te[i], jnp.where(i < nv[0], f, nf - 1), 0)

    grid_spec = pltpu.PrefetchScalarGridSpec(
        num_scalar_prefetch=4,
        grid=(n_tiles, nf),
        in_specs=[
            pl.BlockSpec(memory_space=pl.ANY),
            pl.BlockSpec((1, D_MODEL, TF_MOE), w_in_map),
            pl.BlockSpec((1, D_MODEL, TF_MOE), w_in_map),
            pl.BlockSpec((1, TF_MOE, D_MODEL), w_out_map),
        ],
        out_specs=pl.BlockSpec((TR_MOE, D_MODEL), lambda i, f, te, nv, tb, rt: (i, 0)),
        scratch_shapes=[
            pltpu.VMEM((2, TR_MOE, D_MODEL), F32),
            pltpu.VMEM((TR_MOE, D_MODEL), BF16),
            pltpu.VMEM((TR_MOE, D_MODEL), F32),
            pltpu.SemaphoreType.DMA((2,)),
        ],
    )
    return pl.pallas_call(
        _moe_ffn_kernel,
        grid_spec=grid_spec,
        out_shape=jax.ShapeDtypeStruct((n_tiles * TR_MOE, D_MODEL), F32),
        compiler_params=_cparams(("arbitrary", "arbitrary")),
        name="moe_expert_ffn",
    )(tile_expert, n_valid, tile_base, sorted_token, h, wg, wu, wd)


def _moe_combine_kernel(tr_ref, x_ref, rinfo_ref, y_hbm, fg_ref, o_ref, ybuf, sem):
    i = pl.program_id(0)
    slot = i % 2
    n_rows = TOP_K * TM_COMB

    @pl.when(i == 0)
    def _first():
        _issue_row_gather(y_hbm, tr_ref, 0, ybuf.at[0], sem.at[0], n_rows)

    _wait_row_gather(y_hbm, ybuf.at[slot], sem.at[slot], n_rows)

    @pl.when(i + 1 < pl.num_programs(0))
    def _prefetch():
        src0 = (i + 1) * n_rows
        dst = ybuf.at[1 - slot]
        for r in range(n_rows):
            pltpu.make_async_copy(y_hbm.at[pl.ds(tr_ref[src0 + r], 1)], dst.at[pl.ds(r, 1)], sem.at[1 - slot]).start()

    rinfo = rinfo_ref[...]
    g1 = jnp.broadcast_to(rinfo[:, 0:1], (TM_COMB, D_MODEL))
    g2 = jnp.broadcast_to(rinfo[:, 1:2], (TM_COMB, D_MODEL))
    y = x_ref[...] + g1 * ybuf[slot, 0:TM_COMB, :] + g2 * ybuf[slot, TM_COMB:n_rows, :]
    ms = jnp.mean(y * y, axis=-1, keepdims=True)
    o_ref[...] = y * lax.rsqrt(ms + EPS) * fg_ref[...]


def _moe_combine(x2, rinfo, y_rows, tok_rows, final_gain):
    t = x2.shape[0]
    grid_spec = pltpu.PrefetchScalarGridSpec(
        num_scalar_prefetch=1,
        grid=(t // TM_COMB,),
        in_specs=[
            pl.BlockSpec((TM_COMB, D_MODEL), lambda i, tr: (i, 0)),
            pl.BlockSpec((TM_COMB, LANES), lambda i, tr: (i, 0)),
            pl.BlockSpec(memory_space=pl.ANY),
            pl.BlockSpec((1, D_MODEL), lambda i, tr: (0, 0)),
        ],
        out_specs=pl.BlockSpec((TM_COMB, D_MODEL), lambda i, tr: (i, 0)),
        scratch_shapes=[
            pltpu.VMEM((2, TOP_K * TM_COMB, D_MODEL), F32),
            pltpu.SemaphoreType.DMA((2,)),
        ],
    )
    return pl.pallas_call(
        _moe_combine_kernel,
        grid_spec=grid_spec,
        out_shape=jax.ShapeDtypeStruct((t, D_MODEL), F32),
        compiler_params=_cparams(("arbitrary",)),
        name="moe_combine_norm",
    )(tok_rows, x2, rinfo, y_rows, final_gain)


def _routing_tables(rinfo):
    t = rinfo.shape[0]
    n_assign = TOP_K * t
    n_tiles = -(-n_assign // TR_MOE) + N_EXPERTS
    flat_e = rinfo[:, 2:2 + TOP_K].astype(jnp.int32).reshape(-1)
    onehot = (flat_e[:, None] == jnp.arange(N_EXPERTS)[None, :]).astype(jnp.int32)
    blocks = onehot.astype(BF16).reshape(n_assign // LANES, LANES, N_EXPERTS)
    tri = (jnp.arange(LANES)[:, None] >= jnp.arange(LANES)[None, :]).astype(BF16)
    within = jnp.einsum("ij,bjk->bik", tri, blocks, preferred_element_type=F32)
    block_end = jnp.cumsum(within[:, -1, :], axis=0)
    csum = (within + (block_end - within[:, -1, :])[:, None, :]).reshape(n_assign, N_EXPERTS).astype(jnp.int32)
    rank = jnp.sum((csum - onehot) * onehot, axis=1)
    counts = csum[-1]
    tiles_per_e = (counts + TR_MOE - 1) // TR_MOE
    tile_end = jnp.cumsum(tiles_per_e)
    first_tile = tile_end - tiles_per_e
    dest = (first_tile * TR_MOE)[flat_e] + rank
    order = jnp.sort(flat_e * n_assign + jnp.arange(n_assign, dtype=jnp.int32))
    sorted_token = jnp.concatenate([(order % n_assign) // TOP_K, jnp.zeros((TR_MOE,), jnp.int32)])
    start = jnp.cumsum(counts) - counts
    n_valid = tile_end[-1:].astype(jnp.int32)
    tile_ids = jnp.arange(n_tiles)
    tile_expert = jnp.sum(tile_ids[:, None] >= tile_end[None, :], axis=1).astype(jnp.int32)
    last_e = jnp.sum(n_valid[0] - 1 >= tile_end).astype(jnp.int32)
    tile_expert = jnp.where(tile_ids < n_valid[0], tile_expert, last_e)
    tile_base = start[tile_expert] + (tile_ids - first_tile[tile_expert]) * TR_MOE
    tile_base = jnp.clip(tile_base, 0, n_assign).astype(jnp.int32)
    tok_rows = dest.reshape(t // TM_COMB, TM_COMB, TOP_K).transpose(0, 2, 1).reshape(-1).astype(jnp.int32)
    return tile_expert, n_valid, tile_base, sorted_token, tok_rows


def _permute_w_in(w):
    sizes = (256, 256, 256, 512, 768, 8, 256, 256, 256, 256)
    offs = np.concatenate([[0], np.cumsum(sizes)])
    part = lambda n: w[:, offs[n]:offs[n + 1]]
    dt_pad = jnp.zeros((w.shape[0], LANES - SSM_HEADS), w.dtype)
    cols = [part(0) * (DA_HEAD_DIM ** -0.5 * LOG2E), part(2), part(1), part(3), part(4),
            part(6), part(7), part(8), part(9), part(5), dt_pad]
    return jnp.concatenate(cols, axis=1).astype(BF16)


def _column_tiles(w, tile):
    *lead, k, n = w.shape
    return jnp.moveaxis(w.reshape(*lead, k, n // tile, tile), -2, -3)


def _pad_lanes(v, fill=0.0):
    return jnp.concatenate([v, jnp.full((LANES - v.shape[0],), fill, v.dtype)])[None, :]


def kernel(x, w_in, w_out, attn_norm, ffn_norm, final_norm, rel_bias, lambda_q1, lambda_k1, lambda_q2,
           lambda_k2, da_head_norm, conv_w, conv_b, dt_bias, a_log, d_skip, ssm_norm, ret_head_norm,
           w_gate, w_up, w_down, router_w, e_gate, e_up, e_down):
    batch, seq, _ = x.shape
    assert DEPTH % 2 == 0, "the final RMSNorm is fused into the last (expert) layer's combine kernel"
    assert seq % max(ATT_T, TM_PROJ, BLK) == 0 and (batch * seq) % TM_FFN == 0
    x2 = x.reshape(batch * seq, D_MODEL)
    bias_tiles = _bias_tiles(rel_bias)
    ret_tabs = _retention_tables(seq)

    for layer in range(DEPTH):
        lam_init = 0.8 - 0.6 * math.exp(-0.3 * layer)
        lam = (jnp.exp(jnp.sum(lambda_q1[layer] * lambda_k1[layer]))
               - jnp.exp(jnp.sum(lambda_q2[layer] * lambda_k2[layer])) + lam_init).astype(F32)
        lam = jnp.full((1, ATT_T), lam, F32)

        qkv, qvt, pf = _inproj(x2, attn_norm[layer][None, :], _permute_w_in(w_in[layer]))

        da_gain = (jnp.tile(da_head_norm[layer], DA_HEADS) * (1.0 - lam_init))[:, None]
        out_a = _attention(qkv, qvt, bias_tiles, da_gain, lam, batch, seq)

        cw, cb = conv_w[layer], conv_b[layer]
        out_b = _ssd(pf, cw[:, :SSM_WIDTH], cw[:, SSM_WIDTH:], cb[None, :SSM_WIDTH], cb[None, SSM_WIDTH:],
                     _pad_lanes(dt_bias[layer]), _pad_lanes(a_log[layer]),
                     jnp.repeat(d_skip[layer], SSM_HEAD_DIM)[None, :], ssm_norm[layer][None, :], batch, seq)

        ret_gain = jnp.tile(ret_head_norm[layer], RET_HEADS)[None, :]
        out_c = _retention(pf, *ret_tabs, ret_gain, batch, seq)

        w_o = w_out[layer].astype(BF16)
        if layer % 2 == 0:
            i = layer // 2
            pad = FFN_PAD - FFN_DENSE
            wg = _column_tiles(jnp.pad(w_gate[i], ((0, 0), (0, pad))).astype(BF16), TF_FFN)
            wu = _column_tiles(jnp.pad(w_up[i], ((0, 0), (0, pad))).astype(BF16), TF_FFN)
            wd = jnp.pad(w_down[i], ((0, pad), (0, 0))).astype(BF16)
            x2 = _outproj_dense_ffn(x2, out_a, out_b, out_c, w_o, ffn_norm[layer][None, :], wg, wu, wd)
        else:
            i = layer // 2
            rw = jnp.pad(router_w[i], ((0, 0), (0, LANES - N_EXPERTS))).astype(BF16)
            x2, h, rinfo = _outproj_router(x2, out_a, out_b, out_c, w_o, ffn_norm[layer][None, :], rw)
            tile_expert, n_valid, tile_base, sorted_token, tok_rows = _routing_tables(rinfo)
            y_rows = _moe_ffn(h, tile_expert, n_valid, tile_base, sorted_token,
                              e_gate[i], e_up[i], e_down[i])
            x2 = _moe_combine(x2, rinfo, y_rows, tok_rows, final_norm[None, :])
    return x2.reshape(batch, seq, D_MODEL)
```

```python
import math

import jax
import jax.numpy as jnp
import numpy as np
from jax import lax
from jax.experimental import pallas as pl
from jax.experimental.pallas import tpu as pltpu

F32 = jnp.float32
BF16 = jnp.bfloat16

D_MODEL = 1024
DEPTH = 2
CHUNK = 64
EPS = 1e-6
DA_HEADS = 4
DA_HEAD_DIM = 32
DA_V_DIM = 64
DA_WIDTH = 256
SSM_HEADS = 8
SSM_HEAD_DIM = 64
SSM_WIDTH = 512
SSM_STATE = 64
SSM_GROUPS = 2
SSM_CONV = 4
RET_HEADS = 4
RET_KEY_DIM = 64
RET_WIDTH = 256
ROPE_BASE = 10000.0
REL_BUCKETS = 32
REL_MAX_DIST = 128
FFN_DENSE = 2752
N_EXPERTS = 8
FFN_EXPERT = 3584

LANES = 128
VMEM_LIMIT = 48 * 1024 * 1024
NEG_BIG = -1e30

TM_PROJ = 512
ATT_T = 512
LOG2E = math.log2(math.e)
BLK = 256
FFN_PAD = 2816
TM_FFN = 512
TF_FFN = 1408
TF_MOE = 512
TR_MOE = 896
TM_COMB = 256
TOP_K = 2

C_Z, C_XS, C_BC, C_RQ, C_RK, C_RV, C_RG, C_DT = 0, 512, 1024, 1280, 1536, 1792, 2048, 2304
PF_W = 2432
PF_CHUNK = 640
QKV_W = 768


def _cparams(sem):
    return pltpu.CompilerParams(dimension_semantics=sem, vmem_limit_bytes=VMEM_LIMIT)


def _group_mask(shape, axis, group, idx):
    io = lax.broadcasted_iota(jnp.int32, shape, axis)
    return (io >= idx * group) & (io < (idx + 1) * group)


def _lane_tile(x, n, axis=1):
    return jnp.concatenate([x] * n, axis=axis)


def _silu(x):
    h = 0.5 * x
    return h + h * jnp.tanh(h)


def _inproj_kernel(x_ref, g_ref, w_ref, qkv_ref, qvt_ref, pf_ref):
    x = x_ref[...]
    ms = jnp.mean(x * x, axis=-1, keepdims=True)
    h = (x * lax.rsqrt(ms + EPS) * g_ref[...]).astype(BF16)
    qkv = jnp.dot(h, w_ref[:, :QKV_W], preferred_element_type=F32)
    qkv_ref[...] = qkv.astype(BF16)
    qvt_ref[...] = qkv[:, :2 * DA_WIDTH].T.astype(BF16)
    for lo in range(0, PF_W, PF_CHUNK):
        hi = min(lo + PF_CHUNK, PF_W)
        pf_ref[:, lo:hi] = jnp.dot(h, w_ref[:, QKV_W + lo:QKV_W + hi], preferred_element_type=F32)


def _inproj(x2, gain, w):
    t = x2.shape[0]
    return pl.pallas_call(
        _inproj_kernel,
        grid=(t // TM_PROJ,),
        in_specs=[
            pl.BlockSpec((TM_PROJ, D_MODEL), lambda i: (i, 0)),
            pl.BlockSpec((1, D_MODEL), lambda i: (0, 0)),
            pl.BlockSpec((D_MODEL, QKV_W + PF_W), lambda i: (0, 0)),
        ],
        out_specs=[
            pl.BlockSpec((TM_PROJ, QKV_W), lambda i: (i, 0)),
            pl.BlockSpec((2 * DA_WIDTH, TM_PROJ), lambda i: (0, i)),
            pl.BlockSpec((TM_PROJ, PF_W), lambda i: (i, 0)),
        ],
        out_shape=[
            jax.ShapeDtypeStruct((t, QKV_W), BF16),
            jax.ShapeDtypeStruct((2 * DA_WIDTH, t), BF16),
            jax.ShapeDtypeStruct((t, PF_W), F32),
        ],
        compiler_params=_cparams(("parallel",)),
        name="inproj",
    )(x2, gain, w)


def _attn_kernel(qi_ref, kj_ref, qt_ref, k_ref, vt_ref, bias_ref, gain_ref, lam_ref, o_ref,
                 qm_ref, m_ref, l_ref, acc_ref, s_ref, e_ref, al_ref):
    p = pl.program_id(1)
    qi = qi_ref[p]
    kj = kj_ref[p]
    n_ht = 2 * DA_HEADS

    @pl.when(kj == 0)
    def _init():
        qt = qt_ref[...]
        for ht in range(n_ht):
            qm_ref[ht] = jnp.where(_group_mask(qt.shape, 0, DA_HEAD_DIM, ht), qt, jnp.zeros_like(qt))
        m_ref[...] = jnp.full(m_ref.shape, NEG_BIG, F32)
        l_ref[...] = jnp.zeros(l_ref.shape, F32)
        acc_ref[...] = jnp.zeros(acc_ref.shape, F32)

    def step(with_bias):
        k = k_ref[...]
        def scores(a, buf):
            s_ref[buf] = jnp.dot(k, qm_ref[a], preferred_element_type=F32)

        def softmax(b, buf):
            for c0 in range(0, ATT_T, LANES):
                cols = slice(c0, c0 + LANES)

                def block():
                    blk = s_ref[buf, :, cols]
                    return blk + bias_ref[qi - kj, b // 2, :, cols] if with_bias else blk

                m_prev = m_ref[b, :, cols]
                m_new = jnp.maximum(m_prev, jnp.max(block(), axis=0, keepdims=True))
                alpha = jnp.exp2(m_prev - m_new)
                e = jnp.exp2(block() - m_new)
                l_ref[b, :, cols] = alpha * l_ref[b, :, cols] + jnp.sum(e, axis=0, keepdims=True)
                m_ref[b, :, cols] = m_new
                al_ref[buf, :, cols] = alpha
                e_ref[buf, :, cols] = e.astype(BF16)

        def values(c, buf):
            r0 = (c // 4) * LANES
            if not isinstance(r0, int):
                r0 = pl.multiple_of(r0, LANES)
            pv = jnp.dot(vt_ref[pl.ds(r0, LANES), :], e_ref[buf], preferred_element_type=F32)
            acc_ref[c] = acc_ref[c] * al_ref[buf] + pv

        scores(0, 0)
        scores(1, 1)
        softmax(0, 0)

        def trip(u, carry):
            t = 2 * u + 1
            scores(t + 1, 0)
            softmax(t, 1)
            values(t - 1, 0)
            scores(t + 2, 1)
            softmax(t + 1, 0)
            values(t, 1)
            return carry

        for u in range((n_ht - 2) // 2):
            trip(u, 0)
        softmax(n_ht - 1, 1)
        values(n_ht - 2, 0)
        values(n_ht - 1, 1)

    @pl.when(kj >= qi - 1)
    def _near():
        step(True)

    @pl.when(kj < qi - 1)
    def _far():
        step(False)

    @pl.when(kj == qi)
    def _finish():
        lam = lam_ref[...]
        row_lo = lax.broadcasted_iota(jnp.int32, (LANES, ATT_T), 0) < DA_V_DIM
        for pair in range(DA_HEADS // 2):
            halves = []
            for sub in range(2):
                h = 2 * pair + sub
                o = acc_ref[2 * h] * (1.0 / l_ref[2 * h]) - lam * (acc_ref[2 * h + 1] * (1.0 / l_ref[2 * h + 1]))
                own = row_lo if sub == 0 else jnp.logical_not(row_lo)
                ms = jnp.sum(jnp.where(own, o * o, 0.0), axis=0, keepdims=True) * (1.0 / DA_V_DIM)
                halves.append(o * lax.rsqrt(ms + EPS))
            r0 = pair * LANES
            blk = jnp.where(row_lo, halves[0], halves[1]) * gain_ref[r0:r0 + LANES, :]
            o_ref[:, r0:r0 + LANES] = blk.T.astype(BF16)


def _attention(qkv, qvt, bias_tiles, gain, lam, batch, seq):
    nq = seq // ATT_T
    qi_l, kj_l = [], []
    for qi in range(nq):
        for kj in range(qi + 1):
            qi_l.append(qi)
            kj_l.append(kj)
    qi_tbl = jnp.asarray(np.array(qi_l, np.int32))
    kj_tbl = jnp.asarray(np.array(kj_l, np.int32))
    n_pairs = len(qi_l)
    n_ht = 2 * DA_HEADS

    grid_spec = pltpu.PrefetchScalarGridSpec(
        num_scalar_prefetch=2,
        grid=(batch, n_pairs),
        in_specs=[
            pl.BlockSpec((DA_WIDTH, ATT_T), lambda b, p, qi, kj: (0, b * nq + qi[p])),
            pl.BlockSpec((ATT_T, DA_WIDTH), lambda b, p, qi, kj: (b * nq + kj[p], 2)),
            pl.BlockSpec((DA_WIDTH, ATT_T), lambda b, p, qi, kj: (1, b * nq + kj[p])),
            pl.BlockSpec((2, DA_HEADS, ATT_T, ATT_T), lambda b, p, qi, kj: (0, 0, 0, 0)),
            pl.BlockSpec((DA_WIDTH, 1), lambda b, p, qi, kj: (0, 0)),
            pl.BlockSpec((1, ATT_T), lambda b, p, qi, kj: (0, 0)),
        ],
        out_specs=pl.BlockSpec((ATT_T, DA_WIDTH), lambda b, p, qi, kj: (b * nq + qi[p], 0)),
        scratch_shapes=[
            pltpu.VMEM((n_ht, DA_WIDTH, ATT_T), BF16),
            pltpu.VMEM((n_ht, 1, ATT_T), F32),
            pltpu.VMEM((n_ht, 1, ATT_T), F32),
            pltpu.VMEM((n_ht, LANES, ATT_T), F32),
            pltpu.VMEM((2, ATT_T, ATT_T), F32),
            pltpu.VMEM((2, ATT_T, ATT_T), BF16),
            pltpu.VMEM((2, 1, ATT_T), F32),
        ],
    )
    return pl.pallas_call(
        _attn_kernel,
        grid_spec=grid_spec,
        out_shape=jax.ShapeDtypeStruct((batch * seq, DA_WIDTH), BF16),
        compiler_params=_cparams(("parallel", "arbitrary")),
        name="diff_attention",
    )(qi_tbl, kj_tbl, qvt, qkv, qvt, bias_tiles, gain, lam)


def _t5_bucket(rel):
    half = REL_BUCKETS // 2
    max_exact = half // 2
    ret = jnp.where(rel > 0, half, 0)
    n = jnp.abs(rel)
    nf = jnp.maximum(n, 1).astype(F32)
    large = max_exact + (jnp.log(nf / max_exact) / math.log(REL_MAX_DIST / max_exact)
                         * (half - max_exact)).astype(jnp.int32)
    large = jnp.minimum(large, half - 1)
    return ret + jnp.where(n < max_exact, n, large)


def _bias_tiles(rel_bias):
    n = ATT_T
    i = jnp.arange(n)[None, :]
    j = jnp.arange(n)[:, None]
    far = rel_bias[_t5_bucket(jnp.int32(-2 * n))].astype(F32)
    tiles = []
    for off in range(2):
        rel = (n - 1 - off * n) - jnp.arange(2 * n)
        bucket = _t5_bucket(rel)
        g = jnp.zeros((DA_HEADS, 2 * n), F32)
        for c in range(REL_BUCKETS):
            g = jnp.where((bucket == c)[None], rel_bias[c].astype(F32)[:, None], g)
        g = (g - far[:, None]) * LOG2E
        skew = jnp.broadcast_to(g[:, None, :], (DA_HEADS, n, 2 * n)).reshape(DA_HEADS, 2 * n * n)
        skew = skew[:, :n * (2 * n - 1)].reshape(DA_HEADS, n, 2 * n - 1)
        b = skew[:, :, n - 1:]
        if off == 0:
            allowed = (j // CHUNK) <= (i // CHUNK)
            b = jnp.where(allowed[None], b, NEG_BIG)
        tiles.append(b)
    return jnp.stack(tiles)


def _cumsum_rows(x):
    n = x.shape[0]
    row = lax.broadcasted_iota(jnp.int32, x.shape, 0)
    sh = 1
    while sh < n:
        x = x + jnp.where(row >= sh, pltpu.roll(x, sh, axis=0), 0.0)
        sh *= 2
    return x


def _expand_heads(v, width, group):
    out = jnp.zeros((v.shape[0], width), F32)
    for h in range(width // group):
        col = jnp.broadcast_to(v[:, h:h + 1], (v.shape[0], width))
        out = jnp.where(_group_mask(out.shape, 1, group, h), col, out)
    return out


def _ssd_kernel(z_ref, xs_ref, bc_ref, dt_ref, cwx_ref, cwb_ref, cbx_ref, cbb_ref, dtb_ref, alog_ref,
                dskip_ref, ng_ref, o_ref, extx_ref, extb_ref, st_ref):
    j = pl.program_id(1)

    @pl.when(j == 0)
    def _reset():
        extx_ref[0:8, :] = jnp.zeros((8, SSM_WIDTH), F32)
        extb_ref[0:8, :] = jnp.zeros((8, 2 * LANES), F32)
        st_ref[...] = jnp.zeros(st_ref.shape, F32)

    extx_ref[8:8 + BLK, :] = xs_ref[...]
    extb_ref[8:8 + BLK, :] = bc_ref[...]
    yx = jnp.broadcast_to(cbx_ref[...], (BLK, SSM_WIDTH))
    yb = jnp.broadcast_to(cbb_ref[...], (BLK, 2 * LANES))
    for w in range(SSM_CONV):
        lo = 8 - (SSM_CONV - 1) + w
        yx = yx + extx_ref[lo:lo + BLK, :] * cwx_ref[w:w + 1, :]
        yb = yb + extb_ref[lo:lo + BLK, :] * cwb_ref[w:w + 1, :]
    tail_x = extx_ref[BLK:BLK + 8, :]
    tail_b = extb_ref[BLK:BLK + 8, :]
    extx_ref[0:8, :] = tail_x
    extb_ref[0:8, :] = tail_b
    xs = _silu(yx)
    bc = _silu(yb)
    bm = bc[:, :LANES]
    cm = bc[:, LANES:]

    dt = jax.nn.softplus(dt_ref[...] + dtb_ref[...])
    a = -jnp.exp(alog_ref[...]) * dt
    a_cs = _cumsum_rows(a)
    a_cs_t = a_cs.T
    a_last = a_cs[BLK - 1:BLK, :]

    dt_full = _expand_heads(dt, SSM_WIDTH, SSM_HEAD_DIM)
    ea_full = _expand_heads(jnp.exp(a_cs), SSM_WIDTH, SSM_HEAD_DIM)
    dec_full = _expand_heads(jnp.exp(a_last - a_cs), SSM_WIDTH, SSM_HEAD_DIM)
    cdec_full = _expand_heads(jnp.exp(a_last), SSM_WIDTH, SSM_HEAD_DIM)

    xdt = xs * dt_full
    xdt_b = xdt.astype(BF16)
    cm_b = cm.astype(BF16)
    bm_b = bm.astype(BF16)

    st = st_ref[...]
    y = jnp.dot(cm_b, st.astype(BF16), preferred_element_type=F32) * ea_full + xs * dskip_ref[...]

    row = lax.broadcasted_iota(jnp.int32, (BLK, BLK), 0)
    colio = lax.broadcasted_iota(jnp.int32, (BLK, BLK), 1)
    causal = colio <= row
    rep = SSM_HEADS // SSM_GROUPS
    lane_lo = lax.broadcasted_iota(jnp.int32, (BLK, LANES), 1) < SSM_HEAD_DIM
    pieces = []
    for g in range(SSM_GROUPS):
        cg = jnp.where(_group_mask(cm_b.shape, 1, SSM_STATE, g), cm_b, jnp.zeros_like(cm_b))
        cb = lax.dot_general(cg, bm_b, (((1,), (1,)), ((), ())), preferred_element_type=F32)
        for pair in range(rep // 2):
            halves = []
            for sub in range(2):
                h = g * rep + pair * 2 + sub
                seg = jnp.broadcast_to(a_cs[:, h:h + 1], (BLK, BLK)) - a_cs_t[h:h + 1, :]
                lmat = jnp.exp(jnp.where(causal, seg, NEG_BIG))
                mh = (cb * lmat).astype(BF16)
                c0 = (h // 2) * LANES
                halves.append(jnp.dot(mh, xdt_b[:, c0:c0 + LANES], preferred_element_type=F32))
            pieces.append(jnp.where(lane_lo, halves[0], halves[1]))
    y = y + jnp.concatenate(pieces, axis=1)

    upd = jnp.dot(bm.T.astype(BF16), (xdt * dec_full).astype(BF16), preferred_element_type=F32)
    r_io = lax.broadcasted_iota(jnp.int32, upd.shape, 0) // SSM_STATE
    c_io = lax.broadcasted_iota(jnp.int32, upd.shape, 1) // (SSM_HEAD_DIM * rep)
    st_ref[...] = st * cdec_full + jnp.where(r_io == c_io, upd, 0.0)

    y = y * _silu(z_ref[...])
    gw = SSM_WIDTH // SSM_GROUPS
    for g in range(SSM_GROUPS):
        yg = y[:, g * gw:(g + 1) * gw]
        ms = jnp.mean(yg * yg, axis=-1, keepdims=True)
        o_ref[:, g * gw:(g + 1) * gw] = (yg * lax.rsqrt(ms + EPS) * ng_ref[:, g * gw:(g + 1) * gw]).astype(BF16)


def _ssd(pf, cwx, cwb, cbx, cbb, dtb, alog, dskip, ng, batch, seq):
    nb = seq // BLK
    row = lambda w, c: pl.BlockSpec((BLK, w), lambda b, j: (b * nb + j, c))
    const = lambda r, w: pl.BlockSpec((r, w), lambda b, j: (0, 0))
    return pl.pallas_call(
        _ssd_kernel,
        grid=(batch, nb),
        in_specs=[
            row(SSM_WIDTH, C_Z // SSM_WIDTH), row(SSM_WIDTH, C_XS // SSM_WIDTH), row(2 * LANES, C_BC // (2 * LANES)),
            row(LANES, C_DT // LANES),
            const(SSM_CONV, SSM_WIDTH), const(SSM_CONV, 2 * LANES), const(1, SSM_WIDTH), const(1, 2 * LANES),
            const(1, LANES), const(1, LANES), const(1, SSM_WIDTH), const(1, SSM_WIDTH),
        ],
        out_specs=pl.BlockSpec((BLK, SSM_WIDTH), lambda b, j: (b * nb + j, 0)),
        out_shape=jax.ShapeDtypeStruct((batch * seq, SSM_WIDTH), BF16),
        scratch_shapes=[
            pltpu.VMEM((BLK + 8, SSM_WIDTH), F32),
            pltpu.VMEM((BLK + 8, 2 * LANES), F32),
            pltpu.VMEM((SSM_GROUPS * SSM_STATE, SSM_WIDTH), F32),
        ],
        compiler_params=_cparams(("parallel", "arbitrary")),
        name="ssd_mixer",
    )(pf, pf, pf, pf, cwx, cwb, cbx, cbb, dtb, alog, dskip, ng)


def _rotary(u, cos, sin_signed):
    first = (lax.broadcasted_iota(jnp.int32, u.shape, 1) % RET_KEY_DIM) < (RET_KEY_DIM // 2)
    half = RET_KEY_DIM // 2
    swapped = jnp.where(first, pltpu.roll(u, u.shape[1] - half, axis=1), pltpu.roll(u, half, axis=1))
    return u * cos + swapped * sin_signed


def _ret_kernel(q_ref, k_ref, v_ref, g_ref, cos_ref, sin_ref, dmat_ref, qdec_ref, kdec_ref, gdec_ref,
                bd_ref, gain_ref, o_ref, st_ref):
    j = pl.program_id(1)

    @pl.when(j == 0)
    def _reset():
        st_ref[...] = jnp.zeros(st_ref.shape, F32)

    cos = _lane_tile(cos_ref[...], RET_WIDTH // LANES)
    sin = _lane_tile(sin_ref[...], RET_WIDTH // LANES)
    q = _rotary(q_ref[...], cos, sin)
    k = _rotary(k_ref[...], cos, sin) * (RET_KEY_DIM ** -0.5)
    v_b = v_ref[...].astype(BF16)
    q_b = q.astype(BF16)
    k_b = k.astype(BF16)

    st = st_ref[...]
    o = jnp.dot((q * qdec_ref[...]).astype(BF16), st.astype(BF16), preferred_element_type=F32)
    for h in range(RET_HEADS):
        cm = _group_mask(q_b.shape, 1, RET_KEY_DIM, h)
        qh = jnp.where(cm, q_b, jnp.zeros_like(q_b))
        s = lax.dot_general(qh, k_b, (((1,), (1,)), ((), ())), preferred_element_type=F32) * dmat_ref[h]
        oh = jnp.dot(s.astype(BF16), v_b, preferred_element_type=F32)
        o = o + jnp.where(cm, oh, 0.0)

    upd = jnp.dot((k * kdec_ref[...]).T.astype(BF16), v_b, preferred_element_type=F32)
    st_ref[...] = st * gdec_ref[...] + upd * bd_ref[...]

    out = jnp.zeros(o.shape, F32)
    for h in range(RET_HEADS):
        cm = _group_mask(o.shape, 1, RET_KEY_DIM, h)
        ms = jnp.sum(jnp.where(cm, o * o, 0.0), axis=-1, keepdims=True) * (1.0 / RET_KEY_DIM)
        out = out + jnp.where(cm, o * lax.rsqrt(ms + EPS), 0.0)
    o_ref[...] = (_silu(g_ref[...]) * (out * gain_ref[...])).astype(BF16)


def _retention(pf, cos, sin, dmat, qdec, kdec, gdec, bd, gain, batch, seq):
    nb = seq // BLK
    row = lambda c: pl.BlockSpec((BLK, RET_WIDTH), lambda b, j: (b * nb + j, c))
    const2 = pl.BlockSpec((BLK, RET_WIDTH), lambda b, j: (0, 0))
    return pl.pallas_call(
        _ret_kernel,
        grid=(batch, nb),
        in_specs=[
            row(C_RQ // RET_WIDTH), row(C_RK // RET_WIDTH), row(C_RV // RET_WIDTH), row(C_RG // RET_WIDTH),
            pl.BlockSpec((BLK, LANES), lambda b, j: (j, 0)),
            pl.BlockSpec((BLK, LANES), lambda b, j: (j, 0)),
            pl.BlockSpec((RET_HEADS, BLK, BLK), lambda b, j: (0, 0, 0)),
            const2, const2, const2, const2,
            pl.BlockSpec((1, RET_WIDTH), lambda b, j: (0, 0)),
        ],
        out_specs=pl.BlockSpec((BLK, RET_WIDTH), lambda b, j: (b * nb + j, 0)),
        out_shape=jax.ShapeDtypeStruct((batch * seq, RET_WIDTH), BF16),
        scratch_shapes=[pltpu.VMEM((RET_WIDTH, RET_WIDTH), F32)],
        compiler_params=_cparams(("parallel", "arbitrary")),
        name="retention",
    )(pf, pf, pf, pf, cos, sin, dmat, qdec, kdec, gdec, bd, gain)


def _retention_tables(seq):
    f32 = np.float32
    inv = (f32(1.0) / (f32(ROPE_BASE) ** (np.arange(0, RET_KEY_DIM, 2, dtype=f32) / f32(RET_KEY_DIM)))).astype(f32)
    ang = (np.arange(seq, dtype=f32)[:, None] * inv[None, :]).astype(f32)
    cos_h = np.concatenate([np.cos(ang), np.cos(ang)], axis=-1)
    sin_h = np.concatenate([-np.sin(ang), np.sin(ang)], axis=-1)
    cos = np.tile(cos_h, (1, 2)).astype(f32)
    sin = np.tile(sin_h, (1, 2)).astype(f32)
    log_gamma = np.log1p(-np.power(f32(2.0), f32(-5.0) - np.arange(RET_HEADS, dtype=f32))).astype(f32)
    idx = np.arange(BLK, dtype=f32)
    ii = np.arange(BLK)
    same_or_earlier_chunk = (ii[None, :] // CHUNK) <= (ii[:, None] // CHUNK)
    dmat = np.exp(log_gamma[:, None, None] * np.abs(idx[:, None] - idx[None, :]))
    dmat = np.where(same_or_earlier_chunk[None], dmat, 0.0).astype(f32)
    lg_cols = np.repeat(log_gamma, RET_KEY_DIM)
    qdec = np.exp(lg_cols[None, :] * (idx + 1.0)[:, None]).astype(f32)
    kdec = np.exp(lg_cols[None, :] * (BLK - 1.0 - idx)[:, None]).astype(f32)
    head_of = np.arange(RET_WIDTH) // RET_KEY_DIM
    bd = (head_of[:, None] == head_of[None, :]).astype(f32)
    gdec = (np.exp(lg_cols * f32(BLK))[:, None] * bd).astype(f32)
    return tuple(jnp.asarray(a) for a in (cos, sin, dmat, qdec, kdec, gdec, bd))


def _outproj_router_kernel(x_ref, a_ref, b_ref, c_ref, w_ref, g_ref, rw_ref, xo_ref, h_ref, rinfo_ref):
    acc = x_ref[...]
    acc = acc + jnp.dot(a_ref[...], w_ref[0:DA_WIDTH, :], preferred_element_type=F32)
    acc = acc + jnp.dot(b_ref[...], w_ref[DA_WIDTH:DA_WIDTH + SSM_WIDTH, :], preferred_element_type=F32)
    acc = acc + jnp.dot(c_ref[...], w_ref[DA_WIDTH + SSM_WIDTH:, :], preferred_element_type=F32)
    xo_ref[...] = acc
    ms = jnp.mean(acc * acc, axis=-1, keepdims=True)
    hf = acc * lax.rsqrt(ms + EPS) * g_ref[...]
    h = hf.astype(BF16)
    h_ref[...] = hf
    logits = jnp.dot(h, rw_ref[...], preferred_element_type=F32)
    lane = lax.broadcasted_iota(jnp.int32, logits.shape, 1).astype(F32)
    logits = jnp.where(lane < N_EXPERTS, logits, NEG_BIG)
    v1 = jnp.max(logits, axis=-1, keepdims=True)
    i1 = jnp.min(jnp.where(logits == v1, lane, float(LANES)), axis=-1, keepdims=True)
    rest_l = jnp.where(lane == i1, NEG_BIG, logits)
    v2 = jnp.max(rest_l, axis=-1, keepdims=True)
    i2 = jnp.min(jnp.where(rest_l == v2, lane, float(LANES)), axis=-1, keepdims=True)
    e2 = jnp.exp(v2 - v1)
    g1 = 1.0 / (1.0 + e2)
    g2 = e2 * g1
    rinfo_ref[...] = (jnp.where(lane == 0.0, g1, 0.0) + jnp.where(lane == 1.0, g2, 0.0)
                      + jnp.where(lane == 2.0, i1, 0.0) + jnp.where(lane == 3.0, i2, 0.0))


def _outproj_router(x2, oa, ob, oc, w, gain, router_w):
    t = x2.shape[0]
    tok = lambda w_: pl.BlockSpec((TM_PROJ, w_), lambda i: (i, 0))
    return pl.pallas_call(
        _outproj_router_kernel,
        grid=(t // TM_PROJ,),
        in_specs=[tok(D_MODEL), tok(DA_WIDTH), tok(SSM_WIDTH), tok(RET_WIDTH),
                  pl.BlockSpec((D_MODEL, D_MODEL), lambda i: (0, 0)),
                  pl.BlockSpec((1, D_MODEL), lambda i: (0, 0)),
                  pl.BlockSpec((D_MODEL, LANES), lambda i: (0, 0))],
        out_specs=[tok(D_MODEL), tok(D_MODEL), tok(LANES)],
        out_shape=[jax.ShapeDtypeStruct((t, D_MODEL), F32), jax.ShapeDtypeStruct((t, D_MODEL), F32),
                   jax.ShapeDtypeStruct((t, LANES), F32)],
        compiler_params=_cparams(("parallel",)),
        name="outproj_router",
    )(x2, oa, ob, oc, w, gain, router_w)


def _ffn_kernel(x_ref, a_ref, b_ref, c_ref, wo_ref, g_ref, wg_ref, wu_ref, wd_ref, o_ref, acc_ref, h_ref):
    f = pl.program_id(1)

    @pl.when(f == 0)
    def _init():
        acc = x_ref[...]
        acc = acc + jnp.dot(a_ref[...], wo_ref[0:DA_WIDTH, :], preferred_element_type=F32)
        acc = acc + jnp.dot(b_ref[...], wo_ref[DA_WIDTH:DA_WIDTH + SSM_WIDTH, :], preferred_element_type=F32)
        acc = acc + jnp.dot(c_ref[...], wo_ref[DA_WIDTH + SSM_WIDTH:, :], preferred_element_type=F32)
        acc_ref[...] = acc
        ms = jnp.mean(acc * acc, axis=-1, keepdims=True)
        h_ref[...] = (acc * lax.rsqrt(ms + EPS) * g_ref[...]).astype(BF16)

    h = h_ref[...]
    gate = jnp.dot(h, wg_ref[0], preferred_element_type=F32)
    up = jnp.dot(h, wu_ref[0], preferred_element_type=F32)
    act = (_silu(gate) * up).astype(BF16)
    acc_ref[...] += jnp.dot(act, wd_ref[...], preferred_element_type=F32)

    @pl.when(f == pl.num_programs(1) - 1)
    def _done():
        o_ref[...] = acc_ref[...]


def _outproj_dense_ffn(x2, oa, ob, oc, w_o, gain, wg, wu, wd):
    t = x2.shape[0]
    nf = wg.shape[0]
    tok = lambda w_: pl.BlockSpec((TM_FFN, w_), lambda i, f: (i, 0))
    return pl.pallas_call(
        _ffn_kernel,
        grid=(t // TM_FFN, nf),
        in_specs=[
            tok(D_MODEL), tok(DA_WIDTH), tok(SSM_WIDTH), tok(RET_WIDTH),
            pl.BlockSpec((D_MODEL, D_MODEL), lambda i, f: (0, 0)),
            pl.BlockSpec((1, D_MODEL), lambda i, f: (0, 0)),
            pl.BlockSpec((1, D_MODEL, TF_FFN), lambda i, f: (f, 0, 0)),
            pl.BlockSpec((1, D_MODEL, TF_FFN), lambda i, f: (f, 0, 0)),
            pl.BlockSpec((TF_FFN, D_MODEL), lambda i, f: (f, 0)),
        ],
        out_specs=pl.BlockSpec((TM_FFN, D_MODEL), lambda i, f: (i, 0)),
        out_shape=jax.ShapeDtypeStruct((t, D_MODEL), F32),
        scratch_shapes=[pltpu.VMEM((TM_FFN, D_MODEL), F32), pltpu.VMEM((TM_FFN, D_MODEL), BF16)],
        compiler_params=_cparams(("parallel", "arbitrary")),
        name="outproj_dense_swiglu",
    )(x2, oa, ob, oc, w_o, gain, wg, wu, wd)


def _issue_row_gather(src_hbm, idx_ref, idx_base, dst, sem, n_rows):
    def body(r, carry):
        tok = idx_ref[idx_base + r]
        pltpu.make_async_copy(src_hbm.at[pl.ds(tok, 1)], dst.at[pl.ds(r, 1)], sem).start()
        return carry
    lax.fori_loop(0, n_rows, body, 0, unroll=8)


def _wait_row_gather(src_hbm, dst, sem, n_rows):
    pltpu.make_async_copy(src_hbm.at[pl.ds(0, n_rows)], dst, sem).wait()


def _moe_ffn_kernel(te_ref, nv_ref, tb_ref, rt_ref, h_hbm, wg_ref, wu_ref, wd_ref, y_ref, hbuf, xb_ref, acc_ref, sem):
    i = pl.program_id(0)
    f = pl.program_id(1)
    n_valid = nv_ref[0]
    valid = i < n_valid
    slot = i % 2

    @pl.when((f == 0) & (i == 0) & valid)
    def _first():
        _issue_row_gather(h_hbm, rt_ref, tb_ref[0], hbuf.at[0], sem.at[0], TR_MOE)

    @pl.when((f == 0) & valid)
    def _arrive():
        _wait_row_gather(h_hbm, hbuf.at[slot], sem.at[slot], TR_MOE)
        xb_ref[...] = hbuf[slot].astype(BF16)

    def compute(prefetch_next):
        if prefetch_next:
            rows_per_step = TR_MOE // (FFN_EXPERT // TF_MOE)
            r0 = f * rows_per_step
            src0 = tb_ref[i + 1] + r0
            for j in range(rows_per_step):
                tok = rt_ref[src0 + j]
                pltpu.make_async_copy(h_hbm.at[pl.ds(tok, 1)], hbuf.at[1 - slot, pl.ds(r0 + j, 1)],
                                      sem.at[1 - slot]).start()
        xb = xb_ref[...]
        gate = jnp.dot(xb, wg_ref[0].astype(BF16), preferred_element_type=F32)
        up = jnp.dot(xb, wu_ref[0].astype(BF16), preferred_element_type=F32)
        act = (_silu(gate) * up).astype(BF16)
        contrib = jnp.dot(act, wd_ref[0].astype(BF16), preferred_element_type=F32)

        @pl.when(f == 0)
        def _():
            acc_ref[...] = contrib

        @pl.when(f > 0)
        def _():
            acc_ref[...] += contrib

    has_next = i + 1 < n_valid

    @pl.when(valid & has_next)
    def _compute_and_prefetch():
        compute(True)

    @pl.when(valid & jnp.logical_not(has_next))
    def _compute_last():
        compute(False)

    @pl.when(f == pl.num_programs(1) - 1)
    def _store():
        y_ref[...] = jnp.where(valid, acc_ref[...], 0.0)


def _moe_ffn(h, tile_expert, n_valid, tile_base, sorted_token, wg, wu, wd):
    n_tiles = tile_expert.shape[0]
    nf = wg.shape[2] // TF_MOE

    def w_in_map(i, f, te, nv, tb, rt):
        return (te[i], 0, jnp.where(i < nv[0], f, nf - 1))

    def w_out_map(i, f, te, nv, tb, rt):
        return (te[i], jnp.where(i < nv[0], f, nf - 1), 0)

    grid_spec = pltpu.PrefetchScalarGridSpec(
        num_scalar_prefetch=4,
        grid=(n_tiles, nf),
        in_specs=[
            pl.BlockSpec(memory_space=pl.ANY),
            pl.BlockSpec((1, D_MODEL, TF_MOE), w_in_map),
            pl.BlockSpec((1, D_MODEL, TF_MOE), w_in_map),
            pl.BlockSpec((1, TF_MOE, D_MODEL), w_out_map),
        ],
        out_specs=pl.BlockSpec((TR_MOE, D_MODEL), lambda i, f, te, nv, tb, rt: (i, 0)),
        scratch_shapes=[
            pltpu.VMEM((2, TR_MOE, D_MODEL), F32),
            pltpu.VMEM((TR_MOE, D_MODEL), BF16),
            pltpu.VMEM((TR_MOE, D_MODEL), F32),
            pltpu.SemaphoreType.DMA((2,)),
        ],
    )
    return pl.pallas_call(
        _moe_ffn_kernel,
        grid_spec=grid_spec,
        out_shape=jax.ShapeDtypeStruct((n_tiles * TR_MOE, D_MODEL), F32),
        compiler_params=_cparams(("arbitrary", "arbitrary")),
        name="moe_expert_ffn",
    )(tile_expert, n_valid, tile_base, sorted_token, h, wg, wu, wd)


def _moe_combine_kernel(tr_ref, x_ref, rinfo_ref, y_hbm, fg_ref, o_ref, ybuf, sem):
    i = pl.program_id(0)
    slot = i % 2
    n_rows = TOP_K * TM_COMB

    @pl.when(i == 0)
    def _first():
        _issue_row_gather(y_hbm, tr_ref, 0, ybuf.at[0], sem.at[0], n_rows)

    _wait_row_gather(y_hbm, ybuf.at[slot], sem.at[slot], n_rows)

    @pl.when(i + 1 < pl.num_programs(0))
    def _prefetch():
        src0 = (i + 1) * n_rows
        dst = ybuf.at[1 - slot]
        for r in range(n_rows):
            pltpu.make_async_copy(y_hbm.at[pl.ds(tr_ref[src0 + r], 1)], dst.at[pl.ds(r, 1)], sem.at[1 - slot]).start()

    rinfo = rinfo_ref[...]
    g1 = jnp.broadcast_to(rinfo[:, 0:1], (TM_COMB, D_MODEL))
    g2 = jnp.broadcast_to(rinfo[:, 1:2], (TM_COMB, D_MODEL))
    y = x_ref[...] + g1 * ybuf[slot, 0:TM_COMB, :] + g2 * ybuf[slot, TM_COMB:n_rows, :]
    ms = jnp.mean(y * y, axis=-1, keepdims=True)
    o_ref[...] = y * lax.rsqrt(ms + EPS) * fg_ref[...]


def _moe_combine(x2, rinfo, y_rows, tok_rows, final_gain):
    t = x2.shape[0]
    grid_spec = pltpu.PrefetchScalarGridSpec(
        num_scalar_prefetch=1,
        grid=(t // TM_COMB,),
        in_specs=[
            pl.BlockSpec((TM_COMB, D_MODEL), lambda i, tr: (i, 0)),
            pl.BlockSpec((TM_COMB, LANES), lambda i, tr: (i, 0)),
            pl.BlockSpec(memory_space=pl.ANY),
            pl.BlockSpec((1, D_MODEL), lambda i, tr: (0, 0)),
        ],
        out_specs=pl.BlockSpec((TM_COMB, D_MODEL), lambda i, tr: (i, 0)),
        scratch_shapes=[
            pltpu.VMEM((2, TOP_K * TM_COMB, D_MODEL), F32),
            pltpu.SemaphoreType.DMA((2,)),
        ],
    )
    return pl.pallas_call(
        _moe_combine_kernel,
        grid_spec=grid_spec,
        out_shape=jax.ShapeDtypeStruct((t, D_MODEL), F32),
        compiler_params=_cparams(("arbitrary",)),
        name="moe_combine_norm",
    )(tok_rows, x2, rinfo, y_rows, final_gain)


def _routing_tables(rinfo):
    t = rinfo.shape[0]
    n_assign = TOP_K * t
    n_tiles = -(-n_assign // TR_MOE) + N_EXPERTS
    flat_e = rinfo[:, 2:2 + TOP_K].astype(jnp.int32).reshape(-1)
    onehot = (flat_e[:, None] == jnp.arange(N_EXPERTS)[None, :]).astype(jnp.int32)
    blocks = onehot.astype(BF16).reshape(n_assign // LANES, LANES, N_EXPERTS)
    tri = (jnp.arange(LANES)[:, None] >= jnp.arange(LANES)[None, :]).astype(BF16)
    within = jnp.einsum("ij,bjk->bik", tri, blocks, preferred_element_type=F32)
    block_end = jnp.cumsum(within[:, -1, :], axis=0)
    csum = (within + (block_end - within[:, -1, :])[:, None, :]).reshape(n_assign, N_EXPERTS).astype(jnp.int32)
    rank = jnp.sum((csum - onehot) * onehot, axis=1)
    counts = csum[-1]
    tiles_per_e = (counts + TR_MOE - 1) // TR_MOE
    tile_end = jnp.cumsum(tiles_per_e)
    first_tile = tile_end - tiles_per_e
    dest = (first_tile * TR_MOE)[flat_e] + rank
    order = jnp.sort(flat_e * n_assign + jnp.arange(n_assign, dtype=jnp.int32))
    sorted_token = jnp.concatenate([(order % n_assign) // TOP_K, jnp.zeros((TR_MOE,), jnp.int32)])
    start = jnp.cumsum(counts) - counts
    n_valid = tile_end[-1:].astype(jnp.int32)
    tile_ids = jnp.arange(n_tiles)
    tile_expert = jnp.sum(tile_ids[:, None] >= tile_end[None, :], axis=1).astype(jnp.int32)
    last_e = jnp.sum(n_valid[0] - 1 >= tile_end).astype(jnp.int32)
    tile_expert = jnp.where(tile_ids < n_valid[0], tile_expert, last_e)
    tile_base = start[tile_expert] + (tile_ids - first_tile[tile_expert]) * TR_MOE
    tile_base = jnp.clip(tile_base, 0, n_assign).astype(jnp.int32)
    tok_rows = dest.reshape(t // TM_COMB, TM_COMB, TOP_K).transpose(0, 2, 1).reshape(-1).astype(jnp.int32)
    return tile_expert, n_valid, tile_base, sorted_token, tok_rows


def _permute_w_in(w):
    sizes = (256, 256, 256, 512, 768, 8, 256, 256, 256, 256)
    offs = np.concatenate([[0], np.cumsum(sizes)])
    part = lambda n: w[:, offs[n]:offs[n + 1]]
    dt_pad = jnp.zeros((w.shape[0], LANES - SSM_HEADS), w.dtype)
    cols = [part(0) * (DA_HEAD_DIM ** -0.5 * LOG2E), part(2), part(1), part(3), part(4),
            part(6), part(7), part(8), part(9), part(5), dt_pad]
    return jnp.concatenate(cols, axis=1).astype(BF16)


def _column_tiles(w, tile):
    *lead, k, n = w.shape
    return jnp.moveaxis(w.reshape(*lead, k, n // tile, tile), -2, -3)


def _pad_lanes(v, fill=0.0):
    return jnp.concatenate([v, jnp.full((LANES - v.shape[0],), fill, v.dtype)])[None, :]


def kernel(x, w_in, w_out, attn_norm, ffn_norm, final_norm, rel_bias, lambda_q1, lambda_k1, lambda_q2,
           lambda_k2, da_head_norm, conv_w, conv_b, dt_bias, a_log, d_skip, ssm_norm, ret_head_norm,
           w_gate, w_up, w_down, router_w, e_gate, e_up, e_down):
    batch, seq, _ = x.shape
    assert DEPTH % 2 == 0, "the final RMSNorm is fused into the last (expert) layer's combine kernel"
    assert seq % max(ATT_T, TM_PROJ, BLK) == 0 and (batch * seq) % TM_FFN == 0
    x2 = x.reshape(batch * seq, D_MODEL)
    bias_tiles = _bias_tiles(rel_bias)
    ret_tabs = _retention_tables(seq)

    for layer in range(DEPTH):
        lam_init = 0.8 - 0.6 * math.exp(-0.3 * layer)
        lam = (jnp.exp(jnp.sum(lambda_q1[layer] * lambda_k1[layer]))
               - jnp.exp(jnp.sum(lambda_q2[layer] * lambda_k2[layer])) + lam_init).astype(F32)
        lam = jnp.full((1, ATT_T), lam, F32)

        qkv, qvt, pf = _inproj(x2, attn_norm[layer][None, :], _permute_w_in(w_in[layer]))

        da_gain = (jnp.tile(da_head_norm[layer], DA_HEADS) * (1.0 - lam_init))[:, None]
        out_a = _attention(qkv, qvt, bias_tiles, da_gain, lam, batch, seq)

        cw, cb = conv_w[layer], conv_b[layer]
        out_b = _ssd(pf, cw[:, :SSM_WIDTH], cw[:, SSM_WIDTH:], cb[None, :SSM_WIDTH], cb[None, SSM_WIDTH:],
                     _pad_lanes(dt_bias[layer]), _pad_lanes(a_log[layer]),
                     jnp.repeat(d_skip[layer], SSM_HEAD_DIM)[None, :], ssm_norm[layer][None, :], batch, seq)

        ret_gain = jnp.tile(ret_head_norm[layer], RET_HEADS)[None, :]
        out_c = _retention(pf, *ret_tabs, ret_gain, batch, seq)

        w_o = w_out[layer].astype(BF16)
        if layer % 2 == 0:
            i = layer // 2
            pad = FFN_PAD - FFN_DENSE
            wg = _column_tiles(jnp.pad(w_gate[i], ((0, 0), (0, pad))).astype(BF16), TF_FFN)
            wu = _column_tiles(jnp.pad(w_up[i], ((0, 0), (0, pad))).astype(BF16), TF_FFN)
            wd = jnp.pad(w_down[i], ((0, pad), (0, 0))).astype(BF16)
            x2 = _outproj_dense_ffn(x2, out_a, out_b, out_c, w_o, ffn_norm[layer][None, :], wg, wu, wd)
        else:
            i = layer // 2
            rw = jnp.pad(router_w[i], ((0, 0), (0, LANES - N_EXPERTS))).astype(BF16)
            x2, h, rinfo = _outproj_router(x2, out_a, out_b, out_c, w_o, ffn_norm[layer][None, :], rw)
            tile_expert, n_valid, tile_base, sorted_token, tok_rows = _routing_tables(rinfo)
            y_rows = _moe_ffn(h, tile_expert, n_valid, tile_base, sorted_token,
                              e_gate[i], e_up[i], e_down[i])
            x2 = _moe_combine(x2, rinfo, y_rows, tok_rows, final_norm[None, :])
    return x2.reshape(batch, seq, D_MODEL)
```
